```python
import math
import jax, jax.numpy as jnp
from jax import lax
import numpy as np

D_MODEL = 1024
BATCH = 16
SEQ = 4096
DEPTH = 2

HEAD_DIM = 64
LRU_WIDTH = D_MODEL // 2
LRU_BLOCKS = LRU_WIDTH // HEAD_DIM
SB_HEADS = D_MODEL // 256
FOX_HEADS = D_MODEL // 256
SB_WIDTH = SB_HEADS * HEAD_DIM
FOX_WIDTH = FOX_HEADS * HEAD_DIM
D_MIX = LRU_WIDTH + SB_WIDTH + FOX_WIDTH
CONV_WIDTH = 4
LRU_C = 8.0
D_FF = 2816
Q_BLOCK = 128
N_SUB = 3
EPS = 1e-6

SPLIT_SIZES = (LRU_WIDTH, LRU_WIDTH,
               SB_WIDTH, SB_WIDTH, SB_WIDTH,
               FOX_WIDTH, FOX_WIDTH, FOX_WIDTH,
               FOX_HEADS)
N_IN = sum(SPLIT_SIZES)
SPLIT_POINTS = tuple(int(v) for v in np.cumsum(SPLIT_SIZES)[:-1])

kernel_name = "hybrid_macaron_rglru_stickbreak_fox"


def _rms(x):
    xf = x.astype(jnp.float32)
    y = xf * lax.rsqrt(jnp.mean(xf * xf, axis=-1, keepdims=True) + EPS)
    return y.astype(x.dtype)


def _swiglu(h, w_up, w_down):
    gu = h @ w_up
    g, u = jnp.split(gu, 2, axis=-1)
    return (jax.nn.silu(g) * u) @ w_down


def _causal_depthwise_conv(x, w, b):
    y = lax.conv_general_dilated(
        x, w[:, None, :], window_strides=(1,), padding=[(CONV_WIDTH - 1, 0)],
        dimension_numbers=("NWC", "WIO", "NWC"), feature_group_count=x.shape[-1])
    return y + b


def _lru_combine(left, right):
    a1, b1 = left
    a2, b2 = right
    return a1 * a2, a2 * b1 + b2


def _rg_lru(u, w_r, b_r, w_i, b_i, lam):
    B, S, _ = u.shape
    ub = u.reshape(B, S, LRU_BLOCKS, HEAD_DIM)
    r = jax.nn.sigmoid(jnp.einsum("bshi,hij->bshj", ub, w_r).reshape(B, S, LRU_WIDTH) + b_r)
    i = jax.nn.sigmoid(jnp.einsum("bshi,hij->bshj", ub, w_i).reshape(B, S, LRU_WIDTH) + b_i)
    r = r.astype(jnp.float32)
    log_a = -LRU_C * r * jax.nn.softplus(-lam.astype(jnp.float32))
    a = jnp.exp(log_a)
    mult = jnp.sqrt(-jnp.expm1(2.0 * log_a))
    bt = mult * (i.astype(jnp.float32) * u.astype(jnp.float32))
    _, h = lax.associative_scan(_lru_combine, (a, bt), axis=1)
    return h.astype(u.dtype)


def _to_heads(t, n_heads):
    B, S, _ = t.shape
    return t.reshape(B, S, n_heads, HEAD_DIM).transpose(0, 2, 1, 3)


def _from_blocks(out):
    nb, B, H, Q, Dh = out.shape
    return out.transpose(1, 0, 3, 2, 4).reshape(B, nb * Q, H * Dh)


def _stick_breaking_attention(q, k, v):
    S = q.shape[2]
    scale = HEAD_DIM ** -0.5
    key_pos = jnp.arange(S)

    def block(bi):
        start = bi * Q_BLOCK
        qb = lax.dynamic_slice_in_dim(q, start, Q_BLOCK, axis=2)
        z = jnp.einsum("bhqd,bhkd->bhqk", qb, k).astype(jnp.float32) * scale
        q_pos = start + jnp.arange(Q_BLOCK)
        past = key_pos[None, :] < q_pos[:, None]
        log_beta = jax.nn.log_sigmoid(z)
        log_1mb = jnp.where(past, jax.nn.log_sigmoid(-z), 0.0)
        after = lax.cumsum(log_1mb, axis=3, reverse=True) - log_1mb
        w = jnp.where(past, jnp.exp(log_beta + after), 0.0)
        return jnp.einsum("bhqk,bhkd->bhqd", w.astype(v.dtype), v)

    return _from_blocks(lax.map(block, jnp.arange(S // Q_BLOCK)))


def _forgetting_attention(q, k, v, log_f):
    S = q.shape[2]
    scale = HEAD_DIM ** -0.5
    key_pos = jnp.arange(S)
    F = jnp.cumsum(log_f, axis=-1)

    def block(bi):
        start = bi * Q_BLOCK
        qb = lax.dynamic_slice_in_dim(q, start, Q_BLOCK, axis=2)
        Fq = lax.dynamic_slice_in_dim(F, start, Q_BLOCK, axis=2)
        logits = jnp.einsum("bhqd,bhkd->bhqk", qb, k).astype(jnp.float32) * scale
        logits = logits + Fq[..., :, None] - F[..., None, :]
        q_pos = start + jnp.arange(Q_BLOCK)
        causal = key_pos[None, :] <= q_pos[:, None]
        logits = jnp.where(causal, logits, -jnp.inf)
        p = jax.nn.softmax(logits, axis=-1)
        return jnp.einsum("bhqk,bhkd->bhqd", p.astype(v.dtype), v)

    return _from_blocks(lax.map(block, jnp.arange(S // Q_BLOCK)))


def _mixer(h, w_in, b_fgate, conv_w, conv_b, w_rgate, b_rgate, w_igate, b_igate,
           lru_lambda, g_qk, g_mix_out, w_out):
    proj = h @ w_in
    (lru_x, lru_g, sb_q, sb_k, sb_v, fx_q, fx_k, fx_v, fx_f) = jnp.split(proj, SPLIT_POINTS, axis=-1)

    u = _causal_depthwise_conv(lru_x, conv_w, conv_b)
    y_lru = _rg_lru(u, w_rgate, b_rgate, w_igate, b_igate, lru_lambda) * jax.nn.gelu(lru_g)

    y_sb = _stick_breaking_attention(_to_heads(sb_q, SB_HEADS), _to_heads(sb_k, SB_HEADS),
                                     _to_heads(sb_v, SB_HEADS))

    fq = _rms(_to_heads(fx_q, FOX_HEADS)) * g_qk[0]
    fk = _rms(_to_heads(fx_k, FOX_HEADS)) * g_qk[1]
    log_f = jax.nn.log_sigmoid(fx_f.astype(jnp.float32) + b_fgate.astype(jnp.float32))
    y_fox = _forgetting_attention(fq, fk, _to_heads(fx_v, FOX_HEADS), log_f.transpose(0, 2, 1))

    y = jnp.concatenate([_rms(y_lru), _rms(y_sb), _rms(y_fox)], axis=-1) * g_mix_out
    return y @ w_out


def _fwd_setup_inputs(seed: int = 0) -> dict:
    key = jax.random.key(seed)
    ks = jax.random.split(key, 20)
    f32 = jnp.float32
    nrm = lambda k, shape, s: jax.random.normal(k, shape, f32) * s
    x = jax.random.normal(ks[0], (BATCH, SEQ, D_MODEL), f32)
    c = jax.random.normal(ks[1], (BATCH, D_MODEL), f32)
    w_ada = nrm(ks[2], (DEPTH, D_MODEL, N_SUB * 3 * D_MODEL), 0.1 * D_MODEL ** -0.5)
    b_ada = nrm(ks[3], (DEPTH, N_SUB * 3 * D_MODEL), 0.01)
    g_norm = 1.0 + nrm(ks[4], (DEPTH, N_SUB, D_MODEL), 0.05)
    w_ffn_up = nrm(ks[5], (DEPTH, 2, D_MODEL, 2 * D_FF), D_MODEL ** -0.5)
    w_ffn_down = nrm(ks[6], (DEPTH, 2, D_FF, D_MODEL), D_FF ** -0.5)
    w_in = nrm(ks[7], (DEPTH, D_MODEL, N_IN), D_MODEL ** -0.5)
    b_fgate = 3.0 + nrm(ks[8], (DEPTH, FOX_HEADS), 0.1)
    conv_w = nrm(ks[9], (DEPTH, CONV_WIDTH, LRU_WIDTH), CONV_WIDTH ** -0.5)
    conv_b = nrm(ks[10], (DEPTH, LRU_WIDTH), 0.01)
    w_rgate = nrm(ks[11], (DEPTH, LRU_BLOCKS, HEAD_DIM, HEAD_DIM), HEAD_DIM ** -0.5)
    b_rgate = nrm(ks[12], (DEPTH, LRU_WIDTH), 0.01)
    w_igate = nrm(ks[13], (DEPTH, LRU_BLOCKS, HEAD_DIM, HEAD_DIM), HEAD_DIM ** -0.5)
    b_igate = nrm(ks[14], (DEPTH, LRU_WIDTH), 0.01)
    a_c = jax.random.uniform(ks[15], (DEPTH, LRU_WIDTH), f32, 0.9, 0.999)
    s = a_c ** (1.0 / LRU_C)
    lru_lambda = jnp.log(s) - jnp.log1p(-s)
    g_qk = 1.0 + nrm(ks[16], (DEPTH, 2, HEAD_DIM), 0.05)
    g_mix_out = 1.0 + nrm(ks[17], (DEPTH, D_MIX), 0.05)
    w_out = nrm(ks[18], (DEPTH, D_MIX, D_MODEL), D_MIX ** -0.5)
    return {"x": x, "c": c, "w_ada": w_ada, "b_ada": b_ada, "g_norm": g_norm,
            "w_ffn_up": w_ffn_up, "w_ffn_down": w_ffn_down, "w_in": w_in,
            "b_fgate": b_fgate, "conv_w": conv_w, "conv_b": conv_b,
            "w_rgate": w_rgate, "b_rgate": b_rgate, "w_igate": w_igate,
            "b_igate": b_igate, "lru_lambda": lru_lambda, "g_qk": g_qk,
            "g_mix_out": g_mix_out, "w_out": w_out}


def _fwd_reference(x, c, w_ada, b_ada, g_norm, w_ffn_up, w_ffn_down, w_in, b_fgate, conv_w,
              conv_b, w_rgate, b_rgate, w_igate, b_igate, lru_lambda, g_qk, g_mix_out, w_out):
    B = x.shape[0]
    c_act = jax.nn.silu(c)
    for l in range(DEPTH):
        mod = (c_act @ w_ada[l] + b_ada[l]).reshape(B, N_SUB, 3, D_MODEL)

        def norm_mod(h, j):
            shift = mod[:, j, 0][:, None, :]
            scale = mod[:, j, 1][:, None, :]
            return _rms(h) * g_norm[l, j] * (1.0 + scale) + shift

        def gate(j):
            return (1.0 + mod[:, j, 2])[:, None, :]

        x = x + 0.5 * gate(0) * _swiglu(norm_mod(x, 0), w_ffn_up[l, 0], w_ffn_down[l, 0])
        x = x + gate(1) * _mixer(norm_mod(x, 1), w_in[l], b_fgate[l], conv_w[l], conv_b[l],
                                 w_rgate[l], b_rgate[l], w_igate[l], b_igate[l],
                                 lru_lambda[l], g_qk[l], g_mix_out[l], w_out[l])
        x = x + 0.5 * gate(2) * _swiglu(norm_mod(x, 2), w_ffn_up[l, 1], w_ffn_down[l, 1])
    return x


import jax as _jax
import jax.numpy as _jnp

TWIN_FORMAT = 'train_step'
FWD_PARAMS = ['x', 'c', 'w_ada', 'b_ada', 'g_norm', 'w_ffn_up', 'w_ffn_down', 'w_in', 'b_fgate', 'conv_w', 'conv_b', 'w_rgate', 'b_rgate', 'w_igate', 'b_igate', 'lru_lambda', 'g_qk', 'g_mix_out', 'w_out']
TWIN_WEIGHTS = ['w_ada', 'b_ada', 'g_norm', 'w_ffn_up', 'w_ffn_down', 'w_in', 'b_fgate', 'conv_w', 'conv_b', 'w_rgate', 'b_rgate', 'w_igate', 'b_igate', 'lru_lambda', 'g_qk', 'g_mix_out', 'w_out']
TWIN_DIFF_INPUT = 'x'
TWIN_INPUTS = ['x', 'c', 'w_ada', 'b_ada', 'g_norm', 'w_ffn_up', 'w_ffn_down', 'w_in', 'b_fgate', 'conv_w', 'conv_b', 'w_rgate', 'b_rgate', 'w_igate', 'b_igate', 'lru_lambda', 'g_qk', 'g_mix_out', 'w_out', 'loss_target', 'm_w_ada', 'm_b_ada', 'm_g_norm', 'm_w_ffn_up', 'm_w_ffn_down', 'm_w_in', 'm_b_fgate', 'm_conv_w', 'm_conv_b', 'm_w_rgate', 'm_b_rgate', 'm_w_igate', 'm_b_igate', 'm_lru_lambda', 'm_g_qk', 'm_g_mix_out', 'm_w_out', 'v_w_ada', 'v_b_ada', 'v_g_norm', 'v_w_ffn_up', 'v_w_ffn_down', 'v_w_in', 'v_b_fgate', 'v_conv_w', 'v_conv_b', 'v_w_rgate', 'v_b_rgate', 'v_w_igate', 'v_b_igate', 'v_lru_lambda', 'v_g_qk', 'v_g_mix_out', 'v_w_out']
TWIN_OUTPUTS = ['loss', 'grad_x', 'grad_w_ada', 'grad_b_ada', 'grad_g_norm', 'grad_w_ffn_up', 'grad_w_ffn_down', 'grad_w_in', 'grad_b_fgate', 'grad_conv_w', 'grad_conv_b', 'grad_w_rgate', 'grad_b_rgate', 'grad_w_igate', 'grad_b_igate', 'grad_lru_lambda', 'grad_g_qk', 'grad_g_mix_out', 'grad_w_out', 'delta_w_ada', 'delta_b_ada', 'delta_g_norm', 'delta_w_ffn_up', 'delta_w_ffn_down', 'delta_w_in', 'delta_b_fgate', 'delta_conv_w', 'delta_conv_b', 'delta_w_rgate', 'delta_b_rgate', 'delta_w_igate', 'delta_b_igate', 'delta_lru_lambda', 'delta_g_qk', 'delta_g_mix_out', 'delta_w_out', 'new_m_w_ada', 'new_m_b_ada', 'new_m_g_norm', 'new_m_w_ffn_up', 'new_m_w_ffn_down', 'new_m_w_in', 'new_m_b_fgate', 'new_m_conv_w', 'new_m_conv_b', 'new_m_w_rgate', 'new_m_b_rgate', 'new_m_w_igate', 'new_m_b_igate', 'new_m_lru_lambda', 'new_m_g_qk', 'new_m_g_mix_out', 'new_m_w_out', 'new_v_w_ada', 'new_v_b_ada', 'new_v_g_norm', 'new_v_w_ffn_up', 'new_v_w_ffn_down', 'new_v_w_in', 'new_v_b_fgate', 'new_v_conv_w', 'new_v_conv_b', 'new_v_w_rgate', 'new_v_b_rgate', 'new_v_w_igate', 'new_v_b_igate', 'new_v_lru_lambda', 'new_v_g_qk', 'new_v_g_mix_out', 'new_v_w_out']
TWIN_LEAF_KINDS = {'loss': 'loss', 'grad_x': 'grad_x', 'grad_w_ada': 'grad_w', 'grad_b_ada': 'grad_w', 'grad_g_norm': 'grad_w', 'grad_w_ffn_up': 'grad_w', 'grad_w_ffn_down': 'grad_w', 'grad_w_in': 'grad_w', 'grad_b_fgate': 'grad_w', 'grad_conv_w': 'grad_w', 'grad_conv_b': 'grad_w', 'grad_w_rgate': 'grad_w', 'grad_b_rgate': 'grad_w', 'grad_w_igate': 'grad_w', 'grad_b_igate': 'grad_w', 'grad_lru_lambda': 'grad_w', 'grad_g_qk': 'grad_w', 'grad_g_mix_out': 'grad_w', 'grad_w_out': 'grad_w', 'delta_w_ada': 'delta_w', 'delta_b_ada': 'delta_w', 'delta_g_norm': 'delta_w', 'delta_w_ffn_up': 'delta_w', 'delta_w_ffn_down': 'delta_w', 'delta_w_in': 'delta_w', 'delta_b_fgate': 'delta_w', 'delta_conv_w': 'delta_w', 'delta_conv_b': 'delta_w', 'delta_w_rgate': 'delta_w', 'delta_b_rgate': 'delta_w', 'delta_w_igate': 'delta_w', 'delta_b_igate': 'delta_w', 'delta_lru_lambda': 'delta_w', 'delta_g_qk': 'delta_w', 'delta_g_mix_out': 'delta_w', 'delta_w_out': 'delta_w', 'new_m_w_ada': 'new_m', 'new_m_b_ada': 'new_m', 'new_m_g_norm': 'new_m', 'new_m_w_ffn_up': 'new_m', 'new_m_w_ffn_down': 'new_m', 'new_m_w_in': 'new_m', 'new_m_b_fgate': 'new_m', 'new_m_conv_w': 'new_m', 'new_m_conv_b': 'new_m', 'new_m_w_rgate': 'new_m', 'new_m_b_rgate': 'new_m', 'new_m_w_igate': 'new_m', 'new_m_b_igate': 'new_m', 'new_m_lru_lambda': 'new_m', 'new_m_g_qk': 'new_m', 'new_m_g_mix_out': 'new_m', 'new_m_w_out': 'new_m', 'new_v_w_ada': 'new_v', 'new_v_b_ada': 'new_v', 'new_v_g_norm': 'new_v', 'new_v_w_ffn_up': 'new_v', 'new_v_w_ffn_down': 'new_v', 'new_v_w_in': 'new_v', 'new_v_b_fgate': 'new_v', 'new_v_conv_w': 'new_v', 'new_v_conv_b': 'new_v', 'new_v_w_rgate': 'new_v', 'new_v_b_rgate': 'new_v', 'new_v_w_igate': 'new_v', 'new_v_b_igate': 'new_v', 'new_v_lru_lambda': 'new_v', 'new_v_g_qk': 'new_v', 'new_v_g_mix_out': 'new_v', 'new_v_w_out': 'new_v'}


def _forward(args):
    return _fwd_reference(*[args[k] for k in FWD_PARAMS])


def _output_shape():
    out = _jax.eval_shape(lambda: _forward(_fwd_setup_inputs(0)))
    return out.shape, out.dtype

N_MICROBATCH = 1
ADAM_LR = 0.001
ADAM_B1 = 0.9
ADAM_B2 = 0.999
ADAM_EPS = 1e-08
ADAM_WD = 0.01
ADAM_STEP = 10
PER_EXAMPLE_BATCH_AXIS = {'x': 0, 'c': 0, 'loss_target': 0}
SHARED_INPUTS = []
_WEIGHT_DTYPES = {'w_ada': _jnp.float32, 'b_ada': _jnp.float32, 'g_norm': _jnp.float32, 'w_ffn_up': _jnp.float32, 'w_ffn_down': _jnp.float32, 'w_in': _jnp.float32, 'b_fgate': _jnp.float32, 'conv_w': _jnp.float32, 'conv_b': _jnp.float32, 'w_rgate': _jnp.float32, 'b_rgate': _jnp.float32, 'w_igate': _jnp.float32, 'b_igate': _jnp.float32, 'lru_lambda': _jnp.float32, 'g_qk': _jnp.float32, 'g_mix_out': _jnp.float32, 'w_out': _jnp.float32}
MOMENT_SCALE = {'w_ada': 1.616899e+01, 'b_ada': 3.660617e+01, 'g_norm': 9.702967e+00, 'w_ffn_up': 3.552184e-01, 'w_ffn_down': 6.396489e-01, 'w_in': 3.339689e+00, 'b_fgate': 2.017861e+01, 'conv_w': 1.138708e+01, 'conv_b': 7.509518e+01, 'w_rgate': 3.053067e+00, 'b_rgate': 1.945169e+00, 'w_igate': 5.565048e+00, 'b_igate': 4.705268e+00, 'lru_lambda': 3.521245e+00, 'g_qk': 1.115822e+00, 'g_mix_out': 9.105058e+01, 'w_out': 8.085843e+00}


def _to_microbatches(a, axis):
    t = _jnp.moveaxis(a, axis, 0)
    t = t.reshape((N_MICROBATCH, t.shape[0] // N_MICROBATCH) + t.shape[1:])
    return _jnp.moveaxis(t, 1, axis + 1)


def setup_inputs(seed: int = 0) -> dict:
    inp = _fwd_setup_inputs(seed)
    key = _jax.random.fold_in(_jax.random.key(seed), 7919)
    shape, _ = _output_shape()
    out = dict(inp)
    out["loss_target"] = _jax.random.normal(_jax.random.fold_in(key, 0), shape, _jnp.float32)
    for i, name in enumerate(TWIN_WEIGHTS):
        w = inp[name].astype(_jnp.float32)
        if MOMENT_SCALE is None:
            s = _jnp.sqrt(_jnp.mean(_jnp.square(w)) + 1e-30)
        else:
            s = MOMENT_SCALE[name]
        km, kv = _jax.random.split(_jax.random.fold_in(key, i + 1))
        out[name] = w
        out["m_" + name] = s * _jax.random.normal(km, w.shape, _jnp.float32)
        out["v_" + name] = (s * s) * _jax.random.uniform(kv, w.shape, _jnp.float32, 0.5, 1.5)
    if N_MICROBATCH > 1:
        for name, axis in PER_EXAMPLE_BATCH_AXIS.items():
            out[name] = _to_microbatches(out[name], axis)
    return {'x': out['x'], 'c': out['c'], 'w_ada': out['w_ada'], 'b_ada': out['b_ada'], 'g_norm': out['g_norm'], 'w_ffn_up': out['w_ffn_up'], 'w_ffn_down': out['w_ffn_down'], 'w_in': out['w_in'], 'b_fgate': out['b_fgate'], 'conv_w': out['conv_w'], 'conv_b': out['conv_b'], 'w_rgate': out['w_rgate'], 'b_rgate': out['b_rgate'], 'w_igate': out['w_igate'], 'b_igate': out['b_igate'], 'lru_lambda': out['lru_lambda'], 'g_qk': out['g_qk'], 'g_mix_out': out['g_mix_out'], 'w_out': out['w_out'], 'loss_target': out['loss_target'], 'm_w_ada': out['m_w_ada'], 'm_b_ada': out['m_b_ada'], 'm_g_norm': out['m_g_norm'], 'm_w_ffn_up': out['m_w_ffn_up'], 'm_w_ffn_down': out['m_w_ffn_down'], 'm_w_in': out['m_w_in'], 'm_b_fgate': out['m_b_fgate'], 'm_conv_w': out['m_conv_w'], 'm_conv_b': out['m_conv_b'], 'm_w_rgate': out['m_w_rgate'], 'm_b_rgate': out['m_b_rgate'], 'm_w_igate': out['m_w_igate'], 'm_b_igate': out['m_b_igate'], 'm_lru_lambda': out['m_lru_lambda'], 'm_g_qk': out['m_g_qk'], 'm_g_mix_out': out['m_g_mix_out'], 'm_w_out': out['m_w_out'], 'v_w_ada': out['v_w_ada'], 'v_b_ada': out['v_b_ada'], 'v_g_norm': out['v_g_norm'], 'v_w_ffn_up': out['v_w_ffn_up'], 'v_w_ffn_down': out['v_w_ffn_down'], 'v_w_in': out['v_w_in'], 'v_b_fgate': out['v_b_fgate'], 'v_conv_w': out['v_conv_w'], 'v_conv_b': out['v_conv_b'], 'v_w_rgate': out['v_w_rgate'], 'v_b_rgate': out['v_b_rgate'], 'v_w_igate': out['v_w_igate'], 'v_b_igate': out['v_b_igate'], 'v_lru_lambda': out['v_lru_lambda'], 'v_g_qk': out['v_g_qk'], 'v_g_mix_out': out['v_g_mix_out'], 'v_w_out': out['v_w_out']}


def _loss(weights, diff, rest, loss_target):
    with _jax.named_scope("forward"):
        args = {**rest, TWIN_DIFF_INPUT: diff, **{k: w.astype(_WEIGHT_DTYPES[k]) for k, w in weights.items()}}
        y = _forward(args)
    with _jax.named_scope("loss_head"):
        err = _jnp.square(y.astype(_jnp.float32) - loss_target)
        return 0.5 * _jnp.sum(_jnp.mean(err, axis=-1)) if err.ndim else 0.5 * err


def _adamw(w, g, m, v):
    m = ADAM_B1 * m + (1.0 - ADAM_B1) * g
    v = ADAM_B2 * v + (1.0 - ADAM_B2) * _jnp.square(g)
    m_hat = m / (1.0 - ADAM_B1 ** ADAM_STEP)
    v_hat = v / (1.0 - ADAM_B2 ** ADAM_STEP)
    delta = -ADAM_LR * (m_hat / (_jnp.sqrt(v_hat) + ADAM_EPS) + ADAM_WD * w)
    return delta, m, v


def reference(x, c, w_ada, b_ada, g_norm, w_ffn_up, w_ffn_down, w_in, b_fgate, conv_w, conv_b, w_rgate, b_rgate, w_igate, b_igate, lru_lambda, g_qk, g_mix_out, w_out, loss_target, m_w_ada, m_b_ada, m_g_norm, m_w_ffn_up, m_w_ffn_down, m_w_in, m_b_fgate, m_conv_w, m_conv_b, m_w_rgate, m_b_rgate, m_w_igate, m_b_igate, m_lru_lambda, m_g_qk, m_g_mix_out, m_w_out, v_w_ada, v_b_ada, v_g_norm, v_w_ffn_up, v_w_ffn_down, v_w_in, v_b_fgate, v_conv_w, v_conv_b, v_w_rgate, v_b_rgate, v_w_igate, v_b_igate, v_lru_lambda, v_g_qk, v_g_mix_out, v_w_out):
    given = dict(x=x, c=c, w_ada=w_ada, b_ada=b_ada, g_norm=g_norm, w_ffn_up=w_ffn_up, w_ffn_down=w_ffn_down, w_in=w_in, b_fgate=b_fgate, conv_w=conv_w, conv_b=conv_b, w_rgate=w_rgate, b_rgate=b_rgate, w_igate=w_igate, b_igate=b_igate, lru_lambda=lru_lambda, g_qk=g_qk, g_mix_out=g_mix_out, w_out=w_out, loss_target=loss_target, m_w_ada=m_w_ada, m_b_ada=m_b_ada, m_g_norm=m_g_norm, m_w_ffn_up=m_w_ffn_up, m_w_ffn_down=m_w_ffn_down, m_w_in=m_w_in, m_b_fgate=m_b_fgate, m_conv_w=m_conv_w, m_conv_b=m_conv_b, m_w_rgate=m_w_rgate, m_b_rgate=m_b_rgate, m_w_igate=m_w_igate, m_b_igate=m_b_igate, m_lru_lambda=m_lru_lambda, m_g_qk=m_g_qk, m_g_mix_out=m_g_mix_out, m_w_out=m_w_out, v_w_ada=v_w_ada, v_b_ada=v_b_ada, v_g_norm=v_g_norm, v_w_ffn_up=v_w_ffn_up, v_w_ffn_down=v_w_ffn_down, v_w_in=v_w_in, v_b_fgate=v_b_fgate, v_conv_w=v_conv_w, v_conv_b=v_conv_b, v_w_rgate=v_w_rgate, v_b_rgate=v_b_rgate, v_w_igate=v_w_igate, v_b_igate=v_b_igate, v_lru_lambda=v_lru_lambda, v_g_qk=v_g_qk, v_g_mix_out=v_g_mix_out, v_w_out=v_w_out)
    weights = {n: given[n] for n in TWIN_WEIGHTS}
    shared = {n: given[n] for n in SHARED_INPUTS}
    per_example = {n: given[n] for n in ['x', 'c']}
    grad_fn = _jax.value_and_grad(_loss, argnums=(0, 1))

    def one_microbatch(ex, loss_target):
        ex = dict(ex)
        diff = ex.pop(TWIN_DIFF_INPUT)
        return grad_fn(weights, diff, {**shared, **ex}, loss_target)

    if N_MICROBATCH == 1:
        loss, (grad_w, grad_x) = one_microbatch(per_example, given["loss_target"])
    else:
        def body(carry, xs):
            loss_sum, grad_sum = carry
            l_k, (gw_k, gx_k) = one_microbatch(xs[0], xs[1])
            with _jax.named_scope("update"):
                return (loss_sum + l_k, _jax.tree.map(_jnp.add, grad_sum, gw_k)), gx_k

        init = (_jnp.zeros((), _jnp.float32), _jax.tree.map(_jnp.zeros_like, weights))
        (loss, grad_w), grad_x = _jax.lax.scan(body, init, (per_example, given["loss_target"]))
    with _jax.named_scope("update"):
        delta_w, new_m, new_v = {}, {}, {}
        for n in TWIN_WEIGHTS:
            delta_w[n], new_m[n], new_v[n] = _adamw(weights[n], grad_w[n], given["m_" + n], given["v_" + n])
    return (loss, grad_x, *[grad_w[n] for n in TWIN_WEIGHTS], *[delta_w[n] for n in TWIN_WEIGHTS],
            *[new_m[n] for n in TWIN_WEIGHTS], *[new_v[n] for n in TWIN_WEIGHTS])
```

```python
import functools
import math

import jax
import jax.numpy as jnp
from jax import lax
from jax.experimental import pallas as pl
from jax.experimental.pallas import tpu as pltpu

F32 = jnp.float32
BF16 = jnp.bfloat16

N_DEV = 8
D_MODEL = 1024
D_FF = 2816
FF_HALF = D_FF // 2
HEAD_DIM = 64
LRU_W = 512
ATT_W = 256
N_IN = 2564
N_IN_PAD = 2688
LANES = 128
SUBLANES = 8
BLK = 128
EPS = 1e-6
LRU_C = 8.0
NEG_BIG = -1e30
VMEM_LIMIT_BYTES = 48 * 1024 * 1024

ADAM_LR, ADAM_B1, ADAM_B2, ADAM_EPS, ADAM_WD, ADAM_STEP = 0.001, 0.9, 0.999, 1e-08, 0.01, 10

COL_SBQ, COL_SBK, COL_SBV = 8, 10, 12
COL_FXV, COL_FXF = 18, 20

NN = (((1,), (0,)), ((), ()))
NT = (((1,), (1,)), ((), ()))
TN = (((0,), (0,)), ((), ()))


def _params(n_axes):
    return pltpu.CompilerParams(dimension_semantics=("arbitrary",) * n_axes, vmem_limit_bytes=VMEM_LIMIT_BYTES)


def _tok_tile(seq):
    for t in (512, 256, 128):
        if seq % t == 0:
            return t
    raise ValueError(f"sequence length {seq} is not a multiple of 128")


def _dot(a, b, dims=NN):
    return lax.dot_general(a, b, dims, preferred_element_type=F32)


def _sigmoid(x):
    return 1.0 / (1.0 + jnp.exp(-x))


def _softplus(x):
    return jnp.maximum(x, 0.0) + jnp.log(1.0 + jnp.exp(-jnp.abs(x)))


def _gelu_parts(x):
    k0, k1 = math.sqrt(2.0 / math.pi), 0.044715
    t = jnp.tanh(k0 * (x + k1 * x * x * x))
    gelu = 0.5 * x * (1.0 + t)
    dgelu = 0.5 * (1.0 + t) + 0.5 * x * (1.0 - t * t) * k0 * (1.0 + 3.0 * k1 * x * x)
    return gelu, dgelu


def _neg_expm1(x):
    series = -x * (1.0 + x * (0.5 + x * (1.0 / 6.0 + x * (1.0 / 24.0 + x * (1.0 / 120.0 + x * (1.0 / 720.0))))))
    return jnp.where(x > -0.25, series, 1.0 - jnp.exp(x))


def _split2(x):
    hi = x.astype(BF16)
    lo = (x - hi.astype(F32)).astype(BF16)
    return hi, lo


def _split3(x):
    hi = x.astype(BF16)
    r = x - hi.astype(F32)
    mid = r.astype(BF16)
    lo = (r - mid.astype(F32)).astype(BF16)
    return hi, mid, lo


def _rows_to_block(rows, width):
    r = lax.broadcasted_iota(jnp.int32, (SUBLANES, width), 0)
    out = jnp.zeros((SUBLANES, width), F32)
    for n, v in enumerate(rows):
        out = jnp.where(r == n, jnp.broadcast_to(v, (SUBLANES, width)), out)
    return out


def _colsum(x):
    return jnp.sum(x, axis=0, keepdims=True)


def _exchange(gathers, scatters, name):
    n_g = len(gathers)
    ops = list(gathers) + list(scatters)
    n = len(ops)
    out_shape = [jax.ShapeDtypeStruct((N_DEV,) + a.shape, a.dtype) for a in gathers]
    out_shape += [jax.ShapeDtypeStruct(a.shape, a.dtype) for a in scatters]

    def body(*refs):
        ins, outs = refs[:n], refs[n:2 * n]
        send_sems, recv_sems, local_sems = refs[2 * n:]
        x, y, c = lax.axis_index("x"), lax.axis_index("y"), lax.axis_index("c")
        me = 4 * x + 2 * y + c

        local = []
        for a in range(n):
            src = ins[a] if a < n_g else ins[a].at[me]
            cp = pltpu.make_async_copy(src, outs[a].at[me], local_sems.at[a])
            cp.start()
            local.append(cp)
        remote = []
        for r in range(1, N_DEV):
            px = 1 - x if (r >> 2) & 1 else x
            py = 1 - y if (r >> 1) & 1 else y
            pc = 1 - c if r & 1 else c
            peer = 4 * px + 2 * py + pc
            for a in range(n):
                src = ins[a] if a < n_g else ins[a].at[peer]
                cp = pltpu.make_async_remote_copy(
                    src_ref=src, dst_ref=outs[a].at[me],
                    send_sem=send_sems.at[a, r - 1], recv_sem=recv_sems.at[a, r - 1],
                    device_id=(px, py, pc), device_id_type=pl.DeviceIdType.MESH)
                cp.start()
                remote.append(cp)
        for cp in remote:
            cp.wait()
        for cp in local:
            cp.wait()

    hbm = pl.BlockSpec(memory_space=pltpu.HBM)
    return pl.pallas_call(
        body, name=name, out_shape=out_shape,
        in_specs=[hbm] * n, out_specs=[hbm] * n,
        scratch_shapes=[pltpu.SemaphoreType.DMA((n, N_DEV - 1)), pltpu.SemaphoreType.DMA((n, N_DEV - 1)),
                        pltpu.SemaphoreType.DMA((n,))],
    )(*ops)


def _mm(a, b, *, mode, tm, tn, tk, outs, epilogue, name, extras=(), a_lead=(), b_lead=(), a_pre=None):
    if mode == "tn":
        kdim, mdim = a.shape[-2:]
    else:
        mdim, kdim = a.shape[-2:]
    ndim = b.shape[-2] if mode == "nt" else b.shape[-1]
    assert mdim % tm == 0 and ndim % tn == 0 and kdim % tk == 0, (name, mdim, ndim, kdim, tm, tn, tk)
    ni, nj, nk = mdim // tm, ndim // tn, kdim // tk
    a_lead, b_lead = tuple(a_lead), tuple(b_lead)
    a_block = (None,) * len(a_lead) + ((tk, tm) if mode == "tn" else (tm, tk))
    b_block = (None,) * len(b_lead) + ((tn, tk) if mode == "nt" else (tk, tn))
    dims = {"nn": NN, "nt": NT, "tn": TN}[mode]
    ne, no = len(extras), len(outs)

    def a_index(i, j, k):
        return a_lead + ((k, i) if mode == "tn" else (i, k))

    def b_index(i, j, k):
        return b_lead + ((j, k) if mode == "nt" else (k, j))

    def body(*refs):
        a_ref, b_ref = refs[0], refs[1]
        e_refs, o_refs = refs[2:2 + ne], refs[2 + ne:2 + ne + no]
        av, bv = a_ref[...], b_ref[...]
        if a_pre is not None:
            av = a_pre(av)
        p = _dot(av.astype(BF16), bv.astype(BF16), dims)
        if nk == 1:
            epilogue(p, e_refs, o_refs)
        else:
            acc = refs[-1]
            k = pl.program_id(2)

            @pl.when(k == 0)
            def _():
                acc[...] = p

            @pl.when(k > 0)
            def _():
                acc[...] += p

            @pl.when(k == nk - 1)
            def _():
                epilogue(acc[...], e_refs, o_refs)

    in_specs = [pl.BlockSpec(a_block, a_index), pl.BlockSpec(b_block, b_index)]
    in_specs += [pl.BlockSpec(blk, functools.partial(lambda i, j, k, f: f(i, j), f=f)) for _, blk, f in extras]
    out_specs = [pl.BlockSpec(blk, functools.partial(lambda i, j, k, f: f(i, j), f=f)) for _, _, blk, f in outs]
    res = pl.pallas_call(
        body, name=name, grid=(ni, nj, nk), in_specs=in_specs, out_specs=out_specs,
        out_shape=[jax.ShapeDtypeStruct(s, d) for s, d, _, _ in outs],
        scratch_shapes=[pltpu.VMEM((tm, tn), F32)] if nk > 1 else [],
        compiler_params=_params(3),
    )(a, b, *[e[0] for e in extras])
    return res


def _store_epilogue(dtypes):
    def epi(p, e_refs, o_refs):
        for o, dt in zip(o_refs, dtypes):
            o[...] = p.astype(dt)
    return epi


def _normmod(x, gn, scale, shift, seq, name):
    m, d = x.shape
    tm = _tok_tile(seq)
    tpb = seq // tm

    def body(x_ref, gn_ref, sc_ref, sh_ref, h_ref):
        xv = x_ref[...]
        rstd = lax.rsqrt(jnp.mean(xv * xv, axis=-1, keepdims=True) + EPS)
        h_ref[...] = (xv * rstd * gn_ref[...] * (1.0 + sc_ref[0]) + sh_ref[0]).astype(BF16)

    vec = pl.BlockSpec((1, 1, d), lambda i: (i // tpb, 0, 0))
    return pl.pallas_call(
        body, name=name, grid=(m // tm,),
        in_specs=[pl.BlockSpec((tm, d), lambda i: (i, 0)), pl.BlockSpec((1, d), lambda i: (0, 0)), vec, vec],
        out_specs=pl.BlockSpec((tm, d), lambda i: (i, 0)),
        out_shape=jax.ShapeDtypeStruct((m, d), BF16), compiler_params=_params(1),
    )(x, gn, scale, shift)


def _normmod_bwd_epilogue(p, e_refs, o_refs):
    x_ref, dxo_ref, gn_ref, sc_ref = e_refs
    xv = x_ref[...]
    rstd = lax.rsqrt(jnp.mean(xv * xv, axis=-1, keepdims=True) + EPS)
    xhat = xv * rstd
    gn, sc1 = gn_ref[...], 1.0 + sc_ref[0]
    dxhat = p * (gn * sc1)
    dx = rstd * (dxhat - xhat * jnp.mean(dxhat * xhat, axis=-1, keepdims=True))
    o_refs[0][...] = dxo_ref[...] + dx
    t = p * xhat
    o_refs[1][0] = _rows_to_block([_colsum(p), _colsum(t * gn), _colsum(t * sc1)], p.shape[1])


def _residual_bwd(dx, f, gate, fac, seq, name):
    m, d = dx.shape
    tm = _tok_tile(seq)
    tpb = seq // tm

    def body(dx_ref, f_ref, g_ref, df_ref, dg_ref):
        dxv = dx_ref[...]
        df_ref[...] = ((fac * (1.0 + g_ref[0])) * dxv).astype(BF16)
        dg_ref[0] = _rows_to_block([_colsum((fac * dxv) * f_ref[...].astype(F32))], d)

    tile = pl.BlockSpec((tm, d), lambda i: (i, 0))
    return pl.pallas_call(
        body, name=name, grid=(m // tm,),
        in_specs=[tile, tile, pl.BlockSpec((1, 1, d), lambda i: (i // tpb, 0, 0))],
        out_specs=[tile, pl.BlockSpec((1, SUBLANES, d), lambda i: (i, 0, 0))],
        out_shape=[jax.ShapeDtypeStruct((m, d), BF16), jax.ShapeDtypeStruct((m // tm, SUBLANES, d), F32)],
        compiler_params=_params(1),
    )(dx, f, gate)


def _loss_head(y, target, seq):
    m, d = y.shape
    tm = _tok_tile(seq)

    def body(y_ref, t_ref, dy_ref, l_ref):
        err = y_ref[...] - t_ref[...]
        dy_ref[...] = err * (1.0 / d)
        part = 0.5 * jnp.sum(jnp.mean(err * err, axis=-1, keepdims=True), axis=0, keepdims=True)
        l_ref[0] = jnp.broadcast_to(part, (SUBLANES, LANES))

    tile = pl.BlockSpec((tm, d), lambda i: (i, 0))
    return pl.pallas_call(
        body, name="loss_head", grid=(m // tm,), in_specs=[tile, tile],
        out_specs=[tile, pl.BlockSpec((1, SUBLANES, LANES), lambda i: (i, 0, 0))],
        out_shape=[jax.ShapeDtypeStruct((m, d), F32), jax.ShapeDtypeStruct((m // tm, SUBLANES, LANES), F32)],
        compiler_params=_params(1),
    )(y, target)


def _ffn_fwd(x, h, wup, wdown, lead, gate, seq, tag):
    m, d = x.shape
    tm = _tok_tile(seq)
    tpb = seq // tm

    def up_epilogue(p, e_refs, o_refs):
        g, u = p[:, :FF_HALF], p[:, FF_HALF:]
        o_refs[0][...] = (g * _sigmoid(g) * u).astype(BF16)
        o_refs[1][...] = p.astype(BF16)

    a, gu = _mm(h, wup, mode="nn", tm=tm, tn=2 * FF_HALF, tk=d, b_lead=lead, name=f"ffn_up_{tag}",
                outs=[((m, D_FF), BF16, (tm, FF_HALF), lambda i, j: (i, j)),
                      ((m, 2 * D_FF), BF16, (tm, 2 * FF_HALF), lambda i, j: (i, j))],
                epilogue=up_epilogue)

    def down_epilogue(p, e_refs, o_refs):
        x_ref, g_ref = e_refs
        o_refs[0][...] = x_ref[...] + (0.5 * (1.0 + g_ref[0])) * p
        o_refs[1][...] = p.astype(BF16)

    x_out, f = _mm(a, wdown, mode="nn", tm=tm, tn=d, tk=D_FF, b_lead=lead, name=f"ffn_down_{tag}",
                   extras=[(x, (tm, d), lambda i, j: (i, 0)), (gate, (1, 1, d), lambda i, j: (i // tpb, 0, 0))],
                   outs=[((m, d), F32, (tm, d), lambda i, j: (i, 0)), ((m, d), BF16, (tm, d), lambda i, j: (i, 0))],
                   epilogue=down_epilogue)
    return x_out, (a, gu, f)


def _ffn_bwd(dx_out, x, h, saved, wup, wdown, lead, gn, scale, gate, seq, tag):
    a, gu, f = saved
    m, d = x.shape
    tm = _tok_tile(seq)
    tpb = seq // tm
    df, dgate_parts = _residual_bwd(dx_out, f, gate, 0.5, seq, f"ffn_res_bwd_{tag}")

    def act_bwd_epilogue(p, e_refs, o_refs):
        guv = e_refs[0][...].astype(F32)
        g, u = guv[:, :FF_HALF], guv[:, FF_HALF:]
        sg = _sigmoid(g)
        silu = g * sg
        dsilu = sg * (1.0 + g * (1.0 - sg))
        o_refs[0][...] = jnp.concatenate([p * u * dsilu, p * silu], axis=1).astype(BF16)

    (dgu,) = _mm(df, wdown, mode="nt", tm=tm, tn=FF_HALF, tk=d, b_lead=lead, name=f"ffn_down_dx_{tag}",
                 extras=[(gu, (tm, 2 * FF_HALF), lambda i, j: (i, j))],
                 outs=[((m, 2 * D_FF), BF16, (tm, 2 * FF_HALF), lambda i, j: (i, j))],
                 epilogue=act_bwd_epilogue)
    (dwdown,) = _mm(a, df, mode="tn", tm=FF_HALF, tn=d, tk=tm, name=f"ffn_dwdown_{tag}",
                    outs=[((D_FF, d), BF16, (FF_HALF, d), lambda i, j: (i, j))], epilogue=_store_epilogue([BF16]))
    (dwup,) = _mm(h, dgu, mode="tn", tm=d, tn=FF_HALF, tk=tm, name=f"ffn_dwup_{tag}",
                  outs=[((d, 2 * D_FF), BF16, (d, FF_HALF), lambda i, j: (i, j))], epilogue=_store_epilogue([BF16]))
    dx, nm_parts = _mm(dgu, wup, mode="nt", tm=tm, tn=d, tk=FF_HALF, b_lead=lead, name=f"ffn_up_dx_{tag}",
                       extras=[(x, (tm, d), lambda i, j: (i, 0)), (dx_out, (tm, d), lambda i, j: (i, 0)),
                               (gn, (1, d), lambda i, j: (0, 0)), (scale, (1, 1, d), lambda i, j: (i // tpb, 0, 0))],
                       outs=[((m, d), F32, (tm, d), lambda i, j: (i, 0)),
                             ((m // tm, SUBLANES, d), F32, (1, SUBLANES, d), lambda i, j: (i, 0, 0))],
                       epilogue=_normmod_bwd_epilogue)
    return dx, dwup, dwdown, nm_parts, dgate_parts


def _shift_down(ext, n, rows):
    if n:
        ext = pltpu.roll(ext, n, 0)
    return ext[SUBLANES:SUBLANES + rows]


def _lru_gates(u, wr_ref, br_ref, wi_ref, bi_ref, lam_ref):
    ub = u.astype(BF16)
    r = _sigmoid(_dot(ub, wr_ref[...]) + br_ref[...])
    ig = _sigmoid(_dot(ub, wi_ref[...]) + bi_ref[...])
    sp = _softplus(-lam_ref[...])
    log_a = (-LRU_C * r) * sp
    a = jnp.exp(log_a)
    mult = jnp.sqrt(_neg_expm1(2.0 * log_a))
    return r, ig, sp, a, mult


def _conv(ext, cw_ref, cb_ref, rows):
    u = cb_ref[...] + cw_ref[3:4, :] * _shift_down(ext, 0, rows)
    for k in range(3):
        u = u + cw_ref[k:k + 1, :] * _shift_down(ext, 3 - k, rows)
    return u


def _lru_halo_spec(seq, ts):
    return pl.BlockSpec((SUBLANES, LRU_W),
                        lambda b, i: (jnp.maximum(b * (seq // SUBLANES) + i * (ts // SUBLANES) - 1, 0), 0))


def _lru_fwd(proj32, conv_w, conv_b, wr, br, wi, bi, lam, batch, seq):
    m = proj32.shape[0]
    ts = _tok_tile(seq)
    nt = seq // ts
    row = lambda b, i: (b * nt + i, 0)

    def body(x_ref, halo_ref, g_ref, cw_ref, cb_ref, wr_ref, br_ref, wi_ref, bi_ref, lam_ref,
             y_ref, h_ref, a_scr, b_scr, carry):
        i = pl.program_id(1)
        halo = jnp.where(i > 0, halo_ref[...], 0.0)
        ext = jnp.concatenate([halo, x_ref[...]], axis=0)
        u = _conv(ext, cw_ref, cb_ref, ts)
        _, ig, _, a, mult = _lru_gates(u, wr_ref, br_ref, wi_ref, bi_ref, lam_ref)
        a_scr[...] = a
        b_scr[...] = mult * (ig * u)

        @pl.when(i == 0)
        def _():
            carry[...] = jnp.zeros_like(carry)

        rid = lax.broadcasted_iota(jnp.int32, (SUBLANES, LRU_W), 0)

        def chunk(c, hprev):
            off = pl.multiple_of(c * SUBLANES, SUBLANES)
            av, bv = a_scr[pl.ds(off, SUBLANES), :], b_scr[pl.ds(off, SUBLANES), :]
            for d in (1, 2, 4):
                keep = rid >= d
                bv = jnp.where(keep, av * pltpu.roll(bv, d, 0) + bv, bv)
                av = jnp.where(keep, av * pltpu.roll(av, d, 0), av)
            h = av * hprev + bv
            h_ref[pl.ds(off, SUBLANES), :] = h
            return h[SUBLANES - 1:SUBLANES, :]

        carry[...] = lax.fori_loop(0, ts // SUBLANES, chunk, carry[...])
        gelu, _ = _gelu_parts(g_ref[...])
        y_ref[...] = h_ref[...] * gelu

    full = lambda shape: pl.BlockSpec(shape, lambda b, i: (0,) * len(shape))
    return pl.pallas_call(
        body, name="lru_fwd", grid=(batch, nt),
        in_specs=[pl.BlockSpec((ts, LRU_W), row), _lru_halo_spec(seq, ts),
                  pl.BlockSpec((ts, LRU_W), lambda b, i: (b * nt + i, 1)),
                  full((4, LRU_W)), full((1, LRU_W)), full((LRU_W, LRU_W)), full((1, LRU_W)),
                  full((LRU_W, LRU_W)), full((1, LRU_W)), full((1, LRU_W))],
        out_specs=[pl.BlockSpec((ts, LRU_W), row), pl.BlockSpec((ts, LRU_W), row)],
        out_shape=[jax.ShapeDtypeStruct((m, LRU_W), F32), jax.ShapeDtypeStruct((m, LRU_W), F32)],
        scratch_shapes=[pltpu.VMEM((ts, LRU_W), F32), pltpu.VMEM((ts, LRU_W), F32), pltpu.VMEM((1, LRU_W), F32)],
        compiler_params=_params(2),
    )(proj32, proj32, proj32, conv_w, conv_b, wr, br, wi, bi, lam)


def _lru_bwd(dy, proj32, h, conv_w, conv_b, wr, br, wi, bi, lam, batch, seq):
    m = proj32.shape[0]
    ts = _tok_tile(seq)
    nt = seq // ts
    row = lambda b, i: (b * nt + (nt - 1 - i), 0)
    halo = pl.BlockSpec((SUBLANES, LRU_W),
                        lambda b, i: (jnp.maximum(b * (seq // SUBLANES) + (nt - 1 - i) * (ts // SUBLANES) - 1, 0), 0))

    def body(dy_ref, x_ref, xhalo_ref, g_ref, h_ref, hhalo_ref, cw_ref, cb_ref, wr_ref, br_ref, wi_ref, bi_ref,
             lam_ref, dx_ref, dg_ref, dwr_ref, dwi_ref, sums_ref, a_scr, dh_scr, g_scr, carry, du_next):
        b, i = pl.program_id(0), pl.program_id(1)
        first_tile = i == nt - 1

        @pl.when((b == 0) & (i == 0))
        def _():
            dwr_ref[...] = jnp.zeros_like(dwr_ref)
            dwi_ref[...] = jnp.zeros_like(dwi_ref)
            sums_ref[...] = jnp.zeros_like(sums_ref)

        @pl.when(i == 0)
        def _():
            carry[...] = jnp.zeros_like(carry)
            du_next[...] = jnp.zeros_like(du_next)

        xhalo = jnp.where(first_tile, 0.0, xhalo_ref[...])
        ext = jnp.concatenate([xhalo, x_ref[...]], axis=0)
        u = _conv(ext, cw_ref, cb_ref, ts)
        r, ig, sp, a, mult = _lru_gates(u, wr_ref, br_ref, wi_ref, bi_ref, lam_ref)
        gelu, dgelu = _gelu_parts(g_ref[...])
        dyv, hv = dy_ref[...], h_ref[...]
        dg_ref[...] = (dyv * hv * dgelu).astype(BF16)
        a_scr[...] = a
        dh_scr[...] = dyv * gelu

        rid = lax.broadcasted_iota(jnp.int32, (SUBLANES, LRU_W), 0)
        nchunk = ts // SUBLANES

        def chunk(n, cg):
            off = pl.multiple_of((nchunk - 1 - n) * SUBLANES, SUBLANES)
            av, beta = a_scr[pl.ds(off, SUBLANES), :], dh_scr[pl.ds(off, SUBLANES), :]
            alpha = jnp.where(rid == SUBLANES - 1, 1.0, pltpu.roll(av, SUBLANES - 1, 0))
            for d in (1, 2, 4):
                keep = rid + d <= SUBLANES - 1
                beta = jnp.where(keep, beta + alpha * pltpu.roll(beta, SUBLANES - d, 0), beta)
                alpha = jnp.where(keep, alpha * pltpu.roll(alpha, SUBLANES - d, 0), alpha)
            gv = beta + alpha * cg
            g_scr[pl.ds(off, SUBLANES), :] = gv
            return av[0:1, :] * gv[0:1, :]

        carry[...] = lax.fori_loop(0, nchunk, chunk, carry[...])
        gv = g_scr[...]
        hhalo = jnp.where(first_tile, 0.0, hhalo_ref[...])
        hprev = _shift_down(jnp.concatenate([hhalo, hv], axis=0), 1, ts)
        dmult = gv * ig * u
        dig = gv * mult * u
        du = gv * mult * ig
        dlog_a = gv * hprev * a - dmult * a * a / mult
        dr = dlog_a * (-LRU_C * sp)
        dr_pre = dr * r * (1.0 - r)
        di_pre = dig * ig * (1.0 - ig)
        drb, dib, ub = dr_pre.astype(BF16), di_pre.astype(BF16), u.astype(BF16)
        du = du + _dot(drb, wr_ref[...], NT) + _dot(dib, wi_ref[...], NT)
        dwr_ref[...] += _dot(ub, drb, TN)
        dwi_ref[...] += _dot(ub, dib, TN)

        ext_du = jnp.concatenate([du, du_next[...]], axis=0)
        du_next[...] = du[0:SUBLANES, :]
        n_ext = ts + SUBLANES
        dx = cw_ref[3:4, :] * du
        sums = [_colsum(dr_pre), _colsum(di_pre), _colsum(dlog_a * (-LRU_C * r)), _colsum(du)]
        dcw = []
        for k in range(3):
            dx = dx + cw_ref[k:k + 1, :] * pltpu.roll(ext_du, n_ext - (3 - k), 0)[0:ts]
            dcw.append(_colsum(du * _shift_down(ext, 3 - k, ts)))
        dcw.append(_colsum(du * _shift_down(ext, 0, ts)))
        dx_ref[...] = dx.astype(BF16)
        sums_ref[...] += _rows_to_block(sums + dcw, LRU_W)

    full = lambda shape: pl.BlockSpec(shape, lambda b, i: (0,) * len(shape))
    tile = pl.BlockSpec((ts, LRU_W), row)
    return pl.pallas_call(
        body, name="lru_bwd", grid=(batch, nt),
        in_specs=[tile, tile, halo, pl.BlockSpec((ts, LRU_W), lambda b, i: (b * nt + (nt - 1 - i), 1)), tile, halo,
                  full((4, LRU_W)), full((1, LRU_W)), full((LRU_W, LRU_W)), full((1, LRU_W)),
                  full((LRU_W, LRU_W)), full((1, LRU_W)), full((1, LRU_W))],
        out_specs=[tile, tile, full((LRU_W, LRU_W)), full((LRU_W, LRU_W)), full((SUBLANES, LRU_W))],
        out_shape=[jax.ShapeDtypeStruct((m, LRU_W), BF16), jax.ShapeDtypeStruct((m, LRU_W), BF16),
                   jax.ShapeDtypeStruct((LRU_W, LRU_W), F32), jax.ShapeDtypeStruct((LRU_W, LRU_W), F32),
                   jax.ShapeDtypeStruct((SUBLANES, LRU_W), F32)],
        scratch_shapes=[pltpu.VMEM((ts, LRU_W), F32), pltpu.VMEM((ts, LRU_W), F32), pltpu.VMEM((ts, LRU_W), F32),
                        pltpu.VMEM((1, LRU_W), F32), pltpu.VMEM((SUBLANES, LRU_W), F32)],
        compiler_params=_params(2),
    )(dy, proj32, proj32, proj32, h, h, conv_w, conv_b, wr, br, wi, bi, lam)


def _head_masks():
    lane = lax.broadcasted_iota(jnp.int32, (1, LANES), 1)
    return lane < HEAD_DIM


def _tri(cmp):
    r = lax.broadcasted_iota(jnp.int32, (BLK, BLK), 0)
    c = lax.broadcasted_iota(jnp.int32, (BLK, BLK), 1)
    return cmp(r, c)


def _dot_split(x, tri):
    hi, lo = _split2(x)
    return _dot(hi, tri) + _dot(lo, tri)


def _sb_fwd(proj16, batch, seq):
    nq = seq // BLK
    scale = HEAD_DIM ** -0.5

    def body(q_ref, k_ref, v_ref, y_ref, t_ref):
        qi = pl.program_id(2)
        lo = _head_masks()
        q2 = q_ref[0]
        heads = (jnp.where(lo, q2, jnp.zeros_like(q2)), jnp.where(lo, jnp.zeros_like(q2), q2))
        valid = _tri(lambda r, c: c < r)
        tri_after = _tri(lambda r, c: r > c).astype(BF16)

        def block(kb, carry, masked):
            ks = pl.multiple_of(kb * BLK, BLK)
            k2, v2 = k_ref[0, pl.ds(ks, BLK), :], v_ref[0, pl.ds(ks, BLK), :]
            new = []
            for qh, (acc, c) in zip(heads, carry):
                z = _dot(qh, k2, NT) * scale
                sp = _softplus(z)
                l = -sp
                if masked:
                    l = jnp.where(valid, l, 0.0)
                w = jnp.exp((z - sp) + _dot_split(l, tri_after) + c)
                if masked:
                    w = jnp.where(valid, w, 0.0)
                new.append((acc + _dot(w.astype(BF16), v2), c + jnp.sum(l, axis=1, keepdims=True)))
            return tuple(new)

        zero = (jnp.zeros((BLK, LANES), F32), jnp.zeros((BLK, 1), F32))
        carry = block(qi, (zero, zero), True)
        carry = lax.fori_loop(0, qi, lambda n, cr: block(qi - 1 - n, cr, False), carry)
        (acc_a, c_a), (acc_b, c_b) = carry
        y_ref[...] = jnp.where(lo, acc_a, acc_b)
        t_ref[0] = jnp.where(lo, c_a, c_b)

    m = batch * seq
    return pl.pallas_call(
        body, name="sb_fwd", grid=(batch, 2, nq),
        in_specs=[pl.BlockSpec((1, BLK, LANES), lambda b, p, q: (b, q, COL_SBQ + p)),
                  pl.BlockSpec((1, seq, LANES), lambda b, p, q: (b, 0, COL_SBK + p)),
                  pl.BlockSpec((1, seq, LANES), lambda b, p, q: (b, 0, COL_SBV + p))],
        out_specs=[pl.BlockSpec((BLK, LANES), lambda b, p, q: (b * nq + q, p)),
                   pl.BlockSpec((1, BLK, LANES), lambda b, p, q: (p, b * nq + q, 0))],
        out_shape=[jax.ShapeDtypeStruct((m, ATT_W), F32), jax.ShapeDtypeStruct((2, m, LANES), F32)],
        compiler_params=_params(3),
    )(proj16, proj16, proj16)


def _sb_bwd(dy, t, proj16, batch, seq):
    nq = seq // BLK
    scale = HEAD_DIM ** -0.5

    def body(dy_ref, t_ref, q_ref, k_ref, v_ref, dq_ref, dk_ref, dv_ref):
        qi = pl.program_id(2)

        @pl.when(qi == 0)
        def _():
            dk_ref[...] = jnp.zeros_like(dk_ref)
            dv_ref[...] = jnp.zeros_like(dv_ref)

        lo = _head_masks()
        q2, dy2, t2 = q_ref[0], dy_ref[...], t_ref[0]
        dyb = dy2.astype(BF16)
        zq, zd = jnp.zeros_like(q2), jnp.zeros_like(dyb)
        heads = ((jnp.where(lo, q2, zq), jnp.where(lo, dyb, zd), t2[:, 0:1]),
                 (jnp.where(lo, zq, q2), jnp.where(lo, zd, dyb), t2[:, HEAD_DIM:HEAD_DIM + 1]))
        valid = _tri(lambda r, c: c < r)
        tri_incl = _tri(lambda r, c: r <= c).astype(BF16)
        tri_excl = _tri(lambda r, c: r < c).astype(BF16)

        def block(kb, carry, masked):
            ks = pl.multiple_of(kb * BLK, BLK)
            k2, v2 = k_ref[0, pl.ds(ks, BLK), :], v_ref[0, pl.ds(ks, BLK), :]
            new, dks, dvs = [], [], []
            for (qh, dyh, tot), (dq, pc, ec) in zip(heads, carry):
                z = _dot(qh, k2, NT) * scale
                sp = _softplus(z)
                l, b = -sp, z - sp
                sig = jnp.exp(b)
                if masked:
                    l = jnp.where(valid, l, 0.0)
                after = tot - (pc + _dot_split(l, tri_incl))
                w = jnp.exp(b + after)
                if masked:
                    w = jnp.where(valid, w, 0.0)
                e = _dot(dyh, v2, NT) * w
                et = ec + _dot_split(e, tri_excl)
                dz = e * (1.0 - sig) - et * sig
                if masked:
                    dz = jnp.where(valid, dz, 0.0)
                dzb = (dz * scale).astype(BF16)
                dks.append(_dot(dzb, q2, TN))
                dvs.append(_dot(w.astype(BF16), dyb, TN))
                new.append((dq + _dot(dzb, k2), pc + jnp.sum(l, axis=1, keepdims=True),
                            ec + jnp.sum(e, axis=1, keepdims=True)))
            dk_ref[0, pl.ds(ks, BLK), :] += jnp.where(lo, dks[0], dks[1])
            dv_ref[0, pl.ds(ks, BLK), :] += jnp.where(lo, dvs[0], dvs[1])
            return tuple(new)

        zero = (jnp.zeros((BLK, LANES), F32), jnp.zeros((BLK, 1), F32), jnp.zeros((BLK, 1), F32))
        carry = lax.fori_loop(0, qi, lambda n, cr: block(n, cr, False), (zero, zero))
        (dq_a, _, _), (dq_b, _, _) = block(qi, carry, True)
        dq_ref[...] = jnp.where(lo, dq_a, dq_b)

    m = batch * seq
    whole = lambda col: pl.BlockSpec((1, seq, LANES), lambda b, p, q: (b, 0, col + p))
    return pl.pallas_call(
        body, name="sb_bwd", grid=(batch, 2, nq),
        in_specs=[pl.BlockSpec((BLK, LANES), lambda b, p, q: (b * nq + q, p)),
                  pl.BlockSpec((1, BLK, LANES), lambda b, p, q: (p, b * nq + q, 0)),
                  pl.BlockSpec((1, BLK, LANES), lambda b, p, q: (b, q, COL_SBQ + p)),
                  whole(COL_SBK), whole(COL_SBV)],
        out_specs=[pl.BlockSpec((BLK, LANES), lambda b, p, q: (b * nq + q, p)), whole(0), whole(0)],
        out_shape=[jax.ShapeDtypeStruct((m, ATT_W), F32), jax.ShapeDtypeStruct((batch, seq, ATT_W), F32),
                   jax.ShapeDtypeStruct((batch, seq, ATT_W), F32)],
        compiler_params=_params(3),
    )(dy, t, proj16, proj16, proj16)


def _fox_pre(proj32, gq, gk, bf, group_mean, batch, seq):
    m = proj32.shape[0]
    ts = _tok_tile(seq)
    nt = seq // ts

    def body(q_ref, k_ref, f_ref, gq_ref, gk_ref, bf_ref, gm_ref, fq_ref, fk_ref, fc_ref, carry):
        i = pl.program_id(1)

        @pl.when(i == 0)
        def _():
            carry[...] = jnp.zeros_like(carry)

        gm = gm_ref[...]
        for src, g_ref, dst in ((q_ref, gq_ref, fq_ref), (k_ref, gk_ref, fk_ref)):
            v = src[...]
            ms = _dot_split(v * v, gm)
            dst[...] = (v * lax.rsqrt(ms + EPS) * g_ref[...]).astype(BF16)
        z = f_ref[...] + bf_ref[...]
        lf = jnp.minimum(z, 0.0) - jnp.log(1.0 + jnp.exp(-jnp.abs(z)))
        r = lax.broadcasted_iota(jnp.int32, (ts, ts), 0)
        c = lax.broadcasted_iota(jnp.int32, (ts, ts), 1)
        tri = (r >= c).astype(BF16)
        hi, mid, low = _split3(lf)
        fc = _dot(tri, hi) + _dot(tri, mid) + _dot(tri, low) + carry[...]
        fc_ref[...] = fc
        carry[...] = fc[ts - 1:ts, :]

    full = lambda shape: pl.BlockSpec(shape, lambda b, i: (0,) * len(shape))
    return pl.pallas_call(
        body, name="fox_pre", grid=(batch, nt),
        in_specs=[pl.BlockSpec((ts, ATT_W), lambda b, i: (b * nt + i, 7)),
                  pl.BlockSpec((ts, ATT_W), lambda b, i: (b * nt + i, 8)),
                  pl.BlockSpec((ts, LANES), lambda b, i: (b * nt + i, COL_FXF)),
                  full((1, ATT_W)), full((1, ATT_W)), full((1, LANES)), full((ATT_W, ATT_W))],
        out_specs=[pl.BlockSpec((ts, ATT_W), lambda b, i: (b * nt + i, 0)),
                   pl.BlockSpec((ts, ATT_W), lambda b, i: (b * nt + i, 0)),
                   pl.BlockSpec((ts, LANES), lambda b, i: (b * nt + i, 0))],
        out_shape=[jax.ShapeDtypeStruct((m, ATT_W), BF16), jax.ShapeDtypeStruct((m, ATT_W), BF16),
                   jax.ShapeDtypeStruct((m, LANES), F32)],
        scratch_shapes=[pltpu.VMEM((1, LANES), F32)],
        compiler_params=_params(2),
    )(proj32, proj32, proj32, gq, gk, bf, group_mean)


def _fox_specs(batch, seq):
    nq = seq // BLK
    return dict(
        qblk=pl.BlockSpec((1, BLK, LANES), lambda b, p, q: (b, q, p)),
        whole=pl.BlockSpec((1, seq, LANES), lambda b, p, q: (b, 0, p)),
        vwhole=pl.BlockSpec((1, seq, LANES), lambda b, p, q: (b, 0, COL_FXV + p)),
        fcol=pl.BlockSpec((1, 1, BLK, 2), lambda b, p, q: (b, p, q, 0)),
        frow=pl.BlockSpec((1, 1, 2, seq), lambda b, p, q: (b, p, 0, 0)),
        rows=pl.BlockSpec((BLK, LANES), lambda b, p, q: (b * nq + q, p)),
        stat=pl.BlockSpec((1, BLK, LANES), lambda b, p, q: (p, b * nq + q, 0)),
    )


def _fox_fwd(fq, fk, proj16, fcol, frow, batch, seq):
    nq = seq // BLK
    scale = HEAD_DIM ** -0.5

    def body(q_ref, k_ref, v_ref, fc_ref, fr_ref, y_ref, lse_ref):
        qi = pl.program_id(2)
        lo = _head_masks()
        q2 = q_ref[0]
        zq = jnp.zeros_like(q2)
        fcv = fc_ref[0, 0]
        heads = ((jnp.where(lo, q2, zq), fcv[:, 0:1], 0), (jnp.where(lo, zq, q2), fcv[:, 1:2], 1))
        valid = _tri(lambda r, c: c <= r)

        def block(kb, carry, masked):
            ks = pl.multiple_of(kb * BLK, BLK)
            k2, v2 = k_ref[0, pl.ds(ks, BLK), :], v_ref[0, pl.ds(ks, BLK), :]
            new = []
            for (qh, fqh, hh), (acc, mx, den) in zip(heads, carry):
                s = _dot(qh, k2, NT) * scale + fqh - fr_ref[0, 0, hh:hh + 1, pl.ds(ks, BLK)]
                if masked:
                    s = jnp.where(valid, s, NEG_BIG)
                mx_new = jnp.maximum(mx, jnp.max(s, axis=1, keepdims=True))
                p = jnp.exp(s - mx_new)
                alpha = jnp.exp(mx - mx_new)
                new.append((alpha * acc + _dot(p.astype(BF16), v2), mx_new,
                            alpha * den + jnp.sum(p, axis=1, keepdims=True)))
            return tuple(new)

        zero = (jnp.zeros((BLK, LANES), F32), jnp.full((BLK, 1), NEG_BIG, F32), jnp.zeros((BLK, 1), F32))
        carry = lax.fori_loop(0, qi, lambda n, cr: block(n, cr, False), (zero, zero))
        (acc_a, mx_a, den_a), (acc_b, mx_b, den_b) = block(qi, carry, True)
        y_ref[...] = jnp.where(lo, acc_a / den_a, acc_b / den_b)
        lse_ref[0] = jnp.where(lo, mx_a + jnp.log(den_a), mx_b + jnp.log(den_b))

    m = batch * seq
    sp = _fox_specs(batch, seq)
    return pl.pallas_call(
        body, name="fox_fwd", grid=(batch, 2, nq),
        in_specs=[sp["qblk"], sp["whole"], sp["vwhole"], sp["fcol"], sp["frow"]],
        out_specs=[sp["rows"], sp["stat"]],
        out_shape=[jax.ShapeDtypeStruct((m, ATT_W), F32), jax.ShapeDtypeStruct((2, m, LANES), F32)],
        compiler_params=_params(3),
    )(fq, fk, proj16, fcol, frow)


def _fox_bwd(dy, y, lse, fq, fk, proj16, fcol, frow, batch, seq):
    nq = seq // BLK
    scale = HEAD_DIM ** -0.5

    def body(dy_ref, y_ref, lse_ref, q_ref, k_ref, v_ref, fc_ref, fr_ref, dq_ref, dk_ref, dv_ref, dfc_ref):
        qi = pl.program_id(2)

        @pl.when(qi == 0)
        def _():
            dk_ref[...] = jnp.zeros_like(dk_ref)
            dv_ref[...] = jnp.zeros_like(dv_ref)
            dfc_ref[...] = jnp.zeros_like(dfc_ref)

        lo = _head_masks()
        lane = lax.broadcasted_iota(jnp.int32, (1, LANES), 1)
        q2, dy2, lse2, fcv = q_ref[0], dy_ref[...], lse_ref[0], fc_ref[0, 0]
        dyb = dy2.astype(BF16)
        dyy = dy2 * y_ref[...]
        zq, zd = jnp.zeros_like(q2), jnp.zeros_like(dyb)
        heads = ((jnp.where(lo, q2, zq), jnp.where(lo, dyb, zd), lse2[:, 0:1], fcv[:, 0:1],
                  jnp.sum(jnp.where(lo, dyy, 0.0), axis=1, keepdims=True), 0),
                 (jnp.where(lo, zq, q2), jnp.where(lo, zd, dyb), lse2[:, HEAD_DIM:HEAD_DIM + 1], fcv[:, 1:2],
                  jnp.sum(jnp.where(lo, 0.0, dyy), axis=1, keepdims=True), 1))
        valid = _tri(lambda r, c: c <= r)
        ones = jnp.ones((BLK, LANES), BF16)

        def block(kb, carry, masked):
            ks = pl.multiple_of(kb * BLK, BLK)
            k2, v2 = k_ref[0, pl.ds(ks, BLK), :], v_ref[0, pl.ds(ks, BLK), :]
            new, dks, dvs, dfs = [], [], [], []
            for (qh, dyh, lse_h, fqh, delta, hh), (dq, rs) in zip(heads, carry):
                s = _dot(qh, k2, NT) * scale + fqh - fr_ref[0, 0, hh:hh + 1, pl.ds(ks, BLK)]
                p = jnp.exp(s - lse_h)
                if masked:
                    p = jnp.where(valid, p, 0.0)
                ds = p * (_dot(dyh, v2, NT) - delta)
                dsb = (ds * scale).astype(BF16)
                dks.append(_dot(dsb, q2, TN))
                dvs.append(_dot(p.astype(BF16), dyb, TN))
                ds_hi, ds_lo = _split2(ds)
                dfs.append(_dot(ds_hi, ones, TN) + _dot(ds_lo, ones, TN))
                new.append((dq + _dot(dsb, k2), rs + jnp.sum(ds, axis=1, keepdims=True)))
            dk_ref[0, pl.ds(ks, BLK), :] += jnp.where(lo, dks[0], dks[1])
            dv_ref[0, pl.ds(ks, BLK), :] += jnp.where(lo, dvs[0], dvs[1])
            dfc_ref[0, 0, pl.ds(ks, BLK), :] -= jnp.where(lane == 0, dfs[0], jnp.where(lane == 1, dfs[1], 0.0))
            return tuple(new)

        zero = (jnp.zeros((BLK, LANES), F32), jnp.zeros((BLK, 1), F32))
        carry = lax.fori_loop(0, qi, lambda n, cr: block(n, cr, False), (zero, zero))
        (dq_a, rs_a), (dq_b, rs_b) = block(qi, carry, True)
        dq_ref[...] = jnp.where(lo, dq_a, dq_b)
        qs = pl.multiple_of(qi * BLK, BLK)
        dfc_ref[0, 0, pl.ds(qs, BLK), :] += jnp.where(lane == 0, rs_a, jnp.where(lane == 1, rs_b, 0.0))

    m = batch * seq
    sp = _fox_specs(batch, seq)
    return pl.pallas_call(
        body, name="fox_bwd", grid=(batch, 2, nq),
        in_specs=[sp["rows"], sp["rows"], sp["stat"], sp["qblk"], sp["whole"], sp["vwhole"], sp["fcol"], sp["frow"]],
        out_specs=[sp["rows"], sp["whole"], sp["whole"],
                   pl.BlockSpec((1, 1, seq, LANES), lambda b, p, q: (b, p, 0, 0))],
        out_shape=[jax.ShapeDtypeStruct((m, ATT_W), F32), jax.ShapeDtypeStruct((batch, seq, ATT_W), F32),
                   jax.ShapeDtypeStruct((batch, seq, ATT_W), F32),
                   jax.ShapeDtypeStruct((batch, 2, seq, LANES), F32)],
        compiler_params=_params(3),
    )(dy, y, lse, fq, fk, proj16, fcol, frow)


def _fox_post_bwd(dfq, dfk, dfc, proj32, gq, gk, bf, group_mean, batch, seq):
    m = proj32.shape[0]
    ts = _tok_tile(seq)
    nt = seq // ts
    tile = lambda w, col: pl.BlockSpec((ts, w), lambda b, i: (b * nt + (nt - 1 - i), col))

    def body(dfq_ref, dfk_ref, dfc_ref, q_ref, k_ref, f_ref, gq_ref, gk_ref, bf_ref, gm_ref,
             dq_ref, dk_ref, df_ref, gs_ref, bs_ref, carry):
        i = pl.program_id(1)

        @pl.when(i == 0)
        def _():
            carry[...] = jnp.zeros_like(carry)

        gm = gm_ref[...]
        rows = []
        for src, g_ref, d_ref, dst in ((q_ref, gq_ref, dfq_ref, dq_ref), (k_ref, gk_ref, dfk_ref, dk_ref)):
            v, dv = src[...], d_ref[...]
            rstd = lax.rsqrt(_dot_split(v * v, gm) + EPS)
            vhat = v * rstd
            rows.append(_colsum(dv * vhat))
            dvh = dv * g_ref[...]
            dst[...] = (rstd * (dvh - vhat * _dot_split(dvh * vhat, gm))).astype(BF16)
        gs_ref[0] = _rows_to_block(rows, ATT_W)

        dfc_v = dfc_ref[...]
        r = lax.broadcasted_iota(jnp.int32, (ts, ts), 0)
        c = lax.broadcasted_iota(jnp.int32, (ts, ts), 1)
        tri = (r <= c).astype(BF16)
        hi, mid, low = _split3(dfc_v)
        dlf = _dot(tri, hi) + _dot(tri, mid) + _dot(tri, low) + carry[...]
        carry[...] = dlf[0:1, :]
        z = f_ref[...] + bf_ref[...]
        dz = dlf * _sigmoid(-z)
        df_ref[...] = dz.astype(BF16)
        bs_ref[0] = _rows_to_block([_colsum(dz)], LANES)

    full = lambda shape: pl.BlockSpec(shape, lambda b, i: (0,) * len(shape))
    part = lambda w: pl.BlockSpec((1, SUBLANES, w), lambda b, i: (b * nt + (nt - 1 - i), 0, 0))
    return pl.pallas_call(
        body, name="fox_post_bwd", grid=(batch, nt),
        in_specs=[tile(ATT_W, 0), tile(ATT_W, 0), tile(LANES, 0), tile(ATT_W, 7), tile(ATT_W, 8), tile(LANES, COL_FXF),
                  full((1, ATT_W)), full((1, ATT_W)), full((1, LANES)), full((ATT_W, ATT_W))],
        out_specs=[tile(ATT_W, 0), tile(ATT_W, 0), tile(LANES, 0), part(ATT_W), part(LANES)],
        out_shape=[jax.ShapeDtypeStruct((m, ATT_W), BF16), jax.ShapeDtypeStruct((m, ATT_W), BF16),
                   jax.ShapeDtypeStruct((m, LANES), BF16),
                   jax.ShapeDtypeStruct((batch * nt, SUBLANES, ATT_W), F32),
                   jax.ShapeDtypeStruct((batch * nt, SUBLANES, LANES), F32)],
        scratch_shapes=[pltpu.VMEM((1, LANES), F32)],
        compiler_params=_params(2),
    )(dfq, dfk, dfc, proj32, proj32, proj32, gq, gk, bf, group_mean)


_GROUPS = ((0, LRU_W), (LRU_W, LRU_W + ATT_W), (LRU_W + ATT_W, LRU_W + 2 * ATT_W))


def _outnorm(y_lru, y_sb, y_fox, gmix, seq):
    m = y_lru.shape[0]
    tm = _tok_tile(seq)

    def body(a_ref, b_ref, c_ref, g_ref, o_ref):
        parts = []
        for ref in (a_ref, b_ref, c_ref):
            v = ref[...]
            parts.append(v * lax.rsqrt(jnp.mean(v * v, axis=-1, keepdims=True) + EPS))
        o_ref[...] = (jnp.concatenate(parts, axis=1) * g_ref[...]).astype(BF16)

    t = lambda w: pl.BlockSpec((tm, w), lambda i: (i, 0))
    return pl.pallas_call(
        body, name="outnorm", grid=(m // tm,),
        in_specs=[t(LRU_W), t(ATT_W), t(ATT_W), pl.BlockSpec((1, D_MODEL), lambda i: (0, 0))],
        out_specs=t(D_MODEL), out_shape=jax.ShapeDtypeStruct((m, D_MODEL), BF16), compiler_params=_params(1),
    )(y_lru, y_sb, y_fox, gmix)


def _outnorm_bwd_epilogue(p, e_refs, o_refs):
    gmix = e_refs[3][...]
    dg = []
    for n, (lo, hi) in enumerate(_GROUPS):
        v, dyn = e_refs[n][...], p[:, lo:hi]
        rstd = lax.rsqrt(jnp.mean(v * v, axis=-1, keepdims=True) + EPS)
        vhat = v * rstd
        dg.append(_colsum(dyn * vhat))
        dvh = dyn * gmix[:, lo:hi]
        o_refs[n][...] = rstd * (dvh - vhat * jnp.mean(dvh * vhat, axis=-1, keepdims=True))
    o_refs[3][0] = _rows_to_block([jnp.concatenate(dg, axis=1)], p.shape[1])


def _pair_layouts(fcum, batch, seq):
    f4 = fcum[:, :4].reshape(batch, seq, 2, 2)
    return f4.transpose(0, 2, 1, 3), f4.transpose(0, 2, 3, 1)


def _mixer_fwd(x, h, w, gate, batch, seq):
    m, d = x.shape
    tm = _tok_tile(seq)
    tpb = seq // tm

    def in_epilogue(p, e_refs, o_refs):
        o_refs[0][...] = p
        o_refs[1][...] = p.astype(BF16)

    tn_in = 896
    proj32, proj16 = _mm(h, w["w_in"], mode="nn", tm=tm, tn=tn_in, tk=d, b_lead=w["lead"], name="mix_in",
                         outs=[((m, N_IN_PAD), F32, (tm, tn_in), lambda i, j: (i, j)),
                               ((m, N_IN_PAD), BF16, (tm, tn_in), lambda i, j: (i, j))],
                         epilogue=in_epilogue)
    y_lru, h_lru = _lru_fwd(proj32, w["conv_w"], w["conv_b"], w["wr"], w["br"], w["wi"], w["bi"], w["lam"], batch, seq)
    p16 = proj16.reshape(batch, seq, N_IN_PAD)
    y_sb, t_sb = _sb_fwd(p16, batch, seq)
    fq, fk, fcum = _fox_pre(proj32, w["gq"], w["gk"], w["bf"], w["group_mean"], batch, seq)
    fcol, frow = _pair_layouts(fcum, batch, seq)
    fq3, fk3 = fq.reshape(batch, seq, ATT_W), fk.reshape(batch, seq, ATT_W)
    y_fox, lse = _fox_fwd(fq3, fk3, p16, fcol, frow, batch, seq)
    ynorm = _outnorm(y_lru, y_sb, y_fox, w["gmix"], seq)

    def out_epilogue(p, e_refs, o_refs):
        x_ref, g_ref = e_refs
        o_refs[0][...] = x_ref[...] + (1.0 + g_ref[0]) * p
        o_refs[1][...] = p.astype(BF16)

    x_out, out = _mm(ynorm, w["w_out"], mode="nn", tm=tm, tn=d, tk=d, b_lead=w["lead"], name="mix_out",
                     extras=[(x, (tm, d), lambda i, j: (i, 0)), (gate, (1, 1, d), lambda i, j: (i // tpb, 0, 0))],
                     outs=[((m, d), F32, (tm, d), lambda i, j: (i, 0)), ((m, d), BF16, (tm, d), lambda i, j: (i, 0))],
                     epilogue=out_epilogue)
    saved = dict(proj32=proj32, p16=p16, h_lru=h_lru, y_lru=y_lru, y_sb=y_sb, t_sb=t_sb, fq3=fq3, fk3=fk3,
                 fcol=fcol, frow=frow, y_fox=y_fox, lse=lse, ynorm=ynorm, out=out)
    return x_out, saved


def _mixer_bwd(dx_out, x, h, s, w, gn, scale, gate, batch, seq):
    m, d = x.shape
    tm = _tok_tile(seq)
    tpb = seq // tm
    dout, dgate_parts = _residual_bwd(dx_out, s["out"], gate, 1.0, seq, "mix_res_bwd")
    (dw_out,) = _mm(s["ynorm"], dout, mode="tn", tm=d, tn=d, tk=tm, name="mix_dwout",
                    outs=[((d, d), BF16, (d, d), lambda i, j: (i, j))], epilogue=_store_epilogue([BF16]))
    dy_lru, dy_sb, dy_fox, gmix_parts = _mm(
        dout, w["w_out"], mode="nt", tm=tm, tn=d, tk=d, b_lead=w["lead"], name="mix_out_dx",
        extras=[(s["y_lru"], (tm, LRU_W), lambda i, j: (i, 0)), (s["y_sb"], (tm, ATT_W), lambda i, j: (i, 0)),
                (s["y_fox"], (tm, ATT_W), lambda i, j: (i, 0)), (w["gmix"], (1, d), lambda i, j: (0, 0))],
        outs=[((m, LRU_W), F32, (tm, LRU_W), lambda i, j: (i, 0)), ((m, ATT_W), F32, (tm, ATT_W), lambda i, j: (i, 0)),
              ((m, ATT_W), F32, (tm, ATT_W), lambda i, j: (i, 0)),
              ((m // tm, SUBLANES, d), F32, (1, SUBLANES, d), lambda i, j: (i, 0, 0))],
        epilogue=_outnorm_bwd_epilogue)

    dsq, dsk, dsv = _sb_bwd(dy_sb, s["t_sb"], s["p16"], batch, seq)
    dfq, dfk, dfv, dfc = _fox_bwd(dy_fox, s["y_fox"], s["lse"], s["fq3"], s["fk3"], s["p16"], s["fcol"], s["frow"],
                                  batch, seq)
    dfc_cols = dfc[..., :2].transpose(0, 2, 1, 3).reshape(m, 4)
    dfc_cols = jnp.pad(dfc_cols, ((0, 0), (0, LANES - 4)))
    dxq, dxk, dxf, gqk_parts, bf_parts = _fox_post_bwd(dfq, dfk.reshape(m, ATT_W), dfc_cols, s["proj32"],
                                                       w["gq"], w["gk"], w["bf"], w["group_mean"], batch, seq)
    dlx, dlg, dwr, dwi, lru_sums = _lru_bwd(dy_lru, s["proj32"], s["h_lru"], w["conv_w"], w["conv_b"], w["wr"],
                                            w["br"], w["wi"], w["bi"], w["lam"], batch, seq)
    dproj = jnp.concatenate([dlx, dlg, dsq.astype(BF16), dsk.reshape(m, ATT_W).astype(BF16),
                             dsv.reshape(m, ATT_W).astype(BF16), dxq, dxk, dfv.reshape(m, ATT_W).astype(BF16), dxf],
                            axis=1)
    tn_in = 896
    (dw_in,) = _mm(h, dproj, mode="tn", tm=d, tn=tn_in, tk=tm, name="mix_dwin",
                   outs=[((d, N_IN_PAD), BF16, (d, tn_in), lambda i, j: (i, j))], epilogue=_store_epilogue([BF16]))
    dx, nm_parts = _mm(dproj, w["w_in"], mode="nt", tm=tm, tn=d, tk=tn_in, b_lead=w["lead"], name="mix_in_dx",
                       extras=[(x, (tm, d), lambda i, j: (i, 0)), (dx_out, (tm, d), lambda i, j: (i, 0)),
                               (gn, (1, d), lambda i, j: (0, 0)), (scale, (1, 1, d), lambda i, j: (i // tpb, 0, 0))],
                       outs=[((m, d), F32, (tm, d), lambda i, j: (i, 0)),
                             ((m // tm, SUBLANES, d), F32, (1, SUBLANES, d), lambda i, j: (i, 0, 0))],
                       epilogue=_normmod_bwd_epilogue)
    grads = dict(dw_in=dw_in, dw_out=dw_out, dwr=dwr, dwi=dwi, lru_sums=lru_sums, gmix_parts=gmix_parts,
                 gqk_parts=gqk_parts, bf_parts=bf_parts)
    return dx, grads, nm_parts, dgate_parts


def _block_diag(w):
    nb = w.shape[0]
    eye = jnp.eye(nb, dtype=w.dtype)
    return (eye[:, None, :, None] * w[:, :, None, :]).reshape(nb * HEAD_DIM, nb * HEAD_DIM)


def _block_diag_grad(g):
    nb = LRU_W // HEAD_DIM
    g4 = g.reshape(nb, HEAD_DIM, nb, HEAD_DIM)
    return jnp.stack([g4[n, :, n, :] for n in range(nb)])


def _per_batch(parts, batch, row):
    r = parts[:, row, :]
    return r.reshape(batch, -1, r.shape[-1]).sum(axis=1)


def _local_step(x3, target3, mod, wts):
    batch, seq, d = x3.shape
    m = batch * seq
    n_layers = mod.shape[0]
    x = x3.reshape(m, d)
    group_mean = _block_diag(jnp.full((ATT_W // HEAD_DIM, HEAD_DIM, HEAD_DIM), 1.0 / HEAD_DIM, BF16))
    vec = lambda l, j, t: mod[l, :, j, t][:, None, :]

    layers, saved = [], []
    for l in range(n_layers):
        gq = jnp.tile(wts["g_qk"][l, 0], ATT_W // HEAD_DIM)[None, :]
        gk = jnp.tile(wts["g_qk"][l, 1], ATT_W // HEAD_DIM)[None, :]
        bf = jnp.pad(wts["b_fgate"][l], (0, LANES - 4))[None, :]
        lw = dict(lead=(l,), w_in=wts["w_in"], w_out=wts["w_out"], conv_w=wts["conv_w"][l],
                  conv_b=wts["conv_b"][l][None, :], wr=_block_diag(wts["w_rgate"][l]).astype(BF16),
                  br=wts["b_rgate"][l][None, :], wi=_block_diag(wts["w_igate"][l]).astype(BF16),
                  bi=wts["b_igate"][l][None, :], lam=wts["lru_lambda"][l][None, :], gq=gq, gk=gk, bf=bf,
                  group_mean=group_mean, gmix=wts["g_mix_out"][l][None, :])
        layers.append(lw)
        gn = lambda j: wts["g_norm"][l, j][None, :]
        sv = dict(x0=x)
        sv["h0"] = _normmod(x, gn(0), vec(l, 0, 1), vec(l, 0, 0), seq, f"normmod_{l}_0")
        x, sv["ffn0"] = _ffn_fwd(x, sv["h0"], wts["w_up"], wts["w_down"], (l, 0), vec(l, 0, 2), seq, f"{l}_0")
        sv["x1"] = x
        sv["h1"] = _normmod(x, gn(1), vec(l, 1, 1), vec(l, 1, 0), seq, f"normmod_{l}_1")
        x, sv["mix"] = _mixer_fwd(x, sv["h1"], lw, vec(l, 1, 2), batch, seq)
        sv["x2"] = x
        sv["h2"] = _normmod(x, gn(2), vec(l, 2, 1), vec(l, 2, 0), seq, f"normmod_{l}_2")
        x, sv["ffn1"] = _ffn_fwd(x, sv["h2"], wts["w_up"], wts["w_down"], (l, 1), vec(l, 2, 2), seq, f"{l}_1")
        saved.append(sv)

    dx, loss_parts = _loss_head(x, target3.reshape(m, d), seq)
    loss = jnp.sum(loss_parts[:, 0, 0])

    big = dict(w_up=[], w_down=[], w_in=[], w_out=[])
    small = {k: [] for k in ("dmod", "g_norm", "b_fgate", "conv_w", "conv_b", "w_rgate", "b_rgate", "w_igate",
                             "b_igate", "lru_lambda", "g_qk", "g_mix_out")}
    for l in reversed(range(n_layers)):
        sv, lw = saved[l], layers[l]
        gn = lambda j: wts["g_norm"][l, j][None, :]
        dx, dwup1, dwdown1, nm2, dg2 = _ffn_bwd(dx, sv["x2"], sv["h2"], sv["ffn1"], wts["w_up"], wts["w_down"], (l, 1),
                                               gn(2), vec(l, 2, 1), vec(l, 2, 2), seq, f"{l}_1")
        dx, mg, nm1, dg1 = _mixer_bwd(dx, sv["x1"], sv["h1"], sv["mix"], lw, gn(1), vec(l, 1, 1), vec(l, 1, 2),
                                      batch, seq)
        dx, dwup0, dwdown0, nm0, dg0 = _ffn_bwd(dx, sv["x0"], sv["h0"], sv["ffn0"], wts["w_up"], wts["w_down"], (l, 0),
                                               gn(0), vec(l, 0, 1), vec(l, 0, 2), seq, f"{l}_0")
        big["w_up"].insert(0, jnp.stack([dwup0, dwup1]))
        big["w_down"].insert(0, jnp.stack([dwdown0, dwdown1]))
        big["w_in"].insert(0, mg["dw_in"])
        big["w_out"].insert(0, mg["dw_out"])
        dmod_l, gnorm_l = [], []
        for nm, dg in ((nm0, dg0), (nm1, dg1), (nm2, dg2)):
            dmod_l.append(jnp.stack([_per_batch(nm, batch, 0), _per_batch(nm, batch, 1), _per_batch(dg, batch, 0)],
                                    axis=1))
            gnorm_l.append(jnp.sum(nm[:, 2, :], axis=0))
        small["dmod"].insert(0, jnp.stack(dmod_l, axis=1))
        small["g_norm"].insert(0, jnp.stack(gnorm_l))
        ls = mg["lru_sums"]
        small["b_rgate"].insert(0, ls[0])
        small["b_igate"].insert(0, ls[1])
        small["lru_lambda"].insert(0, ls[2] * (-_sigmoid(-wts["lru_lambda"][l])))
        small["conv_b"].insert(0, ls[3])
        small["conv_w"].insert(0, ls[4:8])
        small["w_rgate"].insert(0, _block_diag_grad(mg["dwr"]))
        small["w_igate"].insert(0, _block_diag_grad(mg["dwi"]))
        small["g_mix_out"].insert(0, jnp.sum(mg["gmix_parts"][:, 0, :], axis=0))
        gqk = jnp.sum(mg["gqk_parts"][:, :2, :], axis=0).reshape(2, ATT_W // HEAD_DIM, HEAD_DIM).sum(axis=1)
        small["g_qk"].insert(0, gqk)
        small["b_fgate"].insert(0, jnp.sum(mg["bf_parts"][:, 0, :4], axis=0))
    big = {k: jnp.stack(v) for k, v in big.items()}
    small = {k: jnp.stack(v) for k, v in small.items()}
    return loss, dx.reshape(batch, seq, d), big, small


def _row_tile(rows, row_bytes):
    for t in (512, 256, 128, 64, 32, 16):
        if rows % t == 0 and t * row_bytes <= 4 * 1024 * 1024:
            return t
    return rows


def _adamw(parts, w, m, v, name):
    n_parts, rows, cols = parts.shape
    tr = _row_tile(rows, cols * (n_parts * parts.dtype.itemsize + 7 * 4))
    c1 = 1.0 - ADAM_B1 ** ADAM_STEP
    c2 = 1.0 - ADAM_B2 ** ADAM_STEP

    def body(p_ref, w_ref, m_ref, v_ref, g_out, d_out, m_out, v_out):
        g = p_ref[0].astype(F32)
        for n in range(1, n_parts):
            g = g + p_ref[n].astype(F32)
        m_new = ADAM_B1 * m_ref[...] + (1.0 - ADAM_B1) * g
        v_new = ADAM_B2 * v_ref[...] + (1.0 - ADAM_B2) * (g * g)
        g_out[...] = g
        d_out[...] = -ADAM_LR * ((m_new / c1) / (jnp.sqrt(v_new / c2) + ADAM_EPS) + ADAM_WD * w_ref[...])
        m_out[...] = m_new
        v_out[...] = v_new

    tile = pl.BlockSpec((tr, cols), lambda i: (i, 0))
    return pl.pallas_call(
        body, name=name, grid=(rows // tr,),
        in_specs=[pl.BlockSpec((n_parts, tr, cols), lambda i: (0, i, 0)), tile, tile, tile],
        out_specs=[tile] * 4, out_shape=[jax.ShapeDtypeStruct((rows, cols), F32)] * 4,
        compiler_params=_params(1),
    )(parts, w, m, v)


def _sum_parts(parts):
    n_parts, rows, cols = parts.shape

    def body(p_ref, o_ref):
        acc = p_ref[0]
        for n in range(1, n_parts):
            acc = acc + p_ref[n]
        o_ref[...] = acc

    return pl.pallas_call(body, name="sum_small", out_shape=jax.ShapeDtypeStruct((rows, cols), F32),
                          compiler_params=pltpu.CompilerParams(vmem_limit_bytes=VMEM_LIMIT_BYTES))(parts)


def _flatten(arrays, multiple):
    flat = jnp.concatenate([a.reshape(-1).astype(F32) for a in arrays])
    pad = (-flat.shape[0]) % multiple
    return jnp.pad(flat, (0, pad)).reshape(-1, LANES)


def _unflatten(flat2d, shapes):
    flat, out, off = flat2d.reshape(-1), [], 0
    for s in shapes:
        n = math.prod(s)
        out.append(flat[off:off + n].reshape(s))
        off += n
    return out


SMALL_NAMES = ("b_ada", "g_norm", "b_fgate", "conv_w", "conv_b", "w_rgate", "b_rgate", "w_igate", "b_igate",
               "lru_lambda", "g_qk", "g_mix_out")
WEIGHT_NAMES = ("w_ada", "b_ada", "g_norm", "w_ffn_up", "w_ffn_down", "w_in", "b_fgate", "conv_w", "conv_b",
                "w_rgate", "b_rgate", "w_igate", "b_igate", "lru_lambda", "g_qk", "g_mix_out", "w_out")


def _interleave_up(w):
    lead = w.shape[:-1]
    return w.reshape(lead + (2, 2, FF_HALF)).swapaxes(-3, -2).reshape(lead + (2 * D_FF,))


def kernel(x, c, w_ada, b_ada, g_norm, w_ffn_up, w_ffn_down, w_in, b_fgate, conv_w, conv_b, w_rgate, b_rgate, w_igate, b_igate, lru_lambda, g_qk, g_mix_out, w_out, loss_target, m_w_ada, m_b_ada, m_g_norm, m_w_ffn_up, m_w_ffn_down, m_w_in, m_b_fgate, m_conv_w, m_conv_b, m_w_rgate, m_b_rgate, m_w_igate, m_b_igate, m_lru_lambda, m_g_qk, m_g_mix_out, m_w_out, v_w_ada, v_b_ada, v_g_norm, v_w_ffn_up, v_w_ffn_down, v_w_in, v_b_fgate, v_conv_w, v_conv_b, v_w_rgate, v_b_rgate, v_w_igate, v_b_igate, v_lru_lambda, v_g_qk, v_g_mix_out, v_w_out):
    batch, seq, d = x.shape
    n_layers = w_ada.shape[0]
    me = 4 * lax.axis_index("x") + 2 * lax.axis_index("y") + lax.axis_index("c")
    weights = dict(w_ada=w_ada, b_ada=b_ada, g_norm=g_norm, w_ffn_up=w_ffn_up, w_ffn_down=w_ffn_down, w_in=w_in,
                   b_fgate=b_fgate, conv_w=conv_w, conv_b=conv_b, w_rgate=w_rgate, b_rgate=b_rgate, w_igate=w_igate,
                   b_igate=b_igate, lru_lambda=lru_lambda, g_qk=g_qk, g_mix_out=g_mix_out, w_out=w_out)
    moments_m = dict(w_ada=m_w_ada, b_ada=m_b_ada, g_norm=m_g_norm, w_ffn_up=m_w_ffn_up, w_ffn_down=m_w_ffn_down,
                     w_in=m_w_in, b_fgate=m_b_fgate, conv_w=m_conv_w, conv_b=m_conv_b, w_rgate=m_w_rgate,
                     b_rgate=m_b_rgate, w_igate=m_w_igate, b_igate=m_b_igate, lru_lambda=m_lru_lambda, g_qk=m_g_qk,
                     g_mix_out=m_g_mix_out, w_out=m_w_out)
    moments_v = dict(w_ada=v_w_ada, b_ada=v_b_ada, g_norm=v_g_norm, w_ffn_up=v_w_ffn_up, w_ffn_down=v_w_ffn_down,
                     w_in=v_w_in, b_fgate=v_b_fgate, conv_w=v_conv_w, conv_b=v_conv_b, w_rgate=v_w_rgate,
                     b_rgate=v_b_rgate, w_igate=v_w_igate, b_igate=v_b_igate, lru_lambda=v_lru_lambda, g_qk=v_g_qk,
                     g_mix_out=v_g_mix_out, w_out=v_w_out)

    w_in_pad = jnp.pad(w_in, ((0, 0), (0, 0), (0, N_IN_PAD - N_IN))).astype(BF16)
    c_all, gn_all, cw_all, up_all, down_all, in_all, out_all = _exchange(
        [c, g_norm, conv_w, w_ffn_up.astype(BF16), w_ffn_down.astype(BF16), w_in_pad, w_out.astype(BF16)], [],
        "gather_weights")
    c_all = c_all.reshape(N_DEV * batch, d)
    n_ada = w_ada.shape[-1]
    g_norm_full = gn_all.transpose(1, 2, 0, 3).reshape(n_layers, 3, d)
    conv_w_full = cw_all.transpose(1, 2, 0, 3).reshape(n_layers, 4, LRU_W)
    w_up_full = _interleave_up(up_all.transpose(1, 2, 3, 0, 4).reshape(n_layers, 2, d, 2 * D_FF))
    w_down_full = down_all.transpose(1, 2, 0, 3, 4).reshape(n_layers, 2, D_FF, d)
    w_in_full = in_all.transpose(1, 0, 2, 3).reshape(n_layers, d, N_IN_PAD)
    w_out_full = out_all.transpose(1, 0, 2, 3).reshape(n_layers, d, d)

    b_ada_loc = lax.dynamic_slice_in_dim(b_ada, me * n_ada, n_ada, axis=1)
    silu = lambda t: t * _sigmoid(t)

    def bias_epilogue(p, e_refs, o_refs):
        o_refs[0][...] = p + e_refs[0][...]

    mod_loc = []
    for l in range(n_layers):
        (ml,) = _mm(c_all, w_ada, mode="nn", tm=c_all.shape[0], tn=n_ada, tk=d, b_lead=(l,), a_pre=silu,
                    name=f"ada_{l}", extras=[(b_ada_loc[l][None, :], (1, n_ada), lambda i, j: (0, 0))],
                    outs=[((c_all.shape[0], n_ada), F32, (c_all.shape[0], n_ada), lambda i, j: (0, 0))],
                    epilogue=bias_epilogue)
        mod_loc.append(ml)
    (mod_all,) = _exchange([jnp.stack(mod_loc)], [], "gather_mod")
    mod_all = mod_all.transpose(1, 2, 0, 3).reshape(n_layers, N_DEV * batch, 9 * d)
    mod_me = lax.dynamic_slice_in_dim(mod_all, me * batch, batch, axis=1).reshape(n_layers, batch, 3, 3, d)

    wts = dict(w_up=w_up_full, w_down=w_down_full, w_in=w_in_full, w_out=w_out_full, g_norm=g_norm_full,
               conv_w=conv_w_full, conv_b=conv_b, w_rgate=w_rgate, b_rgate=b_rgate, w_igate=w_igate, b_igate=b_igate,
               lru_lambda=lru_lambda, g_qk=g_qk, g_mix_out=g_mix_out, b_fgate=b_fgate)
    loss_part, grad_x, big, small = _local_step(x, loss_target, mod_me, wts)

    dmod_me = small.pop("dmod").reshape(n_layers, batch, 9 * d)
    small["b_ada"] = jnp.sum(dmod_me, axis=1)
    small_shapes = [(1,)] + [weights[k].shape if k not in ("g_norm", "conv_w") else small[k].shape for k in SMALL_NAMES]
    small_flat = _flatten([loss_part.reshape(1)] + [small[k] for k in SMALL_NAMES], 16 * LANES)
    g_up = _interleave_up(big["w_up"]).reshape(n_layers, 2, d, N_DEV, -1).transpose(3, 0, 1, 2, 4)
    g_down = big["w_down"].reshape(n_layers, 2, N_DEV, -1, d).transpose(2, 0, 1, 3, 4)
    g_in = big["w_in"].reshape(n_layers, N_DEV, -1, N_IN_PAD).transpose(1, 0, 2, 3)
    g_out = big["w_out"].reshape(n_layers, N_DEV, -1, d).transpose(1, 0, 2, 3)
    dmod_all, small_all, p_up, p_down, p_in, p_out = _exchange([dmod_me, small_flat], [g_up, g_down, g_in, g_out],
                                                               "exchange_grads")
    small_sum = _unflatten(_sum_parts(small_all), small_shapes)
    loss = small_sum[0].reshape(())
    small_grads = dict(zip(SMALL_NAMES, small_sum[1:]))
    small_grads["g_norm"] = lax.dynamic_slice_in_dim(small_grads["g_norm"], me * g_norm.shape[-1], g_norm.shape[-1], 2)
    small_grads["conv_w"] = lax.dynamic_slice_in_dim(small_grads["conv_w"], me * conv_w.shape[-1], conv_w.shape[-1], 2)

    dmod_all = dmod_all.transpose(1, 0, 2, 3).reshape(n_layers, N_DEV * batch, 9 * d)
    dmod_loc = lax.dynamic_slice_in_dim(dmod_all, me * n_ada, n_ada, axis=2)
    g_ada = []
    for l in range(n_layers):
        (gl,) = _mm(c_all, dmod_loc[l], mode="tn", tm=d, tn=n_ada, tk=c_all.shape[0], a_pre=silu, name=f"dw_ada_{l}",
                    outs=[((d, n_ada), F32, (d, n_ada), lambda i, j: (0, 0))], epilogue=_store_epilogue([F32]))
        g_ada.append(gl)
    g_ada = jnp.stack(g_ada)

    results = {}

    def update(name, parts):
        shape = weights[name].shape
        cols = shape[-1]
        as2d = lambda t: t.reshape(-1, cols)
        outs = _adamw(parts.reshape(parts.shape[0], -1, parts.shape[-1]), as2d(weights[name]), as2d(moments_m[name]),
                      as2d(moments_v[name]), f"adamw_{name}")
        results[name] = [o.reshape(shape) for o in outs]

    update("w_ada", g_ada[None])
    update("w_ffn_up", p_up)
    update("w_ffn_down", p_down)
    update("w_in", p_in[..., :N_IN])
    update("w_out", p_out)
    sm_shapes = [weights[k].shape for k in SMALL_NAMES]
    flat = lambda src: _flatten([src[k] for k in SMALL_NAMES], 16 * LANES)
    sm_out = _adamw(flat(small_grads)[None], flat(weights), flat(moments_m), flat(moments_v), "adamw_small")
    for k, vals in zip(SMALL_NAMES, zip(*[_unflatten(o, sm_shapes) for o in sm_out])):
        results[k] = list(vals)

    outs = [loss, grad_x]
    for n in range(4):
        outs += [results[k][n] for k in WEIGHT_NAMES]
    return tuple(outs)
```

```python
import functools
import math

import jax
import jax.numpy as jnp
from jax import lax
from jax.experimental import pallas as pl
from jax.experimental.pallas import tpu as pltpu

F32 = jnp.float32
BF16 = jnp.bfloat16

N_DEV = 8
D_MODEL = 1024
D_FF = 2816
FF_HALF = D_FF // 2
HEAD_DIM = 64
LRU_W = 512
ATT_W = 256
N_IN = 2564
N_IN_PAD = 2688
LANES = 128
SUBLANES = 8
BLK = 128
TQ = 256
KB_PER_Q = TQ // BLK
EPS = 1e-6
LRU_C = 8.0
NEG_BIG = -1e30
VMEM_LIMIT_BYTES = 48 * 1024 * 1024

ADAM_LR, ADAM_B1, ADAM_B2, ADAM_EPS, ADAM_WD, ADAM_STEP = 0.001, 0.9, 0.999, 1e-08, 0.01, 10

COL_SBQ, COL_SBK, COL_SBV = 8, 10, 12
COL_FXV, COL_FXF = 18, 20

NN = (((1,), (0,)), ((), ()))
NT = (((1,), (1,)), ((), ()))
TN = (((0,), (0,)), ((), ()))


def _params(n_axes):
    return pltpu.CompilerParams(dimension_semantics=("arbitrary",) * n_axes, vmem_limit_bytes=VMEM_LIMIT_BYTES)


def _tok_tile(seq):
    for t in (512, 256, 128):
        if seq % t == 0:
            return t
    raise ValueError(f"sequence length {seq} is not a multiple of 128")


def _dot(a, b, dims=NN):
    return lax.dot_general(a, b, dims, preferred_element_type=F32)


def _sigmoid(x):
    return 1.0 / (1.0 + jnp.exp(-x))


def _softplus(x):
    return jnp.maximum(x, 0.0) + jnp.log(1.0 + jnp.exp(-jnp.abs(x)))


def _gelu_parts(x):
    k0, k1 = math.sqrt(2.0 / math.pi), 0.044715
    t = jnp.tanh(k0 * (x + k1 * x * x * x))
    gelu = 0.5 * x * (1.0 + t)
    dgelu = 0.5 * (1.0 + t) + 0.5 * x * (1.0 - t * t) * k0 * (1.0 + 3.0 * k1 * x * x)
    return gelu, dgelu


def _neg_expm1(x):
    series = -x * (1.0 + x * (0.5 + x * (1.0 / 6.0 + x * (1.0 / 24.0 + x * (1.0 / 120.0 + x * (1.0 / 720.0))))))
    return jnp.where(x > -0.25, series, 1.0 - jnp.exp(x))


def _split2(x):
    hi = x.astype(BF16)
    lo = (x - hi.astype(F32)).astype(BF16)
    return hi, lo


def _split3(x):
    hi = x.astype(BF16)
    r = x - hi.astype(F32)
    mid = r.astype(BF16)
    lo = (r - mid.astype(F32)).astype(BF16)
    return hi, mid, lo


def _rows_to_block(rows, width):
    r = lax.broadcasted_iota(jnp.int32, (SUBLANES, width), 0)
    out = jnp.zeros((SUBLANES, width), F32)
    for n, v in enumerate(rows):
        out = jnp.where(r == n, jnp.broadcast_to(v, (SUBLANES, width)), out)
    return out


def _colsum(x):
    return jnp.sum(x, axis=0, keepdims=True)


def _exchange(gathers, scatters, name):
    n_g = len(gathers)
    ops = list(gathers) + list(scatters)
    n = len(ops)
    out_shape = [jax.ShapeDtypeStruct((N_DEV,) + a.shape, a.dtype) for a in gathers]
    out_shape += [jax.ShapeDtypeStruct(a.shape, a.dtype) for a in scatters]

    def body(*refs):
        ins, outs = refs[:n], refs[n:2 * n]
        send_sems, recv_sems, local_sems = refs[2 * n:]
        x, y, c = lax.axis_index("x"), lax.axis_index("y"), lax.axis_index("c")
        me = 4 * x + 2 * y + c

        local = []
        for a in range(n):
            src = ins[a] if a < n_g else ins[a].at[me]
            cp = pltpu.make_async_copy(src, outs[a].at[me], local_sems.at[a])
            cp.start()
            local.append(cp)
        remote = []
        for r in range(1, N_DEV):
            px = 1 - x if (r >> 2) & 1 else x
            py = 1 - y if (r >> 1) & 1 else y
            pc = 1 - c if r & 1 else c
            peer = 4 * px + 2 * py + pc
            for a in range(n):
                src = ins[a] if a < n_g else ins[a].at[peer]
                cp = pltpu.make_async_remote_copy(
                    src_ref=src, dst_ref=outs[a].at[me],
                    send_sem=send_sems.at[a, r - 1], recv_sem=recv_sems.at[a, r - 1],
                    device_id=(px, py, pc), device_id_type=pl.DeviceIdType.MESH)
                cp.start()
                remote.append(cp)
        for cp in remote:
            cp.wait()
        for cp in local:
            cp.wait()

    hbm = pl.BlockSpec(memory_space=pltpu.HBM)
    return pl.pallas_call(
        body, name=name, out_shape=out_shape,
        in_specs=[hbm] * n, out_specs=[hbm] * n,
        scratch_shapes=[pltpu.SemaphoreType.DMA((n, N_DEV - 1)), pltpu.SemaphoreType.DMA((n, N_DEV - 1)),
                        pltpu.SemaphoreType.DMA((n,))],
    )(*ops)


def _mm(a, b, *, mode, tm, tn, tk, outs, epilogue, name, extras=(), a_lead=(), b_lead=(), a_pre=None):
    if mode == "tn":
        kdim, mdim = a.shape[-2:]
    else:
        mdim, kdim = a.shape[-2:]
    ndim = b.shape[-2] if mode == "nt" else b.shape[-1]
    assert mdim % tm == 0 and ndim % tn == 0 and kdim % tk == 0, (name, mdim, ndim, kdim, tm, tn, tk)
    ni, nj, nk = mdim // tm, ndim // tn, kdim // tk
    a_lead, b_lead = tuple(a_lead), tuple(b_lead)
    a_block = (None,) * len(a_lead) + ((tk, tm) if mode == "tn" else (tm, tk))
    b_block = (None,) * len(b_lead) + ((tn, tk) if mode == "nt" else (tk, tn))
    dims = {"nn": NN, "nt": NT, "tn": TN}[mode]
    ne, no = len(extras), len(outs)

    def a_index(i, j, k):
        return a_lead + ((k, i) if mode == "tn" else (i, k))

    def b_index(i, j, k):
        return b_lead + ((j, k) if mode == "nt" else (k, j))

    def body(*refs):
        a_ref, b_ref = refs[0], refs[1]
        e_refs, o_refs = refs[2:2 + ne], refs[2 + ne:2 + ne + no]
        av, bv = a_ref[...], b_ref[...]
        if a_pre is not None:
            av = a_pre(av)
        p = _dot(av.astype(BF16), bv.astype(BF16), dims)
        if nk == 1:
            epilogue(p, e_refs, o_refs)
        else:
            acc = refs[-1]
            k = pl.program_id(2)

            @pl.when(k == 0)
            def _():
                acc[...] = p

            @pl.when(k > 0)
            def _():
                acc[...] += p

            @pl.when(k == nk - 1)
            def _():
                epilogue(acc[...], e_refs, o_refs)

    in_specs = [pl.BlockSpec(a_block, a_index), pl.BlockSpec(b_block, b_index)]
    in_specs += [pl.BlockSpec(blk, functools.partial(lambda i, j, k, f: f(i, j), f=f)) for _, blk, f in extras]
    out_specs = [pl.BlockSpec(blk, functools.partial(lambda i, j, k, f: f(i, j), f=f)) for _, _, blk, f in outs]
    res = pl.pallas_call(
        body, name=name, grid=(ni, nj, nk), in_specs=in_specs, out_specs=out_specs,
        out_shape=[jax.ShapeDtypeStruct(s, d) for s, d, _, _ in outs],
        scratch_shapes=[pltpu.VMEM((tm, tn), F32)] if nk > 1 else [],
        compiler_params=_params(3),
    )(a, b, *[e[0] for e in extras])
    return res


def _store_epilogue(dtypes):
    def epi(p, e_refs, o_refs):
        for o, dt in zip(o_refs, dtypes):
            o[...] = p.astype(dt)
    return epi


def _normmod(x, gn, scale, shift, seq, name):
    m, d = x.shape
    tm = _tok_tile(seq)
    tpb = seq // tm

    def body(x_ref, gn_ref, sc_ref, sh_ref, h_ref):
        xv = x_ref[...]
        rstd = lax.rsqrt(jnp.mean(xv * xv, axis=-1, keepdims=True) + EPS)
        h_ref[...] = (xv * rstd * gn_ref[...] * (1.0 + sc_ref[0]) + sh_ref[0]).astype(BF16)

    vec = pl.BlockSpec((1, 1, d), lambda i: (i // tpb, 0, 0))
    return pl.pallas_call(
        body, name=name, grid=(m // tm,),
        in_specs=[pl.BlockSpec((tm, d), lambda i: (i, 0)), pl.BlockSpec((1, d), lambda i: (0, 0)), vec, vec],
        out_specs=pl.BlockSpec((tm, d), lambda i: (i, 0)),
        out_shape=jax.ShapeDtypeStruct((m, d), BF16), compiler_params=_params(1),
    )(x, gn, scale, shift)


def _normmod_bwd_epilogue(p, e_refs, o_refs):
    x_ref, dxo_ref, gn_ref, sc_ref = e_refs
    xv = x_ref[...]
    rstd = lax.rsqrt(jnp.mean(xv * xv, axis=-1, keepdims=True) + EPS)
    xhat = xv * rstd
    gn, sc1 = gn_ref[...], 1.0 + sc_ref[0]
    dxhat = p * (gn * sc1)
    dx = rstd * (dxhat - xhat * jnp.mean(dxhat * xhat, axis=-1, keepdims=True))
    o_refs[0][...] = dxo_ref[...] + dx
    t = p * xhat
    o_refs[1][0] = _rows_to_block([_colsum(p), _colsum(t * gn), _colsum(t * sc1)], p.shape[1])


def _residual_bwd(dx, f, gate, fac, seq, name):
    m, d = dx.shape
    tm = _tok_tile(seq)
    tpb = seq // tm

    def body(dx_ref, f_ref, g_ref, df_ref, dg_ref):
        dxv = dx_ref[...]
        df_ref[...] = ((fac * (1.0 + g_ref[0])) * dxv).astype(BF16)
        dg_ref[0] = _rows_to_block([_colsum((fac * dxv) * f_ref[...].astype(F32))], d)

    tile = pl.BlockSpec((tm, d), lambda i: (i, 0))
    return pl.pallas_call(
        body, name=name, grid=(m // tm,),
        in_specs=[tile, tile, pl.BlockSpec((1, 1, d), lambda i: (i // tpb, 0, 0))],
        out_specs=[tile, pl.BlockSpec((1, SUBLANES, d), lambda i: (i, 0, 0))],
        out_shape=[jax.ShapeDtypeStruct((m, d), BF16), jax.ShapeDtypeStruct((m // tm, SUBLANES, d), F32)],
        compiler_params=_params(1),
    )(dx, f, gate)


def _loss_head(y, target, seq):
    m, d = y.shape
    tm = _tok_tile(seq)

    def body(y_ref, t_ref, dy_ref, l_ref):
        err = y_ref[...] - t_ref[...]
        dy_ref[...] = err * (1.0 / d)
        part = 0.5 * jnp.sum(jnp.mean(err * err, axis=-1, keepdims=True), axis=0, keepdims=True)
        l_ref[0] = jnp.broadcast_to(part, (SUBLANES, LANES))

    tile = pl.BlockSpec((tm, d), lambda i: (i, 0))
    return pl.pallas_call(
        body, name="loss_head", grid=(m // tm,), in_specs=[tile, tile],
        out_specs=[tile, pl.BlockSpec((1, SUBLANES, LANES), lambda i: (i, 0, 0))],
        out_shape=[jax.ShapeDtypeStruct((m, d), F32), jax.ShapeDtypeStruct((m // tm, SUBLANES, LANES), F32)],
        compiler_params=_params(1),
    )(y, target)


def _ffn_fwd(x, h, wup, wdown, lead, gate, seq, tag):
    m, d = x.shape
    tm = _tok_tile(seq)
    tpb = seq // tm

    def up_epilogue(p, e_refs, o_refs):
        g, u = p[:, :FF_HALF], p[:, FF_HALF:]
        o_refs[0][...] = (g * _sigmoid(g) * u).astype(BF16)
        o_refs[1][...] = p.astype(BF16)

    a, gu = _mm(h, wup, mode="nn", tm=tm, tn=2 * FF_HALF, tk=d, b_lead=lead, name=f"ffn_up_{tag}",
                outs=[((m, D_FF), BF16, (tm, FF_HALF), lambda i, j: (i, j)),
                      ((m, 2 * D_FF), BF16, (tm, 2 * FF_HALF), lambda i, j: (i, j))],
                epilogue=up_epilogue)

    def down_epilogue(p, e_refs, o_refs):
        x_ref, g_ref = e_refs
        o_refs[0][...] = x_ref[...] + (0.5 * (1.0 + g_ref[0])) * p
        o_refs[1][...] = p.astype(BF16)

    x_out, f = _mm(a, wdown, mode="nn", tm=tm, tn=d, tk=D_FF, b_lead=lead, name=f"ffn_down_{tag}",
                   extras=[(x, (tm, d), lambda i, j: (i, 0)), (gate, (1, 1, d), lambda i, j: (i // tpb, 0, 0))],
                   outs=[((m, d), F32, (tm, d), lambda i, j: (i, 0)), ((m, d), BF16, (tm, d), lambda i, j: (i, 0))],
                   epilogue=down_epilogue)
    return x_out, (a, gu, f)


def _ffn_bwd(dx_out, x, h, saved, wup, wdown, lead, gn, scale, gate, seq, tag):
    a, gu, f = saved
    m, d = x.shape
    tm = _tok_tile(seq)
    tpb = seq // tm
    df, dgate_parts = _residual_bwd(dx_out, f, gate, 0.5, seq, f"ffn_res_bwd_{tag}")

    def act_bwd_epilogue(p, e_refs, o_refs):
        guv = e_refs[0][...].astype(F32)
        g, u = guv[:, :FF_HALF], guv[:, FF_HALF:]
        sg = _sigmoid(g)
        silu = g * sg
        dsilu = sg * (1.0 + g * (1.0 - sg))
        o_refs[0][...] = jnp.concatenate([p * u * dsilu, p * silu], axis=1).astype(BF16)

    (dgu,) = _mm(df, wdown, mode="nt", tm=tm, tn=FF_HALF, tk=d, b_lead=lead, name=f"ffn_down_dx_{tag}",
                 extras=[(gu, (tm, 2 * FF_HALF), lambda i, j: (i, j))],
                 outs=[((m, 2 * D_FF), BF16, (tm, 2 * FF_HALF), lambda i, j: (i, j))],
                 epilogue=act_bwd_epilogue)
    (dwdown,) = _mm(a, df, mode="tn", tm=FF_HALF, tn=d, tk=tm, name=f"ffn_dwdown_{tag}",
                    outs=[((D_FF, d), BF16, (FF_HALF, d), lambda i, j: (i, j))], epilogue=_store_epilogue([BF16]))
    (dwup,) = _mm(h, dgu, mode="tn", tm=d, tn=FF_HALF, tk=tm, name=f"ffn_dwup_{tag}",
                  outs=[((d, 2 * D_FF), BF16, (d, FF_HALF), lambda i, j: (i, j))], epilogue=_store_epilogue([BF16]))
    dx, nm_parts = _mm(dgu, wup, mode="nt", tm=tm, tn=d, tk=FF_HALF, b_lead=lead, name=f"ffn_up_dx_{tag}",
                       extras=[(x, (tm, d), lambda i, j: (i, 0)), (dx_out, (tm, d), lambda i, j: (i, 0)),
                               (gn, (1, d), lambda i, j: (0, 0)), (scale, (1, 1, d), lambda i, j: (i // tpb, 0, 0))],
                       outs=[((m, d), F32, (tm, d), lambda i, j: (i, 0)),
                             ((m // tm, SUBLANES, d), F32, (1, SUBLANES, d), lambda i, j: (i, 0, 0))],
                       epilogue=_normmod_bwd_epilogue)
    return dx, dwup, dwdown, nm_parts, dgate_parts


def _shift_down(ext, n, rows):
    if n:
        ext = pltpu.roll(ext, n, 0)
    return ext[SUBLANES:SUBLANES + rows]


def _lru_gates(u, wr_ref, br_ref, wi_ref, bi_ref, lam_ref):
    ub = u.astype(BF16)
    r = _sigmoid(_dot(ub, wr_ref[...]) + br_ref[...])
    ig = _sigmoid(_dot(ub, wi_ref[...]) + bi_ref[...])
    sp = _softplus(-lam_ref[...])
    log_a = (-LRU_C * r) * sp
    a = jnp.exp(log_a)
    mult = jnp.sqrt(_neg_expm1(2.0 * log_a))
    return r, ig, sp, a, mult


def _conv(ext, cw_ref, cb_ref, rows):
    u = cb_ref[...] + cw_ref[3:4, :] * _shift_down(ext, 0, rows)
    for k in range(3):
        u = u + cw_ref[k:k + 1, :] * _shift_down(ext, 3 - k, rows)
    return u


def _lru_halo_spec(seq, ts):
    return pl.BlockSpec((SUBLANES, LRU_W),
                        lambda b, i: (jnp.maximum(b * (seq // SUBLANES) + i * (ts // SUBLANES) - 1, 0), 0))


def _lru_fwd(proj32, conv_w, conv_b, wr, br, wi, bi, lam, batch, seq):
    m = proj32.shape[0]
    ts = _tok_tile(seq)
    nt = seq // ts
    row = lambda b, i: (b * nt + i, 0)

    def body(x_ref, halo_ref, g_ref, cw_ref, cb_ref, wr_ref, br_ref, wi_ref, bi_ref, lam_ref,
             y_ref, h_ref, a_scr, b_scr, carry):
        i = pl.program_id(1)
        halo = jnp.where(i > 0, halo_ref[...], 0.0)
        ext = jnp.concatenate([halo, x_ref[...]], axis=0)
        u = _conv(ext, cw_ref, cb_ref, ts)
        _, ig, _, a, mult = _lru_gates(u, wr_ref, br_ref, wi_ref, bi_ref, lam_ref)
        a_scr[...] = a
        b_scr[...] = mult * (ig * u)

        @pl.when(i == 0)
        def _():
            carry[...] = jnp.zeros_like(carry)

        rid = lax.broadcasted_iota(jnp.int32, (SUBLANES, LRU_W), 0)

        def chunk(c, hprev):
            off = pl.multiple_of(c * SUBLANES, SUBLANES)
            av, bv = a_scr[pl.ds(off, SUBLANES), :], b_scr[pl.ds(off, SUBLANES), :]
            for d in (1, 2, 4):
                keep = rid >= d
                bv = jnp.where(keep, av * pltpu.roll(bv, d, 0) + bv, bv)
                av = jnp.where(keep, av * pltpu.roll(av, d, 0), av)
            h = av * hprev + bv
            h_ref[pl.ds(off, SUBLANES), :] = h
            return h[SUBLANES - 1:SUBLANES, :]

        carry[...] = lax.fori_loop(0, ts // SUBLANES, chunk, carry[...])
        gelu, _ = _gelu_parts(g_ref[...])
        y_ref[...] = h_ref[...] * gelu

    full = lambda shape: pl.BlockSpec(shape, lambda b, i: (0,) * len(shape))
    return pl.pallas_call(
        body, name="lru_fwd", grid=(batch, nt),
        in_specs=[pl.BlockSpec((ts, LRU_W), row), _lru_halo_spec(seq, ts),
                  pl.BlockSpec((ts, LRU_W), lambda b, i: (b * nt + i, 1)),
                  full((4, LRU_W)), full((1, LRU_W)), full((LRU_W, LRU_W)), full((1, LRU_W)),
                  full((LRU_W, LRU_W)), full((1, LRU_W)), full((1, LRU_W))],
        out_specs=[pl.BlockSpec((ts, LRU_W), row), pl.BlockSpec((ts, LRU_W), row)],
        out_shape=[jax.ShapeDtypeStruct((m, LRU_W), F32), jax.ShapeDtypeStruct((m, LRU_W), F32)],
        scratch_shapes=[pltpu.VMEM((ts, LRU_W), F32), pltpu.VMEM((ts, LRU_W), F32), pltpu.VMEM((1, LRU_W), F32)],
        compiler_params=_params(2),
    )(proj32, proj32, proj32, conv_w, conv_b, wr, br, wi, bi, lam)


def _lru_bwd(dy, proj32, h, conv_w, conv_b, wr, br, wi, bi, lam, batch, seq):
    m = proj32.shape[0]
    ts = _tok_tile(seq)
    nt = seq // ts
    row = lambda b, i: (b * nt + (nt - 1 - i), 0)
    halo = pl.BlockSpec((SUBLANES, LRU_W),
                        lambda b, i: (jnp.maximum(b * (seq // SUBLANES) + (nt - 1 - i) * (ts // SUBLANES) - 1, 0), 0))

    def body(dy_ref, x_ref, xhalo_ref, g_ref, h_ref, hhalo_ref, cw_ref, cb_ref, wr_ref, br_ref, wi_ref, bi_ref,
             lam_ref, dx_ref, dg_ref, dwr_ref, dwi_ref, sums_ref, a_scr, dh_scr, g_scr, carry, du_next):
        b, i = pl.program_id(0), pl.program_id(1)
        first_tile = i == nt - 1

        @pl.when((b == 0) & (i == 0))
        def _():
            dwr_ref[...] = jnp.zeros_like(dwr_ref)
            dwi_ref[...] = jnp.zeros_like(dwi_ref)
            sums_ref[...] = jnp.zeros_like(sums_ref)

        @pl.when(i == 0)
        def _():
            carry[...] = jnp.zeros_like(carry)
            du_next[...] = jnp.zeros_like(du_next)

        xhalo = jnp.where(first_tile, 0.0, xhalo_ref[...])
        ext = jnp.concatenate([xhalo, x_ref[...]], axis=0)
        u = _conv(ext, cw_ref, cb_ref, ts)
        r, ig, sp, a, mult = _lru_gates(u, wr_ref, br_ref, wi_ref, bi_ref, lam_ref)
        gelu, dgelu = _gelu_parts(g_ref[...])
        dyv, hv = dy_ref[...], h_ref[...]
        dg_ref[...] = (dyv * hv * dgelu).astype(BF16)
        a_scr[...] = a
        dh_scr[...] = dyv * gelu

        rid = lax.broadcasted_iota(jnp.int32, (SUBLANES, LRU_W), 0)
        nchunk = ts // SUBLANES

        def chunk(n, cg):
            off = pl.multiple_of((nchunk - 1 - n) * SUBLANES, SUBLANES)
            av, beta = a_scr[pl.ds(off, SUBLANES), :], dh_scr[pl.ds(off, SUBLANES), :]
            alpha = jnp.where(rid == SUBLANES - 1, 1.0, pltpu.roll(av, SUBLANES - 1, 0))
            for d in (1, 2, 4):
                keep = rid + d <= SUBLANES - 1
                beta = jnp.where(keep, beta + alpha * pltpu.roll(beta, SUBLANES - d, 0), beta)
                alpha = jnp.where(keep, alpha * pltpu.roll(alpha, SUBLANES - d, 0), alpha)
            gv = beta + alpha * cg
            g_scr[pl.ds(off, SUBLANES), :] = gv
            return av[0:1, :] * gv[0:1, :]

        carry[...] = lax.fori_loop(0, nchunk, chunk, carry[...])
        gv = g_scr[...]
        hhalo = jnp.where(first_tile, 0.0, hhalo_ref[...])
        hprev = _shift_down(jnp.concatenate([hhalo, hv], axis=0), 1, ts)
        dmult = gv * ig * u
        dig = gv * mult * u
        du = gv * mult * ig
        dlog_a = gv * hprev * a - dmult * a * a / mult
        dr = dlog_a * (-LRU_C * sp)
        dr_pre = dr * r * (1.0 - r)
        di_pre = dig * ig * (1.0 - ig)
        drb, dib, ub = dr_pre.astype(BF16), di_pre.astype(BF16), u.astype(BF16)
        du = du + _dot(drb, wr_ref[...], NT) + _dot(dib, wi_ref[...], NT)
        dwr_ref[...] += _dot(ub, drb, TN)
        dwi_ref[...] += _dot(ub, dib, TN)

        ext_du = jnp.concatenate([du, du_next[...]], axis=0)
        du_next[...] = du[0:SUBLANES, :]
        n_ext = ts + SUBLANES
        dx = cw_ref[3:4, :] * du
        sums = [_colsum(dr_pre), _colsum(di_pre), _colsum(dlog_a * (-LRU_C * r)), _colsum(du)]
        dcw = []
        for k in range(3):
            dx = dx + cw_ref[k:k + 1, :] * pltpu.roll(ext_du, n_ext - (3 - k), 0)[0:ts]
            dcw.append(_colsum(du * _shift_down(ext, 3 - k, ts)))
        dcw.append(_colsum(du * _shift_down(ext, 0, ts)))
        dx_ref[...] = dx.astype(BF16)
        sums_ref[...] += _rows_to_block(sums + dcw, LRU_W)

    full = lambda shape: pl.BlockSpec(shape, lambda b, i: (0,) * len(shape))
    tile = pl.BlockSpec((ts, LRU_W), row)
    return pl.pallas_call(
        body, name="lru_bwd", grid=(batch, nt),
        in_specs=[tile, tile, halo, pl.BlockSpec((ts, LRU_W), lambda b, i: (b * nt + (nt - 1 - i), 1)), tile, halo,
                  full((4, LRU_W)), full((1, LRU_W)), full((LRU_W, LRU_W)), full((1, LRU_W)),
                  full((LRU_W, LRU_W)), full((1, LRU_W)), full((1, LRU_W))],
        out_specs=[tile, tile, full((LRU_W, LRU_W)), full((LRU_W, LRU_W)), full((SUBLANES, LRU_W))],
        out_shape=[jax.ShapeDtypeStruct((m, LRU_W), BF16), jax.ShapeDtypeStruct((m, LRU_W), BF16),
                   jax.ShapeDtypeStruct((LRU_W, LRU_W), F32), jax.ShapeDtypeStruct((LRU_W, LRU_W), F32),
                   jax.ShapeDtypeStruct((SUBLANES, LRU_W), F32)],
        scratch_shapes=[pltpu.VMEM((ts, LRU_W), F32), pltpu.VMEM((ts, LRU_W), F32), pltpu.VMEM((ts, LRU_W), F32),
                        pltpu.VMEM((1, LRU_W), F32), pltpu.VMEM((SUBLANES, LRU_W), F32)],
        compiler_params=_params(2),
    )(dy, proj32, proj32, proj32, h, h, conv_w, conv_b, wr, br, wi, bi, lam)


def _head_masks():
    lane = lax.broadcasted_iota(jnp.int32, (1, LANES), 1)
    return lane < HEAD_DIM


def _stack_heads(x2):
    lo, zero = _head_masks(), jnp.zeros_like(x2)
    return jnp.concatenate([jnp.where(lo, x2, zero), jnp.where(lo, zero, x2)], axis=0)


def _unstack_heads(y):
    return jnp.where(_head_masks(), y[:TQ], y[TQ:])


def _stack_cols(a, b):
    return jnp.concatenate([a, b], axis=0)


def _causal(qi, kb, strict):
    r = jnp.bitwise_and(lax.broadcasted_iota(jnp.int32, (2 * TQ, BLK), 0), TQ - 1) + qi * TQ
    c = lax.broadcasted_iota(jnp.int32, (2 * TQ, BLK), 1) + kb * BLK
    return (c < r) if strict else (c <= r)


def _key_loop(qi, block, carry, descending=False):
    def trip(n, cr):
        for j in range(KB_PER_Q):
            done = n * KB_PER_Q + j
            cr = block(qi * KB_PER_Q - 1 - done if descending else done, cr, False)
        return cr

    return lax.fori_loop(0, qi, trip, carry)


def _tri(cmp):
    r = lax.broadcasted_iota(jnp.int32, (BLK, BLK), 0)
    c = lax.broadcasted_iota(jnp.int32, (BLK, BLK), 1)
    return cmp(r, c)


def _dot_split(x, tri):
    hi, lo = _split2(x)
    return _dot(hi, tri) + _dot(lo, tri)


def _sb_fwd(proj16, batch, seq):
    nq = seq // TQ
    scale = HEAD_DIM ** -0.5

    def body(q_ref, k_ref, v_ref, y_ref, t_ref):
        qi = pl.program_id(2)
        qs = _stack_heads(q_ref[0])
        tri_after = _tri(lambda r, c: r > c).astype(BF16)

        def block(kb, carry, masked):
            acc, c = carry
            ks = pl.multiple_of(kb * BLK, BLK)
            k2, v2 = k_ref[0, pl.ds(ks, BLK), :], v_ref[0, pl.ds(ks, BLK), :]
            z = _dot(qs, k2, NT) * scale
            sp = _softplus(z)
            l = -sp
            if masked:
                valid = _causal(qi, kb, True)
                l = jnp.where(valid, l, 0.0)
            w = jnp.exp((z - sp) + _dot_split(l, tri_after) + c)
            if masked:
                w = jnp.where(valid, w, 0.0)
            return acc + _dot(w.astype(BF16), v2), c + jnp.sum(l, axis=1, keepdims=True)

        carry = (jnp.zeros((2 * TQ, LANES), F32), jnp.zeros((2 * TQ, 1), F32))
        first = qi * KB_PER_Q
        for n in reversed(range(KB_PER_Q)):
            carry = block(first + n, carry, True)
        acc, c = _key_loop(qi, block, carry, descending=True)
        y_ref[...] = _unstack_heads(acc)
        t_ref[0] = _unstack_heads(jnp.broadcast_to(c, (2 * TQ, LANES)))

    m = batch * seq
    return pl.pallas_call(
        body, name="sb_fwd", grid=(batch, 2, nq),
        in_specs=[pl.BlockSpec((1, TQ, LANES), lambda b, p, q: (b, q, COL_SBQ + p)),
                  pl.BlockSpec((1, seq, LANES), lambda b, p, q: (b, 0, COL_SBK + p)),
                  pl.BlockSpec((1, seq, LANES), lambda b, p, q: (b, 0, COL_SBV + p))],
        out_specs=[pl.BlockSpec((TQ, LANES), lambda b, p, q: (b * nq + q, p)),
                   pl.BlockSpec((1, TQ, LANES), lambda b, p, q: (p, b * nq + q, 0))],
        out_shape=[jax.ShapeDtypeStruct((m, ATT_W), F32), jax.ShapeDtypeStruct((2, m, LANES), F32)],
        compiler_params=_params(3),
    )(proj16, proj16, proj16)


def _sb_bwd(dy, t, proj16, batch, seq):
    nq = seq // TQ
    scale = HEAD_DIM ** -0.5

    def body(dy_ref, t_ref, q_ref, k_ref, v_ref, dq_ref, dk_ref, dv_ref):
        qi = pl.program_id(2)

        @pl.when(qi == 0)
        def _():
            dk_ref[...] = jnp.zeros_like(dk_ref)
            dv_ref[...] = jnp.zeros_like(dv_ref)

        t2 = t_ref[0]
        qs, dys = _stack_heads(q_ref[0]), _stack_heads(dy_ref[...].astype(BF16))
        tot = _stack_cols(t2[:, 0:1], t2[:, HEAD_DIM:HEAD_DIM + 1])
        tri_incl = _tri(lambda r, c: r <= c).astype(BF16)
        tri_excl = _tri(lambda r, c: r < c).astype(BF16)

        def block(kb, carry, masked):
            dq, pc, ec = carry
            ks = pl.multiple_of(kb * BLK, BLK)
            k2, v2 = k_ref[0, pl.ds(ks, BLK), :], v_ref[0, pl.ds(ks, BLK), :]
            z = _dot(qs, k2, NT) * scale
            sp = _softplus(z)
            l, b = -sp, z - sp
            sig = jnp.exp(b)
            if masked:
                valid = _causal(qi, kb, True)
                l = jnp.where(valid, l, 0.0)
            after = tot - (pc + _dot_split(l, tri_incl))
            w = jnp.exp(b + after)
            if masked:
                w = jnp.where(valid, w, 0.0)
            e = _dot(dys, v2, NT) * w
            et = ec + _dot_split(e, tri_excl)
            dz = e * (1.0 - sig) - et * sig
            if masked:
                dz = jnp.where(valid, dz, 0.0)
            dzb = (dz * scale).astype(BF16)
            dk_ref[0, pl.ds(ks, BLK), :] += _dot(dzb, qs, TN)
            dv_ref[0, pl.ds(ks, BLK), :] += _dot(w.astype(BF16), dys, TN)
            return (dq + _dot(dzb, k2), pc + jnp.sum(l, axis=1, keepdims=True),
                    ec + jnp.sum(e, axis=1, keepdims=True))

        col = jnp.zeros((2 * TQ, 1), F32)
        first = qi * KB_PER_Q
        carry = _key_loop(qi, block, (jnp.zeros((2 * TQ, LANES), F32), col, col))
        for n in range(KB_PER_Q):
            carry = block(first + n, carry, True)
        dq_ref[...] = _unstack_heads(carry[0])

    m = batch * seq
    whole = lambda col: pl.BlockSpec((1, seq, LANES), lambda b, p, q: (b, 0, col + p))
    return pl.pallas_call(
        body, name="sb_bwd", grid=(batch, 2, nq),
        in_specs=[pl.BlockSpec((TQ, LANES), lambda b, p, q: (b * nq + q, p)),
                  pl.BlockSpec((1, TQ, LANES), lambda b, p, q: (p, b * nq + q, 0)),
                  pl.BlockSpec((1, TQ, LANES), lambda b, p, q: (b, q, COL_SBQ + p)),
                  whole(COL_SBK), whole(COL_SBV)],
        out_specs=[pl.BlockSpec((TQ, LANES), lambda b, p, q: (b * nq + q, p)), whole(0), whole(0)],
        out_shape=[jax.ShapeDtypeStruct((m, ATT_W), F32), jax.ShapeDtypeStruct((batch, seq, ATT_W), F32),
                   jax.ShapeDtypeStruct((batch, seq, ATT_W), F32)],
        compiler_params=_params(3),
    )(dy, t, proj16, proj16, proj16)


def _fox_pre(proj32, gq, gk, bf, group_mean, batch, seq):
    m = proj32.shape[0]
    ts = _tok_tile(seq)
    nt = seq // ts

    def body(q_ref, k_ref, f_ref, gq_ref, gk_ref, bf_ref, gm_ref, fq_ref, fk_ref, fc_ref, carry):
        i = pl.program_id(1)

        @pl.when(i == 0)
        def _():
            carry[...] = jnp.zeros_like(carry)

        gm = gm_ref[...]
        for src, g_ref, dst in ((q_ref, gq_ref, fq_ref), (k_ref, gk_ref, fk_ref)):
            v = src[...]
            ms = _dot_split(v * v, gm)
            dst[...] = (v * lax.rsqrt(ms + EPS) * g_ref[...]).astype(BF16)
        z = f_ref[...] + bf_ref[...]
        lf = jnp.minimum(z, 0.0) - jnp.log(1.0 + jnp.exp(-jnp.abs(z)))
        r = lax.broadcasted_iota(jnp.int32, (ts, ts), 0)
        c = lax.broadcasted_iota(jnp.int32, (ts, ts), 1)
        tri = (r >= c).astype(BF16)
        hi, mid, low = _split3(lf)
        fc = _dot(tri, hi) + _dot(tri, mid) + _dot(tri, low) + carry[...]
        fc_ref[...] = fc
        carry[...] = fc[ts - 1:ts, :]

    full = lambda shape: pl.BlockSpec(shape, lambda b, i: (0,) * len(shape))
    return pl.pallas_call(
        body, name="fox_pre", grid=(batch, nt),
        in_specs=[pl.BlockSpec((ts, ATT_W), lambda b, i: (b * nt + i, 7)),
                  pl.BlockSpec((ts, ATT_W), lambda b, i: (b * nt + i, 8)),
                  pl.BlockSpec((ts, LANES), lambda b, i: (b * nt + i, COL_FXF)),
                  full((1, ATT_W)), full((1, ATT_W)), full((1, LANES)), full((ATT_W, ATT_W))],
        out_specs=[pl.BlockSpec((ts, ATT_W), lambda b, i: (b * nt + i, 0)),
                   pl.BlockSpec((ts, ATT_W), lambda b, i: (b * nt + i, 0)),
                   pl.BlockSpec((ts, LANES), lambda b, i: (b * nt + i, 0))],
        out_shape=[jax.ShapeDtypeStruct((m, ATT_W), BF16), jax.ShapeDtypeStruct((m, ATT_W), BF16),
                   jax.ShapeDtypeStruct((m, LANES), F32)],
        scratch_shapes=[pltpu.VMEM((1, LANES), F32)],
        compiler_params=_params(2),
    )(proj32, proj32, proj32, gq, gk, bf, group_mean)


def _fox_specs(batch, seq):
    nq = seq // TQ
    return dict(
        qblk=pl.BlockSpec((1, TQ, LANES), lambda b, p, q: (b, q, p)),
        whole=pl.BlockSpec((1, seq, LANES), lambda b, p, q: (b, 0, p)),
        vwhole=pl.BlockSpec((1, seq, LANES), lambda b, p, q: (b, 0, COL_FXV + p)),
        fcol=pl.BlockSpec((1, 1, TQ, 2), lambda b, p, q: (b, p, q, 0)),
        frow=pl.BlockSpec((1, 1, 2, seq), lambda b, p, q: (b, p, 0, 0)),
        rows=pl.BlockSpec((TQ, LANES), lambda b, p, q: (b * nq + q, p)),
        stat=pl.BlockSpec((1, TQ, LANES), lambda b, p, q: (p, b * nq + q, 0)),
    )


def _fox_logits(qs, k2, fq_col, fr_ref, ks, is_a, scale):
    fk_row = jnp.where(is_a, fr_ref[0, 0, 0:1, pl.ds(ks, BLK)], fr_ref[0, 0, 1:2, pl.ds(ks, BLK)])
    return _dot(qs, k2, NT) * scale + fq_col - fk_row


def _fox_fwd(fq, fk, proj16, fcol, frow, batch, seq):
    nq = seq // TQ
    scale = HEAD_DIM ** -0.5

    def body(q_ref, k_ref, v_ref, fc_ref, fr_ref, y_ref, lse_ref):
        qi = pl.program_id(2)
        qs = _stack_heads(q_ref[0])
        fcv = fc_ref[0, 0]
        fq_col = _stack_cols(fcv[:, 0:1], fcv[:, 1:2])
        is_a = lax.broadcasted_iota(jnp.int32, (2 * TQ, 1), 0) < TQ

        def block(kb, carry, masked):
            acc, mx, den = carry
            ks = pl.multiple_of(kb * BLK, BLK)
            k2, v2 = k_ref[0, pl.ds(ks, BLK), :], v_ref[0, pl.ds(ks, BLK), :]
            s = _fox_logits(qs, k2, fq_col, fr_ref, ks, is_a, scale)
            if masked:
                s = jnp.where(_causal(qi, kb, False), s, NEG_BIG)
            mx_new = jnp.maximum(mx, jnp.max(s, axis=1, keepdims=True))
            p = jnp.exp(s - mx_new)
            alpha = jnp.exp(mx - mx_new)
            return (alpha * acc + _dot(p.astype(BF16), v2), mx_new, alpha * den + jnp.sum(p, axis=1, keepdims=True))

        first = qi * KB_PER_Q
        carry = (jnp.zeros((2 * TQ, LANES), F32), jnp.full((2 * TQ, 1), NEG_BIG, F32), jnp.zeros((2 * TQ, 1), F32))
        carry = _key_loop(qi, block, carry)
        for n in range(KB_PER_Q):
            carry = block(first + n, carry, True)
        acc, mx, den = carry
        y_ref[...] = _unstack_heads(acc / den)
        lse_ref[0] = _unstack_heads(jnp.broadcast_to(mx + jnp.log(den), (2 * TQ, LANES)))

    m = batch * seq
    sp = _fox_specs(batch, seq)
    return pl.pallas_call(
        body, name="fox_fwd", grid=(batch, 2, nq),
        in_specs=[sp["qblk"], sp["whole"], sp["vwhole"], sp["fcol"], sp["frow"]],
        out_specs=[sp["rows"], sp["stat"]],
        out_shape=[jax.ShapeDtypeStruct((m, ATT_W), F32), jax.ShapeDtypeStruct((2, m, LANES), F32)],
        compiler_params=_params(3),
    )(fq, fk, proj16, fcol, frow)


def _fox_bwd(dy, y, lse, fq, fk, proj16, fcol, frow, batch, seq):
    nq = seq // TQ
    scale = HEAD_DIM ** -0.5

    def body(dy_ref, y_ref, lse_ref, q_ref, k_ref, v_ref, fc_ref, fr_ref, dq_ref, dk_ref, dv_ref, dfc_ref):
        qi = pl.program_id(2)

        @pl.when(qi == 0)
        def _():
            dk_ref[...] = jnp.zeros_like(dk_ref)
            dv_ref[...] = jnp.zeros_like(dv_ref)
            dfc_ref[...] = jnp.zeros_like(dfc_ref)

        lo = _head_masks()
        lane = lax.broadcasted_iota(jnp.int32, (1, LANES), 1)
        dy2, lse2, fcv = dy_ref[...], lse_ref[0], fc_ref[0, 0]
        qs, dys = _stack_heads(q_ref[0]), _stack_heads(dy2.astype(BF16))
        dyy = dy2 * y_ref[...]
        delta = _stack_cols(jnp.sum(jnp.where(lo, dyy, 0.0), axis=1, keepdims=True),
                            jnp.sum(jnp.where(lo, 0.0, dyy), axis=1, keepdims=True))
        lse_col = _stack_cols(lse2[:, 0:1], lse2[:, HEAD_DIM:HEAD_DIM + 1])
        fq_col = _stack_cols(fcv[:, 0:1], fcv[:, 1:2])
        is_a = lax.broadcasted_iota(jnp.int32, (2 * TQ, 1), 0) < TQ
        pick = jnp.where(is_a, (lane == 0).astype(F32), (lane == 1).astype(F32)).astype(BF16)

        def block(kb, carry, masked):
            dq, rs = carry
            ks = pl.multiple_of(kb * BLK, BLK)
            k2, v2 = k_ref[0, pl.ds(ks, BLK), :], v_ref[0, pl.ds(ks, BLK), :]
            p = jnp.exp(_fox_logits(qs, k2, fq_col, fr_ref, ks, is_a, scale) - lse_col)
            if masked:
                p = jnp.where(_causal(qi, kb, False), p, 0.0)
            ds = p * (_dot(dys, v2, NT) - delta)
            dsb = (ds * scale).astype(BF16)
            dk_ref[0, pl.ds(ks, BLK), :] += _dot(dsb, qs, TN)
            dv_ref[0, pl.ds(ks, BLK), :] += _dot(p.astype(BF16), dys, TN)
            ds_hi, ds_lo = _split2(ds)
            dfc_ref[0, 0, pl.ds(ks, BLK), :] -= _dot(ds_hi, pick, TN) + _dot(ds_lo, pick, TN)
            return dq + _dot(dsb, k2), rs + jnp.sum(ds, axis=1, keepdims=True)

        first = qi * KB_PER_Q
        carry = _key_loop(qi, block, (jnp.zeros((2 * TQ, LANES), F32), jnp.zeros((2 * TQ, 1), F32)))
        for n in range(KB_PER_Q):
            carry = block(first + n, carry, True)
        dq, rs = carry
        dq_ref[...] = _unstack_heads(dq)
        q0 = pl.multiple_of(qi * TQ, TQ)
        dfc_ref[0, 0, pl.ds(q0, TQ), :] += jnp.where(lane == 0, rs[:TQ], jnp.where(lane == 1, rs[TQ:], 0.0))

    m = batch * seq
    sp = _fox_specs(batch, seq)
    return pl.pallas_call(
        body, name="fox_bwd", grid=(batch, 2, nq),
        in_specs=[sp["rows"], sp["rows"], sp["stat"], sp["qblk"], sp["whole"], sp["vwhole"], sp["fcol"], sp["frow"]],
        out_specs=[sp["rows"], sp["whole"], sp["whole"],
                   pl.BlockSpec((1, 1, seq, LANES), lambda b, p, q: (b, p, 0, 0))],
        out_shape=[jax.ShapeDtypeStruct((m, ATT_W), F32), jax.ShapeDtypeStruct((batch, seq, ATT_W), F32),
                   jax.ShapeDtypeStruct((batch, seq, ATT_W), F32),
                   jax.ShapeDtypeStruct((batch, 2, seq, LANES), F32)],
        compiler_params=_params(3),
    )(dy, y, lse, fq, fk, proj16, fcol, frow)


def _fox_post_bwd(dfq, dfk, dfc, proj32, gq, gk, bf, group_mean, batch, seq):
    m = proj32.shape[0]
    ts = _tok_tile(seq)
    nt = seq // ts
    tile = lambda w, col: pl.BlockSpec((ts, w), lambda b, i: (b * nt + (nt - 1 - i), col))

    def body(dfq_ref, dfk_ref, dfc_ref, q_ref, k_ref, f_ref, gq_ref, gk_ref, bf_ref, gm_ref,
             dq_ref, dk_ref, df_ref, gs_ref, bs_ref, carry):
        i = pl.program_id(1)

        @pl.when(i == 0)
        def _():
            carry[...] = jnp.zeros_like(carry)

        gm = gm_ref[...]
        rows = []
        for src, g_ref, d_ref, dst in ((q_ref, gq_ref, dfq_ref, dq_ref), (k_ref, gk_ref, dfk_ref, dk_ref)):
            v, dv = src[...], d_ref[...]
            rstd = lax.rsqrt(_dot_split(v * v, gm) + EPS)
            vhat = v * rstd
            rows.append(_colsum(dv * vhat))
            dvh = dv * g_ref[...]
            dst[...] = (rstd * (dvh - vhat * _dot_split(dvh * vhat, gm))).astype(BF16)
        gs_ref[0] = _rows_to_block(rows, ATT_W)

        dfc_v = dfc_ref[...]
        r = lax.broadcasted_iota(jnp.int32, (ts, ts), 0)
        c = lax.broadcasted_iota(jnp.int32, (ts, ts), 1)
        tri = (r <= c).astype(BF16)
        hi, mid, low = _split3(dfc_v)
        dlf = _dot(tri, hi) + _dot(tri, mid) + _dot(tri, low) + carry[...]
        carry[...] = dlf[0:1, :]
        z = f_ref[...] + bf_ref[...]
        dz = dlf * _sigmoid(-z)
        df_ref[...] = dz.astype(BF16)
        bs_ref[0] = _rows_to_block([_colsum(dz)], LANES)

    full = lambda shape: pl.BlockSpec(shape, lambda b, i: (0,) * len(shape))
    part = lambda w: pl.BlockSpec((1, SUBLANES, w), lambda b, i: (b * nt + (nt - 1 - i), 0, 0))
    return pl.pallas_call(
        body, name="fox_post_bwd", grid=(batch, nt),
        in_specs=[tile(ATT_W, 0), tile(ATT_W, 0), tile(LANES, 0), tile(ATT_W, 7), tile(ATT_W, 8), tile(LANES, COL_FXF),
                  full((1, ATT_W)), full((1, ATT_W)), full((1, LANES)), full((ATT_W, ATT_W))],
        out_specs=[tile(ATT_W, 0), tile(ATT_W, 0), tile(LANES, 0), part(ATT_W), part(LANES)],
        out_shape=[jax.ShapeDtypeStruct((m, ATT_W), BF16), jax.ShapeDtypeStruct((m, ATT_W), BF16),
                   jax.ShapeDtypeStruct((m, LANES), BF16),
                   jax.ShapeDtypeStruct((batch * nt, SUBLANES, ATT_W), F32),
                   jax.ShapeDtypeStruct((batch * nt, SUBLANES, LANES), F32)],
        scratch_shapes=[pltpu.VMEM((1, LANES), F32)],
        compiler_params=_params(2),
    )(dfq, dfk, dfc, proj32, proj32, proj32, gq, gk, bf, group_mean)


_GROUPS = ((0, LRU_W), (LRU_W, LRU_W + ATT_W), (LRU_W + ATT_W, LRU_W + 2 * ATT_W))


def _outnorm(y_lru, y_sb, y_fox, gmix, seq):
    m = y_lru.shape[0]
    tm = _tok_tile(seq)

    def body(a_ref, b_ref, c_ref, g_ref, o_ref):
        parts = []
        for ref in (a_ref, b_ref, c_ref):
            v = ref[...]
            parts.append(v * lax.rsqrt(jnp.mean(v * v, axis=-1, keepdims=True) + EPS))
        o_ref[...] = (jnp.concatenate(parts, axis=1) * g_ref[...]).astype(BF16)

    t = lambda w: pl.BlockSpec((tm, w), lambda i: (i, 0))
    return pl.pallas_call(
        body, name="outnorm", grid=(m // tm,),
        in_specs=[t(LRU_W), t(ATT_W), t(ATT_W), pl.BlockSpec((1, D_MODEL), lambda i: (0, 0))],
        out_specs=t(D_MODEL), out_shape=jax.ShapeDtypeStruct((m, D_MODEL), BF16), compiler_params=_params(1),
    )(y_lru, y_sb, y_fox, gmix)


def _outnorm_bwd_epilogue(p, e_refs, o_refs):
    gmix = e_refs[3][...]
    dg = []
    for n, (lo, hi) in enumerate(_GROUPS):
        v, dyn = e_refs[n][...], p[:, lo:hi]
        rstd = lax.rsqrt(jnp.mean(v * v, axis=-1, keepdims=True) + EPS)
        vhat = v * rstd
        dg.append(_colsum(dyn * vhat))
        dvh = dyn * gmix[:, lo:hi]
        o_refs[n][...] = rstd * (dvh - vhat * jnp.mean(dvh * vhat, axis=-1, keepdims=True))
    o_refs[3][0] = _rows_to_block([jnp.concatenate(dg, axis=1)], p.shape[1])


def _pair_layouts(fcum, batch, seq):
    f4 = fcum[:, :4].reshape(batch, seq, 2, 2)
    return f4.transpose(0, 2, 1, 3), f4.transpose(0, 2, 3, 1)


def _mixer_fwd(x, h, w, gate, batch, seq):
    m, d = x.shape
    tm = _tok_tile(seq)
    tpb = seq // tm

    def in_epilogue(p, e_refs, o_refs):
        o_refs[0][...] = p
        o_refs[1][...] = p.astype(BF16)

    tn_in = 896
    proj32, proj16 = _mm(h, w["w_in"], mode="nn", tm=tm, tn=tn_in, tk=d, b_lead=w["lead"], name="mix_in",
                         outs=[((m, N_IN_PAD), F32, (tm, tn_in), lambda i, j: (i, j)),
                               ((m, N_IN_PAD), BF16, (tm, tn_in), lambda i, j: (i, j))],
                         epilogue=in_epilogue)
    y_lru, h_lru = _lru_fwd(proj32, w["conv_w"], w["conv_b"], w["wr"], w["br"], w["wi"], w["bi"], w["lam"], batch, seq)
    p16 = proj16.reshape(batch, seq, N_IN_PAD)
    y_sb, t_sb = _sb_fwd(p16, batch, seq)
    fq, fk, fcum = _fox_pre(proj32, w["gq"], w["gk"], w["bf"], w["group_mean"], batch, seq)
    fcol, frow = _pair_layouts(fcum, batch, seq)
    fq3, fk3 = fq.reshape(batch, seq, ATT_W), fk.reshape(batch, seq, ATT_W)
    y_fox, lse = _fox_fwd(fq3, fk3, p16, fcol, frow, batch, seq)
    ynorm = _outnorm(y_lru, y_sb, y_fox, w["gmix"], seq)

    def out_epilogue(p, e_refs, o_refs):
        x_ref, g_ref = e_refs
        o_refs[0][...] = x_ref[...] + (1.0 + g_ref[0]) * p
        o_refs[1][...] = p.astype(BF16)

    x_out, out = _mm(ynorm, w["w_out"], mode="nn", tm=tm, tn=d, tk=d, b_lead=w["lead"], name="mix_out",
                     extras=[(x, (tm, d), lambda i, j: (i, 0)), (gate, (1, 1, d), lambda i, j: (i // tpb, 0, 0))],
                     outs=[((m, d), F32, (tm, d), lambda i, j: (i, 0)), ((m, d), BF16, (tm, d), lambda i, j: (i, 0))],
                     epilogue=out_epilogue)
    saved = dict(proj32=proj32, p16=p16, h_lru=h_lru, y_lru=y_lru, y_sb=y_sb, t_sb=t_sb, fq3=fq3, fk3=fk3,
                 fcol=fcol, frow=frow, y_fox=y_fox, lse=lse, ynorm=ynorm, out=out)
    return x_out, saved


def _mixer_bwd(dx_out, x, h, s, w, gn, scale, gate, batch, seq):
    m, d = x.shape
    tm = _tok_tile(seq)
    tpb = seq // tm
    dout, dgate_parts = _residual_bwd(dx_out, s["out"], gate, 1.0, seq, "mix_res_bwd")
    (dw_out,) = _mm(s["ynorm"], dout, mode="tn", tm=d, tn=d, tk=tm, name="mix_dwout",
                    outs=[((d, d), BF16, (d, d), lambda i, j: (i, j))], epilogue=_store_epilogue([BF16]))
    dy_lru, dy_sb, dy_fox, gmix_parts = _mm(
        dout, w["w_out"], mode="nt", tm=tm, tn=d, tk=d, b_lead=w["lead"], name="mix_out_dx",
        extras=[(s["y_lru"], (tm, LRU_W), lambda i, j: (i, 0)), (s["y_sb"], (tm, ATT_W), lambda i, j: (i, 0)),
                (s["y_fox"], (tm, ATT_W), lambda i, j: (i, 0)), (w["gmix"], (1, d), lambda i, j: (0, 0))],
        outs=[((m, LRU_W), F32, (tm, LRU_W), lambda i, j: (i, 0)), ((m, ATT_W), F32, (tm, ATT_W), lambda i, j: (i, 0)),
              ((m, ATT_W), F32, (tm, ATT_W), lambda i, j: (i, 0)),
              ((m // tm, SUBLANES, d), F32, (1, SUBLANES, d), lambda i, j: (i, 0, 0))],
        epilogue=_outnorm_bwd_epilogue)

    dsq, dsk, dsv = _sb_bwd(dy_sb, s["t_sb"], s["p16"], batch, seq)
    dfq, dfk, dfv, dfc = _fox_bwd(dy_fox, s["y_fox"], s["lse"], s["fq3"], s["fk3"], s["p16"], s["fcol"], s["frow"],
                                  batch, seq)
    dfc_cols = dfc[..., :2].transpose(0, 2, 1, 3).reshape(m, 4)
    dfc_cols = jnp.pad(dfc_cols, ((0, 0), (0, LANES - 4)))
    dxq, dxk, dxf, gqk_parts, bf_parts = _fox_post_bwd(dfq, dfk.reshape(m, ATT_W), dfc_cols, s["proj32"],
                                                       w["gq"], w["gk"], w["bf"], w["group_mean"], batch, seq)
    dlx, dlg, dwr, dwi, lru_sums = _lru_bwd(dy_lru, s["proj32"], s["h_lru"], w["conv_w"], w["conv_b"], w["wr"],
                                            w["br"], w["wi"], w["bi"], w["lam"], batch, seq)
    dproj = jnp.concatenate([dlx, dlg, dsq.astype(BF16), dsk.reshape(m, ATT_W).astype(BF16),
                             dsv.reshape(m, ATT_W).astype(BF16), dxq, dxk, dfv.reshape(m, ATT_W).astype(BF16), dxf],
                            axis=1)
    tn_in = 896
    (dw_in,) = _mm(h, dproj, mode="tn", tm=d, tn=tn_in, tk=tm, name="mix_dwin",
                   outs=[((d, N_IN_PAD), BF16, (d, tn_in), lambda i, j: (i, j))], epilogue=_store_epilogue([BF16]))
    dx, nm_parts = _mm(dproj, w["w_in"], mode="nt", tm=tm, tn=d, tk=tn_in, b_lead=w["lead"], name="mix_in_dx",
                       extras=[(x, (tm, d), lambda i, j: (i, 0)), (dx_out, (tm, d), lambda i, j: (i, 0)),
                               (gn, (1, d), lambda i, j: (0, 0)), (scale, (1, 1, d), lambda i, j: (i // tpb, 0, 0))],
                       outs=[((m, d), F32, (tm, d), lambda i, j: (i, 0)),
                             ((m // tm, SUBLANES, d), F32, (1, SUBLANES, d), lambda i, j: (i, 0, 0))],
                       epilogue=_normmod_bwd_epilogue)
    grads = dict(dw_in=dw_in, dw_out=dw_out, dwr=dwr, dwi=dwi, lru_sums=lru_sums, gmix_parts=gmix_parts,
                 gqk_parts=gqk_parts, bf_parts=bf_parts)
    return dx, grads, nm_parts, dgate_parts


def _block_diag(w):
    nb = w.shape[0]
    eye = jnp.eye(nb, dtype=w.dtype)
    return (eye[:, None, :, None] * w[:, :, None, :]).reshape(nb * HEAD_DIM, nb * HEAD_DIM)


def _block_diag_grad(g):
    nb = LRU_W // HEAD_DIM
    g4 = g.reshape(nb, HEAD_DIM, nb, HEAD_DIM)
    return jnp.stack([g4[n, :, n, :] for n in range(nb)])


def _per_batch(parts, batch, row):
    r = parts[:, row, :]
    return r.reshape(batch, -1, r.shape[-1]).sum(axis=1)


def _local_step(x3, target3, mod, wts):
    batch, seq, d = x3.shape
    assert seq % TQ == 0, seq
    m = batch * seq
    n_layers = mod.shape[0]
    x = x3.reshape(m, d)
    group_mean = _block_diag(jnp.full((ATT_W // HEAD_DIM, HEAD_DIM, HEAD_DIM), 1.0 / HEAD_DIM, BF16))
    vec = lambda l, j, t: mod[l, :, j, t][:, None, :]

    layers, saved = [], []
    for l in range(n_layers):
        gq = jnp.tile(wts["g_qk"][l, 0], ATT_W // HEAD_DIM)[None, :]
        gk = jnp.tile(wts["g_qk"][l, 1], ATT_W // HEAD_DIM)[None, :]
        bf = jnp.pad(wts["b_fgate"][l], (0, LANES - 4))[None, :]
        lw = dict(lead=(l,), w_in=wts["w_in"], w_out=wts["w_out"], conv_w=wts["conv_w"][l],
                  conv_b=wts["conv_b"][l][None, :], wr=_block_diag(wts["w_rgate"][l]).astype(BF16),
                  br=wts["b_rgate"][l][None, :], wi=_block_diag(wts["w_igate"][l]).astype(BF16),
                  bi=wts["b_igate"][l][None, :], lam=wts["lru_lambda"][l][None, :], gq=gq, gk=gk, bf=bf,
                  group_mean=group_mean, gmix=wts["g_mix_out"][l][None, :])
        layers.append(lw)
        gn = lambda j: wts["g_norm"][l, j][None, :]
        sv = dict(x0=x)
        sv["h0"] = _normmod(x, gn(0), vec(l, 0, 1), vec(l, 0, 0), seq, f"normmod_{l}_0")
        x, sv["ffn0"] = _ffn_fwd(x, sv["h0"], wts["w_up"], wts["w_down"], (l, 0), vec(l, 0, 2), seq, f"{l}_0")
        sv["x1"] = x
        sv["h1"] = _normmod(x, gn(1), vec(l, 1, 1), vec(l, 1, 0), seq, f"normmod_{l}_1")
        x, sv["mix"] = _mixer_fwd(x, sv["h1"], lw, vec(l, 1, 2), batch, seq)
        sv["x2"] = x
        sv["h2"] = _normmod(x, gn(2), vec(l, 2, 1), vec(l, 2, 0), seq, f"normmod_{l}_2")
        x, sv["ffn1"] = _ffn_fwd(x, sv["h2"], wts["w_up"], wts["w_down"], (l, 1), vec(l, 2, 2), seq, f"{l}_1")
        saved.append(sv)

    dx, loss_parts = _loss_head(x, target3.reshape(m, d), seq)
    loss = jnp.sum(loss_parts[:, 0, 0])

    big = dict(w_up=[], w_down=[], w_in=[], w_out=[])
    small = {k: [] for k in ("dmod", "g_norm", "b_fgate", "conv_w", "conv_b", "w_rgate", "b_rgate", "w_igate",
                             "b_igate", "lru_lambda", "g_qk", "g_mix_out")}
    for l in reversed(range(n_layers)):
        sv, lw = saved[l], layers[l]
        gn = lambda j: wts["g_norm"][l, j][None, :]
        dx, dwup1, dwdown1, nm2, dg2 = _ffn_bwd(dx, sv["x2"], sv["h2"], sv["ffn1"], wts["w_up"], wts["w_down"], (l, 1),
                                               gn(2), vec(l, 2, 1), vec(l, 2, 2), seq, f"{l}_1")
        dx, mg, nm1, dg1 = _mixer_bwd(dx, sv["x1"], sv["h1"], sv["mix"], lw, gn(1), vec(l, 1, 1), vec(l, 1, 2),
                                      batch, seq)
        dx, dwup0, dwdown0, nm0, dg0 = _ffn_bwd(dx, sv["x0"], sv["h0"], sv["ffn0"], wts["w_up"], wts["w_down"], (l, 0),
                                               gn(0), vec(l, 0, 1), vec(l, 0, 2), seq, f"{l}_0")
        big["w_up"].insert(0, jnp.stack([dwup0, dwup1]))
        big["w_down"].insert(0, jnp.stack([dwdown0, dwdown1]))
        big["w_in"].insert(0, mg["dw_in"])
        big["w_out"].insert(0, mg["dw_out"])
        dmod_l, gnorm_l = [], []
        for nm, dg in ((nm0, dg0), (nm1, dg1), (nm2, dg2)):
            dmod_l.append(jnp.stack([_per_batch(nm, batch, 0), _per_batch(nm, batch, 1), _per_batch(dg, batch, 0)],
                                    axis=1))
            gnorm_l.append(jnp.sum(nm[:, 2, :], axis=0))
        small["dmod"].insert(0, jnp.stack(dmod_l, axis=1))
        small["g_norm"].insert(0, jnp.stack(gnorm_l))
        ls = mg["lru_sums"]
        small["b_rgate"].insert(0, ls[0])
        small["b_igate"].insert(0, ls[1])
        small["lru_lambda"].insert(0, ls[2] * (-_sigmoid(-wts["lru_lambda"][l])))
        small["conv_b"].insert(0, ls[3])
        small["conv_w"].insert(0, ls[4:8])
        small["w_rgate"].insert(0, _block_diag_grad(mg["dwr"]))
        small["w_igate"].insert(0, _block_diag_grad(mg["dwi"]))
        small["g_mix_out"].insert(0, jnp.sum(mg["gmix_parts"][:, 0, :], axis=0))
        gqk = jnp.sum(mg["gqk_parts"][:, :2, :], axis=0).reshape(2, ATT_W // HEAD_DIM, HEAD_DIM).sum(axis=1)
        small["g_qk"].insert(0, gqk)
        small["b_fgate"].insert(0, jnp.sum(mg["bf_parts"][:, 0, :4], axis=0))
    big = {k: jnp.stack(v) for k, v in big.items()}
    small = {k: jnp.stack(v) for k, v in small.items()}
    return loss, dx.reshape(batch, seq, d), big, small


def _row_tile(rows, row_bytes):
    for t in (512, 256, 128, 64, 32, 16):
        if rows % t == 0 and t * row_bytes <= 4 * 1024 * 1024:
            return t
    return rows


def _adamw(parts, w, m, v, name):
    n_parts, rows, cols = parts.shape
    tr = _row_tile(rows, cols * (n_parts * parts.dtype.itemsize + 7 * 4))
    c1 = 1.0 - ADAM_B1 ** ADAM_STEP
    c2 = 1.0 - ADAM_B2 ** ADAM_STEP

    def body(p_ref, w_ref, m_ref, v_ref, g_out, d_out, m_out, v_out):
        g = p_ref[0].astype(F32)
        for n in range(1, n_parts):
            g = g + p_ref[n].astype(F32)
        m_new = ADAM_B1 * m_ref[...] + (1.0 - ADAM_B1) * g
        v_new = ADAM_B2 * v_ref[...] + (1.0 - ADAM_B2) * (g * g)
        g_out[...] = g
        d_out[...] = -ADAM_LR * ((m_new / c1) / (jnp.sqrt(v_new / c2) + ADAM_EPS) + ADAM_WD * w_ref[...])
        m_out[...] = m_new
        v_out[...] = v_new

    tile = pl.BlockSpec((tr, cols), lambda i: (i, 0))
    return pl.pallas_call(
        body, name=name, grid=(rows // tr,),
        in_specs=[pl.BlockSpec((n_parts, tr, cols), lambda i: (0, i, 0)), tile, tile, tile],
        out_specs=[tile] * 4, out_shape=[jax.ShapeDtypeStruct((rows, cols), F32)] * 4,
        compiler_params=_params(1),
    )(parts, w, m, v)


def _sum_parts(parts):
    n_parts, rows, cols = parts.shape

    def body(p_ref, o_ref):
        acc = p_ref[0]
        for n in range(1, n_parts):
            acc = acc + p_ref[n]
        o_ref[...] = acc

    return pl.pallas_call(body, name="sum_small", out_shape=jax.ShapeDtypeStruct((rows, cols), F32),
                          compiler_params=pltpu.CompilerParams(vmem_limit_bytes=VMEM_LIMIT_BYTES))(parts)


def _flatten(arrays, multiple):
    flat = jnp.concatenate([a.reshape(-1).astype(F32) for a in arrays])
    pad = (-flat.shape[0]) % multiple
    return jnp.pad(flat, (0, pad)).reshape(-1, LANES)


def _unflatten(flat2d, shapes):
    flat, out, off = flat2d.reshape(-1), [], 0
    for s in shapes:
        n = math.prod(s)
        out.append(flat[off:off + n].reshape(s))
        off += n
    return out


SMALL_NAMES = ("b_ada", "g_norm", "b_fgate", "conv_w", "conv_b", "w_rgate", "b_rgate", "w_igate", "b_igate",
               "lru_lambda", "g_qk", "g_mix_out")
WEIGHT_NAMES = ("w_ada", "b_ada", "g_norm", "w_ffn_up", "w_ffn_down", "w_in", "b_fgate", "conv_w", "conv_b",
                "w_rgate", "b_rgate", "w_igate", "b_igate", "lru_lambda", "g_qk", "g_mix_out", "w_out")


def _interleave_up(w):
    lead = w.shape[:-1]
    return w.reshape(lead + (2, 2, FF_HALF)).swapaxes(-3, -2).reshape(lead + (2 * D_FF,))


def kernel(x, c, w_ada, b_ada, g_norm, w_ffn_up, w_ffn_down, w_in, b_fgate, conv_w, conv_b, w_rgate, b_rgate, w_igate, b_igate, lru_lambda, g_qk, g_mix_out, w_out, loss_target, m_w_ada, m_b_ada, m_g_norm, m_w_ffn_up, m_w_ffn_down, m_w_in, m_b_fgate, m_conv_w, m_conv_b, m_w_rgate, m_b_rgate, m_w_igate, m_b_igate, m_lru_lambda, m_g_qk, m_g_mix_out, m_w_out, v_w_ada, v_b_ada, v_g_norm, v_w_ffn_up, v_w_ffn_down, v_w_in, v_b_fgate, v_conv_w, v_conv_b, v_w_rgate, v_b_rgate, v_w_igate, v_b_igate, v_lru_lambda, v_g_qk, v_g_mix_out, v_w_out):
    batch, seq, d = x.shape
    n_layers = w_ada.shape[0]
    me = 4 * lax.axis_index("x") + 2 * lax.axis_index("y") + lax.axis_index("c")
    weights = dict(w_ada=w_ada, b_ada=b_ada, g_norm=g_norm, w_ffn_up=w_ffn_up, w_ffn_down=w_ffn_down, w_in=w_in,
                   b_fgate=b_fgate, conv_w=conv_w, conv_b=conv_b, w_rgate=w_rgate, b_rgate=b_rgate, w_igate=w_igate,
                   b_igate=b_igate, lru_lambda=lru_lambda, g_qk=g_qk, g_mix_out=g_mix_out, w_out=w_out)
    moments_m = dict(w_ada=m_w_ada, b_ada=m_b_ada, g_norm=m_g_norm, w_ffn_up=m_w_ffn_up, w_ffn_down=m_w_ffn_down,
                     w_in=m_w_in, b_fgate=m_b_fgate, conv_w=m_conv_w, conv_b=m_conv_b, w_rgate=m_w_rgate,
                     b_rgate=m_b_rgate, w_igate=m_w_igate, b_igate=m_b_igate, lru_lambda=m_lru_lambda, g_qk=m_g_qk,
                     g_mix_out=m_g_mix_out, w_out=m_w_out)
    moments_v = dict(w_ada=v_w_ada, b_ada=v_b_ada, g_norm=v_g_norm, w_ffn_up=v_w_ffn_up, w_ffn_down=v_w_ffn_down,
                     w_in=v_w_in, b_fgate=v_b_fgate, conv_w=v_conv_w, conv_b=v_conv_b, w_rgate=v_w_rgate,
                     b_rgate=v_b_rgate, w_igate=v_w_igate, b_igate=v_b_igate, lru_lambda=v_lru_lambda, g_qk=v_g_qk,
                     g_mix_out=v_g_mix_out, w_out=v_w_out)

    w_in_pad = jnp.pad(w_in, ((0, 0), (0, 0), (0, N_IN_PAD - N_IN))).astype(BF16)
    c_all, gn_all, cw_all, up_all, down_all, in_all, out_all = _exchange(
        [c, g_norm, conv_w, w_ffn_up.astype(BF16), w_ffn_down.astype(BF16), w_in_pad, w_out.astype(BF16)], [],
        "gather_weights")
    c_all = c_all.reshape(N_DEV * batch, d)
    n_ada = w_ada.shape[-1]
    g_norm_full = gn_all.transpose(1, 2, 0, 3).reshape(n_layers, 3, d)
    conv_w_full = cw_all.transpose(1, 2, 0, 3).reshape(n_layers, 4, LRU_W)
    w_up_full = _interleave_up(up_all.transpose(1, 2, 3, 0, 4).reshape(n_layers, 2, d, 2 * D_FF))
    w_down_full = down_all.transpose(1, 2, 0, 3, 4).reshape(n_layers, 2, D_FF, d)
    w_in_full = in_all.transpose(1, 0, 2, 3).reshape(n_layers, d, N_IN_PAD)
    w_out_full = out_all.transpose(1, 0, 2, 3).reshape(n_layers, d, d)

    b_ada_loc = lax.dynamic_slice_in_dim(b_ada, me * n_ada, n_ada, axis=1)
    silu = lambda t: t * _sigmoid(t)

    def bias_epilogue(p, e_refs, o_refs):
        o_refs[0][...] = p + e_refs[0][...]

    mod_loc = []
    for l in range(n_layers):
        (ml,) = _mm(c_all, w_ada, mode="nn", tm=c_all.shape[0], tn=n_ada, tk=d, b_lead=(l,), a_pre=silu,
                    name=f"ada_{l}", extras=[(b_ada_loc[l][None, :], (1, n_ada), lambda i, j: (0, 0))],
                    outs=[((c_all.shape[0], n_ada), F32, (c_all.shape[0], n_ada), lambda i, j: (0, 0))],
                    epilogue=bias_epilogue)
        mod_loc.append(ml)
    (mod_all,) = _exchange([jnp.stack(mod_loc)], [], "gather_mod")
    mod_all = mod_all.transpose(1, 2, 0, 3).reshape(n_layers, N_DEV * batch, 9 * d)
    mod_me = lax.dynamic_slice_in_dim(mod_all, me * batch, batch, axis=1).reshape(n_layers, batch, 3, 3, d)

    wts = dict(w_up=w_up_full, w_down=w_down_full, w_in=w_in_full, w_out=w_out_full, g_norm=g_norm_full,
               conv_w=conv_w_full, conv_b=conv_b, w_rgate=w_rgate, b_rgate=b_rgate, w_igate=w_igate, b_igate=b_igate,
               lru_lambda=lru_lambda, g_qk=g_qk, g_mix_out=g_mix_out, b_fgate=b_fgate)
    loss_part, grad_x, big, small = _local_step(x, loss_target, mod_me, wts)

    dmod_me = small.pop("dmod").reshape(n_layers, batch, 9 * d)
    small["b_ada"] = jnp.sum(dmod_me, axis=1)
    small_shapes = [(1,)] + [weights[k].shape if k not in ("g_norm", "conv_w") else small[k].shape for k in SMALL_NAMES]
    small_flat = _flatten([loss_part.reshape(1)] + [small[k] for k in SMALL_NAMES], 16 * LANES)
    g_up = _interleave_up(big["w_up"]).reshape(n_layers, 2, d, N_DEV, -1).transpose(3, 0, 1, 2, 4)
    g_down = big["w_down"].reshape(n_layers, 2, N_DEV, -1, d).transpose(2, 0, 1, 3, 4)
    g_in = big["w_in"].reshape(n_layers, N_DEV, -1, N_IN_PAD).transpose(1, 0, 2, 3)
    g_out = big["w_out"].reshape(n_layers, N_DEV, -1, d).transpose(1, 0, 2, 3)
    dmod_all, small_all, p_up, p_down, p_in, p_out = _exchange([dmod_me, small_flat], [g_up, g_down, g_in, g_out],
                                                               "exchange_grads")
    small_sum = _unflatten(_sum_parts(small_all), small_shapes)
    loss = small_sum[0].reshape(())
    small_grads = dict(zip(SMALL_NAMES, small_sum[1:]))
    small_grads["g_norm"] = lax.dynamic_slice_in_dim(small_grads["g_norm"], me * g_norm.shape[-1], g_norm.shape[-1], 2)
    small_grads["conv_w"] = lax.dynamic_slice_in_dim(small_grads["conv_w"], me * conv_w.shape[-1], conv_w.shape[-1], 2)

    dmod_all = dmod_all.transpose(1, 0, 2, 3).reshape(n_layers, N_DEV * batch, 9 * d)
    dmod_loc = lax.dynamic_slice_in_dim(dmod_all, me * n_ada, n_ada, axis=2)
    g_ada = []
    for l in range(n_layers):
        (gl,) = _mm(c_all, dmod_loc[l], mode="tn", tm=d, tn=n_ada, tk=c_all.shape[0], a_pre=silu, name=f"dw_ada_{l}",
                    outs=[((d, n_ada), F32, (d, n_ada), lambda i, j: (0, 0))], epilogue=_store_epilogue([F32]))
        g_ada.append(gl)
    g_ada = jnp.stack(g_ada)

    results = {}

    def update(name, parts):
        shape = weights[name].shape
        cols = shape[-1]
        as2d = lambda t: t.reshape(-1, cols)
        outs = _adamw(parts.reshape(parts.shape[0], -1, parts.shape[-1]), as2d(weights[name]), as2d(moments_m[name]),
                      as2d(moments_v[name]), f"adamw_{name}")
        results[name] = [o.reshape(shape) for o in outs]

    update("w_ada", g_ada[None])
    update("w_ffn_up", p_up)
    update("w_ffn_down", p_down)
    update("w_in", p_in[..., :N_IN])
    update("w_out", p_out)
    sm_shapes = [weights[k].shape for k in SMALL_NAMES]
    flat = lambda src: _flatten([src[k] for k in SMALL_NAMES], 16 * LANES)
    sm_out = _adamw(flat(small_grads)[None], flat(weights), flat(moments_m), flat(moments_v), "adamw_small")
    for k, vals in zip(SMALL_NAMES, zip(*[_unflatten(o, sm_shapes) for o in sm_out])):
        results[k] = list(vals)

    outs = [loss, grad_x]
    for n in range(4):
        outs += [results[k][n] for k in WEIGHT_NAMES]
    return tuple(outs)
```

```python
import functools
import math

import jax
import jax.numpy as jnp
from jax import lax
from jax.experimental import pallas as pl
from jax.experimental.pallas import tpu as pltpu

F32 = jnp.float32
BF16 = jnp.bfloat16

N_DEV = 8
D_MODEL = 1024
D_FF = 2816
FF_SHARD = 2 * D_FF // N_DEV
N_FF_SHARD = D_FF // FF_SHARD
HEAD_DIM = 64
LRU_W = 512
ATT_W = 256
N_IN = 2564
N_IN_PAD = 2688
LANES = 128
SUBLANES = 8
BLK = 128
TQ = 256
KB_PER_Q = TQ // BLK
EPS = 1e-6
LRU_C = 8.0
NEG_BIG = -1e30
VMEM_LIMIT_BYTES = 48 * 1024 * 1024

ADAM_LR, ADAM_B1, ADAM_B2, ADAM_EPS, ADAM_WD, ADAM_STEP = 0.001, 0.9, 0.999, 1e-08, 0.01, 10

COL_SBQ, COL_SBK, COL_SBV = 8, 10, 12
COL_FXV, COL_FXF = 18, 20

NN = (((1,), (0,)), ((), ()))
NT = (((1,), (1,)), ((), ()))
TN = (((0,), (0,)), ((), ()))


def _params(n_axes):
    return pltpu.CompilerParams(dimension_semantics=("arbitrary",) * n_axes, vmem_limit_bytes=VMEM_LIMIT_BYTES)


def _tok_tile(seq):
    for t in (512, 256, 128):
        if seq % t == 0:
            return t
    raise ValueError(f"sequence length {seq} is not a multiple of 128")


def _dot(a, b, dims=NN):
    return lax.dot_general(a, b, dims, preferred_element_type=F32)


def _sigmoid(x):
    return 1.0 / (1.0 + jnp.exp(-x))


def _softplus(x):
    return jnp.maximum(x, 0.0) + jnp.log(1.0 + jnp.exp(-jnp.abs(x)))


def _gelu_parts(x):
    k0, k1 = math.sqrt(2.0 / math.pi), 0.044715
    t = jnp.tanh(k0 * (x + k1 * x * x * x))
    gelu = 0.5 * x * (1.0 + t)
    dgelu = 0.5 * (1.0 + t) + 0.5 * x * (1.0 - t * t) * k0 * (1.0 + 3.0 * k1 * x * x)
    return gelu, dgelu


def _neg_expm1(x):
    series = -x * (1.0 + x * (0.5 + x * (1.0 / 6.0 + x * (1.0 / 24.0 + x * (1.0 / 120.0 + x * (1.0 / 720.0))))))
    return jnp.where(x > -0.25, series, 1.0 - jnp.exp(x))


def _split2(x):
    hi = x.astype(BF16)
    lo = (x - hi.astype(F32)).astype(BF16)
    return hi, lo


def _split3(x):
    hi = x.astype(BF16)
    r = x - hi.astype(F32)
    mid = r.astype(BF16)
    lo = (r - mid.astype(F32)).astype(BF16)
    return hi, mid, lo


def _rows_to_block(rows, width):
    r = lax.broadcasted_iota(jnp.int32, (SUBLANES, width), 0)
    out = jnp.zeros((SUBLANES, width), F32)
    for n, v in enumerate(rows):
        out = jnp.where(r == n, jnp.broadcast_to(v, (SUBLANES, width)), out)
    return out


def _colsum(x):
    return jnp.sum(x, axis=0, keepdims=True)


def _exchange(gathers, scatters, name, two_level=False):
    assert not (two_level and scatters)
    n_g = len(gathers)
    ops = [a for a, _ in gathers] + [a for a, _ in scatters]
    n = len(ops)
    out_shape = [jax.ShapeDtypeStruct(a.shape[:nl] + (N_DEV,) + a.shape[nl:], a.dtype) for a, nl in gathers]
    out_shape += [jax.ShapeDtypeStruct(a.shape, a.dtype) for a, _ in scatters]
    items = []
    for k, (a, nl) in enumerate(list(gathers) + list(scatters)):
        for flat in range(math.prod(a.shape[:nl])):
            idx, rem = [], flat
            for dim in reversed(a.shape[:nl]):
                idx.insert(0, rem % dim)
                rem //= dim
            items.append((k, tuple(idx)))
    n_items = len(items)

    def body(*refs):
        ins, outs = refs[:n], refs[n:2 * n]
        send_sems, recv_sems, local_sems = refs[2 * n:]
        x, y, c = lax.axis_index("x"), lax.axis_index("y"), lax.axis_index("c")
        me = 4 * x + 2 * y + c

        def at(ref, idx):
            return ref.at[idx] if idx else ref

        def src(it, peer):
            k, idx = items[it]
            return at(ins[k], idx) if k < n_g else at(ins[k], idx + (peer,))

        def slot(it, s):
            k, idx = items[it]
            return at(outs[k], idx + (s,))

        def remote(it, rel, source, s, to):
            return pltpu.make_async_remote_copy(
                src_ref=source, dst_ref=slot(it, s), send_sem=send_sems.at[it, rel], recv_sem=recv_sems.at[it, rel],
                device_id=to, device_id_type=pl.DeviceIdType.MESH)

        local = [pltpu.make_async_copy(src(it, me), slot(it, me), local_sems.at[it]) for it in range(n_items)]
        for cp in local:
            cp.start()

        if not two_level:
            started = []
            for r in range(1, N_DEV):
                px = 1 - x if (r >> 2) & 1 else x
                py = 1 - y if (r >> 1) & 1 else y
                pc = 1 - c if r & 1 else c
                for it in range(n_items):
                    cp = remote(it, r - 1, src(it, 4 * px + 2 * py + pc), me, (px, py, pc))
                    cp.start()
                    started.append(cp)
            for cp in started:
                cp.wait()
        else:
            sibling, chips = (x, y, 1 - c), [(1 - x, y), (x, 1 - y), (1 - x, 1 - y)]
            sib = 4 * x + 2 * y + (1 - c)
            started = []
            for it in range(n_items):
                started.append(remote(it, 0, src(it, me), me, sibling))
                started += [remote(it, 1 + j, src(it, me), me, (cx, cy, c)) for j, (cx, cy) in enumerate(chips)]
            for cp in started:
                cp.start()
            for j, (cx, cy) in enumerate(chips):
                s = 4 * cx + 2 * cy + c
                for it in range(n_items):
                    remote(it, 1 + j, slot(it, s), s, sibling).wait_recv()
                    cp = remote(it, 4 + j, slot(it, s), s, sibling)
                    cp.start()
                    started.append(cp)
            for it in range(n_items):
                remote(it, 0, slot(it, sib), sib, sibling).wait_recv()
                for j, (cx, cy) in enumerate(chips):
                    s = 4 * cx + 2 * cy + (1 - c)
                    remote(it, 4 + j, slot(it, s), s, sibling).wait_recv()
            for cp in started:
                cp.wait_send()
        for cp in local:
            cp.wait()

    hbm = pl.BlockSpec(memory_space=pltpu.HBM)
    return pl.pallas_call(
        body, name=name, out_shape=out_shape,
        in_specs=[hbm] * n, out_specs=[hbm] * n,
        scratch_shapes=[pltpu.SemaphoreType.DMA((n_items, N_DEV - 1)), pltpu.SemaphoreType.DMA((n_items, N_DEV - 1)),
                        pltpu.SemaphoreType.DMA((n_items,))],
    )(*ops)


def _mm(a, b, *, mode, tm, tn, tk, outs, epilogue, name, extras=(), a_lead=(), b_lead=(), a_pre=None,
        a_spec=None, b_spec=None, shape=None, ksub=1):
    if shape is not None:
        mdim, ndim, kdim = shape
    else:
        if mode == "tn":
            kdim, mdim = a.shape[-2:]
        else:
            mdim, kdim = a.shape[-2:]
        ndim = b.shape[-2] if mode == "nt" else b.shape[-1]
    assert mdim % tm == 0 and ndim % tn == 0 and kdim % tk == 0, (name, mdim, ndim, kdim, tm, tn, tk)
    ni, nj, nk = mdim // tm, ndim // tn, kdim // tk
    a_lead, b_lead = tuple(a_lead), tuple(b_lead)
    a_block = (None,) * len(a_lead) + ((tk, tm) if mode == "tn" else (tm, tk))
    b_block = (None,) * len(b_lead) + ((tn, tk) if mode == "nt" else (tk, tn))
    dims = {"nn": NN, "nt": NT, "tn": TN}[mode]
    ne, no = len(extras), len(outs)

    def a_index(i, j, k):
        return a_lead + ((k, i) if mode == "tn" else (i, k))

    def b_index(i, j, k):
        return b_lead + ((j, k) if mode == "nt" else (k, j))

    if a_spec is not None:
        a_block, a_index = a_spec
    if b_spec is not None:
        b_block, b_index = b_spec

    def body(*refs):
        a_ref, b_ref = refs[0], refs[1]
        e_refs, o_refs = refs[2:2 + ne], refs[2 + ne:2 + ne + no]
        if ksub == 1:
            av = a_ref[...] if a_pre is None else a_pre(a_ref[...])
            p = _dot(av.astype(BF16), b_ref[...].astype(BF16), dims)
        else:
            p = _dot(a_ref[0], b_ref[0], dims)
            for s in range(1, ksub):
                p = p + _dot(a_ref[s], b_ref[s], dims)
        if nk == 1:
            epilogue(p, e_refs, o_refs)
        else:
            acc = refs[-1]
            k = pl.program_id(2)

            @pl.when(k == 0)
            def _():
                acc[...] = p

            @pl.when(k > 0)
            def _():
                acc[...] += p

            @pl.when(k == nk - 1)
            def _():
                epilogue(acc[...], e_refs, o_refs)

    in_specs = [pl.BlockSpec(a_block, a_index), pl.BlockSpec(b_block, b_index)]
    in_specs += [pl.BlockSpec(blk, functools.partial(lambda i, j, k, f: f(i, j), f=f)) for _, blk, f in extras]
    out_specs = [pl.BlockSpec(blk, functools.partial(lambda i, j, k, f: f(i, j), f=f)) for _, _, blk, f in outs]
    res = pl.pallas_call(
        body, name=name, grid=(ni, nj, nk), in_specs=in_specs, out_specs=out_specs,
        out_shape=[jax.ShapeDtypeStruct(s, d) for s, d, _, _ in outs],
        scratch_shapes=[pltpu.VMEM((tm, tn), F32)] if nk > 1 else [],
        compiler_params=_params(3),
    )(a, b, *[e[0] for e in extras])
    return res


def _store_epilogue(dtypes):
    def epi(p, e_refs, o_refs):
        for o, dt in zip(o_refs, dtypes):
            o[...] = p.astype(dt)
    return epi


def _normmod(x, gn, scale, shift, seq, name):
    m, d = x.shape
    tm = _tok_tile(seq)
    tpb = seq // tm

    def body(x_ref, gn_ref, sc_ref, sh_ref, h_ref):
        xv = x_ref[...]
        rstd = lax.rsqrt(jnp.mean(xv * xv, axis=-1, keepdims=True) + EPS)
        h_ref[...] = (xv * rstd * gn_ref[...] * (1.0 + sc_ref[0]) + sh_ref[0]).astype(BF16)

    vec = pl.BlockSpec((1, 1, d), lambda i: (i // tpb, 0, 0))
    return pl.pallas_call(
        body, name=name, grid=(m // tm,),
        in_specs=[pl.BlockSpec((tm, d), lambda i: (i, 0)), pl.BlockSpec((1, d), lambda i: (0, 0)), vec, vec],
        out_specs=pl.BlockSpec((tm, d), lambda i: (i, 0)),
        out_shape=jax.ShapeDtypeStruct((m, d), BF16), compiler_params=_params(1),
    )(x, gn, scale, shift)


def _normmod_bwd_epilogue(p, e_refs, o_refs):
    x_ref, dxo_ref, gn_ref, sc_ref = e_refs
    xv = x_ref[...]
    rstd = lax.rsqrt(jnp.mean(xv * xv, axis=-1, keepdims=True) + EPS)
    xhat = xv * rstd
    gn, sc1 = gn_ref[...], 1.0 + sc_ref[0]
    dxhat = p * (gn * sc1)
    dx = rstd * (dxhat - xhat * jnp.mean(dxhat * xhat, axis=-1, keepdims=True))
    o_refs[0][...] = dxo_ref[...] + dx
    t = p * xhat
    o_refs[1][0] = _rows_to_block([_colsum(p), _colsum(t * gn), _colsum(t * sc1)], p.shape[1])


def _residual_bwd(dx, f, gate, fac, seq, name):
    m, d = dx.shape
    tm = _tok_tile(seq)
    tpb = seq // tm

    def body(dx_ref, f_ref, g_ref, df_ref, dg_ref):
        dxv = dx_ref[...]
        df_ref[...] = ((fac * (1.0 + g_ref[0])) * dxv).astype(BF16)
        dg_ref[0] = _rows_to_block([_colsum((fac * dxv) * f_ref[...].astype(F32))], d)

    tile = pl.BlockSpec((tm, d), lambda i: (i, 0))
    return pl.pallas_call(
        body, name=name, grid=(m // tm,),
        in_specs=[tile, tile, pl.BlockSpec((1, 1, d), lambda i: (i // tpb, 0, 0))],
        out_specs=[tile, pl.BlockSpec((1, SUBLANES, d), lambda i: (i, 0, 0))],
        out_shape=[jax.ShapeDtypeStruct((m, d), BF16), jax.ShapeDtypeStruct((m // tm, SUBLANES, d), F32)],
        compiler_params=_params(1),
    )(dx, f, gate)


def _loss_head(y, target, seq):
    m, d = y.shape
    tm = _tok_tile(seq)

    def body(y_ref, t_ref, dy_ref, l_ref):
        err = y_ref[...] - t_ref[...]
        dy_ref[...] = err * (1.0 / d)
        part = 0.5 * jnp.sum(jnp.mean(err * err, axis=-1, keepdims=True), axis=0, keepdims=True)
        l_ref[0] = jnp.broadcast_to(part, (SUBLANES, LANES))

    tile = pl.BlockSpec((tm, d), lambda i: (i, 0))
    return pl.pallas_call(
        body, name="loss_head", grid=(m // tm,), in_specs=[tile, tile],
        out_specs=[tile, pl.BlockSpec((1, SUBLANES, LANES), lambda i: (i, 0, 0))],
        out_shape=[jax.ShapeDtypeStruct((m, d), F32), jax.ShapeDtypeStruct((m // tm, SUBLANES, LANES), F32)],
        compiler_params=_params(1),
    )(y, target)


def _ffn_fwd(x, h, wup, wdown, lead, gate, seq, tag):
    m, d = x.shape
    tm = _tok_tile(seq)
    tpb = seq // tm
    l, f_idx = lead

    def up_body(h_ref, wg_ref, wu_ref, a_ref, gu_ref):
        hv = h_ref[...]
        g, u = _dot(hv, wg_ref[...]), _dot(hv, wu_ref[...])
        a_ref[...] = (g * _sigmoid(g) * u).astype(BF16)
        gu_ref[0] = g.astype(BF16)
        gu_ref[1] = u.astype(BF16)

    wblk = (None, None, None, d, FF_SHARD)
    a, gu = pl.pallas_call(
        up_body, name=f"ffn_up_{tag}", grid=(N_FF_SHARD, m // tm),
        in_specs=[pl.BlockSpec((tm, d), lambda j, i: (i, 0)),
                  pl.BlockSpec(wblk, lambda j, i: (l, f_idx, j, 0, 0)),
                  pl.BlockSpec(wblk, lambda j, i: (l, f_idx, j + N_FF_SHARD, 0, 0))],
        out_specs=[pl.BlockSpec((None, tm, FF_SHARD), lambda j, i: (j, i, 0)),
                   pl.BlockSpec((2, None, tm, FF_SHARD), lambda j, i: (0, j, i, 0))],
        out_shape=[jax.ShapeDtypeStruct((N_FF_SHARD, m, FF_SHARD), BF16),
                   jax.ShapeDtypeStruct((2, N_FF_SHARD, m, FF_SHARD), BF16)],
        compiler_params=_params(2),
    )(h, wup, wup)

    def down_epilogue(p, e_refs, o_refs):
        x_ref, g_ref = e_refs
        o_refs[0][...] = x_ref[...] + (0.5 * (1.0 + g_ref[0])) * p
        o_refs[1][...] = p.astype(BF16)

    wdown5 = wdown.reshape(wdown.shape[:2] + (N_FF_SHARD, FF_SHARD, d))
    x_out, f = _mm(a, wdown5, mode="nn", tm=tm, tn=d, tk=D_FF, ksub=N_FF_SHARD, name=f"ffn_down_{tag}",
                   shape=(m, d, D_FF), a_spec=((N_FF_SHARD, tm, FF_SHARD), lambda i, j, k: (0, i, 0)),
                   b_spec=((None, None, N_FF_SHARD, FF_SHARD, d), lambda i, j, k: (l, f_idx, 0, 0, 0)),
                   extras=[(x, (tm, d), lambda i, j: (i, 0)), (gate, (1, 1, d), lambda i, j: (i // tpb, 0, 0))],
                   outs=[((m, d), F32, (tm, d), lambda i, j: (i, 0)), ((m, d), BF16, (tm, d), lambda i, j: (i, 0))],
                   epilogue=down_epilogue)
    return x_out, (a, gu, f)


def _ffn_bwd(dx_out, x, h, saved, wup, wdown, lead, gn, scale, gate, seq, tag):
    a, gu, f = saved
    m, d = x.shape
    tm = _tok_tile(seq)
    tpb = seq // tm
    l, f_idx = lead
    df, dgate_parts = _residual_bwd(dx_out, f, gate, 0.5, seq, f"ffn_res_bwd_{tag}")

    def act_bwd_epilogue(p, e_refs, o_refs):
        g, u = e_refs[0][0].astype(F32), e_refs[0][1].astype(F32)
        sg = _sigmoid(g)
        o_refs[0][0] = (p * u * (sg * (1.0 + g * (1.0 - sg)))).astype(BF16)
        o_refs[0][1] = (p * (g * sg)).astype(BF16)

    gu_blk = (2, None, tm, FF_SHARD)
    (dgu,) = _mm(df, wdown, mode="nt", tm=tm, tn=FF_SHARD, tk=d, name=f"ffn_down_dx_{tag}", shape=(m, D_FF, d),
                 b_spec=((None, None, FF_SHARD, d), lambda i, j, k: (l, f_idx, j, 0)),
                 extras=[(gu, gu_blk, lambda i, j: (0, j, i, 0))],
                 outs=[((2, N_FF_SHARD, m, FF_SHARD), BF16, gu_blk, lambda i, j: (0, j, i, 0))],
                 epilogue=act_bwd_epilogue)
    tt = 2 * tm if m % (2 * tm) == 0 else tm
    (dwdown,) = _mm(a, df, mode="tn", tm=FF_SHARD, tn=d, tk=tt, name=f"ffn_dwdown_{tag}", shape=(D_FF, d, m),
                    a_spec=((None, tt, FF_SHARD), lambda i, j, k: (i, k, 0)),
                    outs=[((D_FF, d), BF16, (FF_SHARD, d), lambda i, j: (i, 0))], epilogue=_store_epilogue([BF16]))
    dgu8 = dgu.reshape(2 * N_FF_SHARD, m, FF_SHARD)
    (dwup,) = _mm(h, dgu8, mode="tn", tm=d, tn=FF_SHARD, tk=tt, name=f"ffn_dwup_{tag}", shape=(d, 2 * D_FF, m),
                  b_spec=((None, tt, FF_SHARD), lambda i, j, k: (j, k, 0)),
                  outs=[((2 * N_FF_SHARD, d, FF_SHARD), BF16, (None, d, FF_SHARD), lambda i, j: (j, 0, 0))],
                  epilogue=_store_epilogue([BF16]))
    dx, nm_parts = _mm(dgu8, wup, mode="nt", tm=tm, tn=d, tk=D_FF, ksub=N_FF_SHARD, name=f"ffn_up_dx_{tag}",
                       shape=(m, d, 2 * D_FF), a_spec=((N_FF_SHARD, tm, FF_SHARD), lambda i, j, k: (k, i, 0)),
                       b_spec=((None, None, N_FF_SHARD, d, FF_SHARD), lambda i, j, k: (l, f_idx, k, 0, 0)),
                       extras=[(x, (tm, d), lambda i, j: (i, 0)), (dx_out, (tm, d), lambda i, j: (i, 0)),
                               (gn, (1, d), lambda i, j: (0, 0)), (scale, (1, 1, d), lambda i, j: (i // tpb, 0, 0))],
                       outs=[((m, d), F32, (tm, d), lambda i, j: (i, 0)),
                             ((m // tm, SUBLANES, d), F32, (1, SUBLANES, d), lambda i, j: (i, 0, 0))],
                       epilogue=_normmod_bwd_epilogue)
    return dx, dwup, dwdown, nm_parts, dgate_parts


def _shift_down(ext, n, rows):
    if n:
        ext = pltpu.roll(ext, n, 0)
    return ext[SUBLANES:SUBLANES + rows]


def _lru_gates(u, wr_ref, br_ref, wi_ref, bi_ref, lam_ref):
    ub = u.astype(BF16)
    r = _sigmoid(_dot(ub, wr_ref[...]) + br_ref[...])
    ig = _sigmoid(_dot(ub, wi_ref[...]) + bi_ref[...])
    sp = _softplus(-lam_ref[...])
    log_a = (-LRU_C * r) * sp
    a = jnp.exp(log_a)
    mult = jnp.sqrt(_neg_expm1(2.0 * log_a))
    return r, ig, sp, a, mult


def _conv(ext, cw_ref, cb_ref, rows):
    u = cb_ref[...] + cw_ref[3:4, :] * _shift_down(ext, 0, rows)
    for k in range(3):
        u = u + cw_ref[k:k + 1, :] * _shift_down(ext, 3 - k, rows)
    return u


def _lru_halo_spec(seq, ts):
    return pl.BlockSpec((SUBLANES, LRU_W),
                        lambda b, i: (jnp.maximum(b * (seq // SUBLANES) + i * (ts // SUBLANES) - 1, 0), 0))


def _lru_fwd(proj32, conv_w, conv_b, wr, br, wi, bi, lam, batch, seq):
    m = proj32.shape[0]
    ts = _tok_tile(seq)
    nt = seq // ts
    row = lambda b, i: (b * nt + i, 0)

    def body(x_ref, halo_ref, g_ref, cw_ref, cb_ref, wr_ref, br_ref, wi_ref, bi_ref, lam_ref,
             y_ref, h_ref, a_scr, b_scr, carry):
        i = pl.program_id(1)
        halo = jnp.where(i > 0, halo_ref[...], 0.0)
        ext = jnp.concatenate([halo, x_ref[...]], axis=0)
        u = _conv(ext, cw_ref, cb_ref, ts)
        _, ig, _, a, mult = _lru_gates(u, wr_ref, br_ref, wi_ref, bi_ref, lam_ref)
        a_scr[...] = a
        b_scr[...] = mult * (ig * u)

        @pl.when(i == 0)
        def _():
            carry[...] = jnp.zeros_like(carry)

        rid = lax.broadcasted_iota(jnp.int32, (SUBLANES, LRU_W), 0)

        def chunk(c, hprev):
            off = pl.multiple_of(c * SUBLANES, SUBLANES)
            av, bv = a_scr[pl.ds(off, SUBLANES), :], b_scr[pl.ds(off, SUBLANES), :]
            for d in (1, 2, 4):
                keep = rid >= d
                bv = jnp.where(keep, av * pltpu.roll(bv, d, 0) + bv, bv)
                av = jnp.where(keep, av * pltpu.roll(av, d, 0), av)
            h = av * hprev + bv
            h_ref[pl.ds(off, SUBLANES), :] = h
            return h[SUBLANES - 1:SUBLANES, :]

        carry[...] = lax.fori_loop(0, ts // SUBLANES, chunk, carry[...])
        gelu, _ = _gelu_parts(g_ref[...])
        y_ref[...] = h_ref[...] * gelu

    full = lambda shape: pl.BlockSpec(shape, lambda b, i: (0,) * len(shape))
    return pl.pallas_call(
        body, name="lru_fwd", grid=(batch, nt),
        in_specs=[pl.BlockSpec((ts, LRU_W), row), _lru_halo_spec(seq, ts),
                  pl.BlockSpec((ts, LRU_W), lambda b, i: (b * nt + i, 1)),
                  full((4, LRU_W)), full((1, LRU_W)), full((LRU_W, LRU_W)), full((1, LRU_W)),
                  full((LRU_W, LRU_W)), full((1, LRU_W)), full((1, LRU_W))],
        out_specs=[pl.BlockSpec((ts, LRU_W), row), pl.BlockSpec((ts, LRU_W), row)],
        out_shape=[jax.ShapeDtypeStruct((m, LRU_W), F32), jax.ShapeDtypeStruct((m, LRU_W), F32)],
        scratch_shapes=[pltpu.VMEM((ts, LRU_W), F32), pltpu.VMEM((ts, LRU_W), F32), pltpu.VMEM((1, LRU_W), F32)],
        compiler_params=_params(2),
    )(proj32, proj32, proj32, conv_w, conv_b, wr, br, wi, bi, lam)


def _lru_bwd(dy, proj32, h, conv_w, conv_b, wr, br, wi, bi, lam, batch, seq):
    m = proj32.shape[0]
    ts = _tok_tile(seq)
    nt = seq // ts
    row = lambda b, i: (b * nt + (nt - 1 - i), 0)
    halo = pl.BlockSpec((SUBLANES, LRU_W),
                        lambda b, i: (jnp.maximum(b * (seq // SUBLANES) + (nt - 1 - i) * (ts // SUBLANES) - 1, 0), 0))

    def body(dy_ref, x_ref, xhalo_ref, g_ref, h_ref, hhalo_ref, cw_ref, cb_ref, wr_ref, br_ref, wi_ref, bi_ref,
             lam_ref, dx_ref, dg_ref, dwr_ref, dwi_ref, sums_ref, a_scr, dh_scr, g_scr, carry, du_next):
        b, i = pl.program_id(0), pl.program_id(1)
        first_tile = i == nt - 1

        @pl.when((b == 0) & (i == 0))
        def _():
            dwr_ref[...] = jnp.zeros_like(dwr_ref)
            dwi_ref[...] = jnp.zeros_like(dwi_ref)
            sums_ref[...] = jnp.zeros_like(sums_ref)

        @pl.when(i == 0)
        def _():
            carry[...] = jnp.zeros_like(carry)
            du_next[...] = jnp.zeros_like(du_next)

        xhalo = jnp.where(first_tile, 0.0, xhalo_ref[...])
        ext = jnp.concatenate([xhalo, x_ref[...]], axis=0)
        u = _conv(ext, cw_ref, cb_ref, ts)
        r, ig, sp, a, mult = _lru_gates(u, wr_ref, br_ref, wi_ref, bi_ref, lam_ref)
        gelu, dgelu = _gelu_parts(g_ref[...])
        dyv, hv = dy_ref[...], h_ref[...]
        dg_ref[...] = (dyv * hv * dgelu).astype(BF16)
        a_scr[...] = a
        dh_scr[...] = dyv * gelu

        rid = lax.broadcasted_iota(jnp.int32, (SUBLANES, LRU_W), 0)
        nchunk = ts // SUBLANES

        def chunk(n, cg):
            off = pl.multiple_of((nchunk - 1 - n) * SUBLANES, SUBLANES)
            av, beta = a_scr[pl.ds(off, SUBLANES), :], dh_scr[pl.ds(off, SUBLANES), :]
            alpha = jnp.where(rid == SUBLANES - 1, 1.0, pltpu.roll(av, SUBLANES - 1, 0))
            for d in (1, 2, 4):
                keep = rid + d <= SUBLANES - 1
                beta = jnp.where(keep, beta + alpha * pltpu.roll(beta, SUBLANES - d, 0), beta)
                alpha = jnp.where(keep, alpha * pltpu.roll(alpha, SUBLANES - d, 0), alpha)
            gv = beta + alpha * cg
            g_scr[pl.ds(off, SUBLANES), :] = gv
            return av[0:1, :] * gv[0:1, :]

        carry[...] = lax.fori_loop(0, nchunk, chunk, carry[...])
        gv = g_scr[...]
        hhalo = jnp.where(first_tile, 0.0, hhalo_ref[...])
        hprev = _shift_down(jnp.concatenate([hhalo, hv], axis=0), 1, ts)
        dmult = gv * ig * u
        dig = gv * mult * u
        du = gv * mult * ig
        dlog_a = gv * hprev * a - dmult * a * a / mult
        dr = dlog_a * (-LRU_C * sp)
        dr_pre = dr * r * (1.0 - r)
        di_pre = dig * ig * (1.0 - ig)
        drb, dib, ub = dr_pre.astype(BF16), di_pre.astype(BF16), u.astype(BF16)
        du = du + _dot(drb, wr_ref[...], NT) + _dot(dib, wi_ref[...], NT)
        dwr_ref[...] += _dot(ub, drb, TN)
        dwi_ref[...] += _dot(ub, dib, TN)

        ext_du = jnp.concatenate([du, du_next[...]], axis=0)
        du_next[...] = du[0:SUBLANES, :]
        n_ext = ts + SUBLANES
        dx = cw_ref[3:4, :] * du
        sums = [_colsum(dr_pre), _colsum(di_pre), _colsum(dlog_a * (-LRU_C * r)), _colsum(du)]
        dcw = []
        for k in range(3):
            dx = dx + cw_ref[k:k + 1, :] * pltpu.roll(ext_du, n_ext - (3 - k), 0)[0:ts]
            dcw.append(_colsum(du * _shift_down(ext, 3 - k, ts)))
        dcw.append(_colsum(du * _shift_down(ext, 0, ts)))
        dx_ref[...] = dx.astype(BF16)
        sums_ref[...] += _rows_to_block(sums + dcw, LRU_W)

    full = lambda shape: pl.BlockSpec(shape, lambda b, i: (0,) * len(shape))
    tile = pl.BlockSpec((ts, LRU_W), row)
    return pl.pallas_call(
        body, name="lru_bwd", grid=(batch, nt),
        in_specs=[tile, tile, halo, pl.BlockSpec((ts, LRU_W), lambda b, i: (b * nt + (nt - 1 - i), 1)), tile, halo,
                  full((4, LRU_W)), full((1, LRU_W)), full((LRU_W, LRU_W)), full((1, LRU_W)),
                  full((LRU_W, LRU_W)), full((1, LRU_W)), full((1, LRU_W))],
        out_specs=[tile, tile, full((LRU_W, LRU_W)), full((LRU_W, LRU_W)), full((SUBLANES, LRU_W))],
        out_shape=[jax.ShapeDtypeStruct((m, LRU_W), BF16), jax.ShapeDtypeStruct((m, LRU_W), BF16),
                   jax.ShapeDtypeStruct((LRU_W, LRU_W), F32), jax.ShapeDtypeStruct((LRU_W, LRU_W), F32),
                   jax.ShapeDtypeStruct((SUBLANES, LRU_W), F32)],
        scratch_shapes=[pltpu.VMEM((ts, LRU_W), F32), pltpu.VMEM((ts, LRU_W), F32), pltpu.VMEM((ts, LRU_W), F32),
                        pltpu.VMEM((1, LRU_W), F32), pltpu.VMEM((SUBLANES, LRU_W), F32)],
        compiler_params=_params(2),
    )(dy, proj32, proj32, proj32, h, h, conv_w, conv_b, wr, br, wi, bi, lam)


def _head_masks():
    lane = lax.broadcasted_iota(jnp.int32, (1, LANES), 1)
    return lane < HEAD_DIM


def _stack_heads(x2):
    lo, zero = _head_masks(), jnp.zeros_like(x2)
    return jnp.concatenate([jnp.where(lo, x2, zero), jnp.where(lo, zero, x2)], axis=0)


def _unstack_heads(y):
    return jnp.where(_head_masks(), y[:TQ], y[TQ:])


def _stack_cols(a, b):
    return jnp.concatenate([a, b], axis=0)


def _causal(qi, kb, strict):
    r = jnp.bitwise_and(lax.broadcasted_iota(jnp.int32, (2 * TQ, BLK), 0), TQ - 1) + qi * TQ
    c = lax.broadcasted_iota(jnp.int32, (2 * TQ, BLK), 1) + kb * BLK
    return (c < r) if strict else (c <= r)


def _key_loop(qi, block, carry, descending=False):
    def trip(n, cr):
        for j in range(KB_PER_Q):
            done = n * KB_PER_Q + j
            cr = block(qi * KB_PER_Q - 1 - done if descending else done, cr, False)
        return cr

    return lax.fori_loop(0, qi, trip, carry)


def _tri(cmp):
    r = lax.broadcasted_iota(jnp.int32, (BLK, BLK), 0)
    c = lax.broadcasted_iota(jnp.int32, (BLK, BLK), 1)
    return cmp(r, c)


def _dot_split(x, tri):
    hi, lo = _split2(x)
    return _dot(hi, tri) + _dot(lo, tri)


def _sb_fwd(proj16, batch, seq):
    nq = seq // TQ
    scale = HEAD_DIM ** -0.5

    def body(q_ref, k_ref, v_ref, y_ref, t_ref):
        qi = pl.program_id(2)
        qs = _stack_heads(q_ref[0])
        tri_after = _tri(lambda r, c: r > c).astype(BF16)

        def block(kb, carry, masked):
            acc, c = carry
            ks = pl.multiple_of(kb * BLK, BLK)
            k2, v2 = k_ref[0, pl.ds(ks, BLK), :], v_ref[0, pl.ds(ks, BLK), :]
            z = _dot(qs, k2, NT) * scale
            sp = _softplus(z)
            l = -sp
            if masked:
                valid = _causal(qi, kb, True)
                l = jnp.where(valid, l, 0.0)
            w = jnp.exp((z - sp) + _dot_split(l, tri_after) + c)
            if masked:
                w = jnp.where(valid, w, 0.0)
            return acc + _dot(w.astype(BF16), v2), c + jnp.sum(l, axis=1, keepdims=True)

        carry = (jnp.zeros((2 * TQ, LANES), F32), jnp.zeros((2 * TQ, 1), F32))
        first = qi * KB_PER_Q
        for n in reversed(range(KB_PER_Q)):
            carry = block(first + n, carry, True)
        acc, c = _key_loop(qi, block, carry, descending=True)
        y_ref[...] = _unstack_heads(acc)
        t_ref[0] = _unstack_heads(jnp.broadcast_to(c, (2 * TQ, LANES)))

    m = batch * seq
    return pl.pallas_call(
        body, name="sb_fwd", grid=(batch, 2, nq),
        in_specs=[pl.BlockSpec((1, TQ, LANES), lambda b, p, q: (b, q, COL_SBQ + p)),
                  pl.BlockSpec((1, seq, LANES), lambda b, p, q: (b, 0, COL_SBK + p)),
                  pl.BlockSpec((1, seq, LANES), lambda b, p, q: (b, 0, COL_SBV + p))],
        out_specs=[pl.BlockSpec((TQ, LANES), lambda b, p, q: (b * nq + q, p)),
                   pl.BlockSpec((1, TQ, LANES), lambda b, p, q: (p, b * nq + q, 0))],
        out_shape=[jax.ShapeDtypeStruct((m, ATT_W), F32), jax.ShapeDtypeStruct((2, m, LANES), F32)],
        compiler_params=_params(3),
    )(proj16, proj16, proj16)


def _sb_bwd(dy, t, proj16, batch, seq):
    nq = seq // TQ
    scale = HEAD_DIM ** -0.5

    def body(dy_ref, t_ref, q_ref, k_ref, v_ref, dq_ref, dk_ref, dv_ref):
        qi = pl.program_id(2)

        @pl.when(qi == 0)
        def _():
            dk_ref[...] = jnp.zeros_like(dk_ref)
            dv_ref[...] = jnp.zeros_like(dv_ref)

        t2 = t_ref[0]
        qs, dys = _stack_heads(q_ref[0]), _stack_heads(dy_ref[...].astype(BF16))
        tot = _stack_cols(t2[:, 0:1], t2[:, HEAD_DIM:HEAD_DIM + 1])
        tri_incl = _tri(lambda r, c: r <= c).astype(BF16)
        tri_excl = _tri(lambda r, c: r < c).astype(BF16)

        def block(kb, carry, masked):
            dq, pc, ec = carry
            ks = pl.multiple_of(kb * BLK, BLK)
            k2, v2 = k_ref[0, pl.ds(ks, BLK), :], v_ref[0, pl.ds(ks, BLK), :]
            z = _dot(qs, k2, NT) * scale
            sp = _softplus(z)
            l, b = -sp, z - sp
            sig = jnp.exp(b)
            if masked:
                valid = _causal(qi, kb, True)
                l = jnp.where(valid, l, 0.0)
            after = tot - (pc + _dot_split(l, tri_incl))
            w = jnp.exp(b + after)
            if masked:
                w = jnp.where(valid, w, 0.0)
            e = _dot(dys, v2, NT) * w
            et = ec + _dot_split(e, tri_excl)
            dz = e * (1.0 - sig) - et * sig
            if masked:
                dz = jnp.where(valid, dz, 0.0)
            dzb = (dz * scale).astype(BF16)
            dk_ref[0, pl.ds(ks, BLK), :] += _dot(dzb, qs, TN)
            dv_ref[0, pl.ds(ks, BLK), :] += _dot(w.astype(BF16), dys, TN)
            return (dq + _dot(dzb, k2), pc + jnp.sum(l, axis=1, keepdims=True),
                    ec + jnp.sum(e, axis=1, keepdims=True))

        col = jnp.zeros((2 * TQ, 1), F32)
        first = qi * KB_PER_Q
        carry = _key_loop(qi, block, (jnp.zeros((2 * TQ, LANES), F32), col, col))
        for n in range(KB_PER_Q):
            carry = block(first + n, carry, True)
        dq_ref[...] = _unstack_heads(carry[0])

    m = batch * seq
    whole = lambda col: pl.BlockSpec((1, seq, LANES), lambda b, p, q: (b, 0, col + p))
    return pl.pallas_call(
        body, name="sb_bwd", grid=(batch, 2, nq),
        in_specs=[pl.BlockSpec((TQ, LANES), lambda b, p, q: (b * nq + q, p)),
                  pl.BlockSpec((1, TQ, LANES), lambda b, p, q: (p, b * nq + q, 0)),
                  pl.BlockSpec((1, TQ, LANES), lambda b, p, q: (b, q, COL_SBQ + p)),
                  whole(COL_SBK), whole(COL_SBV)],
        out_specs=[pl.BlockSpec((TQ, LANES), lambda b, p, q: (b * nq + q, p)), whole(0), whole(0)],
        out_shape=[jax.ShapeDtypeStruct((m, ATT_W), F32), jax.ShapeDtypeStruct((batch, seq, ATT_W), F32),
                   jax.ShapeDtypeStruct((batch, seq, ATT_W), F32)],
        compiler_params=_params(3),
    )(dy, t, proj16, proj16, proj16)


def _fox_pre(proj32, gq, gk, bf, group_mean, batch, seq):
    m = proj32.shape[0]
    ts = _tok_tile(seq)
    nt = seq // ts

    def body(q_ref, k_ref, f_ref, gq_ref, gk_ref, bf_ref, gm_ref, fq_ref, fk_ref, fc_ref, carry):
        i = pl.program_id(1)

        @pl.when(i == 0)
        def _():
            carry[...] = jnp.zeros_like(carry)

        gm = gm_ref[...]
        for src, g_ref, dst in ((q_ref, gq_ref, fq_ref), (k_ref, gk_ref, fk_ref)):
            v = src[...]
            ms = _dot_split(v * v, gm)
            dst[...] = (v * lax.rsqrt(ms + EPS) * g_ref[...]).astype(BF16)
        z = f_ref[...] + bf_ref[...]
        lf = jnp.minimum(z, 0.0) - jnp.log(1.0 + jnp.exp(-jnp.abs(z)))
        r = lax.broadcasted_iota(jnp.int32, (ts, ts), 0)
        c = lax.broadcasted_iota(jnp.int32, (ts, ts), 1)
        tri = (r >= c).astype(BF16)
        hi, mid, low = _split3(lf)
        fc = _dot(tri, hi) + _dot(tri, mid) + _dot(tri, low) + carry[...]
        fc_ref[...] = fc
        carry[...] = fc[ts - 1:ts, :]

    full = lambda shape: pl.BlockSpec(shape, lambda b, i: (0,) * len(shape))
    return pl.pallas_call(
        body, name="fox_pre", grid=(batch, nt),
        in_specs=[pl.BlockSpec((ts, ATT_W), lambda b, i: (b * nt + i, 7)),
                  pl.BlockSpec((ts, ATT_W), lambda b, i: (b * nt + i, 8)),
                  pl.BlockSpec((ts, LANES), lambda b, i: (b * nt + i, COL_FXF)),
                  full((1, ATT_W)), full((1, ATT_W)), full((1, LANES)), full((ATT_W, ATT_W))],
        out_specs=[pl.BlockSpec((ts, ATT_W), lambda b, i: (b * nt + i, 0)),
                   pl.BlockSpec((ts, ATT_W), lambda b, i: (b * nt + i, 0)),
                   pl.BlockSpec((ts, LANES), lambda b, i: (b * nt + i, 0))],
        out_shape=[jax.ShapeDtypeStruct((m, ATT_W), BF16), jax.ShapeDtypeStruct((m, ATT_W), BF16),
                   jax.ShapeDtypeStruct((m, LANES), F32)],
        scratch_shapes=[pltpu.VMEM((1, LANES), F32)],
        compiler_params=_params(2),
    )(proj32, proj32, proj32, gq, gk, bf, group_mean)


def _fox_specs(batch, seq):
    nq = seq // TQ
    return dict(
        qblk=pl.BlockSpec((1, TQ, LANES), lambda b, p, q: (b, q, p)),
        whole=pl.BlockSpec((1, seq, LANES), lambda b, p, q: (b, 0, p)),
        vwhole=pl.BlockSpec((1, seq, LANES), lambda b, p, q: (b, 0, COL_FXV + p)),
        fcol=pl.BlockSpec((1, 1, TQ, 2), lambda b, p, q: (b, p, q, 0)),
        frow=pl.BlockSpec((1, 1, 2, seq), lambda b, p, q: (b, p, 0, 0)),
        rows=pl.BlockSpec((TQ, LANES), lambda b, p, q: (b * nq + q, p)),
        stat=pl.BlockSpec((1, TQ, LANES), lambda b, p, q: (p, b * nq + q, 0)),
    )


def _fox_logits(qs, k2, fq_col, fr_ref, ks, is_a, scale):
    fk_row = jnp.where(is_a, fr_ref[0, 0, 0:1, pl.ds(ks, BLK)], fr_ref[0, 0, 1:2, pl.ds(ks, BLK)])
    return _dot(qs, k2, NT) * scale + fq_col - fk_row


def _fox_fwd(fq, fk, proj16, fcol, frow, batch, seq):
    nq = seq // TQ
    scale = HEAD_DIM ** -0.5

    def body(q_ref, k_ref, v_ref, fc_ref, fr_ref, y_ref, lse_ref):
        qi = pl.program_id(2)
        qs = _stack_heads(q_ref[0])
        fcv = fc_ref[0, 0]
        fq_col = _stack_cols(fcv[:, 0:1], fcv[:, 1:2])
        is_a = lax.broadcasted_iota(jnp.int32, (2 * TQ, 1), 0) < TQ

        def block(kb, carry, masked):
            acc, mx, den = carry
            ks = pl.multiple_of(kb * BLK, BLK)
            k2, v2 = k_ref[0, pl.ds(ks, BLK), :], v_ref[0, pl.ds(ks, BLK), :]
            s = _fox_logits(qs, k2, fq_col, fr_ref, ks, is_a, scale)
            if masked:
                s = jnp.where(_causal(qi, kb, False), s, NEG_BIG)
            mx_new = jnp.maximum(mx, jnp.max(s, axis=1, keepdims=True))
            p = jnp.exp(s - mx_new)
            alpha = jnp.exp(mx - mx_new)
            return (alpha * acc + _dot(p.astype(BF16), v2), mx_new, alpha * den + jnp.sum(p, axis=1, keepdims=True))

        first = qi * KB_PER_Q
        carry = (jnp.zeros((2 * TQ, LANES), F32), jnp.full((2 * TQ, 1), NEG_BIG, F32), jnp.zeros((2 * TQ, 1), F32))
        carry = _key_loop(qi, block, carry)
        for n in range(KB_PER_Q):
            carry = block(first + n, carry, True)
        acc, mx, den = carry
        y_ref[...] = _unstack_heads(acc / den)
        lse_ref[0] = _unstack_heads(jnp.broadcast_to(mx + jnp.log(den), (2 * TQ, LANES)))

    m = batch * seq
    sp = _fox_specs(batch, seq)
    return pl.pallas_call(
        body, name="fox_fwd", grid=(batch, 2, nq),
        in_specs=[sp["qblk"], sp["whole"], sp["vwhole"], sp["fcol"], sp["frow"]],
        out_specs=[sp["rows"], sp["stat"]],
        out_shape=[jax.ShapeDtypeStruct((m, ATT_W), F32), jax.ShapeDtypeStruct((2, m, LANES), F32)],
        compiler_params=_params(3),
    )(fq, fk, proj16, fcol, frow)


def _fox_bwd(dy, y, lse, fq, fk, proj16, fcol, frow, batch, seq):
    nq = seq // TQ
    scale = HEAD_DIM ** -0.5

    def body(dy_ref, y_ref, lse_ref, q_ref, k_ref, v_ref, fc_ref, fr_ref, dq_ref, dk_ref, dv_ref, dfc_ref):
        qi = pl.program_id(2)

        @pl.when(qi == 0)
        def _():
            dk_ref[...] = jnp.zeros_like(dk_ref)
            dv_ref[...] = jnp.zeros_like(dv_ref)
            dfc_ref[...] = jnp.zeros_like(dfc_ref)

        lo = _head_masks()
        lane = lax.broadcasted_iota(jnp.int32, (1, LANES), 1)
        dy2, lse2, fcv = dy_ref[...], lse_ref[0], fc_ref[0, 0]
        qs, dys = _stack_heads(q_ref[0]), _stack_heads(dy2.astype(BF16))
        dyy = dy2 * y_ref[...]
        delta = _stack_cols(jnp.sum(jnp.where(lo, dyy, 0.0), axis=1, keepdims=True),
                            jnp.sum(jnp.where(lo, 0.0, dyy), axis=1, keepdims=True))
        lse_col = _stack_cols(lse2[:, 0:1], lse2[:, HEAD_DIM:HEAD_DIM + 1])
        fq_col = _stack_cols(fcv[:, 0:1], fcv[:, 1:2])
        is_a = lax.broadcasted_iota(jnp.int32, (2 * TQ, 1), 0) < TQ
        pick = jnp.where(is_a, (lane == 0).astype(F32), (lane == 1).astype(F32)).astype(BF16)

        def block(kb, carry, masked):
            dq, rs = carry
            ks = pl.multiple_of(kb * BLK, BLK)
            k2, v2 = k_ref[0, pl.ds(ks, BLK), :], v_ref[0, pl.ds(ks, BLK), :]
            p = jnp.exp(_fox_logits(qs, k2, fq_col, fr_ref, ks, is_a, scale) - lse_col)
            if masked:
                p = jnp.where(_causal(qi, kb, False), p, 0.0)
            ds = p * (_dot(dys, v2, NT) - delta)
            dsb = (ds * scale).astype(BF16)
            dk_ref[0, pl.ds(ks, BLK), :] += _dot(dsb, qs, TN)
            dv_ref[0, pl.ds(ks, BLK), :] += _dot(p.astype(BF16), dys, TN)
            ds_hi, ds_lo = _split2(ds)
            dfc_ref[0, 0, pl.ds(ks, BLK), :] -= _dot(ds_hi, pick, TN) + _dot(ds_lo, pick, TN)
            return dq + _dot(dsb, k2), rs + jnp.sum(ds, axis=1, keepdims=True)

        first = qi * KB_PER_Q
        carry = _key_loop(qi, block, (jnp.zeros((2 * TQ, LANES), F32), jnp.zeros((2 * TQ, 1), F32)))
        for n in range(KB_PER_Q):
            carry = block(first + n, carry, True)
        dq, rs = carry
        dq_ref[...] = _unstack_heads(dq)
        q0 = pl.multiple_of(qi * TQ, TQ)
        dfc_ref[0, 0, pl.ds(q0, TQ), :] += jnp.where(lane == 0, rs[:TQ], jnp.where(lane == 1, rs[TQ:], 0.0))

    m = batch * seq
    sp = _fox_specs(batch, seq)
    return pl.pallas_call(
        body, name="fox_bwd", grid=(batch, 2, nq),
        in_specs=[sp["rows"], sp["rows"], sp["stat"], sp["qblk"], sp["whole"], sp["vwhole"], sp["fcol"], sp["frow"]],
        out_specs=[sp["rows"], sp["whole"], sp["whole"],
                   pl.BlockSpec((1, 1, seq, LANES), lambda b, p, q: (b, p, 0, 0))],
        out_shape=[jax.ShapeDtypeStruct((m, ATT_W), F32), jax.ShapeDtypeStruct((batch, seq, ATT_W), F32),
                   jax.ShapeDtypeStruct((batch, seq, ATT_W), F32),
                   jax.ShapeDtypeStruct((batch, 2, seq, LANES), F32)],
        compiler_params=_params(3),
    )(dy, y, lse, fq, fk, proj16, fcol, frow)


def _fox_post_bwd(dfq, dfk, dfc, proj32, gq, gk, bf, group_mean, batch, seq):
    m = proj32.shape[0]
    ts = _tok_tile(seq)
    nt = seq // ts
    tile = lambda w, col: pl.BlockSpec((ts, w), lambda b, i: (b * nt + (nt - 1 - i), col))

    def body(dfq_ref, dfk_ref, dfc_ref, q_ref, k_ref, f_ref, gq_ref, gk_ref, bf_ref, gm_ref,
             dq_ref, dk_ref, df_ref, gs_ref, bs_ref, carry):
        i = pl.program_id(1)

        @pl.when(i == 0)
        def _():
            carry[...] = jnp.zeros_like(carry)

        gm = gm_ref[...]
        rows = []
        for src, g_ref, d_ref, dst in ((q_ref, gq_ref, dfq_ref, dq_ref), (k_ref, gk_ref, dfk_ref, dk_ref)):
            v, dv = src[...], d_ref[...]
            rstd = lax.rsqrt(_dot_split(v * v, gm) + EPS)
            vhat = v * rstd
            rows.append(_colsum(dv * vhat))
            dvh = dv * g_ref[...]
            dst[...] = (rstd * (dvh - vhat * _dot_split(dvh * vhat, gm))).astype(BF16)
        gs_ref[0] = _rows_to_block(rows, ATT_W)

        dfc_v = dfc_ref[...]
        r = lax.broadcasted_iota(jnp.int32, (ts, ts), 0)
        c = lax.broadcasted_iota(jnp.int32, (ts, ts), 1)
        tri = (r <= c).astype(BF16)
        hi, mid, low = _split3(dfc_v)
        dlf = _dot(tri, hi) + _dot(tri, mid) + _dot(tri, low) + carry[...]
        carry[...] = dlf[0:1, :]
        z = f_ref[...] + bf_ref[...]
        dz = dlf * _sigmoid(-z)
        df_ref[...] = dz.astype(BF16)
        bs_ref[0] = _rows_to_block([_colsum(dz)], LANES)

    full = lambda shape: pl.BlockSpec(shape, lambda b, i: (0,) * len(shape))
    part = lambda w: pl.BlockSpec((1, SUBLANES, w), lambda b, i: (b * nt + (nt - 1 - i), 0, 0))
    return pl.pallas_call(
        body, name="fox_post_bwd", grid=(batch, nt),
        in_specs=[tile(ATT_W, 0), tile(ATT_W, 0), tile(LANES, 0), tile(ATT_W, 7), tile(ATT_W, 8), tile(LANES, COL_FXF),
                  full((1, ATT_W)), full((1, ATT_W)), full((1, LANES)), full((ATT_W, ATT_W))],
        out_specs=[tile(ATT_W, 0), tile(ATT_W, 0), tile(LANES, 0), part(ATT_W), part(LANES)],
        out_shape=[jax.ShapeDtypeStruct((m, ATT_W), BF16), jax.ShapeDtypeStruct((m, ATT_W), BF16),
                   jax.ShapeDtypeStruct((m, LANES), BF16),
                   jax.ShapeDtypeStruct((batch * nt, SUBLANES, ATT_W), F32),
                   jax.ShapeDtypeStruct((batch * nt, SUBLANES, LANES), F32)],
        scratch_shapes=[pltpu.VMEM((1, LANES), F32)],
        compiler_params=_params(2),
    )(dfq, dfk, dfc, proj32, proj32, proj32, gq, gk, bf, group_mean)


_GROUPS = ((0, LRU_W), (LRU_W, LRU_W + ATT_W), (LRU_W + ATT_W, LRU_W + 2 * ATT_W))


def _outnorm(y_lru, y_sb, y_fox, gmix, seq):
    m = y_lru.shape[0]
    tm = _tok_tile(seq)

    def body(a_ref, b_ref, c_ref, g_ref, o_ref):
        parts = []
        for ref in (a_ref, b_ref, c_ref):
            v = ref[...]
            parts.append(v * lax.rsqrt(jnp.mean(v * v, axis=-1, keepdims=True) + EPS))
        o_ref[...] = (jnp.concatenate(parts, axis=1) * g_ref[...]).astype(BF16)

    t = lambda w: pl.BlockSpec((tm, w), lambda i: (i, 0))
    return pl.pallas_call(
        body, name="outnorm", grid=(m // tm,),
        in_specs=[t(LRU_W), t(ATT_W), t(ATT_W), pl.BlockSpec((1, D_MODEL), lambda i: (0, 0))],
        out_specs=t(D_MODEL), out_shape=jax.ShapeDtypeStruct((m, D_MODEL), BF16), compiler_params=_params(1),
    )(y_lru, y_sb, y_fox, gmix)


def _outnorm_bwd_epilogue(p, e_refs, o_refs):
    gmix = e_refs[3][...]
    dg = []
    for n, (lo, hi) in enumerate(_GROUPS):
        v, dyn = e_refs[n][...], p[:, lo:hi]
        rstd = lax.rsqrt(jnp.mean(v * v, axis=-1, keepdims=True) + EPS)
        vhat = v * rstd
        dg.append(_colsum(dyn * vhat))
        dvh = dyn * gmix[:, lo:hi]
        o_refs[n][...] = rstd * (dvh - vhat * jnp.mean(dvh * vhat, axis=-1, keepdims=True))
    o_refs[3][0] = _rows_to_block([jnp.concatenate(dg, axis=1)], p.shape[1])


def _pair_layouts(fcum, batch, seq):
    f4 = fcum[:, :4].reshape(batch, seq, 2, 2)
    return f4.transpose(0, 2, 1, 3), f4.transpose(0, 2, 3, 1)


def _mixer_fwd(x, h, w, gate, batch, seq):
    m, d = x.shape
    tm = _tok_tile(seq)
    tpb = seq // tm

    def in_epilogue(p, e_refs, o_refs):
        o_refs[0][...] = p
        o_refs[1][...] = p.astype(BF16)

    tn_in = 896
    proj32, proj16 = _mm(h, w["w_in"], mode="nn", tm=tm, tn=tn_in, tk=d, b_lead=w["lead"], name="mix_in",
                         outs=[((m, N_IN_PAD), F32, (tm, tn_in), lambda i, j: (i, j)),
                               ((m, N_IN_PAD), BF16, (tm, tn_in), lambda i, j: (i, j))],
                         epilogue=in_epilogue)
    y_lru, h_lru = _lru_fwd(proj32, w["conv_w"], w["conv_b"], w["wr"], w["br"], w["wi"], w["bi"], w["lam"], batch, seq)
    p16 = proj16.reshape(batch, seq, N_IN_PAD)
    y_sb, t_sb = _sb_fwd(p16, batch, seq)
    fq, fk, fcum = _fox_pre(proj32, w["gq"], w["gk"], w["bf"], w["group_mean"], batch, seq)
    fcol, frow = _pair_layouts(fcum, batch, seq)
    fq3, fk3 = fq.reshape(batch, seq, ATT_W), fk.reshape(batch, seq, ATT_W)
    y_fox, lse = _fox_fwd(fq3, fk3, p16, fcol, frow, batch, seq)
    ynorm = _outnorm(y_lru, y_sb, y_fox, w["gmix"], seq)

    def out_epilogue(p, e_refs, o_refs):
        x_ref, g_ref = e_refs
        o_refs[0][...] = x_ref[...] + (1.0 + g_ref[0]) * p
        o_refs[1][...] = p.astype(BF16)

    x_out, out = _mm(ynorm, w["w_out"], mode="nn", tm=tm, tn=d, tk=d, b_lead=w["lead"], name="mix_out",
                     extras=[(x, (tm, d), lambda i, j: (i, 0)), (gate, (1, 1, d), lambda i, j: (i // tpb, 0, 0))],
                     outs=[((m, d), F32, (tm, d), lambda i, j: (i, 0)), ((m, d), BF16, (tm, d), lambda i, j: (i, 0))],
                     epilogue=out_epilogue)
    saved = dict(proj32=proj32, p16=p16, h_lru=h_lru, y_lru=y_lru, y_sb=y_sb, t_sb=t_sb, fq3=fq3, fk3=fk3,
                 fcol=fcol, frow=frow, y_fox=y_fox, lse=lse, ynorm=ynorm, out=out)
    return x_out, saved


def _mixer_bwd(dx_out, x, h, s, w, gn, scale, gate, batch, seq):
    m, d = x.shape
    tm = _tok_tile(seq)
    tpb = seq // tm
    dout, dgate_parts = _residual_bwd(dx_out, s["out"], gate, 1.0, seq, "mix_res_bwd")
    (dw_out,) = _mm(s["ynorm"], dout, mode="tn", tm=d, tn=d, tk=tm, name="mix_dwout",
                    outs=[((d, d), BF16, (d, d), lambda i, j: (i, j))], epilogue=_store_epilogue([BF16]))
    dy_lru, dy_sb, dy_fox, gmix_parts = _mm(
        dout, w["w_out"], mode="nt", tm=tm, tn=d, tk=d, b_lead=w["lead"], name="mix_out_dx",
        extras=[(s["y_lru"], (tm, LRU_W), lambda i, j: (i, 0)), (s["y_sb"], (tm, ATT_W), lambda i, j: (i, 0)),
                (s["y_fox"], (tm, ATT_W), lambda i, j: (i, 0)), (w["gmix"], (1, d), lambda i, j: (0, 0))],
        outs=[((m, LRU_W), F32, (tm, LRU_W), lambda i, j: (i, 0)), ((m, ATT_W), F32, (tm, ATT_W), lambda i, j: (i, 0)),
              ((m, ATT_W), F32, (tm, ATT_W), lambda i, j: (i, 0)),
              ((m // tm, SUBLANES, d), F32, (1, SUBLANES, d), lambda i, j: (i, 0, 0))],
        epilogue=_outnorm_bwd_epilogue)

    dsq, dsk, dsv = _sb_bwd(dy_sb, s["t_sb"], s["p16"], batch, seq)
    dfq, dfk, dfv, dfc = _fox_bwd(dy_fox, s["y_fox"], s["lse"], s["fq3"], s["fk3"], s["p16"], s["fcol"], s["frow"],
                                  batch, seq)
    dfc_cols = dfc[..., :2].transpose(0, 2, 1, 3).reshape(m, 4)
    dfc_cols = jnp.pad(dfc_cols, ((0, 0), (0, LANES - 4)))
    dxq, dxk, dxf, gqk_parts, bf_parts = _fox_post_bwd(dfq, dfk.reshape(m, ATT_W), dfc_cols, s["proj32"],
                                                       w["gq"], w["gk"], w["bf"], w["group_mean"], batch, seq)
    dlx, dlg, dwr, dwi, lru_sums = _lru_bwd(dy_lru, s["proj32"], s["h_lru"], w["conv_w"], w["conv_b"], w["wr"],
                                            w["br"], w["wi"], w["bi"], w["lam"], batch, seq)
    dproj = jnp.concatenate([dlx, dlg, dsq.astype(BF16), dsk.reshape(m, ATT_W).astype(BF16),
                             dsv.reshape(m, ATT_W).astype(BF16), dxq, dxk, dfv.reshape(m, ATT_W).astype(BF16), dxf],
                            axis=1)
    tn_in = 896
    (dw_in,) = _mm(h, dproj, mode="tn", tm=d, tn=tn_in, tk=tm, name="mix_dwin",
                   outs=[((d, N_IN_PAD), BF16, (d, tn_in), lambda i, j: (i, j))], epilogue=_store_epilogue([BF16]))
    dx, nm_parts = _mm(dproj, w["w_in"], mode="nt", tm=tm, tn=d, tk=tn_in, b_lead=w["lead"], name="mix_in_dx",
                       extras=[(x, (tm, d), lambda i, j: (i, 0)), (dx_out, (tm, d), lambda i, j: (i, 0)),
                               (gn, (1, d), lambda i, j: (0, 0)), (scale, (1, 1, d), lambda i, j: (i // tpb, 0, 0))],
                       outs=[((m, d), F32, (tm, d), lambda i, j: (i, 0)),
                             ((m // tm, SUBLANES, d), F32, (1, SUBLANES, d), lambda i, j: (i, 0, 0))],
                       epilogue=_normmod_bwd_epilogue)
    grads = dict(dw_in=dw_in, dw_out=dw_out, dwr=dwr, dwi=dwi, lru_sums=lru_sums, gmix_parts=gmix_parts,
                 gqk_parts=gqk_parts, bf_parts=bf_parts)
    return dx, grads, nm_parts, dgate_parts


def _block_diag(w):
    nb = w.shape[0]
    eye = jnp.eye(nb, dtype=w.dtype)
    return (eye[:, None, :, None] * w[:, :, None, :]).reshape(nb * HEAD_DIM, nb * HEAD_DIM)


def _block_diag_grad(g):
    nb = LRU_W // HEAD_DIM
    g4 = g.reshape(nb, HEAD_DIM, nb, HEAD_DIM)
    return jnp.stack([g4[n, :, n, :] for n in range(nb)])


def _per_batch(parts, batch, row):
    r = parts[:, row, :]
    return r.reshape(batch, -1, r.shape[-1]).sum(axis=1)


def _local_step(x3, target3, mod, wts):
    batch, seq, d = x3.shape
    assert seq % TQ == 0, seq
    m = batch * seq
    n_layers = mod.shape[0]
    x = x3.reshape(m, d)
    group_mean = _block_diag(jnp.full((ATT_W // HEAD_DIM, HEAD_DIM, HEAD_DIM), 1.0 / HEAD_DIM, BF16))
    vec = lambda l, j, t: mod[l, :, j, t][:, None, :]

    layers, saved = [], []
    for l in range(n_layers):
        gq = jnp.tile(wts["g_qk"][l, 0], ATT_W // HEAD_DIM)[None, :]
        gk = jnp.tile(wts["g_qk"][l, 1], ATT_W // HEAD_DIM)[None, :]
        bf = jnp.pad(wts["b_fgate"][l], (0, LANES - 4))[None, :]
        lw = dict(lead=(l,), w_in=wts["w_in"], w_out=wts["w_out"], conv_w=wts["conv_w"][l],
                  conv_b=wts["conv_b"][l][None, :], wr=_block_diag(wts["w_rgate"][l]).astype(BF16),
                  br=wts["b_rgate"][l][None, :], wi=_block_diag(wts["w_igate"][l]).astype(BF16),
                  bi=wts["b_igate"][l][None, :], lam=wts["lru_lambda"][l][None, :], gq=gq, gk=gk, bf=bf,
                  group_mean=group_mean, gmix=wts["g_mix_out"][l][None, :])
        layers.append(lw)
        gn = lambda j: wts["g_norm"][l, j][None, :]
        sv = dict(x0=x)
        sv["h0"] = _normmod(x, gn(0), vec(l, 0, 1), vec(l, 0, 0), seq, f"normmod_{l}_0")
        x, sv["ffn0"] = _ffn_fwd(x, sv["h0"], wts["w_up"], wts["w_down"], (l, 0), vec(l, 0, 2), seq, f"{l}_0")
        sv["x1"] = x
        sv["h1"] = _normmod(x, gn(1), vec(l, 1, 1), vec(l, 1, 0), seq, f"normmod_{l}_1")
        x, sv["mix"] = _mixer_fwd(x, sv["h1"], lw, vec(l, 1, 2), batch, seq)
        sv["x2"] = x
        sv["h2"] = _normmod(x, gn(2), vec(l, 2, 1), vec(l, 2, 0), seq, f"normmod_{l}_2")
        x, sv["ffn1"] = _ffn_fwd(x, sv["h2"], wts["w_up"], wts["w_down"], (l, 1), vec(l, 2, 2), seq, f"{l}_1")
        saved.append(sv)

    dx, loss_parts = _loss_head(x, target3.reshape(m, d), seq)
    loss = jnp.sum(loss_parts[:, 0, 0])

    big = dict(w_up=[], w_down=[], w_in=[], w_out=[])
    small = {k: [] for k in ("dmod", "g_norm", "b_fgate", "conv_w", "conv_b", "w_rgate", "b_rgate", "w_igate",
                             "b_igate", "lru_lambda", "g_qk", "g_mix_out")}
    for l in reversed(range(n_layers)):
        sv, lw = saved[l], layers[l]
        gn = lambda j: wts["g_norm"][l, j][None, :]
        dx, dwup1, dwdown1, nm2, dg2 = _ffn_bwd(dx, sv["x2"], sv["h2"], sv["ffn1"], wts["w_up"], wts["w_down"], (l, 1),
                                               gn(2), vec(l, 2, 1), vec(l, 2, 2), seq, f"{l}_1")
        dx, mg, nm1, dg1 = _mixer_bwd(dx, sv["x1"], sv["h1"], sv["mix"], lw, gn(1), vec(l, 1, 1), vec(l, 1, 2),
                                      batch, seq)
        dx, dwup0, dwdown0, nm0, dg0 = _ffn_bwd(dx, sv["x0"], sv["h0"], sv["ffn0"], wts["w_up"], wts["w_down"], (l, 0),
                                               gn(0), vec(l, 0, 1), vec(l, 0, 2), seq, f"{l}_0")
        big["w_up"].insert(0, jnp.stack([dwup0, dwup1]))
        big["w_down"].insert(0, jnp.stack([dwdown0, dwdown1]))
        big["w_in"].insert(0, mg["dw_in"])
        big["w_out"].insert(0, mg["dw_out"])
        dmod_l, gnorm_l = [], []
        for nm, dg in ((nm0, dg0), (nm1, dg1), (nm2, dg2)):
            dmod_l.append(jnp.stack([_per_batch(nm, batch, 0), _per_batch(nm, batch, 1), _per_batch(dg, batch, 0)],
                                    axis=1))
            gnorm_l.append(jnp.sum(nm[:, 2, :], axis=0))
        small["dmod"].insert(0, jnp.stack(dmod_l, axis=1))
        small["g_norm"].insert(0, jnp.stack(gnorm_l))
        ls = mg["lru_sums"]
        small["b_rgate"].insert(0, ls[0])
        small["b_igate"].insert(0, ls[1])
        small["lru_lambda"].insert(0, ls[2] * (-_sigmoid(-wts["lru_lambda"][l])))
        small["conv_b"].insert(0, ls[3])
        small["conv_w"].insert(0, ls[4:8])
        small["w_rgate"].insert(0, _block_diag_grad(mg["dwr"]))
        small["w_igate"].insert(0, _block_diag_grad(mg["dwi"]))
        small["g_mix_out"].insert(0, jnp.sum(mg["gmix_parts"][:, 0, :], axis=0))
        gqk = jnp.sum(mg["gqk_parts"][:, :2, :], axis=0).reshape(2, ATT_W // HEAD_DIM, HEAD_DIM).sum(axis=1)
        small["g_qk"].insert(0, gqk)
        small["b_fgate"].insert(0, jnp.sum(mg["bf_parts"][:, 0, :4], axis=0))
    big = {k: jnp.stack(v) for k, v in big.items()}
    small = {k: jnp.stack(v) for k, v in small.items()}
    return loss, dx.reshape(batch, seq, d), big, small


def _row_tile(rows, row_bytes):
    for t in (512, 256, 128, 64, 32, 16):
        if rows % t == 0 and t * row_bytes <= 4 * 1024 * 1024:
            return t
    return rows


def _adamw(parts, w, m, v, name):
    groups, n_parts, rows, cols = parts.shape
    tr = _row_tile(rows, cols * (n_parts * parts.dtype.itemsize + 7 * 4))
    c1 = 1.0 - ADAM_B1 ** ADAM_STEP
    c2 = 1.0 - ADAM_B2 ** ADAM_STEP

    def body(p_ref, w_ref, m_ref, v_ref, g_out, d_out, m_out, v_out):
        g = p_ref[0].astype(F32)
        for n in range(1, n_parts):
            g = g + p_ref[n].astype(F32)
        m_new = ADAM_B1 * m_ref[...] + (1.0 - ADAM_B1) * g
        v_new = ADAM_B2 * v_ref[...] + (1.0 - ADAM_B2) * (g * g)
        g_out[...] = g
        d_out[...] = -ADAM_LR * ((m_new / c1) / (jnp.sqrt(v_new / c2) + ADAM_EPS) + ADAM_WD * w_ref[...])
        m_out[...] = m_new
        v_out[...] = v_new

    tile = pl.BlockSpec((None, tr, cols), lambda g, i: (g, i, 0))
    return pl.pallas_call(
        body, name=name, grid=(groups, rows // tr),
        in_specs=[pl.BlockSpec((None, n_parts, tr, cols), lambda g, i: (g, 0, i, 0)), tile, tile, tile],
        out_specs=[tile] * 4, out_shape=[jax.ShapeDtypeStruct((groups, rows, cols), F32)] * 4,
        compiler_params=_params(2),
    )(parts, w, m, v)


def _sum_parts(parts):
    n_parts, rows, cols = parts.shape

    def body(p_ref, o_ref):
        acc = p_ref[0]
        for n in range(1, n_parts):
            acc = acc + p_ref[n]
        o_ref[...] = acc

    return pl.pallas_call(body, name="sum_small", out_shape=jax.ShapeDtypeStruct((rows, cols), F32),
                          compiler_params=pltpu.CompilerParams(vmem_limit_bytes=VMEM_LIMIT_BYTES))(parts)


def _flatten(arrays, multiple):
    flat = jnp.concatenate([a.reshape(-1).astype(F32) for a in arrays])
    pad = (-flat.shape[0]) % multiple
    return jnp.pad(flat, (0, pad)).reshape(-1, LANES)


def _unflatten(flat2d, shapes):
    flat, out, off = flat2d.reshape(-1), [], 0
    for s in shapes:
        n = math.prod(s)
        out.append(flat[off:off + n].reshape(s))
        off += n
    return out


SMALL_NAMES = ("b_ada", "g_norm", "b_fgate", "conv_w", "conv_b", "w_rgate", "b_rgate", "w_igate", "b_igate",
               "lru_lambda", "g_qk", "g_mix_out")
WEIGHT_NAMES = ("w_ada", "b_ada", "g_norm", "w_ffn_up", "w_ffn_down", "w_in", "b_fgate", "conv_w", "conv_b",
                "w_rgate", "b_rgate", "w_igate", "b_igate", "lru_lambda", "g_qk", "g_mix_out", "w_out")


def kernel(x, c, w_ada, b_ada, g_norm, w_ffn_up, w_ffn_down, w_in, b_fgate, conv_w, conv_b, w_rgate, b_rgate, w_igate, b_igate, lru_lambda, g_qk, g_mix_out, w_out, loss_target, m_w_ada, m_b_ada, m_g_norm, m_w_ffn_up, m_w_ffn_down, m_w_in, m_b_fgate, m_conv_w, m_conv_b, m_w_rgate, m_b_rgate, m_w_igate, m_b_igate, m_lru_lambda, m_g_qk, m_g_mix_out, m_w_out, v_w_ada, v_b_ada, v_g_norm, v_w_ffn_up, v_w_ffn_down, v_w_in, v_b_fgate, v_conv_w, v_conv_b, v_w_rgate, v_b_rgate, v_w_igate, v_b_igate, v_lru_lambda, v_g_qk, v_g_mix_out, v_w_out):
    batch, seq, d = x.shape
    n_layers = w_ada.shape[0]
    me = 4 * lax.axis_index("x") + 2 * lax.axis_index("y") + lax.axis_index("c")
    weights = dict(w_ada=w_ada, b_ada=b_ada, g_norm=g_norm, w_ffn_up=w_ffn_up, w_ffn_down=w_ffn_down, w_in=w_in,
                   b_fgate=b_fgate, conv_w=conv_w, conv_b=conv_b, w_rgate=w_rgate, b_rgate=b_rgate, w_igate=w_igate,
                   b_igate=b_igate, lru_lambda=lru_lambda, g_qk=g_qk, g_mix_out=g_mix_out, w_out=w_out)
    moments_m = dict(w_ada=m_w_ada, b_ada=m_b_ada, g_norm=m_g_norm, w_ffn_up=m_w_ffn_up, w_ffn_down=m_w_ffn_down,
                     w_in=m_w_in, b_fgate=m_b_fgate, conv_w=m_conv_w, conv_b=m_conv_b, w_rgate=m_w_rgate,
                     b_rgate=m_b_rgate, w_igate=m_w_igate, b_igate=m_b_igate, lru_lambda=m_lru_lambda, g_qk=m_g_qk,
                     g_mix_out=m_g_mix_out, w_out=m_w_out)
    moments_v = dict(w_ada=v_w_ada, b_ada=v_b_ada, g_norm=v_g_norm, w_ffn_up=v_w_ffn_up, w_ffn_down=v_w_ffn_down,
                     w_in=v_w_in, b_fgate=v_b_fgate, conv_w=v_conv_w, conv_b=v_conv_b, w_rgate=v_w_rgate,
                     b_rgate=v_b_rgate, w_igate=v_w_igate, b_igate=v_b_igate, lru_lambda=v_lru_lambda, g_qk=v_g_qk,
                     g_mix_out=v_g_mix_out, w_out=v_w_out)

    w_in_pad = jnp.pad(w_in, ((0, 0), (0, 0), (0, N_IN_PAD - N_IN))).astype(BF16)
    c_all, gn_all, cw_all, w_up_full, down_all, in_all, out_all = _exchange(
        [(c, 0), (g_norm, 0), (conv_w, 0), (w_ffn_up.astype(BF16), 2), (w_ffn_down.astype(BF16), 2), (w_in_pad, 1),
         (w_out.astype(BF16), 1)], [], "gather_weights", two_level=True)
    c_all = c_all.reshape(N_DEV * batch, d)
    n_ada = w_ada.shape[-1]
    g_norm_full = gn_all.transpose(1, 2, 0, 3).reshape(n_layers, 3, d)
    conv_w_full = cw_all.transpose(1, 2, 0, 3).reshape(n_layers, 4, LRU_W)
    w_down_full = down_all.reshape(n_layers, 2, D_FF, d)
    w_in_full = in_all.reshape(n_layers, d, N_IN_PAD)
    w_out_full = out_all.reshape(n_layers, d, d)

    b_ada_loc = lax.dynamic_slice_in_dim(b_ada, me * n_ada, n_ada, axis=1)
    silu = lambda t: t * _sigmoid(t)

    def bias_epilogue(p, e_refs, o_refs):
        o_refs[0][...] = p + e_refs[0][...]

    mod_loc = []
    for l in range(n_layers):
        (ml,) = _mm(c_all, w_ada, mode="nn", tm=c_all.shape[0], tn=n_ada, tk=d, b_lead=(l,), a_pre=silu,
                    name=f"ada_{l}", extras=[(b_ada_loc[l][None, :], (1, n_ada), lambda i, j: (0, 0))],
                    outs=[((c_all.shape[0], n_ada), F32, (c_all.shape[0], n_ada), lambda i, j: (0, 0))],
                    epilogue=bias_epilogue)
        mod_loc.append(ml)
    (mod_all,) = _exchange([(jnp.stack(mod_loc), 0)], [], "gather_mod")
    mod_all = mod_all.transpose(1, 2, 0, 3).reshape(n_layers, N_DEV * batch, 9 * d)
    mod_me = lax.dynamic_slice_in_dim(mod_all, me * batch, batch, axis=1).reshape(n_layers, batch, 3, 3, d)

    wts = dict(w_up=w_up_full, w_down=w_down_full, w_in=w_in_full, w_out=w_out_full, g_norm=g_norm_full,
               conv_w=conv_w_full, conv_b=conv_b, w_rgate=w_rgate, b_rgate=b_rgate, w_igate=w_igate, b_igate=b_igate,
               lru_lambda=lru_lambda, g_qk=g_qk, g_mix_out=g_mix_out, b_fgate=b_fgate)
    loss_part, grad_x, big, small = _local_step(x, loss_target, mod_me, wts)

    dmod_me = small.pop("dmod").reshape(n_layers, batch, 9 * d)
    small["b_ada"] = jnp.sum(dmod_me, axis=1)
    small_shapes = [(1,)] + [weights[k].shape if k not in ("g_norm", "conv_w") else small[k].shape for k in SMALL_NAMES]
    small_flat = _flatten([loss_part.reshape(1)] + [small[k] for k in SMALL_NAMES], 16 * LANES)
    g_down = big["w_down"].reshape(n_layers, 2, N_DEV, -1, d)
    g_in = big["w_in"].reshape(n_layers, N_DEV, -1, N_IN_PAD)
    g_out = big["w_out"].reshape(n_layers, N_DEV, -1, d)
    dmod_all, small_all, p_up, p_down, p_in, p_out = _exchange(
        [(dmod_me, 0), (small_flat, 0)], [(big["w_up"], 2), (g_down, 2), (g_in, 1), (g_out, 1)], "exchange_grads")
    small_sum = _unflatten(_sum_parts(small_all), small_shapes)
    loss = small_sum[0].reshape(())
    small_grads = dict(zip(SMALL_NAMES, small_sum[1:]))
    small_grads["g_norm"] = lax.dynamic_slice_in_dim(small_grads["g_norm"], me * g_norm.shape[-1], g_norm.shape[-1], 2)
    small_grads["conv_w"] = lax.dynamic_slice_in_dim(small_grads["conv_w"], me * conv_w.shape[-1], conv_w.shape[-1], 2)

    dmod_all = dmod_all.transpose(1, 0, 2, 3).reshape(n_layers, N_DEV * batch, 9 * d)
    dmod_loc = lax.dynamic_slice_in_dim(dmod_all, me * n_ada, n_ada, axis=2)
    g_ada = []
    for l in range(n_layers):
        (gl,) = _mm(c_all, dmod_loc[l], mode="tn", tm=d, tn=n_ada, tk=c_all.shape[0], a_pre=silu, name=f"dw_ada_{l}",
                    outs=[((d, n_ada), F32, (d, n_ada), lambda i, j: (0, 0))], epilogue=_store_epilogue([F32]))
        g_ada.append(gl)
    g_ada = jnp.stack(g_ada)

    results = {}

    def update(name, parts):
        shape = weights[name].shape
        as3d = lambda t: t.reshape((-1,) + shape[-2:])
        outs = _adamw(parts.reshape((-1,) + parts.shape[-3:]), as3d(weights[name]), as3d(moments_m[name]),
                      as3d(moments_v[name]), f"adamw_{name}")
        results[name] = [o.reshape(shape) for o in outs]

    update("w_ada", g_ada[:, None])
    update("w_ffn_up", p_up)
    update("w_ffn_down", p_down)
    update("w_in", p_in[..., :N_IN])
    update("w_out", p_out)
    sm_shapes = [weights[k].shape for k in SMALL_NAMES]
    flat = lambda src: _flatten([src[k] for k in SMALL_NAMES], 16 * LANES)
    sm_out = _adamw(flat(small_grads)[None, None], flat(weights)[None], flat(moments_m)[None], flat(moments_v)[None],
                    "adamw_small")
    for k, vals in zip(SMALL_NAMES, zip(*[_unflatten(o, sm_shapes) for o in sm_out])):
        results[k] = list(vals)

    outs = [loss, grad_x]
    for n in range(4):
        outs += [results[k][n] for k in WEIGHT_NAMES]
    return tuple(outs)
```

```python
import functools
import math

import jax
import jax.numpy as jnp
from jax import lax
from jax.experimental import pallas as pl
from jax.experimental.pallas import tpu as pltpu

F32 = jnp.float32
BF16 = jnp.bfloat16

N_DEV = 8
D_MODEL = 1024
D_FF = 2816
FF_SHARD = 2 * D_FF // N_DEV
N_FF_SHARD = D_FF // FF_SHARD
HEAD_DIM = 64
LRU_W = 512
ATT_W = 256
N_IN = 2564
N_IN_PAD = 2688
LANES = 128
SUBLANES = 8
BLK = 256
TQ = 512
KB_PER_Q = TQ // BLK
EPS = 1e-6
LRU_C = 8.0
NEG_BIG = -1e30
VMEM_LIMIT_BYTES = 48 * 1024 * 1024

ADAM_LR, ADAM_B1, ADAM_B2, ADAM_EPS, ADAM_WD, ADAM_STEP = 0.001, 0.9, 0.999, 1e-08, 0.01, 10

COL_SBQ, COL_SBK, COL_SBV = 8, 10, 12
COL_FXV, COL_FXF = 18, 20

NN = (((1,), (0,)), ((), ()))
NT = (((1,), (1,)), ((), ()))
TN = (((0,), (0,)), ((), ()))


def _params(n_axes):
    return pltpu.CompilerParams(dimension_semantics=("arbitrary",) * n_axes, vmem_limit_bytes=VMEM_LIMIT_BYTES)


def _tok_tile(seq):
    for t in (512, 256, 128):
        if seq % t == 0:
            return t
    raise ValueError(f"sequence length {seq} is not a multiple of 128")


def _dot(a, b, dims=NN):
    return lax.dot_general(a, b, dims, preferred_element_type=F32)


def _sigmoid(x):
    return 1.0 / (1.0 + jnp.exp(-x))


def _softplus(x):
    return jnp.maximum(x, 0.0) + jnp.log(1.0 + jnp.exp(-jnp.abs(x)))


def _gelu_parts(x):
    k0, k1 = math.sqrt(2.0 / math.pi), 0.044715
    t = jnp.tanh(k0 * (x + k1 * x * x * x))
    gelu = 0.5 * x * (1.0 + t)
    dgelu = 0.5 * (1.0 + t) + 0.5 * x * (1.0 - t * t) * k0 * (1.0 + 3.0 * k1 * x * x)
    return gelu, dgelu


def _neg_expm1(x):
    series = -x * (1.0 + x * (0.5 + x * (1.0 / 6.0 + x * (1.0 / 24.0 + x * (1.0 / 120.0 + x * (1.0 / 720.0))))))
    return jnp.where(x > -0.25, series, 1.0 - jnp.exp(x))


def _split2(x):
    hi = x.astype(BF16)
    lo = (x - hi.astype(F32)).astype(BF16)
    return hi, lo


def _split3(x):
    hi = x.astype(BF16)
    r = x - hi.astype(F32)
    mid = r.astype(BF16)
    lo = (r - mid.astype(F32)).astype(BF16)
    return hi, mid, lo


def _rows_to_block(rows, width):
    r = lax.broadcasted_iota(jnp.int32, (SUBLANES, width), 0)
    out = jnp.zeros((SUBLANES, width), F32)
    for n, v in enumerate(rows):
        out = jnp.where(r == n, jnp.broadcast_to(v, (SUBLANES, width)), out)
    return out


def _colsum(x):
    return jnp.sum(x, axis=0, keepdims=True)


def _exchange(gathers, scatters, name, two_level=False):
    assert not (two_level and scatters)
    n_g = len(gathers)
    ops = [a for a, _ in gathers] + [a for a, _ in scatters]
    n = len(ops)
    out_shape = [jax.ShapeDtypeStruct(a.shape[:nl] + (N_DEV,) + a.shape[nl:], a.dtype) for a, nl in gathers]
    out_shape += [jax.ShapeDtypeStruct(a.shape, a.dtype) for a, _ in scatters]
    items = []
    for k, (a, nl) in enumerate(list(gathers) + list(scatters)):
        for flat in range(math.prod(a.shape[:nl])):
            idx, rem = [], flat
            for dim in reversed(a.shape[:nl]):
                idx.insert(0, rem % dim)
                rem //= dim
            items.append((k, tuple(idx)))
    n_items = len(items)

    def body(*refs):
        ins, outs = refs[:n], refs[n:2 * n]
        send_sems, recv_sems, local_sems = refs[2 * n:]
        x, y, c = lax.axis_index("x"), lax.axis_index("y"), lax.axis_index("c")
        me = 4 * x + 2 * y + c

        def at(ref, idx):
            return ref.at[idx] if idx else ref

        def src(it, peer):
            k, idx = items[it]
            return at(ins[k], idx) if k < n_g else at(ins[k], idx + (peer,))

        def slot(it, s):
            k, idx = items[it]
            return at(outs[k], idx + (s,))

        def remote(it, rel, source, s, to):
            return pltpu.make_async_remote_copy(
                src_ref=source, dst_ref=slot(it, s), send_sem=send_sems.at[it, rel], recv_sem=recv_sems.at[it, rel],
                device_id=to, device_id_type=pl.DeviceIdType.MESH)

        local = [pltpu.make_async_copy(src(it, me), slot(it, me), local_sems.at[it]) for it in range(n_items)]
        for cp in local:
            cp.start()

        if not two_level:
            started = []
            for r in range(1, N_DEV):
                px = 1 - x if (r >> 2) & 1 else x
                py = 1 - y if (r >> 1) & 1 else y
                pc = 1 - c if r & 1 else c
                for it in range(n_items):
                    cp = remote(it, r - 1, src(it, 4 * px + 2 * py + pc), me, (px, py, pc))
                    cp.start()
                    started.append(cp)
            for cp in started:
                cp.wait()
        else:
            sibling, chips = (x, y, 1 - c), [(1 - x, y), (x, 1 - y), (1 - x, 1 - y)]
            sib = 4 * x + 2 * y + (1 - c)
            started = []
            for it in range(n_items):
                started.append(remote(it, 0, src(it, me), me, sibling))
                started += [remote(it, 1 + j, src(it, me), me, (cx, cy, c)) for j, (cx, cy) in enumerate(chips)]
            for cp in started:
                cp.start()
            for j, (cx, cy) in enumerate(chips):
                s = 4 * cx + 2 * cy + c
                for it in range(n_items):
                    remote(it, 1 + j, slot(it, s), s, sibling).wait_recv()
                    cp = remote(it, 4 + j, slot(it, s), s, sibling)
                    cp.start()
                    started.append(cp)
            for it in range(n_items):
                remote(it, 0, slot(it, sib), sib, sibling).wait_recv()
                for j, (cx, cy) in enumerate(chips):
                    s = 4 * cx + 2 * cy + (1 - c)
                    remote(it, 4 + j, slot(it, s), s, sibling).wait_recv()
            for cp in started:
                cp.wait_send()
        for cp in local:
            cp.wait()

    hbm = pl.BlockSpec(memory_space=pltpu.HBM)
    return pl.pallas_call(
        body, name=name, out_shape=out_shape,
        in_specs=[hbm] * n, out_specs=[hbm] * n,
        scratch_shapes=[pltpu.SemaphoreType.DMA((n_items, N_DEV - 1)), pltpu.SemaphoreType.DMA((n_items, N_DEV - 1)),
                        pltpu.SemaphoreType.DMA((n_items,))],
    )(*ops)


def _mm(a, b, *, mode, tm, tn, tk, outs, epilogue, name, extras=(), a_lead=(), b_lead=(), a_pre=None,
        a_spec=None, b_spec=None, shape=None, ksub=1):
    if shape is not None:
        mdim, ndim, kdim = shape
    else:
        if mode == "tn":
            kdim, mdim = a.shape[-2:]
        else:
            mdim, kdim = a.shape[-2:]
        ndim = b.shape[-2] if mode == "nt" else b.shape[-1]
    assert mdim % tm == 0 and ndim % tn == 0 and kdim % tk == 0, (name, mdim, ndim, kdim, tm, tn, tk)
    ni, nj, nk = mdim // tm, ndim // tn, kdim // tk
    a_lead, b_lead = tuple(a_lead), tuple(b_lead)
    a_block = (None,) * len(a_lead) + ((tk, tm) if mode == "tn" else (tm, tk))
    b_block = (None,) * len(b_lead) + ((tn, tk) if mode == "nt" else (tk, tn))
    dims = {"nn": NN, "nt": NT, "tn": TN}[mode]
    ne, no = len(extras), len(outs)

    def a_index(i, j, k):
        return a_lead + ((k, i) if mode == "tn" else (i, k))

    def b_index(i, j, k):
        return b_lead + ((j, k) if mode == "nt" else (k, j))

    if a_spec is not None:
        a_block, a_index = a_spec
    if b_spec is not None:
        b_block, b_index = b_spec

    def body(*refs):
        a_ref, b_ref = refs[0], refs[1]
        e_refs, o_refs = refs[2:2 + ne], refs[2 + ne:2 + ne + no]
        if ksub == 1:
            av = a_ref[...] if a_pre is None else a_pre(a_ref[...])
            p = _dot(av.astype(BF16), b_ref[...].astype(BF16), dims)
        else:
            p = _dot(a_ref[0], b_ref[0], dims)
            for s in range(1, ksub):
                p = p + _dot(a_ref[s], b_ref[s], dims)
        if nk == 1:
            epilogue(p, e_refs, o_refs)
        else:
            acc = refs[-1]
            k = pl.program_id(2)

            @pl.when(k == 0)
            def _():
                acc[...] = p

            @pl.when(k > 0)
            def _():
                acc[...] += p

            @pl.when(k == nk - 1)
            def _():
                epilogue(acc[...], e_refs, o_refs)

    in_specs = [pl.BlockSpec(a_block, a_index), pl.BlockSpec(b_block, b_index)]
    in_specs += [pl.BlockSpec(blk, functools.partial(lambda i, j, k, f: f(i, j), f=f)) for _, blk, f in extras]
    out_specs = [pl.BlockSpec(blk, functools.partial(lambda i, j, k, f: f(i, j), f=f)) for _, _, blk, f in outs]
    res = pl.pallas_call(
        body, name=name, grid=(ni, nj, nk), in_specs=in_specs, out_specs=out_specs,
        out_shape=[jax.ShapeDtypeStruct(s, d) for s, d, _, _ in outs],
        scratch_shapes=[pltpu.VMEM((tm, tn), F32)] if nk > 1 else [],
        compiler_params=_params(3),
    )(a, b, *[e[0] for e in extras])
    return res


def _store_epilogue(dtypes):
    def epi(p, e_refs, o_refs):
        for o, dt in zip(o_refs, dtypes):
            o[...] = p.astype(dt)
    return epi


def _normmod(x, gn, scale, shift, seq, name):
    m, d = x.shape
    tm = _tok_tile(seq)
    tpb = seq // tm

    def body(x_ref, gn_ref, sc_ref, sh_ref, h_ref):
        xv = x_ref[...]
        rstd = lax.rsqrt(jnp.mean(xv * xv, axis=-1, keepdims=True) + EPS)
        h_ref[...] = (xv * rstd * gn_ref[...] * (1.0 + sc_ref[0]) + sh_ref[0]).astype(BF16)

    vec = pl.BlockSpec((1, 1, d), lambda i: (i // tpb, 0, 0))
    return pl.pallas_call(
        body, name=name, grid=(m // tm,),
        in_specs=[pl.BlockSpec((tm, d), lambda i: (i, 0)), pl.BlockSpec((1, d), lambda i: (0, 0)), vec, vec],
        out_specs=pl.BlockSpec((tm, d), lambda i: (i, 0)),
        out_shape=jax.ShapeDtypeStruct((m, d), BF16), compiler_params=_params(1),
    )(x, gn, scale, shift)


def _normmod_bwd_epilogue(p, e_refs, o_refs):
    x_ref, dxo_ref, gn_ref, sc_ref = e_refs
    xv = x_ref[...]
    rstd = lax.rsqrt(jnp.mean(xv * xv, axis=-1, keepdims=True) + EPS)
    xhat = xv * rstd
    gn, sc1 = gn_ref[...], 1.0 + sc_ref[0]
    dxhat = p * (gn * sc1)
    dx = rstd * (dxhat - xhat * jnp.mean(dxhat * xhat, axis=-1, keepdims=True))
    o_refs[0][...] = dxo_ref[...] + dx
    t = p * xhat
    o_refs[1][0] = _rows_to_block([_colsum(p), _colsum(t * gn), _colsum(t * sc1)], p.shape[1])


def _residual_bwd(dx, f, gate, fac, seq, name):
    m, d = dx.shape
    tm = _tok_tile(seq)
    tpb = seq // tm

    def body(dx_ref, f_ref, g_ref, df_ref, dg_ref):
        dxv = dx_ref[...]
        df_ref[...] = ((fac * (1.0 + g_ref[0])) * dxv).astype(BF16)
        dg_ref[0] = _rows_to_block([_colsum((fac * dxv) * f_ref[...].astype(F32))], d)

    tile = pl.BlockSpec((tm, d), lambda i: (i, 0))
    return pl.pallas_call(
        body, name=name, grid=(m // tm,),
        in_specs=[tile, tile, pl.BlockSpec((1, 1, d), lambda i: (i // tpb, 0, 0))],
        out_specs=[tile, pl.BlockSpec((1, SUBLANES, d), lambda i: (i, 0, 0))],
        out_shape=[jax.ShapeDtypeStruct((m, d), BF16), jax.ShapeDtypeStruct((m // tm, SUBLANES, d), F32)],
        compiler_params=_params(1),
    )(dx, f, gate)


def _loss_head(y, target, seq):
    m, d = y.shape
    tm = _tok_tile(seq)

    def body(y_ref, t_ref, dy_ref, l_ref):
        err = y_ref[...] - t_ref[...]
        dy_ref[...] = err * (1.0 / d)
        part = 0.5 * jnp.sum(jnp.mean(err * err, axis=-1, keepdims=True), axis=0, keepdims=True)
        l_ref[0] = jnp.broadcast_to(part, (SUBLANES, LANES))

    tile = pl.BlockSpec((tm, d), lambda i: (i, 0))
    return pl.pallas_call(
        body, name="loss_head", grid=(m // tm,), in_specs=[tile, tile],
        out_specs=[tile, pl.BlockSpec((1, SUBLANES, LANES), lambda i: (i, 0, 0))],
        out_shape=[jax.ShapeDtypeStruct((m, d), F32), jax.ShapeDtypeStruct((m // tm, SUBLANES, LANES), F32)],
        compiler_params=_params(1),
    )(y, target)


def _ffn_fwd(x, h, wup, wdown, lead, gate, seq, tag):
    m, d = x.shape
    tm = _tok_tile(seq)
    tpb = seq // tm
    l, f_idx = lead

    def up_body(h_ref, wg_ref, wu_ref, a_ref, gu_ref):
        hv = h_ref[...]
        g, u = _dot(hv, wg_ref[...]), _dot(hv, wu_ref[...])
        a_ref[...] = (g * _sigmoid(g) * u).astype(BF16)
        gu_ref[0] = g.astype(BF16)
        gu_ref[1] = u.astype(BF16)

    wblk = (None, None, None, d, FF_SHARD)
    a, gu = pl.pallas_call(
        up_body, name=f"ffn_up_{tag}", grid=(N_FF_SHARD, m // tm),
        in_specs=[pl.BlockSpec((tm, d), lambda j, i: (i, 0)),
                  pl.BlockSpec(wblk, lambda j, i: (l, f_idx, j, 0, 0)),
                  pl.BlockSpec(wblk, lambda j, i: (l, f_idx, j + N_FF_SHARD, 0, 0))],
        out_specs=[pl.BlockSpec((None, tm, FF_SHARD), lambda j, i: (j, i, 0)),
                   pl.BlockSpec((2, None, tm, FF_SHARD), lambda j, i: (0, j, i, 0))],
        out_shape=[jax.ShapeDtypeStruct((N_FF_SHARD, m, FF_SHARD), BF16),
                   jax.ShapeDtypeStruct((2, N_FF_SHARD, m, FF_SHARD), BF16)],
        compiler_params=_params(2),
    )(h, wup, wup)

    def down_epilogue(p, e_refs, o_refs):
        x_ref, g_ref = e_refs
        o_refs[0][...] = x_ref[...] + (0.5 * (1.0 + g_ref[0])) * p
        o_refs[1][...] = p.astype(BF16)

    wdown5 = wdown.reshape(wdown.shape[:2] + (N_FF_SHARD, FF_SHARD, d))
    x_out, f = _mm(a, wdown5, mode="nn", tm=tm, tn=d, tk=D_FF, ksub=N_FF_SHARD, name=f"ffn_down_{tag}",
                   shape=(m, d, D_FF), a_spec=((N_FF_SHARD, tm, FF_SHARD), lambda i, j, k: (0, i, 0)),
                   b_spec=((None, None, N_FF_SHARD, FF_SHARD, d), lambda i, j, k: (l, f_idx, 0, 0, 0)),
                   extras=[(x, (tm, d), lambda i, j: (i, 0)), (gate, (1, 1, d), lambda i, j: (i // tpb, 0, 0))],
                   outs=[((m, d), F32, (tm, d), lambda i, j: (i, 0)), ((m, d), BF16, (tm, d), lambda i, j: (i, 0))],
                   epilogue=down_epilogue)
    return x_out, (a, gu, f)


def _ffn_bwd(dx_out, x, h, saved, wup, wdown, lead, gn, scale, gate, seq, tag):
    a, gu, f = saved
    m, d = x.shape
    tm = _tok_tile(seq)
    tpb = seq // tm
    l, f_idx = lead
    df, dgate_parts = _residual_bwd(dx_out, f, gate, 0.5, seq, f"ffn_res_bwd_{tag}")

    def act_bwd_epilogue(p, e_refs, o_refs):
        g, u = e_refs[0][0].astype(F32), e_refs[0][1].astype(F32)
        sg = _sigmoid(g)
        o_refs[0][0] = (p * u * (sg * (1.0 + g * (1.0 - sg)))).astype(BF16)
        o_refs[0][1] = (p * (g * sg)).astype(BF16)

    gu_blk = (2, None, tm, FF_SHARD)
    (dgu,) = _mm(df, wdown, mode="nt", tm=tm, tn=FF_SHARD, tk=d, name=f"ffn_down_dx_{tag}", shape=(m, D_FF, d),
                 b_spec=((None, None, FF_SHARD, d), lambda i, j, k: (l, f_idx, j, 0)),
                 extras=[(gu, gu_blk, lambda i, j: (0, j, i, 0))],
                 outs=[((2, N_FF_SHARD, m, FF_SHARD), BF16, gu_blk, lambda i, j: (0, j, i, 0))],
                 epilogue=act_bwd_epilogue)
    tt = 2 * tm if m % (2 * tm) == 0 else tm
    (dwdown,) = _mm(a, df, mode="tn", tm=FF_SHARD, tn=d, tk=tt, name=f"ffn_dwdown_{tag}", shape=(D_FF, d, m),
                    a_spec=((None, tt, FF_SHARD), lambda i, j, k: (i, k, 0)),
                    outs=[((D_FF, d), BF16, (FF_SHARD, d), lambda i, j: (i, 0))], epilogue=_store_epilogue([BF16]))
    dgu8 = dgu.reshape(2 * N_FF_SHARD, m, FF_SHARD)
    (dwup,) = _mm(h, dgu8, mode="tn", tm=d, tn=FF_SHARD, tk=tt, name=f"ffn_dwup_{tag}", shape=(d, 2 * D_FF, m),
                  b_spec=((None, tt, FF_SHARD), lambda i, j, k: (j, k, 0)),
                  outs=[((2 * N_FF_SHARD, d, FF_SHARD), BF16, (None, d, FF_SHARD), lambda i, j: (j, 0, 0))],
                  epilogue=_store_epilogue([BF16]))
    dx, nm_parts = _mm(dgu8, wup, mode="nt", tm=tm, tn=d, tk=D_FF, ksub=N_FF_SHARD, name=f"ffn_up_dx_{tag}",
                       shape=(m, d, 2 * D_FF), a_spec=((N_FF_SHARD, tm, FF_SHARD), lambda i, j, k: (k, i, 0)),
                       b_spec=((None, None, N_FF_SHARD, d, FF_SHARD), lambda i, j, k: (l, f_idx, k, 0, 0)),
                       extras=[(x, (tm, d), lambda i, j: (i, 0)), (dx_out, (tm, d), lambda i, j: (i, 0)),
                               (gn, (1, d), lambda i, j: (0, 0)), (scale, (1, 1, d), lambda i, j: (i // tpb, 0, 0))],
                       outs=[((m, d), F32, (tm, d), lambda i, j: (i, 0)),
                             ((m // tm, SUBLANES, d), F32, (1, SUBLANES, d), lambda i, j: (i, 0, 0))],
                       epilogue=_normmod_bwd_epilogue)
    return dx, dwup, dwdown, nm_parts, dgate_parts


def _shift_down(ext, n, rows):
    if n:
        ext = pltpu.roll(ext, n, 0)
    return ext[SUBLANES:SUBLANES + rows]


def _lru_gates(u, wr_ref, br_ref, wi_ref, bi_ref, lam_ref):
    ub = u.astype(BF16)
    r = _sigmoid(_dot(ub, wr_ref[...]) + br_ref[...])
    ig = _sigmoid(_dot(ub, wi_ref[...]) + bi_ref[...])
    sp = _softplus(-lam_ref[...])
    log_a = (-LRU_C * r) * sp
    a = jnp.exp(log_a)
    mult = jnp.sqrt(_neg_expm1(2.0 * log_a))
    return r, ig, sp, a, mult


def _conv(ext, cw_ref, cb_ref, rows):
    u = cb_ref[...] + cw_ref[3:4, :] * _shift_down(ext, 0, rows)
    for k in range(3):
        u = u + cw_ref[k:k + 1, :] * _shift_down(ext, 3 - k, rows)
    return u


def _lru_halo_spec(seq, ts):
    return pl.BlockSpec((SUBLANES, LRU_W),
                        lambda b, i: (jnp.maximum(b * (seq // SUBLANES) + i * (ts // SUBLANES) - 1, 0), 0))


def _lru_fwd(proj32, conv_w, conv_b, wr, br, wi, bi, lam, batch, seq):
    m = proj32.shape[0]
    ts = _tok_tile(seq)
    nt = seq // ts
    row = lambda b, i: (b * nt + i, 0)

    def body(x_ref, halo_ref, g_ref, cw_ref, cb_ref, wr_ref, br_ref, wi_ref, bi_ref, lam_ref,
             y_ref, h_ref, a_scr, b_scr, carry):
        i = pl.program_id(1)
        halo = jnp.where(i > 0, halo_ref[...], 0.0)
        ext = jnp.concatenate([halo, x_ref[...]], axis=0)
        u = _conv(ext, cw_ref, cb_ref, ts)
        _, ig, _, a, mult = _lru_gates(u, wr_ref, br_ref, wi_ref, bi_ref, lam_ref)
        a_scr[...] = a
        b_scr[...] = mult * (ig * u)

        @pl.when(i == 0)
        def _():
            carry[...] = jnp.zeros_like(carry)

        rid = lax.broadcasted_iota(jnp.int32, (SUBLANES, LRU_W), 0)

        def chunk(c, hprev):
            off = pl.multiple_of(c * SUBLANES, SUBLANES)
            av, bv = a_scr[pl.ds(off, SUBLANES), :], b_scr[pl.ds(off, SUBLANES), :]
            for d in (1, 2, 4):
                keep = rid >= d
                bv = jnp.where(keep, av * pltpu.roll(bv, d, 0) + bv, bv)
                av = jnp.where(keep, av * pltpu.roll(av, d, 0), av)
            h = av * hprev + bv
            h_ref[pl.ds(off, SUBLANES), :] = h
            return h[SUBLANES - 1:SUBLANES, :]

        carry[...] = lax.fori_loop(0, ts // SUBLANES, chunk, carry[...])
        gelu, _ = _gelu_parts(g_ref[...])
        y_ref[...] = h_ref[...] * gelu

    full = lambda shape: pl.BlockSpec(shape, lambda b, i: (0,) * len(shape))
    return pl.pallas_call(
        body, name="lru_fwd", grid=(batch, nt),
        in_specs=[pl.BlockSpec((ts, LRU_W), row), _lru_halo_spec(seq, ts),
                  pl.BlockSpec((ts, LRU_W), lambda b, i: (b * nt + i, 1)),
                  full((4, LRU_W)), full((1, LRU_W)), full((LRU_W, LRU_W)), full((1, LRU_W)),
                  full((LRU_W, LRU_W)), full((1, LRU_W)), full((1, LRU_W))],
        out_specs=[pl.BlockSpec((ts, LRU_W), row), pl.BlockSpec((ts, LRU_W), row)],
        out_shape=[jax.ShapeDtypeStruct((m, LRU_W), F32), jax.ShapeDtypeStruct((m, LRU_W), F32)],
        scratch_shapes=[pltpu.VMEM((ts, LRU_W), F32), pltpu.VMEM((ts, LRU_W), F32), pltpu.VMEM((1, LRU_W), F32)],
        compiler_params=_params(2),
    )(proj32, proj32, proj32, conv_w, conv_b, wr, br, wi, bi, lam)


def _lru_bwd(dy, proj32, h, conv_w, conv_b, wr, br, wi, bi, lam, batch, seq):
    m = proj32.shape[0]
    ts = _tok_tile(seq)
    nt = seq // ts
    row = lambda b, i: (b * nt + (nt - 1 - i), 0)
    halo = pl.BlockSpec((SUBLANES, LRU_W),
                        lambda b, i: (jnp.maximum(b * (seq // SUBLANES) + (nt - 1 - i) * (ts // SUBLANES) - 1, 0), 0))

    def body(dy_ref, x_ref, xhalo_ref, g_ref, h_ref, hhalo_ref, cw_ref, cb_ref, wr_ref, br_ref, wi_ref, bi_ref,
             lam_ref, dx_ref, dg_ref, dwr_ref, dwi_ref, sums_ref, a_scr, dh_scr, g_scr, carry, du_next):
        b, i = pl.program_id(0), pl.program_id(1)
        first_tile = i == nt - 1

        @pl.when((b == 0) & (i == 0))
        def _():
            dwr_ref[...] = jnp.zeros_like(dwr_ref)
            dwi_ref[...] = jnp.zeros_like(dwi_ref)
            sums_ref[...] = jnp.zeros_like(sums_ref)

        @pl.when(i == 0)
        def _():
            carry[...] = jnp.zeros_like(carry)
            du_next[...] = jnp.zeros_like(du_next)

        xhalo = jnp.where(first_tile, 0.0, xhalo_ref[...])
        ext = jnp.concatenate([xhalo, x_ref[...]], axis=0)
        u = _conv(ext, cw_ref, cb_ref, ts)
        r, ig, sp, a, mult = _lru_gates(u, wr_ref, br_ref, wi_ref, bi_ref, lam_ref)
        gelu, dgelu = _gelu_parts(g_ref[...])
        dyv, hv = dy_ref[...], h_ref[...]
        dg_ref[...] = (dyv * hv * dgelu).astype(BF16)
        a_scr[...] = a
        dh_scr[...] = dyv * gelu

        rid = lax.broadcasted_iota(jnp.int32, (SUBLANES, LRU_W), 0)
        nchunk = ts // SUBLANES

        def chunk(n, cg):
            off = pl.multiple_of((nchunk - 1 - n) * SUBLANES, SUBLANES)
            av, beta = a_scr[pl.ds(off, SUBLANES), :], dh_scr[pl.ds(off, SUBLANES), :]
            alpha = jnp.where(rid == SUBLANES - 1, 1.0, pltpu.roll(av, SUBLANES - 1, 0))
            for d in (1, 2, 4):
                keep = rid + d <= SUBLANES - 1
                beta = jnp.where(keep, beta + alpha * pltpu.roll(beta, SUBLANES - d, 0), beta)
                alpha = jnp.where(keep, alpha * pltpu.roll(alpha, SUBLANES - d, 0), alpha)
            gv = beta + alpha * cg
            g_scr[pl.ds(off, SUBLANES), :] = gv
            return av[0:1, :] * gv[0:1, :]

        carry[...] = lax.fori_loop(0, nchunk, chunk, carry[...])
        gv = g_scr[...]
        hhalo = jnp.where(first_tile, 0.0, hhalo_ref[...])
        hprev = _shift_down(jnp.concatenate([hhalo, hv], axis=0), 1, ts)
        dmult = gv * ig * u
        dig = gv * mult * u
        du = gv * mult * ig
        dlog_a = gv * hprev * a - dmult * a * a / mult
        dr = dlog_a * (-LRU_C * sp)
        dr_pre = dr * r * (1.0 - r)
        di_pre = dig * ig * (1.0 - ig)
        drb, dib, ub = dr_pre.astype(BF16), di_pre.astype(BF16), u.astype(BF16)
        du = du + _dot(drb, wr_ref[...], NT) + _dot(dib, wi_ref[...], NT)
        dwr_ref[...] += _dot(ub, drb, TN)
        dwi_ref[...] += _dot(ub, dib, TN)

        ext_du = jnp.concatenate([du, du_next[...]], axis=0)
        du_next[...] = du[0:SUBLANES, :]
        n_ext = ts + SUBLANES
        dx = cw_ref[3:4, :] * du
        sums = [_colsum(dr_pre), _colsum(di_pre), _colsum(dlog_a * (-LRU_C * r)), _colsum(du)]
        dcw = []
        for k in range(3):
            dx = dx + cw_ref[k:k + 1, :] * pltpu.roll(ext_du, n_ext - (3 - k), 0)[0:ts]
            dcw.append(_colsum(du * _shift_down(ext, 3 - k, ts)))
        dcw.append(_colsum(du * _shift_down(ext, 0, ts)))
        dx_ref[...] = dx.astype(BF16)
        sums_ref[...] += _rows_to_block(sums + dcw, LRU_W)

    full = lambda shape: pl.BlockSpec(shape, lambda b, i: (0,) * len(shape))
    tile = pl.BlockSpec((ts, LRU_W), row)
    return pl.pallas_call(
        body, name="lru_bwd", grid=(batch, nt),
        in_specs=[tile, tile, halo, pl.BlockSpec((ts, LRU_W), lambda b, i: (b * nt + (nt - 1 - i), 1)), tile, halo,
                  full((4, LRU_W)), full((1, LRU_W)), full((LRU_W, LRU_W)), full((1, LRU_W)),
                  full((LRU_W, LRU_W)), full((1, LRU_W)), full((1, LRU_W))],
        out_specs=[tile, tile, full((LRU_W, LRU_W)), full((LRU_W, LRU_W)), full((SUBLANES, LRU_W))],
        out_shape=[jax.ShapeDtypeStruct((m, LRU_W), BF16), jax.ShapeDtypeStruct((m, LRU_W), BF16),
                   jax.ShapeDtypeStruct((LRU_W, LRU_W), F32), jax.ShapeDtypeStruct((LRU_W, LRU_W), F32),
                   jax.ShapeDtypeStruct((SUBLANES, LRU_W), F32)],
        scratch_shapes=[pltpu.VMEM((ts, LRU_W), F32), pltpu.VMEM((ts, LRU_W), F32), pltpu.VMEM((ts, LRU_W), F32),
                        pltpu.VMEM((1, LRU_W), F32), pltpu.VMEM((SUBLANES, LRU_W), F32)],
        compiler_params=_params(2),
    )(dy, proj32, proj32, proj32, h, h, conv_w, conv_b, wr, br, wi, bi, lam)


def _head_masks():
    lane = lax.broadcasted_iota(jnp.int32, (1, LANES), 1)
    return lane < HEAD_DIM


def _stack_heads(x2):
    lo, zero = _head_masks(), jnp.zeros_like(x2)
    return jnp.concatenate([jnp.where(lo, x2, zero), jnp.where(lo, zero, x2)], axis=0)


def _unstack_heads(y):
    return jnp.where(_head_masks(), y[:TQ], y[TQ:])


def _stack_cols(a, b):
    return jnp.concatenate([a, b], axis=0)


def _causal(qi, kb, strict):
    r = jnp.bitwise_and(lax.broadcasted_iota(jnp.int32, (2 * TQ, BLK), 0), TQ - 1) + qi * TQ
    c = lax.broadcasted_iota(jnp.int32, (2 * TQ, BLK), 1) + kb * BLK
    return (c < r) if strict else (c <= r)


def _key_loop(qi, group, carry, descending=False):
    def trip(n, cr):
        done = [n * KB_PER_Q + j for j in range(KB_PER_Q)]
        return group([qi * KB_PER_Q - 1 - t for t in done] if descending else done, cr)

    return lax.fori_loop(0, qi, trip, carry)


def _one_by_one(block):
    def group(kbs, carry):
        for kb in kbs:
            carry = block(kb, carry, False)
        return carry
    return group


def _tri(cmp):
    r = lax.broadcasted_iota(jnp.int32, (BLK, BLK), 0)
    c = lax.broadcasted_iota(jnp.int32, (BLK, BLK), 1)
    return cmp(r, c)


def _dot_split(x, tri):
    hi, lo = _split2(x)
    return _dot(hi, tri) + _dot(lo, tri)


def _sb_fwd(proj16, batch, seq):
    nq = seq // TQ
    scale = HEAD_DIM ** -0.5

    def body(q_ref, k_ref, v_ref, y_ref, t_ref):
        qi = pl.program_id(2)
        qs = _stack_heads(q_ref[0])
        tri_after = _tri(lambda r, c: r > c).astype(BF16)

        def block(kb, carry, masked):
            acc, c = carry
            ks = pl.multiple_of(kb * BLK, BLK)
            k2, v2 = k_ref[0, pl.ds(ks, BLK), :], v_ref[0, pl.ds(ks, BLK), :]
            z = _dot(qs, k2, NT) * scale
            sp = _softplus(z)
            l = -sp
            if masked:
                valid = _causal(qi, kb, True)
                l = jnp.where(valid, l, 0.0)
            w = jnp.exp((z - sp) + _dot_split(l, tri_after) + c)
            if masked:
                w = jnp.where(valid, w, 0.0)
            return acc + _dot(w.astype(BF16), v2), c + jnp.sum(l, axis=1, keepdims=True)

        def group(kbs, carry):
            acc, c = carry
            kv = [(k_ref[0, pl.ds(pl.multiple_of(kb * BLK, BLK), BLK), :],
                   v_ref[0, pl.ds(pl.multiple_of(kb * BLK, BLK), BLK), :]) for kb in kbs]
            zs = [_dot(qs, k2, NT) * scale for k2, _ in kv]
            sps = [_softplus(z) for z in zs]
            afters = [_dot_split(-sp, tri_after) for sp in sps]
            for z, sp, after, (_, v2) in zip(zs, sps, afters, kv):
                acc = acc + _dot(jnp.exp((z - sp) + after + c).astype(BF16), v2)
                c = c - jnp.sum(sp, axis=1, keepdims=True)
            return acc, c

        carry = (jnp.zeros((2 * TQ, LANES), F32), jnp.zeros((2 * TQ, 1), F32))
        first = qi * KB_PER_Q
        for n in reversed(range(KB_PER_Q)):
            carry = block(first + n, carry, True)
        acc, c = _key_loop(qi, group, carry, descending=True)
        y_ref[...] = _unstack_heads(acc)
        t_ref[0] = _unstack_heads(jnp.broadcast_to(c, (2 * TQ, LANES)))

    m = batch * seq
    return pl.pallas_call(
        body, name="sb_fwd", grid=(batch, 2, nq),
        in_specs=[pl.BlockSpec((1, TQ, LANES), lambda b, p, q: (b, q, COL_SBQ + p)),
                  pl.BlockSpec((1, seq, LANES), lambda b, p, q: (b, 0, COL_SBK + p)),
                  pl.BlockSpec((1, seq, LANES), lambda b, p, q: (b, 0, COL_SBV + p))],
        out_specs=[pl.BlockSpec((TQ, LANES), lambda b, p, q: (b * nq + q, p)),
                   pl.BlockSpec((1, TQ, LANES), lambda b, p, q: (p, b * nq + q, 0))],
        out_shape=[jax.ShapeDtypeStruct((m, ATT_W), F32), jax.ShapeDtypeStruct((2, m, LANES), F32)],
        compiler_params=_params(3),
    )(proj16, proj16, proj16)


def _sb_bwd(dy, t, proj16, batch, seq):
    nq = seq // TQ
    scale = HEAD_DIM ** -0.5

    def body(dy_ref, t_ref, q_ref, k_ref, v_ref, dq_ref, dk_ref, dv_ref):
        qi = pl.program_id(2)

        @pl.when(qi == 0)
        def _():
            dk_ref[...] = jnp.zeros_like(dk_ref)
            dv_ref[...] = jnp.zeros_like(dv_ref)

        t2 = t_ref[0]
        qs, dys = _stack_heads(q_ref[0]), _stack_heads(dy_ref[...].astype(BF16))
        tot = _stack_cols(t2[:, 0:1], t2[:, HEAD_DIM:HEAD_DIM + 1])
        tri_incl = _tri(lambda r, c: r <= c).astype(BF16)
        tri_excl = _tri(lambda r, c: r < c).astype(BF16)

        def block(kb, carry, masked):
            dq, pc, ec = carry
            ks = pl.multiple_of(kb * BLK, BLK)
            k2, v2 = k_ref[0, pl.ds(ks, BLK), :], v_ref[0, pl.ds(ks, BLK), :]
            z = _dot(qs, k2, NT) * scale
            sp = _softplus(z)
            l, b = -sp, z - sp
            sig = jnp.exp(b)
            if masked:
                valid = _causal(qi, kb, True)
                l = jnp.where(valid, l, 0.0)
            after = tot - (pc + _dot_split(l, tri_incl))
            w = jnp.exp(b + after)
            if masked:
                w = jnp.where(valid, w, 0.0)
            e = _dot(dys, v2, NT) * w
            et = ec + _dot_split(e, tri_excl)
            dz = e * (1.0 - sig) - et * sig
            if masked:
                dz = jnp.where(valid, dz, 0.0)
            dzb = (dz * scale).astype(BF16)
            dk_ref[0, pl.ds(ks, BLK), :] += _dot(dzb, qs, TN)
            dv_ref[0, pl.ds(ks, BLK), :] += _dot(w.astype(BF16), dys, TN)
            return (dq + _dot(dzb, k2), pc + jnp.sum(l, axis=1, keepdims=True),
                    ec + jnp.sum(e, axis=1, keepdims=True))

        col = jnp.zeros((2 * TQ, 1), F32)
        first = qi * KB_PER_Q
        carry = _key_loop(qi, _one_by_one(block),(jnp.zeros((2 * TQ, LANES), F32), col, col))
        for n in range(KB_PER_Q):
            carry = block(first + n, carry, True)
        dq_ref[...] = _unstack_heads(carry[0])

    m = batch * seq
    whole = lambda col: pl.BlockSpec((1, seq, LANES), lambda b, p, q: (b, 0, col + p))
    return pl.pallas_call(
        body, name="sb_bwd", grid=(batch, 2, nq),
        in_specs=[pl.BlockSpec((TQ, LANES), lambda b, p, q: (b * nq + q, p)),
                  pl.BlockSpec((1, TQ, LANES), lambda b, p, q: (p, b * nq + q, 0)),
                  pl.BlockSpec((1, TQ, LANES), lambda b, p, q: (b, q, COL_SBQ + p)),
                  whole(COL_SBK), whole(COL_SBV)],
        out_specs=[pl.BlockSpec((TQ, LANES), lambda b, p, q: (b * nq + q, p)), whole(0), whole(0)],
        out_shape=[jax.ShapeDtypeStruct((m, ATT_W), F32), jax.ShapeDtypeStruct((batch, seq, ATT_W), F32),
                   jax.ShapeDtypeStruct((batch, seq, ATT_W), F32)],
        compiler_params=_params(3),
    )(dy, t, proj16, proj16, proj16)


def _fox_pre(proj32, gq, gk, bf, group_mean, batch, seq):
    m = proj32.shape[0]
    ts = _tok_tile(seq)
    nt = seq // ts

    def body(q_ref, k_ref, f_ref, gq_ref, gk_ref, bf_ref, gm_ref, fq_ref, fk_ref, fc_ref, carry):
        i = pl.program_id(1)

        @pl.when(i == 0)
        def _():
            carry[...] = jnp.zeros_like(carry)

        gm = gm_ref[...]
        for src, g_ref, dst in ((q_ref, gq_ref, fq_ref), (k_ref, gk_ref, fk_ref)):
            v = src[...]
            ms = _dot_split(v * v, gm)
            dst[...] = (v * lax.rsqrt(ms + EPS) * g_ref[...]).astype(BF16)
        z = f_ref[...] + bf_ref[...]
        lf = jnp.minimum(z, 0.0) - jnp.log(1.0 + jnp.exp(-jnp.abs(z)))
        r = lax.broadcasted_iota(jnp.int32, (ts, ts), 0)
        c = lax.broadcasted_iota(jnp.int32, (ts, ts), 1)
        tri = (r >= c).astype(BF16)
        hi, mid, low = _split3(lf)
        fc = _dot(tri, hi) + _dot(tri, mid) + _dot(tri, low) + carry[...]
        fc_ref[...] = fc
        carry[...] = fc[ts - 1:ts, :]

    full = lambda shape: pl.BlockSpec(shape, lambda b, i: (0,) * len(shape))
    return pl.pallas_call(
        body, name="fox_pre", grid=(batch, nt),
        in_specs=[pl.BlockSpec((ts, ATT_W), lambda b, i: (b * nt + i, 7)),
                  pl.BlockSpec((ts, ATT_W), lambda b, i: (b * nt + i, 8)),
                  pl.BlockSpec((ts, LANES), lambda b, i: (b * nt + i, COL_FXF)),
                  full((1, ATT_W)), full((1, ATT_W)), full((1, LANES)), full((ATT_W, ATT_W))],
        out_specs=[pl.BlockSpec((ts, ATT_W), lambda b, i: (b * nt + i, 0)),
                   pl.BlockSpec((ts, ATT_W), lambda b, i: (b * nt + i, 0)),
                   pl.BlockSpec((ts, LANES), lambda b, i: (b * nt + i, 0))],
        out_shape=[jax.ShapeDtypeStruct((m, ATT_W), BF16), jax.ShapeDtypeStruct((m, ATT_W), BF16),
                   jax.ShapeDtypeStruct((m, LANES), F32)],
        scratch_shapes=[pltpu.VMEM((1, LANES), F32)],
        compiler_params=_params(2),
    )(proj32, proj32, proj32, gq, gk, bf, group_mean)


def _fox_specs(batch, seq):
    nq = seq // TQ
    return dict(
        qblk=pl.BlockSpec((1, TQ, LANES), lambda b, p, q: (b, q, p)),
        whole=pl.BlockSpec((1, seq, LANES), lambda b, p, q: (b, 0, p)),
        vwhole=pl.BlockSpec((1, seq, LANES), lambda b, p, q: (b, 0, COL_FXV + p)),
        fcol=pl.BlockSpec((1, 1, TQ, 2), lambda b, p, q: (b, p, q, 0)),
        frow=pl.BlockSpec((1, 1, 2, seq), lambda b, p, q: (b, p, 0, 0)),
        rows=pl.BlockSpec((TQ, LANES), lambda b, p, q: (b * nq + q, p)),
        stat=pl.BlockSpec((1, TQ, LANES), lambda b, p, q: (p, b * nq + q, 0)),
    )


def _fox_logits(qs, k2, fq_col, fr_ref, ks, is_a, scale):
    fk_row = jnp.where(is_a, fr_ref[0, 0, 0:1, pl.ds(ks, BLK)], fr_ref[0, 0, 1:2, pl.ds(ks, BLK)])
    return _dot(qs, k2, NT) * scale + fq_col - fk_row


def _fox_fwd(fq, fk, proj16, fcol, frow, batch, seq):
    nq = seq // TQ
    scale = HEAD_DIM ** -0.5

    def body(q_ref, k_ref, v_ref, fc_ref, fr_ref, y_ref, lse_ref):
        qi = pl.program_id(2)
        qs = _stack_heads(q_ref[0])
        fcv = fc_ref[0, 0]
        fq_col = _stack_cols(fcv[:, 0:1], fcv[:, 1:2])
        is_a = lax.broadcasted_iota(jnp.int32, (2 * TQ, 1), 0) < TQ

        def block(kb, carry, masked):
            acc, mx, den = carry
            ks = pl.multiple_of(kb * BLK, BLK)
            k2, v2 = k_ref[0, pl.ds(ks, BLK), :], v_ref[0, pl.ds(ks, BLK), :]
            s = _fox_logits(qs, k2, fq_col, fr_ref, ks, is_a, scale)
            if masked:
                s = jnp.where(_causal(qi, kb, False), s, NEG_BIG)
            mx_new = jnp.maximum(mx, jnp.max(s, axis=1, keepdims=True))
            p = jnp.exp(s - mx_new)
            alpha = jnp.exp(mx - mx_new)
            return (alpha * acc + _dot(p.astype(BF16), v2), mx_new, alpha * den + jnp.sum(p, axis=1, keepdims=True))

        first = qi * KB_PER_Q
        carry = (jnp.zeros((2 * TQ, LANES), F32), jnp.full((2 * TQ, 1), NEG_BIG, F32), jnp.zeros((2 * TQ, 1), F32))
        carry = _key_loop(qi, _one_by_one(block),carry)
        for n in range(KB_PER_Q):
            carry = block(first + n, carry, True)
        acc, mx, den = carry
        y_ref[...] = _unstack_heads(acc / den)
        lse_ref[0] = _unstack_heads(jnp.broadcast_to(mx + jnp.log(den), (2 * TQ, LANES)))

    m = batch * seq
    sp = _fox_specs(batch, seq)
    return pl.pallas_call(
        body, name="fox_fwd", grid=(batch, 2, nq),
        in_specs=[sp["qblk"], sp["whole"], sp["vwhole"], sp["fcol"], sp["frow"]],
        out_specs=[sp["rows"], sp["stat"]],
        out_shape=[jax.ShapeDtypeStruct((m, ATT_W), F32), jax.ShapeDtypeStruct((2, m, LANES), F32)],
        compiler_params=_params(3),
    )(fq, fk, proj16, fcol, frow)


def _fox_bwd(dy, y, lse, fq, fk, proj16, fcol, frow, batch, seq):
    nq = seq // TQ
    scale = HEAD_DIM ** -0.5

    def body(dy_ref, y_ref, lse_ref, q_ref, k_ref, v_ref, fc_ref, fr_ref, dq_ref, dk_ref, dv_ref, dfr_ref, dfc_ref):
        qi = pl.program_id(2)

        @pl.when(qi == 0)
        def _():
            dk_ref[...] = jnp.zeros_like(dk_ref)
            dv_ref[...] = jnp.zeros_like(dv_ref)
            dfr_ref[...] = jnp.zeros_like(dfr_ref)

        lo = _head_masks()
        lane = lax.broadcasted_iota(jnp.int32, (1, LANES), 1)
        dy2, lse2, fcv = dy_ref[...], lse_ref[0], fc_ref[0, 0]
        qs, dys = _stack_heads(q_ref[0]), _stack_heads(dy2.astype(BF16))
        dyy = dy2 * y_ref[...]
        delta = _stack_cols(jnp.sum(jnp.where(lo, dyy, 0.0), axis=1, keepdims=True),
                            jnp.sum(jnp.where(lo, 0.0, dyy), axis=1, keepdims=True))
        lse_col = _stack_cols(lse2[:, 0:1], lse2[:, HEAD_DIM:HEAD_DIM + 1])
        fq_col = _stack_cols(fcv[:, 0:1], fcv[:, 1:2])
        is_a = lax.broadcasted_iota(jnp.int32, (2 * TQ, 1), 0) < TQ

        def block(kb, carry, masked):
            dq, rs = carry
            ks = pl.multiple_of(kb * BLK, BLK)
            k2, v2 = k_ref[0, pl.ds(ks, BLK), :], v_ref[0, pl.ds(ks, BLK), :]
            p = jnp.exp(_fox_logits(qs, k2, fq_col, fr_ref, ks, is_a, scale) - lse_col)
            if masked:
                p = jnp.where(_causal(qi, kb, False), p, 0.0)
            ds = p * (_dot(dys, v2, NT) - delta)
            dsb = (ds * scale).astype(BF16)
            dk_ref[0, pl.ds(ks, BLK), :] += _dot(dsb, qs, TN)
            dv_ref[0, pl.ds(ks, BLK), :] += _dot(p.astype(BF16), dys, TN)
            dfr_ref[0, 0, 0:1, pl.ds(ks, BLK)] -= jnp.sum(ds[:TQ], axis=0, keepdims=True)
            dfr_ref[0, 0, 1:2, pl.ds(ks, BLK)] -= jnp.sum(ds[TQ:], axis=0, keepdims=True)
            return dq + _dot(dsb, k2), rs + jnp.sum(ds, axis=1, keepdims=True)

        first = qi * KB_PER_Q
        carry = _key_loop(qi, _one_by_one(block),(jnp.zeros((2 * TQ, LANES), F32), jnp.zeros((2 * TQ, 1), F32)))
        for n in range(KB_PER_Q):
            carry = block(first + n, carry, True)
        dq, rs = carry
        dq_ref[...] = _unstack_heads(dq)
        dfc_ref[0] = jnp.where(lane == 0, rs[:TQ], jnp.where(lane == 1, rs[TQ:], 0.0))

    m = batch * seq
    sp = _fox_specs(batch, seq)
    return pl.pallas_call(
        body, name="fox_bwd", grid=(batch, 2, nq),
        in_specs=[sp["rows"], sp["rows"], sp["stat"], sp["qblk"], sp["whole"], sp["vwhole"], sp["fcol"], sp["frow"]],
        out_specs=[sp["rows"], sp["whole"], sp["whole"],
                   pl.BlockSpec((1, 1, SUBLANES, seq), lambda b, p, q: (b, p, 0, 0)), sp["stat"]],
        out_shape=[jax.ShapeDtypeStruct((m, ATT_W), F32), jax.ShapeDtypeStruct((batch, seq, ATT_W), F32),
                   jax.ShapeDtypeStruct((batch, seq, ATT_W), F32),
                   jax.ShapeDtypeStruct((batch, 2, SUBLANES, seq), F32), jax.ShapeDtypeStruct((2, m, LANES), F32)],
        compiler_params=_params(3),
    )(dy, y, lse, fq, fk, proj16, fcol, frow)


def _fox_post_bwd(dfq, dfk, dfc, proj32, gq, gk, bf, group_mean, batch, seq):
    m = proj32.shape[0]
    ts = _tok_tile(seq)
    nt = seq // ts
    tile = lambda w, col: pl.BlockSpec((ts, w), lambda b, i: (b * nt + (nt - 1 - i), col))

    def body(dfq_ref, dfk_ref, dfc_ref, q_ref, k_ref, f_ref, gq_ref, gk_ref, bf_ref, gm_ref,
             dq_ref, dk_ref, df_ref, gs_ref, bs_ref, carry):
        i = pl.program_id(1)

        @pl.when(i == 0)
        def _():
            carry[...] = jnp.zeros_like(carry)

        gm = gm_ref[...]
        rows = []
        for src, g_ref, d_ref, dst in ((q_ref, gq_ref, dfq_ref, dq_ref), (k_ref, gk_ref, dfk_ref, dk_ref)):
            v, dv = src[...], d_ref[...]
            rstd = lax.rsqrt(_dot_split(v * v, gm) + EPS)
            vhat = v * rstd
            rows.append(_colsum(dv * vhat))
            dvh = dv * g_ref[...]
            dst[...] = (rstd * (dvh - vhat * _dot_split(dvh * vhat, gm))).astype(BF16)
        gs_ref[0] = _rows_to_block(rows, ATT_W)

        dfc_v = dfc_ref[...]
        r = lax.broadcasted_iota(jnp.int32, (ts, ts), 0)
        c = lax.broadcasted_iota(jnp.int32, (ts, ts), 1)
        tri = (r <= c).astype(BF16)
        hi, mid, low = _split3(dfc_v)
        dlf = _dot(tri, hi) + _dot(tri, mid) + _dot(tri, low) + carry[...]
        carry[...] = dlf[0:1, :]
        z = f_ref[...] + bf_ref[...]
        dz = dlf * _sigmoid(-z)
        df_ref[...] = dz.astype(BF16)
        bs_ref[0] = _rows_to_block([_colsum(dz)], LANES)

    full = lambda shape: pl.BlockSpec(shape, lambda b, i: (0,) * len(shape))
    part = lambda w: pl.BlockSpec((1, SUBLANES, w), lambda b, i: (b * nt + (nt - 1 - i), 0, 0))
    return pl.pallas_call(
        body, name="fox_post_bwd", grid=(batch, nt),
        in_specs=[tile(ATT_W, 0), tile(ATT_W, 0), tile(LANES, 0), tile(ATT_W, 7), tile(ATT_W, 8), tile(LANES, COL_FXF),
                  full((1, ATT_W)), full((1, ATT_W)), full((1, LANES)), full((ATT_W, ATT_W))],
        out_specs=[tile(ATT_W, 0), tile(ATT_W, 0), tile(LANES, 0), part(ATT_W), part(LANES)],
        out_shape=[jax.ShapeDtypeStruct((m, ATT_W), BF16), jax.ShapeDtypeStruct((m, ATT_W), BF16),
                   jax.ShapeDtypeStruct((m, LANES), BF16),
                   jax.ShapeDtypeStruct((batch * nt, SUBLANES, ATT_W), F32),
                   jax.ShapeDtypeStruct((batch * nt, SUBLANES, LANES), F32)],
        scratch_shapes=[pltpu.VMEM((1, LANES), F32)],
        compiler_params=_params(2),
    )(dfq, dfk, dfc, proj32, proj32, proj32, gq, gk, bf, group_mean)


_GROUPS = ((0, LRU_W), (LRU_W, LRU_W + ATT_W), (LRU_W + ATT_W, LRU_W + 2 * ATT_W))


def _outnorm(y_lru, y_sb, y_fox, gmix, seq):
    m = y_lru.shape[0]
    tm = _tok_tile(seq)

    def body(a_ref, b_ref, c_ref, g_ref, o_ref):
        parts = []
        for ref in (a_ref, b_ref, c_ref):
            v = ref[...]
            parts.append(v * lax.rsqrt(jnp.mean(v * v, axis=-1, keepdims=True) + EPS))
        o_ref[...] = (jnp.concatenate(parts, axis=1) * g_ref[...]).astype(BF16)

    t = lambda w: pl.BlockSpec((tm, w), lambda i: (i, 0))
    return pl.pallas_call(
        body, name="outnorm", grid=(m // tm,),
        in_specs=[t(LRU_W), t(ATT_W), t(ATT_W), pl.BlockSpec((1, D_MODEL), lambda i: (0, 0))],
        out_specs=t(D_MODEL), out_shape=jax.ShapeDtypeStruct((m, D_MODEL), BF16), compiler_params=_params(1),
    )(y_lru, y_sb, y_fox, gmix)


def _outnorm_bwd_epilogue(p, e_refs, o_refs):
    gmix = e_refs[3][...]
    dg = []
    for n, (lo, hi) in enumerate(_GROUPS):
        v, dyn = e_refs[n][...], p[:, lo:hi]
        rstd = lax.rsqrt(jnp.mean(v * v, axis=-1, keepdims=True) + EPS)
        vhat = v * rstd
        dg.append(_colsum(dyn * vhat))
        dvh = dyn * gmix[:, lo:hi]
        o_refs[n][...] = rstd * (dvh - vhat * jnp.mean(dvh * vhat, axis=-1, keepdims=True))
    o_refs[3][0] = _rows_to_block([jnp.concatenate(dg, axis=1)], p.shape[1])


def _pair_layouts(fcum, batch, seq):
    f4 = fcum[:, :4].reshape(batch, seq, 2, 2)
    return f4.transpose(0, 2, 1, 3), f4.transpose(0, 2, 3, 1)


def _gate_grad_cols(dfr, dfc, batch, seq):
    keys = dfr[:, :, :2, :].transpose(0, 3, 1, 2).reshape(batch * seq, 4)
    queries = dfc[:, :, :2].transpose(1, 0, 2).reshape(batch * seq, 4)
    return jnp.pad(keys + queries, ((0, 0), (0, LANES - 4)))


def _mixer_fwd(x, h, w, gate, batch, seq):
    m, d = x.shape
    tm = _tok_tile(seq)
    tpb = seq // tm

    def in_epilogue(p, e_refs, o_refs):
        o_refs[0][...] = p
        o_refs[1][...] = p.astype(BF16)

    tn_in = 896
    proj32, proj16 = _mm(h, w["w_in"], mode="nn", tm=tm, tn=tn_in, tk=d, b_lead=w["lead"], name="mix_in",
                         outs=[((m, N_IN_PAD), F32, (tm, tn_in), lambda i, j: (i, j)),
                               ((m, N_IN_PAD), BF16, (tm, tn_in), lambda i, j: (i, j))],
                         epilogue=in_epilogue)
    y_lru, h_lru = _lru_fwd(proj32, w["conv_w"], w["conv_b"], w["wr"], w["br"], w["wi"], w["bi"], w["lam"], batch, seq)
    p16 = proj16.reshape(batch, seq, N_IN_PAD)
    y_sb, t_sb = _sb_fwd(p16, batch, seq)
    fq, fk, fcum = _fox_pre(proj32, w["gq"], w["gk"], w["bf"], w["group_mean"], batch, seq)
    fcol, frow = _pair_layouts(fcum, batch, seq)
    fq3, fk3 = fq.reshape(batch, seq, ATT_W), fk.reshape(batch, seq, ATT_W)
    y_fox, lse = _fox_fwd(fq3, fk3, p16, fcol, frow, batch, seq)
    ynorm = _outnorm(y_lru, y_sb, y_fox, w["gmix"], seq)

    def out_epilogue(p, e_refs, o_refs):
        x_ref, g_ref = e_refs
        o_refs[0][...] = x_ref[...] + (1.0 + g_ref[0]) * p
        o_refs[1][...] = p.astype(BF16)

    x_out, out = _mm(ynorm, w["w_out"], mode="nn", tm=tm, tn=d, tk=d, b_lead=w["lead"], name="mix_out",
                     extras=[(x, (tm, d), lambda i, j: (i, 0)), (gate, (1, 1, d), lambda i, j: (i // tpb, 0, 0))],
                     outs=[((m, d), F32, (tm, d), lambda i, j: (i, 0)), ((m, d), BF16, (tm, d), lambda i, j: (i, 0))],
                     epilogue=out_epilogue)
    saved = dict(proj32=proj32, p16=p16, h_lru=h_lru, y_lru=y_lru, y_sb=y_sb, t_sb=t_sb, fq3=fq3, fk3=fk3,
                 fcol=fcol, frow=frow, y_fox=y_fox, lse=lse, ynorm=ynorm, out=out)
    return x_out, saved


def _mixer_bwd(dx_out, x, h, s, w, gn, scale, gate, batch, seq):
    m, d = x.shape
    tm = _tok_tile(seq)
    tpb = seq // tm
    dout, dgate_parts = _residual_bwd(dx_out, s["out"], gate, 1.0, seq, "mix_res_bwd")
    (dw_out,) = _mm(s["ynorm"], dout, mode="tn", tm=d, tn=d, tk=tm, name="mix_dwout",
                    outs=[((d, d), BF16, (d, d), lambda i, j: (i, j))], epilogue=_store_epilogue([BF16]))
    dy_lru, dy_sb, dy_fox, gmix_parts = _mm(
        dout, w["w_out"], mode="nt", tm=tm, tn=d, tk=d, b_lead=w["lead"], name="mix_out_dx",
        extras=[(s["y_lru"], (tm, LRU_W), lambda i, j: (i, 0)), (s["y_sb"], (tm, ATT_W), lambda i, j: (i, 0)),
                (s["y_fox"], (tm, ATT_W), lambda i, j: (i, 0)), (w["gmix"], (1, d), lambda i, j: (0, 0))],
        outs=[((m, LRU_W), F32, (tm, LRU_W), lambda i, j: (i, 0)), ((m, ATT_W), F32, (tm, ATT_W), lambda i, j: (i, 0)),
              ((m, ATT_W), F32, (tm, ATT_W), lambda i, j: (i, 0)),
              ((m // tm, SUBLANES, d), F32, (1, SUBLANES, d), lambda i, j: (i, 0, 0))],
        epilogue=_outnorm_bwd_epilogue)

    dsq, dsk, dsv = _sb_bwd(dy_sb, s["t_sb"], s["p16"], batch, seq)
    dfq, dfk, dfv, dfr, dfc = _fox_bwd(dy_fox, s["y_fox"], s["lse"], s["fq3"], s["fk3"], s["p16"], s["fcol"],
                                       s["frow"], batch, seq)
    dfc_cols = _gate_grad_cols(dfr, dfc, batch, seq)
    dxq, dxk, dxf, gqk_parts, bf_parts = _fox_post_bwd(dfq, dfk.reshape(m, ATT_W), dfc_cols, s["proj32"],
                                                       w["gq"], w["gk"], w["bf"], w["group_mean"], batch, seq)
    dlx, dlg, dwr, dwi, lru_sums = _lru_bwd(dy_lru, s["proj32"], s["h_lru"], w["conv_w"], w["conv_b"], w["wr"],
                                            w["br"], w["wi"], w["bi"], w["lam"], batch, seq)
    dproj = jnp.concatenate([dlx, dlg, dsq.astype(BF16), dsk.reshape(m, ATT_W).astype(BF16),
                             dsv.reshape(m, ATT_W).astype(BF16), dxq, dxk, dfv.reshape(m, ATT_W).astype(BF16), dxf],
                            axis=1)
    tn_in = 896
    (dw_in,) = _mm(h, dproj, mode="tn", tm=d, tn=tn_in, tk=tm, name="mix_dwin",
                   outs=[((d, N_IN_PAD), BF16, (d, tn_in), lambda i, j: (i, j))], epilogue=_store_epilogue([BF16]))
    dx, nm_parts = _mm(dproj, w["w_in"], mode="nt", tm=tm, tn=d, tk=tn_in, b_lead=w["lead"], name="mix_in_dx",
                       extras=[(x, (tm, d), lambda i, j: (i, 0)), (dx_out, (tm, d), lambda i, j: (i, 0)),
                               (gn, (1, d), lambda i, j: (0, 0)), (scale, (1, 1, d), lambda i, j: (i // tpb, 0, 0))],
                       outs=[((m, d), F32, (tm, d), lambda i, j: (i, 0)),
                             ((m // tm, SUBLANES, d), F32, (1, SUBLANES, d), lambda i, j: (i, 0, 0))],
                       epilogue=_normmod_bwd_epilogue)
    grads = dict(dw_in=dw_in, dw_out=dw_out, dwr=dwr, dwi=dwi, lru_sums=lru_sums, gmix_parts=gmix_parts,
                 gqk_parts=gqk_parts, bf_parts=bf_parts)
    return dx, grads, nm_parts, dgate_parts


def _block_diag(w):
    nb = w.shape[0]
    eye = jnp.eye(nb, dtype=w.dtype)
    return (eye[:, None, :, None] * w[:, :, None, :]).reshape(nb * HEAD_DIM, nb * HEAD_DIM)


def _block_diag_grad(g):
    nb = LRU_W // HEAD_DIM
    g4 = g.reshape(nb, HEAD_DIM, nb, HEAD_DIM)
    return jnp.stack([g4[n, :, n, :] for n in range(nb)])


def _per_batch(parts, batch, row):
    r = parts[:, row, :]
    return r.reshape(batch, -1, r.shape[-1]).sum(axis=1)


def _local_step(x3, target3, mod, wts):
    batch, seq, d = x3.shape
    assert seq % TQ == 0, seq
    m = batch * seq
    n_layers = mod.shape[0]
    x = x3.reshape(m, d)
    group_mean = _block_diag(jnp.full((ATT_W // HEAD_DIM, HEAD_DIM, HEAD_DIM), 1.0 / HEAD_DIM, BF16))
    vec = lambda l, j, t: mod[l, :, j, t][:, None, :]

    layers, saved = [], []
    for l in range(n_layers):
        gq = jnp.tile(wts["g_qk"][l, 0], ATT_W // HEAD_DIM)[None, :]
        gk = jnp.tile(wts["g_qk"][l, 1], ATT_W // HEAD_DIM)[None, :]
        bf = jnp.pad(wts["b_fgate"][l], (0, LANES - 4))[None, :]
        lw = dict(lead=(l,), w_in=wts["w_in"], w_out=wts["w_out"], conv_w=wts["conv_w"][l],
                  conv_b=wts["conv_b"][l][None, :], wr=_block_diag(wts["w_rgate"][l]).astype(BF16),
                  br=wts["b_rgate"][l][None, :], wi=_block_diag(wts["w_igate"][l]).astype(BF16),
                  bi=wts["b_igate"][l][None, :], lam=wts["lru_lambda"][l][None, :], gq=gq, gk=gk, bf=bf,
                  group_mean=group_mean, gmix=wts["g_mix_out"][l][None, :])
        layers.append(lw)
        gn = lambda j: wts["g_norm"][l, j][None, :]
        sv = dict(x0=x)
        sv["h0"] = _normmod(x, gn(0), vec(l, 0, 1), vec(l, 0, 0), seq, f"normmod_{l}_0")
        x, sv["ffn0"] = _ffn_fwd(x, sv["h0"], wts["w_up"], wts["w_down"], (l, 0), vec(l, 0, 2), seq, f"{l}_0")
        sv["x1"] = x
        sv["h1"] = _normmod(x, gn(1), vec(l, 1, 1), vec(l, 1, 0), seq, f"normmod_{l}_1")
        x, sv["mix"] = _mixer_fwd(x, sv["h1"], lw, vec(l, 1, 2), batch, seq)
        sv["x2"] = x
        sv["h2"] = _normmod(x, gn(2), vec(l, 2, 1), vec(l, 2, 0), seq, f"normmod_{l}_2")
        x, sv["ffn1"] = _ffn_fwd(x, sv["h2"], wts["w_up"], wts["w_down"], (l, 1), vec(l, 2, 2), seq, f"{l}_1")
        saved.append(sv)

    dx, loss_parts = _loss_head(x, target3.reshape(m, d), seq)
    loss = jnp.sum(loss_parts[:, 0, 0])

    big = dict(w_up=[], w_down=[], w_in=[], w_out=[])
    small = {k: [] for k in ("dmod", "g_norm", "b_fgate", "conv_w", "conv_b", "w_rgate", "b_rgate", "w_igate",
                             "b_igate", "lru_lambda", "g_qk", "g_mix_out")}
    for l in reversed(range(n_layers)):
        sv, lw = saved[l], layers[l]
        gn = lambda j: wts["g_norm"][l, j][None, :]
        dx, dwup1, dwdown1, nm2, dg2 = _ffn_bwd(dx, sv["x2"], sv["h2"], sv["ffn1"], wts["w_up"], wts["w_down"], (l, 1),
                                               gn(2), vec(l, 2, 1), vec(l, 2, 2), seq, f"{l}_1")
        dx, mg, nm1, dg1 = _mixer_bwd(dx, sv["x1"], sv["h1"], sv["mix"], lw, gn(1), vec(l, 1, 1), vec(l, 1, 2),
                                      batch, seq)
        dx, dwup0, dwdown0, nm0, dg0 = _ffn_bwd(dx, sv["x0"], sv["h0"], sv["ffn0"], wts["w_up"], wts["w_down"], (l, 0),
                                               gn(0), vec(l, 0, 1), vec(l, 0, 2), seq, f"{l}_0")
        big["w_up"].insert(0, jnp.stack([dwup0, dwup1]))
        big["w_down"].insert(0, jnp.stack([dwdown0, dwdown1]))
        big["w_in"].insert(0, mg["dw_in"])
        big["w_out"].insert(0, mg["dw_out"])
        dmod_l, gnorm_l = [], []
        for nm, dg in ((nm0, dg0), (nm1, dg1), (nm2, dg2)):
            dmod_l.append(jnp.stack([_per_batch(nm, batch, 0), _per_batch(nm, batch, 1), _per_batch(dg, batch, 0)],
                                    axis=1))
            gnorm_l.append(jnp.sum(nm[:, 2, :], axis=0))
        small["dmod"].insert(0, jnp.stack(dmod_l, axis=1))
        small["g_norm"].insert(0, jnp.stack(gnorm_l))
        ls = mg["lru_sums"]
        small["b_rgate"].insert(0, ls[0])
        small["b_igate"].insert(0, ls[1])
        small["lru_lambda"].insert(0, ls[2] * (-_sigmoid(-wts["lru_lambda"][l])))
        small["conv_b"].insert(0, ls[3])
        small["conv_w"].insert(0, ls[4:8])
        small["w_rgate"].insert(0, _block_diag_grad(mg["dwr"]))
        small["w_igate"].insert(0, _block_diag_grad(mg["dwi"]))
        small["g_mix_out"].insert(0, jnp.sum(mg["gmix_parts"][:, 0, :], axis=0))
        gqk = jnp.sum(mg["gqk_parts"][:, :2, :], axis=0).reshape(2, ATT_W // HEAD_DIM, HEAD_DIM).sum(axis=1)
        small["g_qk"].insert(0, gqk)
        small["b_fgate"].insert(0, jnp.sum(mg["bf_parts"][:, 0, :4], axis=0))
    big = {k: jnp.stack(v) for k, v in big.items()}
    small = {k: jnp.stack(v) for k, v in small.items()}
    return loss, dx.reshape(batch, seq, d), big, small


def _row_tile(rows, row_bytes):
    for t in (512, 256, 128, 64, 32, 16):
        if rows % t == 0 and t * row_bytes <= 4 * 1024 * 1024:
            return t
    return rows


def _adamw(parts, w, m, v, name):
    groups, n_parts, rows, cols = parts.shape
    tr = _row_tile(rows, cols * (n_parts * parts.dtype.itemsize + 7 * 4))
    c1 = 1.0 - ADAM_B1 ** ADAM_STEP
    c2 = 1.0 - ADAM_B2 ** ADAM_STEP

    def body(p_ref, w_ref, m_ref, v_ref, g_out, d_out, m_out, v_out):
        g = p_ref[0].astype(F32)
        for n in range(1, n_parts):
            g = g + p_ref[n].astype(F32)
        m_new = ADAM_B1 * m_ref[...] + (1.0 - ADAM_B1) * g
        v_new = ADAM_B2 * v_ref[...] + (1.0 - ADAM_B2) * (g * g)
        g_out[...] = g
        d_out[...] = -ADAM_LR * ((m_new / c1) / (jnp.sqrt(v_new / c2) + ADAM_EPS) + ADAM_WD * w_ref[...])
        m_out[...] = m_new
        v_out[...] = v_new

    tile = pl.BlockSpec((None, tr, cols), lambda g, i: (g, i, 0))
    return pl.pallas_call(
        body, name=name, grid=(groups, rows // tr),
        in_specs=[pl.BlockSpec((None, n_parts, tr, cols), lambda g, i: (g, 0, i, 0)), tile, tile, tile],
        out_specs=[tile] * 4, out_shape=[jax.ShapeDtypeStruct((groups, rows, cols), F32)] * 4,
        compiler_params=_params(2),
    )(parts, w, m, v)


def _sum_parts(parts):
    n_parts, rows, cols = parts.shape

    def body(p_ref, o_ref):
        acc = p_ref[0]
        for n in range(1, n_parts):
            acc = acc + p_ref[n]
        o_ref[...] = acc

    return pl.pallas_call(body, name="sum_small", out_shape=jax.ShapeDtypeStruct((rows, cols), F32),
                          compiler_params=pltpu.CompilerParams(vmem_limit_bytes=VMEM_LIMIT_BYTES))(parts)


def _flatten(arrays, multiple):
    flat = jnp.concatenate([a.reshape(-1).astype(F32) for a in arrays])
    pad = (-flat.shape[0]) % multiple
    return jnp.pad(flat, (0, pad)).reshape(-1, LANES)


def _unflatten(flat2d, shapes):
    flat, out, off = flat2d.reshape(-1), [], 0
    for s in shapes:
        n = math.prod(s)
        out.append(flat[off:off + n].reshape(s))
        off += n
    return out


SMALL_NAMES = ("b_ada", "g_norm", "b_fgate", "conv_w", "conv_b", "w_rgate", "b_rgate", "w_igate", "b_igate",
               "lru_lambda", "g_qk", "g_mix_out")
WEIGHT_NAMES = ("w_ada", "b_ada", "g_norm", "w_ffn_up", "w_ffn_down", "w_in", "b_fgate", "conv_w", "conv_b",
                "w_rgate", "b_rgate", "w_igate", "b_igate", "lru_lambda", "g_qk", "g_mix_out", "w_out")


def kernel(x, c, w_ada, b_ada, g_norm, w_ffn_up, w_ffn_down, w_in, b_fgate, conv_w, conv_b, w_rgate, b_rgate, w_igate, b_igate, lru_lambda, g_qk, g_mix_out, w_out, loss_target, m_w_ada, m_b_ada, m_g_norm, m_w_ffn_up, m_w_ffn_down, m_w_in, m_b_fgate, m_conv_w, m_conv_b, m_w_rgate, m_b_rgate, m_w_igate, m_b_igate, m_lru_lambda, m_g_qk, m_g_mix_out, m_w_out, v_w_ada, v_b_ada, v_g_norm, v_w_ffn_up, v_w_ffn_down, v_w_in, v_b_fgate, v_conv_w, v_conv_b, v_w_rgate, v_b_rgate, v_w_igate, v_b_igate, v_lru_lambda, v_g_qk, v_g_mix_out, v_w_out):
    batch, seq, d = x.shape
    n_layers = w_ada.shape[0]
    me = 4 * lax.axis_index("x") + 2 * lax.axis_index("y") + lax.axis_index("c")
    weights = dict(w_ada=w_ada, b_ada=b_ada, g_norm=g_norm, w_ffn_up=w_ffn_up, w_ffn_down=w_ffn_down, w_in=w_in,
                   b_fgate=b_fgate, conv_w=conv_w, conv_b=conv_b, w_rgate=w_rgate, b_rgate=b_rgate, w_igate=w_igate,
                   b_igate=b_igate, lru_lambda=lru_lambda, g_qk=g_qk, g_mix_out=g_mix_out, w_out=w_out)
    moments_m = dict(w_ada=m_w_ada, b_ada=m_b_ada, g_norm=m_g_norm, w_ffn_up=m_w_ffn_up, w_ffn_down=m_w_ffn_down,
                     w_in=m_w_in, b_fgate=m_b_fgate, conv_w=m_conv_w, conv_b=m_conv_b, w_rgate=m_w_rgate,
                     b_rgate=m_b_rgate, w_igate=m_w_igate, b_igate=m_b_igate, lru_lambda=m_lru_lambda, g_qk=m_g_qk,
                     g_mix_out=m_g_mix_out, w_out=m_w_out)
    moments_v = dict(w_ada=v_w_ada, b_ada=v_b_ada, g_norm=v_g_norm, w_ffn_up=v_w_ffn_up, w_ffn_down=v_w_ffn_down,
                     w_in=v_w_in, b_fgate=v_b_fgate, conv_w=v_conv_w, conv_b=v_conv_b, w_rgate=v_w_rgate,
                     b_rgate=v_b_rgate, w_igate=v_w_igate, b_igate=v_b_igate, lru_lambda=v_lru_lambda, g_qk=v_g_qk,
                     g_mix_out=v_g_mix_out, w_out=v_w_out)

    w_in_pad = jnp.pad(w_in, ((0, 0), (0, 0), (0, N_IN_PAD - N_IN))).astype(BF16)
    c_all, gn_all, cw_all, w_up_full, down_all, in_all, out_all = _exchange(
        [(c, 0), (g_norm, 0), (conv_w, 0), (w_ffn_up.astype(BF16), 2), (w_ffn_down.astype(BF16), 2), (w_in_pad, 1),
         (w_out.astype(BF16), 1)], [], "gather_weights", two_level=True)
    c_all = c_all.reshape(N_DEV * batch, d)
    n_ada = w_ada.shape[-1]
    g_norm_full = gn_all.transpose(1, 2, 0, 3).reshape(n_layers, 3, d)
    conv_w_full = cw_all.transpose(1, 2, 0, 3).reshape(n_layers, 4, LRU_W)
    w_down_full = down_all.reshape(n_layers, 2, D_FF, d)
    w_in_full = in_all.reshape(n_layers, d, N_IN_PAD)
    w_out_full = out_all.reshape(n_layers, d, d)

    b_ada_loc = lax.dynamic_slice_in_dim(b_ada, me * n_ada, n_ada, axis=1)
    silu = lambda t: t * _sigmoid(t)

    def bias_epilogue(p, e_refs, o_refs):
        o_refs[0][...] = p + e_refs[0][...]

    mod_loc = []
    for l in range(n_layers):
        (ml,) = _mm(c_all, w_ada, mode="nn", tm=c_all.shape[0], tn=n_ada, tk=d, b_lead=(l,), a_pre=silu,
                    name=f"ada_{l}", extras=[(b_ada_loc[l][None, :], (1, n_ada), lambda i, j: (0, 0))],
                    outs=[((c_all.shape[0], n_ada), F32, (c_all.shape[0], n_ada), lambda i, j: (0, 0))],
                    epilogue=bias_epilogue)
        mod_loc.append(ml)
    (mod_all,) = _exchange([(jnp.stack(mod_loc), 0)], [], "gather_mod")
    mod_all = mod_all.transpose(1, 2, 0, 3).reshape(n_layers, N_DEV * batch, 9 * d)
    mod_me = lax.dynamic_slice_in_dim(mod_all, me * batch, batch, axis=1).reshape(n_layers, batch, 3, 3, d)

    wts = dict(w_up=w_up_full, w_down=w_down_full, w_in=w_in_full, w_out=w_out_full, g_norm=g_norm_full,
               conv_w=conv_w_full, conv_b=conv_b, w_rgate=w_rgate, b_rgate=b_rgate, w_igate=w_igate, b_igate=b_igate,
               lru_lambda=lru_lambda, g_qk=g_qk, g_mix_out=g_mix_out, b_fgate=b_fgate)
    loss_part, grad_x, big, small = _local_step(x, loss_target, mod_me, wts)

    dmod_me = small.pop("dmod").reshape(n_layers, batch, 9 * d)
    small["b_ada"] = jnp.sum(dmod_me, axis=1)
    small_shapes = [(1,)] + [weights[k].shape if k not in ("g_norm", "conv_w") else small[k].shape for k in SMALL_NAMES]
    small_flat = _flatten([loss_part.reshape(1)] + [small[k] for k in SMALL_NAMES], 16 * LANES)
    g_down = big["w_down"].reshape(n_layers, 2, N_DEV, -1, d)
    g_in = big["w_in"].reshape(n_layers, N_DEV, -1, N_IN_PAD)
    g_out = big["w_out"].reshape(n_layers, N_DEV, -1, d)
    dmod_all, small_all, p_up, p_down, p_in, p_out = _exchange(
        [(dmod_me, 0), (small_flat, 0)], [(big["w_up"], 2), (g_down, 2), (g_in, 1), (g_out, 1)], "exchange_grads")
    small_sum = _unflatten(_sum_parts(small_all), small_shapes)
    loss = small_sum[0].reshape(())
    small_grads = dict(zip(SMALL_NAMES, small_sum[1:]))
    small_grads["g_norm"] = lax.dynamic_slice_in_dim(small_grads["g_norm"], me * g_norm.shape[-1], g_norm.shape[-1], 2)
    small_grads["conv_w"] = lax.dynamic_slice_in_dim(small_grads["conv_w"], me * conv_w.shape[-1], conv_w.shape[-1], 2)

    dmod_all = dmod_all.transpose(1, 0, 2, 3).reshape(n_layers, N_DEV * batch, 9 * d)
    dmod_loc = lax.dynamic_slice_in_dim(dmod_all, me * n_ada, n_ada, axis=2)
    g_ada = []
    for l in range(n_layers):
        (gl,) = _mm(c_all, dmod_loc[l], mode="tn", tm=d, tn=n_ada, tk=c_all.shape[0], a_pre=silu, name=f"dw_ada_{l}",
                    outs=[((d, n_ada), F32, (d, n_ada), lambda i, j: (0, 0))], epilogue=_store_epilogue([F32]))
        g_ada.append(gl)
    g_ada = jnp.stack(g_ada)

    results = {}

    def update(name, parts):
        shape = weights[name].shape
        as3d = lambda t: t.reshape((-1,) + shape[-2:])
        outs = _adamw(parts.reshape((-1,) + parts.shape[-3:]), as3d(weights[name]), as3d(moments_m[name]),
                      as3d(moments_v[name]), f"adamw_{name}")
        results[name] = [o.reshape(shape) for o in outs]

    update("w_ada", g_ada[:, None])
    update("w_ffn_up", p_up)
    update("w_ffn_down", p_down)
    update("w_in", p_in[..., :N_IN])
    update("w_out", p_out)
    sm_shapes = [weights[k].shape for k in SMALL_NAMES]
    flat = lambda src: _flatten([src[k] for k in SMALL_NAMES], 16 * LANES)
    sm_out = _adamw(flat(small_grads)[None, None], flat(weights)[None], flat(moments_m)[None], flat(moments_v)[None],
                    "adamw_small")
    for k, vals in zip(SMALL_NAMES, zip(*[_unflatten(o, sm_shapes) for o in sm_out])):
        results[k] = list(vals)

    outs = [loss, grad_x]
    for n in range(4):
        outs += [results[k][n] for k in WEIGHT_NAMES]
    return tuple(outs)
```

```python
import functools
import math

import jax
import jax.numpy as jnp
from jax import lax
from jax.experimental import pallas as pl
from jax.experimental.pallas import tpu as pltpu

F32 = jnp.float32
BF16 = jnp.bfloat16

N_DEV = 8
D_MODEL = 1024
D_FF = 2816
FF_SHARD = 2 * D_FF // N_DEV
N_FF_SHARD = D_FF // FF_SHARD
HEAD_DIM = 64
LRU_W = 512
ATT_W = 256
N_IN = 2564
N_IN_PAD = 2688
LANES = 128
SUBLANES = 8
BLK = 256
TQ = 512
KB_PER_Q = TQ // BLK
EPS = 1e-6
LRU_C = 8.0
NEG_BIG = -1e30
VMEM_LIMIT_BYTES = 48 * 1024 * 1024

ADAM_LR, ADAM_B1, ADAM_B2, ADAM_EPS, ADAM_WD, ADAM_STEP = 0.001, 0.9, 0.999, 1e-08, 0.01, 10

COL_SBQ, COL_SBK, COL_SBV = 8, 10, 12
COL_FXV, COL_FXF = 18, 20

NN = (((1,), (0,)), ((), ()))
NT = (((1,), (1,)), ((), ()))
TN = (((0,), (0,)), ((), ()))


def _params(n_axes):
    return pltpu.CompilerParams(dimension_semantics=("arbitrary",) * n_axes, vmem_limit_bytes=VMEM_LIMIT_BYTES)


def _tok_tile(seq):
    for t in (512, 256, 128):
        if seq % t == 0:
            return t
    raise ValueError(f"sequence length {seq} is not a multiple of 128")


def _dot(a, b, dims=NN):
    return lax.dot_general(a, b, dims, preferred_element_type=F32)


def _sigmoid(x):
    return 1.0 / (1.0 + jnp.exp(-x))


def _softplus(x):
    return jnp.maximum(x, 0.0) + jnp.log(1.0 + jnp.exp(-jnp.abs(x)))


def _gelu_parts(x):
    k0, k1 = math.sqrt(2.0 / math.pi), 0.044715
    t = jnp.tanh(k0 * (x + k1 * x * x * x))
    gelu = 0.5 * x * (1.0 + t)
    dgelu = 0.5 * (1.0 + t) + 0.5 * x * (1.0 - t * t) * k0 * (1.0 + 3.0 * k1 * x * x)
    return gelu, dgelu


def _neg_expm1(x):
    series = -x * (1.0 + x * (0.5 + x * (1.0 / 6.0 + x * (1.0 / 24.0 + x * (1.0 / 120.0 + x * (1.0 / 720.0))))))
    return jnp.where(x > -0.25, series, 1.0 - jnp.exp(x))


def _split2(x):
    hi = x.astype(BF16)
    lo = (x - hi.astype(F32)).astype(BF16)
    return hi, lo


def _split3(x):
    hi = x.astype(BF16)
    r = x - hi.astype(F32)
    mid = r.astype(BF16)
    lo = (r - mid.astype(F32)).astype(BF16)
    return hi, mid, lo


def _rows_to_block(rows, width):
    r = lax.broadcasted_iota(jnp.int32, (SUBLANES, width), 0)
    out = jnp.zeros((SUBLANES, width), F32)
    for n, v in enumerate(rows):
        out = jnp.where(r == n, jnp.broadcast_to(v, (SUBLANES, width)), out)
    return out


def _colsum(x):
    return jnp.sum(x, axis=0, keepdims=True)


def _exchange(gathers, scatters, name, two_level=False):
    assert not (two_level and scatters)
    n_g = len(gathers)
    ops = [a for a, _ in gathers] + [a for a, _ in scatters]
    n = len(ops)
    out_shape = [jax.ShapeDtypeStruct(a.shape[:nl] + (N_DEV,) + a.shape[nl:], a.dtype) for a, nl in gathers]
    out_shape += [jax.ShapeDtypeStruct(a.shape, a.dtype) for a, _ in scatters]
    items = []
    for k, (a, nl) in enumerate(list(gathers) + list(scatters)):
        for flat in range(math.prod(a.shape[:nl])):
            idx, rem = [], flat
            for dim in reversed(a.shape[:nl]):
                idx.insert(0, rem % dim)
                rem //= dim
            items.append((k, tuple(idx)))
    n_items = len(items)

    def body(*refs):
        ins, outs = refs[:n], refs[n:2 * n]
        send_sems, recv_sems, local_sems = refs[2 * n:]
        x, y, c = lax.axis_index("x"), lax.axis_index("y"), lax.axis_index("c")
        me = 4 * x + 2 * y + c

        def at(ref, idx):
            return ref.at[idx] if idx else ref

        def src(it, peer):
            k, idx = items[it]
            return at(ins[k], idx) if k < n_g else at(ins[k], idx + (peer,))

        def slot(it, s):
            k, idx = items[it]
            return at(outs[k], idx + (s,))

        def remote(it, rel, source, s, to):
            return pltpu.make_async_remote_copy(
                src_ref=source, dst_ref=slot(it, s), send_sem=send_sems.at[it, rel], recv_sem=recv_sems.at[it, rel],
                device_id=to, device_id_type=pl.DeviceIdType.MESH)

        local = [pltpu.make_async_copy(src(it, me), slot(it, me), local_sems.at[it]) for it in range(n_items)]
        for cp in local:
            cp.start()

        if not two_level:
            started = []
            for r in range(1, N_DEV):
                px = 1 - x if (r >> 2) & 1 else x
                py = 1 - y if (r >> 1) & 1 else y
                pc = 1 - c if r & 1 else c
                for it in range(n_items):
                    cp = remote(it, r - 1, src(it, 4 * px + 2 * py + pc), me, (px, py, pc))
                    cp.start()
                    started.append(cp)
            for cp in started:
                cp.wait()
        else:
            sibling, chips = (x, y, 1 - c), [(1 - x, y), (x, 1 - y), (1 - x, 1 - y)]
            sib = 4 * x + 2 * y + (1 - c)
            started = []
            for it in range(n_items):
                started.append(remote(it, 0, src(it, me), me, sibling))
                started += [remote(it, 1 + j, src(it, me), me, (cx, cy, c)) for j, (cx, cy) in enumerate(chips)]
            for cp in started:
                cp.start()
            for j, (cx, cy) in enumerate(chips):
                s = 4 * cx + 2 * cy + c
                for it in range(n_items):
                    remote(it, 1 + j, slot(it, s), s, sibling).wait_recv()
                    cp = remote(it, 4 + j, slot(it, s), s, sibling)
                    cp.start()
                    started.append(cp)
            for it in range(n_items):
                remote(it, 0, slot(it, sib), sib, sibling).wait_recv()
                for j, (cx, cy) in enumerate(chips):
                    s = 4 * cx + 2 * cy + (1 - c)
                    remote(it, 4 + j, slot(it, s), s, sibling).wait_recv()
            for cp in started:
                cp.wait_send()
        for cp in local:
            cp.wait()

    hbm = pl.BlockSpec(memory_space=pltpu.HBM)
    return pl.pallas_call(
        body, name=name, out_shape=out_shape,
        in_specs=[hbm] * n, out_specs=[hbm] * n,
        scratch_shapes=[pltpu.SemaphoreType.DMA((n_items, N_DEV - 1)), pltpu.SemaphoreType.DMA((n_items, N_DEV - 1)),
                        pltpu.SemaphoreType.DMA((n_items,))],
    )(*ops)


def _lead_items(ops):
    items = []
    for k, (a, nl) in enumerate(ops):
        for flat in range(math.prod(a.shape[:nl])):
            idx, rem = [], flat
            for dim in reversed(a.shape[:nl]):
                idx.insert(0, rem % dim)
                rem //= dim
            items.append((k, tuple(idx)))
    return items


def _scatter_copies(items, srcs, lands, send_sems, recv_sems):
    x, y, c = lax.axis_index("x"), lax.axis_index("y"), lax.axis_index("c")
    me = 4 * x + 2 * y + c
    copies = []
    for r in range(1, N_DEV):
        px = 1 - x if (r >> 2) & 1 else x
        py = 1 - y if (r >> 1) & 1 else y
        pc = 1 - c if r & 1 else c
        for it, (k, idx) in enumerate(items):
            copies.append(pltpu.make_async_remote_copy(
                src_ref=srcs[k].at[idx + (4 * px + 2 * py + pc,)], dst_ref=lands[k].at[idx + (me,)],
                send_sem=send_sems.at[it * (N_DEV - 1) + r - 1], recv_sem=recv_sems.at[it * (N_DEV - 1) + r - 1],
                device_id=(px, py, pc), device_id_type=pl.DeviceIdType.MESH))
    return copies


def _scatter_start(scatters, name):
    n = len(scatters)
    items = _lead_items(scatters)
    me = 4 * lax.axis_index("x") + 2 * lax.axis_index("y") + lax.axis_index("c")
    srcs, lands = [], []
    for a, nl in scatters:
        own = lax.dynamic_slice_in_dim(a, me, 1, axis=nl)
        start = (0,) * nl + (me,) + (0,) * (a.ndim - nl - 1)
        lands.append(pltpu.with_memory_space_constraint(
            lax.dynamic_update_slice(lax.empty(a.shape, a.dtype), own, start), pltpu.HBM))
        srcs.append(pltpu.with_memory_space_constraint(a, pltpu.HBM))

    def body(*refs):
        send_sems, recv_sems = refs[2 * n], refs[2 * n + 1]
        for cp in _scatter_copies(items, refs[:n], refs[n:2 * n], send_sems, recv_sems):
            cp.start()
        refs[-1][...] = jnp.zeros_like(refs[-1])

    hbm, sem = pl.BlockSpec(memory_space=pltpu.HBM), pl.BlockSpec(memory_space=pltpu.SEMAPHORE)
    sems = pltpu.SemaphoreType.DMA((len(items) * (N_DEV - 1),))
    res = pl.pallas_call(
        body, name=name,
        out_shape=[sems, sems] + [pltpu.HBM(a.shape, a.dtype) for a in srcs + lands]
        + [jax.ShapeDtypeStruct((SUBLANES, LANES), F32)],
        in_specs=[hbm] * (2 * n), out_specs=[sem, sem] + [hbm] * (2 * n) + [pl.BlockSpec(memory_space=pltpu.VMEM)],
        input_output_aliases={i: 2 + i for i in range(2 * n)},
        compiler_params=pltpu.CompilerParams(has_side_effects=pltpu.SideEffectType.DATAFLOW_SIDE_EFFECTING),
    )(*srcs, *lands)
    return (scatters, res[0], res[1], res[2:2 + n], res[2 + n:2 + 2 * n]), res[-1]


def _scatter_wait(handle, after, name):
    scatters, send_sems, recv_sems, srcs, lands = handle
    n = len(scatters)
    items = _lead_items(scatters)

    def body(*refs):
        for cp in _scatter_copies(items, refs[:n], refs[n:2 * n], refs[2 * n], refs[2 * n + 1]):
            cp.wait_send()
            cp.wait_recv()

    hbm, sem = pl.BlockSpec(memory_space=pltpu.HBM), pl.BlockSpec(memory_space=pltpu.SEMAPHORE)
    res = pl.pallas_call(
        body, name=name, out_shape=[pltpu.HBM(a.shape, a.dtype) for a in list(srcs) + list(lands)],
        in_specs=[hbm] * (2 * n) + [sem, sem, pl.BlockSpec(memory_space=pl.ANY)], out_specs=[hbm] * (2 * n),
        input_output_aliases={i: i for i in range(2 * n)},
        compiler_params=pltpu.CompilerParams(has_side_effects=pltpu.SideEffectType.DATAFLOW_SIDE_EFFECTING),
    )(*srcs, *lands, send_sems, recv_sems, after)
    return res[n:]


def _mm(a, b, *, mode, tm, tn, tk, outs, epilogue, name, extras=(), a_lead=(), b_lead=(), a_pre=None,
        a_spec=None, b_spec=None, shape=None, ksub=1):
    if shape is not None:
        mdim, ndim, kdim = shape
    else:
        if mode == "tn":
            kdim, mdim = a.shape[-2:]
        else:
            mdim, kdim = a.shape[-2:]
        ndim = b.shape[-2] if mode == "nt" else b.shape[-1]
    assert mdim % tm == 0 and ndim % tn == 0 and kdim % tk == 0, (name, mdim, ndim, kdim, tm, tn, tk)
    ni, nj, nk = mdim // tm, ndim // tn, kdim // tk
    a_lead, b_lead = tuple(a_lead), tuple(b_lead)
    a_block = (None,) * len(a_lead) + ((tk, tm) if mode == "tn" else (tm, tk))
    b_block = (None,) * len(b_lead) + ((tn, tk) if mode == "nt" else (tk, tn))
    dims = {"nn": NN, "nt": NT, "tn": TN}[mode]
    ne, no = len(extras), len(outs)

    def a_index(i, j, k):
        return a_lead + ((k, i) if mode == "tn" else (i, k))

    def b_index(i, j, k):
        return b_lead + ((j, k) if mode == "nt" else (k, j))

    if a_spec is not None:
        a_block, a_index = a_spec
    if b_spec is not None:
        b_block, b_index = b_spec

    def body(*refs):
        a_ref, b_ref = refs[0], refs[1]
        e_refs, o_refs = refs[2:2 + ne], refs[2 + ne:2 + ne + no]
        if ksub == 1:
            av = a_ref[...] if a_pre is None else a_pre(a_ref[...])
            p = _dot(av.astype(BF16), b_ref[...].astype(BF16), dims)
        else:
            p = _dot(a_ref[0], b_ref[0], dims)
            for s in range(1, ksub):
                p = p + _dot(a_ref[s], b_ref[s], dims)
        if nk == 1:
            epilogue(p, e_refs, o_refs)
        else:
            acc = refs[-1]
            k = pl.program_id(2)

            @pl.when(k == 0)
            def _():
                acc[...] = p

            @pl.when(k > 0)
            def _():
                acc[...] += p

            @pl.when(k == nk - 1)
            def _():
                epilogue(acc[...], e_refs, o_refs)

    in_specs = [pl.BlockSpec(a_block, a_index), pl.BlockSpec(b_block, b_index)]
    in_specs += [pl.BlockSpec(blk, functools.partial(lambda i, j, k, f: f(i, j), f=f)) for _, blk, f in extras]
    out_specs = [pl.BlockSpec(blk, functools.partial(lambda i, j, k, f: f(i, j), f=f)) for _, _, blk, f in outs]
    res = pl.pallas_call(
        body, name=name, grid=(ni, nj, nk), in_specs=in_specs, out_specs=out_specs,
        out_shape=[jax.ShapeDtypeStruct(s, d) for s, d, _, _ in outs],
        scratch_shapes=[pltpu.VMEM((tm, tn), F32)] if nk > 1 else [],
        compiler_params=_params(3),
    )(a, b, *[e[0] for e in extras])
    return res


def _store_epilogue(dtypes):
    def epi(p, e_refs, o_refs):
        for o, dt in zip(o_refs, dtypes):
            o[...] = p.astype(dt)
    return epi


def _normmod(x, gn, scale, shift, seq, name):
    m, d = x.shape
    tm = _tok_tile(seq)
    tpb = seq // tm

    def body(x_ref, gn_ref, sc_ref, sh_ref, h_ref):
        xv = x_ref[...]
        rstd = lax.rsqrt(jnp.mean(xv * xv, axis=-1, keepdims=True) + EPS)
        h_ref[...] = (xv * rstd * gn_ref[...] * (1.0 + sc_ref[0]) + sh_ref[0]).astype(BF16)

    vec = pl.BlockSpec((1, 1, d), lambda i: (i // tpb, 0, 0))
    return pl.pallas_call(
        body, name=name, grid=(m // tm,),
        in_specs=[pl.BlockSpec((tm, d), lambda i: (i, 0)), pl.BlockSpec((1, d), lambda i: (0, 0)), vec, vec],
        out_specs=pl.BlockSpec((tm, d), lambda i: (i, 0)),
        out_shape=jax.ShapeDtypeStruct((m, d), BF16), compiler_params=_params(1),
    )(x, gn, scale, shift)


def _normmod_bwd_epilogue(p, e_refs, o_refs):
    x_ref, dxo_ref, gn_ref, sc_ref = e_refs
    xv = x_ref[...]
    rstd = lax.rsqrt(jnp.mean(xv * xv, axis=-1, keepdims=True) + EPS)
    xhat = xv * rstd
    gn, sc1 = gn_ref[...], 1.0 + sc_ref[0]
    dxhat = p * (gn * sc1)
    dx = rstd * (dxhat - xhat * jnp.mean(dxhat * xhat, axis=-1, keepdims=True))
    o_refs[0][...] = dxo_ref[...] + dx
    t = p * xhat
    o_refs[1][0] = _rows_to_block([_colsum(p), _colsum(t * gn), _colsum(t * sc1)], p.shape[1])


def _residual_bwd(dx, f, gate, fac, seq, name):
    m, d = dx.shape
    tm = _tok_tile(seq)
    tpb = seq // tm

    def body(dx_ref, f_ref, g_ref, df_ref, dg_ref):
        dxv = dx_ref[...]
        df_ref[...] = ((fac * (1.0 + g_ref[0])) * dxv).astype(BF16)
        dg_ref[0] = _rows_to_block([_colsum((fac * dxv) * f_ref[...].astype(F32))], d)

    tile = pl.BlockSpec((tm, d), lambda i: (i, 0))
    return pl.pallas_call(
        body, name=name, grid=(m // tm,),
        in_specs=[tile, tile, pl.BlockSpec((1, 1, d), lambda i: (i // tpb, 0, 0))],
        out_specs=[tile, pl.BlockSpec((1, SUBLANES, d), lambda i: (i, 0, 0))],
        out_shape=[jax.ShapeDtypeStruct((m, d), BF16), jax.ShapeDtypeStruct((m // tm, SUBLANES, d), F32)],
        compiler_params=_params(1),
    )(dx, f, gate)


def _loss_head(y, target, seq):
    m, d = y.shape
    tm = _tok_tile(seq)

    def body(y_ref, t_ref, dy_ref, l_ref):
        err = y_ref[...] - t_ref[...]
        dy_ref[...] = err * (1.0 / d)
        part = 0.5 * jnp.sum(jnp.mean(err * err, axis=-1, keepdims=True), axis=0, keepdims=True)
        l_ref[0] = jnp.broadcast_to(part, (SUBLANES, LANES))

    tile = pl.BlockSpec((tm, d), lambda i: (i, 0))
    return pl.pallas_call(
        body, name="loss_head", grid=(m // tm,), in_specs=[tile, tile],
        out_specs=[tile, pl.BlockSpec((1, SUBLANES, LANES), lambda i: (i, 0, 0))],
        out_shape=[jax.ShapeDtypeStruct((m, d), F32), jax.ShapeDtypeStruct((m // tm, SUBLANES, LANES), F32)],
        compiler_params=_params(1),
    )(y, target)


def _ffn_fwd(x, h, wup, wdown, lead, gate, seq, tag):
    m, d = x.shape
    tm = _tok_tile(seq)
    tpb = seq // tm
    l, f_idx = lead

    def up_body(h_ref, wg_ref, wu_ref, a_ref, gu_ref):
        hv = h_ref[...]
        g, u = _dot(hv, wg_ref[...]), _dot(hv, wu_ref[...])
        a_ref[...] = (g * _sigmoid(g) * u).astype(BF16)
        gu_ref[0] = g.astype(BF16)
        gu_ref[1] = u.astype(BF16)

    wblk = (None, None, None, d, FF_SHARD)
    a, gu = pl.pallas_call(
        up_body, name=f"ffn_up_{tag}", grid=(N_FF_SHARD, m // tm),
        in_specs=[pl.BlockSpec((tm, d), lambda j, i: (i, 0)),
                  pl.BlockSpec(wblk, lambda j, i: (l, f_idx, j, 0, 0)),
                  pl.BlockSpec(wblk, lambda j, i: (l, f_idx, j + N_FF_SHARD, 0, 0))],
        out_specs=[pl.BlockSpec((None, tm, FF_SHARD), lambda j, i: (j, i, 0)),
                   pl.BlockSpec((2, None, tm, FF_SHARD), lambda j, i: (0, j, i, 0))],
        out_shape=[jax.ShapeDtypeStruct((N_FF_SHARD, m, FF_SHARD), BF16),
                   jax.ShapeDtypeStruct((2, N_FF_SHARD, m, FF_SHARD), BF16)],
        compiler_params=_params(2),
    )(h, wup, wup)

    def down_epilogue(p, e_refs, o_refs):
        x_ref, g_ref = e_refs
        o_refs[0][...] = x_ref[...] + (0.5 * (1.0 + g_ref[0])) * p
        o_refs[1][...] = p.astype(BF16)

    wdown5 = wdown.reshape(wdown.shape[:2] + (N_FF_SHARD, FF_SHARD, d))
    x_out, f = _mm(a, wdown5, mode="nn", tm=tm, tn=d, tk=D_FF, ksub=N_FF_SHARD, name=f"ffn_down_{tag}",
                   shape=(m, d, D_FF), a_spec=((N_FF_SHARD, tm, FF_SHARD), lambda i, j, k: (0, i, 0)),
                   b_spec=((None, None, N_FF_SHARD, FF_SHARD, d), lambda i, j, k: (l, f_idx, 0, 0, 0)),
                   extras=[(x, (tm, d), lambda i, j: (i, 0)), (gate, (1, 1, d), lambda i, j: (i // tpb, 0, 0))],
                   outs=[((m, d), F32, (tm, d), lambda i, j: (i, 0)), ((m, d), BF16, (tm, d), lambda i, j: (i, 0))],
                   epilogue=down_epilogue)
    return x_out, (a, gu, f)


def _ffn_bwd(dx_out, x, h, saved, wup, wdown, lead, gn, scale, gate, seq, tag):
    a, gu, f = saved
    m, d = x.shape
    tm = _tok_tile(seq)
    tpb = seq // tm
    l, f_idx = lead
    df, dgate_parts = _residual_bwd(dx_out, f, gate, 0.5, seq, f"ffn_res_bwd_{tag}")

    def act_bwd_epilogue(p, e_refs, o_refs):
        g, u = e_refs[0][0].astype(F32), e_refs[0][1].astype(F32)
        sg = _sigmoid(g)
        o_refs[0][0] = (p * u * (sg * (1.0 + g * (1.0 - sg)))).astype(BF16)
        o_refs[0][1] = (p * (g * sg)).astype(BF16)

    gu_blk = (2, None, tm, FF_SHARD)
    (dgu,) = _mm(df, wdown, mode="nt", tm=tm, tn=FF_SHARD, tk=d, name=f"ffn_down_dx_{tag}", shape=(m, D_FF, d),
                 b_spec=((None, None, FF_SHARD, d), lambda i, j, k: (l, f_idx, j, 0)),
                 extras=[(gu, gu_blk, lambda i, j: (0, j, i, 0))],
                 outs=[((2, N_FF_SHARD, m, FF_SHARD), BF16, gu_blk, lambda i, j: (0, j, i, 0))],
                 epilogue=act_bwd_epilogue)
    tt = 2 * tm if m % (2 * tm) == 0 else tm
    (dwdown,) = _mm(a, df, mode="tn", tm=FF_SHARD, tn=d, tk=tt, name=f"ffn_dwdown_{tag}", shape=(D_FF, d, m),
                    a_spec=((None, tt, FF_SHARD), lambda i, j, k: (i, k, 0)),
                    outs=[((D_FF, d), BF16, (FF_SHARD, d), lambda i, j: (i, 0))], epilogue=_store_epilogue([BF16]))
    dgu8 = dgu.reshape(2 * N_FF_SHARD, m, FF_SHARD)
    (dwup,) = _mm(h, dgu8, mode="tn", tm=d, tn=FF_SHARD, tk=tt, name=f"ffn_dwup_{tag}", shape=(d, 2 * D_FF, m),
                  b_spec=((None, tt, FF_SHARD), lambda i, j, k: (j, k, 0)),
                  outs=[((2 * N_FF_SHARD, d, FF_SHARD), BF16, (None, d, FF_SHARD), lambda i, j: (j, 0, 0))],
                  epilogue=_store_epilogue([BF16]))
    dx, nm_parts = _mm(dgu8, wup, mode="nt", tm=tm, tn=d, tk=D_FF, ksub=N_FF_SHARD, name=f"ffn_up_dx_{tag}",
                       shape=(m, d, 2 * D_FF), a_spec=((N_FF_SHARD, tm, FF_SHARD), lambda i, j, k: (k, i, 0)),
                       b_spec=((None, None, N_FF_SHARD, d, FF_SHARD), lambda i, j, k: (l, f_idx, k, 0, 0)),
                       extras=[(x, (tm, d), lambda i, j: (i, 0)), (dx_out, (tm, d), lambda i, j: (i, 0)),
                               (gn, (1, d), lambda i, j: (0, 0)), (scale, (1, 1, d), lambda i, j: (i // tpb, 0, 0))],
                       outs=[((m, d), F32, (tm, d), lambda i, j: (i, 0)),
                             ((m // tm, SUBLANES, d), F32, (1, SUBLANES, d), lambda i, j: (i, 0, 0))],
                       epilogue=_normmod_bwd_epilogue)
    return dx, dwup, dwdown, nm_parts, dgate_parts


def _shift_down(ext, n, rows):
    if n:
        ext = pltpu.roll(ext, n, 0)
    return ext[SUBLANES:SUBLANES + rows]


def _lru_gates(u, wr_ref, br_ref, wi_ref, bi_ref, lam_ref):
    ub = u.astype(BF16)
    r = _sigmoid(_dot(ub, wr_ref[...]) + br_ref[...])
    ig = _sigmoid(_dot(ub, wi_ref[...]) + bi_ref[...])
    sp = _softplus(-lam_ref[...])
    log_a = (-LRU_C * r) * sp
    a = jnp.exp(log_a)
    mult = jnp.sqrt(_neg_expm1(2.0 * log_a))
    return r, ig, sp, a, mult


def _conv(ext, cw_ref, cb_ref, rows):
    u = cb_ref[...] + cw_ref[3:4, :] * _shift_down(ext, 0, rows)
    for k in range(3):
        u = u + cw_ref[k:k + 1, :] * _shift_down(ext, 3 - k, rows)
    return u


def _lru_halo_spec(seq, ts):
    return pl.BlockSpec((SUBLANES, LRU_W),
                        lambda b, i: (jnp.maximum(b * (seq // SUBLANES) + i * (ts // SUBLANES) - 1, 0), 0))


def _lru_fwd(proj32, conv_w, conv_b, wr, br, wi, bi, lam, batch, seq):
    m = proj32.shape[0]
    ts = _tok_tile(seq)
    nt = seq // ts
    row = lambda b, i: (b * nt + i, 0)

    def body(x_ref, halo_ref, g_ref, cw_ref, cb_ref, wr_ref, br_ref, wi_ref, bi_ref, lam_ref,
             y_ref, h_ref, a_scr, b_scr, carry):
        i = pl.program_id(1)
        halo = jnp.where(i > 0, halo_ref[...], 0.0)
        ext = jnp.concatenate([halo, x_ref[...]], axis=0)
        u = _conv(ext, cw_ref, cb_ref, ts)
        _, ig, _, a, mult = _lru_gates(u, wr_ref, br_ref, wi_ref, bi_ref, lam_ref)
        a_scr[...] = a
        b_scr[...] = mult * (ig * u)

        @pl.when(i == 0)
        def _():
            carry[...] = jnp.zeros_like(carry)

        rid = lax.broadcasted_iota(jnp.int32, (SUBLANES, LRU_W), 0)

        def chunk(c, hprev):
            off = pl.multiple_of(c * SUBLANES, SUBLANES)
            av, bv = a_scr[pl.ds(off, SUBLANES), :], b_scr[pl.ds(off, SUBLANES), :]
            for d in (1, 2, 4):
                keep = rid >= d
                bv = jnp.where(keep, av * pltpu.roll(bv, d, 0) + bv, bv)
                av = jnp.where(keep, av * pltpu.roll(av, d, 0), av)
            h = av * hprev + bv
            h_ref[pl.ds(off, SUBLANES), :] = h
            return h[SUBLANES - 1:SUBLANES, :]

        carry[...] = lax.fori_loop(0, ts // SUBLANES, chunk, carry[...])
        gelu, _ = _gelu_parts(g_ref[...])
        y_ref[...] = h_ref[...] * gelu

    full = lambda shape: pl.BlockSpec(shape, lambda b, i: (0,) * len(shape))
    return pl.pallas_call(
        body, name="lru_fwd", grid=(batch, nt),
        in_specs=[pl.BlockSpec((ts, LRU_W), row), _lru_halo_spec(seq, ts),
                  pl.BlockSpec((ts, LRU_W), lambda b, i: (b * nt + i, 1)),
                  full((4, LRU_W)), full((1, LRU_W)), full((LRU_W, LRU_W)), full((1, LRU_W)),
                  full((LRU_W, LRU_W)), full((1, LRU_W)), full((1, LRU_W))],
        out_specs=[pl.BlockSpec((ts, LRU_W), row), pl.BlockSpec((ts, LRU_W), row)],
        out_shape=[jax.ShapeDtypeStruct((m, LRU_W), F32), jax.ShapeDtypeStruct((m, LRU_W), F32)],
        scratch_shapes=[pltpu.VMEM((ts, LRU_W), F32), pltpu.VMEM((ts, LRU_W), F32), pltpu.VMEM((1, LRU_W), F32)],
        compiler_params=_params(2),
    )(proj32, proj32, proj32, conv_w, conv_b, wr, br, wi, bi, lam)


def _lru_bwd(dy, proj32, h, conv_w, conv_b, wr, br, wi, bi, lam, batch, seq):
    m = proj32.shape[0]
    ts = _tok_tile(seq)
    nt = seq // ts
    row = lambda b, i: (b * nt + (nt - 1 - i), 0)
    halo = pl.BlockSpec((SUBLANES, LRU_W),
                        lambda b, i: (jnp.maximum(b * (seq // SUBLANES) + (nt - 1 - i) * (ts // SUBLANES) - 1, 0), 0))

    def body(dy_ref, x_ref, xhalo_ref, g_ref, h_ref, hhalo_ref, cw_ref, cb_ref, wr_ref, br_ref, wi_ref, bi_ref,
             lam_ref, dx_ref, dg_ref, dwr_ref, dwi_ref, sums_ref, a_scr, dh_scr, g_scr, carry, du_next):
        b, i = pl.program_id(0), pl.program_id(1)
        first_tile = i == nt - 1

        @pl.when((b == 0) & (i == 0))
        def _():
            dwr_ref[...] = jnp.zeros_like(dwr_ref)
            dwi_ref[...] = jnp.zeros_like(dwi_ref)
            sums_ref[...] = jnp.zeros_like(sums_ref)

        @pl.when(i == 0)
        def _():
            carry[...] = jnp.zeros_like(carry)
            du_next[...] = jnp.zeros_like(du_next)

        xhalo = jnp.where(first_tile, 0.0, xhalo_ref[...])
        ext = jnp.concatenate([xhalo, x_ref[...]], axis=0)
        u = _conv(ext, cw_ref, cb_ref, ts)
        r, ig, sp, a, mult = _lru_gates(u, wr_ref, br_ref, wi_ref, bi_ref, lam_ref)
        gelu, dgelu = _gelu_parts(g_ref[...])
        dyv, hv = dy_ref[...], h_ref[...]
        dg_ref[...] = (dyv * hv * dgelu).astype(BF16)
        a_scr[...] = a
        dh_scr[...] = dyv * gelu

        rid = lax.broadcasted_iota(jnp.int32, (SUBLANES, LRU_W), 0)
        nchunk = ts // SUBLANES

        def chunk(n, cg):
            off = pl.multiple_of((nchunk - 1 - n) * SUBLANES, SUBLANES)
            av, beta = a_scr[pl.ds(off, SUBLANES), :], dh_scr[pl.ds(off, SUBLANES), :]
            alpha = jnp.where(rid == SUBLANES - 1, 1.0, pltpu.roll(av, SUBLANES - 1, 0))
            for d in (1, 2, 4):
                keep = rid + d <= SUBLANES - 1
                beta = jnp.where(keep, beta + alpha * pltpu.roll(beta, SUBLANES - d, 0), beta)
                alpha = jnp.where(keep, alpha * pltpu.roll(alpha, SUBLANES - d, 0), alpha)
            gv = beta + alpha * cg
            g_scr[pl.ds(off, SUBLANES), :] = gv
            return av[0:1, :] * gv[0:1, :]

        carry[...] = lax.fori_loop(0, nchunk, chunk, carry[...])
        gv = g_scr[...]
        hhalo = jnp.where(first_tile, 0.0, hhalo_ref[...])
        hprev = _shift_down(jnp.concatenate([hhalo, hv], axis=0), 1, ts)
        dmult = gv * ig * u
        dig = gv * mult * u
        du = gv * mult * ig
        dlog_a = gv * hprev * a - dmult * a * a / mult
        dr = dlog_a * (-LRU_C * sp)
        dr_pre = dr * r * (1.0 - r)
        di_pre = dig * ig * (1.0 - ig)
        drb, dib, ub = dr_pre.astype(BF16), di_pre.astype(BF16), u.astype(BF16)
        du = du + _dot(drb, wr_ref[...], NT) + _dot(dib, wi_ref[...], NT)
        dwr_ref[...] += _dot(ub, drb, TN)
        dwi_ref[...] += _dot(ub, dib, TN)

        ext_du = jnp.concatenate([du, du_next[...]], axis=0)
        du_next[...] = du[0:SUBLANES, :]
        n_ext = ts + SUBLANES
        dx = cw_ref[3:4, :] * du
        sums = [_colsum(dr_pre), _colsum(di_pre), _colsum(dlog_a * (-LRU_C * r)), _colsum(du)]
        dcw = []
        for k in range(3):
            dx = dx + cw_ref[k:k + 1, :] * pltpu.roll(ext_du, n_ext - (3 - k), 0)[0:ts]
            dcw.append(_colsum(du * _shift_down(ext, 3 - k, ts)))
        dcw.append(_colsum(du * _shift_down(ext, 0, ts)))
        dx_ref[...] = dx.astype(BF16)
        sums_ref[...] += _rows_to_block(sums + dcw, LRU_W)

    full = lambda shape: pl.BlockSpec(shape, lambda b, i: (0,) * len(shape))
    tile = pl.BlockSpec((ts, LRU_W), row)
    return pl.pallas_call(
        body, name="lru_bwd", grid=(batch, nt),
        in_specs=[tile, tile, halo, pl.BlockSpec((ts, LRU_W), lambda b, i: (b * nt + (nt - 1 - i), 1)), tile, halo,
                  full((4, LRU_W)), full((1, LRU_W)), full((LRU_W, LRU_W)), full((1, LRU_W)),
                  full((LRU_W, LRU_W)), full((1, LRU_W)), full((1, LRU_W))],
        out_specs=[tile, tile, full((LRU_W, LRU_W)), full((LRU_W, LRU_W)), full((SUBLANES, LRU_W))],
        out_shape=[jax.ShapeDtypeStruct((m, LRU_W), BF16), jax.ShapeDtypeStruct((m, LRU_W), BF16),
                   jax.ShapeDtypeStruct((LRU_W, LRU_W), F32), jax.ShapeDtypeStruct((LRU_W, LRU_W), F32),
                   jax.ShapeDtypeStruct((SUBLANES, LRU_W), F32)],
        scratch_shapes=[pltpu.VMEM((ts, LRU_W), F32), pltpu.VMEM((ts, LRU_W), F32), pltpu.VMEM((ts, LRU_W), F32),
                        pltpu.VMEM((1, LRU_W), F32), pltpu.VMEM((SUBLANES, LRU_W), F32)],
        compiler_params=_params(2),
    )(dy, proj32, proj32, proj32, h, h, conv_w, conv_b, wr, br, wi, bi, lam)


def _head_masks():
    lane = lax.broadcasted_iota(jnp.int32, (1, LANES), 1)
    return lane < HEAD_DIM


def _stack_heads(x2):
    lo, zero = _head_masks(), jnp.zeros_like(x2)
    return jnp.concatenate([jnp.where(lo, x2, zero), jnp.where(lo, zero, x2)], axis=0)


def _unstack_heads(y):
    return jnp.where(_head_masks(), y[:TQ], y[TQ:])


def _stack_cols(a, b):
    return jnp.concatenate([a, b], axis=0)


def _causal(qi, kb, strict):
    r = jnp.bitwise_and(lax.broadcasted_iota(jnp.int32, (2 * TQ, BLK), 0), TQ - 1) + qi * TQ
    c = lax.broadcasted_iota(jnp.int32, (2 * TQ, BLK), 1) + kb * BLK
    return (c < r) if strict else (c <= r)


def _key_loop(qi, group, carry, descending=False):
    def trip(n, cr):
        done = [n * KB_PER_Q + j for j in range(KB_PER_Q)]
        return group([qi * KB_PER_Q - 1 - t for t in done] if descending else done, cr)

    return lax.fori_loop(0, qi, trip, carry)


def _one_by_one(block):
    def group(kbs, carry):
        for kb in kbs:
            carry = block(kb, carry, False)
        return carry
    return group


def _tri(cmp):
    r = lax.broadcasted_iota(jnp.int32, (BLK, BLK), 0)
    c = lax.broadcasted_iota(jnp.int32, (BLK, BLK), 1)
    return cmp(r, c)


def _dot_split(x, tri):
    hi, lo = _split2(x)
    return _dot(hi, tri) + _dot(lo, tri)


def _sb_fwd(proj16, batch, seq):
    nq = seq // TQ
    scale = HEAD_DIM ** -0.5

    def body(q_ref, k_ref, v_ref, y_ref, t_ref):
        qi = pl.program_id(2)
        qs = _stack_heads(q_ref[0])
        tri_after = _tri(lambda r, c: r > c).astype(BF16)

        def block(kb, carry, masked):
            acc, c = carry
            ks = pl.multiple_of(kb * BLK, BLK)
            k2, v2 = k_ref[0, pl.ds(ks, BLK), :], v_ref[0, pl.ds(ks, BLK), :]
            z = _dot(qs, k2, NT) * scale
            sp = _softplus(z)
            l = -sp
            if masked:
                valid = _causal(qi, kb, True)
                l = jnp.where(valid, l, 0.0)
            w = jnp.exp((z - sp) + _dot_split(l, tri_after) + c)
            if masked:
                w = jnp.where(valid, w, 0.0)
            return acc + _dot(w.astype(BF16), v2), c + jnp.sum(l, axis=1, keepdims=True)

        def group(kbs, carry):
            acc, c = carry
            kv = [(k_ref[0, pl.ds(pl.multiple_of(kb * BLK, BLK), BLK), :],
                   v_ref[0, pl.ds(pl.multiple_of(kb * BLK, BLK), BLK), :]) for kb in kbs]
            zs = [_dot(qs, k2, NT) * scale for k2, _ in kv]
            sps = [_softplus(z) for z in zs]
            afters = [_dot_split(-sp, tri_after) for sp in sps]
            for z, sp, after, (_, v2) in zip(zs, sps, afters, kv):
                acc = acc + _dot(jnp.exp((z - sp) + after + c).astype(BF16), v2)
                c = c - jnp.sum(sp, axis=1, keepdims=True)
            return acc, c

        carry = (jnp.zeros((2 * TQ, LANES), F32), jnp.zeros((2 * TQ, 1), F32))
        first = qi * KB_PER_Q
        for n in reversed(range(KB_PER_Q)):
            carry = block(first + n, carry, True)
        acc, c = _key_loop(qi, group, carry, descending=True)
        y_ref[...] = _unstack_heads(acc)
        t_ref[0] = _unstack_heads(jnp.broadcast_to(c, (2 * TQ, LANES)))

    m = batch * seq
    return pl.pallas_call(
        body, name="sb_fwd", grid=(batch, 2, nq),
        in_specs=[pl.BlockSpec((1, TQ, LANES), lambda b, p, q: (b, q, COL_SBQ + p)),
                  pl.BlockSpec((1, seq, LANES), lambda b, p, q: (b, 0, COL_SBK + p)),
                  pl.BlockSpec((1, seq, LANES), lambda b, p, q: (b, 0, COL_SBV + p))],
        out_specs=[pl.BlockSpec((TQ, LANES), lambda b, p, q: (b * nq + q, p)),
                   pl.BlockSpec((1, TQ, LANES), lambda b, p, q: (p, b * nq + q, 0))],
        out_shape=[jax.ShapeDtypeStruct((m, ATT_W), F32), jax.ShapeDtypeStruct((2, m, LANES), F32)],
        compiler_params=_params(3),
    )(proj16, proj16, proj16)


def _sb_bwd(dy, t, proj16, batch, seq):
    nq = seq // TQ
    scale = HEAD_DIM ** -0.5

    def body(dy_ref, t_ref, q_ref, k_ref, v_ref, dq_ref, dk_ref, dv_ref):
        qi = pl.program_id(2)

        @pl.when(qi == 0)
        def _():
            dk_ref[...] = jnp.zeros_like(dk_ref)
            dv_ref[...] = jnp.zeros_like(dv_ref)

        t2 = t_ref[0]
        qs, dys = _stack_heads(q_ref[0]), _stack_heads(dy_ref[...].astype(BF16))
        tot = _stack_cols(t2[:, 0:1], t2[:, HEAD_DIM:HEAD_DIM + 1])
        tri_incl = _tri(lambda r, c: r <= c).astype(BF16)
        tri_excl = _tri(lambda r, c: r < c).astype(BF16)

        def block(kb, carry, masked):
            dq, pc, ec = carry
            ks = pl.multiple_of(kb * BLK, BLK)
            k2, v2 = k_ref[0, pl.ds(ks, BLK), :], v_ref[0, pl.ds(ks, BLK), :]
            z = _dot(qs, k2, NT) * scale
            sp = _softplus(z)
            l, b = -sp, z - sp
            sig = jnp.exp(b)
            if masked:
                valid = _causal(qi, kb, True)
                l = jnp.where(valid, l, 0.0)
            after = tot - (pc + _dot_split(l, tri_incl))
            w = jnp.exp(b + after)
            if masked:
                w = jnp.where(valid, w, 0.0)
            e = _dot(dys, v2, NT) * w
            et = ec + _dot_split(e, tri_excl)
            dz = e * (1.0 - sig) - et * sig
            if masked:
                dz = jnp.where(valid, dz, 0.0)
            dzb = (dz * scale).astype(BF16)
            dk_ref[0, pl.ds(ks, BLK), :] += _dot(dzb, qs, TN)
            dv_ref[0, pl.ds(ks, BLK), :] += _dot(w.astype(BF16), dys, TN)
            return (dq + _dot(dzb, k2), pc + jnp.sum(l, axis=1, keepdims=True),
                    ec + jnp.sum(e, axis=1, keepdims=True))

        col = jnp.zeros((2 * TQ, 1), F32)
        first = qi * KB_PER_Q
        carry = _key_loop(qi, _one_by_one(block),(jnp.zeros((2 * TQ, LANES), F32), col, col))
        for n in range(KB_PER_Q):
            carry = block(first + n, carry, True)
        dq_ref[...] = _unstack_heads(carry[0])

    m = batch * seq
    whole = lambda col: pl.BlockSpec((1, seq, LANES), lambda b, p, q: (b, 0, col + p))
    return pl.pallas_call(
        body, name="sb_bwd", grid=(batch, 2, nq),
        in_specs=[pl.BlockSpec((TQ, LANES), lambda b, p, q: (b * nq + q, p)),
                  pl.BlockSpec((1, TQ, LANES), lambda b, p, q: (p, b * nq + q, 0)),
                  pl.BlockSpec((1, TQ, LANES), lambda b, p, q: (b, q, COL_SBQ + p)),
                  whole(COL_SBK), whole(COL_SBV)],
        out_specs=[pl.BlockSpec((TQ, LANES), lambda b, p, q: (b * nq + q, p)), whole(0), whole(0)],
        out_shape=[jax.ShapeDtypeStruct((m, ATT_W), F32), jax.ShapeDtypeStruct((batch, seq, ATT_W), F32),
                   jax.ShapeDtypeStruct((batch, seq, ATT_W), F32)],
        compiler_params=_params(3),
    )(dy, t, proj16, proj16, proj16)


def _fox_pre(proj32, gq, gk, bf, group_mean, batch, seq):
    m = proj32.shape[0]
    ts = _tok_tile(seq)
    nt = seq // ts

    def body(q_ref, k_ref, f_ref, gq_ref, gk_ref, bf_ref, gm_ref, fq_ref, fk_ref, fc_ref, carry):
        i = pl.program_id(1)

        @pl.when(i == 0)
        def _():
            carry[...] = jnp.zeros_like(carry)

        gm = gm_ref[...]
        for src, g_ref, dst in ((q_ref, gq_ref, fq_ref), (k_ref, gk_ref, fk_ref)):
            v = src[...]
            ms = _dot_split(v * v, gm)
            dst[...] = (v * lax.rsqrt(ms + EPS) * g_ref[...]).astype(BF16)
        z = f_ref[...] + bf_ref[...]
        lf = jnp.minimum(z, 0.0) - jnp.log(1.0 + jnp.exp(-jnp.abs(z)))
        r = lax.broadcasted_iota(jnp.int32, (ts, ts), 0)
        c = lax.broadcasted_iota(jnp.int32, (ts, ts), 1)
        tri = (r >= c).astype(BF16)
        hi, mid, low = _split3(lf)
        fc = _dot(tri, hi) + _dot(tri, mid) + _dot(tri, low) + carry[...]
        fc_ref[...] = fc
        carry[...] = fc[ts - 1:ts, :]

    full = lambda shape: pl.BlockSpec(shape, lambda b, i: (0,) * len(shape))
    return pl.pallas_call(
        body, name="fox_pre", grid=(batch, nt),
        in_specs=[pl.BlockSpec((ts, ATT_W), lambda b, i: (b * nt + i, 7)),
                  pl.BlockSpec((ts, ATT_W), lambda b, i: (b * nt + i, 8)),
                  pl.BlockSpec((ts, LANES), lambda b, i: (b * nt + i, COL_FXF)),
                  full((1, ATT_W)), full((1, ATT_W)), full((1, LANES)), full((ATT_W, ATT_W))],
        out_specs=[pl.BlockSpec((ts, ATT_W), lambda b, i: (b * nt + i, 0)),
                   pl.BlockSpec((ts, ATT_W), lambda b, i: (b * nt + i, 0)),
                   pl.BlockSpec((ts, LANES), lambda b, i: (b * nt + i, 0))],
        out_shape=[jax.ShapeDtypeStruct((m, ATT_W), BF16), jax.ShapeDtypeStruct((m, ATT_W), BF16),
                   jax.ShapeDtypeStruct((m, LANES), F32)],
        scratch_shapes=[pltpu.VMEM((1, LANES), F32)],
        compiler_params=_params(2),
    )(proj32, proj32, proj32, gq, gk, bf, group_mean)


def _fox_specs(batch, seq):
    nq = seq // TQ
    return dict(
        qblk=pl.BlockSpec((1, TQ, LANES), lambda b, p, q: (b, q, p)),
        whole=pl.BlockSpec((1, seq, LANES), lambda b, p, q: (b, 0, p)),
        vwhole=pl.BlockSpec((1, seq, LANES), lambda b, p, q: (b, 0, COL_FXV + p)),
        fcol=pl.BlockSpec((1, 1, TQ, 2), lambda b, p, q: (b, p, q, 0)),
        frow=pl.BlockSpec((1, 1, 2, seq), lambda b, p, q: (b, p, 0, 0)),
        rows=pl.BlockSpec((TQ, LANES), lambda b, p, q: (b * nq + q, p)),
        stat=pl.BlockSpec((1, TQ, LANES), lambda b, p, q: (p, b * nq + q, 0)),
    )


def _fox_logits(qs, k2, fq_col, fr_ref, ks, is_a, scale):
    fk_row = jnp.where(is_a, fr_ref[0, 0, 0:1, pl.ds(ks, BLK)], fr_ref[0, 0, 1:2, pl.ds(ks, BLK)])
    return _dot(qs, k2, NT) * scale + fq_col - fk_row


def _fox_fwd(fq, fk, proj16, fcol, frow, batch, seq):
    nq = seq // TQ
    scale = HEAD_DIM ** -0.5

    def body(q_ref, k_ref, v_ref, fc_ref, fr_ref, y_ref, lse_ref):
        qi = pl.program_id(2)
        qs = _stack_heads(q_ref[0])
        fcv = fc_ref[0, 0]
        fq_col = _stack_cols(fcv[:, 0:1], fcv[:, 1:2])
        is_a = lax.broadcasted_iota(jnp.int32, (2 * TQ, 1), 0) < TQ

        def block(kb, carry, masked):
            acc, mx, den = carry
            ks = pl.multiple_of(kb * BLK, BLK)
            k2, v2 = k_ref[0, pl.ds(ks, BLK), :], v_ref[0, pl.ds(ks, BLK), :]
            s = _fox_logits(qs, k2, fq_col, fr_ref, ks, is_a, scale)
            if masked:
                s = jnp.where(_causal(qi, kb, False), s, NEG_BIG)
            mx_new = jnp.maximum(mx, jnp.max(s, axis=1, keepdims=True))
            p = jnp.exp(s - mx_new)
            alpha = jnp.exp(mx - mx_new)
            return (alpha * acc + _dot(p.astype(BF16), v2), mx_new, alpha * den + jnp.sum(p, axis=1, keepdims=True))

        first = qi * KB_PER_Q
        carry = (jnp.zeros((2 * TQ, LANES), F32), jnp.full((2 * TQ, 1), NEG_BIG, F32), jnp.zeros((2 * TQ, 1), F32))
        carry = _key_loop(qi, _one_by_one(block),carry)
        for n in range(KB_PER_Q):
            carry = block(first + n, carry, True)
        acc, mx, den = carry
        y_ref[...] = _unstack_heads(acc / den)
        lse_ref[0] = _unstack_heads(jnp.broadcast_to(mx + jnp.log(den), (2 * TQ, LANES)))

    m = batch * seq
    sp = _fox_specs(batch, seq)
    return pl.pallas_call(
        body, name="fox_fwd", grid=(batch, 2, nq),
        in_specs=[sp["qblk"], sp["whole"], sp["vwhole"], sp["fcol"], sp["frow"]],
        out_specs=[sp["rows"], sp["stat"]],
        out_shape=[jax.ShapeDtypeStruct((m, ATT_W), F32), jax.ShapeDtypeStruct((2, m, LANES), F32)],
        compiler_params=_params(3),
    )(fq, fk, proj16, fcol, frow)


def _fox_bwd(dy, y, lse, fq, fk, proj16, fcol, frow, batch, seq):
    nq = seq // TQ
    scale = HEAD_DIM ** -0.5

    def body(dy_ref, y_ref, lse_ref, q_ref, k_ref, v_ref, fc_ref, fr_ref, dq_ref, dk_ref, dv_ref, dfr_ref, dfc_ref):
        qi = pl.program_id(2)

        @pl.when(qi == 0)
        def _():
            dk_ref[...] = jnp.zeros_like(dk_ref)
            dv_ref[...] = jnp.zeros_like(dv_ref)
            dfr_ref[...] = jnp.zeros_like(dfr_ref)

        lo = _head_masks()
        lane = lax.broadcasted_iota(jnp.int32, (1, LANES), 1)
        dy2, lse2, fcv = dy_ref[...], lse_ref[0], fc_ref[0, 0]
        qs, dys = _stack_heads(q_ref[0]), _stack_heads(dy2.astype(BF16))
        dyy = dy2 * y_ref[...]
        delta = _stack_cols(jnp.sum(jnp.where(lo, dyy, 0.0), axis=1, keepdims=True),
                            jnp.sum(jnp.where(lo, 0.0, dyy), axis=1, keepdims=True))
        lse_col = _stack_cols(lse2[:, 0:1], lse2[:, HEAD_DIM:HEAD_DIM + 1])
        fq_col = _stack_cols(fcv[:, 0:1], fcv[:, 1:2])
        is_a = lax.broadcasted_iota(jnp.int32, (2 * TQ, 1), 0) < TQ

        def block(kb, carry, masked):
            dq, rs = carry
            ks = pl.multiple_of(kb * BLK, BLK)
            k2, v2 = k_ref[0, pl.ds(ks, BLK), :], v_ref[0, pl.ds(ks, BLK), :]
            p = jnp.exp(_fox_logits(qs, k2, fq_col, fr_ref, ks, is_a, scale) - lse_col)
            if masked:
                p = jnp.where(_causal(qi, kb, False), p, 0.0)
            ds = p * (_dot(dys, v2, NT) - delta)
            dsb = (ds * scale).astype(BF16)
            dk_ref[0, pl.ds(ks, BLK), :] += _dot(dsb, qs, TN)
            dv_ref[0, pl.ds(ks, BLK), :] += _dot(p.astype(BF16), dys, TN)
            dfr_ref[0, 0, 0:1, pl.ds(ks, BLK)] -= jnp.sum(ds[:TQ], axis=0, keepdims=True)
            dfr_ref[0, 0, 1:2, pl.ds(ks, BLK)] -= jnp.sum(ds[TQ:], axis=0, keepdims=True)
            return dq + _dot(dsb, k2), rs + jnp.sum(ds, axis=1, keepdims=True)

        first = qi * KB_PER_Q
        carry = _key_loop(qi, _one_by_one(block),(jnp.zeros((2 * TQ, LANES), F32), jnp.zeros((2 * TQ, 1), F32)))
        for n in range(KB_PER_Q):
            carry = block(first + n, carry, True)
        dq, rs = carry
        dq_ref[...] = _unstack_heads(dq)
        dfc_ref[0] = jnp.where(lane == 0, rs[:TQ], jnp.where(lane == 1, rs[TQ:], 0.0))

    m = batch * seq
    sp = _fox_specs(batch, seq)
    return pl.pallas_call(
        body, name="fox_bwd", grid=(batch, 2, nq),
        in_specs=[sp["rows"], sp["rows"], sp["stat"], sp["qblk"], sp["whole"], sp["vwhole"], sp["fcol"], sp["frow"]],
        out_specs=[sp["rows"], sp["whole"], sp["whole"],
                   pl.BlockSpec((1, 1, SUBLANES, seq), lambda b, p, q: (b, p, 0, 0)), sp["stat"]],
        out_shape=[jax.ShapeDtypeStruct((m, ATT_W), F32), jax.ShapeDtypeStruct((batch, seq, ATT_W), F32),
                   jax.ShapeDtypeStruct((batch, seq, ATT_W), F32),
                   jax.ShapeDtypeStruct((batch, 2, SUBLANES, seq), F32), jax.ShapeDtypeStruct((2, m, LANES), F32)],
        compiler_params=_params(3),
    )(dy, y, lse, fq, fk, proj16, fcol, frow)


def _fox_post_bwd(dfq, dfk, dfc, proj32, gq, gk, bf, group_mean, batch, seq):
    m = proj32.shape[0]
    ts = _tok_tile(seq)
    nt = seq // ts
    tile = lambda w, col: pl.BlockSpec((ts, w), lambda b, i: (b * nt + (nt - 1 - i), col))

    def body(dfq_ref, dfk_ref, dfc_ref, q_ref, k_ref, f_ref, gq_ref, gk_ref, bf_ref, gm_ref,
             dq_ref, dk_ref, df_ref, gs_ref, bs_ref, carry):
        i = pl.program_id(1)

        @pl.when(i == 0)
        def _():
            carry[...] = jnp.zeros_like(carry)

        gm = gm_ref[...]
        rows = []
        for src, g_ref, d_ref, dst in ((q_ref, gq_ref, dfq_ref, dq_ref), (k_ref, gk_ref, dfk_ref, dk_ref)):
            v, dv = src[...], d_ref[...]
            rstd = lax.rsqrt(_dot_split(v * v, gm) + EPS)
            vhat = v * rstd
            rows.append(_colsum(dv * vhat))
            dvh = dv * g_ref[...]
            dst[...] = (rstd * (dvh - vhat * _dot_split(dvh * vhat, gm))).astype(BF16)
        gs_ref[0] = _rows_to_block(rows, ATT_W)

        dfc_v = dfc_ref[...]
        r = lax.broadcasted_iota(jnp.int32, (ts, ts), 0)
        c = lax.broadcasted_iota(jnp.int32, (ts, ts), 1)
        tri = (r <= c).astype(BF16)
        hi, mid, low = _split3(dfc_v)
        dlf = _dot(tri, hi) + _dot(tri, mid) + _dot(tri, low) + carry[...]
        carry[...] = dlf[0:1, :]
        z = f_ref[...] + bf_ref[...]
        dz = dlf * _sigmoid(-z)
        df_ref[...] = dz.astype(BF16)
        bs_ref[0] = _rows_to_block([_colsum(dz)], LANES)

    full = lambda shape: pl.BlockSpec(shape, lambda b, i: (0,) * len(shape))
    part = lambda w: pl.BlockSpec((1, SUBLANES, w), lambda b, i: (b * nt + (nt - 1 - i), 0, 0))
    return pl.pallas_call(
        body, name="fox_post_bwd", grid=(batch, nt),
        in_specs=[tile(ATT_W, 0), tile(ATT_W, 0), tile(LANES, 0), tile(ATT_W, 7), tile(ATT_W, 8), tile(LANES, COL_FXF),
                  full((1, ATT_W)), full((1, ATT_W)), full((1, LANES)), full((ATT_W, ATT_W))],
        out_specs=[tile(ATT_W, 0), tile(ATT_W, 0), tile(LANES, 0), part(ATT_W), part(LANES)],
        out_shape=[jax.ShapeDtypeStruct((m, ATT_W), BF16), jax.ShapeDtypeStruct((m, ATT_W), BF16),
                   jax.ShapeDtypeStruct((m, LANES), BF16),
                   jax.ShapeDtypeStruct((batch * nt, SUBLANES, ATT_W), F32),
                   jax.ShapeDtypeStruct((batch * nt, SUBLANES, LANES), F32)],
        scratch_shapes=[pltpu.VMEM((1, LANES), F32)],
        compiler_params=_params(2),
    )(dfq, dfk, dfc, proj32, proj32, proj32, gq, gk, bf, group_mean)


_GROUPS = ((0, LRU_W), (LRU_W, LRU_W + ATT_W), (LRU_W + ATT_W, LRU_W + 2 * ATT_W))


def _outnorm(y_lru, y_sb, y_fox, gmix, seq):
    m = y_lru.shape[0]
    tm = _tok_tile(seq)

    def body(a_ref, b_ref, c_ref, g_ref, o_ref):
        parts = []
        for ref in (a_ref, b_ref, c_ref):
            v = ref[...]
            parts.append(v * lax.rsqrt(jnp.mean(v * v, axis=-1, keepdims=True) + EPS))
        o_ref[...] = (jnp.concatenate(parts, axis=1) * g_ref[...]).astype(BF16)

    t = lambda w: pl.BlockSpec((tm, w), lambda i: (i, 0))
    return pl.pallas_call(
        body, name="outnorm", grid=(m // tm,),
        in_specs=[t(LRU_W), t(ATT_W), t(ATT_W), pl.BlockSpec((1, D_MODEL), lambda i: (0, 0))],
        out_specs=t(D_MODEL), out_shape=jax.ShapeDtypeStruct((m, D_MODEL), BF16), compiler_params=_params(1),
    )(y_lru, y_sb, y_fox, gmix)


def _outnorm_bwd_epilogue(p, e_refs, o_refs):
    gmix = e_refs[3][...]
    dg = []
    for n, (lo, hi) in enumerate(_GROUPS):
        v, dyn = e_refs[n][...], p[:, lo:hi]
        rstd = lax.rsqrt(jnp.mean(v * v, axis=-1, keepdims=True) + EPS)
        vhat = v * rstd
        dg.append(_colsum(dyn * vhat))
        dvh = dyn * gmix[:, lo:hi]
        o_refs[n][...] = rstd * (dvh - vhat * jnp.mean(dvh * vhat, axis=-1, keepdims=True))
    o_refs[3][0] = _rows_to_block([jnp.concatenate(dg, axis=1)], p.shape[1])


def _pair_layouts(fcum, batch, seq):
    f4 = fcum[:, :4].reshape(batch, seq, 2, 2)
    return f4.transpose(0, 2, 1, 3), f4.transpose(0, 2, 3, 1)


def _gate_grad_cols(dfr, dfc, batch, seq):
    keys = dfr[:, :, :2, :].transpose(0, 3, 1, 2).reshape(batch * seq, 4)
    queries = dfc[:, :, :2].transpose(1, 0, 2).reshape(batch * seq, 4)
    return jnp.pad(keys + queries, ((0, 0), (0, LANES - 4)))


def _mixer_fwd(x, h, w, gate, batch, seq):
    m, d = x.shape
    tm = _tok_tile(seq)
    tpb = seq // tm

    def in_epilogue(p, e_refs, o_refs):
        o_refs[0][...] = p
        o_refs[1][...] = p.astype(BF16)

    tn_in = 896
    proj32, proj16 = _mm(h, w["w_in"], mode="nn", tm=tm, tn=tn_in, tk=d, b_lead=w["lead"], name="mix_in",
                         outs=[((m, N_IN_PAD), F32, (tm, tn_in), lambda i, j: (i, j)),
                               ((m, N_IN_PAD), BF16, (tm, tn_in), lambda i, j: (i, j))],
                         epilogue=in_epilogue)
    y_lru, h_lru = _lru_fwd(proj32, w["conv_w"], w["conv_b"], w["wr"], w["br"], w["wi"], w["bi"], w["lam"], batch, seq)
    p16 = proj16.reshape(batch, seq, N_IN_PAD)
    y_sb, t_sb = _sb_fwd(p16, batch, seq)
    fq, fk, fcum = _fox_pre(proj32, w["gq"], w["gk"], w["bf"], w["group_mean"], batch, seq)
    fcol, frow = _pair_layouts(fcum, batch, seq)
    fq3, fk3 = fq.reshape(batch, seq, ATT_W), fk.reshape(batch, seq, ATT_W)
    y_fox, lse = _fox_fwd(fq3, fk3, p16, fcol, frow, batch, seq)
    ynorm = _outnorm(y_lru, y_sb, y_fox, w["gmix"], seq)

    def out_epilogue(p, e_refs, o_refs):
        x_ref, g_ref = e_refs
        o_refs[0][...] = x_ref[...] + (1.0 + g_ref[0]) * p
        o_refs[1][...] = p.astype(BF16)

    x_out, out = _mm(ynorm, w["w_out"], mode="nn", tm=tm, tn=d, tk=d, b_lead=w["lead"], name="mix_out",
                     extras=[(x, (tm, d), lambda i, j: (i, 0)), (gate, (1, 1, d), lambda i, j: (i // tpb, 0, 0))],
                     outs=[((m, d), F32, (tm, d), lambda i, j: (i, 0)), ((m, d), BF16, (tm, d), lambda i, j: (i, 0))],
                     epilogue=out_epilogue)
    saved = dict(proj32=proj32, p16=p16, h_lru=h_lru, y_lru=y_lru, y_sb=y_sb, t_sb=t_sb, fq3=fq3, fk3=fk3,
                 fcol=fcol, frow=frow, y_fox=y_fox, lse=lse, ynorm=ynorm, out=out)
    return x_out, saved


def _mixer_bwd(dx_out, x, h, s, w, gn, scale, gate, batch, seq):
    m, d = x.shape
    tm = _tok_tile(seq)
    tpb = seq // tm
    dout, dgate_parts = _residual_bwd(dx_out, s["out"], gate, 1.0, seq, "mix_res_bwd")
    (dw_out,) = _mm(s["ynorm"], dout, mode="tn", tm=d, tn=d, tk=tm, name="mix_dwout",
                    outs=[((d, d), BF16, (d, d), lambda i, j: (i, j))], epilogue=_store_epilogue([BF16]))
    dy_lru, dy_sb, dy_fox, gmix_parts = _mm(
        dout, w["w_out"], mode="nt", tm=tm, tn=d, tk=d, b_lead=w["lead"], name="mix_out_dx",
        extras=[(s["y_lru"], (tm, LRU_W), lambda i, j: (i, 0)), (s["y_sb"], (tm, ATT_W), lambda i, j: (i, 0)),
                (s["y_fox"], (tm, ATT_W), lambda i, j: (i, 0)), (w["gmix"], (1, d), lambda i, j: (0, 0))],
        outs=[((m, LRU_W), F32, (tm, LRU_W), lambda i, j: (i, 0)), ((m, ATT_W), F32, (tm, ATT_W), lambda i, j: (i, 0)),
              ((m, ATT_W), F32, (tm, ATT_W), lambda i, j: (i, 0)),
              ((m // tm, SUBLANES, d), F32, (1, SUBLANES, d), lambda i, j: (i, 0, 0))],
        epilogue=_outnorm_bwd_epilogue)

    dsq, dsk, dsv = _sb_bwd(dy_sb, s["t_sb"], s["p16"], batch, seq)
    dfq, dfk, dfv, dfr, dfc = _fox_bwd(dy_fox, s["y_fox"], s["lse"], s["fq3"], s["fk3"], s["p16"], s["fcol"],
                                       s["frow"], batch, seq)
    dfc_cols = _gate_grad_cols(dfr, dfc, batch, seq)
    dxq, dxk, dxf, gqk_parts, bf_parts = _fox_post_bwd(dfq, dfk.reshape(m, ATT_W), dfc_cols, s["proj32"],
                                                       w["gq"], w["gk"], w["bf"], w["group_mean"], batch, seq)
    dlx, dlg, dwr, dwi, lru_sums = _lru_bwd(dy_lru, s["proj32"], s["h_lru"], w["conv_w"], w["conv_b"], w["wr"],
                                            w["br"], w["wi"], w["bi"], w["lam"], batch, seq)
    dproj = jnp.concatenate([dlx, dlg, dsq.astype(BF16), dsk.reshape(m, ATT_W).astype(BF16),
                             dsv.reshape(m, ATT_W).astype(BF16), dxq, dxk, dfv.reshape(m, ATT_W).astype(BF16), dxf],
                            axis=1)
    tn_in = 896
    (dw_in,) = _mm(h, dproj, mode="tn", tm=d, tn=tn_in, tk=tm, name="mix_dwin",
                   outs=[((d, N_IN_PAD), BF16, (d, tn_in), lambda i, j: (i, j))], epilogue=_store_epilogue([BF16]))
    dx, nm_parts = _mm(dproj, w["w_in"], mode="nt", tm=tm, tn=d, tk=tn_in, b_lead=w["lead"], name="mix_in_dx",
                       extras=[(x, (tm, d), lambda i, j: (i, 0)), (dx_out, (tm, d), lambda i, j: (i, 0)),
                               (gn, (1, d), lambda i, j: (0, 0)), (scale, (1, 1, d), lambda i, j: (i // tpb, 0, 0))],
                       outs=[((m, d), F32, (tm, d), lambda i, j: (i, 0)),
                             ((m // tm, SUBLANES, d), F32, (1, SUBLANES, d), lambda i, j: (i, 0, 0))],
                       epilogue=_normmod_bwd_epilogue)
    grads = dict(dw_in=dw_in, dw_out=dw_out, dwr=dwr, dwi=dwi, lru_sums=lru_sums, gmix_parts=gmix_parts,
                 gqk_parts=gqk_parts, bf_parts=bf_parts)
    return dx, grads, nm_parts, dgate_parts


def _block_diag(w):
    nb = w.shape[0]
    eye = jnp.eye(nb, dtype=w.dtype)
    return (eye[:, None, :, None] * w[:, :, None, :]).reshape(nb * HEAD_DIM, nb * HEAD_DIM)


def _block_diag_grad(g):
    nb = LRU_W // HEAD_DIM
    g4 = g.reshape(nb, HEAD_DIM, nb, HEAD_DIM)
    return jnp.stack([g4[n, :, n, :] for n in range(nb)])


def _per_batch(parts, batch, row):
    r = parts[:, row, :]
    return r.reshape(batch, -1, r.shape[-1]).sum(axis=1)


def _local_step(x3, target3, mod, wts):
    batch, seq, d = x3.shape
    assert seq % TQ == 0, seq
    m = batch * seq
    n_layers = mod.shape[0]
    x = x3.reshape(m, d)
    group_mean = _block_diag(jnp.full((ATT_W // HEAD_DIM, HEAD_DIM, HEAD_DIM), 1.0 / HEAD_DIM, BF16))
    vec = lambda l, j, t: mod[l, :, j, t][:, None, :]

    layers, saved = [], []
    for l in range(n_layers):
        gq = jnp.tile(wts["g_qk"][l, 0], ATT_W // HEAD_DIM)[None, :]
        gk = jnp.tile(wts["g_qk"][l, 1], ATT_W // HEAD_DIM)[None, :]
        bf = jnp.pad(wts["b_fgate"][l], (0, LANES - 4))[None, :]
        lw = dict(lead=(l,), w_in=wts["w_in"], w_out=wts["w_out"], conv_w=wts["conv_w"][l],
                  conv_b=wts["conv_b"][l][None, :], wr=_block_diag(wts["w_rgate"][l]).astype(BF16),
                  br=wts["b_rgate"][l][None, :], wi=_block_diag(wts["w_igate"][l]).astype(BF16),
                  bi=wts["b_igate"][l][None, :], lam=wts["lru_lambda"][l][None, :], gq=gq, gk=gk, bf=bf,
                  group_mean=group_mean, gmix=wts["g_mix_out"][l][None, :])
        layers.append(lw)
        gn = lambda j: wts["g_norm"][l, j][None, :]
        sv = dict(x0=x)
        sv["h0"] = _normmod(x, gn(0), vec(l, 0, 1), vec(l, 0, 0), seq, f"normmod_{l}_0")
        x, sv["ffn0"] = _ffn_fwd(x, sv["h0"], wts["w_up"], wts["w_down"], (l, 0), vec(l, 0, 2), seq, f"{l}_0")
        sv["x1"] = x
        sv["h1"] = _normmod(x, gn(1), vec(l, 1, 1), vec(l, 1, 0), seq, f"normmod_{l}_1")
        x, sv["mix"] = _mixer_fwd(x, sv["h1"], lw, vec(l, 1, 2), batch, seq)
        sv["x2"] = x
        sv["h2"] = _normmod(x, gn(2), vec(l, 2, 1), vec(l, 2, 0), seq, f"normmod_{l}_2")
        x, sv["ffn1"] = _ffn_fwd(x, sv["h2"], wts["w_up"], wts["w_down"], (l, 1), vec(l, 2, 2), seq, f"{l}_1")
        saved.append(sv)

    dx, loss_parts = _loss_head(x, target3.reshape(m, d), seq)
    loss = jnp.sum(loss_parts[:, 0, 0])

    handles, token = [None] * n_layers, None
    small = {k: [] for k in ("dmod", "g_norm", "b_fgate", "conv_w", "conv_b", "w_rgate", "b_rgate", "w_igate",
                             "b_igate", "lru_lambda", "g_qk", "g_mix_out")}
    for l in reversed(range(n_layers)):
        sv, lw = saved[l], layers[l]
        gn = lambda j: wts["g_norm"][l, j][None, :]
        gate2 = vec(l, 2, 2)
        if token is not None:
            gate2 = gate2 + token[0, 0]
        dx, dwup1, dwdown1, nm2, dg2 = _ffn_bwd(dx, sv["x2"], sv["h2"], sv["ffn1"], wts["w_up"], wts["w_down"], (l, 1),
                                               gn(2), vec(l, 2, 1), gate2, seq, f"{l}_1")
        dx, mg, nm1, dg1 = _mixer_bwd(dx, sv["x1"], sv["h1"], sv["mix"], lw, gn(1), vec(l, 1, 1), vec(l, 1, 2),
                                      batch, seq)
        dx, dwup0, dwdown0, nm0, dg0 = _ffn_bwd(dx, sv["x0"], sv["h0"], sv["ffn0"], wts["w_up"], wts["w_down"], (l, 0),
                                               gn(0), vec(l, 0, 1), vec(l, 0, 2), seq, f"{l}_0")
        handles[l], token = _scatter_start(
            [(jnp.stack([dwup0, dwup1]), 1), (jnp.stack([dwdown0, dwdown1]).reshape(2, N_DEV, -1, d), 1),
             (mg["dw_in"].reshape(N_DEV, -1, N_IN_PAD), 0), (mg["dw_out"].reshape(N_DEV, -1, d), 0)],
            f"grads_start_{l}")
        dmod_l, gnorm_l = [], []
        for nm, dg in ((nm0, dg0), (nm1, dg1), (nm2, dg2)):
            dmod_l.append(jnp.stack([_per_batch(nm, batch, 0), _per_batch(nm, batch, 1), _per_batch(dg, batch, 0)],
                                    axis=1))
            gnorm_l.append(jnp.sum(nm[:, 2, :], axis=0))
        small["dmod"].insert(0, jnp.stack(dmod_l, axis=1))
        small["g_norm"].insert(0, jnp.stack(gnorm_l))
        ls = mg["lru_sums"]
        small["b_rgate"].insert(0, ls[0])
        small["b_igate"].insert(0, ls[1])
        small["lru_lambda"].insert(0, ls[2] * (-_sigmoid(-wts["lru_lambda"][l])))
        small["conv_b"].insert(0, ls[3])
        small["conv_w"].insert(0, ls[4:8])
        small["w_rgate"].insert(0, _block_diag_grad(mg["dwr"]))
        small["w_igate"].insert(0, _block_diag_grad(mg["dwi"]))
        small["g_mix_out"].insert(0, jnp.sum(mg["gmix_parts"][:, 0, :], axis=0))
        gqk = jnp.sum(mg["gqk_parts"][:, :2, :], axis=0).reshape(2, ATT_W // HEAD_DIM, HEAD_DIM).sum(axis=1)
        small["g_qk"].insert(0, gqk)
        small["b_fgate"].insert(0, jnp.sum(mg["bf_parts"][:, 0, :4], axis=0))
    small = {k: jnp.stack(v) for k, v in small.items()}
    return loss, dx.reshape(batch, seq, d), handles, small


def _row_tile(rows, row_bytes):
    for t in (512, 256, 128, 64, 32, 16):
        if rows % t == 0 and t * row_bytes <= 4 * 1024 * 1024:
            return t
    return rows


def _adamw(parts, w, m, v, name):
    groups, n_parts, rows, cols = parts.shape
    tr = _row_tile(rows, cols * (n_parts * parts.dtype.itemsize + 7 * 4))
    c1 = 1.0 - ADAM_B1 ** ADAM_STEP
    c2 = 1.0 - ADAM_B2 ** ADAM_STEP

    def body(p_ref, w_ref, m_ref, v_ref, g_out, d_out, m_out, v_out):
        g = p_ref[0].astype(F32)
        for n in range(1, n_parts):
            g = g + p_ref[n].astype(F32)
        m_new = ADAM_B1 * m_ref[...] + (1.0 - ADAM_B1) * g
        v_new = ADAM_B2 * v_ref[...] + (1.0 - ADAM_B2) * (g * g)
        g_out[...] = g
        d_out[...] = -ADAM_LR * ((m_new / c1) / (jnp.sqrt(v_new / c2) + ADAM_EPS) + ADAM_WD * w_ref[...])
        m_out[...] = m_new
        v_out[...] = v_new

    tile = pl.BlockSpec((None, tr, cols), lambda g, i: (g, i, 0))
    return pl.pallas_call(
        body, name=name, grid=(groups, rows // tr),
        in_specs=[pl.BlockSpec((None, n_parts, tr, cols), lambda g, i: (g, 0, i, 0)), tile, tile, tile],
        out_specs=[tile] * 4, out_shape=[jax.ShapeDtypeStruct((groups, rows, cols), F32)] * 4,
        compiler_params=_params(2),
    )(parts, w, m, v)


def _sum_parts(parts):
    n_parts, rows, cols = parts.shape

    def body(p_ref, o_ref):
        acc = p_ref[0]
        for n in range(1, n_parts):
            acc = acc + p_ref[n]
        o_ref[...] = acc

    return pl.pallas_call(body, name="sum_small", out_shape=jax.ShapeDtypeStruct((rows, cols), F32),
                          compiler_params=pltpu.CompilerParams(vmem_limit_bytes=VMEM_LIMIT_BYTES))(parts)


def _flatten(arrays, multiple):
    flat = jnp.concatenate([a.reshape(-1).astype(F32) for a in arrays])
    pad = (-flat.shape[0]) % multiple
    return jnp.pad(flat, (0, pad)).reshape(-1, LANES)


def _unflatten(flat2d, shapes):
    flat, out, off = flat2d.reshape(-1), [], 0
    for s in shapes:
        n = math.prod(s)
        out.append(flat[off:off + n].reshape(s))
        off += n
    return out


SMALL_NAMES = ("b_ada", "g_norm", "b_fgate", "conv_w", "conv_b", "w_rgate", "b_rgate", "w_igate", "b_igate",
               "lru_lambda", "g_qk", "g_mix_out")
WEIGHT_NAMES = ("w_ada", "b_ada", "g_norm", "w_ffn_up", "w_ffn_down", "w_in", "b_fgate", "conv_w", "conv_b",
                "w_rgate", "b_rgate", "w_igate", "b_igate", "lru_lambda", "g_qk", "g_mix_out", "w_out")


def kernel(x, c, w_ada, b_ada, g_norm, w_ffn_up, w_ffn_down, w_in, b_fgate, conv_w, conv_b, w_rgate, b_rgate, w_igate, b_igate, lru_lambda, g_qk, g_mix_out, w_out, loss_target, m_w_ada, m_b_ada, m_g_norm, m_w_ffn_up, m_w_ffn_down, m_w_in, m_b_fgate, m_conv_w, m_conv_b, m_w_rgate, m_b_rgate, m_w_igate, m_b_igate, m_lru_lambda, m_g_qk, m_g_mix_out, m_w_out, v_w_ada, v_b_ada, v_g_norm, v_w_ffn_up, v_w_ffn_down, v_w_in, v_b_fgate, v_conv_w, v_conv_b, v_w_rgate, v_b_rgate, v_w_igate, v_b_igate, v_lru_lambda, v_g_qk, v_g_mix_out, v_w_out):
    batch, seq, d = x.shape
    n_layers = w_ada.shape[0]
    me = 4 * lax.axis_index("x") + 2 * lax.axis_index("y") + lax.axis_index("c")
    weights = dict(w_ada=w_ada, b_ada=b_ada, g_norm=g_norm, w_ffn_up=w_ffn_up, w_ffn_down=w_ffn_down, w_in=w_in,
                   b_fgate=b_fgate, conv_w=conv_w, conv_b=conv_b, w_rgate=w_rgate, b_rgate=b_rgate, w_igate=w_igate,
                   b_igate=b_igate, lru_lambda=lru_lambda, g_qk=g_qk, g_mix_out=g_mix_out, w_out=w_out)
    moments_m = dict(w_ada=m_w_ada, b_ada=m_b_ada, g_norm=m_g_norm, w_ffn_up=m_w_ffn_up, w_ffn_down=m_w_ffn_down,
                     w_in=m_w_in, b_fgate=m_b_fgate, conv_w=m_conv_w, conv_b=m_conv_b, w_rgate=m_w_rgate,
                     b_rgate=m_b_rgate, w_igate=m_w_igate, b_igate=m_b_igate, lru_lambda=m_lru_lambda, g_qk=m_g_qk,
                     g_mix_out=m_g_mix_out, w_out=m_w_out)
    moments_v = dict(w_ada=v_w_ada, b_ada=v_b_ada, g_norm=v_g_norm, w_ffn_up=v_w_ffn_up, w_ffn_down=v_w_ffn_down,
                     w_in=v_w_in, b_fgate=v_b_fgate, conv_w=v_conv_w, conv_b=v_conv_b, w_rgate=v_w_rgate,
                     b_rgate=v_b_rgate, w_igate=v_w_igate, b_igate=v_b_igate, lru_lambda=v_lru_lambda, g_qk=v_g_qk,
                     g_mix_out=v_g_mix_out, w_out=v_w_out)

    w_in_pad = jnp.pad(w_in, ((0, 0), (0, 0), (0, N_IN_PAD - N_IN))).astype(BF16)
    c_all, gn_all, cw_all, w_up_full, down_all, in_all, out_all = _exchange(
        [(c, 0), (g_norm, 0), (conv_w, 0), (w_ffn_up.astype(BF16), 2), (w_ffn_down.astype(BF16), 2), (w_in_pad, 1),
         (w_out.astype(BF16), 1)], [], "gather_weights", two_level=True)
    c_all = c_all.reshape(N_DEV * batch, d)
    n_ada = w_ada.shape[-1]
    g_norm_full = gn_all.transpose(1, 2, 0, 3).reshape(n_layers, 3, d)
    conv_w_full = cw_all.transpose(1, 2, 0, 3).reshape(n_layers, 4, LRU_W)
    w_down_full = down_all.reshape(n_layers, 2, D_FF, d)
    w_in_full = in_all.reshape(n_layers, d, N_IN_PAD)
    w_out_full = out_all.reshape(n_layers, d, d)

    b_ada_loc = lax.dynamic_slice_in_dim(b_ada, me * n_ada, n_ada, axis=1)
    silu = lambda t: t * _sigmoid(t)

    def bias_epilogue(p, e_refs, o_refs):
        o_refs[0][...] = p + e_refs[0][...]

    mod_loc = []
    for l in range(n_layers):
        (ml,) = _mm(c_all, w_ada, mode="nn", tm=c_all.shape[0], tn=n_ada, tk=d, b_lead=(l,), a_pre=silu,
                    name=f"ada_{l}", extras=[(b_ada_loc[l][None, :], (1, n_ada), lambda i, j: (0, 0))],
                    outs=[((c_all.shape[0], n_ada), F32, (c_all.shape[0], n_ada), lambda i, j: (0, 0))],
                    epilogue=bias_epilogue)
        mod_loc.append(ml)
    (mod_all,) = _exchange([(jnp.stack(mod_loc), 0)], [], "gather_mod")
    mod_all = mod_all.transpose(1, 2, 0, 3).reshape(n_layers, N_DEV * batch, 9 * d)
    mod_me = lax.dynamic_slice_in_dim(mod_all, me * batch, batch, axis=1).reshape(n_layers, batch, 3, 3, d)

    wts = dict(w_up=w_up_full, w_down=w_down_full, w_in=w_in_full, w_out=w_out_full, g_norm=g_norm_full,
               conv_w=conv_w_full, conv_b=conv_b, w_rgate=w_rgate, b_rgate=b_rgate, w_igate=w_igate, b_igate=b_igate,
               lru_lambda=lru_lambda, g_qk=g_qk, g_mix_out=g_mix_out, b_fgate=b_fgate)
    loss_part, grad_x, handles, small = _local_step(x, loss_target, mod_me, wts)

    dmod_me = small.pop("dmod").reshape(n_layers, batch, 9 * d)
    small["b_ada"] = jnp.sum(dmod_me, axis=1)
    small_shapes = [(1,)] + [weights[k].shape if k not in ("g_norm", "conv_w") else small[k].shape for k in SMALL_NAMES]
    small_flat = _flatten([loss_part.reshape(1)] + [small[k] for k in SMALL_NAMES], 16 * LANES)
    dmod_all, small_all = _exchange([(dmod_me, 0), (small_flat, 0)], [], "gather_small")
    landed = [_scatter_wait(handles[l], small_all, f"grads_wait_{l}") for l in reversed(range(n_layers))][::-1]
    p_up, p_down, p_in, p_out = (jnp.stack([landed[l][n] for l in range(n_layers)]) for n in range(4))
    small_sum = _unflatten(_sum_parts(small_all), small_shapes)
    loss = small_sum[0].reshape(())
    small_grads = dict(zip(SMALL_NAMES, small_sum[1:]))
    small_grads["g_norm"] = lax.dynamic_slice_in_dim(small_grads["g_norm"], me * g_norm.shape[-1], g_norm.shape[-1], 2)
    small_grads["conv_w"] = lax.dynamic_slice_in_dim(small_grads["conv_w"], me * conv_w.shape[-1], conv_w.shape[-1], 2)

    dmod_all = dmod_all.transpose(1, 0, 2, 3).reshape(n_layers, N_DEV * batch, 9 * d)
    dmod_loc = lax.dynamic_slice_in_dim(dmod_all, me * n_ada, n_ada, axis=2)
    g_ada = []
    for l in range(n_layers):
        (gl,) = _mm(c_all, dmod_loc[l], mode="tn", tm=d, tn=n_ada, tk=c_all.shape[0], a_pre=silu, name=f"dw_ada_{l}",
                    outs=[((d, n_ada), F32, (d, n_ada), lambda i, j: (0, 0))], epilogue=_store_epilogue([F32]))
        g_ada.append(gl)
    g_ada = jnp.stack(g_ada)

    results = {}

    def update(name, parts):
        shape = weights[name].shape
        as3d = lambda t: t.reshape((-1,) + shape[-2:])
        outs = _adamw(parts.reshape((-1,) + parts.shape[-3:]), as3d(weights[name]), as3d(moments_m[name]),
                      as3d(moments_v[name]), f"adamw_{name}")
        results[name] = [o.reshape(shape) for o in outs]

    update("w_ada", g_ada[:, None])
    update("w_ffn_up", p_up)
    update("w_ffn_down", p_down)
    update("w_in", p_in[..., :N_IN])
    update("w_out", p_out)
    sm_shapes = [weights[k].shape for k in SMALL_NAMES]
    flat = lambda src: _flatten([src[k] for k in SMALL_NAMES], 16 * LANES)
    sm_out = _adamw(flat(small_grads)[None, None], flat(weights)[None], flat(moments_m)[None], flat(moments_v)[None],
                    "adamw_small")
    for k, vals in zip(SMALL_NAMES, zip(*[_unflatten(o, sm_shapes) for o in sm_out])):
        results[k] = list(vals)

    outs = [loss, grad_x]
    for n in range(4):
        outs += [results[k][n] for k in WEIGHT_NAMES]
    return tuple(outs)
```

```python
import functools
import math

import jax
import jax.numpy as jnp
from jax import lax
from jax.experimental import pallas as pl
from jax.experimental.pallas import tpu as pltpu

F32 = jnp.float32
BF16 = jnp.bfloat16

N_DEV = 8
D_MODEL = 1024
D_FF = 2816
FF_SHARD = 2 * D_FF // N_DEV
N_FF_SHARD = D_FF // FF_SHARD
HEAD_DIM = 64
LRU_W = 512
ATT_W = 256
N_IN = 2564
N_IN_PAD = 2688
LANES = 128
SUBLANES = 8
BLK = 256
TQ = 512
KB_PER_Q = TQ // BLK
EPS = 1e-6
LRU_C = 8.0
NEG_BIG = -1e30
VMEM_LIMIT_BYTES = 48 * 1024 * 1024

ADAM_LR, ADAM_B1, ADAM_B2, ADAM_EPS, ADAM_WD, ADAM_STEP = 0.001, 0.9, 0.999, 1e-08, 0.01, 10

COL_SBQ, COL_SBK, COL_SBV = 8, 10, 12
COL_FXV, COL_FXF = 18, 20

NN = (((1,), (0,)), ((), ()))
NT = (((1,), (1,)), ((), ()))
TN = (((0,), (0,)), ((), ()))


def _params(n_axes):
    return pltpu.CompilerParams(dimension_semantics=("arbitrary",) * n_axes, vmem_limit_bytes=VMEM_LIMIT_BYTES)


def _tok_tile(seq):
    for t in (512, 256, 128):
        if seq % t == 0:
            return t
    raise ValueError(f"sequence length {seq} is not a multiple of 128")


def _dot(a, b, dims=NN):
    return lax.dot_general(a, b, dims, preferred_element_type=F32)


def _sigmoid(x):
    return 1.0 / (1.0 + jnp.exp(-x))


def _softplus(x):
    return jnp.maximum(x, 0.0) + jnp.log(1.0 + jnp.exp(-jnp.abs(x)))


def _gelu_parts(x):
    k0, k1 = math.sqrt(2.0 / math.pi), 0.044715
    t = jnp.tanh(k0 * (x + k1 * x * x * x))
    gelu = 0.5 * x * (1.0 + t)
    dgelu = 0.5 * (1.0 + t) + 0.5 * x * (1.0 - t * t) * k0 * (1.0 + 3.0 * k1 * x * x)
    return gelu, dgelu


def _neg_expm1(x):
    series = -x * (1.0 + x * (0.5 + x * (1.0 / 6.0 + x * (1.0 / 24.0 + x * (1.0 / 120.0 + x * (1.0 / 720.0))))))
    return jnp.where(x > -0.25, series, 1.0 - jnp.exp(x))


def _split2(x):
    hi = x.astype(BF16)
    lo = (x - hi.astype(F32)).astype(BF16)
    return hi, lo


def _split3(x):
    hi = x.astype(BF16)
    r = x - hi.astype(F32)
    mid = r.astype(BF16)
    lo = (r - mid.astype(F32)).astype(BF16)
    return hi, mid, lo


def _rows_to_block(rows, width):
    r = lax.broadcasted_iota(jnp.int32, (SUBLANES, width), 0)
    out = jnp.zeros((SUBLANES, width), F32)
    for n, v in enumerate(rows):
        out = jnp.where(r == n, jnp.broadcast_to(v, (SUBLANES, width)), out)
    return out


def _colsum(x):
    return jnp.sum(x, axis=0, keepdims=True)


def _exchange(gathers, scatters, name, two_level=False):
    assert not (two_level and scatters)
    n_g = len(gathers)
    ops = [a for a, _ in gathers] + [a for a, _ in scatters]
    n = len(ops)
    out_shape = [jax.ShapeDtypeStruct(a.shape[:nl] + (N_DEV,) + a.shape[nl:], a.dtype) for a, nl in gathers]
    out_shape += [jax.ShapeDtypeStruct(a.shape, a.dtype) for a, _ in scatters]
    items = []
    for k, (a, nl) in enumerate(list(gathers) + list(scatters)):
        for flat in range(math.prod(a.shape[:nl])):
            idx, rem = [], flat
            for dim in reversed(a.shape[:nl]):
                idx.insert(0, rem % dim)
                rem //= dim
            items.append((k, tuple(idx)))
    n_items = len(items)

    def body(*refs):
        ins, outs = refs[:n], refs[n:2 * n]
        send_sems, recv_sems, local_sems = refs[2 * n:]
        x, y, c = lax.axis_index("x"), lax.axis_index("y"), lax.axis_index("c")
        me = 4 * x + 2 * y + c

        def at(ref, idx):
            return ref.at[idx] if idx else ref

        def src(it, peer):
            k, idx = items[it]
            return at(ins[k], idx) if k < n_g else at(ins[k], idx + (peer,))

        def slot(it, s):
            k, idx = items[it]
            return at(outs[k], idx + (s,))

        def remote(it, rel, source, s, to):
            return pltpu.make_async_remote_copy(
                src_ref=source, dst_ref=slot(it, s), send_sem=send_sems.at[it, rel], recv_sem=recv_sems.at[it, rel],
                device_id=to, device_id_type=pl.DeviceIdType.MESH)

        local = [pltpu.make_async_copy(src(it, me), slot(it, me), local_sems.at[it]) for it in range(n_items)]
        for cp in local:
            cp.start()

        if not two_level:
            started = []
            for r in range(1, N_DEV):
                px = 1 - x if (r >> 2) & 1 else x
                py = 1 - y if (r >> 1) & 1 else y
                pc = 1 - c if r & 1 else c
                for it in range(n_items):
                    cp = remote(it, r - 1, src(it, 4 * px + 2 * py + pc), me, (px, py, pc))
                    cp.start()
                    started.append(cp)
            for cp in started:
                cp.wait()
        else:
            sibling, chips = (x, y, 1 - c), [(1 - x, y), (x, 1 - y), (1 - x, 1 - y)]
            sib = 4 * x + 2 * y + (1 - c)
            started = []
            for it in range(n_items):
                started.append(remote(it, 0, src(it, me), me, sibling))
                started += [remote(it, 1 + j, src(it, me), me, (cx, cy, c)) for j, (cx, cy) in enumerate(chips)]
            for cp in started:
                cp.start()
            for j, (cx, cy) in enumerate(chips):
                s = 4 * cx + 2 * cy + c
                for it in range(n_items):
                    remote(it, 1 + j, slot(it, s), s, sibling).wait_recv()
                    cp = remote(it, 4 + j, slot(it, s), s, sibling)
                    cp.start()
                    started.append(cp)
            for it in range(n_items):
                remote(it, 0, slot(it, sib), sib, sibling).wait_recv()
                for j, (cx, cy) in enumerate(chips):
                    s = 4 * cx + 2 * cy + (1 - c)
                    remote(it, 4 + j, slot(it, s), s, sibling).wait_recv()
            for cp in started:
                cp.wait_send()
        for cp in local:
            cp.wait()

    hbm = pl.BlockSpec(memory_space=pltpu.HBM)
    return pl.pallas_call(
        body, name=name, out_shape=out_shape,
        in_specs=[hbm] * n, out_specs=[hbm] * n,
        scratch_shapes=[pltpu.SemaphoreType.DMA((n_items, N_DEV - 1)), pltpu.SemaphoreType.DMA((n_items, N_DEV - 1)),
                        pltpu.SemaphoreType.DMA((n_items,))],
    )(*ops)


def _lead_items(ops):
    items = []
    for k, (a, nl) in enumerate(ops):
        for flat in range(math.prod(a.shape[:nl])):
            idx, rem = [], flat
            for dim in reversed(a.shape[:nl]):
                idx.insert(0, rem % dim)
                rem //= dim
            items.append((k, tuple(idx)))
    return items


def _flight_copies(ops, srcs, lands, send_sems, recv_sems):
    x, y, c = lax.axis_index("x"), lax.axis_index("y"), lax.axis_index("c")
    me = 4 * x + 2 * y + c
    copies = []
    for r in range(1, N_DEV):
        px = 1 - x if (r >> 2) & 1 else x
        py = 1 - y if (r >> 1) & 1 else y
        pc = 1 - c if r & 1 else c
        for it, (k, idx) in enumerate(_lead_items([(a, nl) for a, nl, _ in ops])):
            src = srcs[k].at[idx + (4 * px + 2 * py + pc,)] if ops[k][2] == "scatter" else (
                srcs[k].at[idx] if idx else srcs[k])
            copies.append(pltpu.make_async_remote_copy(
                src_ref=src, dst_ref=lands[k].at[idx + (me,)],
                send_sem=send_sems.at[it * (N_DEV - 1) + r - 1], recv_sem=recv_sems.at[it * (N_DEV - 1) + r - 1],
                device_id=(px, py, pc), device_id_type=pl.DeviceIdType.MESH))
    return copies


def _flight_start(ops, name):
    n = len(ops)
    me = 4 * lax.axis_index("x") + 2 * lax.axis_index("y") + lax.axis_index("c")
    srcs, lands = [], []
    for a, nl, kind in ops:
        if kind == "scatter":
            own, shape = lax.dynamic_slice_in_dim(a, me, 1, axis=nl), a.shape
        else:
            own, shape = jnp.expand_dims(a, nl), a.shape[:nl] + (N_DEV,) + a.shape[nl:]
        start = (0,) * nl + (me,) + (0,) * (len(shape) - nl - 1)
        lands.append(pltpu.with_memory_space_constraint(
            lax.dynamic_update_slice(lax.empty(shape, a.dtype), own, start), pltpu.HBM))
        srcs.append(pltpu.with_memory_space_constraint(a, pltpu.HBM))

    def body(*refs):
        for cp in _flight_copies(ops, refs[:n], refs[n:2 * n], refs[2 * n], refs[2 * n + 1]):
            cp.start()
        refs[-1][...] = jnp.zeros_like(refs[-1])

    hbm, sem = pl.BlockSpec(memory_space=pltpu.HBM), pl.BlockSpec(memory_space=pltpu.SEMAPHORE)
    n_items = len(_lead_items([(a, nl) for a, nl, _ in ops]))
    sems = pltpu.SemaphoreType.DMA((n_items * (N_DEV - 1),))
    res = pl.pallas_call(
        body, name=name,
        out_shape=[sems, sems] + [pltpu.HBM(a.shape, a.dtype) for a in srcs + lands]
        + [jax.ShapeDtypeStruct((SUBLANES, LANES), F32)],
        in_specs=[hbm] * (2 * n), out_specs=[sem, sem] + [hbm] * (2 * n) + [pl.BlockSpec(memory_space=pltpu.VMEM)],
        input_output_aliases={i: 2 + i for i in range(2 * n)},
        compiler_params=pltpu.CompilerParams(has_side_effects=pltpu.SideEffectType.DATAFLOW_SIDE_EFFECTING),
    )(*srcs, *lands)
    return (ops, res[0], res[1], res[2:2 + n], res[2 + n:2 + 2 * n]), res[-1]


def _flight_wait(handle, after, name):
    ops, send_sems, recv_sems, srcs, lands = handle
    n = len(ops)

    def body(*refs):
        for cp in _flight_copies(ops, refs[:n], refs[n:2 * n], refs[2 * n], refs[2 * n + 1]):
            cp.wait_send()
            cp.wait_recv()

    hbm, sem = pl.BlockSpec(memory_space=pltpu.HBM), pl.BlockSpec(memory_space=pltpu.SEMAPHORE)
    res = pl.pallas_call(
        body, name=name, out_shape=[pltpu.HBM(a.shape, a.dtype) for a in list(srcs) + list(lands)],
        in_specs=[hbm] * (2 * n) + [sem, sem, pl.BlockSpec(memory_space=pl.ANY)], out_specs=[hbm] * (2 * n),
        input_output_aliases={i: i for i in range(2 * n)},
        compiler_params=pltpu.CompilerParams(has_side_effects=pltpu.SideEffectType.DATAFLOW_SIDE_EFFECTING),
    )(*srcs, *lands, send_sems, recv_sems, after)
    return res[n:]


def _mm(a, b, *, mode, tm, tn, tk, outs, epilogue, name, extras=(), a_lead=(), b_lead=(), a_pre=None,
        a_spec=None, b_spec=None, shape=None, ksub=1):
    if shape is not None:
        mdim, ndim, kdim = shape
    else:
        if mode == "tn":
            kdim, mdim = a.shape[-2:]
        else:
            mdim, kdim = a.shape[-2:]
        ndim = b.shape[-2] if mode == "nt" else b.shape[-1]
    assert mdim % tm == 0 and ndim % tn == 0 and kdim % tk == 0, (name, mdim, ndim, kdim, tm, tn, tk)
    ni, nj, nk = mdim // tm, ndim // tn, kdim // tk
    a_lead, b_lead = tuple(a_lead), tuple(b_lead)
    a_block = (None,) * len(a_lead) + ((tk, tm) if mode == "tn" else (tm, tk))
    b_block = (None,) * len(b_lead) + ((tn, tk) if mode == "nt" else (tk, tn))
    dims = {"nn": NN, "nt": NT, "tn": TN}[mode]
    ne, no = len(extras), len(outs)

    def a_index(i, j, k):
        return a_lead + ((k, i) if mode == "tn" else (i, k))

    def b_index(i, j, k):
        return b_lead + ((j, k) if mode == "nt" else (k, j))

    if a_spec is not None:
        a_block, a_index = a_spec
    if b_spec is not None:
        b_block, b_index = b_spec

    def body(*refs):
        a_ref, b_ref = refs[0], refs[1]
        e_refs, o_refs = refs[2:2 + ne], refs[2 + ne:2 + ne + no]
        if ksub == 1:
            av = a_ref[...] if a_pre is None else a_pre(a_ref[...])
            p = _dot(av.astype(BF16), b_ref[...].astype(BF16), dims)
        else:
            p = _dot(a_ref[0], b_ref[0], dims)
            for s in range(1, ksub):
                p = p + _dot(a_ref[s], b_ref[s], dims)
        if nk == 1:
            epilogue(p, e_refs, o_refs)
        else:
            acc = refs[-1]
            k = pl.program_id(2)

            @pl.when(k == 0)
            def _():
                acc[...] = p

            @pl.when(k > 0)
            def _():
                acc[...] += p

            @pl.when(k == nk - 1)
            def _():
                epilogue(acc[...], e_refs, o_refs)

    in_specs = [pl.BlockSpec(a_block, a_index), pl.BlockSpec(b_block, b_index)]
    in_specs += [pl.BlockSpec(blk, functools.partial(lambda i, j, k, f: f(i, j), f=f)) for _, blk, f in extras]
    out_specs = [pl.BlockSpec(blk, functools.partial(lambda i, j, k, f: f(i, j), f=f)) for _, _, blk, f in outs]
    res = pl.pallas_call(
        body, name=name, grid=(ni, nj, nk), in_specs=in_specs, out_specs=out_specs,
        out_shape=[jax.ShapeDtypeStruct(s, d) for s, d, _, _ in outs],
        scratch_shapes=[pltpu.VMEM((tm, tn), F32)] if nk > 1 else [],
        compiler_params=_params(3),
    )(a, b, *[e[0] for e in extras])
    return res


def _store_epilogue(dtypes):
    def epi(p, e_refs, o_refs):
        for o, dt in zip(o_refs, dtypes):
            o[...] = p.astype(dt)
    return epi


def _normmod(x, gn, scale, shift, seq, name):
    m, d = x.shape
    tm = _tok_tile(seq)
    tpb = seq // tm

    def body(x_ref, gn_ref, sc_ref, sh_ref, h_ref):
        xv = x_ref[...]
        rstd = lax.rsqrt(jnp.mean(xv * xv, axis=-1, keepdims=True) + EPS)
        h_ref[...] = (xv * rstd * gn_ref[...] * (1.0 + sc_ref[0]) + sh_ref[0]).astype(BF16)

    vec = pl.BlockSpec((1, 1, d), lambda i: (i // tpb, 0, 0))
    return pl.pallas_call(
        body, name=name, grid=(m // tm,),
        in_specs=[pl.BlockSpec((tm, d), lambda i: (i, 0)), pl.BlockSpec((1, d), lambda i: (0, 0)), vec, vec],
        out_specs=pl.BlockSpec((tm, d), lambda i: (i, 0)),
        out_shape=jax.ShapeDtypeStruct((m, d), BF16), compiler_params=_params(1),
    )(x, gn, scale, shift)


def _normmod_bwd_epilogue(p, e_refs, o_refs):
    x_ref, dxo_ref, gn_ref, sc_ref = e_refs
    xv = x_ref[...]
    rstd = lax.rsqrt(jnp.mean(xv * xv, axis=-1, keepdims=True) + EPS)
    xhat = xv * rstd
    gn, sc1 = gn_ref[...], 1.0 + sc_ref[0]
    dxhat = p * (gn * sc1)
    dx = rstd * (dxhat - xhat * jnp.mean(dxhat * xhat, axis=-1, keepdims=True))
    o_refs[0][...] = dxo_ref[...] + dx
    t = p * xhat
    o_refs[1][0] = _rows_to_block([_colsum(p), _colsum(t * gn), _colsum(t * sc1)], p.shape[1])


def _residual_bwd(dx, f, gate, fac, seq, name):
    m, d = dx.shape
    tm = _tok_tile(seq)
    tpb = seq // tm

    def body(dx_ref, f_ref, g_ref, df_ref, dg_ref):
        dxv = dx_ref[...]
        df_ref[...] = ((fac * (1.0 + g_ref[0])) * dxv).astype(BF16)
        dg_ref[0] = _rows_to_block([_colsum((fac * dxv) * f_ref[...].astype(F32))], d)

    tile = pl.BlockSpec((tm, d), lambda i: (i, 0))
    return pl.pallas_call(
        body, name=name, grid=(m // tm,),
        in_specs=[tile, tile, pl.BlockSpec((1, 1, d), lambda i: (i // tpb, 0, 0))],
        out_specs=[tile, pl.BlockSpec((1, SUBLANES, d), lambda i: (i, 0, 0))],
        out_shape=[jax.ShapeDtypeStruct((m, d), BF16), jax.ShapeDtypeStruct((m // tm, SUBLANES, d), F32)],
        compiler_params=_params(1),
    )(dx, f, gate)


def _loss_head(y, target, seq):
    m, d = y.shape
    tm = _tok_tile(seq)

    def body(y_ref, t_ref, dy_ref, l_ref):
        err = y_ref[...] - t_ref[...]
        dy_ref[...] = err * (1.0 / d)
        part = 0.5 * jnp.sum(jnp.mean(err * err, axis=-1, keepdims=True), axis=0, keepdims=True)
        l_ref[0] = jnp.broadcast_to(part, (SUBLANES, LANES))

    tile = pl.BlockSpec((tm, d), lambda i: (i, 0))
    return pl.pallas_call(
        body, name="loss_head", grid=(m // tm,), in_specs=[tile, tile],
        out_specs=[tile, pl.BlockSpec((1, SUBLANES, LANES), lambda i: (i, 0, 0))],
        out_shape=[jax.ShapeDtypeStruct((m, d), F32), jax.ShapeDtypeStruct((m // tm, SUBLANES, LANES), F32)],
        compiler_params=_params(1),
    )(y, target)


def _ffn_fwd(x, h, wup, wdown, gate, seq, tag):
    m, d = x.shape
    tm = _tok_tile(seq)
    tpb = seq // tm

    def up_body(h_ref, wg_ref, wu_ref, a_ref, gu_ref):
        hv = h_ref[...]
        g, u = _dot(hv, wg_ref[...]), _dot(hv, wu_ref[...])
        a_ref[...] = (g * _sigmoid(g) * u).astype(BF16)
        gu_ref[0] = g.astype(BF16)
        gu_ref[1] = u.astype(BF16)

    wblk = (None, d, FF_SHARD)
    a, gu = pl.pallas_call(
        up_body, name=f"ffn_up_{tag}", grid=(N_FF_SHARD, m // tm),
        in_specs=[pl.BlockSpec((tm, d), lambda j, i: (i, 0)),
                  pl.BlockSpec(wblk, lambda j, i: (j, 0, 0)),
                  pl.BlockSpec(wblk, lambda j, i: (j + N_FF_SHARD, 0, 0))],
        out_specs=[pl.BlockSpec((None, tm, FF_SHARD), lambda j, i: (j, i, 0)),
                   pl.BlockSpec((2, None, tm, FF_SHARD), lambda j, i: (0, j, i, 0))],
        out_shape=[jax.ShapeDtypeStruct((N_FF_SHARD, m, FF_SHARD), BF16),
                   jax.ShapeDtypeStruct((2, N_FF_SHARD, m, FF_SHARD), BF16)],
        compiler_params=_params(2),
    )(h, wup, wup)

    def down_epilogue(p, e_refs, o_refs):
        x_ref, g_ref = e_refs
        o_refs[0][...] = x_ref[...] + (0.5 * (1.0 + g_ref[0])) * p
        o_refs[1][...] = p.astype(BF16)

    wdown3 = wdown.reshape(N_FF_SHARD, FF_SHARD, d)
    x_out, f = _mm(a, wdown3, mode="nn", tm=tm, tn=d, tk=D_FF, ksub=N_FF_SHARD, name=f"ffn_down_{tag}",
                   shape=(m, d, D_FF), a_spec=((N_FF_SHARD, tm, FF_SHARD), lambda i, j, k: (0, i, 0)),
                   b_spec=((N_FF_SHARD, FF_SHARD, d), lambda i, j, k: (0, 0, 0)),
                   extras=[(x, (tm, d), lambda i, j: (i, 0)), (gate, (1, 1, d), lambda i, j: (i // tpb, 0, 0))],
                   outs=[((m, d), F32, (tm, d), lambda i, j: (i, 0)), ((m, d), BF16, (tm, d), lambda i, j: (i, 0))],
                   epilogue=down_epilogue)
    return x_out, (a, gu, f)


def _ffn_bwd(dx_out, x, h, saved, wup, wdown, gn, scale, gate, seq, tag, on_grads):
    a, gu, f = saved
    m, d = x.shape
    tm = _tok_tile(seq)
    tpb = seq // tm
    df, dgate_parts = _residual_bwd(dx_out, f, gate, 0.5, seq, f"ffn_res_bwd_{tag}")

    def act_bwd_epilogue(p, e_refs, o_refs):
        g, u = e_refs[0][0].astype(F32), e_refs[0][1].astype(F32)
        sg = _sigmoid(g)
        o_refs[0][0] = (p * u * (sg * (1.0 + g * (1.0 - sg)))).astype(BF16)
        o_refs[0][1] = (p * (g * sg)).astype(BF16)

    gu_blk = (2, None, tm, FF_SHARD)
    (dgu,) = _mm(df, wdown, mode="nt", tm=tm, tn=FF_SHARD, tk=d, name=f"ffn_down_dx_{tag}", shape=(m, D_FF, d),
                 b_spec=((FF_SHARD, d), lambda i, j, k: (j, 0)),
                 extras=[(gu, gu_blk, lambda i, j: (0, j, i, 0))],
                 outs=[((2, N_FF_SHARD, m, FF_SHARD), BF16, gu_blk, lambda i, j: (0, j, i, 0))],
                 epilogue=act_bwd_epilogue)
    tt = 2 * tm if m % (2 * tm) == 0 else tm
    (dwdown,) = _mm(a, df, mode="tn", tm=FF_SHARD, tn=d, tk=tt, name=f"ffn_dwdown_{tag}", shape=(D_FF, d, m),
                    a_spec=((None, tt, FF_SHARD), lambda i, j, k: (i, k, 0)),
                    outs=[((D_FF, d), BF16, (FF_SHARD, d), lambda i, j: (i, 0))], epilogue=_store_epilogue([BF16]))
    dgu8 = dgu.reshape(2 * N_FF_SHARD, m, FF_SHARD)
    (dwup,) = _mm(h, dgu8, mode="tn", tm=d, tn=FF_SHARD, tk=tt, name=f"ffn_dwup_{tag}", shape=(d, 2 * D_FF, m),
                  b_spec=((None, tt, FF_SHARD), lambda i, j, k: (j, k, 0)),
                  outs=[((2 * N_FF_SHARD, d, FF_SHARD), BF16, (None, d, FF_SHARD), lambda i, j: (j, 0, 0))],
                  epilogue=_store_epilogue([BF16]))
    scale = scale + on_grads(dwup, dwdown)[0, 0]
    dx, nm_parts = _mm(dgu8, wup, mode="nt", tm=tm, tn=d, tk=D_FF, ksub=N_FF_SHARD, name=f"ffn_up_dx_{tag}",
                       shape=(m, d, 2 * D_FF), a_spec=((N_FF_SHARD, tm, FF_SHARD), lambda i, j, k: (k, i, 0)),
                       b_spec=((N_FF_SHARD, d, FF_SHARD), lambda i, j, k: (k, 0, 0)),
                       extras=[(x, (tm, d), lambda i, j: (i, 0)), (dx_out, (tm, d), lambda i, j: (i, 0)),
                               (gn, (1, d), lambda i, j: (0, 0)), (scale, (1, 1, d), lambda i, j: (i // tpb, 0, 0))],
                       outs=[((m, d), F32, (tm, d), lambda i, j: (i, 0)),
                             ((m // tm, SUBLANES, d), F32, (1, SUBLANES, d), lambda i, j: (i, 0, 0))],
                       epilogue=_normmod_bwd_epilogue)
    return dx, nm_parts, dgate_parts


def _shift_down(ext, n, rows):
    if n:
        ext = pltpu.roll(ext, n, 0)
    return ext[SUBLANES:SUBLANES + rows]


def _lru_gates(u, wr_ref, br_ref, wi_ref, bi_ref, lam_ref):
    ub = u.astype(BF16)
    r = _sigmoid(_dot(ub, wr_ref[...]) + br_ref[...])
    ig = _sigmoid(_dot(ub, wi_ref[...]) + bi_ref[...])
    sp = _softplus(-lam_ref[...])
    log_a = (-LRU_C * r) * sp
    a = jnp.exp(log_a)
    mult = jnp.sqrt(_neg_expm1(2.0 * log_a))
    return r, ig, sp, a, mult


def _conv(ext, cw_ref, cb_ref, rows):
    u = cb_ref[...] + cw_ref[3:4, :] * _shift_down(ext, 0, rows)
    for k in range(3):
        u = u + cw_ref[k:k + 1, :] * _shift_down(ext, 3 - k, rows)
    return u


def _lru_halo_spec(seq, ts):
    return pl.BlockSpec((SUBLANES, LRU_W),
                        lambda b, i: (jnp.maximum(b * (seq // SUBLANES) + i * (ts // SUBLANES) - 1, 0), 0))


def _lru_fwd(proj32, conv_w, conv_b, wr, br, wi, bi, lam, batch, seq):
    m = proj32.shape[0]
    ts = _tok_tile(seq)
    nt = seq // ts
    row = lambda b, i: (b * nt + i, 0)

    def body(x_ref, halo_ref, g_ref, cw_ref, cb_ref, wr_ref, br_ref, wi_ref, bi_ref, lam_ref,
             y_ref, h_ref, a_scr, b_scr, carry):
        i = pl.program_id(1)
        halo = jnp.where(i > 0, halo_ref[...], 0.0)
        ext = jnp.concatenate([halo, x_ref[...]], axis=0)
        u = _conv(ext, cw_ref, cb_ref, ts)
        _, ig, _, a, mult = _lru_gates(u, wr_ref, br_ref, wi_ref, bi_ref, lam_ref)
        a_scr[...] = a
        b_scr[...] = mult * (ig * u)

        @pl.when(i == 0)
        def _():
            carry[...] = jnp.zeros_like(carry)

        rid = lax.broadcasted_iota(jnp.int32, (SUBLANES, LRU_W), 0)

        def chunk(c, hprev):
            off = pl.multiple_of(c * SUBLANES, SUBLANES)
            av, bv = a_scr[pl.ds(off, SUBLANES), :], b_scr[pl.ds(off, SUBLANES), :]
            for d in (1, 2, 4):
                keep = rid >= d
                bv = jnp.where(keep, av * pltpu.roll(bv, d, 0) + bv, bv)
                av = jnp.where(keep, av * pltpu.roll(av, d, 0), av)
            h = av * hprev + bv
            h_ref[pl.ds(off, SUBLANES), :] = h
            return h[SUBLANES - 1:SUBLANES, :]

        carry[...] = lax.fori_loop(0, ts // SUBLANES, chunk, carry[...])
        gelu, _ = _gelu_parts(g_ref[...])
        y_ref[...] = h_ref[...] * gelu

    full = lambda shape: pl.BlockSpec(shape, lambda b, i: (0,) * len(shape))
    return pl.pallas_call(
        body, name="lru_fwd", grid=(batch, nt),
        in_specs=[pl.BlockSpec((ts, LRU_W), row), _lru_halo_spec(seq, ts),
                  pl.BlockSpec((ts, LRU_W), lambda b, i: (b * nt + i, 1)),
                  full((4, LRU_W)), full((1, LRU_W)), full((LRU_W, LRU_W)), full((1, LRU_W)),
                  full((LRU_W, LRU_W)), full((1, LRU_W)), full((1, LRU_W))],
        out_specs=[pl.BlockSpec((ts, LRU_W), row), pl.BlockSpec((ts, LRU_W), row)],
        out_shape=[jax.ShapeDtypeStruct((m, LRU_W), F32), jax.ShapeDtypeStruct((m, LRU_W), F32)],
        scratch_shapes=[pltpu.VMEM((ts, LRU_W), F32), pltpu.VMEM((ts, LRU_W), F32), pltpu.VMEM((1, LRU_W), F32)],
        compiler_params=_params(2),
    )(proj32, proj32, proj32, conv_w, conv_b, wr, br, wi, bi, lam)


def _lru_bwd(dy, proj32, h, conv_w, conv_b, wr, br, wi, bi, lam, batch, seq):
    m = proj32.shape[0]
    ts = _tok_tile(seq)
    nt = seq // ts
    row = lambda b, i: (b * nt + (nt - 1 - i), 0)
    halo = pl.BlockSpec((SUBLANES, LRU_W),
                        lambda b, i: (jnp.maximum(b * (seq // SUBLANES) + (nt - 1 - i) * (ts // SUBLANES) - 1, 0), 0))

    def body(dy_ref, x_ref, xhalo_ref, g_ref, h_ref, hhalo_ref, cw_ref, cb_ref, wr_ref, br_ref, wi_ref, bi_ref,
             lam_ref, dx_ref, dg_ref, dwr_ref, dwi_ref, sums_ref, a_scr, dh_scr, g_scr, carry, du_next):
        b, i = pl.program_id(0), pl.program_id(1)
        first_tile = i == nt - 1

        @pl.when((b == 0) & (i == 0))
        def _():
            dwr_ref[...] = jnp.zeros_like(dwr_ref)
            dwi_ref[...] = jnp.zeros_like(dwi_ref)
            sums_ref[...] = jnp.zeros_like(sums_ref)

        @pl.when(i == 0)
        def _():
            carry[...] = jnp.zeros_like(carry)
            du_next[...] = jnp.zeros_like(du_next)

        xhalo = jnp.where(first_tile, 0.0, xhalo_ref[...])
        ext = jnp.concatenate([xhalo, x_ref[...]], axis=0)
        u = _conv(ext, cw_ref, cb_ref, ts)
        r, ig, sp, a, mult = _lru_gates(u, wr_ref, br_ref, wi_ref, bi_ref, lam_ref)
        gelu, dgelu = _gelu_parts(g_ref[...])
        dyv, hv = dy_ref[...], h_ref[...]
        dg_ref[...] = (dyv * hv * dgelu).astype(BF16)
        a_scr[...] = a
        dh_scr[...] = dyv * gelu

        rid = lax.broadcasted_iota(jnp.int32, (SUBLANES, LRU_W), 0)
        nchunk = ts // SUBLANES

        def chunk(n, cg):
            off = pl.multiple_of((nchunk - 1 - n) * SUBLANES, SUBLANES)
            av, beta = a_scr[pl.ds(off, SUBLANES), :], dh_scr[pl.ds(off, SUBLANES), :]
            alpha = jnp.where(rid == SUBLANES - 1, 1.0, pltpu.roll(av, SUBLANES - 1, 0))
            for d in (1, 2, 4):
                keep = rid + d <= SUBLANES - 1
                beta = jnp.where(keep, beta + alpha * pltpu.roll(beta, SUBLANES - d, 0), beta)
                alpha = jnp.where(keep, alpha * pltpu.roll(alpha, SUBLANES - d, 0), alpha)
            gv = beta + alpha * cg
            g_scr[pl.ds(off, SUBLANES), :] = gv
            return av[0:1, :] * gv[0:1, :]

        carry[...] = lax.fori_loop(0, nchunk, chunk, carry[...])
        gv = g_scr[...]
        hhalo = jnp.where(first_tile, 0.0, hhalo_ref[...])
        hprev = _shift_down(jnp.concatenate([hhalo, hv], axis=0), 1, ts)
        dmult = gv * ig * u
        dig = gv * mult * u
        du = gv * mult * ig
        dlog_a = gv * hprev * a - dmult * a * a / mult
        dr = dlog_a * (-LRU_C * sp)
        dr_pre = dr * r * (1.0 - r)
        di_pre = dig * ig * (1.0 - ig)
        drb, dib, ub = dr_pre.astype(BF16), di_pre.astype(BF16), u.astype(BF16)
        du = du + _dot(drb, wr_ref[...], NT) + _dot(dib, wi_ref[...], NT)
        dwr_ref[...] += _dot(ub, drb, TN)
        dwi_ref[...] += _dot(ub, dib, TN)

        ext_du = jnp.concatenate([du, du_next[...]], axis=0)
        du_next[...] = du[0:SUBLANES, :]
        n_ext = ts + SUBLANES
        dx = cw_ref[3:4, :] * du
        sums = [_colsum(dr_pre), _colsum(di_pre), _colsum(dlog_a * (-LRU_C * r)), _colsum(du)]
        dcw = []
        for k in range(3):
            dx = dx + cw_ref[k:k + 1, :] * pltpu.roll(ext_du, n_ext - (3 - k), 0)[0:ts]
            dcw.append(_colsum(du * _shift_down(ext, 3 - k, ts)))
        dcw.append(_colsum(du * _shift_down(ext, 0, ts)))
        dx_ref[...] = dx.astype(BF16)
        sums_ref[...] += _rows_to_block(sums + dcw, LRU_W)

    full = lambda shape: pl.BlockSpec(shape, lambda b, i: (0,) * len(shape))
    tile = pl.BlockSpec((ts, LRU_W), row)
    return pl.pallas_call(
        body, name="lru_bwd", grid=(batch, nt),
        in_specs=[tile, tile, halo, pl.BlockSpec((ts, LRU_W), lambda b, i: (b * nt + (nt - 1 - i), 1)), tile, halo,
                  full((4, LRU_W)), full((1, LRU_W)), full((LRU_W, LRU_W)), full((1, LRU_W)),
                  full((LRU_W, LRU_W)), full((1, LRU_W)), full((1, LRU_W))],
        out_specs=[tile, tile, full((LRU_W, LRU_W)), full((LRU_W, LRU_W)), full((SUBLANES, LRU_W))],
        out_shape=[jax.ShapeDtypeStruct((m, LRU_W), BF16), jax.ShapeDtypeStruct((m, LRU_W), BF16),
                   jax.ShapeDtypeStruct((LRU_W, LRU_W), F32), jax.ShapeDtypeStruct((LRU_W, LRU_W), F32),
                   jax.ShapeDtypeStruct((SUBLANES, LRU_W), F32)],
        scratch_shapes=[pltpu.VMEM((ts, LRU_W), F32), pltpu.VMEM((ts, LRU_W), F32), pltpu.VMEM((ts, LRU_W), F32),
                        pltpu.VMEM((1, LRU_W), F32), pltpu.VMEM((SUBLANES, LRU_W), F32)],
        compiler_params=_params(2),
    )(dy, proj32, proj32, proj32, h, h, conv_w, conv_b, wr, br, wi, bi, lam)


def _head_masks():
    lane = lax.broadcasted_iota(jnp.int32, (1, LANES), 1)
    return lane < HEAD_DIM


def _stack_heads(x2):
    lo, zero = _head_masks(), jnp.zeros_like(x2)
    return jnp.concatenate([jnp.where(lo, x2, zero), jnp.where(lo, zero, x2)], axis=0)


def _unstack_heads(y):
    return jnp.where(_head_masks(), y[:TQ], y[TQ:])


def _stack_cols(a, b):
    return jnp.concatenate([a, b], axis=0)


def _causal(qi, kb, strict):
    r = jnp.bitwise_and(lax.broadcasted_iota(jnp.int32, (2 * TQ, BLK), 0), TQ - 1) + qi * TQ
    c = lax.broadcasted_iota(jnp.int32, (2 * TQ, BLK), 1) + kb * BLK
    return (c < r) if strict else (c <= r)


def _key_loop(qi, group, carry, descending=False):
    def trip(n, cr):
        done = [n * KB_PER_Q + j for j in range(KB_PER_Q)]
        return group([qi * KB_PER_Q - 1 - t for t in done] if descending else done, cr)

    return lax.fori_loop(0, qi, trip, carry)


def _one_by_one(block):
    def group(kbs, carry):
        for kb in kbs:
            carry = block(kb, carry, False)
        return carry
    return group


def _tri(cmp):
    r = lax.broadcasted_iota(jnp.int32, (BLK, BLK), 0)
    c = lax.broadcasted_iota(jnp.int32, (BLK, BLK), 1)
    return cmp(r, c)


def _dot_split(x, tri):
    hi, lo = _split2(x)
    return _dot(hi, tri) + _dot(lo, tri)


def _sb_fwd(proj16, batch, seq):
    nq = seq // TQ
    scale = HEAD_DIM ** -0.5

    def body(q_ref, k_ref, v_ref, y_ref, t_ref):
        qi = pl.program_id(2)
        qs = _stack_heads(q_ref[0])
        tri_after = _tri(lambda r, c: r > c).astype(BF16)

        def block(kb, carry, masked):
            acc, c = carry
            ks = pl.multiple_of(kb * BLK, BLK)
            k2, v2 = k_ref[0, pl.ds(ks, BLK), :], v_ref[0, pl.ds(ks, BLK), :]
            z = _dot(qs, k2, NT) * scale
            sp = _softplus(z)
            l = -sp
            if masked:
                valid = _causal(qi, kb, True)
                l = jnp.where(valid, l, 0.0)
            w = jnp.exp((z - sp) + _dot_split(l, tri_after) + c)
            if masked:
                w = jnp.where(valid, w, 0.0)
            return acc + _dot(w.astype(BF16), v2), c + jnp.sum(l, axis=1, keepdims=True)

        def group(kbs, carry):
            acc, c = carry
            kv = [(k_ref[0, pl.ds(pl.multiple_of(kb * BLK, BLK), BLK), :],
                   v_ref[0, pl.ds(pl.multiple_of(kb * BLK, BLK), BLK), :]) for kb in kbs]
            zs = [_dot(qs, k2, NT) * scale for k2, _ in kv]
            sps = [_softplus(z) for z in zs]
            afters = [_dot_split(-sp, tri_after) for sp in sps]
            for z, sp, after, (_, v2) in zip(zs, sps, afters, kv):
                acc = acc + _dot(jnp.exp((z - sp) + after + c).astype(BF16), v2)
                c = c - jnp.sum(sp, axis=1, keepdims=True)
            return acc, c

        carry = (jnp.zeros((2 * TQ, LANES), F32), jnp.zeros((2 * TQ, 1), F32))
        first = qi * KB_PER_Q
        for n in reversed(range(KB_PER_Q)):
            carry = block(first + n, carry, True)
        acc, c = _key_loop(qi, group, carry, descending=True)
        y_ref[...] = _unstack_heads(acc)
        t_ref[0] = _unstack_heads(jnp.broadcast_to(c, (2 * TQ, LANES)))

    m = batch * seq
    return pl.pallas_call(
        body, name="sb_fwd", grid=(batch, 2, nq),
        in_specs=[pl.BlockSpec((1, TQ, LANES), lambda b, p, q: (b, q, COL_SBQ + p)),
                  pl.BlockSpec((1, seq, LANES), lambda b, p, q: (b, 0, COL_SBK + p)),
                  pl.BlockSpec((1, seq, LANES), lambda b, p, q: (b, 0, COL_SBV + p))],
        out_specs=[pl.BlockSpec((TQ, LANES), lambda b, p, q: (b * nq + q, p)),
                   pl.BlockSpec((1, TQ, LANES), lambda b, p, q: (p, b * nq + q, 0))],
        out_shape=[jax.ShapeDtypeStruct((m, ATT_W), F32), jax.ShapeDtypeStruct((2, m, LANES), F32)],
        compiler_params=_params(3),
    )(proj16, proj16, proj16)


def _sb_bwd(dy, t, proj16, batch, seq):
    nq = seq // TQ
    scale = HEAD_DIM ** -0.5

    def body(dy_ref, t_ref, q_ref, k_ref, v_ref, dq_ref, dk_ref, dv_ref):
        qi = pl.program_id(2)

        @pl.when(qi == 0)
        def _():
            dk_ref[...] = jnp.zeros_like(dk_ref)
            dv_ref[...] = jnp.zeros_like(dv_ref)

        t2 = t_ref[0]
        qs, dys = _stack_heads(q_ref[0]), _stack_heads(dy_ref[...].astype(BF16))
        tot = _stack_cols(t2[:, 0:1], t2[:, HEAD_DIM:HEAD_DIM + 1])
        tri_incl = _tri(lambda r, c: r <= c).astype(BF16)
        tri_excl = _tri(lambda r, c: r < c).astype(BF16)

        def block(kb, carry, masked):
            dq, pc, ec = carry
            ks = pl.multiple_of(kb * BLK, BLK)
            k2, v2 = k_ref[0, pl.ds(ks, BLK), :], v_ref[0, pl.ds(ks, BLK), :]
            z = _dot(qs, k2, NT) * scale
            sp = _softplus(z)
            l, b = -sp, z - sp
            sig = jnp.exp(b)
            if masked:
                valid = _causal(qi, kb, True)
                l = jnp.where(valid, l, 0.0)
            after = tot - (pc + _dot_split(l, tri_incl))
            w = jnp.exp(b + after)
            if masked:
                w = jnp.where(valid, w, 0.0)
            e = _dot(dys, v2, NT) * w
            et = ec + _dot_split(e, tri_excl)
            dz = e * (1.0 - sig) - et * sig
            if masked:
                dz = jnp.where(valid, dz, 0.0)
            dzb = (dz * scale).astype(BF16)
            dk_ref[0, pl.ds(ks, BLK), :] += _dot(dzb, qs, TN)
            dv_ref[0, pl.ds(ks, BLK), :] += _dot(w.astype(BF16), dys, TN)
            return (dq + _dot(dzb, k2), pc + jnp.sum(l, axis=1, keepdims=True),
                    ec + jnp.sum(e, axis=1, keepdims=True))

        col = jnp.zeros((2 * TQ, 1), F32)
        first = qi * KB_PER_Q
        carry = _key_loop(qi, _one_by_one(block),(jnp.zeros((2 * TQ, LANES), F32), col, col))
        for n in range(KB_PER_Q):
            carry = block(first + n, carry, True)
        dq_ref[...] = _unstack_heads(carry[0])

    m = batch * seq
    whole = lambda col: pl.BlockSpec((1, seq, LANES), lambda b, p, q: (b, 0, col + p))
    return pl.pallas_call(
        body, name="sb_bwd", grid=(batch, 2, nq),
        in_specs=[pl.BlockSpec((TQ, LANES), lambda b, p, q: (b * nq + q, p)),
                  pl.BlockSpec((1, TQ, LANES), lambda b, p, q: (p, b * nq + q, 0)),
                  pl.BlockSpec((1, TQ, LANES), lambda b, p, q: (b, q, COL_SBQ + p)),
                  whole(COL_SBK), whole(COL_SBV)],
        out_specs=[pl.BlockSpec((TQ, LANES), lambda b, p, q: (b * nq + q, p)), whole(0), whole(0)],
        out_shape=[jax.ShapeDtypeStruct((m, ATT_W), F32), jax.ShapeDtypeStruct((batch, seq, ATT_W), F32),
                   jax.ShapeDtypeStruct((batch, seq, ATT_W), F32)],
        compiler_params=_params(3),
    )(dy, t, proj16, proj16, proj16)


def _fox_pre(proj32, gq, gk, bf, group_mean, batch, seq):
    m = proj32.shape[0]
    ts = _tok_tile(seq)
    nt = seq // ts

    def body(q_ref, k_ref, f_ref, gq_ref, gk_ref, bf_ref, gm_ref, fq_ref, fk_ref, fc_ref, carry):
        i = pl.program_id(1)

        @pl.when(i == 0)
        def _():
            carry[...] = jnp.zeros_like(carry)

        gm = gm_ref[...]
        for src, g_ref, dst in ((q_ref, gq_ref, fq_ref), (k_ref, gk_ref, fk_ref)):
            v = src[...]
            ms = _dot_split(v * v, gm)
            dst[...] = (v * lax.rsqrt(ms + EPS) * g_ref[...]).astype(BF16)
        z = f_ref[...] + bf_ref[...]
        lf = jnp.minimum(z, 0.0) - jnp.log(1.0 + jnp.exp(-jnp.abs(z)))
        r = lax.broadcasted_iota(jnp.int32, (ts, ts), 0)
        c = lax.broadcasted_iota(jnp.int32, (ts, ts), 1)
        tri = (r >= c).astype(BF16)
        hi, mid, low = _split3(lf)
        fc = _dot(tri, hi) + _dot(tri, mid) + _dot(tri, low) + carry[...]
        fc_ref[...] = fc
        carry[...] = fc[ts - 1:ts, :]

    full = lambda shape: pl.BlockSpec(shape, lambda b, i: (0,) * len(shape))
    return pl.pallas_call(
        body, name="fox_pre", grid=(batch, nt),
        in_specs=[pl.BlockSpec((ts, ATT_W), lambda b, i: (b * nt + i, 7)),
                  pl.BlockSpec((ts, ATT_W), lambda b, i: (b * nt + i, 8)),
                  pl.BlockSpec((ts, LANES), lambda b, i: (b * nt + i, COL_FXF)),
                  full((1, ATT_W)), full((1, ATT_W)), full((1, LANES)), full((ATT_W, ATT_W))],
        out_specs=[pl.BlockSpec((ts, ATT_W), lambda b, i: (b * nt + i, 0)),
                   pl.BlockSpec((ts, ATT_W), lambda b, i: (b * nt + i, 0)),
                   pl.BlockSpec((ts, LANES), lambda b, i: (b * nt + i, 0))],
        out_shape=[jax.ShapeDtypeStruct((m, ATT_W), BF16), jax.ShapeDtypeStruct((m, ATT_W), BF16),
                   jax.ShapeDtypeStruct((m, LANES), F32)],
        scratch_shapes=[pltpu.VMEM((1, LANES), F32)],
        compiler_params=_params(2),
    )(proj32, proj32, proj32, gq, gk, bf, group_mean)


def _fox_specs(batch, seq):
    nq = seq // TQ
    return dict(
        qblk=pl.BlockSpec((1, TQ, LANES), lambda b, p, q: (b, q, p)),
        whole=pl.BlockSpec((1, seq, LANES), lambda b, p, q: (b, 0, p)),
        vwhole=pl.BlockSpec((1, seq, LANES), lambda b, p, q: (b, 0, COL_FXV + p)),
        fcol=pl.BlockSpec((1, 1, TQ, 2), lambda b, p, q: (b, p, q, 0)),
        frow=pl.BlockSpec((1, 1, 2, seq), lambda b, p, q: (b, p, 0, 0)),
        rows=pl.BlockSpec((TQ, LANES), lambda b, p, q: (b * nq + q, p)),
        stat=pl.BlockSpec((1, TQ, LANES), lambda b, p, q: (p, b * nq + q, 0)),
    )


def _fox_logits(qs, k2, fq_col, fr_ref, ks, is_a, scale):
    fk_row = jnp.where(is_a, fr_ref[0, 0, 0:1, pl.ds(ks, BLK)], fr_ref[0, 0, 1:2, pl.ds(ks, BLK)])
    return _dot(qs, k2, NT) * scale + fq_col - fk_row


def _fox_fwd(fq, fk, proj16, fcol, frow, batch, seq):
    nq = seq // TQ
    scale = HEAD_DIM ** -0.5

    def body(q_ref, k_ref, v_ref, fc_ref, fr_ref, y_ref, lse_ref):
        qi = pl.program_id(2)
        qs = _stack_heads(q_ref[0])
        fcv = fc_ref[0, 0]
        fq_col = _stack_cols(fcv[:, 0:1], fcv[:, 1:2])
        is_a = lax.broadcasted_iota(jnp.int32, (2 * TQ, 1), 0) < TQ

        def block(kb, carry, masked):
            acc, mx, den = carry
            ks = pl.multiple_of(kb * BLK, BLK)
            k2, v2 = k_ref[0, pl.ds(ks, BLK), :], v_ref[0, pl.ds(ks, BLK), :]
            s = _fox_logits(qs, k2, fq_col, fr_ref, ks, is_a, scale)
            if masked:
                s = jnp.where(_causal(qi, kb, False), s, NEG_BIG)
            mx_new = jnp.maximum(mx, jnp.max(s, axis=1, keepdims=True))
            p = jnp.exp(s - mx_new)
            alpha = jnp.exp(mx - mx_new)
            return (alpha * acc + _dot(p.astype(BF16), v2), mx_new, alpha * den + jnp.sum(p, axis=1, keepdims=True))

        first = qi * KB_PER_Q
        carry = (jnp.zeros((2 * TQ, LANES), F32), jnp.full((2 * TQ, 1), NEG_BIG, F32), jnp.zeros((2 * TQ, 1), F32))
        carry = _key_loop(qi, _one_by_one(block),carry)
        for n in range(KB_PER_Q):
            carry = block(first + n, carry, True)
        acc, mx, den = carry
        y_ref[...] = _unstack_heads(acc / den)
        lse_ref[0] = _unstack_heads(jnp.broadcast_to(mx + jnp.log(den), (2 * TQ, LANES)))

    m = batch * seq
    sp = _fox_specs(batch, seq)
    return pl.pallas_call(
        body, name="fox_fwd", grid=(batch, 2, nq),
        in_specs=[sp["qblk"], sp["whole"], sp["vwhole"], sp["fcol"], sp["frow"]],
        out_specs=[sp["rows"], sp["stat"]],
        out_shape=[jax.ShapeDtypeStruct((m, ATT_W), F32), jax.ShapeDtypeStruct((2, m, LANES), F32)],
        compiler_params=_params(3),
    )(fq, fk, proj16, fcol, frow)


def _fox_bwd(dy, y, lse, fq, fk, proj16, fcol, frow, batch, seq):
    nq = seq // TQ
    scale = HEAD_DIM ** -0.5

    def body(dy_ref, y_ref, lse_ref, q_ref, k_ref, v_ref, fc_ref, fr_ref, dq_ref, dk_ref, dv_ref, dfr_ref, dfc_ref):
        qi = pl.program_id(2)

        @pl.when(qi == 0)
        def _():
            dk_ref[...] = jnp.zeros_like(dk_ref)
            dv_ref[...] = jnp.zeros_like(dv_ref)
            dfr_ref[...] = jnp.zeros_like(dfr_ref)

        lo = _head_masks()
        lane = lax.broadcasted_iota(jnp.int32, (1, LANES), 1)
        dy2, lse2, fcv = dy_ref[...], lse_ref[0], fc_ref[0, 0]
        qs, dys = _stack_heads(q_ref[0]), _stack_heads(dy2.astype(BF16))
        dyy = dy2 * y_ref[...]
        delta = _stack_cols(jnp.sum(jnp.where(lo, dyy, 0.0), axis=1, keepdims=True),
                            jnp.sum(jnp.where(lo, 0.0, dyy), axis=1, keepdims=True))
        lse_col = _stack_cols(lse2[:, 0:1], lse2[:, HEAD_DIM:HEAD_DIM + 1])
        fq_col = _stack_cols(fcv[:, 0:1], fcv[:, 1:2])
        is_a = lax.broadcasted_iota(jnp.int32, (2 * TQ, 1), 0) < TQ

        def block(kb, carry, masked):
            dq, rs = carry
            ks = pl.multiple_of(kb * BLK, BLK)
            k2, v2 = k_ref[0, pl.ds(ks, BLK), :], v_ref[0, pl.ds(ks, BLK), :]
            p = jnp.exp(_fox_logits(qs, k2, fq_col, fr_ref, ks, is_a, scale) - lse_col)
            if masked:
                p = jnp.where(_causal(qi, kb, False), p, 0.0)
            ds = p * (_dot(dys, v2, NT) - delta)
            dsb = (ds * scale).astype(BF16)
            dk_ref[0, pl.ds(ks, BLK), :] += _dot(dsb, qs, TN)
            dv_ref[0, pl.ds(ks, BLK), :] += _dot(p.astype(BF16), dys, TN)
            dfr_ref[0, 0, 0:1, pl.ds(ks, BLK)] -= jnp.sum(ds[:TQ], axis=0, keepdims=True)
            dfr_ref[0, 0, 1:2, pl.ds(ks, BLK)] -= jnp.sum(ds[TQ:], axis=0, keepdims=True)
            return dq + _dot(dsb, k2), rs + jnp.sum(ds, axis=1, keepdims=True)

        first = qi * KB_PER_Q
        carry = _key_loop(qi, _one_by_one(block),(jnp.zeros((2 * TQ, LANES), F32), jnp.zeros((2 * TQ, 1), F32)))
        for n in range(KB_PER_Q):
            carry = block(first + n, carry, True)
        dq, rs = carry
        dq_ref[...] = _unstack_heads(dq)
        dfc_ref[0] = jnp.where(lane == 0, rs[:TQ], jnp.where(lane == 1, rs[TQ:], 0.0))

    m = batch * seq
    sp = _fox_specs(batch, seq)
    return pl.pallas_call(
        body, name="fox_bwd", grid=(batch, 2, nq),
        in_specs=[sp["rows"], sp["rows"], sp["stat"], sp["qblk"], sp["whole"], sp["vwhole"], sp["fcol"], sp["frow"]],
        out_specs=[sp["rows"], sp["whole"], sp["whole"],
                   pl.BlockSpec((1, 1, SUBLANES, seq), lambda b, p, q: (b, p, 0, 0)), sp["stat"]],
        out_shape=[jax.ShapeDtypeStruct((m, ATT_W), F32), jax.ShapeDtypeStruct((batch, seq, ATT_W), F32),
                   jax.ShapeDtypeStruct((batch, seq, ATT_W), F32),
                   jax.ShapeDtypeStruct((batch, 2, SUBLANES, seq), F32), jax.ShapeDtypeStruct((2, m, LANES), F32)],
        compiler_params=_params(3),
    )(dy, y, lse, fq, fk, proj16, fcol, frow)


def _fox_post_bwd(dfq, dfk, dfc, proj32, gq, gk, bf, group_mean, batch, seq):
    m = proj32.shape[0]
    ts = _tok_tile(seq)
    nt = seq // ts
    tile = lambda w, col: pl.BlockSpec((ts, w), lambda b, i: (b * nt + (nt - 1 - i), col))

    def body(dfq_ref, dfk_ref, dfc_ref, q_ref, k_ref, f_ref, gq_ref, gk_ref, bf_ref, gm_ref,
             dq_ref, dk_ref, df_ref, gs_ref, bs_ref, carry):
        i = pl.program_id(1)

        @pl.when(i == 0)
        def _():
            carry[...] = jnp.zeros_like(carry)

        gm = gm_ref[...]
        rows = []
        for src, g_ref, d_ref, dst in ((q_ref, gq_ref, dfq_ref, dq_ref), (k_ref, gk_ref, dfk_ref, dk_ref)):
            v, dv = src[...], d_ref[...]
            rstd = lax.rsqrt(_dot_split(v * v, gm) + EPS)
            vhat = v * rstd
            rows.append(_colsum(dv * vhat))
            dvh = dv * g_ref[...]
            dst[...] = (rstd * (dvh - vhat * _dot_split(dvh * vhat, gm))).astype(BF16)
        gs_ref[0] = _rows_to_block(rows, ATT_W)

        dfc_v = dfc_ref[...]
        r = lax.broadcasted_iota(jnp.int32, (ts, ts), 0)
        c = lax.broadcasted_iota(jnp.int32, (ts, ts), 1)
        tri = (r <= c).astype(BF16)
        hi, mid, low = _split3(dfc_v)
        dlf = _dot(tri, hi) + _dot(tri, mid) + _dot(tri, low) + carry[...]
        carry[...] = dlf[0:1, :]
        z = f_ref[...] + bf_ref[...]
        dz = dlf * _sigmoid(-z)
        df_ref[...] = dz.astype(BF16)
        bs_ref[0] = _rows_to_block([_colsum(dz)], LANES)

    full = lambda shape: pl.BlockSpec(shape, lambda b, i: (0,) * len(shape))
    part = lambda w: pl.BlockSpec((1, SUBLANES, w), lambda b, i: (b * nt + (nt - 1 - i), 0, 0))
    return pl.pallas_call(
        body, name="fox_post_bwd", grid=(batch, nt),
        in_specs=[tile(ATT_W, 0), tile(ATT_W, 0), tile(LANES, 0), tile(ATT_W, 7), tile(ATT_W, 8), tile(LANES, COL_FXF),
                  full((1, ATT_W)), full((1, ATT_W)), full((1, LANES)), full((ATT_W, ATT_W))],
        out_specs=[tile(ATT_W, 0), tile(ATT_W, 0), tile(LANES, 0), part(ATT_W), part(LANES)],
        out_shape=[jax.ShapeDtypeStruct((m, ATT_W), BF16), jax.ShapeDtypeStruct((m, ATT_W), BF16),
                   jax.ShapeDtypeStruct((m, LANES), BF16),
                   jax.ShapeDtypeStruct((batch * nt, SUBLANES, ATT_W), F32),
                   jax.ShapeDtypeStruct((batch * nt, SUBLANES, LANES), F32)],
        scratch_shapes=[pltpu.VMEM((1, LANES), F32)],
        compiler_params=_params(2),
    )(dfq, dfk, dfc, proj32, proj32, proj32, gq, gk, bf, group_mean)


_GROUPS = ((0, LRU_W), (LRU_W, LRU_W + ATT_W), (LRU_W + ATT_W, LRU_W + 2 * ATT_W))


def _outnorm(y_lru, y_sb, y_fox, gmix, seq):
    m = y_lru.shape[0]
    tm = _tok_tile(seq)

    def body(a_ref, b_ref, c_ref, g_ref, o_ref):
        parts = []
        for ref in (a_ref, b_ref, c_ref):
            v = ref[...]
            parts.append(v * lax.rsqrt(jnp.mean(v * v, axis=-1, keepdims=True) + EPS))
        o_ref[...] = (jnp.concatenate(parts, axis=1) * g_ref[...]).astype(BF16)

    t = lambda w: pl.BlockSpec((tm, w), lambda i: (i, 0))
    return pl.pallas_call(
        body, name="outnorm", grid=(m // tm,),
        in_specs=[t(LRU_W), t(ATT_W), t(ATT_W), pl.BlockSpec((1, D_MODEL), lambda i: (0, 0))],
        out_specs=t(D_MODEL), out_shape=jax.ShapeDtypeStruct((m, D_MODEL), BF16), compiler_params=_params(1),
    )(y_lru, y_sb, y_fox, gmix)


def _outnorm_bwd_epilogue(p, e_refs, o_refs):
    gmix = e_refs[3][...]
    dg = []
    for n, (lo, hi) in enumerate(_GROUPS):
        v, dyn = e_refs[n][...], p[:, lo:hi]
        rstd = lax.rsqrt(jnp.mean(v * v, axis=-1, keepdims=True) + EPS)
        vhat = v * rstd
        dg.append(_colsum(dyn * vhat))
        dvh = dyn * gmix[:, lo:hi]
        o_refs[n][...] = rstd * (dvh - vhat * jnp.mean(dvh * vhat, axis=-1, keepdims=True))
    o_refs[3][0] = _rows_to_block([jnp.concatenate(dg, axis=1)], p.shape[1])


def _pair_layouts(fcum, batch, seq):
    f4 = fcum[:, :4].reshape(batch, seq, 2, 2)
    return f4.transpose(0, 2, 1, 3), f4.transpose(0, 2, 3, 1)


def _gate_grad_cols(dfr, dfc, batch, seq):
    keys = dfr[:, :, :2, :].transpose(0, 3, 1, 2).reshape(batch * seq, 4)
    queries = dfc[:, :, :2].transpose(1, 0, 2).reshape(batch * seq, 4)
    return jnp.pad(keys + queries, ((0, 0), (0, LANES - 4)))


def _mixer_fwd(x, h, w, gate, batch, seq):
    m, d = x.shape
    tm = _tok_tile(seq)
    tpb = seq // tm

    def in_epilogue(p, e_refs, o_refs):
        o_refs[0][...] = p
        o_refs[1][...] = p.astype(BF16)

    tn_in = 896
    proj32, proj16 = _mm(h, w["w_in"], mode="nn", tm=tm, tn=tn_in, tk=d, name="mix_in",
                         outs=[((m, N_IN_PAD), F32, (tm, tn_in), lambda i, j: (i, j)),
                               ((m, N_IN_PAD), BF16, (tm, tn_in), lambda i, j: (i, j))],
                         epilogue=in_epilogue)
    y_lru, h_lru = _lru_fwd(proj32, w["conv_w"], w["conv_b"], w["wr"], w["br"], w["wi"], w["bi"], w["lam"], batch, seq)
    p16 = proj16.reshape(batch, seq, N_IN_PAD)
    y_sb, t_sb = _sb_fwd(p16, batch, seq)
    fq, fk, fcum = _fox_pre(proj32, w["gq"], w["gk"], w["bf"], w["group_mean"], batch, seq)
    fcol, frow = _pair_layouts(fcum, batch, seq)
    fq3, fk3 = fq.reshape(batch, seq, ATT_W), fk.reshape(batch, seq, ATT_W)
    y_fox, lse = _fox_fwd(fq3, fk3, p16, fcol, frow, batch, seq)
    ynorm = _outnorm(y_lru, y_sb, y_fox, w["gmix"], seq)

    def out_epilogue(p, e_refs, o_refs):
        x_ref, g_ref = e_refs
        o_refs[0][...] = x_ref[...] + (1.0 + g_ref[0]) * p
        o_refs[1][...] = p.astype(BF16)

    x_out, out = _mm(ynorm, w["w_out"], mode="nn", tm=tm, tn=d, tk=d, name="mix_out",
                     extras=[(x, (tm, d), lambda i, j: (i, 0)), (gate, (1, 1, d), lambda i, j: (i // tpb, 0, 0))],
                     outs=[((m, d), F32, (tm, d), lambda i, j: (i, 0)), ((m, d), BF16, (tm, d), lambda i, j: (i, 0))],
                     epilogue=out_epilogue)
    saved = dict(proj32=proj32, p16=p16, h_lru=h_lru, y_lru=y_lru, y_sb=y_sb, t_sb=t_sb, fq3=fq3, fk3=fk3,
                 fcol=fcol, frow=frow, y_fox=y_fox, lse=lse, ynorm=ynorm, out=out)
    return x_out, saved


def _mixer_bwd(dx_out, x, h, s, w, gn, scale, gate, batch, seq, on_grads):
    m, d = x.shape
    tm = _tok_tile(seq)
    tpb = seq // tm
    dout, dgate_parts = _residual_bwd(dx_out, s["out"], gate, 1.0, seq, "mix_res_bwd")
    (dw_out,) = _mm(s["ynorm"], dout, mode="tn", tm=d, tn=d, tk=tm, name="mix_dwout",
                    outs=[((d, d), BF16, (d, d), lambda i, j: (i, j))], epilogue=_store_epilogue([BF16]))
    dy_lru, dy_sb, dy_fox, gmix_parts = _mm(
        dout, w["w_out"], mode="nt", tm=tm, tn=d, tk=d, name="mix_out_dx",
        extras=[(s["y_lru"], (tm, LRU_W), lambda i, j: (i, 0)), (s["y_sb"], (tm, ATT_W), lambda i, j: (i, 0)),
                (s["y_fox"], (tm, ATT_W), lambda i, j: (i, 0)), (w["gmix"], (1, d), lambda i, j: (0, 0))],
        outs=[((m, LRU_W), F32, (tm, LRU_W), lambda i, j: (i, 0)), ((m, ATT_W), F32, (tm, ATT_W), lambda i, j: (i, 0)),
              ((m, ATT_W), F32, (tm, ATT_W), lambda i, j: (i, 0)),
              ((m // tm, SUBLANES, d), F32, (1, SUBLANES, d), lambda i, j: (i, 0, 0))],
        epilogue=_outnorm_bwd_epilogue)

    dsq, dsk, dsv = _sb_bwd(dy_sb, s["t_sb"], s["p16"], batch, seq)
    dfq, dfk, dfv, dfr, dfc = _fox_bwd(dy_fox, s["y_fox"], s["lse"], s["fq3"], s["fk3"], s["p16"], s["fcol"],
                                       s["frow"], batch, seq)
    dfc_cols = _gate_grad_cols(dfr, dfc, batch, seq)
    dxq, dxk, dxf, gqk_parts, bf_parts = _fox_post_bwd(dfq, dfk.reshape(m, ATT_W), dfc_cols, s["proj32"],
                                                       w["gq"], w["gk"], w["bf"], w["group_mean"], batch, seq)
    dlx, dlg, dwr, dwi, lru_sums = _lru_bwd(dy_lru, s["proj32"], s["h_lru"], w["conv_w"], w["conv_b"], w["wr"],
                                            w["br"], w["wi"], w["bi"], w["lam"], batch, seq)
    dproj = jnp.concatenate([dlx, dlg, dsq.astype(BF16), dsk.reshape(m, ATT_W).astype(BF16),
                             dsv.reshape(m, ATT_W).astype(BF16), dxq, dxk, dfv.reshape(m, ATT_W).astype(BF16), dxf],
                            axis=1)
    tn_in = 896
    (dw_in,) = _mm(h, dproj, mode="tn", tm=d, tn=tn_in, tk=tm, name="mix_dwin",
                   outs=[((d, N_IN_PAD), BF16, (d, tn_in), lambda i, j: (i, j))], epilogue=_store_epilogue([BF16]))
    scale = scale + on_grads(dw_in, dw_out)[0, 0]
    dx, nm_parts = _mm(dproj, w["w_in"], mode="nt", tm=tm, tn=d, tk=tn_in, name="mix_in_dx",
                       extras=[(x, (tm, d), lambda i, j: (i, 0)), (dx_out, (tm, d), lambda i, j: (i, 0)),
                               (gn, (1, d), lambda i, j: (0, 0)), (scale, (1, 1, d), lambda i, j: (i // tpb, 0, 0))],
                       outs=[((m, d), F32, (tm, d), lambda i, j: (i, 0)),
                             ((m // tm, SUBLANES, d), F32, (1, SUBLANES, d), lambda i, j: (i, 0, 0))],
                       epilogue=_normmod_bwd_epilogue)
    grads = dict(dwr=dwr, dwi=dwi, lru_sums=lru_sums, gmix_parts=gmix_parts,
                 gqk_parts=gqk_parts, bf_parts=bf_parts)
    return dx, grads, nm_parts, dgate_parts


def _block_diag(w):
    nb = w.shape[0]
    eye = jnp.eye(nb, dtype=w.dtype)
    return (eye[:, None, :, None] * w[:, :, None, :]).reshape(nb * HEAD_DIM, nb * HEAD_DIM)


def _block_diag_grad(g):
    nb = LRU_W // HEAD_DIM
    g4 = g.reshape(nb, HEAD_DIM, nb, HEAD_DIM)
    return jnp.stack([g4[n, :, n, :] for n in range(nb)])


def _per_batch(parts, batch, row):
    r = parts[:, row, :]
    return r.reshape(batch, -1, r.shape[-1]).sum(axis=1)


def _local_step(x3, target3, mod, wts, big_weights):
    batch, seq, d = x3.shape
    assert seq % TQ == 0, seq
    m = batch * seq
    n_layers = mod.shape[0]
    x = x3.reshape(m, d)
    group_mean = _block_diag(jnp.full((ATT_W // HEAD_DIM, HEAD_DIM, HEAD_DIM), 1.0 / HEAD_DIM, BF16))
    vec = lambda l, j, t: mod[l, :, j, t][:, None, :]

    layers, saved = [], []
    for l in range(n_layers):
        gq = jnp.tile(wts["g_qk"][l, 0], ATT_W // HEAD_DIM)[None, :]
        gk = jnp.tile(wts["g_qk"][l, 1], ATT_W // HEAD_DIM)[None, :]
        bf = jnp.pad(wts["b_fgate"][l], (0, LANES - 4))[None, :]
        lw = dict(conv_w=wts["conv_w"][l],
                  conv_b=wts["conv_b"][l][None, :], wr=_block_diag(wts["w_rgate"][l]).astype(BF16),
                  br=wts["b_rgate"][l][None, :], wi=_block_diag(wts["w_igate"][l]).astype(BF16),
                  bi=wts["b_igate"][l][None, :], lam=wts["lru_lambda"][l][None, :], gq=gq, gk=gk, bf=bf,
                  group_mean=group_mean, gmix=wts["g_mix_out"][l][None, :])
        layers.append(lw)
        gn = lambda j: wts["g_norm"][l, j][None, :]
        sv = dict(x0=x)
        sv["h0"] = _normmod(x, gn(0), vec(l, 0, 1), vec(l, 0, 0), seq, f"normmod_{l}_0")
        sv["w_ffn0"] = big_weights(l, "ffn0", sv["h0"])
        x, sv["ffn0"] = _ffn_fwd(x, sv["h0"], *sv["w_ffn0"], vec(l, 0, 2), seq, f"{l}_0")
        sv["x1"] = x
        sv["h1"] = _normmod(x, gn(1), vec(l, 1, 1), vec(l, 1, 0), seq, f"normmod_{l}_1")
        lw["w_in"], lw["w_out"] = big_weights(l, "mix", sv["h1"])
        x, sv["mix"] = _mixer_fwd(x, sv["h1"], lw, vec(l, 1, 2), batch, seq)
        sv["x2"] = x
        sv["h2"] = _normmod(x, gn(2), vec(l, 2, 1), vec(l, 2, 0), seq, f"normmod_{l}_2")
        sv["w_ffn1"] = big_weights(l, "ffn1", sv["h2"])
        x, sv["ffn1"] = _ffn_fwd(x, sv["h2"], *sv["w_ffn1"], vec(l, 2, 2), seq, f"{l}_1")
        saved.append(sv)

    dx, loss_parts = _loss_head(x, target3.reshape(m, d), seq)
    loss = jnp.sum(loss_parts[:, 0, 0])

    handles = {}

    def scatter(key, shapes):
        def on_grads(*grads):
            ops = [(g.reshape(shape), 0, "scatter") for g, shape in zip(grads, shapes)]
            handles[key], token = _flight_start(ops, f"grads_{key[0]}_{key[1]}_start")
            return token
        return on_grads

    ffn_shapes = ((2 * N_FF_SHARD, d, FF_SHARD), (N_DEV, D_FF // N_DEV, d))
    mix_shapes = ((N_DEV, d // N_DEV, N_IN_PAD), (N_DEV, d // N_DEV, d))
    small = {k: [] for k in ("dmod", "g_norm", "b_fgate", "conv_w", "conv_b", "w_rgate", "b_rgate", "w_igate",
                             "b_igate", "lru_lambda", "g_qk", "g_mix_out")}
    for l in reversed(range(n_layers)):
        sv, lw = saved[l], layers[l]
        gn = lambda j: wts["g_norm"][l, j][None, :]
        dx, nm2, dg2 = _ffn_bwd(dx, sv["x2"], sv["h2"], sv["ffn1"], *sv["w_ffn1"], gn(2), vec(l, 2, 1), vec(l, 2, 2),
                                seq, f"{l}_1", scatter((l, "ffn1"), ffn_shapes))
        dx, mg, nm1, dg1 = _mixer_bwd(dx, sv["x1"], sv["h1"], sv["mix"], lw, gn(1), vec(l, 1, 1), vec(l, 1, 2),
                                      batch, seq, scatter((l, "mix"), mix_shapes))
        dx, nm0, dg0 = _ffn_bwd(dx, sv["x0"], sv["h0"], sv["ffn0"], *sv["w_ffn0"], gn(0), vec(l, 0, 1), vec(l, 0, 2),
                                seq, f"{l}_0", scatter((l, "ffn0"), ffn_shapes))
        dmod_l, gnorm_l = [], []
        for nm, dg in ((nm0, dg0), (nm1, dg1), (nm2, dg2)):
            dmod_l.append(jnp.stack([_per_batch(nm, batch, 0), _per_batch(nm, batch, 1), _per_batch(dg, batch, 0)],
                                    axis=1))
            gnorm_l.append(jnp.sum(nm[:, 2, :], axis=0))
        small["dmod"].insert(0, jnp.stack(dmod_l, axis=1))
        small["g_norm"].insert(0, jnp.stack(gnorm_l))
        ls = mg["lru_sums"]
        small["b_rgate"].insert(0, ls[0])
        small["b_igate"].insert(0, ls[1])
        small["lru_lambda"].insert(0, ls[2] * (-_sigmoid(-wts["lru_lambda"][l])))
        small["conv_b"].insert(0, ls[3])
        small["conv_w"].insert(0, ls[4:8])
        small["w_rgate"].insert(0, _block_diag_grad(mg["dwr"]))
        small["w_igate"].insert(0, _block_diag_grad(mg["dwi"]))
        small["g_mix_out"].insert(0, jnp.sum(mg["gmix_parts"][:, 0, :], axis=0))
        gqk = jnp.sum(mg["gqk_parts"][:, :2, :], axis=0).reshape(2, ATT_W // HEAD_DIM, HEAD_DIM).sum(axis=1)
        small["g_qk"].insert(0, gqk)
        small["b_fgate"].insert(0, jnp.sum(mg["bf_parts"][:, 0, :4], axis=0))
    small = {k: jnp.stack(v) for k, v in small.items()}
    return loss, dx.reshape(batch, seq, d), handles, small


def _row_tile(rows, row_bytes):
    for t in (512, 256, 128, 64, 32, 16):
        if rows % t == 0 and t * row_bytes <= 4 * 1024 * 1024:
            return t
    return rows


def _adamw(parts, w, m, v, name):
    groups, n_parts, rows, cols = parts.shape
    tr = _row_tile(rows, cols * (n_parts * parts.dtype.itemsize + 7 * 4))
    c1 = 1.0 - ADAM_B1 ** ADAM_STEP
    c2 = 1.0 - ADAM_B2 ** ADAM_STEP

    def body(p_ref, w_ref, m_ref, v_ref, g_out, d_out, m_out, v_out):
        g = p_ref[0].astype(F32)
        for n in range(1, n_parts):
            g = g + p_ref[n].astype(F32)
        m_new = ADAM_B1 * m_ref[...] + (1.0 - ADAM_B1) * g
        v_new = ADAM_B2 * v_ref[...] + (1.0 - ADAM_B2) * (g * g)
        g_out[...] = g
        d_out[...] = -ADAM_LR * ((m_new / c1) / (jnp.sqrt(v_new / c2) + ADAM_EPS) + ADAM_WD * w_ref[...])
        m_out[...] = m_new
        v_out[...] = v_new

    tile = pl.BlockSpec((None, tr, cols), lambda g, i: (g, i, 0))
    return pl.pallas_call(
        body, name=name, grid=(groups, rows // tr),
        in_specs=[pl.BlockSpec((None, n_parts, tr, cols), lambda g, i: (g, 0, i, 0)), tile, tile, tile],
        out_specs=[tile] * 4, out_shape=[jax.ShapeDtypeStruct((groups, rows, cols), F32)] * 4,
        compiler_params=_params(2),
    )(parts, w, m, v)


def _sum_parts(parts):
    n_parts, rows, cols = parts.shape

    def body(p_ref, o_ref):
        acc = p_ref[0]
        for n in range(1, n_parts):
            acc = acc + p_ref[n]
        o_ref[...] = acc

    return pl.pallas_call(body, name="sum_small", out_shape=jax.ShapeDtypeStruct((rows, cols), F32),
                          compiler_params=pltpu.CompilerParams(vmem_limit_bytes=VMEM_LIMIT_BYTES))(parts)


def _flatten(arrays, multiple):
    flat = jnp.concatenate([a.reshape(-1).astype(F32) for a in arrays])
    pad = (-flat.shape[0]) % multiple
    return jnp.pad(flat, (0, pad)).reshape(-1, LANES)


def _unflatten(flat2d, shapes):
    flat, out, off = flat2d.reshape(-1), [], 0
    for s in shapes:
        n = math.prod(s)
        out.append(flat[off:off + n].reshape(s))
        off += n
    return out


SMALL_NAMES = ("b_ada", "g_norm", "b_fgate", "conv_w", "conv_b", "w_rgate", "b_rgate", "w_igate", "b_igate",
               "lru_lambda", "g_qk", "g_mix_out")
WEIGHT_NAMES = ("w_ada", "b_ada", "g_norm", "w_ffn_up", "w_ffn_down", "w_in", "b_fgate", "conv_w", "conv_b",
                "w_rgate", "b_rgate", "w_igate", "b_igate", "lru_lambda", "g_qk", "g_mix_out", "w_out")


def kernel(x, c, w_ada, b_ada, g_norm, w_ffn_up, w_ffn_down, w_in, b_fgate, conv_w, conv_b, w_rgate, b_rgate, w_igate, b_igate, lru_lambda, g_qk, g_mix_out, w_out, loss_target, m_w_ada, m_b_ada, m_g_norm, m_w_ffn_up, m_w_ffn_down, m_w_in, m_b_fgate, m_conv_w, m_conv_b, m_w_rgate, m_b_rgate, m_w_igate, m_b_igate, m_lru_lambda, m_g_qk, m_g_mix_out, m_w_out, v_w_ada, v_b_ada, v_g_norm, v_w_ffn_up, v_w_ffn_down, v_w_in, v_b_fgate, v_conv_w, v_conv_b, v_w_rgate, v_b_rgate, v_w_igate, v_b_igate, v_lru_lambda, v_g_qk, v_g_mix_out, v_w_out):
    batch, seq, d = x.shape
    n_layers = w_ada.shape[0]
    me = 4 * lax.axis_index("x") + 2 * lax.axis_index("y") + lax.axis_index("c")
    weights = dict(w_ada=w_ada, b_ada=b_ada, g_norm=g_norm, w_ffn_up=w_ffn_up, w_ffn_down=w_ffn_down, w_in=w_in,
                   b_fgate=b_fgate, conv_w=conv_w, conv_b=conv_b, w_rgate=w_rgate, b_rgate=b_rgate, w_igate=w_igate,
                   b_igate=b_igate, lru_lambda=lru_lambda, g_qk=g_qk, g_mix_out=g_mix_out, w_out=w_out)
    moments_m = dict(w_ada=m_w_ada, b_ada=m_b_ada, g_norm=m_g_norm, w_ffn_up=m_w_ffn_up, w_ffn_down=m_w_ffn_down,
                     w_in=m_w_in, b_fgate=m_b_fgate, conv_w=m_conv_w, conv_b=m_conv_b, w_rgate=m_w_rgate,
                     b_rgate=m_b_rgate, w_igate=m_w_igate, b_igate=m_b_igate, lru_lambda=m_lru_lambda, g_qk=m_g_qk,
                     g_mix_out=m_g_mix_out, w_out=m_w_out)
    moments_v = dict(w_ada=v_w_ada, b_ada=v_b_ada, g_norm=v_g_norm, w_ffn_up=v_w_ffn_up, w_ffn_down=v_w_ffn_down,
                     w_in=v_w_in, b_fgate=v_b_fgate, conv_w=v_conv_w, conv_b=v_conv_b, w_rgate=v_w_rgate,
                     b_rgate=v_b_rgate, w_igate=v_w_igate, b_igate=v_b_igate, lru_lambda=v_lru_lambda, g_qk=v_g_qk,
                     g_mix_out=v_g_mix_out, w_out=v_w_out)

    w_in_pad = jnp.pad(w_in, ((0, 0), (0, 0), (0, N_IN_PAD - N_IN))).astype(BF16)
    up16, down16, out16 = w_ffn_up.astype(BF16), w_ffn_down.astype(BF16), w_out.astype(BF16)
    ffn_ops = lambda l, f: [(up16[l, f], 0, "gather"), (down16[l, f], 0, "gather")]
    mix_ops = lambda l: [(w_in_pad[l], 0, "gather"), (out16[l], 0, "gather")]
    groups = {(0, "ffn0"): ffn_ops(0, 0), (0, "mix"): mix_ops(0), (0, "ffn1"): ffn_ops(0, 1)}
    for l in range(1, n_layers):
        groups[(l, "all")] = ffn_ops(l, 0) + mix_ops(l) + ffn_ops(l, 1)
    flights, order = {}, jnp.zeros((), F32)
    for key, ops in groups.items():
        flights[key], token = _flight_start(ops, f"weights_{key[0]}_{key[1]}_start")
        order = order + token[0, 0]
    landed_weights = {}

    def big_weights(l, part, after):
        key = (l, part) if (l, part) in flights else (l, "all")
        if key not in landed_weights:
            landed_weights[key] = _flight_wait(flights[key], after, f"weights_{key[0]}_{key[1]}_wait")
        got = landed_weights[key]
        if key[1] == "all":
            got = got[{"ffn0": 0, "mix": 2, "ffn1": 4}[part]:][:2]
        if part == "mix":
            return got[0].reshape(d, N_IN_PAD), got[1].reshape(d, d)
        return got[0], got[1].reshape(D_FF, d)

    c_all, gn_all, cw_all = _exchange([(c + order, 0), (g_norm, 0), (conv_w, 0)], [], "gather_small_weights")
    c_all = c_all.reshape(N_DEV * batch, d)
    n_ada = w_ada.shape[-1]
    g_norm_full = gn_all.transpose(1, 2, 0, 3).reshape(n_layers, 3, d)
    conv_w_full = cw_all.transpose(1, 2, 0, 3).reshape(n_layers, 4, LRU_W)

    b_ada_loc = lax.dynamic_slice_in_dim(b_ada, me * n_ada, n_ada, axis=1)
    silu = lambda t: t * _sigmoid(t)

    def bias_epilogue(p, e_refs, o_refs):
        o_refs[0][...] = p + e_refs[0][...]

    mod_loc = []
    for l in range(n_layers):
        (ml,) = _mm(c_all, w_ada, mode="nn", tm=c_all.shape[0], tn=n_ada, tk=d, b_lead=(l,), a_pre=silu,
                    name=f"ada_{l}", extras=[(b_ada_loc[l][None, :], (1, n_ada), lambda i, j: (0, 0))],
                    outs=[((c_all.shape[0], n_ada), F32, (c_all.shape[0], n_ada), lambda i, j: (0, 0))],
                    epilogue=bias_epilogue)
        mod_loc.append(ml)
    (mod_all,) = _exchange([(jnp.stack(mod_loc), 0)], [], "gather_mod")
    mod_all = mod_all.transpose(1, 2, 0, 3).reshape(n_layers, N_DEV * batch, 9 * d)
    mod_me = lax.dynamic_slice_in_dim(mod_all, me * batch, batch, axis=1).reshape(n_layers, batch, 3, 3, d)

    wts = dict(g_norm=g_norm_full, conv_w=conv_w_full, conv_b=conv_b, w_rgate=w_rgate, b_rgate=b_rgate, w_igate=w_igate, b_igate=b_igate,
               lru_lambda=lru_lambda, g_qk=g_qk, g_mix_out=g_mix_out, b_fgate=b_fgate)
    loss_part, grad_x, handles, small = _local_step(x, loss_target, mod_me, wts, big_weights)

    dmod_me = small.pop("dmod").reshape(n_layers, batch, 9 * d)
    small["b_ada"] = jnp.sum(dmod_me, axis=1)
    small_shapes = [(1,)] + [weights[k].shape if k not in ("g_norm", "conv_w") else small[k].shape for k in SMALL_NAMES]
    small_flat = _flatten([loss_part.reshape(1)] + [small[k] for k in SMALL_NAMES], 16 * LANES)
    dmod_all, small_all = _exchange([(dmod_me, 0), (small_flat, 0)], [], "gather_small")
    landed = {key: _flight_wait(h, small_all, f"grads_{key[0]}_{key[1]}_wait") for key, h in handles.items()}
    layer_range = range(n_layers)
    p_up = jnp.stack([jnp.stack([landed[(l, "ffn0")][0], landed[(l, "ffn1")][0]]) for l in layer_range])
    p_down = jnp.stack([jnp.stack([landed[(l, "ffn0")][1], landed[(l, "ffn1")][1]]) for l in layer_range])
    p_in = jnp.stack([landed[(l, "mix")][0] for l in layer_range])
    p_out = jnp.stack([landed[(l, "mix")][1] for l in layer_range])
    small_sum = _unflatten(_sum_parts(small_all), small_shapes)
    loss = small_sum[0].reshape(())
    small_grads = dict(zip(SMALL_NAMES, small_sum[1:]))
    small_grads["g_norm"] = lax.dynamic_slice_in_dim(small_grads["g_norm"], me * g_norm.shape[-1], g_norm.shape[-1], 2)
    small_grads["conv_w"] = lax.dynamic_slice_in_dim(small_grads["conv_w"], me * conv_w.shape[-1], conv_w.shape[-1], 2)

    dmod_all = dmod_all.transpose(1, 0, 2, 3).reshape(n_layers, N_DEV * batch, 9 * d)
    dmod_loc = lax.dynamic_slice_in_dim(dmod_all, me * n_ada, n_ada, axis=2)
    g_ada = []
    for l in range(n_layers):
        (gl,) = _mm(c_all, dmod_loc[l], mode="tn", tm=d, tn=n_ada, tk=c_all.shape[0], a_pre=silu, name=f"dw_ada_{l}",
                    outs=[((d, n_ada), F32, (d, n_ada), lambda i, j: (0, 0))], epilogue=_store_epilogue([F32]))
        g_ada.append(gl)
    g_ada = jnp.stack(g_ada)

    results = {}

    def update(name, parts):
        shape = weights[name].shape
        as3d = lambda t: t.reshape((-1,) + shape[-2:])
        outs = _adamw(parts.reshape((-1,) + parts.shape[-3:]), as3d(weights[name]), as3d(moments_m[name]),
                      as3d(moments_v[name]), f"adamw_{name}")
        results[name] = [o.reshape(shape) for o in outs]

    update("w_ada", g_ada[:, None])
    update("w_ffn_up", p_up)
    update("w_ffn_down", p_down)
    update("w_in", p_in[..., :N_IN])
    update("w_out", p_out)
    sm_shapes = [weights[k].shape for k in SMALL_NAMES]
    flat = lambda src: _flatten([src[k] for k in SMALL_NAMES], 16 * LANES)
    sm_out = _adamw(flat(small_grads)[None, None], flat(weights)[None], flat(moments_m)[None], flat(moments_v)[None],
                    "adamw_small")
    for k, vals in zip(SMALL_NAMES, zip(*[_unflatten(o, sm_shapes) for o in sm_out])):
        results[k] = list(vals)

    outs = [loss, grad_x]
    for n in range(4):
        outs += [results[k][n] for k in WEIGHT_NAMES]
    return tuple(outs)
```

```python
import functools
import math

import jax
import jax.numpy as jnp
from jax import lax
from jax.experimental import pallas as pl
from jax.experimental.pallas import tpu as pltpu

F32 = jnp.float32
BF16 = jnp.bfloat16

N_DEV = 8
D_MODEL = 1024
D_FF = 2816
FF_SHARD = 2 * D_FF // N_DEV
N_FF_SHARD = D_FF // FF_SHARD
HEAD_DIM = 64
LRU_W = 512
ATT_W = 256
N_IN = 2564
N_IN_PAD = 2688
LANES = 128
SUBLANES = 8
BLK = 256
TQ = 512
KB_PER_Q = TQ // BLK
EPS = 1e-6
LRU_C = 8.0
NEG_BIG = -1e30
VMEM_LIMIT_BYTES = 48 * 1024 * 1024

ADAM_LR, ADAM_B1, ADAM_B2, ADAM_EPS, ADAM_WD, ADAM_STEP = 0.001, 0.9, 0.999, 1e-08, 0.01, 10

COL_SBQ, COL_SBK, COL_SBV = 8, 10, 12
COL_FXV, COL_FXF = 18, 20

NN = (((1,), (0,)), ((), ()))
NT = (((1,), (1,)), ((), ()))
TN = (((0,), (0,)), ((), ()))


def _params(n_axes):
    return pltpu.CompilerParams(dimension_semantics=("arbitrary",) * n_axes, vmem_limit_bytes=VMEM_LIMIT_BYTES)


def _tok_tile(seq):
    for t in (512, 256, 128):
        if seq % t == 0:
            return t
    raise ValueError(f"sequence length {seq} is not a multiple of 128")


def _dot(a, b, dims=NN):
    return lax.dot_general(a, b, dims, preferred_element_type=F32)


def _sigmoid(x):
    return 1.0 / (1.0 + jnp.exp(-x))


def _softplus(x):
    return jnp.maximum(x, 0.0) + jnp.log(1.0 + jnp.exp(-jnp.abs(x)))


def _gelu_parts(x):
    k0, k1 = math.sqrt(2.0 / math.pi), 0.044715
    t = jnp.tanh(k0 * (x + k1 * x * x * x))
    gelu = 0.5 * x * (1.0 + t)
    dgelu = 0.5 * (1.0 + t) + 0.5 * x * (1.0 - t * t) * k0 * (1.0 + 3.0 * k1 * x * x)
    return gelu, dgelu


def _neg_expm1(x):
    series = -x * (1.0 + x * (0.5 + x * (1.0 / 6.0 + x * (1.0 / 24.0 + x * (1.0 / 120.0 + x * (1.0 / 720.0))))))
    return jnp.where(x > -0.25, series, 1.0 - jnp.exp(x))


def _split2(x):
    hi = x.astype(BF16)
    lo = (x - hi.astype(F32)).astype(BF16)
    return hi, lo


def _split3(x):
    hi = x.astype(BF16)
    r = x - hi.astype(F32)
    mid = r.astype(BF16)
    lo = (r - mid.astype(F32)).astype(BF16)
    return hi, mid, lo


def _rows_to_block(rows, width):
    r = lax.broadcasted_iota(jnp.int32, (SUBLANES, width), 0)
    out = jnp.zeros((SUBLANES, width), F32)
    for n, v in enumerate(rows):
        out = jnp.where(r == n, jnp.broadcast_to(v, (SUBLANES, width)), out)
    return out


def _colsum(x):
    return jnp.sum(x, axis=0, keepdims=True)


def _exchange(gathers, scatters, name, two_level=False):
    assert not (two_level and scatters)
    n_g = len(gathers)
    ops = [a for a, _ in gathers] + [a for a, _ in scatters]
    n = len(ops)
    out_shape = [jax.ShapeDtypeStruct(a.shape[:nl] + (N_DEV,) + a.shape[nl:], a.dtype) for a, nl in gathers]
    out_shape += [jax.ShapeDtypeStruct(a.shape, a.dtype) for a, _ in scatters]
    items = []
    for k, (a, nl) in enumerate(list(gathers) + list(scatters)):
        for flat in range(math.prod(a.shape[:nl])):
            idx, rem = [], flat
            for dim in reversed(a.shape[:nl]):
                idx.insert(0, rem % dim)
                rem //= dim
            items.append((k, tuple(idx)))
    n_items = len(items)

    def body(*refs):
        ins, outs = refs[:n], refs[n:2 * n]
        send_sems, recv_sems, local_sems = refs[2 * n:]
        x, y, c = lax.axis_index("x"), lax.axis_index("y"), lax.axis_index("c")
        me = 4 * x + 2 * y + c

        def at(ref, idx):
            return ref.at[idx] if idx else ref

        def src(it, peer):
            k, idx = items[it]
            return at(ins[k], idx) if k < n_g else at(ins[k], idx + (peer,))

        def slot(it, s):
            k, idx = items[it]
            return at(outs[k], idx + (s,))

        def remote(it, rel, source, s, to):
            return pltpu.make_async_remote_copy(
                src_ref=source, dst_ref=slot(it, s), send_sem=send_sems.at[it, rel], recv_sem=recv_sems.at[it, rel],
                device_id=to, device_id_type=pl.DeviceIdType.MESH)

        local = [pltpu.make_async_copy(src(it, me), slot(it, me), local_sems.at[it]) for it in range(n_items)]
        for cp in local:
            cp.start()

        if not two_level:
            started = []
            for r in range(1, N_DEV):
                px = 1 - x if (r >> 2) & 1 else x
                py = 1 - y if (r >> 1) & 1 else y
                pc = 1 - c if r & 1 else c
                for it in range(n_items):
                    cp = remote(it, r - 1, src(it, 4 * px + 2 * py + pc), me, (px, py, pc))
                    cp.start()
                    started.append(cp)
            for cp in started:
                cp.wait()
        else:
            sibling, chips = (x, y, 1 - c), [(1 - x, y), (x, 1 - y), (1 - x, 1 - y)]
            sib = 4 * x + 2 * y + (1 - c)
            started = []
            for it in range(n_items):
                started.append(remote(it, 0, src(it, me), me, sibling))
                started += [remote(it, 1 + j, src(it, me), me, (cx, cy, c)) for j, (cx, cy) in enumerate(chips)]
            for cp in started:
                cp.start()
            for j, (cx, cy) in enumerate(chips):
                s = 4 * cx + 2 * cy + c
                for it in range(n_items):
                    remote(it, 1 + j, slot(it, s), s, sibling).wait_recv()
                    cp = remote(it, 4 + j, slot(it, s), s, sibling)
                    cp.start()
                    started.append(cp)
            for it in range(n_items):
                remote(it, 0, slot(it, sib), sib, sibling).wait_recv()
                for j, (cx, cy) in enumerate(chips):
                    s = 4 * cx + 2 * cy + (1 - c)
                    remote(it, 4 + j, slot(it, s), s, sibling).wait_recv()
            for cp in started:
                cp.wait_send()
        for cp in local:
            cp.wait()

    hbm = pl.BlockSpec(memory_space=pltpu.HBM)
    return pl.pallas_call(
        body, name=name, out_shape=out_shape,
        in_specs=[hbm] * n, out_specs=[hbm] * n,
        scratch_shapes=[pltpu.SemaphoreType.DMA((n_items, N_DEV - 1)), pltpu.SemaphoreType.DMA((n_items, N_DEV - 1)),
                        pltpu.SemaphoreType.DMA((n_items,))],
    )(*ops)


def _lead_items(ops):
    items = []
    for k, (a, nl) in enumerate(ops):
        for flat in range(math.prod(a.shape[:nl])):
            idx, rem = [], flat
            for dim in reversed(a.shape[:nl]):
                idx.insert(0, rem % dim)
                rem //= dim
            items.append((k, tuple(idx)))
    return items


def _flight_copies(ops, srcs, lands, send_sems, recv_sems):
    x, y, c = lax.axis_index("x"), lax.axis_index("y"), lax.axis_index("c")
    me = 4 * x + 2 * y + c
    copies = []
    for r in range(1, N_DEV):
        px = 1 - x if (r >> 2) & 1 else x
        py = 1 - y if (r >> 1) & 1 else y
        pc = 1 - c if r & 1 else c
        for it, (k, idx) in enumerate(_lead_items([(a, nl) for a, nl, _ in ops])):
            src = srcs[k].at[idx + (4 * px + 2 * py + pc,)] if ops[k][2] == "scatter" else (
                srcs[k].at[idx] if idx else srcs[k])
            copies.append(pltpu.make_async_remote_copy(
                src_ref=src, dst_ref=lands[k].at[idx + (me,)],
                send_sem=send_sems.at[it * (N_DEV - 1) + r - 1], recv_sem=recv_sems.at[it * (N_DEV - 1) + r - 1],
                device_id=(px, py, pc), device_id_type=pl.DeviceIdType.MESH))
    return copies


def _flight_start(ops, name):
    n = len(ops)
    me = 4 * lax.axis_index("x") + 2 * lax.axis_index("y") + lax.axis_index("c")
    srcs, lands = [], []
    for a, nl, kind in ops:
        if kind == "scatter":
            own, shape = lax.dynamic_slice_in_dim(a, me, 1, axis=nl), a.shape
        else:
            own, shape = jnp.expand_dims(a, nl), a.shape[:nl] + (N_DEV,) + a.shape[nl:]
        start = (0,) * nl + (me,) + (0,) * (len(shape) - nl - 1)
        lands.append(pltpu.with_memory_space_constraint(
            lax.dynamic_update_slice(lax.empty(shape, a.dtype), own, start), pltpu.HBM))
        srcs.append(pltpu.with_memory_space_constraint(a, pltpu.HBM))

    def body(*refs):
        for cp in _flight_copies(ops, refs[:n], refs[n:2 * n], refs[2 * n], refs[2 * n + 1]):
            cp.start()
        refs[-1][...] = jnp.zeros_like(refs[-1])

    hbm, sem = pl.BlockSpec(memory_space=pltpu.HBM), pl.BlockSpec(memory_space=pltpu.SEMAPHORE)
    n_items = len(_lead_items([(a, nl) for a, nl, _ in ops]))
    sems = pltpu.SemaphoreType.DMA((n_items * (N_DEV - 1),))
    res = pl.pallas_call(
        body, name=name,
        out_shape=[sems, sems] + [pltpu.HBM(a.shape, a.dtype) for a in srcs + lands]
        + [jax.ShapeDtypeStruct((SUBLANES, LANES), F32)],
        in_specs=[hbm] * (2 * n), out_specs=[sem, sem] + [hbm] * (2 * n) + [pl.BlockSpec(memory_space=pltpu.VMEM)],
        input_output_aliases={i: 2 + i for i in range(2 * n)},
        compiler_params=pltpu.CompilerParams(has_side_effects=pltpu.SideEffectType.DATAFLOW_SIDE_EFFECTING),
    )(*srcs, *lands)
    return (ops, res[0], res[1], res[2:2 + n], res[2 + n:2 + 2 * n]), res[-1]


def _flight_wait(handle, after, name):
    ops, send_sems, recv_sems, srcs, lands = handle
    n = len(ops)

    def body(*refs):
        for cp in _flight_copies(ops, refs[:n], refs[n:2 * n], refs[2 * n], refs[2 * n + 1]):
            cp.wait_send()
            cp.wait_recv()

    hbm, sem = pl.BlockSpec(memory_space=pltpu.HBM), pl.BlockSpec(memory_space=pltpu.SEMAPHORE)
    res = pl.pallas_call(
        body, name=name, out_shape=[pltpu.HBM(a.shape, a.dtype) for a in list(srcs) + list(lands)],
        in_specs=[hbm] * (2 * n) + [sem, sem, pl.BlockSpec(memory_space=pl.ANY)], out_specs=[hbm] * (2 * n),
        input_output_aliases={i: i for i in range(2 * n)},
        compiler_params=pltpu.CompilerParams(has_side_effects=pltpu.SideEffectType.DATAFLOW_SIDE_EFFECTING),
    )(*srcs, *lands, send_sems, recv_sems, after)
    return res[n:]


def _mm(a, b, *, mode, tm, tn, tk, outs, epilogue, name, extras=(), a_lead=(), b_lead=(), a_pre=None,
        a_spec=None, b_spec=None, shape=None, ksub=1):
    if shape is not None:
        mdim, ndim, kdim = shape
    else:
        if mode == "tn":
            kdim, mdim = a.shape[-2:]
        else:
            mdim, kdim = a.shape[-2:]
        ndim = b.shape[-2] if mode == "nt" else b.shape[-1]
    assert mdim % tm == 0 and ndim % tn == 0 and kdim % tk == 0, (name, mdim, ndim, kdim, tm, tn, tk)
    ni, nj, nk = mdim // tm, ndim // tn, kdim // tk
    a_lead, b_lead = tuple(a_lead), tuple(b_lead)
    a_block = (None,) * len(a_lead) + ((tk, tm) if mode == "tn" else (tm, tk))
    b_block = (None,) * len(b_lead) + ((tn, tk) if mode == "nt" else (tk, tn))
    dims = {"nn": NN, "nt": NT, "tn": TN}[mode]
    ne, no = len(extras), len(outs)

    def a_index(i, j, k):
        return a_lead + ((k, i) if mode == "tn" else (i, k))

    def b_index(i, j, k):
        return b_lead + ((j, k) if mode == "nt" else (k, j))

    if a_spec is not None:
        a_block, a_index = a_spec
    if b_spec is not None:
        b_block, b_index = b_spec

    def body(*refs):
        a_ref, b_ref = refs[0], refs[1]
        e_refs, o_refs = refs[2:2 + ne], refs[2 + ne:2 + ne + no]
        if ksub == 1:
            av = a_ref[...] if a_pre is None else a_pre(a_ref[...])
            p = _dot(av.astype(BF16), b_ref[...].astype(BF16), dims)
        else:
            p = _dot(a_ref[0], b_ref[0], dims)
            for s in range(1, ksub):
                p = p + _dot(a_ref[s], b_ref[s], dims)
        if nk == 1:
            epilogue(p, e_refs, o_refs)
        else:
            acc = refs[-1]
            k = pl.program_id(2)

            @pl.when(k == 0)
            def _():
                acc[...] = p

            @pl.when(k > 0)
            def _():
                acc[...] += p

            @pl.when(k == nk - 1)
            def _():
                epilogue(acc[...], e_refs, o_refs)

    in_specs = [pl.BlockSpec(a_block, a_index), pl.BlockSpec(b_block, b_index)]
    in_specs += [pl.BlockSpec(blk, functools.partial(lambda i, j, k, f: f(i, j), f=f)) for _, blk, f in extras]
    out_specs = [pl.BlockSpec(blk, functools.partial(lambda i, j, k, f: f(i, j), f=f)) for _, _, blk, f in outs]
    res = pl.pallas_call(
        body, name=name, grid=(ni, nj, nk), in_specs=in_specs, out_specs=out_specs,
        out_shape=[jax.ShapeDtypeStruct(s, d) for s, d, _, _ in outs],
        scratch_shapes=[pltpu.VMEM((tm, tn), F32)] if nk > 1 else [],
        compiler_params=_params(3),
    )(a, b, *[e[0] for e in extras])
    return res


def _store_epilogue(dtypes):
    def epi(p, e_refs, o_refs):
        for o, dt in zip(o_refs, dtypes):
            o[...] = p.astype(dt)
    return epi


def _normmod(x, gn, scale, shift, seq, name):
    m, d = x.shape
    tm = _tok_tile(seq)
    tpb = seq // tm

    def body(x_ref, gn_ref, sc_ref, sh_ref, h_ref):
        xv = x_ref[...]
        rstd = lax.rsqrt(jnp.mean(xv * xv, axis=-1, keepdims=True) + EPS)
        h_ref[...] = (xv * rstd * gn_ref[...] * (1.0 + sc_ref[0]) + sh_ref[0]).astype(BF16)

    vec = pl.BlockSpec((1, 1, d), lambda i: (i // tpb, 0, 0))
    return pl.pallas_call(
        body, name=name, grid=(m // tm,),
        in_specs=[pl.BlockSpec((tm, d), lambda i: (i, 0)), pl.BlockSpec((1, d), lambda i: (0, 0)), vec, vec],
        out_specs=pl.BlockSpec((tm, d), lambda i: (i, 0)),
        out_shape=jax.ShapeDtypeStruct((m, d), BF16), compiler_params=_params(1),
    )(x, gn, scale, shift)


def _normmod_bwd_epilogue(p, e_refs, o_refs):
    x_ref, dxo_ref, gn_ref, sc_ref = e_refs
    xv = x_ref[...]
    rstd = lax.rsqrt(jnp.mean(xv * xv, axis=-1, keepdims=True) + EPS)
    xhat = xv * rstd
    gn, sc1 = gn_ref[...], 1.0 + sc_ref[0]
    dxhat = p * (gn * sc1)
    dx = rstd * (dxhat - xhat * jnp.mean(dxhat * xhat, axis=-1, keepdims=True))
    o_refs[0][...] = dxo_ref[...] + dx
    t = p * xhat
    o_refs[1][0] = _rows_to_block([_colsum(p), _colsum(t * gn), _colsum(t * sc1)], p.shape[1])


def _residual_bwd(dx, f, gate, fac, seq, name):
    m, d = dx.shape
    tm = _tok_tile(seq)
    tpb = seq // tm

    def body(dx_ref, f_ref, g_ref, df_ref, dg_ref):
        dxv = dx_ref[...]
        df_ref[...] = ((fac * (1.0 + g_ref[0])) * dxv).astype(BF16)
        dg_ref[0] = _rows_to_block([_colsum((fac * dxv) * f_ref[...].astype(F32))], d)

    tile = pl.BlockSpec((tm, d), lambda i: (i, 0))
    return pl.pallas_call(
        body, name=name, grid=(m // tm,),
        in_specs=[tile, tile, pl.BlockSpec((1, 1, d), lambda i: (i // tpb, 0, 0))],
        out_specs=[tile, pl.BlockSpec((1, SUBLANES, d), lambda i: (i, 0, 0))],
        out_shape=[jax.ShapeDtypeStruct((m, d), BF16), jax.ShapeDtypeStruct((m // tm, SUBLANES, d), F32)],
        compiler_params=_params(1),
    )(dx, f, gate)


def _loss_head(y, target, seq):
    m, d = y.shape
    tm = _tok_tile(seq)

    def body(y_ref, t_ref, dy_ref, l_ref):
        err = y_ref[...] - t_ref[...]
        dy_ref[...] = err * (1.0 / d)
        part = 0.5 * jnp.sum(jnp.mean(err * err, axis=-1, keepdims=True), axis=0, keepdims=True)
        l_ref[0] = jnp.broadcast_to(part, (SUBLANES, LANES))

    tile = pl.BlockSpec((tm, d), lambda i: (i, 0))
    return pl.pallas_call(
        body, name="loss_head", grid=(m // tm,), in_specs=[tile, tile],
        out_specs=[tile, pl.BlockSpec((1, SUBLANES, LANES), lambda i: (i, 0, 0))],
        out_shape=[jax.ShapeDtypeStruct((m, d), F32), jax.ShapeDtypeStruct((m // tm, SUBLANES, LANES), F32)],
        compiler_params=_params(1),
    )(y, target)


def _ffn_fwd(x, h, wup, wdown, gate, seq, tag):
    m, d = x.shape
    tm = _tok_tile(seq)
    tpb = seq // tm

    def up_body(h_ref, wg_ref, wu_ref, a_ref, gu_ref):
        hv = h_ref[...]
        g, u = _dot(hv, wg_ref[...]), _dot(hv, wu_ref[...])
        a_ref[...] = (g * _sigmoid(g) * u).astype(BF16)
        gu_ref[0] = g.astype(BF16)
        gu_ref[1] = u.astype(BF16)

    wblk = (None, d, FF_SHARD)
    a, gu = pl.pallas_call(
        up_body, name=f"ffn_up_{tag}", grid=(N_FF_SHARD, m // tm),
        in_specs=[pl.BlockSpec((tm, d), lambda j, i: (i, 0)),
                  pl.BlockSpec(wblk, lambda j, i: (j, 0, 0)),
                  pl.BlockSpec(wblk, lambda j, i: (j + N_FF_SHARD, 0, 0))],
        out_specs=[pl.BlockSpec((None, tm, FF_SHARD), lambda j, i: (j, i, 0)),
                   pl.BlockSpec((2, None, tm, FF_SHARD), lambda j, i: (0, j, i, 0))],
        out_shape=[jax.ShapeDtypeStruct((N_FF_SHARD, m, FF_SHARD), BF16),
                   jax.ShapeDtypeStruct((2, N_FF_SHARD, m, FF_SHARD), BF16)],
        compiler_params=_params(2),
    )(h, wup, wup)

    def down_epilogue(p, e_refs, o_refs):
        x_ref, g_ref = e_refs
        o_refs[0][...] = x_ref[...] + (0.5 * (1.0 + g_ref[0])) * p
        o_refs[1][...] = p.astype(BF16)

    wdown3 = wdown.reshape(N_FF_SHARD, FF_SHARD, d)
    x_out, f = _mm(a, wdown3, mode="nn", tm=tm, tn=d, tk=D_FF, ksub=N_FF_SHARD, name=f"ffn_down_{tag}",
                   shape=(m, d, D_FF), a_spec=((N_FF_SHARD, tm, FF_SHARD), lambda i, j, k: (0, i, 0)),
                   b_spec=((N_FF_SHARD, FF_SHARD, d), lambda i, j, k: (0, 0, 0)),
                   extras=[(x, (tm, d), lambda i, j: (i, 0)), (gate, (1, 1, d), lambda i, j: (i // tpb, 0, 0))],
                   outs=[((m, d), F32, (tm, d), lambda i, j: (i, 0)), ((m, d), BF16, (tm, d), lambda i, j: (i, 0))],
                   epilogue=down_epilogue)
    return x_out, (a, gu, f)


def _ffn_bwd(dx_out, x, h, saved, wup, wdown, gn, scale, gate, seq, tag, on_grads):
    a, gu, f = saved
    m, d = x.shape
    tm = _tok_tile(seq)
    tpb = seq // tm
    df, dgate_parts = _residual_bwd(dx_out, f, gate, 0.5, seq, f"ffn_res_bwd_{tag}")

    def act_bwd_epilogue(p, e_refs, o_refs):
        g, u = e_refs[0][0].astype(F32), e_refs[0][1].astype(F32)
        sg = _sigmoid(g)
        o_refs[0][0] = (p * u * (sg * (1.0 + g * (1.0 - sg)))).astype(BF16)
        o_refs[0][1] = (p * (g * sg)).astype(BF16)

    gu_blk = (2, None, tm, FF_SHARD)
    (dgu,) = _mm(df, wdown, mode="nt", tm=tm, tn=FF_SHARD, tk=d, name=f"ffn_down_dx_{tag}", shape=(m, D_FF, d),
                 b_spec=((FF_SHARD, d), lambda i, j, k: (j, 0)),
                 extras=[(gu, gu_blk, lambda i, j: (0, j, i, 0))],
                 outs=[((2, N_FF_SHARD, m, FF_SHARD), BF16, gu_blk, lambda i, j: (0, j, i, 0))],
                 epilogue=act_bwd_epilogue)
    tt = 2 * tm if m % (2 * tm) == 0 else tm
    (dwdown,) = _mm(a, df, mode="tn", tm=FF_SHARD, tn=d, tk=tt, name=f"ffn_dwdown_{tag}", shape=(D_FF, d, m),
                    a_spec=((None, tt, FF_SHARD), lambda i, j, k: (i, k, 0)),
                    outs=[((D_FF, d), BF16, (FF_SHARD, d), lambda i, j: (i, 0))], epilogue=_store_epilogue([BF16]))
    dgu8 = dgu.reshape(2 * N_FF_SHARD, m, FF_SHARD)
    (dwup,) = _mm(h, dgu8, mode="tn", tm=d, tn=FF_SHARD, tk=tt, name=f"ffn_dwup_{tag}", shape=(d, 2 * D_FF, m),
                  b_spec=((None, tt, FF_SHARD), lambda i, j, k: (j, k, 0)),
                  outs=[((2 * N_FF_SHARD, d, FF_SHARD), BF16, (None, d, FF_SHARD), lambda i, j: (j, 0, 0))],
                  epilogue=_store_epilogue([BF16]))
    scale = scale + on_grads(dwup, dwdown)[0, 0]
    dx, nm_parts = _mm(dgu8, wup, mode="nt", tm=tm, tn=d, tk=D_FF, ksub=N_FF_SHARD, name=f"ffn_up_dx_{tag}",
                       shape=(m, d, 2 * D_FF), a_spec=((N_FF_SHARD, tm, FF_SHARD), lambda i, j, k: (k, i, 0)),
                       b_spec=((N_FF_SHARD, d, FF_SHARD), lambda i, j, k: (k, 0, 0)),
                       extras=[(x, (tm, d), lambda i, j: (i, 0)), (dx_out, (tm, d), lambda i, j: (i, 0)),
                               (gn, (1, d), lambda i, j: (0, 0)), (scale, (1, 1, d), lambda i, j: (i // tpb, 0, 0))],
                       outs=[((m, d), F32, (tm, d), lambda i, j: (i, 0)),
                             ((m // tm, SUBLANES, d), F32, (1, SUBLANES, d), lambda i, j: (i, 0, 0))],
                       epilogue=_normmod_bwd_epilogue)
    return dx, nm_parts, dgate_parts


def _shift_down(ext, n, rows):
    if n:
        ext = pltpu.roll(ext, n, 0)
    return ext[SUBLANES:SUBLANES + rows]


def _lru_gates(u, wr_ref, br_ref, wi_ref, bi_ref, lam_ref):
    ub = u.astype(BF16)
    r = _sigmoid(_dot(ub, wr_ref[...]) + br_ref[...])
    ig = _sigmoid(_dot(ub, wi_ref[...]) + bi_ref[...])
    sp = _softplus(-lam_ref[...])
    log_a = (-LRU_C * r) * sp
    a = jnp.exp(log_a)
    mult = jnp.sqrt(_neg_expm1(2.0 * log_a))
    return r, ig, sp, a, mult


def _conv(ext, cw_ref, cb_ref, rows):
    u = cb_ref[...] + cw_ref[3:4, :] * _shift_down(ext, 0, rows)
    for k in range(3):
        u = u + cw_ref[k:k + 1, :] * _shift_down(ext, 3 - k, rows)
    return u


def _lru_halo_spec(seq, ts):
    return pl.BlockSpec((SUBLANES, LRU_W),
                        lambda b, i: (jnp.maximum(b * (seq // SUBLANES) + i * (ts // SUBLANES) - 1, 0), 0))


def _lru_fwd(proj32, conv_w, conv_b, wr, br, wi, bi, lam, batch, seq):
    m = proj32.shape[0]
    ts = _tok_tile(seq)
    nt = seq // ts
    row = lambda b, i: (b * nt + i, 0)

    def body(x_ref, halo_ref, g_ref, cw_ref, cb_ref, wr_ref, br_ref, wi_ref, bi_ref, lam_ref,
             y_ref, h_ref, a_scr, b_scr, carry):
        i = pl.program_id(1)
        halo = jnp.where(i > 0, halo_ref[...], 0.0)
        ext = jnp.concatenate([halo, x_ref[...]], axis=0)
        u = _conv(ext, cw_ref, cb_ref, ts)
        _, ig, _, a, mult = _lru_gates(u, wr_ref, br_ref, wi_ref, bi_ref, lam_ref)
        a_scr[...] = a
        b_scr[...] = mult * (ig * u)

        @pl.when(i == 0)
        def _():
            carry[...] = jnp.zeros_like(carry)

        rid = lax.broadcasted_iota(jnp.int32, (SUBLANES, LRU_W), 0)

        def chunk(c, hprev):
            off = pl.multiple_of(c * SUBLANES, SUBLANES)
            av, bv = a_scr[pl.ds(off, SUBLANES), :], b_scr[pl.ds(off, SUBLANES), :]
            for d in (1, 2, 4):
                keep = rid >= d
                bv = jnp.where(keep, av * pltpu.roll(bv, d, 0) + bv, bv)
                av = jnp.where(keep, av * pltpu.roll(av, d, 0), av)
            h = av * hprev + bv
            h_ref[pl.ds(off, SUBLANES), :] = h
            return h[SUBLANES - 1:SUBLANES, :]

        carry[...] = lax.fori_loop(0, ts // SUBLANES, chunk, carry[...])
        gelu, _ = _gelu_parts(g_ref[...])
        y_ref[...] = h_ref[...] * gelu

    full = lambda shape: pl.BlockSpec(shape, lambda b, i: (0,) * len(shape))
    return pl.pallas_call(
        body, name="lru_fwd", grid=(batch, nt),
        in_specs=[pl.BlockSpec((ts, LRU_W), row), _lru_halo_spec(seq, ts),
                  pl.BlockSpec((ts, LRU_W), lambda b, i: (b * nt + i, 1)),
                  full((4, LRU_W)), full((1, LRU_W)), full((LRU_W, LRU_W)), full((1, LRU_W)),
                  full((LRU_W, LRU_W)), full((1, LRU_W)), full((1, LRU_W))],
        out_specs=[pl.BlockSpec((ts, LRU_W), row), pl.BlockSpec((ts, LRU_W), row)],
        out_shape=[jax.ShapeDtypeStruct((m, LRU_W), F32), jax.ShapeDtypeStruct((m, LRU_W), F32)],
        scratch_shapes=[pltpu.VMEM((ts, LRU_W), F32), pltpu.VMEM((ts, LRU_W), F32), pltpu.VMEM((1, LRU_W), F32)],
        compiler_params=_params(2),
    )(proj32, proj32, proj32, conv_w, conv_b, wr, br, wi, bi, lam)


def _lru_bwd(dy, proj32, h, conv_w, conv_b, wr, br, wi, bi, lam, batch, seq):
    m = proj32.shape[0]
    ts = _tok_tile(seq)
    nt = seq // ts
    row = lambda b, i: (b * nt + (nt - 1 - i), 0)
    halo = pl.BlockSpec((SUBLANES, LRU_W),
                        lambda b, i: (jnp.maximum(b * (seq // SUBLANES) + (nt - 1 - i) * (ts // SUBLANES) - 1, 0), 0))

    def body(dy_ref, x_ref, xhalo_ref, g_ref, h_ref, hhalo_ref, cw_ref, cb_ref, wr_ref, br_ref, wi_ref, bi_ref,
             lam_ref, dx_ref, dg_ref, dwr_ref, dwi_ref, sums_ref, a_scr, dh_scr, g_scr, carry, du_next):
        b, i = pl.program_id(0), pl.program_id(1)
        first_tile = i == nt - 1

        @pl.when((b == 0) & (i == 0))
        def _():
            dwr_ref[...] = jnp.zeros_like(dwr_ref)
            dwi_ref[...] = jnp.zeros_like(dwi_ref)
            sums_ref[...] = jnp.zeros_like(sums_ref)

        @pl.when(i == 0)
        def _():
            carry[...] = jnp.zeros_like(carry)
            du_next[...] = jnp.zeros_like(du_next)

        xhalo = jnp.where(first_tile, 0.0, xhalo_ref[...])
        ext = jnp.concatenate([xhalo, x_ref[...]], axis=0)
        u = _conv(ext, cw_ref, cb_ref, ts)
        r, ig, sp, a, mult = _lru_gates(u, wr_ref, br_ref, wi_ref, bi_ref, lam_ref)
        gelu, dgelu = _gelu_parts(g_ref[...])
        dyv, hv = dy_ref[...], h_ref[...]
        dg_ref[...] = (dyv * hv * dgelu).astype(BF16)
        a_scr[...] = a
        dh_scr[...] = dyv * gelu

        rid = lax.broadcasted_iota(jnp.int32, (SUBLANES, LRU_W), 0)
        nchunk = ts // SUBLANES

        def chunk(n, cg):
            off = pl.multiple_of((nchunk - 1 - n) * SUBLANES, SUBLANES)
            av, beta = a_scr[pl.ds(off, SUBLANES), :], dh_scr[pl.ds(off, SUBLANES), :]
            alpha = jnp.where(rid == SUBLANES - 1, 1.0, pltpu.roll(av, SUBLANES - 1, 0))
            for d in (1, 2, 4):
                keep = rid + d <= SUBLANES - 1
                beta = jnp.where(keep, beta + alpha * pltpu.roll(beta, SUBLANES - d, 0), beta)
                alpha = jnp.where(keep, alpha * pltpu.roll(alpha, SUBLANES - d, 0), alpha)
            gv = beta + alpha * cg
            g_scr[pl.ds(off, SUBLANES), :] = gv
            return av[0:1, :] * gv[0:1, :]

        carry[...] = lax.fori_loop(0, nchunk, chunk, carry[...])
        gv = g_scr[...]
        hhalo = jnp.where(first_tile, 0.0, hhalo_ref[...])
        hprev = _shift_down(jnp.concatenate([hhalo, hv], axis=0), 1, ts)
        dmult = gv * ig * u
        dig = gv * mult * u
        du = gv * mult * ig
        dlog_a = gv * hprev * a - dmult * a * a / mult
        dr = dlog_a * (-LRU_C * sp)
        dr_pre = dr * r * (1.0 - r)
        di_pre = dig * ig * (1.0 - ig)
        drb, dib, ub = dr_pre.astype(BF16), di_pre.astype(BF16), u.astype(BF16)
        du = du + _dot(drb, wr_ref[...], NT) + _dot(dib, wi_ref[...], NT)
        dwr_ref[...] += _dot(ub, drb, TN)
        dwi_ref[...] += _dot(ub, dib, TN)

        ext_du = jnp.concatenate([du, du_next[...]], axis=0)
        du_next[...] = du[0:SUBLANES, :]
        n_ext = ts + SUBLANES
        dx = cw_ref[3:4, :] * du
        sums = [_colsum(dr_pre), _colsum(di_pre), _colsum(dlog_a * (-LRU_C * r)), _colsum(du)]
        dcw = []
        for k in range(3):
            dx = dx + cw_ref[k:k + 1, :] * pltpu.roll(ext_du, n_ext - (3 - k), 0)[0:ts]
            dcw.append(_colsum(du * _shift_down(ext, 3 - k, ts)))
        dcw.append(_colsum(du * _shift_down(ext, 0, ts)))
        dx_ref[...] = dx.astype(BF16)
        sums_ref[...] += _rows_to_block(sums + dcw, LRU_W)

    full = lambda shape: pl.BlockSpec(shape, lambda b, i: (0,) * len(shape))
    tile = pl.BlockSpec((ts, LRU_W), row)
    return pl.pallas_call(
        body, name="lru_bwd", grid=(batch, nt),
        in_specs=[tile, tile, halo, pl.BlockSpec((ts, LRU_W), lambda b, i: (b * nt + (nt - 1 - i), 1)), tile, halo,
                  full((4, LRU_W)), full((1, LRU_W)), full((LRU_W, LRU_W)), full((1, LRU_W)),
                  full((LRU_W, LRU_W)), full((1, LRU_W)), full((1, LRU_W))],
        out_specs=[tile, tile, full((LRU_W, LRU_W)), full((LRU_W, LRU_W)), full((SUBLANES, LRU_W))],
        out_shape=[jax.ShapeDtypeStruct((m, LRU_W), BF16), jax.ShapeDtypeStruct((m, LRU_W), BF16),
                   jax.ShapeDtypeStruct((LRU_W, LRU_W), F32), jax.ShapeDtypeStruct((LRU_W, LRU_W), F32),
                   jax.ShapeDtypeStruct((SUBLANES, LRU_W), F32)],
        scratch_shapes=[pltpu.VMEM((ts, LRU_W), F32), pltpu.VMEM((ts, LRU_W), F32), pltpu.VMEM((ts, LRU_W), F32),
                        pltpu.VMEM((1, LRU_W), F32), pltpu.VMEM((SUBLANES, LRU_W), F32)],
        compiler_params=_params(2),
    )(dy, proj32, proj32, proj32, h, h, conv_w, conv_b, wr, br, wi, bi, lam)


def _head_masks():
    lane = lax.broadcasted_iota(jnp.int32, (1, LANES), 1)
    return lane < HEAD_DIM


def _stack_heads(x2):
    lo, zero = _head_masks(), jnp.zeros_like(x2)
    return jnp.concatenate([jnp.where(lo, x2, zero), jnp.where(lo, zero, x2)], axis=0)


def _unstack_heads(y):
    return jnp.where(_head_masks(), y[:TQ], y[TQ:])


def _stack_cols(a, b):
    return jnp.concatenate([a, b], axis=0)


def _causal(qi, kb, strict):
    r = jnp.bitwise_and(lax.broadcasted_iota(jnp.int32, (2 * TQ, BLK), 0), TQ - 1) + qi * TQ
    c = lax.broadcasted_iota(jnp.int32, (2 * TQ, BLK), 1) + kb * BLK
    return (c < r) if strict else (c <= r)


def _key_loop(qi, group, carry, descending=False):
    def trip(n, cr):
        done = [n * KB_PER_Q + j for j in range(KB_PER_Q)]
        return group([qi * KB_PER_Q - 1 - t for t in done] if descending else done, cr)

    return lax.fori_loop(0, qi, trip, carry)


def _one_by_one(block):
    def group(kbs, carry):
        for kb in kbs:
            carry = block(kb, carry, False)
        return carry
    return group


def _tri(cmp):
    r = lax.broadcasted_iota(jnp.int32, (BLK, BLK), 0)
    c = lax.broadcasted_iota(jnp.int32, (BLK, BLK), 1)
    return cmp(r, c)


def _dot_split(x, tri):
    hi, lo = _split2(x)
    return _dot(hi, tri) + _dot(lo, tri)


def _sb_fwd(proj16, batch, seq):
    nq = seq // TQ
    scale = HEAD_DIM ** -0.5

    def body(q_ref, k_ref, v_ref, y_ref, t_ref):
        qi = pl.program_id(2)
        qs = _stack_heads(q_ref[0])
        tri_after = _tri(lambda r, c: r > c).astype(BF16)

        def block(kb, carry, masked):
            acc, c = carry
            ks = pl.multiple_of(kb * BLK, BLK)
            k2, v2 = k_ref[0, pl.ds(ks, BLK), :], v_ref[0, pl.ds(ks, BLK), :]
            z = _dot(qs, k2, NT) * scale
            sp = _softplus(z)
            l = -sp
            if masked:
                valid = _causal(qi, kb, True)
                l = jnp.where(valid, l, 0.0)
            w = jnp.exp((z - sp) + _dot_split(l, tri_after) + c)
            if masked:
                w = jnp.where(valid, w, 0.0)
            return acc + _dot(w.astype(BF16), v2), c + jnp.sum(l, axis=1, keepdims=True)

        def group(kbs, carry):
            acc, c = carry
            kv = [(k_ref[0, pl.ds(pl.multiple_of(kb * BLK, BLK), BLK), :],
                   v_ref[0, pl.ds(pl.multiple_of(kb * BLK, BLK), BLK), :]) for kb in kbs]
            zs = [_dot(qs, k2, NT) * scale for k2, _ in kv]
            sps = [_softplus(z) for z in zs]
            afters = [_dot_split(-sp, tri_after) for sp in sps]
            for z, sp, after, (_, v2) in zip(zs, sps, afters, kv):
                acc = acc + _dot(jnp.exp((z - sp) + after + c).astype(BF16), v2)
                c = c - jnp.sum(sp, axis=1, keepdims=True)
            return acc, c

        carry = (jnp.zeros((2 * TQ, LANES), F32), jnp.zeros((2 * TQ, 1), F32))
        first = qi * KB_PER_Q
        for n in reversed(range(KB_PER_Q)):
            carry = block(first + n, carry, True)
        acc, c = _key_loop(qi, group, carry, descending=True)
        y_ref[...] = _unstack_heads(acc)
        t_ref[0] = _unstack_heads(jnp.broadcast_to(c, (2 * TQ, LANES)))

    m = batch * seq
    return pl.pallas_call(
        body, name="sb_fwd", grid=(batch, 2, nq),
        in_specs=[pl.BlockSpec((1, TQ, LANES), lambda b, p, q: (b, q, COL_SBQ + p)),
                  pl.BlockSpec((1, seq, LANES), lambda b, p, q: (b, 0, COL_SBK + p)),
                  pl.BlockSpec((1, seq, LANES), lambda b, p, q: (b, 0, COL_SBV + p))],
        out_specs=[pl.BlockSpec((TQ, LANES), lambda b, p, q: (b * nq + q, p)),
                   pl.BlockSpec((1, TQ, LANES), lambda b, p, q: (p, b * nq + q, 0))],
        out_shape=[jax.ShapeDtypeStruct((m, ATT_W), F32), jax.ShapeDtypeStruct((2, m, LANES), F32)],
        compiler_params=_params(3),
    )(proj16, proj16, proj16)


def _sb_bwd(dy, t, proj16, batch, seq):
    nq = seq // TQ
    scale = HEAD_DIM ** -0.5

    def body(dy_ref, t_ref, q_ref, k_ref, v_ref, dq_ref, dk_ref, dv_ref):
        qi = pl.program_id(2)

        @pl.when(qi == 0)
        def _():
            dk_ref[...] = jnp.zeros_like(dk_ref)
            dv_ref[...] = jnp.zeros_like(dv_ref)

        t2 = t_ref[0]
        qs, dys = _stack_heads(q_ref[0]), _stack_heads(dy_ref[...].astype(BF16))
        tot = _stack_cols(t2[:, 0:1], t2[:, HEAD_DIM:HEAD_DIM + 1])
        tri_incl = _tri(lambda r, c: r <= c).astype(BF16)
        tri_excl = _tri(lambda r, c: r < c).astype(BF16)

        def block(kb, carry, masked):
            dq, pc, ec = carry
            ks = pl.multiple_of(kb * BLK, BLK)
            k2, v2 = k_ref[0, pl.ds(ks, BLK), :], v_ref[0, pl.ds(ks, BLK), :]
            z = _dot(qs, k2, NT) * scale
            sp = _softplus(z)
            l, b = -sp, z - sp
            sig = jnp.exp(b)
            if masked:
                valid = _causal(qi, kb, True)
                l = jnp.where(valid, l, 0.0)
            after = tot - (pc + _dot_split(l, tri_incl))
            w = jnp.exp(b + after)
            if masked:
                w = jnp.where(valid, w, 0.0)
            e = _dot(dys, v2, NT) * w
            et = ec + _dot_split(e, tri_excl)
            dz = e * (1.0 - sig) - et * sig
            if masked:
                dz = jnp.where(valid, dz, 0.0)
            dzb = (dz * scale).astype(BF16)
            dk_ref[0, pl.ds(ks, BLK), :] += _dot(dzb, qs, TN)
            dv_ref[0, pl.ds(ks, BLK), :] += _dot(w.astype(BF16), dys, TN)
            return (dq + _dot(dzb, k2), pc + jnp.sum(l, axis=1, keepdims=True),
                    ec + jnp.sum(e, axis=1, keepdims=True))

        col = jnp.zeros((2 * TQ, 1), F32)
        first = qi * KB_PER_Q
        carry = _key_loop(qi, _one_by_one(block),(jnp.zeros((2 * TQ, LANES), F32), col, col))
        for n in range(KB_PER_Q):
            carry = block(first + n, carry, True)
        dq_ref[...] = _unstack_heads(carry[0])

    m = batch * seq
    whole = lambda col: pl.BlockSpec((1, seq, LANES), lambda b, p, q: (b, 0, col + p))
    return pl.pallas_call(
        body, name="sb_bwd", grid=(batch, 2, nq),
        in_specs=[pl.BlockSpec((TQ, LANES), lambda b, p, q: (b * nq + q, p)),
                  pl.BlockSpec((1, TQ, LANES), lambda b, p, q: (p, b * nq + q, 0)),
                  pl.BlockSpec((1, TQ, LANES), lambda b, p, q: (b, q, COL_SBQ + p)),
                  whole(COL_SBK), whole(COL_SBV)],
        out_specs=[pl.BlockSpec((TQ, LANES), lambda b, p, q: (b * nq + q, p)), whole(0), whole(0)],
        out_shape=[jax.ShapeDtypeStruct((m, ATT_W), F32), jax.ShapeDtypeStruct((batch, seq, ATT_W), F32),
                   jax.ShapeDtypeStruct((batch, seq, ATT_W), F32)],
        compiler_params=_params(3),
    )(dy, t, proj16, proj16, proj16)


def _fox_pre(proj32, gq, gk, bf, group_mean, batch, seq):
    m = proj32.shape[0]
    ts = _tok_tile(seq)
    nt = seq // ts

    def body(q_ref, k_ref, f_ref, gq_ref, gk_ref, bf_ref, gm_ref, fq_ref, fk_ref, fc_ref, carry):
        i = pl.program_id(1)

        @pl.when(i == 0)
        def _():
            carry[...] = jnp.zeros_like(carry)

        gm = gm_ref[...]
        for src, g_ref, dst in ((q_ref, gq_ref, fq_ref), (k_ref, gk_ref, fk_ref)):
            v = src[...]
            ms = _dot_split(v * v, gm)
            dst[...] = (v * lax.rsqrt(ms + EPS) * g_ref[...]).astype(BF16)
        z = f_ref[...] + bf_ref[...]
        lf = jnp.minimum(z, 0.0) - jnp.log(1.0 + jnp.exp(-jnp.abs(z)))
        r = lax.broadcasted_iota(jnp.int32, (ts, ts), 0)
        c = lax.broadcasted_iota(jnp.int32, (ts, ts), 1)
        tri = (r >= c).astype(BF16)
        hi, mid, low = _split3(lf)
        fc = _dot(tri, hi) + _dot(tri, mid) + _dot(tri, low) + carry[...]
        fc_ref[...] = fc
        carry[...] = fc[ts - 1:ts, :]

    full = lambda shape: pl.BlockSpec(shape, lambda b, i: (0,) * len(shape))
    return pl.pallas_call(
        body, name="fox_pre", grid=(batch, nt),
        in_specs=[pl.BlockSpec((ts, ATT_W), lambda b, i: (b * nt + i, 7)),
                  pl.BlockSpec((ts, ATT_W), lambda b, i: (b * nt + i, 8)),
                  pl.BlockSpec((ts, LANES), lambda b, i: (b * nt + i, COL_FXF)),
                  full((1, ATT_W)), full((1, ATT_W)), full((1, LANES)), full((ATT_W, ATT_W))],
        out_specs=[pl.BlockSpec((ts, ATT_W), lambda b, i: (b * nt + i, 0)),
                   pl.BlockSpec((ts, ATT_W), lambda b, i: (b * nt + i, 0)),
                   pl.BlockSpec((ts, LANES), lambda b, i: (b * nt + i, 0))],
        out_shape=[jax.ShapeDtypeStruct((m, ATT_W), BF16), jax.ShapeDtypeStruct((m, ATT_W), BF16),
                   jax.ShapeDtypeStruct((m, LANES), F32)],
        scratch_shapes=[pltpu.VMEM((1, LANES), F32)],
        compiler_params=_params(2),
    )(proj32, proj32, proj32, gq, gk, bf, group_mean)


def _fox_specs(batch, seq):
    nq = seq // TQ
    return dict(
        qblk=pl.BlockSpec((1, TQ, LANES), lambda b, p, q: (b, q, p)),
        whole=pl.BlockSpec((1, seq, LANES), lambda b, p, q: (b, 0, p)),
        vwhole=pl.BlockSpec((1, seq, LANES), lambda b, p, q: (b, 0, COL_FXV + p)),
        fcol=pl.BlockSpec((1, 1, TQ, 2), lambda b, p, q: (b, p, q, 0)),
        frow=pl.BlockSpec((1, 1, 2, seq), lambda b, p, q: (b, p, 0, 0)),
        rows=pl.BlockSpec((TQ, LANES), lambda b, p, q: (b * nq + q, p)),
        stat=pl.BlockSpec((1, TQ, LANES), lambda b, p, q: (p, b * nq + q, 0)),
    )


def _fox_logits(qs, k2, fq_col, fr_ref, ks, is_a, scale):
    fk_row = jnp.where(is_a, fr_ref[0, 0, 0:1, pl.ds(ks, BLK)], fr_ref[0, 0, 1:2, pl.ds(ks, BLK)])
    return _dot(qs, k2, NT) * scale + fq_col - fk_row


def _fox_fwd(fq, fk, proj16, fcol, frow, batch, seq):
    nq = seq // TQ
    scale = HEAD_DIM ** -0.5

    def body(q_ref, k_ref, v_ref, fc_ref, fr_ref, y_ref, lse_ref):
        qi = pl.program_id(2)
        qs = _stack_heads(q_ref[0])
        fcv = fc_ref[0, 0]
        fq_col = _stack_cols(fcv[:, 0:1], fcv[:, 1:2])
        is_a = lax.broadcasted_iota(jnp.int32, (2 * TQ, 1), 0) < TQ

        def block(kb, carry, masked):
            acc, mx, den = carry
            ks = pl.multiple_of(kb * BLK, BLK)
            k2, v2 = k_ref[0, pl.ds(ks, BLK), :], v_ref[0, pl.ds(ks, BLK), :]
            s = _fox_logits(qs, k2, fq_col, fr_ref, ks, is_a, scale)
            if masked:
                s = jnp.where(_causal(qi, kb, False), s, NEG_BIG)
            mx_new = jnp.maximum(mx, jnp.max(s, axis=1, keepdims=True))
            p = jnp.exp(s - mx_new)
            alpha = jnp.exp(mx - mx_new)
            return (alpha * acc + _dot(p.astype(BF16), v2), mx_new, alpha * den + jnp.sum(p, axis=1, keepdims=True))

        first = qi * KB_PER_Q
        carry = (jnp.zeros((2 * TQ, LANES), F32), jnp.full((2 * TQ, 1), NEG_BIG, F32), jnp.zeros((2 * TQ, 1), F32))
        carry = _key_loop(qi, _one_by_one(block),carry)
        for n in range(KB_PER_Q):
            carry = block(first + n, carry, True)
        acc, mx, den = carry
        y_ref[...] = _unstack_heads(acc / den)
        lse_ref[0] = _unstack_heads(jnp.broadcast_to(mx + jnp.log(den), (2 * TQ, LANES)))

    m = batch * seq
    sp = _fox_specs(batch, seq)
    return pl.pallas_call(
        body, name="fox_fwd", grid=(batch, 2, nq),
        in_specs=[sp["qblk"], sp["whole"], sp["vwhole"], sp["fcol"], sp["frow"]],
        out_specs=[sp["rows"], sp["stat"]],
        out_shape=[jax.ShapeDtypeStruct((m, ATT_W), F32), jax.ShapeDtypeStruct((2, m, LANES), F32)],
        compiler_params=_params(3),
    )(fq, fk, proj16, fcol, frow)


def _fox_bwd(dy, y, lse, fq, fk, proj16, fcol, frow, batch, seq):
    nq = seq // TQ
    scale = HEAD_DIM ** -0.5

    def body(dy_ref, y_ref, lse_ref, q_ref, k_ref, v_ref, fc_ref, fr_ref, dq_ref, dk_ref, dv_ref, dfr_ref, dfc_ref):
        qi = pl.program_id(2)

        @pl.when(qi == 0)
        def _():
            dk_ref[...] = jnp.zeros_like(dk_ref)
            dv_ref[...] = jnp.zeros_like(dv_ref)
            dfr_ref[...] = jnp.zeros_like(dfr_ref)

        lo = _head_masks()
        lane = lax.broadcasted_iota(jnp.int32, (1, LANES), 1)
        dy2, lse2, fcv = dy_ref[...], lse_ref[0], fc_ref[0, 0]
        qs, dys = _stack_heads(q_ref[0]), _stack_heads(dy2.astype(BF16))
        dyy = dy2 * y_ref[...]
        delta = _stack_cols(jnp.sum(jnp.where(lo, dyy, 0.0), axis=1, keepdims=True),
                            jnp.sum(jnp.where(lo, 0.0, dyy), axis=1, keepdims=True))
        lse_col = _stack_cols(lse2[:, 0:1], lse2[:, HEAD_DIM:HEAD_DIM + 1])
        fq_col = _stack_cols(fcv[:, 0:1], fcv[:, 1:2])
        is_a = lax.broadcasted_iota(jnp.int32, (2 * TQ, 1), 0) < TQ

        def block(kb, carry, masked):
            dq, rs = carry
            ks = pl.multiple_of(kb * BLK, BLK)
            k2, v2 = k_ref[0, pl.ds(ks, BLK), :], v_ref[0, pl.ds(ks, BLK), :]
            p = jnp.exp(_fox_logits(qs, k2, fq_col, fr_ref, ks, is_a, scale) - lse_col)
            if masked:
                p = jnp.where(_causal(qi, kb, False), p, 0.0)
            ds = p * (_dot(dys, v2, NT) - delta)
            dsb = (ds * scale).astype(BF16)
            dk_ref[0, pl.ds(ks, BLK), :] += _dot(dsb, qs, TN)
            dv_ref[0, pl.ds(ks, BLK), :] += _dot(p.astype(BF16), dys, TN)
            dfr_ref[0, 0, 0:1, pl.ds(ks, BLK)] -= jnp.sum(ds[:TQ], axis=0, keepdims=True)
            dfr_ref[0, 0, 1:2, pl.ds(ks, BLK)] -= jnp.sum(ds[TQ:], axis=0, keepdims=True)
            return dq + _dot(dsb, k2), rs + jnp.sum(ds, axis=1, keepdims=True)

        first = qi * KB_PER_Q
        carry = _key_loop(qi, _one_by_one(block),(jnp.zeros((2 * TQ, LANES), F32), jnp.zeros((2 * TQ, 1), F32)))
        for n in range(KB_PER_Q):
            carry = block(first + n, carry, True)
        dq, rs = carry
        dq_ref[...] = _unstack_heads(dq)
        dfc_ref[0] = jnp.where(lane == 0, rs[:TQ], jnp.where(lane == 1, rs[TQ:], 0.0))

    m = batch * seq
    sp = _fox_specs(batch, seq)
    return pl.pallas_call(
        body, name="fox_bwd", grid=(batch, 2, nq),
        in_specs=[sp["rows"], sp["rows"], sp["stat"], sp["qblk"], sp["whole"], sp["vwhole"], sp["fcol"], sp["frow"]],
        out_specs=[sp["rows"], sp["whole"], sp["whole"],
                   pl.BlockSpec((1, 1, SUBLANES, seq), lambda b, p, q: (b, p, 0, 0)), sp["stat"]],
        out_shape=[jax.ShapeDtypeStruct((m, ATT_W), F32), jax.ShapeDtypeStruct((batch, seq, ATT_W), F32),
                   jax.ShapeDtypeStruct((batch, seq, ATT_W), F32),
                   jax.ShapeDtypeStruct((batch, 2, SUBLANES, seq), F32), jax.ShapeDtypeStruct((2, m, LANES), F32)],
        compiler_params=_params(3),
    )(dy, y, lse, fq, fk, proj16, fcol, frow)


def _fox_post_bwd(dfq, dfk, dfc, proj32, gq, gk, bf, group_mean, batch, seq):
    m = proj32.shape[0]
    ts = _tok_tile(seq)
    nt = seq // ts
    tile = lambda w, col: pl.BlockSpec((ts, w), lambda b, i: (b * nt + (nt - 1 - i), col))

    def body(dfq_ref, dfk_ref, dfc_ref, q_ref, k_ref, f_ref, gq_ref, gk_ref, bf_ref, gm_ref,
             dq_ref, dk_ref, df_ref, gs_ref, bs_ref, carry):
        i = pl.program_id(1)

        @pl.when(i == 0)
        def _():
            carry[...] = jnp.zeros_like(carry)

        gm = gm_ref[...]
        rows = []
        for src, g_ref, d_ref, dst in ((q_ref, gq_ref, dfq_ref, dq_ref), (k_ref, gk_ref, dfk_ref, dk_ref)):
            v, dv = src[...], d_ref[...]
            rstd = lax.rsqrt(_dot_split(v * v, gm) + EPS)
            vhat = v * rstd
            rows.append(_colsum(dv * vhat))
            dvh = dv * g_ref[...]
            dst[...] = (rstd * (dvh - vhat * _dot_split(dvh * vhat, gm))).astype(BF16)
        gs_ref[0] = _rows_to_block(rows, ATT_W)

        dfc_v = dfc_ref[...]
        r = lax.broadcasted_iota(jnp.int32, (ts, ts), 0)
        c = lax.broadcasted_iota(jnp.int32, (ts, ts), 1)
        tri = (r <= c).astype(BF16)
        hi, mid, low = _split3(dfc_v)
        dlf = _dot(tri, hi) + _dot(tri, mid) + _dot(tri, low) + carry[...]
        carry[...] = dlf[0:1, :]
        z = f_ref[...] + bf_ref[...]
        dz = dlf * _sigmoid(-z)
        df_ref[...] = dz.astype(BF16)
        bs_ref[0] = _rows_to_block([_colsum(dz)], LANES)

    full = lambda shape: pl.BlockSpec(shape, lambda b, i: (0,) * len(shape))
    part = lambda w: pl.BlockSpec((1, SUBLANES, w), lambda b, i: (b * nt + (nt - 1 - i), 0, 0))
    return pl.pallas_call(
        body, name="fox_post_bwd", grid=(batch, nt),
        in_specs=[tile(ATT_W, 0), tile(ATT_W, 0), tile(LANES, 0), tile(ATT_W, 7), tile(ATT_W, 8), tile(LANES, COL_FXF),
                  full((1, ATT_W)), full((1, ATT_W)), full((1, LANES)), full((ATT_W, ATT_W))],
        out_specs=[tile(ATT_W, 0), tile(ATT_W, 0), tile(LANES, 0), part(ATT_W), part(LANES)],
        out_shape=[jax.ShapeDtypeStruct((m, ATT_W), BF16), jax.ShapeDtypeStruct((m, ATT_W), BF16),
                   jax.ShapeDtypeStruct((m, LANES), BF16),
                   jax.ShapeDtypeStruct((batch * nt, SUBLANES, ATT_W), F32),
                   jax.ShapeDtypeStruct((batch * nt, SUBLANES, LANES), F32)],
        scratch_shapes=[pltpu.VMEM((1, LANES), F32)],
        compiler_params=_params(2),
    )(dfq, dfk, dfc, proj32, proj32, proj32, gq, gk, bf, group_mean)


_GROUPS = ((0, LRU_W), (LRU_W, LRU_W + ATT_W), (LRU_W + ATT_W, LRU_W + 2 * ATT_W))


def _outnorm(y_lru, y_sb, y_fox, gmix, seq):
    m = y_lru.shape[0]
    tm = _tok_tile(seq)

    def body(a_ref, b_ref, c_ref, g_ref, o_ref):
        parts = []
        for ref in (a_ref, b_ref, c_ref):
            v = ref[...]
            parts.append(v * lax.rsqrt(jnp.mean(v * v, axis=-1, keepdims=True) + EPS))
        o_ref[...] = (jnp.concatenate(parts, axis=1) * g_ref[...]).astype(BF16)

    t = lambda w: pl.BlockSpec((tm, w), lambda i: (i, 0))
    return pl.pallas_call(
        body, name="outnorm", grid=(m // tm,),
        in_specs=[t(LRU_W), t(ATT_W), t(ATT_W), pl.BlockSpec((1, D_MODEL), lambda i: (0, 0))],
        out_specs=t(D_MODEL), out_shape=jax.ShapeDtypeStruct((m, D_MODEL), BF16), compiler_params=_params(1),
    )(y_lru, y_sb, y_fox, gmix)


def _outnorm_bwd_epilogue(p, e_refs, o_refs):
    gmix = e_refs[3][...]
    dg = []
    for n, (lo, hi) in enumerate(_GROUPS):
        v, dyn = e_refs[n][...], p[:, lo:hi]
        rstd = lax.rsqrt(jnp.mean(v * v, axis=-1, keepdims=True) + EPS)
        vhat = v * rstd
        dg.append(_colsum(dyn * vhat))
        dvh = dyn * gmix[:, lo:hi]
        o_refs[n][...] = rstd * (dvh - vhat * jnp.mean(dvh * vhat, axis=-1, keepdims=True))
    o_refs[3][0] = _rows_to_block([jnp.concatenate(dg, axis=1)], p.shape[1])


def _pair_layouts(fcum, batch, seq):
    f4 = fcum[:, :4].reshape(batch, seq, 2, 2)
    return f4.transpose(0, 2, 1, 3), f4.transpose(0, 2, 3, 1)


def _gate_grad_cols(dfr, dfc, batch, seq):
    keys = dfr[:, :, :2, :].transpose(0, 3, 1, 2).reshape(batch * seq, 4)
    queries = dfc[:, :, :2].transpose(1, 0, 2).reshape(batch * seq, 4)
    return jnp.pad(keys + queries, ((0, 0), (0, LANES - 4)))


def _mixer_fwd(x, h, w, gate, batch, seq):
    m, d = x.shape
    tm = _tok_tile(seq)
    tpb = seq // tm

    def in_epilogue(p, e_refs, o_refs):
        o_refs[0][...] = p
        o_refs[1][...] = p.astype(BF16)

    tn_in = 896
    proj32, proj16 = _mm(h, w["w_in"], mode="nn", tm=tm, tn=tn_in, tk=d, name="mix_in",
                         outs=[((m, N_IN_PAD), F32, (tm, tn_in), lambda i, j: (i, j)),
                               ((m, N_IN_PAD), BF16, (tm, tn_in), lambda i, j: (i, j))],
                         epilogue=in_epilogue)
    y_lru, h_lru = _lru_fwd(proj32, w["conv_w"], w["conv_b"], w["wr"], w["br"], w["wi"], w["bi"], w["lam"], batch, seq)
    p16 = proj16.reshape(batch, seq, N_IN_PAD)
    y_sb, t_sb = _sb_fwd(p16, batch, seq)
    fq, fk, fcum = _fox_pre(proj32, w["gq"], w["gk"], w["bf"], w["group_mean"], batch, seq)
    fcol, frow = _pair_layouts(fcum, batch, seq)
    fq3, fk3 = fq.reshape(batch, seq, ATT_W), fk.reshape(batch, seq, ATT_W)
    y_fox, lse = _fox_fwd(fq3, fk3, p16, fcol, frow, batch, seq)
    ynorm = _outnorm(y_lru, y_sb, y_fox, w["gmix"], seq)

    def out_epilogue(p, e_refs, o_refs):
        x_ref, g_ref = e_refs
        o_refs[0][...] = x_ref[...] + (1.0 + g_ref[0]) * p
        o_refs[1][...] = p.astype(BF16)

    x_out, out = _mm(ynorm, w["w_out"], mode="nn", tm=tm, tn=d, tk=d, name="mix_out",
                     extras=[(x, (tm, d), lambda i, j: (i, 0)), (gate, (1, 1, d), lambda i, j: (i // tpb, 0, 0))],
                     outs=[((m, d), F32, (tm, d), lambda i, j: (i, 0)), ((m, d), BF16, (tm, d), lambda i, j: (i, 0))],
                     epilogue=out_epilogue)
    saved = dict(proj32=proj32, p16=p16, h_lru=h_lru, y_lru=y_lru, y_sb=y_sb, t_sb=t_sb, fq3=fq3, fk3=fk3,
                 fcol=fcol, frow=frow, y_fox=y_fox, lse=lse, ynorm=ynorm, out=out)
    return x_out, saved


def _mixer_bwd(dx_out, x, h, s, w, gn, scale, gate, batch, seq, on_grads):
    m, d = x.shape
    tm = _tok_tile(seq)
    tpb = seq // tm
    dout, dgate_parts = _residual_bwd(dx_out, s["out"], gate, 1.0, seq, "mix_res_bwd")
    (dw_out,) = _mm(s["ynorm"], dout, mode="tn", tm=d, tn=d, tk=tm, name="mix_dwout",
                    outs=[((d, d), BF16, (d, d), lambda i, j: (i, j))], epilogue=_store_epilogue([BF16]))
    dy_lru, dy_sb, dy_fox, gmix_parts = _mm(
        dout, w["w_out"], mode="nt", tm=tm, tn=d, tk=d, name="mix_out_dx",
        extras=[(s["y_lru"], (tm, LRU_W), lambda i, j: (i, 0)), (s["y_sb"], (tm, ATT_W), lambda i, j: (i, 0)),
                (s["y_fox"], (tm, ATT_W), lambda i, j: (i, 0)), (w["gmix"], (1, d), lambda i, j: (0, 0))],
        outs=[((m, LRU_W), F32, (tm, LRU_W), lambda i, j: (i, 0)), ((m, ATT_W), F32, (tm, ATT_W), lambda i, j: (i, 0)),
              ((m, ATT_W), F32, (tm, ATT_W), lambda i, j: (i, 0)),
              ((m // tm, SUBLANES, d), F32, (1, SUBLANES, d), lambda i, j: (i, 0, 0))],
        epilogue=_outnorm_bwd_epilogue)

    dsq, dsk, dsv = _sb_bwd(dy_sb, s["t_sb"], s["p16"], batch, seq)
    dfq, dfk, dfv, dfr, dfc = _fox_bwd(dy_fox, s["y_fox"], s["lse"], s["fq3"], s["fk3"], s["p16"], s["fcol"],
                                       s["frow"], batch, seq)
    dfc_cols = _gate_grad_cols(dfr, dfc, batch, seq)
    dxq, dxk, dxf, gqk_parts, bf_parts = _fox_post_bwd(dfq, dfk.reshape(m, ATT_W), dfc_cols, s["proj32"],
                                                       w["gq"], w["gk"], w["bf"], w["group_mean"], batch, seq)
    dlx, dlg, dwr, dwi, lru_sums = _lru_bwd(dy_lru, s["proj32"], s["h_lru"], w["conv_w"], w["conv_b"], w["wr"],
                                            w["br"], w["wi"], w["bi"], w["lam"], batch, seq)
    dproj = jnp.concatenate([dlx, dlg, dsq.astype(BF16), dsk.reshape(m, ATT_W).astype(BF16),
                             dsv.reshape(m, ATT_W).astype(BF16), dxq, dxk, dfv.reshape(m, ATT_W).astype(BF16), dxf],
                            axis=1)
    tn_in = 896
    (dw_in,) = _mm(h, dproj, mode="tn", tm=d, tn=tn_in, tk=tm, name="mix_dwin",
                   outs=[((d, N_IN_PAD), BF16, (d, tn_in), lambda i, j: (i, j))], epilogue=_store_epilogue([BF16]))
    scale = scale + on_grads(dw_in, dw_out)[0, 0]
    dx, nm_parts = _mm(dproj, w["w_in"], mode="nt", tm=tm, tn=d, tk=tn_in, name="mix_in_dx",
                       extras=[(x, (tm, d), lambda i, j: (i, 0)), (dx_out, (tm, d), lambda i, j: (i, 0)),
                               (gn, (1, d), lambda i, j: (0, 0)), (scale, (1, 1, d), lambda i, j: (i // tpb, 0, 0))],
                       outs=[((m, d), F32, (tm, d), lambda i, j: (i, 0)),
                             ((m // tm, SUBLANES, d), F32, (1, SUBLANES, d), lambda i, j: (i, 0, 0))],
                       epilogue=_normmod_bwd_epilogue)
    grads = dict(dwr=dwr, dwi=dwi, lru_sums=lru_sums, gmix_parts=gmix_parts,
                 gqk_parts=gqk_parts, bf_parts=bf_parts)
    return dx, grads, nm_parts, dgate_parts


def _block_diag(w):
    nb = w.shape[0]
    eye = jnp.eye(nb, dtype=w.dtype)
    return (eye[:, None, :, None] * w[:, :, None, :]).reshape(nb * HEAD_DIM, nb * HEAD_DIM)


def _block_diag_grad(g):
    nb = LRU_W // HEAD_DIM
    g4 = g.reshape(nb, HEAD_DIM, nb, HEAD_DIM)
    return jnp.stack([g4[n, :, n, :] for n in range(nb)])


def _per_batch(parts, batch, row):
    r = parts[:, row, :]
    return r.reshape(batch, -1, r.shape[-1]).sum(axis=1)


def _local_step(x3, target3, mod, wts, big_weights):
    batch, seq, d = x3.shape
    assert seq % TQ == 0, seq
    m = batch * seq
    n_layers = mod.shape[0]
    x = x3.reshape(m, d)
    group_mean = _block_diag(jnp.full((ATT_W // HEAD_DIM, HEAD_DIM, HEAD_DIM), 1.0 / HEAD_DIM, BF16))
    vec = lambda l, j, t: mod[l, :, j, t][:, None, :]

    layers, saved = [], []
    for l in range(n_layers):
        gq = jnp.tile(wts["g_qk"][l, 0], ATT_W // HEAD_DIM)[None, :]
        gk = jnp.tile(wts["g_qk"][l, 1], ATT_W // HEAD_DIM)[None, :]
        bf = jnp.pad(wts["b_fgate"][l], (0, LANES - 4))[None, :]
        lw = dict(conv_w=wts["conv_w"][l],
                  conv_b=wts["conv_b"][l][None, :], wr=_block_diag(wts["w_rgate"][l]).astype(BF16),
                  br=wts["b_rgate"][l][None, :], wi=_block_diag(wts["w_igate"][l]).astype(BF16),
                  bi=wts["b_igate"][l][None, :], lam=wts["lru_lambda"][l][None, :], gq=gq, gk=gk, bf=bf,
                  group_mean=group_mean, gmix=wts["g_mix_out"][l][None, :])
        layers.append(lw)
        gn = lambda j: wts["g_norm"][l, j][None, :]
        sv = dict(x0=x)
        sv["h0"] = _normmod(x, gn(0), vec(l, 0, 1), vec(l, 0, 0), seq, f"normmod_{l}_0")
        sv["w_ffn0"] = big_weights(l, "ffn0", sv["h0"])
        x, sv["ffn0"] = _ffn_fwd(x, sv["h0"], *sv["w_ffn0"], vec(l, 0, 2), seq, f"{l}_0")
        sv["x1"] = x
        sv["h1"] = _normmod(x, gn(1), vec(l, 1, 1), vec(l, 1, 0), seq, f"normmod_{l}_1")
        lw["w_in"], lw["w_out"] = big_weights(l, "mix", sv["h1"])
        x, sv["mix"] = _mixer_fwd(x, sv["h1"], lw, vec(l, 1, 2), batch, seq)
        sv["x2"] = x
        sv["h2"] = _normmod(x, gn(2), vec(l, 2, 1), vec(l, 2, 0), seq, f"normmod_{l}_2")
        sv["w_ffn1"] = big_weights(l, "ffn1", sv["h2"])
        x, sv["ffn1"] = _ffn_fwd(x, sv["h2"], *sv["w_ffn1"], vec(l, 2, 2), seq, f"{l}_1")
        saved.append(sv)

    dx, loss_parts = _loss_head(x, target3.reshape(m, d), seq)
    loss = jnp.sum(loss_parts[:, 0, 0])

    handles = {}

    def scatter(key, shapes):
        def on_grads(*grads):
            ops = [(g.reshape(shape), 0, "scatter") for g, shape in zip(grads, shapes)]
            handles[key], token = _flight_start(ops, f"grads_{key[0]}_{key[1]}_start")
            return token
        return on_grads

    ffn_shapes = ((2 * N_FF_SHARD, d, FF_SHARD), (N_DEV, D_FF // N_DEV, d))
    mix_shapes = ((N_DEV, d // N_DEV, N_IN_PAD), (N_DEV, d // N_DEV, d))
    small = {k: [] for k in ("dmod", "g_norm", "b_fgate", "conv_w", "conv_b", "w_rgate", "b_rgate", "w_igate",
                             "b_igate", "lru_lambda", "g_qk", "g_mix_out")}
    for l in reversed(range(n_layers)):
        sv, lw = saved[l], layers[l]
        gn = lambda j: wts["g_norm"][l, j][None, :]
        dx, nm2, dg2 = _ffn_bwd(dx, sv["x2"], sv["h2"], sv["ffn1"], *sv["w_ffn1"], gn(2), vec(l, 2, 1), vec(l, 2, 2),
                                seq, f"{l}_1", scatter((l, "ffn1"), ffn_shapes))
        dx, mg, nm1, dg1 = _mixer_bwd(dx, sv["x1"], sv["h1"], sv["mix"], lw, gn(1), vec(l, 1, 1), vec(l, 1, 2),
                                      batch, seq, scatter((l, "mix"), mix_shapes))
        dx, nm0, dg0 = _ffn_bwd(dx, sv["x0"], sv["h0"], sv["ffn0"], *sv["w_ffn0"], gn(0), vec(l, 0, 1), vec(l, 0, 2),
                                seq, f"{l}_0", scatter((l, "ffn0"), ffn_shapes))
        dmod_l, gnorm_l = [], []
        for nm, dg in ((nm0, dg0), (nm1, dg1), (nm2, dg2)):
            dmod_l.append(jnp.stack([_per_batch(nm, batch, 0), _per_batch(nm, batch, 1), _per_batch(dg, batch, 0)],
                                    axis=1))
            gnorm_l.append(jnp.sum(nm[:, 2, :], axis=0))
        small["dmod"].insert(0, jnp.stack(dmod_l, axis=1))
        small["g_norm"].insert(0, jnp.stack(gnorm_l))
        ls = mg["lru_sums"]
        small["b_rgate"].insert(0, ls[0])
        small["b_igate"].insert(0, ls[1])
        small["lru_lambda"].insert(0, ls[2] * (-_sigmoid(-wts["lru_lambda"][l])))
        small["conv_b"].insert(0, ls[3])
        small["conv_w"].insert(0, ls[4:8])
        small["w_rgate"].insert(0, _block_diag_grad(mg["dwr"]))
        small["w_igate"].insert(0, _block_diag_grad(mg["dwi"]))
        small["g_mix_out"].insert(0, jnp.sum(mg["gmix_parts"][:, 0, :], axis=0))
        gqk = jnp.sum(mg["gqk_parts"][:, :2, :], axis=0).reshape(2, ATT_W // HEAD_DIM, HEAD_DIM).sum(axis=1)
        small["g_qk"].insert(0, gqk)
        small["b_fgate"].insert(0, jnp.sum(mg["bf_parts"][:, 0, :4], axis=0))
    small = {k: jnp.stack(v) for k, v in small.items()}
    return loss, dx.reshape(batch, seq, d), handles, small


def _row_tile(rows, row_bytes):
    for t in (512, 256, 128, 64, 32, 16):
        if rows % t == 0 and t * row_bytes <= 4 * 1024 * 1024:
            return t
    return rows


def _adamw(parts, w, m, v, name):
    groups, n_parts, rows, cols = parts.shape
    tr = _row_tile(rows, cols * (n_parts * parts.dtype.itemsize + 7 * 4))
    c1 = 1.0 - ADAM_B1 ** ADAM_STEP
    c2 = 1.0 - ADAM_B2 ** ADAM_STEP

    def body(p_ref, w_ref, m_ref, v_ref, g_out, d_out, m_out, v_out):
        g = p_ref[0].astype(F32)
        for n in range(1, n_parts):
            g = g + p_ref[n].astype(F32)
        m_new = ADAM_B1 * m_ref[...] + (1.0 - ADAM_B1) * g
        v_new = ADAM_B2 * v_ref[...] + (1.0 - ADAM_B2) * (g * g)
        g_out[...] = g
        d_out[...] = -ADAM_LR * ((m_new / c1) / (jnp.sqrt(v_new / c2) + ADAM_EPS) + ADAM_WD * w_ref[...])
        m_out[...] = m_new
        v_out[...] = v_new

    tile = pl.BlockSpec((None, tr, cols), lambda g, i: (g, i, 0))
    return pl.pallas_call(
        body, name=name, grid=(groups, rows // tr),
        in_specs=[pl.BlockSpec((None, n_parts, tr, cols), lambda g, i: (g, 0, i, 0)), tile, tile, tile],
        out_specs=[tile] * 4, out_shape=[jax.ShapeDtypeStruct((groups, rows, cols), F32)] * 4,
        compiler_params=_params(2),
    )(parts, w, m, v)


def _sum_parts(parts):
    n_parts, rows, cols = parts.shape

    def body(p_ref, o_ref):
        acc = p_ref[0]
        for n in range(1, n_parts):
            acc = acc + p_ref[n]
        o_ref[...] = acc

    return pl.pallas_call(body, name="sum_small", out_shape=jax.ShapeDtypeStruct((rows, cols), F32),
                          compiler_params=pltpu.CompilerParams(vmem_limit_bytes=VMEM_LIMIT_BYTES))(parts)


def _flatten(arrays, multiple):
    flat = jnp.concatenate([a.reshape(-1).astype(F32) for a in arrays])
    pad = (-flat.shape[0]) % multiple
    return jnp.pad(flat, (0, pad)).reshape(-1, LANES)


def _unflatten(flat2d, shapes):
    flat, out, off = flat2d.reshape(-1), [], 0
    for s in shapes:
        n = math.prod(s)
        out.append(flat[off:off + n].reshape(s))
        off += n
    return out


SMALL_NAMES = ("b_ada", "g_norm", "b_fgate", "conv_w", "conv_b", "w_rgate", "b_rgate", "w_igate", "b_igate",
               "lru_lambda", "g_qk", "g_mix_out")
WEIGHT_NAMES = ("w_ada", "b_ada", "g_norm", "w_ffn_up", "w_ffn_down", "w_in", "b_fgate", "conv_w", "conv_b",
                "w_rgate", "b_rgate", "w_igate", "b_igate", "lru_lambda", "g_qk", "g_mix_out", "w_out")


def kernel(x, c, w_ada, b_ada, g_norm, w_ffn_up, w_ffn_down, w_in, b_fgate, conv_w, conv_b, w_rgate, b_rgate, w_igate, b_igate, lru_lambda, g_qk, g_mix_out, w_out, loss_target, m_w_ada, m_b_ada, m_g_norm, m_w_ffn_up, m_w_ffn_down, m_w_in, m_b_fgate, m_conv_w, m_conv_b, m_w_rgate, m_b_rgate, m_w_igate, m_b_igate, m_lru_lambda, m_g_qk, m_g_mix_out, m_w_out, v_w_ada, v_b_ada, v_g_norm, v_w_ffn_up, v_w_ffn_down, v_w_in, v_b_fgate, v_conv_w, v_conv_b, v_w_rgate, v_b_rgate, v_w_igate, v_b_igate, v_lru_lambda, v_g_qk, v_g_mix_out, v_w_out):
    batch, seq, d = x.shape
    n_layers = w_ada.shape[0]
    me = 4 * lax.axis_index("x") + 2 * lax.axis_index("y") + lax.axis_index("c")
    weights = dict(w_ada=w_ada, b_ada=b_ada, g_norm=g_norm, w_ffn_up=w_ffn_up, w_ffn_down=w_ffn_down, w_in=w_in,
                   b_fgate=b_fgate, conv_w=conv_w, conv_b=conv_b, w_rgate=w_rgate, b_rgate=b_rgate, w_igate=w_igate,
                   b_igate=b_igate, lru_lambda=lru_lambda, g_qk=g_qk, g_mix_out=g_mix_out, w_out=w_out)
    moments_m = dict(w_ada=m_w_ada, b_ada=m_b_ada, g_norm=m_g_norm, w_ffn_up=m_w_ffn_up, w_ffn_down=m_w_ffn_down,
                     w_in=m_w_in, b_fgate=m_b_fgate, conv_w=m_conv_w, conv_b=m_conv_b, w_rgate=m_w_rgate,
                     b_rgate=m_b_rgate, w_igate=m_w_igate, b_igate=m_b_igate, lru_lambda=m_lru_lambda, g_qk=m_g_qk,
                     g_mix_out=m_g_mix_out, w_out=m_w_out)
    moments_v = dict(w_ada=v_w_ada, b_ada=v_b_ada, g_norm=v_g_norm, w_ffn_up=v_w_ffn_up, w_ffn_down=v_w_ffn_down,
                     w_in=v_w_in, b_fgate=v_b_fgate, conv_w=v_conv_w, conv_b=v_conv_b, w_rgate=v_w_rgate,
                     b_rgate=v_b_rgate, w_igate=v_w_igate, b_igate=v_b_igate, lru_lambda=v_lru_lambda, g_qk=v_g_qk,
                     g_mix_out=v_g_mix_out, w_out=v_w_out)

    c_all, gn_all, cw_all = _exchange([(c, 0), (g_norm, 0), (conv_w, 0)], [], "gather_small_weights")
    c_all = c_all.reshape(N_DEV * batch, d)
    n_ada = w_ada.shape[-1]
    g_norm_full = gn_all.transpose(1, 2, 0, 3).reshape(n_layers, 3, d)
    conv_w_full = cw_all.transpose(1, 2, 0, 3).reshape(n_layers, 4, LRU_W)

    b_ada_loc = lax.dynamic_slice_in_dim(b_ada, me * n_ada, n_ada, axis=1)
    silu = lambda t: t * _sigmoid(t)

    def bias_epilogue(p, e_refs, o_refs):
        o_refs[0][...] = p + e_refs[0][...]

    mod_loc = []
    for l in range(n_layers):
        (ml,) = _mm(c_all, w_ada, mode="nn", tm=c_all.shape[0], tn=n_ada, tk=d, b_lead=(l,), a_pre=silu,
                    name=f"ada_{l}", extras=[(b_ada_loc[l][None, :], (1, n_ada), lambda i, j: (0, 0))],
                    outs=[((c_all.shape[0], n_ada), F32, (c_all.shape[0], n_ada), lambda i, j: (0, 0))],
                    epilogue=bias_epilogue)
        mod_loc.append(ml)
    (mod_all,) = _exchange([(jnp.stack(mod_loc), 0)], [], "gather_mod")
    mod_all = mod_all.transpose(1, 2, 0, 3).reshape(n_layers, N_DEV * batch, 9 * d)

    w_in_pad = jnp.pad(w_in, ((0, 0), (0, 0), (0, N_IN_PAD - N_IN))).astype(BF16)
    big16 = (w_ffn_up.astype(BF16), w_ffn_down.astype(BF16), w_in_pad, w_out.astype(BF16))
    (up16, down16, in16, out16), mod_all = lax.optimization_barrier((big16, mod_all))
    ffn_ops = lambda l, f: [(up16[l, f], 0, "gather"), (down16[l, f], 0, "gather")]
    mix_ops = lambda l: [(in16[l], 0, "gather"), (out16[l], 0, "gather")]
    groups = {(0, "ffn0"): ffn_ops(0, 0), (0, "mix"): mix_ops(0), (0, "ffn1"): ffn_ops(0, 1)}
    for l in range(1, n_layers):
        groups[(l, "all")] = ffn_ops(l, 0) + mix_ops(l) + ffn_ops(l, 1)
    flights, order = {}, jnp.zeros((), F32)
    for key, ops in groups.items():
        flights[key], token = _flight_start(ops, f"weights_{key[0]}_{key[1]}_start")
        order = order + token[0, 0]
    landed_weights = {}

    def big_weights(l, part, after):
        key = (l, part) if (l, part) in flights else (l, "all")
        if key not in landed_weights:
            landed_weights[key] = _flight_wait(flights[key], after, f"weights_{key[0]}_{key[1]}_wait")
        got = landed_weights[key]
        if key[1] == "all":
            got = got[{"ffn0": 0, "mix": 2, "ffn1": 4}[part]:][:2]
        if part == "mix":
            return got[0].reshape(d, N_IN_PAD), got[1].reshape(d, d)
        return got[0], got[1].reshape(D_FF, d)

    mod_me = lax.dynamic_slice_in_dim(mod_all + order, me * batch, batch, axis=1).reshape(n_layers, batch, 3, 3, d)

    wts = dict(g_norm=g_norm_full, conv_w=conv_w_full, conv_b=conv_b, w_rgate=w_rgate, b_rgate=b_rgate, w_igate=w_igate, b_igate=b_igate,
               lru_lambda=lru_lambda, g_qk=g_qk, g_mix_out=g_mix_out, b_fgate=b_fgate)
    loss_part, grad_x, handles, small = _local_step(x, loss_target, mod_me, wts, big_weights)

    dmod_me = small.pop("dmod").reshape(n_layers, batch, 9 * d)
    small["b_ada"] = jnp.sum(dmod_me, axis=1)
    small_shapes = [(1,)] + [weights[k].shape if k not in ("g_norm", "conv_w") else small[k].shape for k in SMALL_NAMES]
    small_flat = _flatten([loss_part.reshape(1)] + [small[k] for k in SMALL_NAMES], 16 * LANES)
    dmod_all, small_all = _exchange([(dmod_me, 0), (small_flat, 0)], [], "gather_small")
    landed = {key: _flight_wait(h, small_all, f"grads_{key[0]}_{key[1]}_wait") for key, h in handles.items()}
    layer_range = range(n_layers)
    p_up = jnp.stack([jnp.stack([landed[(l, "ffn0")][0], landed[(l, "ffn1")][0]]) for l in layer_range])
    p_down = jnp.stack([jnp.stack([landed[(l, "ffn0")][1], landed[(l, "ffn1")][1]]) for l in layer_range])
    p_in = jnp.stack([landed[(l, "mix")][0] for l in layer_range])
    p_out = jnp.stack([landed[(l, "mix")][1] for l in layer_range])
    small_sum = _unflatten(_sum_parts(small_all), small_shapes)
    loss = small_sum[0].reshape(())
    small_grads = dict(zip(SMALL_NAMES, small_sum[1:]))
    small_grads["g_norm"] = lax.dynamic_slice_in_dim(small_grads["g_norm"], me * g_norm.shape[-1], g_norm.shape[-1], 2)
    small_grads["conv_w"] = lax.dynamic_slice_in_dim(small_grads["conv_w"], me * conv_w.shape[-1], conv_w.shape[-1], 2)

    dmod_all = dmod_all.transpose(1, 0, 2, 3).reshape(n_layers, N_DEV * batch, 9 * d)
    dmod_loc = lax.dynamic_slice_in_dim(dmod_all, me * n_ada, n_ada, axis=2)
    g_ada = []
    for l in range(n_layers):
        (gl,) = _mm(c_all, dmod_loc[l], mode="tn", tm=d, tn=n_ada, tk=c_all.shape[0], a_pre=silu, name=f"dw_ada_{l}",
                    outs=[((d, n_ada), F32, (d, n_ada), lambda i, j: (0, 0))], epilogue=_store_epilogue([F32]))
        g_ada.append(gl)
    g_ada = jnp.stack(g_ada)

    results = {}

    def update(name, parts):
        shape = weights[name].shape
        as3d = lambda t: t.reshape((-1,) + shape[-2:])
        outs = _adamw(parts.reshape((-1,) + parts.shape[-3:]), as3d(weights[name]), as3d(moments_m[name]),
                      as3d(moments_v[name]), f"adamw_{name}")
        results[name] = [o.reshape(shape) for o in outs]

    update("w_ada", g_ada[:, None])
    update("w_ffn_up", p_up)
    update("w_ffn_down", p_down)
    update("w_in", p_in[..., :N_IN])
    update("w_out", p_out)
    sm_shapes = [weights[k].shape for k in SMALL_NAMES]
    flat = lambda src: _flatten([src[k] for k in SMALL_NAMES], 16 * LANES)
    sm_out = _adamw(flat(small_grads)[None, None], flat(weights)[None], flat(moments_m)[None], flat(moments_v)[None],
                    "adamw_small")
    for k, vals in zip(SMALL_NAMES, zip(*[_unflatten(o, sm_shapes) for o in sm_out])):
        results[k] = list(vals)

    outs = [loss, grad_x]
    for n in range(4):
        outs += [results[k][n] for k in WEIGHT_NAMES]
    return tuple(outs)
```

```python
import functools
import math

import jax
import jax.numpy as jnp
from jax import lax
from jax.experimental import pallas as pl
from jax.experimental.pallas import tpu as pltpu

F32 = jnp.float32
BF16 = jnp.bfloat16

N_DEV = 8
D_MODEL = 1024
D_FF = 2816
FF_SHARD = 2 * D_FF // N_DEV
N_FF_SHARD = D_FF // FF_SHARD
HEAD_DIM = 64
LRU_W = 512
ATT_W = 256
N_IN = 2564
N_IN_PAD = 2688
LANES = 128
SUBLANES = 8
BLK = 256
TQ = 512
KB_PER_Q = TQ // BLK
EPS = 1e-6
LRU_C = 8.0
NEG_BIG = -1e30
VMEM_LIMIT_BYTES = 48 * 1024 * 1024

ADAM_LR, ADAM_B1, ADAM_B2, ADAM_EPS, ADAM_WD, ADAM_STEP = 0.001, 0.9, 0.999, 1e-08, 0.01, 10

COL_SBQ, COL_SBK, COL_SBV = 8, 10, 12
COL_FXV, COL_FXF = 18, 20

NN = (((1,), (0,)), ((), ()))
NT = (((1,), (1,)), ((), ()))
TN = (((0,), (0,)), ((), ()))


def _params(n_axes):
    return pltpu.CompilerParams(dimension_semantics=("arbitrary",) * n_axes, vmem_limit_bytes=VMEM_LIMIT_BYTES)


def _tok_tile(seq):
    for t in (512, 256, 128):
        if seq % t == 0:
            return t
    raise ValueError(f"sequence length {seq} is not a multiple of 128")


def _dot(a, b, dims=NN):
    return lax.dot_general(a, b, dims, preferred_element_type=F32)


def _sigmoid(x):
    return 1.0 / (1.0 + jnp.exp(-x))


def _softplus(x):
    return jnp.maximum(x, 0.0) + jnp.log(1.0 + jnp.exp(-jnp.abs(x)))


def _gelu_parts(x):
    k0, k1 = math.sqrt(2.0 / math.pi), 0.044715
    t = jnp.tanh(k0 * (x + k1 * x * x * x))
    gelu = 0.5 * x * (1.0 + t)
    dgelu = 0.5 * (1.0 + t) + 0.5 * x * (1.0 - t * t) * k0 * (1.0 + 3.0 * k1 * x * x)
    return gelu, dgelu


def _neg_expm1(x):
    series = -x * (1.0 + x * (0.5 + x * (1.0 / 6.0 + x * (1.0 / 24.0 + x * (1.0 / 120.0 + x * (1.0 / 720.0))))))
    return jnp.where(x > -0.25, series, 1.0 - jnp.exp(x))


def _split2(x):
    hi = x.astype(BF16)
    lo = (x - hi.astype(F32)).astype(BF16)
    return hi, lo


def _split3(x):
    hi = x.astype(BF16)
    r = x - hi.astype(F32)
    mid = r.astype(BF16)
    lo = (r - mid.astype(F32)).astype(BF16)
    return hi, mid, lo


def _rows_to_block(rows, width):
    r = lax.broadcasted_iota(jnp.int32, (SUBLANES, width), 0)
    out = jnp.zeros((SUBLANES, width), F32)
    for n, v in enumerate(rows):
        out = jnp.where(r == n, jnp.broadcast_to(v, (SUBLANES, width)), out)
    return out


def _colsum(x):
    return jnp.sum(x, axis=0, keepdims=True)


def _exchange(gathers, scatters, name, two_level=False):
    assert not (two_level and scatters)
    n_g = len(gathers)
    ops = [a for a, _ in gathers] + [a for a, _ in scatters]
    n = len(ops)
    out_shape = [jax.ShapeDtypeStruct(a.shape[:nl] + (N_DEV,) + a.shape[nl:], a.dtype) for a, nl in gathers]
    out_shape += [jax.ShapeDtypeStruct(a.shape, a.dtype) for a, _ in scatters]
    items = []
    for k, (a, nl) in enumerate(list(gathers) + list(scatters)):
        for flat in range(math.prod(a.shape[:nl])):
            idx, rem = [], flat
            for dim in reversed(a.shape[:nl]):
                idx.insert(0, rem % dim)
                rem //= dim
            items.append((k, tuple(idx)))
    n_items = len(items)

    def body(*refs):
        ins, outs = refs[:n], refs[n:2 * n]
        send_sems, recv_sems, local_sems = refs[2 * n:]
        x, y, c = lax.axis_index("x"), lax.axis_index("y"), lax.axis_index("c")
        me = 4 * x + 2 * y + c

        def at(ref, idx):
            return ref.at[idx] if idx else ref

        def src(it, peer):
            k, idx = items[it]
            return at(ins[k], idx) if k < n_g else at(ins[k], idx + (peer,))

        def slot(it, s):
            k, idx = items[it]
            return at(outs[k], idx + (s,))

        def remote(it, rel, source, s, to):
            return pltpu.make_async_remote_copy(
                src_ref=source, dst_ref=slot(it, s), send_sem=send_sems.at[it, rel], recv_sem=recv_sems.at[it, rel],
                device_id=to, device_id_type=pl.DeviceIdType.MESH)

        local = [pltpu.make_async_copy(src(it, me), slot(it, me), local_sems.at[it]) for it in range(n_items)]
        for cp in local:
            cp.start()

        if not two_level:
            started = []
            for r in range(1, N_DEV):
                px = 1 - x if (r >> 2) & 1 else x
                py = 1 - y if (r >> 1) & 1 else y
                pc = 1 - c if r & 1 else c
                for it in range(n_items):
                    cp = remote(it, r - 1, src(it, 4 * px + 2 * py + pc), me, (px, py, pc))
                    cp.start()
                    started.append(cp)
            for cp in started:
                cp.wait()
        else:
            sibling, chips = (x, y, 1 - c), [(1 - x, y), (x, 1 - y), (1 - x, 1 - y)]
            sib = 4 * x + 2 * y + (1 - c)
            started = []
            for it in range(n_items):
                started.append(remote(it, 0, src(it, me), me, sibling))
                started += [remote(it, 1 + j, src(it, me), me, (cx, cy, c)) for j, (cx, cy) in enumerate(chips)]
            for cp in started:
                cp.start()
            for j, (cx, cy) in enumerate(chips):
                s = 4 * cx + 2 * cy + c
                for it in range(n_items):
                    remote(it, 1 + j, slot(it, s), s, sibling).wait_recv()
                    cp = remote(it, 4 + j, slot(it, s), s, sibling)
                    cp.start()
                    started.append(cp)
            for it in range(n_items):
                remote(it, 0, slot(it, sib), sib, sibling).wait_recv()
                for j, (cx, cy) in enumerate(chips):
                    s = 4 * cx + 2 * cy + (1 - c)
                    remote(it, 4 + j, slot(it, s), s, sibling).wait_recv()
            for cp in started:
                cp.wait_send()
        for cp in local:
            cp.wait()

    hbm = pl.BlockSpec(memory_space=pltpu.HBM)
    return pl.pallas_call(
        body, name=name, out_shape=out_shape,
        in_specs=[hbm] * n, out_specs=[hbm] * n,
        scratch_shapes=[pltpu.SemaphoreType.DMA((n_items, N_DEV - 1)), pltpu.SemaphoreType.DMA((n_items, N_DEV - 1)),
                        pltpu.SemaphoreType.DMA((n_items,))],
    )(*ops)


def _lead_items(ops):
    items = []
    for k, (a, nl) in enumerate(ops):
        for flat in range(math.prod(a.shape[:nl])):
            idx, rem = [], flat
            for dim in reversed(a.shape[:nl]):
                idx.insert(0, rem % dim)
                rem //= dim
            items.append((k, tuple(idx)))
    return items


def _flight_copies(ops, srcs, lands, send_sems, recv_sems):
    x, y, c = lax.axis_index("x"), lax.axis_index("y"), lax.axis_index("c")
    me = 4 * x + 2 * y + c
    copies = []
    for r in range(1, N_DEV):
        px = 1 - x if (r >> 2) & 1 else x
        py = 1 - y if (r >> 1) & 1 else y
        pc = 1 - c if r & 1 else c
        for it, (k, idx) in enumerate(_lead_items([(a, nl) for a, nl, _ in ops])):
            src = srcs[k].at[idx + (4 * px + 2 * py + pc,)] if ops[k][2] == "scatter" else (
                srcs[k].at[idx] if idx else srcs[k])
            copies.append(pltpu.make_async_remote_copy(
                src_ref=src, dst_ref=lands[k].at[idx + (me,)],
                send_sem=send_sems.at[it * (N_DEV - 1) + r - 1], recv_sem=recv_sems.at[it * (N_DEV - 1) + r - 1],
                device_id=(px, py, pc), device_id_type=pl.DeviceIdType.MESH))
    return copies


def _flight_start(ops, name):
    n = len(ops)
    me = 4 * lax.axis_index("x") + 2 * lax.axis_index("y") + lax.axis_index("c")
    srcs, lands = [], []
    for a, nl, kind in ops:
        if kind == "scatter":
            own, shape = lax.dynamic_slice_in_dim(a, me, 1, axis=nl), a.shape
        else:
            own, shape = jnp.expand_dims(a, nl), a.shape[:nl] + (N_DEV,) + a.shape[nl:]
        start = (0,) * nl + (me,) + (0,) * (len(shape) - nl - 1)
        lands.append(pltpu.with_memory_space_constraint(
            lax.dynamic_update_slice(lax.empty(shape, a.dtype), own, start), pltpu.HBM))
        srcs.append(pltpu.with_memory_space_constraint(a, pltpu.HBM))

    def body(*refs):
        for cp in _flight_copies(ops, refs[:n], refs[n:2 * n], refs[2 * n], refs[2 * n + 1]):
            cp.start()
        refs[-1][...] = jnp.zeros_like(refs[-1])

    hbm, sem = pl.BlockSpec(memory_space=pltpu.HBM), pl.BlockSpec(memory_space=pltpu.SEMAPHORE)
    n_items = len(_lead_items([(a, nl) for a, nl, _ in ops]))
    sems = pltpu.SemaphoreType.DMA((n_items * (N_DEV - 1),))
    res = pl.pallas_call(
        body, name=name,
        out_shape=[sems, sems] + [pltpu.HBM(a.shape, a.dtype) for a in srcs + lands]
        + [jax.ShapeDtypeStruct((SUBLANES, LANES), F32)],
        in_specs=[hbm] * (2 * n), out_specs=[sem, sem] + [hbm] * (2 * n) + [pl.BlockSpec(memory_space=pltpu.VMEM)],
        input_output_aliases={i: 2 + i for i in range(2 * n)},
        compiler_params=pltpu.CompilerParams(has_side_effects=pltpu.SideEffectType.DATAFLOW_SIDE_EFFECTING),
    )(*srcs, *lands)
    return (ops, res[0], res[1], res[2:2 + n], res[2 + n:2 + 2 * n]), res[-1]


def _flight_wait(handle, after, name):
    ops, send_sems, recv_sems, srcs, lands = handle
    n = len(ops)

    def body(*refs):
        for cp in _flight_copies(ops, refs[:n], refs[n:2 * n], refs[2 * n], refs[2 * n + 1]):
            cp.wait_send()
            cp.wait_recv()

    hbm, sem = pl.BlockSpec(memory_space=pltpu.HBM), pl.BlockSpec(memory_space=pltpu.SEMAPHORE)
    res = pl.pallas_call(
        body, name=name, out_shape=[pltpu.HBM(a.shape, a.dtype) for a in list(srcs) + list(lands)],
        in_specs=[hbm] * (2 * n) + [sem, sem, pl.BlockSpec(memory_space=pl.ANY)], out_specs=[hbm] * (2 * n),
        input_output_aliases={i: i for i in range(2 * n)},
        compiler_params=pltpu.CompilerParams(has_side_effects=pltpu.SideEffectType.DATAFLOW_SIDE_EFFECTING),
    )(*srcs, *lands, send_sems, recv_sems, after)
    return res[n:]


def _mm(a, b, *, mode, tm, tn, tk, outs, epilogue, name, extras=(), a_lead=(), b_lead=(), a_pre=None,
        a_spec=None, b_spec=None, shape=None, ksub=1):
    if shape is not None:
        mdim, ndim, kdim = shape
    else:
        if mode == "tn":
            kdim, mdim = a.shape[-2:]
        else:
            mdim, kdim = a.shape[-2:]
        ndim = b.shape[-2] if mode == "nt" else b.shape[-1]
    assert mdim % tm == 0 and ndim % tn == 0 and kdim % tk == 0, (name, mdim, ndim, kdim, tm, tn, tk)
    ni, nj, nk = mdim // tm, ndim // tn, kdim // tk
    a_lead, b_lead = tuple(a_lead), tuple(b_lead)
    a_block = (None,) * len(a_lead) + ((tk, tm) if mode == "tn" else (tm, tk))
    b_block = (None,) * len(b_lead) + ((tn, tk) if mode == "nt" else (tk, tn))
    dims = {"nn": NN, "nt": NT, "tn": TN}[mode]
    ne, no = len(extras), len(outs)

    def a_index(i, j, k):
        return a_lead + ((k, i) if mode == "tn" else (i, k))

    def b_index(i, j, k):
        return b_lead + ((j, k) if mode == "nt" else (k, j))

    if a_spec is not None:
        a_block, a_index = a_spec
    if b_spec is not None:
        b_block, b_index = b_spec

    def body(*refs):
        a_ref, b_ref = refs[0], refs[1]
        e_refs, o_refs = refs[2:2 + ne], refs[2 + ne:2 + ne + no]
        if ksub == 1:
            av = a_ref[...] if a_pre is None else a_pre(a_ref[...])
            p = _dot(av.astype(BF16), b_ref[...].astype(BF16), dims)
        else:
            p = _dot(a_ref[0], b_ref[0], dims)
            for s in range(1, ksub):
                p = p + _dot(a_ref[s], b_ref[s], dims)
        if nk == 1:
            epilogue(p, e_refs, o_refs)
        else:
            acc = refs[-1]
            k = pl.program_id(2)

            @pl.when(k == 0)
            def _():
                acc[...] = p

            @pl.when(k > 0)
            def _():
                acc[...] += p

            @pl.when(k == nk - 1)
            def _():
                epilogue(acc[...], e_refs, o_refs)

    in_specs = [pl.BlockSpec(a_block, a_index), pl.BlockSpec(b_block, b_index)]
    in_specs += [pl.BlockSpec(blk, functools.partial(lambda i, j, k, f: f(i, j), f=f)) for _, blk, f in extras]
    out_specs = [pl.BlockSpec(blk, functools.partial(lambda i, j, k, f: f(i, j), f=f)) for _, _, blk, f in outs]
    res = pl.pallas_call(
        body, name=name, grid=(ni, nj, nk), in_specs=in_specs, out_specs=out_specs,
        out_shape=[jax.ShapeDtypeStruct(s, d) for s, d, _, _ in outs],
        scratch_shapes=[pltpu.VMEM((tm, tn), F32)] if nk > 1 else [],
        compiler_params=_params(3),
    )(a, b, *[e[0] for e in extras])
    return res


def _store_epilogue(dtypes):
    def epi(p, e_refs, o_refs):
        for o, dt in zip(o_refs, dtypes):
            o[...] = p.astype(dt)
    return epi


def _normmod(x, gn, scale, shift, seq, name):
    m, d = x.shape
    tm = _tok_tile(seq)
    tpb = seq // tm

    def body(x_ref, gn_ref, sc_ref, sh_ref, h_ref):
        xv = x_ref[...]
        rstd = lax.rsqrt(jnp.mean(xv * xv, axis=-1, keepdims=True) + EPS)
        h_ref[...] = (xv * rstd * gn_ref[...] * (1.0 + sc_ref[0]) + sh_ref[0]).astype(BF16)

    vec = pl.BlockSpec((1, 1, d), lambda i: (i // tpb, 0, 0))
    return pl.pallas_call(
        body, name=name, grid=(m // tm,),
        in_specs=[pl.BlockSpec((tm, d), lambda i: (i, 0)), pl.BlockSpec((1, d), lambda i: (0, 0)), vec, vec],
        out_specs=pl.BlockSpec((tm, d), lambda i: (i, 0)),
        out_shape=jax.ShapeDtypeStruct((m, d), BF16), compiler_params=_params(1),
    )(x, gn, scale, shift)


def _normmod_bwd_epilogue(p, e_refs, o_refs):
    x_ref, dxo_ref, gn_ref, sc_ref = e_refs
    xv = x_ref[...]
    rstd = lax.rsqrt(jnp.mean(xv * xv, axis=-1, keepdims=True) + EPS)
    xhat = xv * rstd
    gn, sc1 = gn_ref[...], 1.0 + sc_ref[0]
    dxhat = p * (gn * sc1)
    dx = rstd * (dxhat - xhat * jnp.mean(dxhat * xhat, axis=-1, keepdims=True))
    o_refs[0][...] = dxo_ref[...] + dx
    t = p * xhat
    o_refs[1][0] = _rows_to_block([_colsum(p), _colsum(t * gn), _colsum(t * sc1)], p.shape[1])


def _residual_bwd(dx, f, gate, fac, seq, name):
    m, d = dx.shape
    tm = _tok_tile(seq)
    tpb = seq // tm

    def body(dx_ref, f_ref, g_ref, df_ref, dg_ref):
        dxv = dx_ref[...]
        df_ref[...] = ((fac * (1.0 + g_ref[0])) * dxv).astype(BF16)
        dg_ref[0] = _rows_to_block([_colsum((fac * dxv) * f_ref[...].astype(F32))], d)

    tile = pl.BlockSpec((tm, d), lambda i: (i, 0))
    return pl.pallas_call(
        body, name=name, grid=(m // tm,),
        in_specs=[tile, tile, pl.BlockSpec((1, 1, d), lambda i: (i // tpb, 0, 0))],
        out_specs=[tile, pl.BlockSpec((1, SUBLANES, d), lambda i: (i, 0, 0))],
        out_shape=[jax.ShapeDtypeStruct((m, d), BF16), jax.ShapeDtypeStruct((m // tm, SUBLANES, d), F32)],
        compiler_params=_params(1),
    )(dx, f, gate)


def _loss_head(y, target, seq):
    m, d = y.shape
    tm = _tok_tile(seq)

    def body(y_ref, t_ref, dy_ref, l_ref):
        err = y_ref[...] - t_ref[...]
        dy_ref[...] = err * (1.0 / d)
        part = 0.5 * jnp.sum(jnp.mean(err * err, axis=-1, keepdims=True), axis=0, keepdims=True)
        l_ref[0] = jnp.broadcast_to(part, (SUBLANES, LANES))

    tile = pl.BlockSpec((tm, d), lambda i: (i, 0))
    return pl.pallas_call(
        body, name="loss_head", grid=(m // tm,), in_specs=[tile, tile],
        out_specs=[tile, pl.BlockSpec((1, SUBLANES, LANES), lambda i: (i, 0, 0))],
        out_shape=[jax.ShapeDtypeStruct((m, d), F32), jax.ShapeDtypeStruct((m // tm, SUBLANES, LANES), F32)],
        compiler_params=_params(1),
    )(y, target)


def _ffn_fwd(x, h, wup, wdown, gate, seq, tag):
    m, d = x.shape
    tm = _tok_tile(seq)
    tpb = seq // tm

    def up_body(h_ref, wg_ref, wu_ref, a_ref, gu_ref):
        hv = h_ref[...]
        g, u = _dot(hv, wg_ref[...]), _dot(hv, wu_ref[...])
        a_ref[...] = (g * _sigmoid(g) * u).astype(BF16)
        gu_ref[0] = g.astype(BF16)
        gu_ref[1] = u.astype(BF16)

    wblk = (None, d, FF_SHARD)
    a, gu = pl.pallas_call(
        up_body, name=f"ffn_up_{tag}", grid=(N_FF_SHARD, m // tm),
        in_specs=[pl.BlockSpec((tm, d), lambda j, i: (i, 0)),
                  pl.BlockSpec(wblk, lambda j, i: (j, 0, 0)),
                  pl.BlockSpec(wblk, lambda j, i: (j + N_FF_SHARD, 0, 0))],
        out_specs=[pl.BlockSpec((None, tm, FF_SHARD), lambda j, i: (j, i, 0)),
                   pl.BlockSpec((2, None, tm, FF_SHARD), lambda j, i: (0, j, i, 0))],
        out_shape=[jax.ShapeDtypeStruct((N_FF_SHARD, m, FF_SHARD), BF16),
                   jax.ShapeDtypeStruct((2, N_FF_SHARD, m, FF_SHARD), BF16)],
        compiler_params=_params(2),
    )(h, wup, wup)

    def down_epilogue(p, e_refs, o_refs):
        x_ref, g_ref = e_refs
        o_refs[0][...] = x_ref[...] + (0.5 * (1.0 + g_ref[0])) * p
        o_refs[1][...] = p.astype(BF16)

    if callable(wdown):
        wdown = wdown(a)
    wdown3 = wdown.reshape(N_FF_SHARD, FF_SHARD, d)
    x_out, f = _mm(a, wdown3, mode="nn", tm=tm, tn=d, tk=D_FF, ksub=N_FF_SHARD, name=f"ffn_down_{tag}",
                   shape=(m, d, D_FF), a_spec=((N_FF_SHARD, tm, FF_SHARD), lambda i, j, k: (0, i, 0)),
                   b_spec=((N_FF_SHARD, FF_SHARD, d), lambda i, j, k: (0, 0, 0)),
                   extras=[(x, (tm, d), lambda i, j: (i, 0)), (gate, (1, 1, d), lambda i, j: (i // tpb, 0, 0))],
                   outs=[((m, d), F32, (tm, d), lambda i, j: (i, 0)), ((m, d), BF16, (tm, d), lambda i, j: (i, 0))],
                   epilogue=down_epilogue)
    return x_out, (a, gu, f), wdown


def _ffn_bwd(dx_out, x, h, saved, wup, wdown, gn, scale, gate, seq, tag, on_grads):
    a, gu, f = saved
    m, d = x.shape
    tm = _tok_tile(seq)
    tpb = seq // tm
    df, dgate_parts = _residual_bwd(dx_out, f, gate, 0.5, seq, f"ffn_res_bwd_{tag}")

    def act_bwd_epilogue(p, e_refs, o_refs):
        g, u = e_refs[0][0].astype(F32), e_refs[0][1].astype(F32)
        sg = _sigmoid(g)
        o_refs[0][0] = (p * u * (sg * (1.0 + g * (1.0 - sg)))).astype(BF16)
        o_refs[0][1] = (p * (g * sg)).astype(BF16)

    gu_blk = (2, None, tm, FF_SHARD)
    (dgu,) = _mm(df, wdown, mode="nt", tm=tm, tn=FF_SHARD, tk=d, name=f"ffn_down_dx_{tag}", shape=(m, D_FF, d),
                 b_spec=((FF_SHARD, d), lambda i, j, k: (j, 0)),
                 extras=[(gu, gu_blk, lambda i, j: (0, j, i, 0))],
                 outs=[((2, N_FF_SHARD, m, FF_SHARD), BF16, gu_blk, lambda i, j: (0, j, i, 0))],
                 epilogue=act_bwd_epilogue)
    tt = 2 * tm if m % (2 * tm) == 0 else tm
    (dwdown,) = _mm(a, df, mode="tn", tm=FF_SHARD, tn=d, tk=tt, name=f"ffn_dwdown_{tag}", shape=(D_FF, d, m),
                    a_spec=((None, tt, FF_SHARD), lambda i, j, k: (i, k, 0)),
                    outs=[((D_FF, d), BF16, (FF_SHARD, d), lambda i, j: (i, 0))], epilogue=_store_epilogue([BF16]))
    dgu8 = dgu.reshape(2 * N_FF_SHARD, m, FF_SHARD)
    (dwup,) = _mm(h, dgu8, mode="tn", tm=d, tn=FF_SHARD, tk=tt, name=f"ffn_dwup_{tag}", shape=(d, 2 * D_FF, m),
                  b_spec=((None, tt, FF_SHARD), lambda i, j, k: (j, k, 0)),
                  outs=[((2 * N_FF_SHARD, d, FF_SHARD), BF16, (None, d, FF_SHARD), lambda i, j: (j, 0, 0))],
                  epilogue=_store_epilogue([BF16]))
    scale = scale + on_grads(dwup, dwdown)[0, 0]
    dx, nm_parts = _mm(dgu8, wup, mode="nt", tm=tm, tn=d, tk=D_FF, ksub=N_FF_SHARD, name=f"ffn_up_dx_{tag}",
                       shape=(m, d, 2 * D_FF), a_spec=((N_FF_SHARD, tm, FF_SHARD), lambda i, j, k: (k, i, 0)),
                       b_spec=((N_FF_SHARD, d, FF_SHARD), lambda i, j, k: (k, 0, 0)),
                       extras=[(x, (tm, d), lambda i, j: (i, 0)), (dx_out, (tm, d), lambda i, j: (i, 0)),
                               (gn, (1, d), lambda i, j: (0, 0)), (scale, (1, 1, d), lambda i, j: (i // tpb, 0, 0))],
                       outs=[((m, d), F32, (tm, d), lambda i, j: (i, 0)),
                             ((m // tm, SUBLANES, d), F32, (1, SUBLANES, d), lambda i, j: (i, 0, 0))],
                       epilogue=_normmod_bwd_epilogue)
    return dx, nm_parts, dgate_parts


def _shift_down(ext, n, rows):
    if n:
        ext = pltpu.roll(ext, n, 0)
    return ext[SUBLANES:SUBLANES + rows]


def _lru_gates(u, wr_ref, br_ref, wi_ref, bi_ref, lam_ref):
    ub = u.astype(BF16)
    r = _sigmoid(_dot(ub, wr_ref[...]) + br_ref[...])
    ig = _sigmoid(_dot(ub, wi_ref[...]) + bi_ref[...])
    sp = _softplus(-lam_ref[...])
    log_a = (-LRU_C * r) * sp
    a = jnp.exp(log_a)
    mult = jnp.sqrt(_neg_expm1(2.0 * log_a))
    return r, ig, sp, a, mult


def _conv(ext, cw_ref, cb_ref, rows):
    u = cb_ref[...] + cw_ref[3:4, :] * _shift_down(ext, 0, rows)
    for k in range(3):
        u = u + cw_ref[k:k + 1, :] * _shift_down(ext, 3 - k, rows)
    return u


def _lru_halo_spec(seq, ts):
    return pl.BlockSpec((SUBLANES, LRU_W),
                        lambda b, i: (jnp.maximum(b * (seq // SUBLANES) + i * (ts // SUBLANES) - 1, 0), 0))


def _lru_fwd(proj32, conv_w, conv_b, wr, br, wi, bi, lam, batch, seq):
    m = proj32.shape[0]
    ts = _tok_tile(seq)
    nt = seq // ts
    row = lambda b, i: (b * nt + i, 0)

    def body(x_ref, halo_ref, g_ref, cw_ref, cb_ref, wr_ref, br_ref, wi_ref, bi_ref, lam_ref,
             y_ref, h_ref, a_scr, b_scr, carry):
        i = pl.program_id(1)
        halo = jnp.where(i > 0, halo_ref[...], 0.0)
        ext = jnp.concatenate([halo, x_ref[...]], axis=0)
        u = _conv(ext, cw_ref, cb_ref, ts)
        _, ig, _, a, mult = _lru_gates(u, wr_ref, br_ref, wi_ref, bi_ref, lam_ref)
        a_scr[...] = a
        b_scr[...] = mult * (ig * u)

        @pl.when(i == 0)
        def _():
            carry[...] = jnp.zeros_like(carry)

        rid = lax.broadcasted_iota(jnp.int32, (SUBLANES, LRU_W), 0)

        def chunk(c, hprev):
            off = pl.multiple_of(c * SUBLANES, SUBLANES)
            av, bv = a_scr[pl.ds(off, SUBLANES), :], b_scr[pl.ds(off, SUBLANES), :]
            for d in (1, 2, 4):
                keep = rid >= d
                bv = jnp.where(keep, av * pltpu.roll(bv, d, 0) + bv, bv)
                av = jnp.where(keep, av * pltpu.roll(av, d, 0), av)
            h = av * hprev + bv
            h_ref[pl.ds(off, SUBLANES), :] = h
            return h[SUBLANES - 1:SUBLANES, :]

        carry[...] = lax.fori_loop(0, ts // SUBLANES, chunk, carry[...])
        gelu, _ = _gelu_parts(g_ref[...])
        y_ref[...] = h_ref[...] * gelu

    full = lambda shape: pl.BlockSpec(shape, lambda b, i: (0,) * len(shape))
    return pl.pallas_call(
        body, name="lru_fwd", grid=(batch, nt),
        in_specs=[pl.BlockSpec((ts, LRU_W), row), _lru_halo_spec(seq, ts),
                  pl.BlockSpec((ts, LRU_W), lambda b, i: (b * nt + i, 1)),
                  full((4, LRU_W)), full((1, LRU_W)), full((LRU_W, LRU_W)), full((1, LRU_W)),
                  full((LRU_W, LRU_W)), full((1, LRU_W)), full((1, LRU_W))],
        out_specs=[pl.BlockSpec((ts, LRU_W), row), pl.BlockSpec((ts, LRU_W), row)],
        out_shape=[jax.ShapeDtypeStruct((m, LRU_W), F32), jax.ShapeDtypeStruct((m, LRU_W), F32)],
        scratch_shapes=[pltpu.VMEM((ts, LRU_W), F32), pltpu.VMEM((ts, LRU_W), F32), pltpu.VMEM((1, LRU_W), F32)],
        compiler_params=_params(2),
    )(proj32, proj32, proj32, conv_w, conv_b, wr, br, wi, bi, lam)


def _lru_bwd(dy, proj32, h, conv_w, conv_b, wr, br, wi, bi, lam, batch, seq):
    m = proj32.shape[0]
    ts = _tok_tile(seq)
    nt = seq // ts
    row = lambda b, i: (b * nt + (nt - 1 - i), 0)
    halo = pl.BlockSpec((SUBLANES, LRU_W),
                        lambda b, i: (jnp.maximum(b * (seq // SUBLANES) + (nt - 1 - i) * (ts // SUBLANES) - 1, 0), 0))

    def body(dy_ref, x_ref, xhalo_ref, g_ref, h_ref, hhalo_ref, cw_ref, cb_ref, wr_ref, br_ref, wi_ref, bi_ref,
             lam_ref, dx_ref, dg_ref, dwr_ref, dwi_ref, sums_ref, a_scr, dh_scr, g_scr, carry, du_next):
        b, i = pl.program_id(0), pl.program_id(1)
        first_tile = i == nt - 1

        @pl.when((b == 0) & (i == 0))
        def _():
            dwr_ref[...] = jnp.zeros_like(dwr_ref)
            dwi_ref[...] = jnp.zeros_like(dwi_ref)
            sums_ref[...] = jnp.zeros_like(sums_ref)

        @pl.when(i == 0)
        def _():
            carry[...] = jnp.zeros_like(carry)
            du_next[...] = jnp.zeros_like(du_next)

        xhalo = jnp.where(first_tile, 0.0, xhalo_ref[...])
        ext = jnp.concatenate([xhalo, x_ref[...]], axis=0)
        u = _conv(ext, cw_ref, cb_ref, ts)
        r, ig, sp, a, mult = _lru_gates(u, wr_ref, br_ref, wi_ref, bi_ref, lam_ref)
        gelu, dgelu = _gelu_parts(g_ref[...])
        dyv, hv = dy_ref[...], h_ref[...]
        dg_ref[...] = (dyv * hv * dgelu).astype(BF16)
        a_scr[...] = a
        dh_scr[...] = dyv * gelu

        rid = lax.broadcasted_iota(jnp.int32, (SUBLANES, LRU_W), 0)
        nchunk = ts // SUBLANES

        def chunk(n, cg):
            off = pl.multiple_of((nchunk - 1 - n) * SUBLANES, SUBLANES)
            av, beta = a_scr[pl.ds(off, SUBLANES), :], dh_scr[pl.ds(off, SUBLANES), :]
            alpha = jnp.where(rid == SUBLANES - 1, 1.0, pltpu.roll(av, SUBLANES - 1, 0))
            for d in (1, 2, 4):
                keep = rid + d <= SUBLANES - 1
                beta = jnp.where(keep, beta + alpha * pltpu.roll(beta, SUBLANES - d, 0), beta)
                alpha = jnp.where(keep, alpha * pltpu.roll(alpha, SUBLANES - d, 0), alpha)
            gv = beta + alpha * cg
            g_scr[pl.ds(off, SUBLANES), :] = gv
            return av[0:1, :] * gv[0:1, :]

        carry[...] = lax.fori_loop(0, nchunk, chunk, carry[...])
        gv = g_scr[...]
        hhalo = jnp.where(first_tile, 0.0, hhalo_ref[...])
        hprev = _shift_down(jnp.concatenate([hhalo, hv], axis=0), 1, ts)
        dmult = gv * ig * u
        dig = gv * mult * u
        du = gv * mult * ig
        dlog_a = gv * hprev * a - dmult * a * a / mult
        dr = dlog_a * (-LRU_C * sp)
        dr_pre = dr * r * (1.0 - r)
        di_pre = dig * ig * (1.0 - ig)
        drb, dib, ub = dr_pre.astype(BF16), di_pre.astype(BF16), u.astype(BF16)
        du = du + _dot(drb, wr_ref[...], NT) + _dot(dib, wi_ref[...], NT)
        dwr_ref[...] += _dot(ub, drb, TN)
        dwi_ref[...] += _dot(ub, dib, TN)

        ext_du = jnp.concatenate([du, du_next[...]], axis=0)
        du_next[...] = du[0:SUBLANES, :]
        n_ext = ts + SUBLANES
        dx = cw_ref[3:4, :] * du
        sums = [_colsum(dr_pre), _colsum(di_pre), _colsum(dlog_a * (-LRU_C * r)), _colsum(du)]
        dcw = []
        for k in range(3):
            dx = dx + cw_ref[k:k + 1, :] * pltpu.roll(ext_du, n_ext - (3 - k), 0)[0:ts]
            dcw.append(_colsum(du * _shift_down(ext, 3 - k, ts)))
        dcw.append(_colsum(du * _shift_down(ext, 0, ts)))
        dx_ref[...] = dx.astype(BF16)
        sums_ref[...] += _rows_to_block(sums + dcw, LRU_W)

    full = lambda shape: pl.BlockSpec(shape, lambda b, i: (0,) * len(shape))
    tile = pl.BlockSpec((ts, LRU_W), row)
    return pl.pallas_call(
        body, name="lru_bwd", grid=(batch, nt),
        in_specs=[tile, tile, halo, pl.BlockSpec((ts, LRU_W), lambda b, i: (b * nt + (nt - 1 - i), 1)), tile, halo,
                  full((4, LRU_W)), full((1, LRU_W)), full((LRU_W, LRU_W)), full((1, LRU_W)),
                  full((LRU_W, LRU_W)), full((1, LRU_W)), full((1, LRU_W))],
        out_specs=[tile, tile, full((LRU_W, LRU_W)), full((LRU_W, LRU_W)), full((SUBLANES, LRU_W))],
        out_shape=[jax.ShapeDtypeStruct((m, LRU_W), BF16), jax.ShapeDtypeStruct((m, LRU_W), BF16),
                   jax.ShapeDtypeStruct((LRU_W, LRU_W), F32), jax.ShapeDtypeStruct((LRU_W, LRU_W), F32),
                   jax.ShapeDtypeStruct((SUBLANES, LRU_W), F32)],
        scratch_shapes=[pltpu.VMEM((ts, LRU_W), F32), pltpu.VMEM((ts, LRU_W), F32), pltpu.VMEM((ts, LRU_W), F32),
                        pltpu.VMEM((1, LRU_W), F32), pltpu.VMEM((SUBLANES, LRU_W), F32)],
        compiler_params=_params(2),
    )(dy, proj32, proj32, proj32, h, h, conv_w, conv_b, wr, br, wi, bi, lam)


def _head_masks():
    lane = lax.broadcasted_iota(jnp.int32, (1, LANES), 1)
    return lane < HEAD_DIM


def _stack_heads(x2):
    lo, zero = _head_masks(), jnp.zeros_like(x2)
    return jnp.concatenate([jnp.where(lo, x2, zero), jnp.where(lo, zero, x2)], axis=0)


def _unstack_heads(y):
    return jnp.where(_head_masks(), y[:TQ], y[TQ:])


def _stack_cols(a, b):
    return jnp.concatenate([a, b], axis=0)


def _causal(qi, kb, strict):
    r = jnp.bitwise_and(lax.broadcasted_iota(jnp.int32, (2 * TQ, BLK), 0), TQ - 1) + qi * TQ
    c = lax.broadcasted_iota(jnp.int32, (2 * TQ, BLK), 1) + kb * BLK
    return (c < r) if strict else (c <= r)


def _key_loop(qi, group, carry, descending=False):
    def trip(n, cr):
        done = [n * KB_PER_Q + j for j in range(KB_PER_Q)]
        return group([qi * KB_PER_Q - 1 - t for t in done] if descending else done, cr)

    return lax.fori_loop(0, qi, trip, carry)


def _one_by_one(block):
    def group(kbs, carry):
        for kb in kbs:
            carry = block(kb, carry, False)
        return carry
    return group


def _tri(cmp):
    r = lax.broadcasted_iota(jnp.int32, (BLK, BLK), 0)
    c = lax.broadcasted_iota(jnp.int32, (BLK, BLK), 1)
    return cmp(r, c)


def _dot_split(x, tri):
    hi, lo = _split2(x)
    return _dot(hi, tri) + _dot(lo, tri)


def _sb_fwd(proj16, batch, seq):
    nq = seq // TQ
    scale = HEAD_DIM ** -0.5

    def body(q_ref, k_ref, v_ref, y_ref, t_ref):
        qi = pl.program_id(2)
        qs = _stack_heads(q_ref[0])
        tri_after = _tri(lambda r, c: r > c).astype(BF16)

        def block(kb, carry, masked):
            acc, c = carry
            ks = pl.multiple_of(kb * BLK, BLK)
            k2, v2 = k_ref[0, pl.ds(ks, BLK), :], v_ref[0, pl.ds(ks, BLK), :]
            z = _dot(qs, k2, NT) * scale
            sp = _softplus(z)
            l = -sp
            if masked:
                valid = _causal(qi, kb, True)
                l = jnp.where(valid, l, 0.0)
            w = jnp.exp((z - sp) + _dot_split(l, tri_after) + c)
            if masked:
                w = jnp.where(valid, w, 0.0)
            return acc + _dot(w.astype(BF16), v2), c + jnp.sum(l, axis=1, keepdims=True)

        def group(kbs, carry):
            acc, c = carry
            kv = [(k_ref[0, pl.ds(pl.multiple_of(kb * BLK, BLK), BLK), :],
                   v_ref[0, pl.ds(pl.multiple_of(kb * BLK, BLK), BLK), :]) for kb in kbs]
            zs = [_dot(qs, k2, NT) * scale for k2, _ in kv]
            sps = [_softplus(z) for z in zs]
            afters = [_dot_split(-sp, tri_after) for sp in sps]
            for z, sp, after, (_, v2) in zip(zs, sps, afters, kv):
                acc = acc + _dot(jnp.exp((z - sp) + after + c).astype(BF16), v2)
                c = c - jnp.sum(sp, axis=1, keepdims=True)
            return acc, c

        carry = (jnp.zeros((2 * TQ, LANES), F32), jnp.zeros((2 * TQ, 1), F32))
        first = qi * KB_PER_Q
        for n in reversed(range(KB_PER_Q)):
            carry = block(first + n, carry, True)
        acc, c = _key_loop(qi, group, carry, descending=True)
        y_ref[...] = _unstack_heads(acc)
        t_ref[0] = _unstack_heads(jnp.broadcast_to(c, (2 * TQ, LANES)))

    m = batch * seq
    return pl.pallas_call(
        body, name="sb_fwd", grid=(batch, 2, nq),
        in_specs=[pl.BlockSpec((1, TQ, LANES), lambda b, p, q: (b, q, COL_SBQ + p)),
                  pl.BlockSpec((1, seq, LANES), lambda b, p, q: (b, 0, COL_SBK + p)),
                  pl.BlockSpec((1, seq, LANES), lambda b, p, q: (b, 0, COL_SBV + p))],
        out_specs=[pl.BlockSpec((TQ, LANES), lambda b, p, q: (b * nq + q, p)),
                   pl.BlockSpec((1, TQ, LANES), lambda b, p, q: (p, b * nq + q, 0))],
        out_shape=[jax.ShapeDtypeStruct((m, ATT_W), F32), jax.ShapeDtypeStruct((2, m, LANES), F32)],
        compiler_params=_params(3),
    )(proj16, proj16, proj16)


def _sb_bwd(dy, t, proj16, batch, seq):
    nq = seq // TQ
    scale = HEAD_DIM ** -0.5

    def body(dy_ref, t_ref, q_ref, k_ref, v_ref, dq_ref, dk_ref, dv_ref):
        qi = pl.program_id(2)

        @pl.when(qi == 0)
        def _():
            dk_ref[...] = jnp.zeros_like(dk_ref)
            dv_ref[...] = jnp.zeros_like(dv_ref)

        t2 = t_ref[0]
        qs, dys = _stack_heads(q_ref[0]), _stack_heads(dy_ref[...].astype(BF16))
        tot = _stack_cols(t2[:, 0:1], t2[:, HEAD_DIM:HEAD_DIM + 1])
        tri_incl = _tri(lambda r, c: r <= c).astype(BF16)
        tri_excl = _tri(lambda r, c: r < c).astype(BF16)

        def block(kb, carry, masked):
            dq, pc, ec = carry
            ks = pl.multiple_of(kb * BLK, BLK)
            k2, v2 = k_ref[0, pl.ds(ks, BLK), :], v_ref[0, pl.ds(ks, BLK), :]
            z = _dot(qs, k2, NT) * scale
            sp = _softplus(z)
            l, b = -sp, z - sp
            sig = jnp.exp(b)
            if masked:
                valid = _causal(qi, kb, True)
                l = jnp.where(valid, l, 0.0)
            after = tot - (pc + _dot_split(l, tri_incl))
            w = jnp.exp(b + after)
            if masked:
                w = jnp.where(valid, w, 0.0)
            e = _dot(dys, v2, NT) * w
            et = ec + _dot_split(e, tri_excl)
            dz = e * (1.0 - sig) - et * sig
            if masked:
                dz = jnp.where(valid, dz, 0.0)
            dzb = (dz * scale).astype(BF16)
            dk_ref[0, pl.ds(ks, BLK), :] += _dot(dzb, qs, TN)
            dv_ref[0, pl.ds(ks, BLK), :] += _dot(w.astype(BF16), dys, TN)
            return (dq + _dot(dzb, k2), pc + jnp.sum(l, axis=1, keepdims=True),
                    ec + jnp.sum(e, axis=1, keepdims=True))

        col = jnp.zeros((2 * TQ, 1), F32)
        first = qi * KB_PER_Q
        carry = _key_loop(qi, _one_by_one(block),(jnp.zeros((2 * TQ, LANES), F32), col, col))
        for n in range(KB_PER_Q):
            carry = block(first + n, carry, True)
        dq_ref[...] = _unstack_heads(carry[0])

    m = batch * seq
    whole = lambda col: pl.BlockSpec((1, seq, LANES), lambda b, p, q: (b, 0, col + p))
    return pl.pallas_call(
        body, name="sb_bwd", grid=(batch, 2, nq),
        in_specs=[pl.BlockSpec((TQ, LANES), lambda b, p, q: (b * nq + q, p)),
                  pl.BlockSpec((1, TQ, LANES), lambda b, p, q: (p, b * nq + q, 0)),
                  pl.BlockSpec((1, TQ, LANES), lambda b, p, q: (b, q, COL_SBQ + p)),
                  whole(COL_SBK), whole(COL_SBV)],
        out_specs=[pl.BlockSpec((TQ, LANES), lambda b, p, q: (b * nq + q, p)), whole(0), whole(0)],
        out_shape=[jax.ShapeDtypeStruct((m, ATT_W), F32), jax.ShapeDtypeStruct((batch, seq, ATT_W), F32),
                   jax.ShapeDtypeStruct((batch, seq, ATT_W), F32)],
        compiler_params=_params(3),
    )(dy, t, proj16, proj16, proj16)


def _fox_pre(proj32, gq, gk, bf, group_mean, batch, seq):
    m = proj32.shape[0]
    ts = _tok_tile(seq)
    nt = seq // ts

    def body(q_ref, k_ref, f_ref, gq_ref, gk_ref, bf_ref, gm_ref, fq_ref, fk_ref, fc_ref, carry):
        i = pl.program_id(1)

        @pl.when(i == 0)
        def _():
            carry[...] = jnp.zeros_like(carry)

        gm = gm_ref[...]
        for src, g_ref, dst in ((q_ref, gq_ref, fq_ref), (k_ref, gk_ref, fk_ref)):
            v = src[...]
            ms = _dot_split(v * v, gm)
            dst[...] = (v * lax.rsqrt(ms + EPS) * g_ref[...]).astype(BF16)
        z = f_ref[...] + bf_ref[...]
        lf = jnp.minimum(z, 0.0) - jnp.log(1.0 + jnp.exp(-jnp.abs(z)))
        r = lax.broadcasted_iota(jnp.int32, (ts, ts), 0)
        c = lax.broadcasted_iota(jnp.int32, (ts, ts), 1)
        tri = (r >= c).astype(BF16)
        hi, mid, low = _split3(lf)
        fc = _dot(tri, hi) + _dot(tri, mid) + _dot(tri, low) + carry[...]
        fc_ref[...] = fc
        carry[...] = fc[ts - 1:ts, :]

    full = lambda shape: pl.BlockSpec(shape, lambda b, i: (0,) * len(shape))
    return pl.pallas_call(
        body, name="fox_pre", grid=(batch, nt),
        in_specs=[pl.BlockSpec((ts, ATT_W), lambda b, i: (b * nt + i, 7)),
                  pl.BlockSpec((ts, ATT_W), lambda b, i: (b * nt + i, 8)),
                  pl.BlockSpec((ts, LANES), lambda b, i: (b * nt + i, COL_FXF)),
                  full((1, ATT_W)), full((1, ATT_W)), full((1, LANES)), full((ATT_W, ATT_W))],
        out_specs=[pl.BlockSpec((ts, ATT_W), lambda b, i: (b * nt + i, 0)),
                   pl.BlockSpec((ts, ATT_W), lambda b, i: (b * nt + i, 0)),
                   pl.BlockSpec((ts, LANES), lambda b, i: (b * nt + i, 0))],
        out_shape=[jax.ShapeDtypeStruct((m, ATT_W), BF16), jax.ShapeDtypeStruct((m, ATT_W), BF16),
                   jax.ShapeDtypeStruct((m, LANES), F32)],
        scratch_shapes=[pltpu.VMEM((1, LANES), F32)],
        compiler_params=_params(2),
    )(proj32, proj32, proj32, gq, gk, bf, group_mean)


def _fox_specs(batch, seq):
    nq = seq // TQ
    return dict(
        qblk=pl.BlockSpec((1, TQ, LANES), lambda b, p, q: (b, q, p)),
        whole=pl.BlockSpec((1, seq, LANES), lambda b, p, q: (b, 0, p)),
        vwhole=pl.BlockSpec((1, seq, LANES), lambda b, p, q: (b, 0, COL_FXV + p)),
        fcol=pl.BlockSpec((1, 1, TQ, 2), lambda b, p, q: (b, p, q, 0)),
        frow=pl.BlockSpec((1, 1, 2, seq), lambda b, p, q: (b, p, 0, 0)),
        rows=pl.BlockSpec((TQ, LANES), lambda b, p, q: (b * nq + q, p)),
        stat=pl.BlockSpec((1, TQ, LANES), lambda b, p, q: (p, b * nq + q, 0)),
    )


def _fox_logits(qs, k2, fq_col, fr_ref, ks, is_a, scale):
    fk_row = jnp.where(is_a, fr_ref[0, 0, 0:1, pl.ds(ks, BLK)], fr_ref[0, 0, 1:2, pl.ds(ks, BLK)])
    return _dot(qs, k2, NT) * scale + fq_col - fk_row


def _fox_fwd(fq, fk, proj16, fcol, frow, batch, seq):
    nq = seq // TQ
    scale = HEAD_DIM ** -0.5

    def body(q_ref, k_ref, v_ref, fc_ref, fr_ref, y_ref, lse_ref):
        qi = pl.program_id(2)
        qs = _stack_heads(q_ref[0])
        fcv = fc_ref[0, 0]
        fq_col = _stack_cols(fcv[:, 0:1], fcv[:, 1:2])
        is_a = lax.broadcasted_iota(jnp.int32, (2 * TQ, 1), 0) < TQ

        def block(kb, carry, masked):
            acc, mx, den = carry
            ks = pl.multiple_of(kb * BLK, BLK)
            k2, v2 = k_ref[0, pl.ds(ks, BLK), :], v_ref[0, pl.ds(ks, BLK), :]
            s = _fox_logits(qs, k2, fq_col, fr_ref, ks, is_a, scale)
            if masked:
                s = jnp.where(_causal(qi, kb, False), s, NEG_BIG)
            mx_new = jnp.maximum(mx, jnp.max(s, axis=1, keepdims=True))
            p = jnp.exp(s - mx_new)
            alpha = jnp.exp(mx - mx_new)
            return (alpha * acc + _dot(p.astype(BF16), v2), mx_new, alpha * den + jnp.sum(p, axis=1, keepdims=True))

        first = qi * KB_PER_Q
        carry = (jnp.zeros((2 * TQ, LANES), F32), jnp.full((2 * TQ, 1), NEG_BIG, F32), jnp.zeros((2 * TQ, 1), F32))
        carry = _key_loop(qi, _one_by_one(block),carry)
        for n in range(KB_PER_Q):
            carry = block(first + n, carry, True)
        acc, mx, den = carry
        y_ref[...] = _unstack_heads(acc / den)
        lse_ref[0] = _unstack_heads(jnp.broadcast_to(mx + jnp.log(den), (2 * TQ, LANES)))

    m = batch * seq
    sp = _fox_specs(batch, seq)
    return pl.pallas_call(
        body, name="fox_fwd", grid=(batch, 2, nq),
        in_specs=[sp["qblk"], sp["whole"], sp["vwhole"], sp["fcol"], sp["frow"]],
        out_specs=[sp["rows"], sp["stat"]],
        out_shape=[jax.ShapeDtypeStruct((m, ATT_W), F32), jax.ShapeDtypeStruct((2, m, LANES), F32)],
        compiler_params=_params(3),
    )(fq, fk, proj16, fcol, frow)


def _fox_bwd(dy, y, lse, fq, fk, proj16, fcol, frow, batch, seq):
    nq = seq // TQ
    scale = HEAD_DIM ** -0.5

    def body(dy_ref, y_ref, lse_ref, q_ref, k_ref, v_ref, fc_ref, fr_ref, dq_ref, dk_ref, dv_ref, dfr_ref, dfc_ref):
        qi = pl.program_id(2)

        @pl.when(qi == 0)
        def _():
            dk_ref[...] = jnp.zeros_like(dk_ref)
            dv_ref[...] = jnp.zeros_like(dv_ref)
            dfr_ref[...] = jnp.zeros_like(dfr_ref)

        lo = _head_masks()
        lane = lax.broadcasted_iota(jnp.int32, (1, LANES), 1)
        dy2, lse2, fcv = dy_ref[...], lse_ref[0], fc_ref[0, 0]
        qs, dys = _stack_heads(q_ref[0]), _stack_heads(dy2.astype(BF16))
        dyy = dy2 * y_ref[...]
        delta = _stack_cols(jnp.sum(jnp.where(lo, dyy, 0.0), axis=1, keepdims=True),
                            jnp.sum(jnp.where(lo, 0.0, dyy), axis=1, keepdims=True))
        lse_col = _stack_cols(lse2[:, 0:1], lse2[:, HEAD_DIM:HEAD_DIM + 1])
        fq_col = _stack_cols(fcv[:, 0:1], fcv[:, 1:2])
        is_a = lax.broadcasted_iota(jnp.int32, (2 * TQ, 1), 0) < TQ

        def block(kb, carry, masked):
            dq, rs = carry
            ks = pl.multiple_of(kb * BLK, BLK)
            k2, v2 = k_ref[0, pl.ds(ks, BLK), :], v_ref[0, pl.ds(ks, BLK), :]
            p = jnp.exp(_fox_logits(qs, k2, fq_col, fr_ref, ks, is_a, scale) - lse_col)
            if masked:
                p = jnp.where(_causal(qi, kb, False), p, 0.0)
            ds = p * (_dot(dys, v2, NT) - delta)
            dsb = (ds * scale).astype(BF16)
            dk_ref[0, pl.ds(ks, BLK), :] += _dot(dsb, qs, TN)
            dv_ref[0, pl.ds(ks, BLK), :] += _dot(p.astype(BF16), dys, TN)
            dfr_ref[0, 0, 0:1, pl.ds(ks, BLK)] -= jnp.sum(ds[:TQ], axis=0, keepdims=True)
            dfr_ref[0, 0, 1:2, pl.ds(ks, BLK)] -= jnp.sum(ds[TQ:], axis=0, keepdims=True)
            return dq + _dot(dsb, k2), rs + jnp.sum(ds, axis=1, keepdims=True)

        first = qi * KB_PER_Q
        carry = _key_loop(qi, _one_by_one(block),(jnp.zeros((2 * TQ, LANES), F32), jnp.zeros((2 * TQ, 1), F32)))
        for n in range(KB_PER_Q):
            carry = block(first + n, carry, True)
        dq, rs = carry
        dq_ref[...] = _unstack_heads(dq)
        dfc_ref[0] = jnp.where(lane == 0, rs[:TQ], jnp.where(lane == 1, rs[TQ:], 0.0))

    m = batch * seq
    sp = _fox_specs(batch, seq)
    return pl.pallas_call(
        body, name="fox_bwd", grid=(batch, 2, nq),
        in_specs=[sp["rows"], sp["rows"], sp["stat"], sp["qblk"], sp["whole"], sp["vwhole"], sp["fcol"], sp["frow"]],
        out_specs=[sp["rows"], sp["whole"], sp["whole"],
                   pl.BlockSpec((1, 1, SUBLANES, seq), lambda b, p, q: (b, p, 0, 0)), sp["stat"]],
        out_shape=[jax.ShapeDtypeStruct((m, ATT_W), F32), jax.ShapeDtypeStruct((batch, seq, ATT_W), F32),
                   jax.ShapeDtypeStruct((batch, seq, ATT_W), F32),
                   jax.ShapeDtypeStruct((batch, 2, SUBLANES, seq), F32), jax.ShapeDtypeStruct((2, m, LANES), F32)],
        compiler_params=_params(3),
    )(dy, y, lse, fq, fk, proj16, fcol, frow)


def _fox_post_bwd(dfq, dfk, dfc, proj32, gq, gk, bf, group_mean, batch, seq):
    m = proj32.shape[0]
    ts = _tok_tile(seq)
    nt = seq // ts
    tile = lambda w, col: pl.BlockSpec((ts, w), lambda b, i: (b * nt + (nt - 1 - i), col))

    def body(dfq_ref, dfk_ref, dfc_ref, q_ref, k_ref, f_ref, gq_ref, gk_ref, bf_ref, gm_ref,
             dq_ref, dk_ref, df_ref, gs_ref, bs_ref, carry):
        i = pl.program_id(1)

        @pl.when(i == 0)
        def _():
            carry[...] = jnp.zeros_like(carry)

        gm = gm_ref[...]
        rows = []
        for src, g_ref, d_ref, dst in ((q_ref, gq_ref, dfq_ref, dq_ref), (k_ref, gk_ref, dfk_ref, dk_ref)):
            v, dv = src[...], d_ref[...]
            rstd = lax.rsqrt(_dot_split(v * v, gm) + EPS)
            vhat = v * rstd
            rows.append(_colsum(dv * vhat))
            dvh = dv * g_ref[...]
            dst[...] = (rstd * (dvh - vhat * _dot_split(dvh * vhat, gm))).astype(BF16)
        gs_ref[0] = _rows_to_block(rows, ATT_W)

        dfc_v = dfc_ref[...]
        r = lax.broadcasted_iota(jnp.int32, (ts, ts), 0)
        c = lax.broadcasted_iota(jnp.int32, (ts, ts), 1)
        tri = (r <= c).astype(BF16)
        hi, mid, low = _split3(dfc_v)
        dlf = _dot(tri, hi) + _dot(tri, mid) + _dot(tri, low) + carry[...]
        carry[...] = dlf[0:1, :]
        z = f_ref[...] + bf_ref[...]
        dz = dlf * _sigmoid(-z)
        df_ref[...] = dz.astype(BF16)
        bs_ref[0] = _rows_to_block([_colsum(dz)], LANES)

    full = lambda shape: pl.BlockSpec(shape, lambda b, i: (0,) * len(shape))
    part = lambda w: pl.BlockSpec((1, SUBLANES, w), lambda b, i: (b * nt + (nt - 1 - i), 0, 0))
    return pl.pallas_call(
        body, name="fox_post_bwd", grid=(batch, nt),
        in_specs=[tile(ATT_W, 0), tile(ATT_W, 0), tile(LANES, 0), tile(ATT_W, 7), tile(ATT_W, 8), tile(LANES, COL_FXF),
                  full((1, ATT_W)), full((1, ATT_W)), full((1, LANES)), full((ATT_W, ATT_W))],
        out_specs=[tile(ATT_W, 0), tile(ATT_W, 0), tile(LANES, 0), part(ATT_W), part(LANES)],
        out_shape=[jax.ShapeDtypeStruct((m, ATT_W), BF16), jax.ShapeDtypeStruct((m, ATT_W), BF16),
                   jax.ShapeDtypeStruct((m, LANES), BF16),
                   jax.ShapeDtypeStruct((batch * nt, SUBLANES, ATT_W), F32),
                   jax.ShapeDtypeStruct((batch * nt, SUBLANES, LANES), F32)],
        scratch_shapes=[pltpu.VMEM((1, LANES), F32)],
        compiler_params=_params(2),
    )(dfq, dfk, dfc, proj32, proj32, proj32, gq, gk, bf, group_mean)


_GROUPS = ((0, LRU_W), (LRU_W, LRU_W + ATT_W), (LRU_W + ATT_W, LRU_W + 2 * ATT_W))


def _outnorm(y_lru, y_sb, y_fox, gmix, seq):
    m = y_lru.shape[0]
    tm = _tok_tile(seq)

    def body(a_ref, b_ref, c_ref, g_ref, o_ref):
        parts = []
        for ref in (a_ref, b_ref, c_ref):
            v = ref[...]
            parts.append(v * lax.rsqrt(jnp.mean(v * v, axis=-1, keepdims=True) + EPS))
        o_ref[...] = (jnp.concatenate(parts, axis=1) * g_ref[...]).astype(BF16)

    t = lambda w: pl.BlockSpec((tm, w), lambda i: (i, 0))
    return pl.pallas_call(
        body, name="outnorm", grid=(m // tm,),
        in_specs=[t(LRU_W), t(ATT_W), t(ATT_W), pl.BlockSpec((1, D_MODEL), lambda i: (0, 0))],
        out_specs=t(D_MODEL), out_shape=jax.ShapeDtypeStruct((m, D_MODEL), BF16), compiler_params=_params(1),
    )(y_lru, y_sb, y_fox, gmix)


def _outnorm_bwd_epilogue(p, e_refs, o_refs):
    gmix = e_refs[3][...]
    dg = []
    for n, (lo, hi) in enumerate(_GROUPS):
        v, dyn = e_refs[n][...], p[:, lo:hi]
        rstd = lax.rsqrt(jnp.mean(v * v, axis=-1, keepdims=True) + EPS)
        vhat = v * rstd
        dg.append(_colsum(dyn * vhat))
        dvh = dyn * gmix[:, lo:hi]
        o_refs[n][...] = rstd * (dvh - vhat * jnp.mean(dvh * vhat, axis=-1, keepdims=True))
    o_refs[3][0] = _rows_to_block([jnp.concatenate(dg, axis=1)], p.shape[1])


def _pair_layouts(fcum, batch, seq):
    f4 = fcum[:, :4].reshape(batch, seq, 2, 2)
    return f4.transpose(0, 2, 1, 3), f4.transpose(0, 2, 3, 1)


def _gate_grad_cols(dfr, dfc, batch, seq):
    keys = dfr[:, :, :2, :].transpose(0, 3, 1, 2).reshape(batch * seq, 4)
    queries = dfc[:, :, :2].transpose(1, 0, 2).reshape(batch * seq, 4)
    return jnp.pad(keys + queries, ((0, 0), (0, LANES - 4)))


def _mixer_fwd(x, h, w, gate, batch, seq):
    m, d = x.shape
    tm = _tok_tile(seq)
    tpb = seq // tm

    def in_epilogue(p, e_refs, o_refs):
        o_refs[0][...] = p
        o_refs[1][...] = p.astype(BF16)

    tn_in = 896
    proj32, proj16 = _mm(h, w["w_in"], mode="nn", tm=tm, tn=tn_in, tk=d, name="mix_in",
                         outs=[((m, N_IN_PAD), F32, (tm, tn_in), lambda i, j: (i, j)),
                               ((m, N_IN_PAD), BF16, (tm, tn_in), lambda i, j: (i, j))],
                         epilogue=in_epilogue)
    y_lru, h_lru = _lru_fwd(proj32, w["conv_w"], w["conv_b"], w["wr"], w["br"], w["wi"], w["bi"], w["lam"], batch, seq)
    p16 = proj16.reshape(batch, seq, N_IN_PAD)
    y_sb, t_sb = _sb_fwd(p16, batch, seq)
    fq, fk, fcum = _fox_pre(proj32, w["gq"], w["gk"], w["bf"], w["group_mean"], batch, seq)
    fcol, frow = _pair_layouts(fcum, batch, seq)
    fq3, fk3 = fq.reshape(batch, seq, ATT_W), fk.reshape(batch, seq, ATT_W)
    y_fox, lse = _fox_fwd(fq3, fk3, p16, fcol, frow, batch, seq)
    ynorm = _outnorm(y_lru, y_sb, y_fox, w["gmix"], seq)

    def out_epilogue(p, e_refs, o_refs):
        x_ref, g_ref = e_refs
        o_refs[0][...] = x_ref[...] + (1.0 + g_ref[0]) * p
        o_refs[1][...] = p.astype(BF16)

    x_out, out = _mm(ynorm, w["w_out"], mode="nn", tm=tm, tn=d, tk=d, name="mix_out",
                     extras=[(x, (tm, d), lambda i, j: (i, 0)), (gate, (1, 1, d), lambda i, j: (i // tpb, 0, 0))],
                     outs=[((m, d), F32, (tm, d), lambda i, j: (i, 0)), ((m, d), BF16, (tm, d), lambda i, j: (i, 0))],
                     epilogue=out_epilogue)
    saved = dict(proj32=proj32, p16=p16, h_lru=h_lru, y_lru=y_lru, y_sb=y_sb, t_sb=t_sb, fq3=fq3, fk3=fk3,
                 fcol=fcol, frow=frow, y_fox=y_fox, lse=lse, ynorm=ynorm, out=out)
    return x_out, saved


def _mixer_bwd(dx_out, x, h, s, w, gn, scale, gate, batch, seq, on_grads):
    m, d = x.shape
    tm = _tok_tile(seq)
    tpb = seq // tm
    dout, dgate_parts = _residual_bwd(dx_out, s["out"], gate, 1.0, seq, "mix_res_bwd")
    (dw_out,) = _mm(s["ynorm"], dout, mode="tn", tm=d, tn=d, tk=tm, name="mix_dwout",
                    outs=[((d, d), BF16, (d, d), lambda i, j: (i, j))], epilogue=_store_epilogue([BF16]))
    dy_lru, dy_sb, dy_fox, gmix_parts = _mm(
        dout, w["w_out"], mode="nt", tm=tm, tn=d, tk=d, name="mix_out_dx",
        extras=[(s["y_lru"], (tm, LRU_W), lambda i, j: (i, 0)), (s["y_sb"], (tm, ATT_W), lambda i, j: (i, 0)),
                (s["y_fox"], (tm, ATT_W), lambda i, j: (i, 0)), (w["gmix"], (1, d), lambda i, j: (0, 0))],
        outs=[((m, LRU_W), F32, (tm, LRU_W), lambda i, j: (i, 0)), ((m, ATT_W), F32, (tm, ATT_W), lambda i, j: (i, 0)),
              ((m, ATT_W), F32, (tm, ATT_W), lambda i, j: (i, 0)),
              ((m // tm, SUBLANES, d), F32, (1, SUBLANES, d), lambda i, j: (i, 0, 0))],
        epilogue=_outnorm_bwd_epilogue)

    dsq, dsk, dsv = _sb_bwd(dy_sb, s["t_sb"], s["p16"], batch, seq)
    dfq, dfk, dfv, dfr, dfc = _fox_bwd(dy_fox, s["y_fox"], s["lse"], s["fq3"], s["fk3"], s["p16"], s["fcol"],
                                       s["frow"], batch, seq)
    dfc_cols = _gate_grad_cols(dfr, dfc, batch, seq)
    dxq, dxk, dxf, gqk_parts, bf_parts = _fox_post_bwd(dfq, dfk.reshape(m, ATT_W), dfc_cols, s["proj32"],
                                                       w["gq"], w["gk"], w["bf"], w["group_mean"], batch, seq)
    dlx, dlg, dwr, dwi, lru_sums = _lru_bwd(dy_lru, s["proj32"], s["h_lru"], w["conv_w"], w["conv_b"], w["wr"],
                                            w["br"], w["wi"], w["bi"], w["lam"], batch, seq)
    dproj = jnp.concatenate([dlx, dlg, dsq.astype(BF16), dsk.reshape(m, ATT_W).astype(BF16),
                             dsv.reshape(m, ATT_W).astype(BF16), dxq, dxk, dfv.reshape(m, ATT_W).astype(BF16), dxf],
                            axis=1)
    tn_in = 896
    (dw_in,) = _mm(h, dproj, mode="tn", tm=d, tn=tn_in, tk=tm, name="mix_dwin",
                   outs=[((d, N_IN_PAD), BF16, (d, tn_in), lambda i, j: (i, j))], epilogue=_store_epilogue([BF16]))
    scale = scale + on_grads(dw_in, dw_out)[0, 0]
    dx, nm_parts = _mm(dproj, w["w_in"], mode="nt", tm=tm, tn=d, tk=tn_in, name="mix_in_dx",
                       extras=[(x, (tm, d), lambda i, j: (i, 0)), (dx_out, (tm, d), lambda i, j: (i, 0)),
                               (gn, (1, d), lambda i, j: (0, 0)), (scale, (1, 1, d), lambda i, j: (i // tpb, 0, 0))],
                       outs=[((m, d), F32, (tm, d), lambda i, j: (i, 0)),
                             ((m // tm, SUBLANES, d), F32, (1, SUBLANES, d), lambda i, j: (i, 0, 0))],
                       epilogue=_normmod_bwd_epilogue)
    grads = dict(dwr=dwr, dwi=dwi, lru_sums=lru_sums, gmix_parts=gmix_parts,
                 gqk_parts=gqk_parts, bf_parts=bf_parts)
    return dx, grads, nm_parts, dgate_parts


def _block_diag(w):
    nb = w.shape[0]
    eye = jnp.eye(nb, dtype=w.dtype)
    return (eye[:, None, :, None] * w[:, :, None, :]).reshape(nb * HEAD_DIM, nb * HEAD_DIM)


def _block_diag_grad(g):
    nb = LRU_W // HEAD_DIM
    g4 = g.reshape(nb, HEAD_DIM, nb, HEAD_DIM)
    return jnp.stack([g4[n, :, n, :] for n in range(nb)])


def _per_batch(parts, batch, row):
    r = parts[:, row, :]
    return r.reshape(batch, -1, r.shape[-1]).sum(axis=1)


def _local_step(x3, target3, mod, wts, big_weights):
    batch, seq, d = x3.shape
    assert seq % TQ == 0, seq
    m = batch * seq
    n_layers = mod.shape[0]
    x = x3.reshape(m, d)
    group_mean = _block_diag(jnp.full((ATT_W // HEAD_DIM, HEAD_DIM, HEAD_DIM), 1.0 / HEAD_DIM, BF16))
    vec = lambda l, j, t: mod[l, :, j, t][:, None, :]

    layers, saved = [], []
    for l in range(n_layers):
        gq = jnp.tile(wts["g_qk"][l, 0], ATT_W // HEAD_DIM)[None, :]
        gk = jnp.tile(wts["g_qk"][l, 1], ATT_W // HEAD_DIM)[None, :]
        bf = jnp.pad(wts["b_fgate"][l], (0, LANES - 4))[None, :]
        lw = dict(conv_w=wts["conv_w"][l],
                  conv_b=wts["conv_b"][l][None, :], wr=_block_diag(wts["w_rgate"][l]).astype(BF16),
                  br=wts["b_rgate"][l][None, :], wi=_block_diag(wts["w_igate"][l]).astype(BF16),
                  bi=wts["b_igate"][l][None, :], lam=wts["lru_lambda"][l][None, :], gq=gq, gk=gk, bf=bf,
                  group_mean=group_mean, gmix=wts["g_mix_out"][l][None, :])
        layers.append(lw)
        gn = lambda j: wts["g_norm"][l, j][None, :]
        sv = dict(x0=x)
        sv["h0"] = _normmod(x, gn(0), vec(l, 0, 1), vec(l, 0, 0), seq, f"normmod_{l}_0")
        wup, wdown = big_weights(l, "ffn0", sv["h0"])
        x, sv["ffn0"], wdown = _ffn_fwd(x, sv["h0"], wup, wdown, vec(l, 0, 2), seq, f"{l}_0")
        sv["w_ffn0"] = (wup, wdown)
        sv["x1"] = x
        sv["h1"] = _normmod(x, gn(1), vec(l, 1, 1), vec(l, 1, 0), seq, f"normmod_{l}_1")
        lw["w_in"], lw["w_out"] = big_weights(l, "mix", sv["h1"])
        x, sv["mix"] = _mixer_fwd(x, sv["h1"], lw, vec(l, 1, 2), batch, seq)
        sv["x2"] = x
        sv["h2"] = _normmod(x, gn(2), vec(l, 2, 1), vec(l, 2, 0), seq, f"normmod_{l}_2")
        wup, wdown = big_weights(l, "ffn1", sv["h2"])
        x, sv["ffn1"], wdown = _ffn_fwd(x, sv["h2"], wup, wdown, vec(l, 2, 2), seq, f"{l}_1")
        sv["w_ffn1"] = (wup, wdown)
        saved.append(sv)

    dx, loss_parts = _loss_head(x, target3.reshape(m, d), seq)
    loss = jnp.sum(loss_parts[:, 0, 0])

    handles = {}

    def scatter(key, shapes):
        def on_grads(*grads):
            ops = [(g.reshape(shape), 0, "scatter") for g, shape in zip(grads, shapes)]
            handles[key], token = _flight_start(ops, f"grads_{key[0]}_{key[1]}_start")
            return token
        return on_grads

    ffn_shapes = ((2 * N_FF_SHARD, d, FF_SHARD), (N_DEV, D_FF // N_DEV, d))
    mix_shapes = ((N_DEV, d // N_DEV, N_IN_PAD), (N_DEV, d // N_DEV, d))
    small = {k: [] for k in ("dmod", "g_norm", "b_fgate", "conv_w", "conv_b", "w_rgate", "b_rgate", "w_igate",
                             "b_igate", "lru_lambda", "g_qk", "g_mix_out")}
    for l in reversed(range(n_layers)):
        sv, lw = saved[l], layers[l]
        gn = lambda j: wts["g_norm"][l, j][None, :]
        dx, nm2, dg2 = _ffn_bwd(dx, sv["x2"], sv["h2"], sv["ffn1"], *sv["w_ffn1"], gn(2), vec(l, 2, 1), vec(l, 2, 2),
                                seq, f"{l}_1", scatter((l, "ffn1"), ffn_shapes))
        dx, mg, nm1, dg1 = _mixer_bwd(dx, sv["x1"], sv["h1"], sv["mix"], lw, gn(1), vec(l, 1, 1), vec(l, 1, 2),
                                      batch, seq, scatter((l, "mix"), mix_shapes))
        dx, nm0, dg0 = _ffn_bwd(dx, sv["x0"], sv["h0"], sv["ffn0"], *sv["w_ffn0"], gn(0), vec(l, 0, 1), vec(l, 0, 2),
                                seq, f"{l}_0", scatter((l, "ffn0"), ffn_shapes))
        dmod_l, gnorm_l = [], []
        for nm, dg in ((nm0, dg0), (nm1, dg1), (nm2, dg2)):
            dmod_l.append(jnp.stack([_per_batch(nm, batch, 0), _per_batch(nm, batch, 1), _per_batch(dg, batch, 0)],
                                    axis=1))
            gnorm_l.append(jnp.sum(nm[:, 2, :], axis=0))
        small["dmod"].insert(0, jnp.stack(dmod_l, axis=1))
        small["g_norm"].insert(0, jnp.stack(gnorm_l))
        ls = mg["lru_sums"]
        small["b_rgate"].insert(0, ls[0])
        small["b_igate"].insert(0, ls[1])
        small["lru_lambda"].insert(0, ls[2] * (-_sigmoid(-wts["lru_lambda"][l])))
        small["conv_b"].insert(0, ls[3])
        small["conv_w"].insert(0, ls[4:8])
        small["w_rgate"].insert(0, _block_diag_grad(mg["dwr"]))
        small["w_igate"].insert(0, _block_diag_grad(mg["dwi"]))
        small["g_mix_out"].insert(0, jnp.sum(mg["gmix_parts"][:, 0, :], axis=0))
        gqk = jnp.sum(mg["gqk_parts"][:, :2, :], axis=0).reshape(2, ATT_W // HEAD_DIM, HEAD_DIM).sum(axis=1)
        small["g_qk"].insert(0, gqk)
        small["b_fgate"].insert(0, jnp.sum(mg["bf_parts"][:, 0, :4], axis=0))
    small = {k: jnp.stack(v) for k, v in small.items()}
    return loss, dx.reshape(batch, seq, d), handles, small


def _row_tile(rows, row_bytes):
    for t in (512, 256, 128, 64, 32, 16):
        if rows % t == 0 and t * row_bytes <= 4 * 1024 * 1024:
            return t
    return rows


def _adamw(parts, w, m, v, name):
    groups, n_parts, rows, cols = parts.shape
    tr = _row_tile(rows, cols * (n_parts * parts.dtype.itemsize + 7 * 4))
    c1 = 1.0 - ADAM_B1 ** ADAM_STEP
    c2 = 1.0 - ADAM_B2 ** ADAM_STEP

    def body(p_ref, w_ref, m_ref, v_ref, g_out, d_out, m_out, v_out):
        g = p_ref[0].astype(F32)
        for n in range(1, n_parts):
            g = g + p_ref[n].astype(F32)
        m_new = ADAM_B1 * m_ref[...] + (1.0 - ADAM_B1) * g
        v_new = ADAM_B2 * v_ref[...] + (1.0 - ADAM_B2) * (g * g)
        g_out[...] = g
        d_out[...] = -ADAM_LR * ((m_new / c1) / (jnp.sqrt(v_new / c2) + ADAM_EPS) + ADAM_WD * w_ref[...])
        m_out[...] = m_new
        v_out[...] = v_new

    tile = pl.BlockSpec((None, tr, cols), lambda g, i: (g, i, 0))
    return pl.pallas_call(
        body, name=name, grid=(groups, rows // tr),
        in_specs=[pl.BlockSpec((None, n_parts, tr, cols), lambda g, i: (g, 0, i, 0)), tile, tile, tile],
        out_specs=[tile] * 4, out_shape=[jax.ShapeDtypeStruct((groups, rows, cols), F32)] * 4,
        compiler_params=_params(2),
    )(parts, w, m, v)


def _sum_parts(parts):
    n_parts, rows, cols = parts.shape

    def body(p_ref, o_ref):
        acc = p_ref[0]
        for n in range(1, n_parts):
            acc = acc + p_ref[n]
        o_ref[...] = acc

    return pl.pallas_call(body, name="sum_small", out_shape=jax.ShapeDtypeStruct((rows, cols), F32),
                          compiler_params=pltpu.CompilerParams(vmem_limit_bytes=VMEM_LIMIT_BYTES))(parts)


def _flatten(arrays, multiple):
    flat = jnp.concatenate([a.reshape(-1).astype(F32) for a in arrays])
    pad = (-flat.shape[0]) % multiple
    return jnp.pad(flat, (0, pad)).reshape(-1, LANES)


def _unflatten(flat2d, shapes):
    flat, out, off = flat2d.reshape(-1), [], 0
    for s in shapes:
        n = math.prod(s)
        out.append(flat[off:off + n].reshape(s))
        off += n
    return out


SMALL_NAMES = ("b_ada", "g_norm", "b_fgate", "conv_w", "conv_b", "w_rgate", "b_rgate", "w_igate", "b_igate",
               "lru_lambda", "g_qk", "g_mix_out")
WEIGHT_NAMES = ("w_ada", "b_ada", "g_norm", "w_ffn_up", "w_ffn_down", "w_in", "b_fgate", "conv_w", "conv_b",
                "w_rgate", "b_rgate", "w_igate", "b_igate", "lru_lambda", "g_qk", "g_mix_out", "w_out")


def kernel(x, c, w_ada, b_ada, g_norm, w_ffn_up, w_ffn_down, w_in, b_fgate, conv_w, conv_b, w_rgate, b_rgate, w_igate, b_igate, lru_lambda, g_qk, g_mix_out, w_out, loss_target, m_w_ada, m_b_ada, m_g_norm, m_w_ffn_up, m_w_ffn_down, m_w_in, m_b_fgate, m_conv_w, m_conv_b, m_w_rgate, m_b_rgate, m_w_igate, m_b_igate, m_lru_lambda, m_g_qk, m_g_mix_out, m_w_out, v_w_ada, v_b_ada, v_g_norm, v_w_ffn_up, v_w_ffn_down, v_w_in, v_b_fgate, v_conv_w, v_conv_b, v_w_rgate, v_b_rgate, v_w_igate, v_b_igate, v_lru_lambda, v_g_qk, v_g_mix_out, v_w_out):
    batch, seq, d = x.shape
    n_layers = w_ada.shape[0]
    me = 4 * lax.axis_index("x") + 2 * lax.axis_index("y") + lax.axis_index("c")
    weights = dict(w_ada=w_ada, b_ada=b_ada, g_norm=g_norm, w_ffn_up=w_ffn_up, w_ffn_down=w_ffn_down, w_in=w_in,
                   b_fgate=b_fgate, conv_w=conv_w, conv_b=conv_b, w_rgate=w_rgate, b_rgate=b_rgate, w_igate=w_igate,
                   b_igate=b_igate, lru_lambda=lru_lambda, g_qk=g_qk, g_mix_out=g_mix_out, w_out=w_out)
    moments_m = dict(w_ada=m_w_ada, b_ada=m_b_ada, g_norm=m_g_norm, w_ffn_up=m_w_ffn_up, w_ffn_down=m_w_ffn_down,
                     w_in=m_w_in, b_fgate=m_b_fgate, conv_w=m_conv_w, conv_b=m_conv_b, w_rgate=m_w_rgate,
                     b_rgate=m_b_rgate, w_igate=m_w_igate, b_igate=m_b_igate, lru_lambda=m_lru_lambda, g_qk=m_g_qk,
                     g_mix_out=m_g_mix_out, w_out=m_w_out)
    moments_v = dict(w_ada=v_w_ada, b_ada=v_b_ada, g_norm=v_g_norm, w_ffn_up=v_w_ffn_up, w_ffn_down=v_w_ffn_down,
                     w_in=v_w_in, b_fgate=v_b_fgate, conv_w=v_conv_w, conv_b=v_conv_b, w_rgate=v_w_rgate,
                     b_rgate=v_b_rgate, w_igate=v_w_igate, b_igate=v_b_igate, lru_lambda=v_lru_lambda, g_qk=v_g_qk,
                     g_mix_out=v_g_mix_out, w_out=v_w_out)

    c_all, gn_all, cw_all = _exchange([(c, 0), (g_norm, 0), (conv_w, 0)], [], "gather_small_weights")
    c_all = c_all.reshape(N_DEV * batch, d)
    n_ada = w_ada.shape[-1]
    g_norm_full = gn_all.transpose(1, 2, 0, 3).reshape(n_layers, 3, d)
    conv_w_full = cw_all.transpose(1, 2, 0, 3).reshape(n_layers, 4, LRU_W)

    b_ada_loc = lax.dynamic_slice_in_dim(b_ada, me * n_ada, n_ada, axis=1)
    silu = lambda t: t * _sigmoid(t)

    def bias_epilogue(p, e_refs, o_refs):
        o_refs[0][...] = p + e_refs[0][...]

    mod_loc = []
    for l in range(n_layers):
        (ml,) = _mm(c_all, w_ada, mode="nn", tm=c_all.shape[0], tn=n_ada, tk=d, b_lead=(l,), a_pre=silu,
                    name=f"ada_{l}", extras=[(b_ada_loc[l][None, :], (1, n_ada), lambda i, j: (0, 0))],
                    outs=[((c_all.shape[0], n_ada), F32, (c_all.shape[0], n_ada), lambda i, j: (0, 0))],
                    epilogue=bias_epilogue)
        mod_loc.append(ml)
    (mod_all,) = _exchange([(jnp.stack(mod_loc), 0)], [], "gather_mod")
    mod_all = mod_all.transpose(1, 2, 0, 3).reshape(n_layers, N_DEV * batch, 9 * d)

    w_in_pad = jnp.pad(w_in, ((0, 0), (0, 0), (0, N_IN_PAD - N_IN))).astype(BF16)
    up16, down16, out16 = w_ffn_up.astype(BF16), w_ffn_down.astype(BF16), w_out.astype(BF16)
    ffn_ops = lambda l, f: [(up16[l, f], 0, "gather"), (down16[l, f], 0, "gather")]
    mix_ops = lambda l: [(w_in_pad[l], 0, "gather"), (out16[l], 0, "gather")]
    tie = lambda t, dep: lax.optimization_barrier((t, dep))[0]
    flights, landed_rest = {}, []

    def start(key, ops, dep):
        flights[key], token = _flight_start([(tie(a, dep), nl, kind) for a, nl, kind in ops], f"weights_{key}_start")
        return token

    def wait(key, after):
        return _flight_wait(flights[key], after, f"weights_{key}_wait")

    def big_weights(l, part, after):
        if (l, part) == (0, "ffn0"):
            (wup,) = wait("up", after)
            token = start("down", ffn_ops(0, 0)[1:], wup)

            def wdown(after_up):
                (wd,) = wait("down", after_up)
                return tie(wd, start("mix", mix_ops(0), wd)).reshape(D_FF, d)

            return tie(wup, token), wdown
        if (l, part) == (0, "mix"):
            wi, wo = wait("mix", after)
            rest = ffn_ops(0, 1)
            for ll in range(1, n_layers):
                rest += ffn_ops(ll, 0) + mix_ops(ll) + ffn_ops(ll, 1)
            return tie(wi, start("rest", rest, wi)).reshape(d, N_IN_PAD), wo.reshape(d, d)
        if not landed_rest:
            landed_rest.extend(wait("rest", after))
        at = 0 if l == 0 else 2 + 6 * (l - 1) + {"ffn0": 0, "mix": 2, "ffn1": 4}[part]
        first, second = landed_rest[at], landed_rest[at + 1]
        if part == "mix":
            return first.reshape(d, N_IN_PAD), second.reshape(d, d)
        return first, second.reshape(D_FF, d)

    token = start("up", ffn_ops(0, 0)[:1], mod_all)
    mod_me = lax.dynamic_slice_in_dim(mod_all + token[0, 0], me * batch, batch, axis=1)
    mod_me = mod_me.reshape(n_layers, batch, 3, 3, d)

    wts = dict(g_norm=g_norm_full, conv_w=conv_w_full, conv_b=conv_b, w_rgate=w_rgate, b_rgate=b_rgate, w_igate=w_igate, b_igate=b_igate,
               lru_lambda=lru_lambda, g_qk=g_qk, g_mix_out=g_mix_out, b_fgate=b_fgate)
    loss_part, grad_x, handles, small = _local_step(x, loss_target, mod_me, wts, big_weights)

    dmod_me = small.pop("dmod").reshape(n_layers, batch, 9 * d)
    small["b_ada"] = jnp.sum(dmod_me, axis=1)
    small_shapes = [(1,)] + [weights[k].shape if k not in ("g_norm", "conv_w") else small[k].shape for k in SMALL_NAMES]
    small_flat = _flatten([loss_part.reshape(1)] + [small[k] for k in SMALL_NAMES], 16 * LANES)
    dmod_all, small_all = _exchange([(dmod_me, 0), (small_flat, 0)], [], "gather_small")
    landed = {key: _flight_wait(h, small_all, f"grads_{key[0]}_{key[1]}_wait") for key, h in handles.items()}
    layer_range = range(n_layers)
    p_up = jnp.stack([jnp.stack([landed[(l, "ffn0")][0], landed[(l, "ffn1")][0]]) for l in layer_range])
    p_down = jnp.stack([jnp.stack([landed[(l, "ffn0")][1], landed[(l, "ffn1")][1]]) for l in layer_range])
    p_in = jnp.stack([landed[(l, "mix")][0] for l in layer_range])
    p_out = jnp.stack([landed[(l, "mix")][1] for l in layer_range])
    small_sum = _unflatten(_sum_parts(small_all), small_shapes)
    loss = small_sum[0].reshape(())
    small_grads = dict(zip(SMALL_NAMES, small_sum[1:]))
    small_grads["g_norm"] = lax.dynamic_slice_in_dim(small_grads["g_norm"], me * g_norm.shape[-1], g_norm.shape[-1], 2)
    small_grads["conv_w"] = lax.dynamic_slice_in_dim(small_grads["conv_w"], me * conv_w.shape[-1], conv_w.shape[-1], 2)

    dmod_all = dmod_all.transpose(1, 0, 2, 3).reshape(n_layers, N_DEV * batch, 9 * d)
    dmod_loc = lax.dynamic_slice_in_dim(dmod_all, me * n_ada, n_ada, axis=2)
    g_ada = []
    for l in range(n_layers):
        (gl,) = _mm(c_all, dmod_loc[l], mode="tn", tm=d, tn=n_ada, tk=c_all.shape[0], a_pre=silu, name=f"dw_ada_{l}",
                    outs=[((d, n_ada), F32, (d, n_ada), lambda i, j: (0, 0))], epilogue=_store_epilogue([F32]))
        g_ada.append(gl)
    g_ada = jnp.stack(g_ada)

    results = {}

    def update(name, parts):
        shape = weights[name].shape
        as3d = lambda t: t.reshape((-1,) + shape[-2:])
        outs = _adamw(parts.reshape((-1,) + parts.shape[-3:]), as3d(weights[name]), as3d(moments_m[name]),
                      as3d(moments_v[name]), f"adamw_{name}")
        results[name] = [o.reshape(shape) for o in outs]

    update("w_ada", g_ada[:, None])
    update("w_ffn_up", p_up)
    update("w_ffn_down", p_down)
    update("w_in", p_in[..., :N_IN])
    update("w_out", p_out)
    sm_shapes = [weights[k].shape for k in SMALL_NAMES]
    flat = lambda src: _flatten([src[k] for k in SMALL_NAMES], 16 * LANES)
    sm_out = _adamw(flat(small_grads)[None, None], flat(weights)[None], flat(moments_m)[None], flat(moments_v)[None],
                    "adamw_small")
    for k, vals in zip(SMALL_NAMES, zip(*[_unflatten(o, sm_shapes) for o in sm_out])):
        results[k] = list(vals)

    outs = [loss, grad_x]
    for n in range(4):
        outs += [results[k][n] for k in WEIGHT_NAMES]
    return tuple(outs)
```

```python
import functools
import math

import jax
import jax.numpy as jnp
from jax import lax
from jax.experimental import pallas as pl
from jax.experimental.pallas import tpu as pltpu

F32 = jnp.float32
BF16 = jnp.bfloat16

N_DEV = 8
D_MODEL = 1024
D_FF = 2816
FF_SHARD = 2 * D_FF // N_DEV
N_FF_SHARD = D_FF // FF_SHARD
HEAD_DIM = 64
LRU_W = 512
ATT_W = 256
N_IN = 2564
N_IN_PAD = 2688
LANES = 128
SUBLANES = 8
BLK = 256
TQ = 512
KB_PER_Q = TQ // BLK
EPS = 1e-6
LRU_C = 8.0
NEG_BIG = -1e30
VMEM_LIMIT_BYTES = 48 * 1024 * 1024

ADAM_LR, ADAM_B1, ADAM_B2, ADAM_EPS, ADAM_WD, ADAM_STEP = 0.001, 0.9, 0.999, 1e-08, 0.01, 10

COL_SBQ, COL_SBK, COL_SBV = 8, 10, 12
COL_FXV, COL_FXF = 18, 20

NN = (((1,), (0,)), ((), ()))
NT = (((1,), (1,)), ((), ()))
TN = (((0,), (0,)), ((), ()))


def _params(n_axes):
    return pltpu.CompilerParams(dimension_semantics=("arbitrary",) * n_axes, vmem_limit_bytes=VMEM_LIMIT_BYTES)


def _tok_tile(seq):
    for t in (512, 256, 128):
        if seq % t == 0:
            return t
    raise ValueError(f"sequence length {seq} is not a multiple of 128")


def _dot(a, b, dims=NN):
    return lax.dot_general(a, b, dims, preferred_element_type=F32)


def _sigmoid(x):
    return 1.0 / (1.0 + jnp.exp(-x))


def _softplus(x):
    return jnp.maximum(x, 0.0) + jnp.log(1.0 + jnp.exp(-jnp.abs(x)))


def _gelu_parts(x):
    k0, k1 = math.sqrt(2.0 / math.pi), 0.044715
    t = jnp.tanh(k0 * (x + k1 * x * x * x))
    gelu = 0.5 * x * (1.0 + t)
    dgelu = 0.5 * (1.0 + t) + 0.5 * x * (1.0 - t * t) * k0 * (1.0 + 3.0 * k1 * x * x)
    return gelu, dgelu


def _neg_expm1(x):
    series = -x * (1.0 + x * (0.5 + x * (1.0 / 6.0 + x * (1.0 / 24.0 + x * (1.0 / 120.0 + x * (1.0 / 720.0))))))
    return jnp.where(x > -0.25, series, 1.0 - jnp.exp(x))


def _split2(x):
    hi = x.astype(BF16)
    lo = (x - hi.astype(F32)).astype(BF16)
    return hi, lo


def _split3(x):
    hi = x.astype(BF16)
    r = x - hi.astype(F32)
    mid = r.astype(BF16)
    lo = (r - mid.astype(F32)).astype(BF16)
    return hi, mid, lo


def _rows_to_block(rows, width):
    r = lax.broadcasted_iota(jnp.int32, (SUBLANES, width), 0)
    out = jnp.zeros((SUBLANES, width), F32)
    for n, v in enumerate(rows):
        out = jnp.where(r == n, jnp.broadcast_to(v, (SUBLANES, width)), out)
    return out


def _colsum(x):
    return jnp.sum(x, axis=0, keepdims=True)


def _exchange(gathers, scatters, name, two_level=False):
    assert not (two_level and scatters)
    n_g = len(gathers)
    ops = [a for a, _ in gathers] + [a for a, _ in scatters]
    n = len(ops)
    out_shape = [jax.ShapeDtypeStruct(a.shape[:nl] + (N_DEV,) + a.shape[nl:], a.dtype) for a, nl in gathers]
    out_shape += [jax.ShapeDtypeStruct(a.shape, a.dtype) for a, _ in scatters]
    items = []
    for k, (a, nl) in enumerate(list(gathers) + list(scatters)):
        for flat in range(math.prod(a.shape[:nl])):
            idx, rem = [], flat
            for dim in reversed(a.shape[:nl]):
                idx.insert(0, rem % dim)
                rem //= dim
            items.append((k, tuple(idx)))
    n_items = len(items)

    def body(*refs):
        ins, outs = refs[:n], refs[n:2 * n]
        send_sems, recv_sems, local_sems = refs[2 * n:]
        x, y, c = lax.axis_index("x"), lax.axis_index("y"), lax.axis_index("c")
        me = 4 * x + 2 * y + c

        def at(ref, idx):
            return ref.at[idx] if idx else ref

        def src(it, peer):
            k, idx = items[it]
            return at(ins[k], idx) if k < n_g else at(ins[k], idx + (peer,))

        def slot(it, s):
            k, idx = items[it]
            return at(outs[k], idx + (s,))

        def remote(it, rel, source, s, to):
            return pltpu.make_async_remote_copy(
                src_ref=source, dst_ref=slot(it, s), send_sem=send_sems.at[it, rel], recv_sem=recv_sems.at[it, rel],
                device_id=to, device_id_type=pl.DeviceIdType.MESH)

        local = [pltpu.make_async_copy(src(it, me), slot(it, me), local_sems.at[it]) for it in range(n_items)]
        for cp in local:
            cp.start()

        if not two_level:
            started = []
            for r in range(1, N_DEV):
                px = 1 - x if (r >> 2) & 1 else x
                py = 1 - y if (r >> 1) & 1 else y
                pc = 1 - c if r & 1 else c
                for it in range(n_items):
                    cp = remote(it, r - 1, src(it, 4 * px + 2 * py + pc), me, (px, py, pc))
                    cp.start()
                    started.append(cp)
            for cp in started:
                cp.wait()
        else:
            sibling, chips = (x, y, 1 - c), [(1 - x, y), (x, 1 - y), (1 - x, 1 - y)]
            sib = 4 * x + 2 * y + (1 - c)
            started = []
            for it in range(n_items):
                started.append(remote(it, 0, src(it, me), me, sibling))
                started += [remote(it, 1 + j, src(it, me), me, (cx, cy, c)) for j, (cx, cy) in enumerate(chips)]
            for cp in started:
                cp.start()
            for j, (cx, cy) in enumerate(chips):
                s = 4 * cx + 2 * cy + c
                for it in range(n_items):
                    remote(it, 1 + j, slot(it, s), s, sibling).wait_recv()
                    cp = remote(it, 4 + j, slot(it, s), s, sibling)
                    cp.start()
                    started.append(cp)
            for it in range(n_items):
                remote(it, 0, slot(it, sib), sib, sibling).wait_recv()
                for j, (cx, cy) in enumerate(chips):
                    s = 4 * cx + 2 * cy + (1 - c)
                    remote(it, 4 + j, slot(it, s), s, sibling).wait_recv()
            for cp in started:
                cp.wait_send()
        for cp in local:
            cp.wait()

    hbm = pl.BlockSpec(memory_space=pltpu.HBM)
    return pl.pallas_call(
        body, name=name, out_shape=out_shape,
        in_specs=[hbm] * n, out_specs=[hbm] * n,
        scratch_shapes=[pltpu.SemaphoreType.DMA((n_items, N_DEV - 1)), pltpu.SemaphoreType.DMA((n_items, N_DEV - 1)),
                        pltpu.SemaphoreType.DMA((n_items,))],
    )(*ops)


def _lead_items(ops):
    items = []
    for k, (a, nl) in enumerate(ops):
        for flat in range(math.prod(a.shape[:nl])):
            idx, rem = [], flat
            for dim in reversed(a.shape[:nl]):
                idx.insert(0, rem % dim)
                rem //= dim
            items.append((k, tuple(idx)))
    return items


def _flight_copies(ops, srcs, lands, send_sems, recv_sems):
    x, y, c = lax.axis_index("x"), lax.axis_index("y"), lax.axis_index("c")
    me = 4 * x + 2 * y + c
    copies = []
    for r in range(1, N_DEV):
        px = 1 - x if (r >> 2) & 1 else x
        py = 1 - y if (r >> 1) & 1 else y
        pc = 1 - c if r & 1 else c
        for it, (k, idx) in enumerate(_lead_items([(a, nl) for a, nl, _ in ops])):
            src = srcs[k].at[idx + (4 * px + 2 * py + pc,)] if ops[k][2] == "scatter" else (
                srcs[k].at[idx] if idx else srcs[k])
            copies.append(pltpu.make_async_remote_copy(
                src_ref=src, dst_ref=lands[k].at[idx + (me,)],
                send_sem=send_sems.at[it * (N_DEV - 1) + r - 1], recv_sem=recv_sems.at[it * (N_DEV - 1) + r - 1],
                device_id=(px, py, pc), device_id_type=pl.DeviceIdType.MESH))
    return copies


def _flight_start(ops, name):
    n = len(ops)
    me = 4 * lax.axis_index("x") + 2 * lax.axis_index("y") + lax.axis_index("c")
    srcs, lands = [], []
    for a, nl, kind in ops:
        if kind == "scatter":
            own, shape = lax.dynamic_slice_in_dim(a, me, 1, axis=nl), a.shape
        else:
            own, shape = jnp.expand_dims(a, nl), a.shape[:nl] + (N_DEV,) + a.shape[nl:]
        start = (0,) * nl + (me,) + (0,) * (len(shape) - nl - 1)
        lands.append(pltpu.with_memory_space_constraint(
            lax.dynamic_update_slice(lax.empty(shape, a.dtype), own, start), pltpu.HBM))
        srcs.append(pltpu.with_memory_space_constraint(a, pltpu.HBM))

    def body(*refs):
        for cp in _flight_copies(ops, refs[:n], refs[n:2 * n], refs[2 * n], refs[2 * n + 1]):
            cp.start()
        refs[-1][...] = jnp.zeros_like(refs[-1])

    hbm, sem = pl.BlockSpec(memory_space=pltpu.HBM), pl.BlockSpec(memory_space=pltpu.SEMAPHORE)
    n_items = len(_lead_items([(a, nl) for a, nl, _ in ops]))
    sems = pltpu.SemaphoreType.DMA((n_items * (N_DEV - 1),))
    res = pl.pallas_call(
        body, name=name,
        out_shape=[sems, sems] + [pltpu.HBM(a.shape, a.dtype) for a in srcs + lands]
        + [jax.ShapeDtypeStruct((SUBLANES, LANES), F32)],
        in_specs=[hbm] * (2 * n), out_specs=[sem, sem] + [hbm] * (2 * n) + [pl.BlockSpec(memory_space=pltpu.VMEM)],
        input_output_aliases={i: 2 + i for i in range(2 * n)},
        compiler_params=pltpu.CompilerParams(has_side_effects=pltpu.SideEffectType.DATAFLOW_SIDE_EFFECTING),
    )(*srcs, *lands)
    return (ops, res[0], res[1], res[2:2 + n], res[2 + n:2 + 2 * n]), res[-1]


def _flight_wait(handle, after, name):
    ops, send_sems, recv_sems, srcs, lands = handle
    n = len(ops)

    def body(*refs):
        for cp in _flight_copies(ops, refs[:n], refs[n:2 * n], refs[2 * n], refs[2 * n + 1]):
            cp.wait_send()
            cp.wait_recv()
        refs[-1][...] = jnp.zeros_like(refs[-1])

    hbm, sem = pl.BlockSpec(memory_space=pltpu.HBM), pl.BlockSpec(memory_space=pltpu.SEMAPHORE)
    res = pl.pallas_call(
        body, name=name,
        out_shape=[pltpu.HBM(a.shape, a.dtype) for a in list(srcs) + list(lands)]
        + [jax.ShapeDtypeStruct((SUBLANES, LANES), F32)],
        in_specs=[hbm] * (2 * n) + [sem, sem, pl.BlockSpec(memory_space=pl.ANY)],
        out_specs=[hbm] * (2 * n) + [pl.BlockSpec(memory_space=pltpu.VMEM)],
        input_output_aliases={i: i for i in range(2 * n)},
        compiler_params=pltpu.CompilerParams(has_side_effects=pltpu.SideEffectType.DATAFLOW_SIDE_EFFECTING),
    )(*srcs, *lands, send_sems, recv_sems, after)
    return res[n:2 * n], res[-1]


def _mm(a, b, *, mode, tm, tn, tk, outs, epilogue, name, extras=(), a_lead=(), b_lead=(), a_pre=None,
        a_spec=None, b_spec=None, shape=None, ksub=1):
    if shape is not None:
        mdim, ndim, kdim = shape
    else:
        if mode == "tn":
            kdim, mdim = a.shape[-2:]
        else:
            mdim, kdim = a.shape[-2:]
        ndim = b.shape[-2] if mode == "nt" else b.shape[-1]
    assert mdim % tm == 0 and ndim % tn == 0 and kdim % tk == 0, (name, mdim, ndim, kdim, tm, tn, tk)
    ni, nj, nk = mdim // tm, ndim // tn, kdim // tk
    a_lead, b_lead = tuple(a_lead), tuple(b_lead)
    a_block = (None,) * len(a_lead) + ((tk, tm) if mode == "tn" else (tm, tk))
    b_block = (None,) * len(b_lead) + ((tn, tk) if mode == "nt" else (tk, tn))
    dims = {"nn": NN, "nt": NT, "tn": TN}[mode]
    ne, no = len(extras), len(outs)

    def a_index(i, j, k):
        return a_lead + ((k, i) if mode == "tn" else (i, k))

    def b_index(i, j, k):
        return b_lead + ((j, k) if mode == "nt" else (k, j))

    if a_spec is not None:
        a_block, a_index = a_spec
    if b_spec is not None:
        b_block, b_index = b_spec

    def body(*refs):
        a_ref, b_ref = refs[0], refs[1]
        e_refs, o_refs = refs[2:2 + ne], refs[2 + ne:2 + ne + no]
        if ksub == 1:
            av = a_ref[...] if a_pre is None else a_pre(a_ref[...])
            p = _dot(av.astype(BF16), b_ref[...].astype(BF16), dims)
        else:
            p = _dot(a_ref[0], b_ref[0], dims)
            for s in range(1, ksub):
                p = p + _dot(a_ref[s], b_ref[s], dims)
        if nk == 1:
            epilogue(p, e_refs, o_refs)
        else:
            acc = refs[-1]
            k = pl.program_id(2)

            @pl.when(k == 0)
            def _():
                acc[...] = p

            @pl.when(k > 0)
            def _():
                acc[...] += p

            @pl.when(k == nk - 1)
            def _():
                epilogue(acc[...], e_refs, o_refs)

    in_specs = [pl.BlockSpec(a_block, a_index), pl.BlockSpec(b_block, b_index)]
    in_specs += [pl.BlockSpec(blk, functools.partial(lambda i, j, k, f: f(i, j), f=f)) for _, blk, f in extras]
    out_specs = [pl.BlockSpec(blk, functools.partial(lambda i, j, k, f: f(i, j), f=f)) for _, _, blk, f in outs]
    res = pl.pallas_call(
        body, name=name, grid=(ni, nj, nk), in_specs=in_specs, out_specs=out_specs,
        out_shape=[jax.ShapeDtypeStruct(s, d) for s, d, _, _ in outs],
        scratch_shapes=[pltpu.VMEM((tm, tn), F32)] if nk > 1 else [],
        compiler_params=_params(3),
    )(a, b, *[e[0] for e in extras])
    return res


def _store_epilogue(dtypes):
    def epi(p, e_refs, o_refs):
        for o, dt in zip(o_refs, dtypes):
            o[...] = p.astype(dt)
    return epi


def _normmod(x, gn, scale, shift, seq, name):
    m, d = x.shape
    tm = _tok_tile(seq)
    tpb = seq // tm

    def body(x_ref, gn_ref, sc_ref, sh_ref, h_ref):
        xv = x_ref[...]
        rstd = lax.rsqrt(jnp.mean(xv * xv, axis=-1, keepdims=True) + EPS)
        h_ref[...] = (xv * rstd * gn_ref[...] * (1.0 + sc_ref[0]) + sh_ref[0]).astype(BF16)

    vec = pl.BlockSpec((1, 1, d), lambda i: (i // tpb, 0, 0))
    return pl.pallas_call(
        body, name=name, grid=(m // tm,),
        in_specs=[pl.BlockSpec((tm, d), lambda i: (i, 0)), pl.BlockSpec((1, d), lambda i: (0, 0)), vec, vec],
        out_specs=pl.BlockSpec((tm, d), lambda i: (i, 0)),
        out_shape=jax.ShapeDtypeStruct((m, d), BF16), compiler_params=_params(1),
    )(x, gn, scale, shift)


def _normmod_bwd_epilogue(p, e_refs, o_refs):
    x_ref, dxo_ref, gn_ref, sc_ref = e_refs
    xv = x_ref[...]
    rstd = lax.rsqrt(jnp.mean(xv * xv, axis=-1, keepdims=True) + EPS)
    xhat = xv * rstd
    gn, sc1 = gn_ref[...], 1.0 + sc_ref[0]
    dxhat = p * (gn * sc1)
    dx = rstd * (dxhat - xhat * jnp.mean(dxhat * xhat, axis=-1, keepdims=True))
    o_refs[0][...] = dxo_ref[...] + dx
    t = p * xhat
    o_refs[1][0] = _rows_to_block([_colsum(p), _colsum(t * gn), _colsum(t * sc1)], p.shape[1])


def _residual_bwd(dx, f, gate, fac, seq, name):
    m, d = dx.shape
    tm = _tok_tile(seq)
    tpb = seq // tm

    def body(dx_ref, f_ref, g_ref, df_ref, dg_ref):
        dxv = dx_ref[...]
        df_ref[...] = ((fac * (1.0 + g_ref[0])) * dxv).astype(BF16)
        dg_ref[0] = _rows_to_block([_colsum((fac * dxv) * f_ref[...].astype(F32))], d)

    tile = pl.BlockSpec((tm, d), lambda i: (i, 0))
    return pl.pallas_call(
        body, name=name, grid=(m // tm,),
        in_specs=[tile, tile, pl.BlockSpec((1, 1, d), lambda i: (i // tpb, 0, 0))],
        out_specs=[tile, pl.BlockSpec((1, SUBLANES, d), lambda i: (i, 0, 0))],
        out_shape=[jax.ShapeDtypeStruct((m, d), BF16), jax.ShapeDtypeStruct((m // tm, SUBLANES, d), F32)],
        compiler_params=_params(1),
    )(dx, f, gate)


def _loss_head(y, target, seq):
    m, d = y.shape
    tm = _tok_tile(seq)

    def body(y_ref, t_ref, dy_ref, l_ref):
        err = y_ref[...] - t_ref[...]
        dy_ref[...] = err * (1.0 / d)
        part = 0.5 * jnp.sum(jnp.mean(err * err, axis=-1, keepdims=True), axis=0, keepdims=True)
        l_ref[0] = jnp.broadcast_to(part, (SUBLANES, LANES))

    tile = pl.BlockSpec((tm, d), lambda i: (i, 0))
    return pl.pallas_call(
        body, name="loss_head", grid=(m // tm,), in_specs=[tile, tile],
        out_specs=[tile, pl.BlockSpec((1, SUBLANES, LANES), lambda i: (i, 0, 0))],
        out_shape=[jax.ShapeDtypeStruct((m, d), F32), jax.ShapeDtypeStruct((m // tm, SUBLANES, LANES), F32)],
        compiler_params=_params(1),
    )(y, target)


def _ffn_fwd(x, h, wup, wdown, gate, seq, tag):
    m, d = x.shape
    tm = _tok_tile(seq)
    tpb = seq // tm

    def up_body(h_ref, wg_ref, wu_ref, a_ref, gu_ref):
        hv = h_ref[...]
        g, u = _dot(hv, wg_ref[...]), _dot(hv, wu_ref[...])
        a_ref[...] = (g * _sigmoid(g) * u).astype(BF16)
        gu_ref[0] = g.astype(BF16)
        gu_ref[1] = u.astype(BF16)

    wblk = (None, d, FF_SHARD)
    a, gu = pl.pallas_call(
        up_body, name=f"ffn_up_{tag}", grid=(N_FF_SHARD, m // tm),
        in_specs=[pl.BlockSpec((tm, d), lambda j, i: (i, 0)),
                  pl.BlockSpec(wblk, lambda j, i: (j, 0, 0)),
                  pl.BlockSpec(wblk, lambda j, i: (j + N_FF_SHARD, 0, 0))],
        out_specs=[pl.BlockSpec((None, tm, FF_SHARD), lambda j, i: (j, i, 0)),
                   pl.BlockSpec((2, None, tm, FF_SHARD), lambda j, i: (0, j, i, 0))],
        out_shape=[jax.ShapeDtypeStruct((N_FF_SHARD, m, FF_SHARD), BF16),
                   jax.ShapeDtypeStruct((2, N_FF_SHARD, m, FF_SHARD), BF16)],
        compiler_params=_params(2),
    )(h, wup, wup)

    def down_epilogue(p, e_refs, o_refs):
        x_ref, g_ref = e_refs
        o_refs[0][...] = x_ref[...] + (0.5 * (1.0 + g_ref[0])) * p
        o_refs[1][...] = p.astype(BF16)

    if callable(wdown):
        wdown = wdown(a)
    wdown3 = wdown.reshape(N_FF_SHARD, FF_SHARD, d)
    x_out, f = _mm(a, wdown3, mode="nn", tm=tm, tn=d, tk=D_FF, ksub=N_FF_SHARD, name=f"ffn_down_{tag}",
                   shape=(m, d, D_FF), a_spec=((N_FF_SHARD, tm, FF_SHARD), lambda i, j, k: (0, i, 0)),
                   b_spec=((N_FF_SHARD, FF_SHARD, d), lambda i, j, k: (0, 0, 0)),
                   extras=[(x, (tm, d), lambda i, j: (i, 0)), (gate, (1, 1, d), lambda i, j: (i // tpb, 0, 0))],
                   outs=[((m, d), F32, (tm, d), lambda i, j: (i, 0)), ((m, d), BF16, (tm, d), lambda i, j: (i, 0))],
                   epilogue=down_epilogue)
    return x_out, (a, gu, f), wdown


def _ffn_bwd(dx_out, x, h, saved, wup, wdown, gn, scale, gate, seq, tag, on_grads):
    a, gu, f = saved
    m, d = x.shape
    tm = _tok_tile(seq)
    tpb = seq // tm
    df, dgate_parts = _residual_bwd(dx_out, f, gate, 0.5, seq, f"ffn_res_bwd_{tag}")

    def act_bwd_epilogue(p, e_refs, o_refs):
        g, u = e_refs[0][0].astype(F32), e_refs[0][1].astype(F32)
        sg = _sigmoid(g)
        o_refs[0][0] = (p * u * (sg * (1.0 + g * (1.0 - sg)))).astype(BF16)
        o_refs[0][1] = (p * (g * sg)).astype(BF16)

    gu_blk = (2, None, tm, FF_SHARD)
    (dgu,) = _mm(df, wdown, mode="nt", tm=tm, tn=FF_SHARD, tk=d, name=f"ffn_down_dx_{tag}", shape=(m, D_FF, d),
                 b_spec=((FF_SHARD, d), lambda i, j, k: (j, 0)),
                 extras=[(gu, gu_blk, lambda i, j: (0, j, i, 0))],
                 outs=[((2, N_FF_SHARD, m, FF_SHARD), BF16, gu_blk, lambda i, j: (0, j, i, 0))],
                 epilogue=act_bwd_epilogue)
    tt = 2 * tm if m % (2 * tm) == 0 else tm
    (dwdown,) = _mm(a, df, mode="tn", tm=FF_SHARD, tn=d, tk=tt, name=f"ffn_dwdown_{tag}", shape=(D_FF, d, m),
                    a_spec=((None, tt, FF_SHARD), lambda i, j, k: (i, k, 0)),
                    outs=[((D_FF, d), BF16, (FF_SHARD, d), lambda i, j: (i, 0))], epilogue=_store_epilogue([BF16]))
    dgu8 = dgu.reshape(2 * N_FF_SHARD, m, FF_SHARD)
    (dwup,) = _mm(h, dgu8, mode="tn", tm=d, tn=FF_SHARD, tk=tt, name=f"ffn_dwup_{tag}", shape=(d, 2 * D_FF, m),
                  b_spec=((None, tt, FF_SHARD), lambda i, j, k: (j, k, 0)),
                  outs=[((2 * N_FF_SHARD, d, FF_SHARD), BF16, (None, d, FF_SHARD), lambda i, j: (j, 0, 0))],
                  epilogue=_store_epilogue([BF16]))
    scale = scale + on_grads(dwup, dwdown)[0, 0]
    dx, nm_parts = _mm(dgu8, wup, mode="nt", tm=tm, tn=d, tk=D_FF, ksub=N_FF_SHARD, name=f"ffn_up_dx_{tag}",
                       shape=(m, d, 2 * D_FF), a_spec=((N_FF_SHARD, tm, FF_SHARD), lambda i, j, k: (k, i, 0)),
                       b_spec=((N_FF_SHARD, d, FF_SHARD), lambda i, j, k: (k, 0, 0)),
                       extras=[(x, (tm, d), lambda i, j: (i, 0)), (dx_out, (tm, d), lambda i, j: (i, 0)),
                               (gn, (1, d), lambda i, j: (0, 0)), (scale, (1, 1, d), lambda i, j: (i // tpb, 0, 0))],
                       outs=[((m, d), F32, (tm, d), lambda i, j: (i, 0)),
                             ((m // tm, SUBLANES, d), F32, (1, SUBLANES, d), lambda i, j: (i, 0, 0))],
                       epilogue=_normmod_bwd_epilogue)
    return dx, nm_parts, dgate_parts


def _shift_down(ext, n, rows):
    if n:
        ext = pltpu.roll(ext, n, 0)
    return ext[SUBLANES:SUBLANES + rows]


def _lru_gates(u, wr_ref, br_ref, wi_ref, bi_ref, lam_ref):
    ub = u.astype(BF16)
    r = _sigmoid(_dot(ub, wr_ref[...]) + br_ref[...])
    ig = _sigmoid(_dot(ub, wi_ref[...]) + bi_ref[...])
    sp = _softplus(-lam_ref[...])
    log_a = (-LRU_C * r) * sp
    a = jnp.exp(log_a)
    mult = jnp.sqrt(_neg_expm1(2.0 * log_a))
    return r, ig, sp, a, mult


def _conv(ext, cw_ref, cb_ref, rows):
    u = cb_ref[...] + cw_ref[3:4, :] * _shift_down(ext, 0, rows)
    for k in range(3):
        u = u + cw_ref[k:k + 1, :] * _shift_down(ext, 3 - k, rows)
    return u


def _lru_halo_spec(seq, ts):
    return pl.BlockSpec((SUBLANES, LRU_W),
                        lambda b, i: (jnp.maximum(b * (seq // SUBLANES) + i * (ts // SUBLANES) - 1, 0), 0))


def _lru_fwd(proj32, conv_w, conv_b, wr, br, wi, bi, lam, batch, seq):
    m = proj32.shape[0]
    ts = _tok_tile(seq)
    nt = seq // ts
    row = lambda b, i: (b * nt + i, 0)

    def body(x_ref, halo_ref, g_ref, cw_ref, cb_ref, wr_ref, br_ref, wi_ref, bi_ref, lam_ref,
             y_ref, h_ref, a_scr, b_scr, carry):
        i = pl.program_id(1)
        halo = jnp.where(i > 0, halo_ref[...], 0.0)
        ext = jnp.concatenate([halo, x_ref[...]], axis=0)
        u = _conv(ext, cw_ref, cb_ref, ts)
        _, ig, _, a, mult = _lru_gates(u, wr_ref, br_ref, wi_ref, bi_ref, lam_ref)
        a_scr[...] = a
        b_scr[...] = mult * (ig * u)

        @pl.when(i == 0)
        def _():
            carry[...] = jnp.zeros_like(carry)

        rid = lax.broadcasted_iota(jnp.int32, (SUBLANES, LRU_W), 0)

        def chunk(c, hprev):
            off = pl.multiple_of(c * SUBLANES, SUBLANES)
            av, bv = a_scr[pl.ds(off, SUBLANES), :], b_scr[pl.ds(off, SUBLANES), :]
            for d in (1, 2, 4):
                keep = rid >= d
                bv = jnp.where(keep, av * pltpu.roll(bv, d, 0) + bv, bv)
                av = jnp.where(keep, av * pltpu.roll(av, d, 0), av)
            h = av * hprev + bv
            h_ref[pl.ds(off, SUBLANES), :] = h
            return h[SUBLANES - 1:SUBLANES, :]

        carry[...] = lax.fori_loop(0, ts // SUBLANES, chunk, carry[...])
        gelu, _ = _gelu_parts(g_ref[...])
        y_ref[...] = h_ref[...] * gelu

    full = lambda shape: pl.BlockSpec(shape, lambda b, i: (0,) * len(shape))
    return pl.pallas_call(
        body, name="lru_fwd", grid=(batch, nt),
        in_specs=[pl.BlockSpec((ts, LRU_W), row), _lru_halo_spec(seq, ts),
                  pl.BlockSpec((ts, LRU_W), lambda b, i: (b * nt + i, 1)),
                  full((4, LRU_W)), full((1, LRU_W)), full((LRU_W, LRU_W)), full((1, LRU_W)),
                  full((LRU_W, LRU_W)), full((1, LRU_W)), full((1, LRU_W))],
        out_specs=[pl.BlockSpec((ts, LRU_W), row), pl.BlockSpec((ts, LRU_W), row)],
        out_shape=[jax.ShapeDtypeStruct((m, LRU_W), F32), jax.ShapeDtypeStruct((m, LRU_W), F32)],
        scratch_shapes=[pltpu.VMEM((ts, LRU_W), F32), pltpu.VMEM((ts, LRU_W), F32), pltpu.VMEM((1, LRU_W), F32)],
        compiler_params=_params(2),
    )(proj32, proj32, proj32, conv_w, conv_b, wr, br, wi, bi, lam)


def _lru_bwd(dy, proj32, h, conv_w, conv_b, wr, br, wi, bi, lam, batch, seq):
    m = proj32.shape[0]
    ts = _tok_tile(seq)
    nt = seq // ts
    row = lambda b, i: (b * nt + (nt - 1 - i), 0)
    halo = pl.BlockSpec((SUBLANES, LRU_W),
                        lambda b, i: (jnp.maximum(b * (seq // SUBLANES) + (nt - 1 - i) * (ts // SUBLANES) - 1, 0), 0))

    def body(dy_ref, x_ref, xhalo_ref, g_ref, h_ref, hhalo_ref, cw_ref, cb_ref, wr_ref, br_ref, wi_ref, bi_ref,
             lam_ref, dx_ref, dg_ref, dwr_ref, dwi_ref, sums_ref, a_scr, dh_scr, g_scr, carry, du_next):
        b, i = pl.program_id(0), pl.program_id(1)
        first_tile = i == nt - 1

        @pl.when((b == 0) & (i == 0))
        def _():
            dwr_ref[...] = jnp.zeros_like(dwr_ref)
            dwi_ref[...] = jnp.zeros_like(dwi_ref)
            sums_ref[...] = jnp.zeros_like(sums_ref)

        @pl.when(i == 0)
        def _():
            carry[...] = jnp.zeros_like(carry)
            du_next[...] = jnp.zeros_like(du_next)

        xhalo = jnp.where(first_tile, 0.0, xhalo_ref[...])
        ext = jnp.concatenate([xhalo, x_ref[...]], axis=0)
        u = _conv(ext, cw_ref, cb_ref, ts)
        r, ig, sp, a, mult = _lru_gates(u, wr_ref, br_ref, wi_ref, bi_ref, lam_ref)
        gelu, dgelu = _gelu_parts(g_ref[...])
        dyv, hv = dy_ref[...], h_ref[...]
        dg_ref[...] = (dyv * hv * dgelu).astype(BF16)
        a_scr[...] = a
        dh_scr[...] = dyv * gelu

        rid = lax.broadcasted_iota(jnp.int32, (SUBLANES, LRU_W), 0)
        nchunk = ts // SUBLANES

        def chunk(n, cg):
            off = pl.multiple_of((nchunk - 1 - n) * SUBLANES, SUBLANES)
            av, beta = a_scr[pl.ds(off, SUBLANES), :], dh_scr[pl.ds(off, SUBLANES), :]
            alpha = jnp.where(rid == SUBLANES - 1, 1.0, pltpu.roll(av, SUBLANES - 1, 0))
            for d in (1, 2, 4):
                keep = rid + d <= SUBLANES - 1
                beta = jnp.where(keep, beta + alpha * pltpu.roll(beta, SUBLANES - d, 0), beta)
                alpha = jnp.where(keep, alpha * pltpu.roll(alpha, SUBLANES - d, 0), alpha)
            gv = beta + alpha * cg
            g_scr[pl.ds(off, SUBLANES), :] = gv
            return av[0:1, :] * gv[0:1, :]

        carry[...] = lax.fori_loop(0, nchunk, chunk, carry[...])
        gv = g_scr[...]
        hhalo = jnp.where(first_tile, 0.0, hhalo_ref[...])
        hprev = _shift_down(jnp.concatenate([hhalo, hv], axis=0), 1, ts)
        dmult = gv * ig * u
        dig = gv * mult * u
        du = gv * mult * ig
        dlog_a = gv * hprev * a - dmult * a * a / mult
        dr = dlog_a * (-LRU_C * sp)
        dr_pre = dr * r * (1.0 - r)
        di_pre = dig * ig * (1.0 - ig)
        drb, dib, ub = dr_pre.astype(BF16), di_pre.astype(BF16), u.astype(BF16)
        du = du + _dot(drb, wr_ref[...], NT) + _dot(dib, wi_ref[...], NT)
        dwr_ref[...] += _dot(ub, drb, TN)
        dwi_ref[...] += _dot(ub, dib, TN)

        ext_du = jnp.concatenate([du, du_next[...]], axis=0)
        du_next[...] = du[0:SUBLANES, :]
        n_ext = ts + SUBLANES
        dx = cw_ref[3:4, :] * du
        sums = [_colsum(dr_pre), _colsum(di_pre), _colsum(dlog_a * (-LRU_C * r)), _colsum(du)]
        dcw = []
        for k in range(3):
            dx = dx + cw_ref[k:k + 1, :] * pltpu.roll(ext_du, n_ext - (3 - k), 0)[0:ts]
            dcw.append(_colsum(du * _shift_down(ext, 3 - k, ts)))
        dcw.append(_colsum(du * _shift_down(ext, 0, ts)))
        dx_ref[...] = dx.astype(BF16)
        sums_ref[...] += _rows_to_block(sums + dcw, LRU_W)

    full = lambda shape: pl.BlockSpec(shape, lambda b, i: (0,) * len(shape))
    tile = pl.BlockSpec((ts, LRU_W), row)
    return pl.pallas_call(
        body, name="lru_bwd", grid=(batch, nt),
        in_specs=[tile, tile, halo, pl.BlockSpec((ts, LRU_W), lambda b, i: (b * nt + (nt - 1 - i), 1)), tile, halo,
                  full((4, LRU_W)), full((1, LRU_W)), full((LRU_W, LRU_W)), full((1, LRU_W)),
                  full((LRU_W, LRU_W)), full((1, LRU_W)), full((1, LRU_W))],
        out_specs=[tile, tile, full((LRU_W, LRU_W)), full((LRU_W, LRU_W)), full((SUBLANES, LRU_W))],
        out_shape=[jax.ShapeDtypeStruct((m, LRU_W), BF16), jax.ShapeDtypeStruct((m, LRU_W), BF16),
                   jax.ShapeDtypeStruct((LRU_W, LRU_W), F32), jax.ShapeDtypeStruct((LRU_W, LRU_W), F32),
                   jax.ShapeDtypeStruct((SUBLANES, LRU_W), F32)],
        scratch_shapes=[pltpu.VMEM((ts, LRU_W), F32), pltpu.VMEM((ts, LRU_W), F32), pltpu.VMEM((ts, LRU_W), F32),
                        pltpu.VMEM((1, LRU_W), F32), pltpu.VMEM((SUBLANES, LRU_W), F32)],
        compiler_params=_params(2),
    )(dy, proj32, proj32, proj32, h, h, conv_w, conv_b, wr, br, wi, bi, lam)


def _head_masks():
    lane = lax.broadcasted_iota(jnp.int32, (1, LANES), 1)
    return lane < HEAD_DIM


def _stack_heads(x2):
    lo, zero = _head_masks(), jnp.zeros_like(x2)
    return jnp.concatenate([jnp.where(lo, x2, zero), jnp.where(lo, zero, x2)], axis=0)


def _unstack_heads(y):
    return jnp.where(_head_masks(), y[:TQ], y[TQ:])


def _stack_cols(a, b):
    return jnp.concatenate([a, b], axis=0)


def _causal(qi, kb, strict):
    r = jnp.bitwise_and(lax.broadcasted_iota(jnp.int32, (2 * TQ, BLK), 0), TQ - 1) + qi * TQ
    c = lax.broadcasted_iota(jnp.int32, (2 * TQ, BLK), 1) + kb * BLK
    return (c < r) if strict else (c <= r)


def _key_loop(qi, group, carry, descending=False):
    def trip(n, cr):
        done = [n * KB_PER_Q + j for j in range(KB_PER_Q)]
        return group([qi * KB_PER_Q - 1 - t for t in done] if descending else done, cr)

    return lax.fori_loop(0, qi, trip, carry)


def _one_by_one(block):
    def group(kbs, carry):
        for kb in kbs:
            carry = block(kb, carry, False)
        return carry
    return group


def _tri(cmp):
    r = lax.broadcasted_iota(jnp.int32, (BLK, BLK), 0)
    c = lax.broadcasted_iota(jnp.int32, (BLK, BLK), 1)
    return cmp(r, c)


def _dot_split(x, tri):
    hi, lo = _split2(x)
    return _dot(hi, tri) + _dot(lo, tri)


def _sb_fwd(proj16, batch, seq):
    nq = seq // TQ
    scale = HEAD_DIM ** -0.5

    def body(q_ref, k_ref, v_ref, y_ref, t_ref):
        qi = pl.program_id(2)
        qs = _stack_heads(q_ref[0])
        tri_after = _tri(lambda r, c: r > c).astype(BF16)

        def block(kb, carry, masked):
            acc, c = carry
            ks = pl.multiple_of(kb * BLK, BLK)
            k2, v2 = k_ref[0, pl.ds(ks, BLK), :], v_ref[0, pl.ds(ks, BLK), :]
            z = _dot(qs, k2, NT) * scale
            sp = _softplus(z)
            l = -sp
            if masked:
                valid = _causal(qi, kb, True)
                l = jnp.where(valid, l, 0.0)
            w = jnp.exp((z - sp) + _dot_split(l, tri_after) + c)
            if masked:
                w = jnp.where(valid, w, 0.0)
            return acc + _dot(w.astype(BF16), v2), c + jnp.sum(l, axis=1, keepdims=True)

        def group(kbs, carry):
            acc, c = carry
            kv = [(k_ref[0, pl.ds(pl.multiple_of(kb * BLK, BLK), BLK), :],
                   v_ref[0, pl.ds(pl.multiple_of(kb * BLK, BLK), BLK), :]) for kb in kbs]
            zs = [_dot(qs, k2, NT) * scale for k2, _ in kv]
            sps = [_softplus(z) for z in zs]
            afters = [_dot_split(-sp, tri_after) for sp in sps]
            for z, sp, after, (_, v2) in zip(zs, sps, afters, kv):
                acc = acc + _dot(jnp.exp((z - sp) + after + c).astype(BF16), v2)
                c = c - jnp.sum(sp, axis=1, keepdims=True)
            return acc, c

        carry = (jnp.zeros((2 * TQ, LANES), F32), jnp.zeros((2 * TQ, 1), F32))
        first = qi * KB_PER_Q
        for n in reversed(range(KB_PER_Q)):
            carry = block(first + n, carry, True)
        acc, c = _key_loop(qi, group, carry, descending=True)
        y_ref[...] = _unstack_heads(acc)
        t_ref[0] = _unstack_heads(jnp.broadcast_to(c, (2 * TQ, LANES)))

    m = batch * seq
    return pl.pallas_call(
        body, name="sb_fwd", grid=(batch, 2, nq),
        in_specs=[pl.BlockSpec((1, TQ, LANES), lambda b, p, q: (b, q, COL_SBQ + p)),
                  pl.BlockSpec((1, seq, LANES), lambda b, p, q: (b, 0, COL_SBK + p)),
                  pl.BlockSpec((1, seq, LANES), lambda b, p, q: (b, 0, COL_SBV + p))],
        out_specs=[pl.BlockSpec((TQ, LANES), lambda b, p, q: (b * nq + q, p)),
                   pl.BlockSpec((1, TQ, LANES), lambda b, p, q: (p, b * nq + q, 0))],
        out_shape=[jax.ShapeDtypeStruct((m, ATT_W), F32), jax.ShapeDtypeStruct((2, m, LANES), F32)],
        compiler_params=_params(3),
    )(proj16, proj16, proj16)


def _sb_bwd(dy, t, proj16, batch, seq):
    nq = seq // TQ
    scale = HEAD_DIM ** -0.5

    def body(dy_ref, t_ref, q_ref, k_ref, v_ref, dq_ref, dk_ref, dv_ref):
        qi = pl.program_id(2)

        @pl.when(qi == 0)
        def _():
            dk_ref[...] = jnp.zeros_like(dk_ref)
            dv_ref[...] = jnp.zeros_like(dv_ref)

        t2 = t_ref[0]
        qs, dys = _stack_heads(q_ref[0]), _stack_heads(dy_ref[...].astype(BF16))
        tot = _stack_cols(t2[:, 0:1], t2[:, HEAD_DIM:HEAD_DIM + 1])
        tri_incl = _tri(lambda r, c: r <= c).astype(BF16)
        tri_excl = _tri(lambda r, c: r < c).astype(BF16)

        def block(kb, carry, masked):
            dq, pc, ec = carry
            ks = pl.multiple_of(kb * BLK, BLK)
            k2, v2 = k_ref[0, pl.ds(ks, BLK), :], v_ref[0, pl.ds(ks, BLK), :]
            z = _dot(qs, k2, NT) * scale
            sp = _softplus(z)
            l, b = -sp, z - sp
            sig = jnp.exp(b)
            if masked:
                valid = _causal(qi, kb, True)
                l = jnp.where(valid, l, 0.0)
            after = tot - (pc + _dot_split(l, tri_incl))
            w = jnp.exp(b + after)
            if masked:
                w = jnp.where(valid, w, 0.0)
            e = _dot(dys, v2, NT) * w
            et = ec + _dot_split(e, tri_excl)
            dz = e * (1.0 - sig) - et * sig
            if masked:
                dz = jnp.where(valid, dz, 0.0)
            dzb = (dz * scale).astype(BF16)
            dk_ref[0, pl.ds(ks, BLK), :] += _dot(dzb, qs, TN)
            dv_ref[0, pl.ds(ks, BLK), :] += _dot(w.astype(BF16), dys, TN)
            return (dq + _dot(dzb, k2), pc + jnp.sum(l, axis=1, keepdims=True),
                    ec + jnp.sum(e, axis=1, keepdims=True))

        col = jnp.zeros((2 * TQ, 1), F32)
        first = qi * KB_PER_Q
        carry = _key_loop(qi, _one_by_one(block),(jnp.zeros((2 * TQ, LANES), F32), col, col))
        for n in range(KB_PER_Q):
            carry = block(first + n, carry, True)
        dq_ref[...] = _unstack_heads(carry[0])

    m = batch * seq
    whole = lambda col: pl.BlockSpec((1, seq, LANES), lambda b, p, q: (b, 0, col + p))
    return pl.pallas_call(
        body, name="sb_bwd", grid=(batch, 2, nq),
        in_specs=[pl.BlockSpec((TQ, LANES), lambda b, p, q: (b * nq + q, p)),
                  pl.BlockSpec((1, TQ, LANES), lambda b, p, q: (p, b * nq + q, 0)),
                  pl.BlockSpec((1, TQ, LANES), lambda b, p, q: (b, q, COL_SBQ + p)),
                  whole(COL_SBK), whole(COL_SBV)],
        out_specs=[pl.BlockSpec((TQ, LANES), lambda b, p, q: (b * nq + q, p)), whole(0), whole(0)],
        out_shape=[jax.ShapeDtypeStruct((m, ATT_W), F32), jax.ShapeDtypeStruct((batch, seq, ATT_W), F32),
                   jax.ShapeDtypeStruct((batch, seq, ATT_W), F32)],
        compiler_params=_params(3),
    )(dy, t, proj16, proj16, proj16)


def _fox_pre(proj32, gq, gk, bf, group_mean, batch, seq):
    m = proj32.shape[0]
    ts = _tok_tile(seq)
    nt = seq // ts

    def body(q_ref, k_ref, f_ref, gq_ref, gk_ref, bf_ref, gm_ref, fq_ref, fk_ref, fc_ref, carry):
        i = pl.program_id(1)

        @pl.when(i == 0)
        def _():
            carry[...] = jnp.zeros_like(carry)

        gm = gm_ref[...]
        for src, g_ref, dst in ((q_ref, gq_ref, fq_ref), (k_ref, gk_ref, fk_ref)):
            v = src[...]
            ms = _dot_split(v * v, gm)
            dst[...] = (v * lax.rsqrt(ms + EPS) * g_ref[...]).astype(BF16)
        z = f_ref[...] + bf_ref[...]
        lf = jnp.minimum(z, 0.0) - jnp.log(1.0 + jnp.exp(-jnp.abs(z)))
        r = lax.broadcasted_iota(jnp.int32, (ts, ts), 0)
        c = lax.broadcasted_iota(jnp.int32, (ts, ts), 1)
        tri = (r >= c).astype(BF16)
        hi, mid, low = _split3(lf)
        fc = _dot(tri, hi) + _dot(tri, mid) + _dot(tri, low) + carry[...]
        fc_ref[...] = fc
        carry[...] = fc[ts - 1:ts, :]

    full = lambda shape: pl.BlockSpec(shape, lambda b, i: (0,) * len(shape))
    return pl.pallas_call(
        body, name="fox_pre", grid=(batch, nt),
        in_specs=[pl.BlockSpec((ts, ATT_W), lambda b, i: (b * nt + i, 7)),
                  pl.BlockSpec((ts, ATT_W), lambda b, i: (b * nt + i, 8)),
                  pl.BlockSpec((ts, LANES), lambda b, i: (b * nt + i, COL_FXF)),
                  full((1, ATT_W)), full((1, ATT_W)), full((1, LANES)), full((ATT_W, ATT_W))],
        out_specs=[pl.BlockSpec((ts, ATT_W), lambda b, i: (b * nt + i, 0)),
                   pl.BlockSpec((ts, ATT_W), lambda b, i: (b * nt + i, 0)),
                   pl.BlockSpec((ts, LANES), lambda b, i: (b * nt + i, 0))],
        out_shape=[jax.ShapeDtypeStruct((m, ATT_W), BF16), jax.ShapeDtypeStruct((m, ATT_W), BF16),
                   jax.ShapeDtypeStruct((m, LANES), F32)],
        scratch_shapes=[pltpu.VMEM((1, LANES), F32)],
        compiler_params=_params(2),
    )(proj32, proj32, proj32, gq, gk, bf, group_mean)


def _fox_specs(batch, seq):
    nq = seq // TQ
    return dict(
        qblk=pl.BlockSpec((1, TQ, LANES), lambda b, p, q: (b, q, p)),
        whole=pl.BlockSpec((1, seq, LANES), lambda b, p, q: (b, 0, p)),
        vwhole=pl.BlockSpec((1, seq, LANES), lambda b, p, q: (b, 0, COL_FXV + p)),
        fcol=pl.BlockSpec((1, 1, TQ, 2), lambda b, p, q: (b, p, q, 0)),
        frow=pl.BlockSpec((1, 1, 2, seq), lambda b, p, q: (b, p, 0, 0)),
        rows=pl.BlockSpec((TQ, LANES), lambda b, p, q: (b * nq + q, p)),
        stat=pl.BlockSpec((1, TQ, LANES), lambda b, p, q: (p, b * nq + q, 0)),
    )


def _fox_logits(qs, k2, fq_col, fr_ref, ks, is_a, scale):
    fk_row = jnp.where(is_a, fr_ref[0, 0, 0:1, pl.ds(ks, BLK)], fr_ref[0, 0, 1:2, pl.ds(ks, BLK)])
    return _dot(qs, k2, NT) * scale + fq_col - fk_row


def _fox_fwd(fq, fk, proj16, fcol, frow, batch, seq):
    nq = seq // TQ
    scale = HEAD_DIM ** -0.5

    def body(q_ref, k_ref, v_ref, fc_ref, fr_ref, y_ref, lse_ref):
        qi = pl.program_id(2)
        qs = _stack_heads(q_ref[0])
        fcv = fc_ref[0, 0]
        fq_col = _stack_cols(fcv[:, 0:1], fcv[:, 1:2])
        is_a = lax.broadcasted_iota(jnp.int32, (2 * TQ, 1), 0) < TQ

        def block(kb, carry, masked):
            acc, mx, den = carry
            ks = pl.multiple_of(kb * BLK, BLK)
            k2, v2 = k_ref[0, pl.ds(ks, BLK), :], v_ref[0, pl.ds(ks, BLK), :]
            s = _fox_logits(qs, k2, fq_col, fr_ref, ks, is_a, scale)
            if masked:
                s = jnp.where(_causal(qi, kb, False), s, NEG_BIG)
            mx_new = jnp.maximum(mx, jnp.max(s, axis=1, keepdims=True))
            p = jnp.exp(s - mx_new)
            alpha = jnp.exp(mx - mx_new)
            return (alpha * acc + _dot(p.astype(BF16), v2), mx_new, alpha * den + jnp.sum(p, axis=1, keepdims=True))

        first = qi * KB_PER_Q
        carry = (jnp.zeros((2 * TQ, LANES), F32), jnp.full((2 * TQ, 1), NEG_BIG, F32), jnp.zeros((2 * TQ, 1), F32))
        carry = _key_loop(qi, _one_by_one(block),carry)
        for n in range(KB_PER_Q):
            carry = block(first + n, carry, True)
        acc, mx, den = carry
        y_ref[...] = _unstack_heads(acc / den)
        lse_ref[0] = _unstack_heads(jnp.broadcast_to(mx + jnp.log(den), (2 * TQ, LANES)))

    m = batch * seq
    sp = _fox_specs(batch, seq)
    return pl.pallas_call(
        body, name="fox_fwd", grid=(batch, 2, nq),
        in_specs=[sp["qblk"], sp["whole"], sp["vwhole"], sp["fcol"], sp["frow"]],
        out_specs=[sp["rows"], sp["stat"]],
        out_shape=[jax.ShapeDtypeStruct((m, ATT_W), F32), jax.ShapeDtypeStruct((2, m, LANES), F32)],
        compiler_params=_params(3),
    )(fq, fk, proj16, fcol, frow)


def _fox_bwd(dy, y, lse, fq, fk, proj16, fcol, frow, batch, seq):
    nq = seq // TQ
    scale = HEAD_DIM ** -0.5

    def body(dy_ref, y_ref, lse_ref, q_ref, k_ref, v_ref, fc_ref, fr_ref, dq_ref, dk_ref, dv_ref, dfr_ref, dfc_ref):
        qi = pl.program_id(2)

        @pl.when(qi == 0)
        def _():
            dk_ref[...] = jnp.zeros_like(dk_ref)
            dv_ref[...] = jnp.zeros_like(dv_ref)
            dfr_ref[...] = jnp.zeros_like(dfr_ref)

        lo = _head_masks()
        lane = lax.broadcasted_iota(jnp.int32, (1, LANES), 1)
        dy2, lse2, fcv = dy_ref[...], lse_ref[0], fc_ref[0, 0]
        qs, dys = _stack_heads(q_ref[0]), _stack_heads(dy2.astype(BF16))
        dyy = dy2 * y_ref[...]
        delta = _stack_cols(jnp.sum(jnp.where(lo, dyy, 0.0), axis=1, keepdims=True),
                            jnp.sum(jnp.where(lo, 0.0, dyy), axis=1, keepdims=True))
        lse_col = _stack_cols(lse2[:, 0:1], lse2[:, HEAD_DIM:HEAD_DIM + 1])
        fq_col = _stack_cols(fcv[:, 0:1], fcv[:, 1:2])
        is_a = lax.broadcasted_iota(jnp.int32, (2 * TQ, 1), 0) < TQ

        def block(kb, carry, masked):
            dq, rs = carry
            ks = pl.multiple_of(kb * BLK, BLK)
            k2, v2 = k_ref[0, pl.ds(ks, BLK), :], v_ref[0, pl.ds(ks, BLK), :]
            p = jnp.exp(_fox_logits(qs, k2, fq_col, fr_ref, ks, is_a, scale) - lse_col)
            if masked:
                p = jnp.where(_causal(qi, kb, False), p, 0.0)
            ds = p * (_dot(dys, v2, NT) - delta)
            dsb = (ds * scale).astype(BF16)
            dk_ref[0, pl.ds(ks, BLK), :] += _dot(dsb, qs, TN)
            dv_ref[0, pl.ds(ks, BLK), :] += _dot(p.astype(BF16), dys, TN)
            dfr_ref[0, 0, 0:1, pl.ds(ks, BLK)] -= jnp.sum(ds[:TQ], axis=0, keepdims=True)
            dfr_ref[0, 0, 1:2, pl.ds(ks, BLK)] -= jnp.sum(ds[TQ:], axis=0, keepdims=True)
            return dq + _dot(dsb, k2), rs + jnp.sum(ds, axis=1, keepdims=True)

        first = qi * KB_PER_Q
        carry = _key_loop(qi, _one_by_one(block),(jnp.zeros((2 * TQ, LANES), F32), jnp.zeros((2 * TQ, 1), F32)))
        for n in range(KB_PER_Q):
            carry = block(first + n, carry, True)
        dq, rs = carry
        dq_ref[...] = _unstack_heads(dq)
        dfc_ref[0] = jnp.where(lane == 0, rs[:TQ], jnp.where(lane == 1, rs[TQ:], 0.0))

    m = batch * seq
    sp = _fox_specs(batch, seq)
    return pl.pallas_call(
        body, name="fox_bwd", grid=(batch, 2, nq),
        in_specs=[sp["rows"], sp["rows"], sp["stat"], sp["qblk"], sp["whole"], sp["vwhole"], sp["fcol"], sp["frow"]],
        out_specs=[sp["rows"], sp["whole"], sp["whole"],
                   pl.BlockSpec((1, 1, SUBLANES, seq), lambda b, p, q: (b, p, 0, 0)), sp["stat"]],
        out_shape=[jax.ShapeDtypeStruct((m, ATT_W), F32), jax.ShapeDtypeStruct((batch, seq, ATT_W), F32),
                   jax.ShapeDtypeStruct((batch, seq, ATT_W), F32),
                   jax.ShapeDtypeStruct((batch, 2, SUBLANES, seq), F32), jax.ShapeDtypeStruct((2, m, LANES), F32)],
        compiler_params=_params(3),
    )(dy, y, lse, fq, fk, proj16, fcol, frow)


def _fox_post_bwd(dfq, dfk, dfc, proj32, gq, gk, bf, group_mean, batch, seq):
    m = proj32.shape[0]
    ts = _tok_tile(seq)
    nt = seq // ts
    tile = lambda w, col: pl.BlockSpec((ts, w), lambda b, i: (b * nt + (nt - 1 - i), col))

    def body(dfq_ref, dfk_ref, dfc_ref, q_ref, k_ref, f_ref, gq_ref, gk_ref, bf_ref, gm_ref,
             dq_ref, dk_ref, df_ref, gs_ref, bs_ref, carry):
        i = pl.program_id(1)

        @pl.when(i == 0)
        def _():
            carry[...] = jnp.zeros_like(carry)

        gm = gm_ref[...]
        rows = []
        for src, g_ref, d_ref, dst in ((q_ref, gq_ref, dfq_ref, dq_ref), (k_ref, gk_ref, dfk_ref, dk_ref)):
            v, dv = src[...], d_ref[...]
            rstd = lax.rsqrt(_dot_split(v * v, gm) + EPS)
            vhat = v * rstd
            rows.append(_colsum(dv * vhat))
            dvh = dv * g_ref[...]
            dst[...] = (rstd * (dvh - vhat * _dot_split(dvh * vhat, gm))).astype(BF16)
        gs_ref[0] = _rows_to_block(rows, ATT_W)

        dfc_v = dfc_ref[...]
        r = lax.broadcasted_iota(jnp.int32, (ts, ts), 0)
        c = lax.broadcasted_iota(jnp.int32, (ts, ts), 1)
        tri = (r <= c).astype(BF16)
        hi, mid, low = _split3(dfc_v)
        dlf = _dot(tri, hi) + _dot(tri, mid) + _dot(tri, low) + carry[...]
        carry[...] = dlf[0:1, :]
        z = f_ref[...] + bf_ref[...]
        dz = dlf * _sigmoid(-z)
        df_ref[...] = dz.astype(BF16)
        bs_ref[0] = _rows_to_block([_colsum(dz)], LANES)

    full = lambda shape: pl.BlockSpec(shape, lambda b, i: (0,) * len(shape))
    part = lambda w: pl.BlockSpec((1, SUBLANES, w), lambda b, i: (b * nt + (nt - 1 - i), 0, 0))
    return pl.pallas_call(
        body, name="fox_post_bwd", grid=(batch, nt),
        in_specs=[tile(ATT_W, 0), tile(ATT_W, 0), tile(LANES, 0), tile(ATT_W, 7), tile(ATT_W, 8), tile(LANES, COL_FXF),
                  full((1, ATT_W)), full((1, ATT_W)), full((1, LANES)), full((ATT_W, ATT_W))],
        out_specs=[tile(ATT_W, 0), tile(ATT_W, 0), tile(LANES, 0), part(ATT_W), part(LANES)],
        out_shape=[jax.ShapeDtypeStruct((m, ATT_W), BF16), jax.ShapeDtypeStruct((m, ATT_W), BF16),
                   jax.ShapeDtypeStruct((m, LANES), BF16),
                   jax.ShapeDtypeStruct((batch * nt, SUBLANES, ATT_W), F32),
                   jax.ShapeDtypeStruct((batch * nt, SUBLANES, LANES), F32)],
        scratch_shapes=[pltpu.VMEM((1, LANES), F32)],
        compiler_params=_params(2),
    )(dfq, dfk, dfc, proj32, proj32, proj32, gq, gk, bf, group_mean)


_GROUPS = ((0, LRU_W), (LRU_W, LRU_W + ATT_W), (LRU_W + ATT_W, LRU_W + 2 * ATT_W))


def _outnorm(y_lru, y_sb, y_fox, gmix, seq):
    m = y_lru.shape[0]
    tm = _tok_tile(seq)

    def body(a_ref, b_ref, c_ref, g_ref, o_ref):
        parts = []
        for ref in (a_ref, b_ref, c_ref):
            v = ref[...]
            parts.append(v * lax.rsqrt(jnp.mean(v * v, axis=-1, keepdims=True) + EPS))
        o_ref[...] = (jnp.concatenate(parts, axis=1) * g_ref[...]).astype(BF16)

    t = lambda w: pl.BlockSpec((tm, w), lambda i: (i, 0))
    return pl.pallas_call(
        body, name="outnorm", grid=(m // tm,),
        in_specs=[t(LRU_W), t(ATT_W), t(ATT_W), pl.BlockSpec((1, D_MODEL), lambda i: (0, 0))],
        out_specs=t(D_MODEL), out_shape=jax.ShapeDtypeStruct((m, D_MODEL), BF16), compiler_params=_params(1),
    )(y_lru, y_sb, y_fox, gmix)


def _outnorm_bwd_epilogue(p, e_refs, o_refs):
    gmix = e_refs[3][...]
    dg = []
    for n, (lo, hi) in enumerate(_GROUPS):
        v, dyn = e_refs[n][...], p[:, lo:hi]
        rstd = lax.rsqrt(jnp.mean(v * v, axis=-1, keepdims=True) + EPS)
        vhat = v * rstd
        dg.append(_colsum(dyn * vhat))
        dvh = dyn * gmix[:, lo:hi]
        o_refs[n][...] = rstd * (dvh - vhat * jnp.mean(dvh * vhat, axis=-1, keepdims=True))
    o_refs[3][0] = _rows_to_block([jnp.concatenate(dg, axis=1)], p.shape[1])


def _pair_layouts(fcum, batch, seq):
    f4 = fcum[:, :4].reshape(batch, seq, 2, 2)
    return f4.transpose(0, 2, 1, 3), f4.transpose(0, 2, 3, 1)


def _gate_grad_cols(dfr, dfc, batch, seq):
    keys = dfr[:, :, :2, :].transpose(0, 3, 1, 2).reshape(batch * seq, 4)
    queries = dfc[:, :, :2].transpose(1, 0, 2).reshape(batch * seq, 4)
    return jnp.pad(keys + queries, ((0, 0), (0, LANES - 4)))


def _mixer_fwd(x, h, w, gate, batch, seq):
    m, d = x.shape
    tm = _tok_tile(seq)
    tpb = seq // tm

    def in_epilogue(p, e_refs, o_refs):
        o_refs[0][...] = p
        o_refs[1][...] = p.astype(BF16)

    tn_in = 896
    proj32, proj16 = _mm(h, w["w_in"], mode="nn", tm=tm, tn=tn_in, tk=d, name="mix_in",
                         outs=[((m, N_IN_PAD), F32, (tm, tn_in), lambda i, j: (i, j)),
                               ((m, N_IN_PAD), BF16, (tm, tn_in), lambda i, j: (i, j))],
                         epilogue=in_epilogue)
    y_lru, h_lru = _lru_fwd(proj32, w["conv_w"], w["conv_b"], w["wr"], w["br"], w["wi"], w["bi"], w["lam"], batch, seq)
    p16 = proj16.reshape(batch, seq, N_IN_PAD)
    y_sb, t_sb = _sb_fwd(p16, batch, seq)
    fq, fk, fcum = _fox_pre(proj32, w["gq"], w["gk"], w["bf"], w["group_mean"], batch, seq)
    fcol, frow = _pair_layouts(fcum, batch, seq)
    fq3, fk3 = fq.reshape(batch, seq, ATT_W), fk.reshape(batch, seq, ATT_W)
    y_fox, lse = _fox_fwd(fq3, fk3, p16, fcol, frow, batch, seq)
    ynorm = _outnorm(y_lru, y_sb, y_fox, w["gmix"], seq)

    def out_epilogue(p, e_refs, o_refs):
        x_ref, g_ref = e_refs
        o_refs[0][...] = x_ref[...] + (1.0 + g_ref[0]) * p
        o_refs[1][...] = p.astype(BF16)

    x_out, out = _mm(ynorm, w["w_out"], mode="nn", tm=tm, tn=d, tk=d, name="mix_out",
                     extras=[(x, (tm, d), lambda i, j: (i, 0)), (gate, (1, 1, d), lambda i, j: (i // tpb, 0, 0))],
                     outs=[((m, d), F32, (tm, d), lambda i, j: (i, 0)), ((m, d), BF16, (tm, d), lambda i, j: (i, 0))],
                     epilogue=out_epilogue)
    saved = dict(proj32=proj32, p16=p16, h_lru=h_lru, y_lru=y_lru, y_sb=y_sb, t_sb=t_sb, fq3=fq3, fk3=fk3,
                 fcol=fcol, frow=frow, y_fox=y_fox, lse=lse, ynorm=ynorm, out=out)
    return x_out, saved


def _mixer_bwd(dx_out, x, h, s, w, gn, scale, gate, batch, seq, on_grads):
    m, d = x.shape
    tm = _tok_tile(seq)
    tpb = seq // tm
    dout, dgate_parts = _residual_bwd(dx_out, s["out"], gate, 1.0, seq, "mix_res_bwd")
    (dw_out,) = _mm(s["ynorm"], dout, mode="tn", tm=d, tn=d, tk=tm, name="mix_dwout",
                    outs=[((d, d), BF16, (d, d), lambda i, j: (i, j))], epilogue=_store_epilogue([BF16]))
    dy_lru, dy_sb, dy_fox, gmix_parts = _mm(
        dout, w["w_out"], mode="nt", tm=tm, tn=d, tk=d, name="mix_out_dx",
        extras=[(s["y_lru"], (tm, LRU_W), lambda i, j: (i, 0)), (s["y_sb"], (tm, ATT_W), lambda i, j: (i, 0)),
                (s["y_fox"], (tm, ATT_W), lambda i, j: (i, 0)), (w["gmix"], (1, d), lambda i, j: (0, 0))],
        outs=[((m, LRU_W), F32, (tm, LRU_W), lambda i, j: (i, 0)), ((m, ATT_W), F32, (tm, ATT_W), lambda i, j: (i, 0)),
              ((m, ATT_W), F32, (tm, ATT_W), lambda i, j: (i, 0)),
              ((m // tm, SUBLANES, d), F32, (1, SUBLANES, d), lambda i, j: (i, 0, 0))],
        epilogue=_outnorm_bwd_epilogue)

    dsq, dsk, dsv = _sb_bwd(dy_sb, s["t_sb"], s["p16"], batch, seq)
    dfq, dfk, dfv, dfr, dfc = _fox_bwd(dy_fox, s["y_fox"], s["lse"], s["fq3"], s["fk3"], s["p16"], s["fcol"],
                                       s["frow"], batch, seq)
    dfc_cols = _gate_grad_cols(dfr, dfc, batch, seq)
    dxq, dxk, dxf, gqk_parts, bf_parts = _fox_post_bwd(dfq, dfk.reshape(m, ATT_W), dfc_cols, s["proj32"],
                                                       w["gq"], w["gk"], w["bf"], w["group_mean"], batch, seq)
    dlx, dlg, dwr, dwi, lru_sums = _lru_bwd(dy_lru, s["proj32"], s["h_lru"], w["conv_w"], w["conv_b"], w["wr"],
                                            w["br"], w["wi"], w["bi"], w["lam"], batch, seq)
    dproj = jnp.concatenate([dlx, dlg, dsq.astype(BF16), dsk.reshape(m, ATT_W).astype(BF16),
                             dsv.reshape(m, ATT_W).astype(BF16), dxq, dxk, dfv.reshape(m, ATT_W).astype(BF16), dxf],
                            axis=1)
    tn_in = 896
    (dw_in,) = _mm(h, dproj, mode="tn", tm=d, tn=tn_in, tk=tm, name="mix_dwin",
                   outs=[((d, N_IN_PAD), BF16, (d, tn_in), lambda i, j: (i, j))], epilogue=_store_epilogue([BF16]))
    scale = scale + on_grads(dw_in, dw_out)[0, 0]
    dx, nm_parts = _mm(dproj, w["w_in"], mode="nt", tm=tm, tn=d, tk=tn_in, name="mix_in_dx",
                       extras=[(x, (tm, d), lambda i, j: (i, 0)), (dx_out, (tm, d), lambda i, j: (i, 0)),
                               (gn, (1, d), lambda i, j: (0, 0)), (scale, (1, 1, d), lambda i, j: (i // tpb, 0, 0))],
                       outs=[((m, d), F32, (tm, d), lambda i, j: (i, 0)),
                             ((m // tm, SUBLANES, d), F32, (1, SUBLANES, d), lambda i, j: (i, 0, 0))],
                       epilogue=_normmod_bwd_epilogue)
    grads = dict(dwr=dwr, dwi=dwi, lru_sums=lru_sums, gmix_parts=gmix_parts,
                 gqk_parts=gqk_parts, bf_parts=bf_parts)
    return dx, grads, nm_parts, dgate_parts


def _block_diag(w):
    nb = w.shape[0]
    eye = jnp.eye(nb, dtype=w.dtype)
    return (eye[:, None, :, None] * w[:, :, None, :]).reshape(nb * HEAD_DIM, nb * HEAD_DIM)


def _block_diag_grad(g):
    nb = LRU_W // HEAD_DIM
    g4 = g.reshape(nb, HEAD_DIM, nb, HEAD_DIM)
    return jnp.stack([g4[n, :, n, :] for n in range(nb)])


def _per_batch(parts, batch, row):
    r = parts[:, row, :]
    return r.reshape(batch, -1, r.shape[-1]).sum(axis=1)


def _local_step(x3, target3, mod, wts, big_weights):
    batch, seq, d = x3.shape
    assert seq % TQ == 0, seq
    m = batch * seq
    n_layers = mod.shape[0]
    x = x3.reshape(m, d)
    group_mean = _block_diag(jnp.full((ATT_W // HEAD_DIM, HEAD_DIM, HEAD_DIM), 1.0 / HEAD_DIM, BF16))
    vec = lambda l, j, t: mod[l, :, j, t][:, None, :]

    layers, saved = [], []
    for l in range(n_layers):
        gq = jnp.tile(wts["g_qk"][l, 0], ATT_W // HEAD_DIM)[None, :]
        gk = jnp.tile(wts["g_qk"][l, 1], ATT_W // HEAD_DIM)[None, :]
        bf = jnp.pad(wts["b_fgate"][l], (0, LANES - 4))[None, :]
        lw = dict(conv_w=wts["conv_w"][l],
                  conv_b=wts["conv_b"][l][None, :], wr=_block_diag(wts["w_rgate"][l]).astype(BF16),
                  br=wts["b_rgate"][l][None, :], wi=_block_diag(wts["w_igate"][l]).astype(BF16),
                  bi=wts["b_igate"][l][None, :], lam=wts["lru_lambda"][l][None, :], gq=gq, gk=gk, bf=bf,
                  group_mean=group_mean, gmix=wts["g_mix_out"][l][None, :])
        layers.append(lw)
        gn = lambda j: wts["g_norm"][l, j][None, :]
        sv = dict(x0=x)
        sv["h0"] = _normmod(x, gn(0), vec(l, 0, 1), vec(l, 0, 0), seq, f"normmod_{l}_0")
        wup, wdown = big_weights(l, "ffn0", sv["h0"])
        x, sv["ffn0"], wdown = _ffn_fwd(x, sv["h0"], wup, wdown, vec(l, 0, 2), seq, f"{l}_0")
        sv["w_ffn0"] = (wup, wdown)
        sv["x1"] = x
        sv["h1"] = _normmod(x, gn(1), vec(l, 1, 1), vec(l, 1, 0), seq, f"normmod_{l}_1")
        lw["w_in"], lw["w_out"] = big_weights(l, "mix", sv["h1"])
        x, sv["mix"] = _mixer_fwd(x, sv["h1"], lw, vec(l, 1, 2), batch, seq)
        sv["x2"] = x
        sv["h2"] = _normmod(x, gn(2), vec(l, 2, 1), vec(l, 2, 0), seq, f"normmod_{l}_2")
        wup, wdown = big_weights(l, "ffn1", sv["h2"])
        x, sv["ffn1"], wdown = _ffn_fwd(x, sv["h2"], wup, wdown, vec(l, 2, 2), seq, f"{l}_1")
        sv["w_ffn1"] = (wup, wdown)
        saved.append(sv)

    dx, loss_parts = _loss_head(x, target3.reshape(m, d), seq)
    loss = jnp.sum(loss_parts[:, 0, 0])

    handles = {}

    def scatter(key, shapes):
        def on_grads(*grads):
            ops = [(g.reshape(shape), 0, "scatter") for g, shape in zip(grads, shapes)]
            handles[key], token = _flight_start(ops, f"grads_{key[0]}_{key[1]}_start")
            return token
        return on_grads

    ffn_shapes = ((2 * N_FF_SHARD, d, FF_SHARD), (N_DEV, D_FF // N_DEV, d))
    mix_shapes = ((N_DEV, d // N_DEV, N_IN_PAD), (N_DEV, d // N_DEV, d))
    small = {k: [] for k in ("dmod", "g_norm", "b_fgate", "conv_w", "conv_b", "w_rgate", "b_rgate", "w_igate",
                             "b_igate", "lru_lambda", "g_qk", "g_mix_out")}
    for l in reversed(range(n_layers)):
        sv, lw = saved[l], layers[l]
        gn = lambda j: wts["g_norm"][l, j][None, :]
        dx, nm2, dg2 = _ffn_bwd(dx, sv["x2"], sv["h2"], sv["ffn1"], *sv["w_ffn1"], gn(2), vec(l, 2, 1), vec(l, 2, 2),
                                seq, f"{l}_1", scatter((l, "ffn1"), ffn_shapes))
        dx, mg, nm1, dg1 = _mixer_bwd(dx, sv["x1"], sv["h1"], sv["mix"], lw, gn(1), vec(l, 1, 1), vec(l, 1, 2),
                                      batch, seq, scatter((l, "mix"), mix_shapes))
        dx, nm0, dg0 = _ffn_bwd(dx, sv["x0"], sv["h0"], sv["ffn0"], *sv["w_ffn0"], gn(0), vec(l, 0, 1), vec(l, 0, 2),
                                seq, f"{l}_0", scatter((l, "ffn0"), ffn_shapes))
        dmod_l, gnorm_l = [], []
        for nm, dg in ((nm0, dg0), (nm1, dg1), (nm2, dg2)):
            dmod_l.append(jnp.stack([_per_batch(nm, batch, 0), _per_batch(nm, batch, 1), _per_batch(dg, batch, 0)],
                                    axis=1))
            gnorm_l.append(jnp.sum(nm[:, 2, :], axis=0))
        small["dmod"].insert(0, jnp.stack(dmod_l, axis=1))
        small["g_norm"].insert(0, jnp.stack(gnorm_l))
        ls = mg["lru_sums"]
        small["b_rgate"].insert(0, ls[0])
        small["b_igate"].insert(0, ls[1])
        small["lru_lambda"].insert(0, ls[2] * (-_sigmoid(-wts["lru_lambda"][l])))
        small["conv_b"].insert(0, ls[3])
        small["conv_w"].insert(0, ls[4:8])
        small["w_rgate"].insert(0, _block_diag_grad(mg["dwr"]))
        small["w_igate"].insert(0, _block_diag_grad(mg["dwi"]))
        small["g_mix_out"].insert(0, jnp.sum(mg["gmix_parts"][:, 0, :], axis=0))
        gqk = jnp.sum(mg["gqk_parts"][:, :2, :], axis=0).reshape(2, ATT_W // HEAD_DIM, HEAD_DIM).sum(axis=1)
        small["g_qk"].insert(0, gqk)
        small["b_fgate"].insert(0, jnp.sum(mg["bf_parts"][:, 0, :4], axis=0))
    small = {k: jnp.stack(v) for k, v in small.items()}
    return loss, dx.reshape(batch, seq, d), handles, small


def _row_tile(rows, row_bytes):
    for t in (512, 256, 128, 64, 32, 16):
        if rows % t == 0 and t * row_bytes <= 4 * 1024 * 1024:
            return t
    return rows


def _adamw(parts, w, m, v, name):
    groups, n_parts, rows, cols = parts.shape
    tr = _row_tile(rows, cols * (n_parts * parts.dtype.itemsize + 7 * 4))
    c1 = 1.0 - ADAM_B1 ** ADAM_STEP
    c2 = 1.0 - ADAM_B2 ** ADAM_STEP

    def body(p_ref, w_ref, m_ref, v_ref, g_out, d_out, m_out, v_out):
        g = p_ref[0].astype(F32)
        for n in range(1, n_parts):
            g = g + p_ref[n].astype(F32)
        m_new = ADAM_B1 * m_ref[...] + (1.0 - ADAM_B1) * g
        v_new = ADAM_B2 * v_ref[...] + (1.0 - ADAM_B2) * (g * g)
        g_out[...] = g
        d_out[...] = -ADAM_LR * ((m_new / c1) / (jnp.sqrt(v_new / c2) + ADAM_EPS) + ADAM_WD * w_ref[...])
        m_out[...] = m_new
        v_out[...] = v_new

    tile = pl.BlockSpec((None, tr, cols), lambda g, i: (g, i, 0))
    return pl.pallas_call(
        body, name=name, grid=(groups, rows // tr),
        in_specs=[pl.BlockSpec((None, n_parts, tr, cols), lambda g, i: (g, 0, i, 0)), tile, tile, tile],
        out_specs=[tile] * 4, out_shape=[jax.ShapeDtypeStruct((groups, rows, cols), F32)] * 4,
        compiler_params=_params(2),
    )(parts, w, m, v)


def _sum_parts(parts):
    n_parts, rows, cols = parts.shape

    def body(p_ref, o_ref):
        acc = p_ref[0]
        for n in range(1, n_parts):
            acc = acc + p_ref[n]
        o_ref[...] = acc

    return pl.pallas_call(body, name="sum_small", out_shape=jax.ShapeDtypeStruct((rows, cols), F32),
                          compiler_params=pltpu.CompilerParams(vmem_limit_bytes=VMEM_LIMIT_BYTES))(parts)


def _flatten(arrays, multiple):
    flat = jnp.concatenate([a.reshape(-1).astype(F32) for a in arrays])
    pad = (-flat.shape[0]) % multiple
    return jnp.pad(flat, (0, pad)).reshape(-1, LANES)


def _unflatten(flat2d, shapes):
    flat, out, off = flat2d.reshape(-1), [], 0
    for s in shapes:
        n = math.prod(s)
        out.append(flat[off:off + n].reshape(s))
        off += n
    return out


SMALL_NAMES = ("b_ada", "g_norm", "b_fgate", "conv_w", "conv_b", "w_rgate", "b_rgate", "w_igate", "b_igate",
               "lru_lambda", "g_qk", "g_mix_out")
WEIGHT_NAMES = ("w_ada", "b_ada", "g_norm", "w_ffn_up", "w_ffn_down", "w_in", "b_fgate", "conv_w", "conv_b",
                "w_rgate", "b_rgate", "w_igate", "b_igate", "lru_lambda", "g_qk", "g_mix_out", "w_out")


def kernel(x, c, w_ada, b_ada, g_norm, w_ffn_up, w_ffn_down, w_in, b_fgate, conv_w, conv_b, w_rgate, b_rgate, w_igate, b_igate, lru_lambda, g_qk, g_mix_out, w_out, loss_target, m_w_ada, m_b_ada, m_g_norm, m_w_ffn_up, m_w_ffn_down, m_w_in, m_b_fgate, m_conv_w, m_conv_b, m_w_rgate, m_b_rgate, m_w_igate, m_b_igate, m_lru_lambda, m_g_qk, m_g_mix_out, m_w_out, v_w_ada, v_b_ada, v_g_norm, v_w_ffn_up, v_w_ffn_down, v_w_in, v_b_fgate, v_conv_w, v_conv_b, v_w_rgate, v_b_rgate, v_w_igate, v_b_igate, v_lru_lambda, v_g_qk, v_g_mix_out, v_w_out):
    batch, seq, d = x.shape
    n_layers = w_ada.shape[0]
    me = 4 * lax.axis_index("x") + 2 * lax.axis_index("y") + lax.axis_index("c")
    weights = dict(w_ada=w_ada, b_ada=b_ada, g_norm=g_norm, w_ffn_up=w_ffn_up, w_ffn_down=w_ffn_down, w_in=w_in,
                   b_fgate=b_fgate, conv_w=conv_w, conv_b=conv_b, w_rgate=w_rgate, b_rgate=b_rgate, w_igate=w_igate,
                   b_igate=b_igate, lru_lambda=lru_lambda, g_qk=g_qk, g_mix_out=g_mix_out, w_out=w_out)
    moments_m = dict(w_ada=m_w_ada, b_ada=m_b_ada, g_norm=m_g_norm, w_ffn_up=m_w_ffn_up, w_ffn_down=m_w_ffn_down,
                     w_in=m_w_in, b_fgate=m_b_fgate, conv_w=m_conv_w, conv_b=m_conv_b, w_rgate=m_w_rgate,
                     b_rgate=m_b_rgate, w_igate=m_w_igate, b_igate=m_b_igate, lru_lambda=m_lru_lambda, g_qk=m_g_qk,
                     g_mix_out=m_g_mix_out, w_out=m_w_out)
    moments_v = dict(w_ada=v_w_ada, b_ada=v_b_ada, g_norm=v_g_norm, w_ffn_up=v_w_ffn_up, w_ffn_down=v_w_ffn_down,
                     w_in=v_w_in, b_fgate=v_b_fgate, conv_w=v_conv_w, conv_b=v_conv_b, w_rgate=v_w_rgate,
                     b_rgate=v_b_rgate, w_igate=v_w_igate, b_igate=v_b_igate, lru_lambda=v_lru_lambda, g_qk=v_g_qk,
                     g_mix_out=v_g_mix_out, w_out=v_w_out)

    c_all, gn_all, cw_all = _exchange([(c, 0), (g_norm, 0), (conv_w, 0)], [], "gather_small_weights")
    c_all = c_all.reshape(N_DEV * batch, d)
    n_ada = w_ada.shape[-1]
    g_norm_full = gn_all.transpose(1, 2, 0, 3).reshape(n_layers, 3, d)
    conv_w_full = cw_all.transpose(1, 2, 0, 3).reshape(n_layers, 4, LRU_W)

    b_ada_loc = lax.dynamic_slice_in_dim(b_ada, me * n_ada, n_ada, axis=1)
    silu = lambda t: t * _sigmoid(t)

    def bias_epilogue(p, e_refs, o_refs):
        o_refs[0][...] = p + e_refs[0][...]

    mod_loc = []
    for l in range(n_layers):
        (ml,) = _mm(c_all, w_ada, mode="nn", tm=c_all.shape[0], tn=n_ada, tk=d, b_lead=(l,), a_pre=silu,
                    name=f"ada_{l}", extras=[(b_ada_loc[l][None, :], (1, n_ada), lambda i, j: (0, 0))],
                    outs=[((c_all.shape[0], n_ada), F32, (c_all.shape[0], n_ada), lambda i, j: (0, 0))],
                    epilogue=bias_epilogue)
        mod_loc.append(ml)
    (mod_all,) = _exchange([(jnp.stack(mod_loc), 0)], [], "gather_mod")
    mod_all = mod_all.transpose(1, 2, 0, 3).reshape(n_layers, N_DEV * batch, 9 * d)

    cast = lambda w, token: (w + token[0, 0]).astype(BF16)
    ffn_ops = lambda l, f, token: [(cast(w_ffn_up[l, f], token), 0, "gather"),
                                   (cast(w_ffn_down[l, f], token), 0, "gather")]
    mix_ops = lambda l, token: [(cast(jnp.pad(w_in[l], ((0, 0), (0, N_IN_PAD - N_IN))), token), 0, "gather"),
                                (cast(w_out[l], token), 0, "gather")]
    behind = lambda w, token: w + token[0, 0].astype(BF16)
    flights, landed_rest = {}, []

    def start(key, ops):
        flights[key], token = _flight_start(ops, f"weights_{key}_start")
        return token

    def wait(key, after):
        return _flight_wait(flights[key], after, f"weights_{key}_wait")

    def big_weights(l, part, after):
        if (l, part) == (0, "ffn0"):
            (wup,), landed = wait("up", after)
            token = start("down", ffn_ops(0, 0, landed)[1:])

            def wdown(after_up):
                (wd,), landed_down = wait("down", after_up)
                return behind(wd, start("mix", mix_ops(0, landed_down))).reshape(D_FF, d)

            return behind(wup, token), wdown
        if (l, part) == (0, "mix"):
            (wi, wo), landed = wait("mix", after)
            rest = ffn_ops(0, 1, landed)
            for ll in range(1, n_layers):
                rest += ffn_ops(ll, 0, landed) + mix_ops(ll, landed) + ffn_ops(ll, 1, landed)
            return behind(wi, start("rest", rest)).reshape(d, N_IN_PAD), wo.reshape(d, d)
        if not landed_rest:
            landed_rest.extend(wait("rest", after)[0])
        at = 0 if l == 0 else 2 + 6 * (l - 1) + {"ffn0": 0, "mix": 2, "ffn1": 4}[part]
        first, second = landed_rest[at], landed_rest[at + 1]
        if part == "mix":
            return first.reshape(d, N_IN_PAD), second.reshape(d, d)
        return first, second.reshape(D_FF, d)

    w_up_first, mod_all = lax.optimization_barrier((w_ffn_up[0, 0].astype(BF16), mod_all))
    token = start("up", [(w_up_first, 0, "gather")])
    mod_me = lax.dynamic_slice_in_dim(mod_all + token[0, 0], me * batch, batch, axis=1)
    mod_me = mod_me.reshape(n_layers, batch, 3, 3, d)

    wts = dict(g_norm=g_norm_full, conv_w=conv_w_full, conv_b=conv_b, w_rgate=w_rgate, b_rgate=b_rgate, w_igate=w_igate, b_igate=b_igate,
               lru_lambda=lru_lambda, g_qk=g_qk, g_mix_out=g_mix_out, b_fgate=b_fgate)
    loss_part, grad_x, handles, small = _local_step(x, loss_target, mod_me, wts, big_weights)

    dmod_me = small.pop("dmod").reshape(n_layers, batch, 9 * d)
    small["b_ada"] = jnp.sum(dmod_me, axis=1)
    small_shapes = [(1,)] + [weights[k].shape if k not in ("g_norm", "conv_w") else small[k].shape for k in SMALL_NAMES]
    small_flat = _flatten([loss_part.reshape(1)] + [small[k] for k in SMALL_NAMES], 16 * LANES)
    dmod_all, small_all = _exchange([(dmod_me, 0), (small_flat, 0)], [], "gather_small")
    landed = {key: _flight_wait(h, small_all, f"grads_{key[0]}_{key[1]}_wait")[0] for key, h in handles.items()}
    layer_range = range(n_layers)
    p_up = jnp.stack([jnp.stack([landed[(l, "ffn0")][0], landed[(l, "ffn1")][0]]) for l in layer_range])
    p_down = jnp.stack([jnp.stack([landed[(l, "ffn0")][1], landed[(l, "ffn1")][1]]) for l in layer_range])
    p_in = jnp.stack([landed[(l, "mix")][0] for l in layer_range])
    p_out = jnp.stack([landed[(l, "mix")][1] for l in layer_range])
    small_sum = _unflatten(_sum_parts(small_all), small_shapes)
    loss = small_sum[0].reshape(())
    small_grads = dict(zip(SMALL_NAMES, small_sum[1:]))
    small_grads["g_norm"] = lax.dynamic_slice_in_dim(small_grads["g_norm"], me * g_norm.shape[-1], g_norm.shape[-1], 2)
    small_grads["conv_w"] = lax.dynamic_slice_in_dim(small_grads["conv_w"], me * conv_w.shape[-1], conv_w.shape[-1], 2)

    dmod_all = dmod_all.transpose(1, 0, 2, 3).reshape(n_layers, N_DEV * batch, 9 * d)
    dmod_loc = lax.dynamic_slice_in_dim(dmod_all, me * n_ada, n_ada, axis=2)
    g_ada = []
    for l in range(n_layers):
        (gl,) = _mm(c_all, dmod_loc[l], mode="tn", tm=d, tn=n_ada, tk=c_all.shape[0], a_pre=silu, name=f"dw_ada_{l}",
                    outs=[((d, n_ada), F32, (d, n_ada), lambda i, j: (0, 0))], epilogue=_store_epilogue([F32]))
        g_ada.append(gl)
    g_ada = jnp.stack(g_ada)

    results = {}

    def update(name, parts):
        shape = weights[name].shape
        as3d = lambda t: t.reshape((-1,) + shape[-2:])
        outs = _adamw(parts.reshape((-1,) + parts.shape[-3:]), as3d(weights[name]), as3d(moments_m[name]),
                      as3d(moments_v[name]), f"adamw_{name}")
        results[name] = [o.reshape(shape) for o in outs]

    update("w_ada", g_ada[:, None])
    update("w_ffn_up", p_up)
    update("w_ffn_down", p_down)
    update("w_in", p_in[..., :N_IN])
    update("w_out", p_out)
    sm_shapes = [weights[k].shape for k in SMALL_NAMES]
    flat = lambda src: _flatten([src[k] for k in SMALL_NAMES], 16 * LANES)
    sm_out = _adamw(flat(small_grads)[None, None], flat(weights)[None], flat(moments_m)[None], flat(moments_v)[None],
                    "adamw_small")
    for k, vals in zip(SMALL_NAMES, zip(*[_unflatten(o, sm_shapes) for o in sm_out])):
        results[k] = list(vals)

    outs = [loss, grad_x]
    for n in range(4):
        outs += [results[k][n] for k in WEIGHT_NAMES]
    return tuple(outs)
```

```python
import functools
import math

import jax
import jax.numpy as jnp
from jax import lax
from jax.experimental import pallas as pl
from jax.experimental.pallas import tpu as pltpu

F32 = jnp.float32
BF16 = jnp.bfloat16

N_DEV = 8
D_MODEL = 1024
D_FF = 2816
FF_SHARD = 2 * D_FF // N_DEV
N_FF_SHARD = D_FF // FF_SHARD
HEAD_DIM = 64
LRU_W = 512
ATT_W = 256
N_IN = 2564
N_IN_PAD = 2688
LANES = 128
SUBLANES = 8
BLK = 256
TQ = 512
KB_PER_Q = TQ // BLK
EPS = 1e-6
LRU_C = 8.0
NEG_BIG = -1e30
VMEM_LIMIT_BYTES = 48 * 1024 * 1024

ADAM_LR, ADAM_B1, ADAM_B2, ADAM_EPS, ADAM_WD, ADAM_STEP = 0.001, 0.9, 0.999, 1e-08, 0.01, 10

COL_SBQ, COL_SBK, COL_SBV = 8, 10, 12
COL_FXV, COL_FXF = 18, 20

NN = (((1,), (0,)), ((), ()))
NT = (((1,), (1,)), ((), ()))
TN = (((0,), (0,)), ((), ()))


def _params(n_axes):
    return pltpu.CompilerParams(dimension_semantics=("arbitrary",) * n_axes, vmem_limit_bytes=VMEM_LIMIT_BYTES)


def _tok_tile(seq):
    for t in (512, 256, 128):
        if seq % t == 0:
            return t
    raise ValueError(f"sequence length {seq} is not a multiple of 128")


def _dot(a, b, dims=NN):
    return lax.dot_general(a, b, dims, preferred_element_type=F32)


def _sigmoid(x):
    return 1.0 / (1.0 + jnp.exp(-x))


def _softplus(x):
    return jnp.maximum(x, 0.0) + jnp.log(1.0 + jnp.exp(-jnp.abs(x)))


def _gelu_parts(x):
    k0, k1 = math.sqrt(2.0 / math.pi), 0.044715
    t = jnp.tanh(k0 * (x + k1 * x * x * x))
    gelu = 0.5 * x * (1.0 + t)
    dgelu = 0.5 * (1.0 + t) + 0.5 * x * (1.0 - t * t) * k0 * (1.0 + 3.0 * k1 * x * x)
    return gelu, dgelu


def _neg_expm1(x):
    series = -x * (1.0 + x * (0.5 + x * (1.0 / 6.0 + x * (1.0 / 24.0 + x * (1.0 / 120.0 + x * (1.0 / 720.0))))))
    return jnp.where(x > -0.25, series, 1.0 - jnp.exp(x))


def _split2(x):
    hi = x.astype(BF16)
    lo = (x - hi.astype(F32)).astype(BF16)
    return hi, lo


def _split3(x):
    hi = x.astype(BF16)
    r = x - hi.astype(F32)
    mid = r.astype(BF16)
    lo = (r - mid.astype(F32)).astype(BF16)
    return hi, mid, lo


def _rows_to_block(rows, width):
    r = lax.broadcasted_iota(jnp.int32, (SUBLANES, width), 0)
    out = jnp.zeros((SUBLANES, width), F32)
    for n, v in enumerate(rows):
        out = jnp.where(r == n, jnp.broadcast_to(v, (SUBLANES, width)), out)
    return out


def _colsum(x):
    return jnp.sum(x, axis=0, keepdims=True)


def _exchange(gathers, scatters, name, two_level=False):
    assert not (two_level and scatters)
    n_g = len(gathers)
    ops = [a for a, _ in gathers] + [a for a, _ in scatters]
    n = len(ops)
    out_shape = [jax.ShapeDtypeStruct(a.shape[:nl] + (N_DEV,) + a.shape[nl:], a.dtype) for a, nl in gathers]
    out_shape += [jax.ShapeDtypeStruct(a.shape, a.dtype) for a, _ in scatters]
    items = []
    for k, (a, nl) in enumerate(list(gathers) + list(scatters)):
        for flat in range(math.prod(a.shape[:nl])):
            idx, rem = [], flat
            for dim in reversed(a.shape[:nl]):
                idx.insert(0, rem % dim)
                rem //= dim
            items.append((k, tuple(idx)))
    n_items = len(items)

    def body(*refs):
        ins, outs = refs[:n], refs[n:2 * n]
        send_sems, recv_sems, local_sems = refs[2 * n:]
        x, y, c = lax.axis_index("x"), lax.axis_index("y"), lax.axis_index("c")
        me = 4 * x + 2 * y + c

        def at(ref, idx):
            return ref.at[idx] if idx else ref

        def src(it, peer):
            k, idx = items[it]
            return at(ins[k], idx) if k < n_g else at(ins[k], idx + (peer,))

        def slot(it, s):
            k, idx = items[it]
            return at(outs[k], idx + (s,))

        def remote(it, rel, source, s, to):
            return pltpu.make_async_remote_copy(
                src_ref=source, dst_ref=slot(it, s), send_sem=send_sems.at[it, rel], recv_sem=recv_sems.at[it, rel],
                device_id=to, device_id_type=pl.DeviceIdType.MESH)

        local = [pltpu.make_async_copy(src(it, me), slot(it, me), local_sems.at[it]) for it in range(n_items)]
        for cp in local:
            cp.start()

        if not two_level:
            started = []
            for r in range(1, N_DEV):
                px = 1 - x if (r >> 2) & 1 else x
                py = 1 - y if (r >> 1) & 1 else y
                pc = 1 - c if r & 1 else c
                for it in range(n_items):
                    cp = remote(it, r - 1, src(it, 4 * px + 2 * py + pc), me, (px, py, pc))
                    cp.start()
                    started.append(cp)
            for cp in started:
                cp.wait()
        else:
            sibling, chips = (x, y, 1 - c), [(1 - x, y), (x, 1 - y), (1 - x, 1 - y)]
            sib = 4 * x + 2 * y + (1 - c)
            started = []
            for it in range(n_items):
                started.append(remote(it, 0, src(it, me), me, sibling))
                started += [remote(it, 1 + j, src(it, me), me, (cx, cy, c)) for j, (cx, cy) in enumerate(chips)]
            for cp in started:
                cp.start()
            for j, (cx, cy) in enumerate(chips):
                s = 4 * cx + 2 * cy + c
                for it in range(n_items):
                    remote(it, 1 + j, slot(it, s), s, sibling).wait_recv()
                    cp = remote(it, 4 + j, slot(it, s), s, sibling)
                    cp.start()
                    started.append(cp)
            for it in range(n_items):
                remote(it, 0, slot(it, sib), sib, sibling).wait_recv()
                for j, (cx, cy) in enumerate(chips):
                    s = 4 * cx + 2 * cy + (1 - c)
                    remote(it, 4 + j, slot(it, s), s, sibling).wait_recv()
            for cp in started:
                cp.wait_send()
        for cp in local:
            cp.wait()

    hbm = pl.BlockSpec(memory_space=pltpu.HBM)
    return pl.pallas_call(
        body, name=name, out_shape=out_shape,
        in_specs=[hbm] * n, out_specs=[hbm] * n,
        scratch_shapes=[pltpu.SemaphoreType.DMA((n_items, N_DEV - 1)), pltpu.SemaphoreType.DMA((n_items, N_DEV - 1)),
                        pltpu.SemaphoreType.DMA((n_items,))],
    )(*ops)


def _lead_items(ops):
    items = []
    for k, (a, nl) in enumerate(ops):
        for flat in range(math.prod(a.shape[:nl])):
            idx, rem = [], flat
            for dim in reversed(a.shape[:nl]):
                idx.insert(0, rem % dim)
                rem //= dim
            items.append((k, tuple(idx)))
    return items


def _flight_copies(ops, srcs, lands, send_sems, recv_sems):
    x, y, c = lax.axis_index("x"), lax.axis_index("y"), lax.axis_index("c")
    me = 4 * x + 2 * y + c
    copies = []
    for r in range(1, N_DEV):
        px = 1 - x if (r >> 2) & 1 else x
        py = 1 - y if (r >> 1) & 1 else y
        pc = 1 - c if r & 1 else c
        for it, (k, idx) in enumerate(_lead_items([(a, nl) for a, nl, _ in ops])):
            src = srcs[k].at[idx + (4 * px + 2 * py + pc,)] if ops[k][2] == "scatter" else (
                srcs[k].at[idx] if idx else srcs[k])
            copies.append(pltpu.make_async_remote_copy(
                src_ref=src, dst_ref=lands[k].at[idx + (me,)],
                send_sem=send_sems.at[it * (N_DEV - 1) + r - 1], recv_sem=recv_sems.at[it * (N_DEV - 1) + r - 1],
                device_id=(px, py, pc), device_id_type=pl.DeviceIdType.MESH))
    return copies


def _flight_start(ops, name):
    n = len(ops)
    me = 4 * lax.axis_index("x") + 2 * lax.axis_index("y") + lax.axis_index("c")
    srcs, lands = [], []
    for a, nl, kind in ops:
        if kind == "scatter":
            own, shape = lax.dynamic_slice_in_dim(a, me, 1, axis=nl), a.shape
        else:
            own, shape = jnp.expand_dims(a, nl), a.shape[:nl] + (N_DEV,) + a.shape[nl:]
        start = (0,) * nl + (me,) + (0,) * (len(shape) - nl - 1)
        lands.append(pltpu.with_memory_space_constraint(
            lax.dynamic_update_slice(lax.empty(shape, a.dtype), own, start), pltpu.HBM))
        srcs.append(pltpu.with_memory_space_constraint(a, pltpu.HBM))

    def body(*refs):
        for cp in _flight_copies(ops, refs[:n], refs[n:2 * n], refs[2 * n], refs[2 * n + 1]):
            cp.start()
        refs[-1][...] = jnp.zeros_like(refs[-1])

    hbm, sem = pl.BlockSpec(memory_space=pltpu.HBM), pl.BlockSpec(memory_space=pltpu.SEMAPHORE)
    n_items = len(_lead_items([(a, nl) for a, nl, _ in ops]))
    sems = pltpu.SemaphoreType.DMA((n_items * (N_DEV - 1),))
    res = pl.pallas_call(
        body, name=name,
        out_shape=[sems, sems] + [pltpu.HBM(a.shape, a.dtype) for a in srcs + lands]
        + [jax.ShapeDtypeStruct((SUBLANES, LANES), F32)],
        in_specs=[hbm] * (2 * n), out_specs=[sem, sem] + [hbm] * (2 * n) + [pl.BlockSpec(memory_space=pltpu.VMEM)],
        input_output_aliases={i: 2 + i for i in range(2 * n)},
        compiler_params=pltpu.CompilerParams(has_side_effects=pltpu.SideEffectType.DATAFLOW_SIDE_EFFECTING),
    )(*srcs, *lands)
    return (ops, res[0], res[1], res[2:2 + n], res[2 + n:2 + 2 * n]), res[-1]


def _flight_wait(handle, after, name):
    ops, send_sems, recv_sems, srcs, lands = handle
    n = len(ops)

    def body(*refs):
        for cp in _flight_copies(ops, refs[:n], refs[n:2 * n], refs[2 * n], refs[2 * n + 1]):
            cp.wait_send()
            cp.wait_recv()
        refs[-1][...] = jnp.zeros_like(refs[-1])

    hbm, sem = pl.BlockSpec(memory_space=pltpu.HBM), pl.BlockSpec(memory_space=pltpu.SEMAPHORE)
    res = pl.pallas_call(
        body, name=name,
        out_shape=[pltpu.HBM(a.shape, a.dtype) for a in list(srcs) + list(lands)]
        + [jax.ShapeDtypeStruct((SUBLANES, LANES), F32)],
        in_specs=[hbm] * (2 * n) + [sem, sem, pl.BlockSpec(memory_space=pl.ANY)],
        out_specs=[hbm] * (2 * n) + [pl.BlockSpec(memory_space=pltpu.VMEM)],
        input_output_aliases={i: i for i in range(2 * n)},
        compiler_params=pltpu.CompilerParams(has_side_effects=pltpu.SideEffectType.DATAFLOW_SIDE_EFFECTING),
    )(*srcs, *lands, send_sems, recv_sems, after)
    return res[n:2 * n], res[-1]


def _mm(a, b, *, mode, tm, tn, tk, outs, epilogue, name, extras=(), a_lead=(), b_lead=(), a_pre=None,
        a_spec=None, b_spec=None, shape=None, ksub=1):
    if shape is not None:
        mdim, ndim, kdim = shape
    else:
        if mode == "tn":
            kdim, mdim = a.shape[-2:]
        else:
            mdim, kdim = a.shape[-2:]
        ndim = b.shape[-2] if mode == "nt" else b.shape[-1]
    assert mdim % tm == 0 and ndim % tn == 0 and kdim % tk == 0, (name, mdim, ndim, kdim, tm, tn, tk)
    ni, nj, nk = mdim // tm, ndim // tn, kdim // tk
    a_lead, b_lead = tuple(a_lead), tuple(b_lead)
    a_block = (None,) * len(a_lead) + ((tk, tm) if mode == "tn" else (tm, tk))
    b_block = (None,) * len(b_lead) + ((tn, tk) if mode == "nt" else (tk, tn))
    dims = {"nn": NN, "nt": NT, "tn": TN}[mode]
    ne, no = len(extras), len(outs)

    def a_index(i, j, k):
        return a_lead + ((k, i) if mode == "tn" else (i, k))

    def b_index(i, j, k):
        return b_lead + ((j, k) if mode == "nt" else (k, j))

    if a_spec is not None:
        a_block, a_index = a_spec
    if b_spec is not None:
        b_block, b_index = b_spec

    def body(*refs):
        a_ref, b_ref = refs[0], refs[1]
        e_refs, o_refs = refs[2:2 + ne], refs[2 + ne:2 + ne + no]
        if ksub == 1:
            av = a_ref[...] if a_pre is None else a_pre(a_ref[...])
            p = _dot(av.astype(BF16), b_ref[...].astype(BF16), dims)
        else:
            p = _dot(a_ref[0], b_ref[0], dims)
            for s in range(1, ksub):
                p = p + _dot(a_ref[s], b_ref[s], dims)
        if nk == 1:
            epilogue(p, e_refs, o_refs)
        else:
            acc = refs[-1]
            k = pl.program_id(2)

            @pl.when(k == 0)
            def _():
                acc[...] = p

            @pl.when(k > 0)
            def _():
                acc[...] += p

            @pl.when(k == nk - 1)
            def _():
                epilogue(acc[...], e_refs, o_refs)

    in_specs = [pl.BlockSpec(a_block, a_index), pl.BlockSpec(b_block, b_index)]
    in_specs += [pl.BlockSpec(blk, functools.partial(lambda i, j, k, f: f(i, j), f=f)) for _, blk, f in extras]
    out_specs = [pl.BlockSpec(blk, functools.partial(lambda i, j, k, f: f(i, j), f=f)) for _, _, blk, f in outs]
    res = pl.pallas_call(
        body, name=name, grid=(ni, nj, nk), in_specs=in_specs, out_specs=out_specs,
        out_shape=[jax.ShapeDtypeStruct(s, d) for s, d, _, _ in outs],
        scratch_shapes=[pltpu.VMEM((tm, tn), F32)] if nk > 1 else [],
        compiler_params=_params(3),
    )(a, b, *[e[0] for e in extras])
    return res


def _store_epilogue(dtypes):
    def epi(p, e_refs, o_refs):
        for o, dt in zip(o_refs, dtypes):
            o[...] = p.astype(dt)
    return epi


def _normmod(x, gn, scale, shift, seq, name):
    m, d = x.shape
    tm = _tok_tile(seq)
    tpb = seq // tm

    def body(x_ref, gn_ref, sc_ref, sh_ref, h_ref):
        xv = x_ref[...]
        rstd = lax.rsqrt(jnp.mean(xv * xv, axis=-1, keepdims=True) + EPS)
        h_ref[...] = (xv * rstd * gn_ref[...] * (1.0 + sc_ref[0]) + sh_ref[0]).astype(BF16)

    vec = pl.BlockSpec((1, 1, d), lambda i: (i // tpb, 0, 0))
    return pl.pallas_call(
        body, name=name, grid=(m // tm,),
        in_specs=[pl.BlockSpec((tm, d), lambda i: (i, 0)), pl.BlockSpec((1, d), lambda i: (0, 0)), vec, vec],
        out_specs=pl.BlockSpec((tm, d), lambda i: (i, 0)),
        out_shape=jax.ShapeDtypeStruct((m, d), BF16), compiler_params=_params(1),
    )(x, gn, scale, shift)


def _normmod_bwd_epilogue(p, e_refs, o_refs):
    x_ref, dxo_ref, gn_ref, sc_ref = e_refs
    xv = x_ref[...]
    rstd = lax.rsqrt(jnp.mean(xv * xv, axis=-1, keepdims=True) + EPS)
    xhat = xv * rstd
    gn, sc1 = gn_ref[...], 1.0 + sc_ref[0]
    dxhat = p * (gn * sc1)
    dx = rstd * (dxhat - xhat * jnp.mean(dxhat * xhat, axis=-1, keepdims=True))
    o_refs[0][...] = dxo_ref[...] + dx
    t = p * xhat
    o_refs[1][0] = _rows_to_block([_colsum(p), _colsum(t * gn), _colsum(t * sc1)], p.shape[1])


def _residual_bwd(dx, f, gate, fac, seq, name):
    m, d = dx.shape
    tm = _tok_tile(seq)
    tpb = seq // tm

    def body(dx_ref, f_ref, g_ref, df_ref, dg_ref):
        dxv = dx_ref[...]
        df_ref[...] = ((fac * (1.0 + g_ref[0])) * dxv).astype(BF16)
        dg_ref[0] = _rows_to_block([_colsum((fac * dxv) * f_ref[...].astype(F32))], d)

    tile = pl.BlockSpec((tm, d), lambda i: (i, 0))
    return pl.pallas_call(
        body, name=name, grid=(m // tm,),
        in_specs=[tile, tile, pl.BlockSpec((1, 1, d), lambda i: (i // tpb, 0, 0))],
        out_specs=[tile, pl.BlockSpec((1, SUBLANES, d), lambda i: (i, 0, 0))],
        out_shape=[jax.ShapeDtypeStruct((m, d), BF16), jax.ShapeDtypeStruct((m // tm, SUBLANES, d), F32)],
        compiler_params=_params(1),
    )(dx, f, gate)


def _loss_head(y, target, seq):
    m, d = y.shape
    tm = _tok_tile(seq)

    def body(y_ref, t_ref, dy_ref, l_ref):
        err = y_ref[...] - t_ref[...]
        dy_ref[...] = err * (1.0 / d)
        part = 0.5 * jnp.sum(jnp.mean(err * err, axis=-1, keepdims=True), axis=0, keepdims=True)
        l_ref[0] = jnp.broadcast_to(part, (SUBLANES, LANES))

    tile = pl.BlockSpec((tm, d), lambda i: (i, 0))
    return pl.pallas_call(
        body, name="loss_head", grid=(m // tm,), in_specs=[tile, tile],
        out_specs=[tile, pl.BlockSpec((1, SUBLANES, LANES), lambda i: (i, 0, 0))],
        out_shape=[jax.ShapeDtypeStruct((m, d), F32), jax.ShapeDtypeStruct((m // tm, SUBLANES, LANES), F32)],
        compiler_params=_params(1),
    )(y, target)


def _ffn_fwd(x, h, wup, wdown, gate, seq, tag):
    m, d = x.shape
    tm = _tok_tile(seq)
    tpb = seq // tm

    def up_body(h_ref, wg_ref, wu_ref, a_ref, gu_ref):
        hv = h_ref[...]
        g, u = _dot(hv, wg_ref[...]), _dot(hv, wu_ref[...])
        sg = _sigmoid(g)
        silu = g * sg
        a_ref[...] = (silu * u).astype(BF16)
        gu_ref[0] = (u * (sg * (1.0 + g * (1.0 - sg)))).astype(BF16)
        gu_ref[1] = silu.astype(BF16)

    wblk = (None, d, FF_SHARD)
    a, gu = pl.pallas_call(
        up_body, name=f"ffn_up_{tag}", grid=(N_FF_SHARD, m // tm),
        in_specs=[pl.BlockSpec((tm, d), lambda j, i: (i, 0)),
                  pl.BlockSpec(wblk, lambda j, i: (j, 0, 0)),
                  pl.BlockSpec(wblk, lambda j, i: (j + N_FF_SHARD, 0, 0))],
        out_specs=[pl.BlockSpec((None, tm, FF_SHARD), lambda j, i: (j, i, 0)),
                   pl.BlockSpec((2, None, tm, FF_SHARD), lambda j, i: (0, j, i, 0))],
        out_shape=[jax.ShapeDtypeStruct((N_FF_SHARD, m, FF_SHARD), BF16),
                   jax.ShapeDtypeStruct((2, N_FF_SHARD, m, FF_SHARD), BF16)],
        compiler_params=_params(2),
    )(h, wup, wup)

    def down_epilogue(p, e_refs, o_refs):
        x_ref, g_ref = e_refs
        o_refs[0][...] = x_ref[...] + (0.5 * (1.0 + g_ref[0])) * p
        o_refs[1][...] = p.astype(BF16)

    if callable(wdown):
        wdown = wdown(a)
    wdown3 = wdown.reshape(N_FF_SHARD, FF_SHARD, d)
    x_out, f = _mm(a, wdown3, mode="nn", tm=tm, tn=d, tk=D_FF, ksub=N_FF_SHARD, name=f"ffn_down_{tag}",
                   shape=(m, d, D_FF), a_spec=((N_FF_SHARD, tm, FF_SHARD), lambda i, j, k: (0, i, 0)),
                   b_spec=((N_FF_SHARD, FF_SHARD, d), lambda i, j, k: (0, 0, 0)),
                   extras=[(x, (tm, d), lambda i, j: (i, 0)), (gate, (1, 1, d), lambda i, j: (i // tpb, 0, 0))],
                   outs=[((m, d), F32, (tm, d), lambda i, j: (i, 0)), ((m, d), BF16, (tm, d), lambda i, j: (i, 0))],
                   epilogue=down_epilogue)
    return x_out, (a, gu, f), wdown


def _ffn_bwd(dx_out, x, h, saved, wup, wdown, gn, scale, gate, seq, tag, on_grads):
    a, gu, f = saved
    m, d = x.shape
    tm = _tok_tile(seq)
    tpb = seq // tm
    df, dgate_parts = _residual_bwd(dx_out, f, gate, 0.5, seq, f"ffn_res_bwd_{tag}")

    def act_bwd_epilogue(p, e_refs, o_refs):
        o_refs[0][0] = (p * e_refs[0][0].astype(F32)).astype(BF16)
        o_refs[0][1] = (p * e_refs[0][1].astype(F32)).astype(BF16)

    gu_blk = (2, None, tm, FF_SHARD)
    (dgu,) = _mm(df, wdown, mode="nt", tm=tm, tn=FF_SHARD, tk=d, name=f"ffn_down_dx_{tag}", shape=(m, D_FF, d),
                 b_spec=((FF_SHARD, d), lambda i, j, k: (j, 0)),
                 extras=[(gu, gu_blk, lambda i, j: (0, j, i, 0))],
                 outs=[((2, N_FF_SHARD, m, FF_SHARD), BF16, gu_blk, lambda i, j: (0, j, i, 0))],
                 epilogue=act_bwd_epilogue)
    tt = 2 * tm if m % (2 * tm) == 0 else tm
    (dwdown,) = _mm(a, df, mode="tn", tm=FF_SHARD, tn=d, tk=tt, name=f"ffn_dwdown_{tag}", shape=(D_FF, d, m),
                    a_spec=((None, tt, FF_SHARD), lambda i, j, k: (i, k, 0)),
                    outs=[((D_FF, d), BF16, (FF_SHARD, d), lambda i, j: (i, 0))], epilogue=_store_epilogue([BF16]))
    dgu8 = dgu.reshape(2 * N_FF_SHARD, m, FF_SHARD)
    (dwup,) = _mm(h, dgu8, mode="tn", tm=d, tn=FF_SHARD, tk=tt, name=f"ffn_dwup_{tag}", shape=(d, 2 * D_FF, m),
                  b_spec=((None, tt, FF_SHARD), lambda i, j, k: (j, k, 0)),
                  outs=[((2 * N_FF_SHARD, d, FF_SHARD), BF16, (None, d, FF_SHARD), lambda i, j: (j, 0, 0))],
                  epilogue=_store_epilogue([BF16]))
    scale = scale + on_grads(dwup, dwdown)[0, 0]
    dx, nm_parts = _mm(dgu8, wup, mode="nt", tm=tm, tn=d, tk=D_FF, ksub=N_FF_SHARD, name=f"ffn_up_dx_{tag}",
                       shape=(m, d, 2 * D_FF), a_spec=((N_FF_SHARD, tm, FF_SHARD), lambda i, j, k: (k, i, 0)),
                       b_spec=((N_FF_SHARD, d, FF_SHARD), lambda i, j, k: (k, 0, 0)),
                       extras=[(x, (tm, d), lambda i, j: (i, 0)), (dx_out, (tm, d), lambda i, j: (i, 0)),
                               (gn, (1, d), lambda i, j: (0, 0)), (scale, (1, 1, d), lambda i, j: (i // tpb, 0, 0))],
                       outs=[((m, d), F32, (tm, d), lambda i, j: (i, 0)),
                             ((m // tm, SUBLANES, d), F32, (1, SUBLANES, d), lambda i, j: (i, 0, 0))],
                       epilogue=_normmod_bwd_epilogue)
    return dx, nm_parts, dgate_parts


def _shift_down(ext, n, rows):
    if n:
        ext = pltpu.roll(ext, n, 0)
    return ext[SUBLANES:SUBLANES + rows]


def _lru_gates(u, wr_ref, br_ref, wi_ref, bi_ref, lam_ref):
    ub = u.astype(BF16)
    r = _sigmoid(_dot(ub, wr_ref[...]) + br_ref[...])
    ig = _sigmoid(_dot(ub, wi_ref[...]) + bi_ref[...])
    sp = _softplus(-lam_ref[...])
    log_a = (-LRU_C * r) * sp
    a = jnp.exp(log_a)
    mult = jnp.sqrt(_neg_expm1(2.0 * log_a))
    return r, ig, sp, a, mult


def _conv(ext, cw_ref, cb_ref, rows):
    u = cb_ref[...] + cw_ref[3:4, :] * _shift_down(ext, 0, rows)
    for k in range(3):
        u = u + cw_ref[k:k + 1, :] * _shift_down(ext, 3 - k, rows)
    return u


def _lru_halo_spec(seq, ts):
    return pl.BlockSpec((SUBLANES, LRU_W),
                        lambda b, i: (jnp.maximum(b * (seq // SUBLANES) + i * (ts // SUBLANES) - 1, 0), 0))


def _lru_fwd(proj32, conv_w, conv_b, wr, br, wi, bi, lam, batch, seq):
    m = proj32.shape[0]
    ts = _tok_tile(seq)
    nt = seq // ts
    row = lambda b, i: (b * nt + i, 0)

    def body(x_ref, halo_ref, g_ref, cw_ref, cb_ref, wr_ref, br_ref, wi_ref, bi_ref, lam_ref,
             y_ref, h_ref, a_scr, b_scr, carry):
        i = pl.program_id(1)
        halo = jnp.where(i > 0, halo_ref[...], 0.0)
        ext = jnp.concatenate([halo, x_ref[...]], axis=0)
        u = _conv(ext, cw_ref, cb_ref, ts)
        _, ig, _, a, mult = _lru_gates(u, wr_ref, br_ref, wi_ref, bi_ref, lam_ref)
        a_scr[...] = a
        b_scr[...] = mult * (ig * u)

        @pl.when(i == 0)
        def _():
            carry[...] = jnp.zeros_like(carry)

        rid = lax.broadcasted_iota(jnp.int32, (SUBLANES, LRU_W), 0)

        def chunk(c, hprev):
            off = pl.multiple_of(c * SUBLANES, SUBLANES)
            av, bv = a_scr[pl.ds(off, SUBLANES), :], b_scr[pl.ds(off, SUBLANES), :]
            for d in (1, 2, 4):
                keep = rid >= d
                bv = jnp.where(keep, av * pltpu.roll(bv, d, 0) + bv, bv)
                av = jnp.where(keep, av * pltpu.roll(av, d, 0), av)
            h = av * hprev + bv
            h_ref[pl.ds(off, SUBLANES), :] = h
            return h[SUBLANES - 1:SUBLANES, :]

        carry[...] = lax.fori_loop(0, ts // SUBLANES, chunk, carry[...])
        gelu, _ = _gelu_parts(g_ref[...])
        y_ref[...] = h_ref[...] * gelu

    full = lambda shape: pl.BlockSpec(shape, lambda b, i: (0,) * len(shape))
    return pl.pallas_call(
        body, name="lru_fwd", grid=(batch, nt),
        in_specs=[pl.BlockSpec((ts, LRU_W), row), _lru_halo_spec(seq, ts),
                  pl.BlockSpec((ts, LRU_W), lambda b, i: (b * nt + i, 1)),
                  full((4, LRU_W)), full((1, LRU_W)), full((LRU_W, LRU_W)), full((1, LRU_W)),
                  full((LRU_W, LRU_W)), full((1, LRU_W)), full((1, LRU_W))],
        out_specs=[pl.BlockSpec((ts, LRU_W), row), pl.BlockSpec((ts, LRU_W), row)],
        out_shape=[jax.ShapeDtypeStruct((m, LRU_W), F32), jax.ShapeDtypeStruct((m, LRU_W), F32)],
        scratch_shapes=[pltpu.VMEM((ts, LRU_W), F32), pltpu.VMEM((ts, LRU_W), F32), pltpu.VMEM((1, LRU_W), F32)],
        compiler_params=_params(2),
    )(proj32, proj32, proj32, conv_w, conv_b, wr, br, wi, bi, lam)


def _lru_bwd(dy, proj32, h, conv_w, conv_b, wr, br, wi, bi, lam, batch, seq):
    m = proj32.shape[0]
    ts = _tok_tile(seq)
    nt = seq // ts
    row = lambda b, i: (b * nt + (nt - 1 - i), 0)
    halo = pl.BlockSpec((SUBLANES, LRU_W),
                        lambda b, i: (jnp.maximum(b * (seq // SUBLANES) + (nt - 1 - i) * (ts // SUBLANES) - 1, 0), 0))

    def body(dy_ref, x_ref, xhalo_ref, g_ref, h_ref, hhalo_ref, cw_ref, cb_ref, wr_ref, br_ref, wi_ref, bi_ref,
             lam_ref, dx_ref, dg_ref, dwr_ref, dwi_ref, sums_ref, a_scr, dh_scr, g_scr, carry, du_next):
        b, i = pl.program_id(0), pl.program_id(1)
        first_tile = i == nt - 1

        @pl.when((b == 0) & (i == 0))
        def _():
            dwr_ref[...] = jnp.zeros_like(dwr_ref)
            dwi_ref[...] = jnp.zeros_like(dwi_ref)
            sums_ref[...] = jnp.zeros_like(sums_ref)

        @pl.when(i == 0)
        def _():
            carry[...] = jnp.zeros_like(carry)
            du_next[...] = jnp.zeros_like(du_next)

        xhalo = jnp.where(first_tile, 0.0, xhalo_ref[...])
        ext = jnp.concatenate([xhalo, x_ref[...]], axis=0)
        u = _conv(ext, cw_ref, cb_ref, ts)
        r, ig, sp, a, mult = _lru_gates(u, wr_ref, br_ref, wi_ref, bi_ref, lam_ref)
        gelu, dgelu = _gelu_parts(g_ref[...])
        dyv, hv = dy_ref[...], h_ref[...]
        dg_ref[...] = (dyv * hv * dgelu).astype(BF16)
        a_scr[...] = a
        dh_scr[...] = dyv * gelu

        rid = lax.broadcasted_iota(jnp.int32, (SUBLANES, LRU_W), 0)
        nchunk = ts // SUBLANES

        def chunk(n, cg):
            off = pl.multiple_of((nchunk - 1 - n) * SUBLANES, SUBLANES)
            av, beta = a_scr[pl.ds(off, SUBLANES), :], dh_scr[pl.ds(off, SUBLANES), :]
            alpha = jnp.where(rid == SUBLANES - 1, 1.0, pltpu.roll(av, SUBLANES - 1, 0))
            for d in (1, 2, 4):
                keep = rid + d <= SUBLANES - 1
                beta = jnp.where(keep, beta + alpha * pltpu.roll(beta, SUBLANES - d, 0), beta)
                alpha = jnp.where(keep, alpha * pltpu.roll(alpha, SUBLANES - d, 0), alpha)
            gv = beta + alpha * cg
            g_scr[pl.ds(off, SUBLANES), :] = gv
            return av[0:1, :] * gv[0:1, :]

        carry[...] = lax.fori_loop(0, nchunk, chunk, carry[...])
        gv = g_scr[...]
        hhalo = jnp.where(first_tile, 0.0, hhalo_ref[...])
        hprev = _shift_down(jnp.concatenate([hhalo, hv], axis=0), 1, ts)
        dmult = gv * ig * u
        dig = gv * mult * u
        du = gv * mult * ig
        dlog_a = gv * hprev * a - dmult * a * a / mult
        dr = dlog_a * (-LRU_C * sp)
        dr_pre = dr * r * (1.0 - r)
        di_pre = dig * ig * (1.0 - ig)
        drb, dib, ub = dr_pre.astype(BF16), di_pre.astype(BF16), u.astype(BF16)
        du = du + _dot(drb, wr_ref[...], NT) + _dot(dib, wi_ref[...], NT)
        dwr_ref[...] += _dot(ub, drb, TN)
        dwi_ref[...] += _dot(ub, dib, TN)

        ext_du = jnp.concatenate([du, du_next[...]], axis=0)
        du_next[...] = du[0:SUBLANES, :]
        n_ext = ts + SUBLANES
        dx = cw_ref[3:4, :] * du
        sums = [_colsum(dr_pre), _colsum(di_pre), _colsum(dlog_a * (-LRU_C * r)), _colsum(du)]
        dcw = []
        for k in range(3):
            dx = dx + cw_ref[k:k + 1, :] * pltpu.roll(ext_du, n_ext - (3 - k), 0)[0:ts]
            dcw.append(_colsum(du * _shift_down(ext, 3 - k, ts)))
        dcw.append(_colsum(du * _shift_down(ext, 0, ts)))
        dx_ref[...] = dx.astype(BF16)
        sums_ref[...] += _rows_to_block(sums + dcw, LRU_W)

    full = lambda shape: pl.BlockSpec(shape, lambda b, i: (0,) * len(shape))
    tile = pl.BlockSpec((ts, LRU_W), row)
    return pl.pallas_call(
        body, name="lru_bwd", grid=(batch, nt),
        in_specs=[tile, tile, halo, pl.BlockSpec((ts, LRU_W), lambda b, i: (b * nt + (nt - 1 - i), 1)), tile, halo,
                  full((4, LRU_W)), full((1, LRU_W)), full((LRU_W, LRU_W)), full((1, LRU_W)),
                  full((LRU_W, LRU_W)), full((1, LRU_W)), full((1, LRU_W))],
        out_specs=[tile, tile, full((LRU_W, LRU_W)), full((LRU_W, LRU_W)), full((SUBLANES, LRU_W))],
        out_shape=[jax.ShapeDtypeStruct((m, LRU_W), BF16), jax.ShapeDtypeStruct((m, LRU_W), BF16),
                   jax.ShapeDtypeStruct((LRU_W, LRU_W), F32), jax.ShapeDtypeStruct((LRU_W, LRU_W), F32),
                   jax.ShapeDtypeStruct((SUBLANES, LRU_W), F32)],
        scratch_shapes=[pltpu.VMEM((ts, LRU_W), F32), pltpu.VMEM((ts, LRU_W), F32), pltpu.VMEM((ts, LRU_W), F32),
                        pltpu.VMEM((1, LRU_W), F32), pltpu.VMEM((SUBLANES, LRU_W), F32)],
        compiler_params=_params(2),
    )(dy, proj32, proj32, proj32, h, h, conv_w, conv_b, wr, br, wi, bi, lam)


def _head_masks():
    lane = lax.broadcasted_iota(jnp.int32, (1, LANES), 1)
    return lane < HEAD_DIM


def _stack_heads(x2):
    lo, zero = _head_masks(), jnp.zeros_like(x2)
    return jnp.concatenate([jnp.where(lo, x2, zero), jnp.where(lo, zero, x2)], axis=0)


def _unstack_heads(y):
    return jnp.where(_head_masks(), y[:TQ], y[TQ:])


def _stack_cols(a, b):
    return jnp.concatenate([a, b], axis=0)


def _causal(qi, kb, strict):
    r = jnp.bitwise_and(lax.broadcasted_iota(jnp.int32, (2 * TQ, BLK), 0), TQ - 1) + qi * TQ
    c = lax.broadcasted_iota(jnp.int32, (2 * TQ, BLK), 1) + kb * BLK
    return (c < r) if strict else (c <= r)


def _key_loop(qi, group, carry, descending=False):
    def trip(n, cr):
        done = [n * KB_PER_Q + j for j in range(KB_PER_Q)]
        return group([qi * KB_PER_Q - 1 - t for t in done] if descending else done, cr)

    return lax.fori_loop(0, qi, trip, carry)


def _one_by_one(block):
    def group(kbs, carry):
        for kb in kbs:
            carry = block(kb, carry, False)
        return carry
    return group


def _tri(cmp):
    r = lax.broadcasted_iota(jnp.int32, (BLK, BLK), 0)
    c = lax.broadcasted_iota(jnp.int32, (BLK, BLK), 1)
    return cmp(r, c)


def _dot_split(x, tri):
    hi, lo = _split2(x)
    return _dot(hi, tri) + _dot(lo, tri)


def _sb_fwd(proj16, batch, seq):
    nq = seq // TQ
    scale = HEAD_DIM ** -0.5

    def body(q_ref, k_ref, v_ref, y_ref, t_ref):
        qi = pl.program_id(2)
        qs = _stack_heads(q_ref[0])
        tri_after = _tri(lambda r, c: r > c).astype(BF16)

        def block(kb, carry, masked):
            acc, c = carry
            ks = pl.multiple_of(kb * BLK, BLK)
            k2, v2 = k_ref[0, pl.ds(ks, BLK), :], v_ref[0, pl.ds(ks, BLK), :]
            z = _dot(qs, k2, NT) * scale
            sp = _softplus(z)
            l = -sp
            if masked:
                valid = _causal(qi, kb, True)
                l = jnp.where(valid, l, 0.0)
            w = jnp.exp((z - sp) + _dot_split(l, tri_after) + c)
            if masked:
                w = jnp.where(valid, w, 0.0)
            return acc + _dot(w.astype(BF16), v2), c + jnp.sum(l, axis=1, keepdims=True)

        def group(kbs, carry):
            acc, c = carry
            kv = [(k_ref[0, pl.ds(pl.multiple_of(kb * BLK, BLK), BLK), :],
                   v_ref[0, pl.ds(pl.multiple_of(kb * BLK, BLK), BLK), :]) for kb in kbs]
            zs = [_dot(qs, k2, NT) * scale for k2, _ in kv]
            sps = [_softplus(z) for z in zs]
            afters = [_dot_split(-sp, tri_after) for sp in sps]
            for z, sp, after, (_, v2) in zip(zs, sps, afters, kv):
                acc = acc + _dot(jnp.exp((z - sp) + after + c).astype(BF16), v2)
                c = c - jnp.sum(sp, axis=1, keepdims=True)
            return acc, c

        carry = (jnp.zeros((2 * TQ, LANES), F32), jnp.zeros((2 * TQ, 1), F32))
        first = qi * KB_PER_Q
        for n in reversed(range(KB_PER_Q)):
            carry = block(first + n, carry, True)
        acc, c = _key_loop(qi, group, carry, descending=True)
        y_ref[...] = _unstack_heads(acc)
        t_ref[0] = _unstack_heads(jnp.broadcast_to(c, (2 * TQ, LANES)))

    m = batch * seq
    return pl.pallas_call(
        body, name="sb_fwd", grid=(batch, 2, nq),
        in_specs=[pl.BlockSpec((1, TQ, LANES), lambda b, p, q: (b, q, COL_SBQ + p)),
                  pl.BlockSpec((1, seq, LANES), lambda b, p, q: (b, 0, COL_SBK + p)),
                  pl.BlockSpec((1, seq, LANES), lambda b, p, q: (b, 0, COL_SBV + p))],
        out_specs=[pl.BlockSpec((TQ, LANES), lambda b, p, q: (b * nq + q, p)),
                   pl.BlockSpec((1, TQ, LANES), lambda b, p, q: (p, b * nq + q, 0))],
        out_shape=[jax.ShapeDtypeStruct((m, ATT_W), F32), jax.ShapeDtypeStruct((2, m, LANES), F32)],
        compiler_params=_params(3),
    )(proj16, proj16, proj16)


def _sb_bwd(dy, t, proj16, batch, seq):
    nq = seq // TQ
    scale = HEAD_DIM ** -0.5

    def body(dy_ref, t_ref, q_ref, k_ref, v_ref, dq_ref, dk_ref, dv_ref):
        qi = pl.program_id(2)

        @pl.when(qi == 0)
        def _():
            dk_ref[...] = jnp.zeros_like(dk_ref)
            dv_ref[...] = jnp.zeros_like(dv_ref)

        t2 = t_ref[0]
        qs, dys = _stack_heads(q_ref[0]), _stack_heads(dy_ref[...].astype(BF16))
        tot = _stack_cols(t2[:, 0:1], t2[:, HEAD_DIM:HEAD_DIM + 1])
        tri_incl = _tri(lambda r, c: r <= c).astype(BF16)
        tri_excl = _tri(lambda r, c: r < c).astype(BF16)

        def block(kb, carry, masked):
            dq, pc, ec = carry
            ks = pl.multiple_of(kb * BLK, BLK)
            k2, v2 = k_ref[0, pl.ds(ks, BLK), :], v_ref[0, pl.ds(ks, BLK), :]
            z = _dot(qs, k2, NT) * scale
            sp = _softplus(z)
            l, b = -sp, z - sp
            sig = jnp.exp(b)
            if masked:
                valid = _causal(qi, kb, True)
                l = jnp.where(valid, l, 0.0)
            after = tot - (pc + _dot_split(l, tri_incl))
            w = jnp.exp(b + after)
            if masked:
                w = jnp.where(valid, w, 0.0)
            e = _dot(dys, v2, NT) * w
            et = ec + _dot_split(e, tri_excl)
            dz = e * (1.0 - sig) - et * sig
            if masked:
                dz = jnp.where(valid, dz, 0.0)
            dzb = (dz * scale).astype(BF16)
            dk_ref[0, pl.ds(ks, BLK), :] += _dot(dzb, qs, TN)
            dv_ref[0, pl.ds(ks, BLK), :] += _dot(w.astype(BF16), dys, TN)
            return (dq + _dot(dzb, k2), pc + jnp.sum(l, axis=1, keepdims=True),
                    ec + jnp.sum(e, axis=1, keepdims=True))

        def group(kbs, carry):
            dq, pc, ec = carry
            starts = [pl.multiple_of(kb * BLK, BLK) for kb in kbs]
            kv = [(k_ref[0, pl.ds(ks, BLK), :], v_ref[0, pl.ds(ks, BLK), :]) for ks in starts]
            zs = [_dot(qs, k2, NT) * scale for k2, _ in kv]
            dws = [_dot(dys, v2, NT) for _, v2 in kv]
            sps = [_softplus(z) for z in zs]
            pins = [_dot_split(-sp, tri_incl) for sp in sps]
            es, ws, sigs = [], [], []
            for z, sp, pin, dw in zip(zs, sps, pins, dws):
                b = z - sp
                w = jnp.exp(b + (tot - (pc + pin)))
                pc = pc - jnp.sum(sp, axis=1, keepdims=True)
                es.append(dw * w)
                ws.append(w)
                sigs.append(jnp.exp(b))
            eins = [_dot_split(e, tri_excl) for e in es]
            for ks, (k2, _), e, w, sig, ein in zip(starts, kv, es, ws, sigs, eins):
                dzb = ((e * (1.0 - sig) - (ec + ein) * sig) * scale).astype(BF16)
                ec = ec + jnp.sum(e, axis=1, keepdims=True)
                dk_ref[0, pl.ds(ks, BLK), :] += _dot(dzb, qs, TN)
                dv_ref[0, pl.ds(ks, BLK), :] += _dot(w.astype(BF16), dys, TN)
                dq = dq + _dot(dzb, k2)
            return dq, pc, ec

        col = jnp.zeros((2 * TQ, 1), F32)
        first = qi * KB_PER_Q
        carry = _key_loop(qi, group, (jnp.zeros((2 * TQ, LANES), F32), col, col))
        for n in range(KB_PER_Q):
            carry = block(first + n, carry, True)
        dq_ref[...] = _unstack_heads(carry[0])

    m = batch * seq
    whole = lambda col: pl.BlockSpec((1, seq, LANES), lambda b, p, q: (b, 0, col + p))
    return pl.pallas_call(
        body, name="sb_bwd", grid=(batch, 2, nq),
        in_specs=[pl.BlockSpec((TQ, LANES), lambda b, p, q: (b * nq + q, p)),
                  pl.BlockSpec((1, TQ, LANES), lambda b, p, q: (p, b * nq + q, 0)),
                  pl.BlockSpec((1, TQ, LANES), lambda b, p, q: (b, q, COL_SBQ + p)),
                  whole(COL_SBK), whole(COL_SBV)],
        out_specs=[pl.BlockSpec((TQ, LANES), lambda b, p, q: (b * nq + q, p)), whole(0), whole(0)],
        out_shape=[jax.ShapeDtypeStruct((m, ATT_W), F32), jax.ShapeDtypeStruct((batch, seq, ATT_W), F32),
                   jax.ShapeDtypeStruct((batch, seq, ATT_W), F32)],
        compiler_params=_params(3),
    )(dy, t, proj16, proj16, proj16)


def _fox_pre(proj32, gq, gk, bf, group_mean, batch, seq):
    m = proj32.shape[0]
    ts = _tok_tile(seq)
    nt = seq // ts

    def body(q_ref, k_ref, f_ref, gq_ref, gk_ref, bf_ref, gm_ref, fq_ref, fk_ref, fc_ref, carry):
        i = pl.program_id(1)

        @pl.when(i == 0)
        def _():
            carry[...] = jnp.zeros_like(carry)

        gm = gm_ref[...]
        for src, g_ref, dst in ((q_ref, gq_ref, fq_ref), (k_ref, gk_ref, fk_ref)):
            v = src[...]
            ms = _dot_split(v * v, gm)
            dst[...] = (v * lax.rsqrt(ms + EPS) * g_ref[...]).astype(BF16)
        z = f_ref[...] + bf_ref[...]
        lf = jnp.minimum(z, 0.0) - jnp.log(1.0 + jnp.exp(-jnp.abs(z)))
        r = lax.broadcasted_iota(jnp.int32, (ts, ts), 0)
        c = lax.broadcasted_iota(jnp.int32, (ts, ts), 1)
        tri = (r >= c).astype(BF16)
        hi, mid, low = _split3(lf)
        fc = _dot(tri, hi) + _dot(tri, mid) + _dot(tri, low) + carry[...]
        fc_ref[...] = fc
        carry[...] = fc[ts - 1:ts, :]

    full = lambda shape: pl.BlockSpec(shape, lambda b, i: (0,) * len(shape))
    return pl.pallas_call(
        body, name="fox_pre", grid=(batch, nt),
        in_specs=[pl.BlockSpec((ts, ATT_W), lambda b, i: (b * nt + i, 7)),
                  pl.BlockSpec((ts, ATT_W), lambda b, i: (b * nt + i, 8)),
                  pl.BlockSpec((ts, LANES), lambda b, i: (b * nt + i, COL_FXF)),
                  full((1, ATT_W)), full((1, ATT_W)), full((1, LANES)), full((ATT_W, ATT_W))],
        out_specs=[pl.BlockSpec((ts, ATT_W), lambda b, i: (b * nt + i, 0)),
                   pl.BlockSpec((ts, ATT_W), lambda b, i: (b * nt + i, 0)),
                   pl.BlockSpec((ts, LANES), lambda b, i: (b * nt + i, 0))],
        out_shape=[jax.ShapeDtypeStruct((m, ATT_W), BF16), jax.ShapeDtypeStruct((m, ATT_W), BF16),
                   jax.ShapeDtypeStruct((m, LANES), F32)],
        scratch_shapes=[pltpu.VMEM((1, LANES), F32)],
        compiler_params=_params(2),
    )(proj32, proj32, proj32, gq, gk, bf, group_mean)


def _fox_specs(batch, seq):
    nq = seq // TQ
    return dict(
        qblk=pl.BlockSpec((1, TQ, LANES), lambda b, p, q: (b, q, p)),
        whole=pl.BlockSpec((1, seq, LANES), lambda b, p, q: (b, 0, p)),
        vwhole=pl.BlockSpec((1, seq, LANES), lambda b, p, q: (b, 0, COL_FXV + p)),
        fcol=pl.BlockSpec((1, 1, TQ, 2), lambda b, p, q: (b, p, q, 0)),
        frow=pl.BlockSpec((1, 1, 2, seq), lambda b, p, q: (b, p, 0, 0)),
        rows=pl.BlockSpec((TQ, LANES), lambda b, p, q: (b * nq + q, p)),
        stat=pl.BlockSpec((1, TQ, LANES), lambda b, p, q: (p, b * nq + q, 0)),
    )


def _fox_logits(qs, k2, fq_col, fr_ref, ks, is_a, scale):
    fk_row = jnp.where(is_a, fr_ref[0, 0, 0:1, pl.ds(ks, BLK)], fr_ref[0, 0, 1:2, pl.ds(ks, BLK)])
    return _dot(qs, k2, NT) * scale + fq_col - fk_row


def _fox_fwd(fq, fk, proj16, fcol, frow, batch, seq):
    nq = seq // TQ
    scale = HEAD_DIM ** -0.5

    def body(q_ref, k_ref, v_ref, fc_ref, fr_ref, y_ref, lse_ref):
        qi = pl.program_id(2)
        qs = _stack_heads(q_ref[0])
        fcv = fc_ref[0, 0]
        fq_col = _stack_cols(fcv[:, 0:1], fcv[:, 1:2])
        is_a = lax.broadcasted_iota(jnp.int32, (2 * TQ, 1), 0) < TQ

        def block(kb, carry, masked):
            acc, mx, den = carry
            ks = pl.multiple_of(kb * BLK, BLK)
            k2, v2 = k_ref[0, pl.ds(ks, BLK), :], v_ref[0, pl.ds(ks, BLK), :]
            s = _fox_logits(qs, k2, fq_col, fr_ref, ks, is_a, scale)
            if masked:
                s = jnp.where(_causal(qi, kb, False), s, NEG_BIG)
            mx_new = jnp.maximum(mx, jnp.max(s, axis=1, keepdims=True))
            p = jnp.exp(s - mx_new)
            alpha = jnp.exp(mx - mx_new)
            return (alpha * acc + _dot(p.astype(BF16), v2), mx_new, alpha * den + jnp.sum(p, axis=1, keepdims=True))

        first = qi * KB_PER_Q
        carry = (jnp.zeros((2 * TQ, LANES), F32), jnp.full((2 * TQ, 1), NEG_BIG, F32), jnp.zeros((2 * TQ, 1), F32))
        carry = _key_loop(qi, _one_by_one(block),carry)
        for n in range(KB_PER_Q):
            carry = block(first + n, carry, True)
        acc, mx, den = carry
        y_ref[...] = _unstack_heads(acc / den)
        lse_ref[0] = _unstack_heads(jnp.broadcast_to(mx + jnp.log(den), (2 * TQ, LANES)))

    m = batch * seq
    sp = _fox_specs(batch, seq)
    return pl.pallas_call(
        body, name="fox_fwd", grid=(batch, 2, nq),
        in_specs=[sp["qblk"], sp["whole"], sp["vwhole"], sp["fcol"], sp["frow"]],
        out_specs=[sp["rows"], sp["stat"]],
        out_shape=[jax.ShapeDtypeStruct((m, ATT_W), F32), jax.ShapeDtypeStruct((2, m, LANES), F32)],
        compiler_params=_params(3),
    )(fq, fk, proj16, fcol, frow)


def _fox_bwd(dy, y, lse, fq, fk, proj16, fcol, frow, batch, seq):
    nq = seq // TQ
    scale = HEAD_DIM ** -0.5

    def body(dy_ref, y_ref, lse_ref, q_ref, k_ref, v_ref, fc_ref, fr_ref, dq_ref, dk_ref, dv_ref, dfr_ref, dfc_ref):
        qi = pl.program_id(2)

        @pl.when(qi == 0)
        def _():
            dk_ref[...] = jnp.zeros_like(dk_ref)
            dv_ref[...] = jnp.zeros_like(dv_ref)
            dfr_ref[...] = jnp.zeros_like(dfr_ref)

        lo = _head_masks()
        lane = lax.broadcasted_iota(jnp.int32, (1, LANES), 1)
        dy2, lse2, fcv = dy_ref[...], lse_ref[0], fc_ref[0, 0]
        qs, dys = _stack_heads(q_ref[0]), _stack_heads(dy2.astype(BF16))
        dyy = dy2 * y_ref[...]
        delta = _stack_cols(jnp.sum(jnp.where(lo, dyy, 0.0), axis=1, keepdims=True),
                            jnp.sum(jnp.where(lo, 0.0, dyy), axis=1, keepdims=True))
        lse_col = _stack_cols(lse2[:, 0:1], lse2[:, HEAD_DIM:HEAD_DIM + 1])
        fq_col = _stack_cols(fcv[:, 0:1], fcv[:, 1:2])
        is_a = lax.broadcasted_iota(jnp.int32, (2 * TQ, 1), 0) < TQ

        def block(kb, carry, masked):
            dq, rs = carry
            ks = pl.multiple_of(kb * BLK, BLK)
            k2, v2 = k_ref[0, pl.ds(ks, BLK), :], v_ref[0, pl.ds(ks, BLK), :]
            p = jnp.exp(_fox_logits(qs, k2, fq_col, fr_ref, ks, is_a, scale) - lse_col)
            if masked:
                p = jnp.where(_causal(qi, kb, False), p, 0.0)
            ds = p * (_dot(dys, v2, NT) - delta)
            dsb = (ds * scale).astype(BF16)
            dk_ref[0, pl.ds(ks, BLK), :] += _dot(dsb, qs, TN)
            dv_ref[0, pl.ds(ks, BLK), :] += _dot(p.astype(BF16), dys, TN)
            dfr_ref[0, 0, 0:1, pl.ds(ks, BLK)] -= jnp.sum(ds[:TQ], axis=0, keepdims=True)
            dfr_ref[0, 0, 1:2, pl.ds(ks, BLK)] -= jnp.sum(ds[TQ:], axis=0, keepdims=True)
            return dq + _dot(dsb, k2), rs + jnp.sum(ds, axis=1, keepdims=True)

        first = qi * KB_PER_Q
        carry = _key_loop(qi, _one_by_one(block),(jnp.zeros((2 * TQ, LANES), F32), jnp.zeros((2 * TQ, 1), F32)))
        for n in range(KB_PER_Q):
            carry = block(first + n, carry, True)
        dq, rs = carry
        dq_ref[...] = _unstack_heads(dq)
        dfc_ref[0] = jnp.where(lane == 0, rs[:TQ], jnp.where(lane == 1, rs[TQ:], 0.0))

    m = batch * seq
    sp = _fox_specs(batch, seq)
    return pl.pallas_call(
        body, name="fox_bwd", grid=(batch, 2, nq),
        in_specs=[sp["rows"], sp["rows"], sp["stat"], sp["qblk"], sp["whole"], sp["vwhole"], sp["fcol"], sp["frow"]],
        out_specs=[sp["rows"], sp["whole"], sp["whole"],
                   pl.BlockSpec((1, 1, SUBLANES, seq), lambda b, p, q: (b, p, 0, 0)), sp["stat"]],
        out_shape=[jax.ShapeDtypeStruct((m, ATT_W), F32), jax.ShapeDtypeStruct((batch, seq, ATT_W), F32),
                   jax.ShapeDtypeStruct((batch, seq, ATT_W), F32),
                   jax.ShapeDtypeStruct((batch, 2, SUBLANES, seq), F32), jax.ShapeDtypeStruct((2, m, LANES), F32)],
        compiler_params=_params(3),
    )(dy, y, lse, fq, fk, proj16, fcol, frow)


def _fox_post_bwd(dfq, dfk, dfc, proj32, gq, gk, bf, group_mean, batch, seq):
    m = proj32.shape[0]
    ts = _tok_tile(seq)
    nt = seq // ts
    tile = lambda w, col: pl.BlockSpec((ts, w), lambda b, i: (b * nt + (nt - 1 - i), col))

    def body(dfq_ref, dfk_ref, dfc_ref, q_ref, k_ref, f_ref, gq_ref, gk_ref, bf_ref, gm_ref,
             dq_ref, dk_ref, df_ref, gs_ref, bs_ref, carry):
        i = pl.program_id(1)

        @pl.when(i == 0)
        def _():
            carry[...] = jnp.zeros_like(carry)

        gm = gm_ref[...]
        rows = []
        for src, g_ref, d_ref, dst in ((q_ref, gq_ref, dfq_ref, dq_ref), (k_ref, gk_ref, dfk_ref, dk_ref)):
            v, dv = src[...], d_ref[...]
            rstd = lax.rsqrt(_dot_split(v * v, gm) + EPS)
            vhat = v * rstd
            rows.append(_colsum(dv * vhat))
            dvh = dv * g_ref[...]
            dst[...] = (rstd * (dvh - vhat * _dot_split(dvh * vhat, gm))).astype(BF16)
        gs_ref[0] = _rows_to_block(rows, ATT_W)

        dfc_v = dfc_ref[...]
        r = lax.broadcasted_iota(jnp.int32, (ts, ts), 0)
        c = lax.broadcasted_iota(jnp.int32, (ts, ts), 1)
        tri = (r <= c).astype(BF16)
        hi, mid, low = _split3(dfc_v)
        dlf = _dot(tri, hi) + _dot(tri, mid) + _dot(tri, low) + carry[...]
        carry[...] = dlf[0:1, :]
        z = f_ref[...] + bf_ref[...]
        dz = dlf * _sigmoid(-z)
        df_ref[...] = dz.astype(BF16)
        bs_ref[0] = _rows_to_block([_colsum(dz)], LANES)

    full = lambda shape: pl.BlockSpec(shape, lambda b, i: (0,) * len(shape))
    part = lambda w: pl.BlockSpec((1, SUBLANES, w), lambda b, i: (b * nt + (nt - 1 - i), 0, 0))
    return pl.pallas_call(
        body, name="fox_post_bwd", grid=(batch, nt),
        in_specs=[tile(ATT_W, 0), tile(ATT_W, 0), tile(LANES, 0), tile(ATT_W, 7), tile(ATT_W, 8), tile(LANES, COL_FXF),
                  full((1, ATT_W)), full((1, ATT_W)), full((1, LANES)), full((ATT_W, ATT_W))],
        out_specs=[tile(ATT_W, 0), tile(ATT_W, 0), tile(LANES, 0), part(ATT_W), part(LANES)],
        out_shape=[jax.ShapeDtypeStruct((m, ATT_W), BF16), jax.ShapeDtypeStruct((m, ATT_W), BF16),
                   jax.ShapeDtypeStruct((m, LANES), BF16),
                   jax.ShapeDtypeStruct((batch * nt, SUBLANES, ATT_W), F32),
                   jax.ShapeDtypeStruct((batch * nt, SUBLANES, LANES), F32)],
        scratch_shapes=[pltpu.VMEM((1, LANES), F32)],
        compiler_params=_params(2),
    )(dfq, dfk, dfc, proj32, proj32, proj32, gq, gk, bf, group_mean)


_GROUPS = ((0, LRU_W), (LRU_W, LRU_W + ATT_W), (LRU_W + ATT_W, LRU_W + 2 * ATT_W))


def _outnorm(y_lru, y_sb, y_fox, gmix, seq):
    m = y_lru.shape[0]
    tm = _tok_tile(seq)

    def body(a_ref, b_ref, c_ref, g_ref, o_ref):
        parts = []
        for ref in (a_ref, b_ref, c_ref):
            v = ref[...]
            parts.append(v * lax.rsqrt(jnp.mean(v * v, axis=-1, keepdims=True) + EPS))
        o_ref[...] = (jnp.concatenate(parts, axis=1) * g_ref[...]).astype(BF16)

    t = lambda w: pl.BlockSpec((tm, w), lambda i: (i, 0))
    return pl.pallas_call(
        body, name="outnorm", grid=(m // tm,),
        in_specs=[t(LRU_W), t(ATT_W), t(ATT_W), pl.BlockSpec((1, D_MODEL), lambda i: (0, 0))],
        out_specs=t(D_MODEL), out_shape=jax.ShapeDtypeStruct((m, D_MODEL), BF16), compiler_params=_params(1),
    )(y_lru, y_sb, y_fox, gmix)


def _outnorm_bwd_epilogue(p, e_refs, o_refs):
    gmix = e_refs[3][...]
    dg = []
    for n, (lo, hi) in enumerate(_GROUPS):
        v, dyn = e_refs[n][...], p[:, lo:hi]
        rstd = lax.rsqrt(jnp.mean(v * v, axis=-1, keepdims=True) + EPS)
        vhat = v * rstd
        dg.append(_colsum(dyn * vhat))
        dvh = dyn * gmix[:, lo:hi]
        o_refs[n][...] = rstd * (dvh - vhat * jnp.mean(dvh * vhat, axis=-1, keepdims=True))
    o_refs[3][0] = _rows_to_block([jnp.concatenate(dg, axis=1)], p.shape[1])


def _pair_layouts(fcum, batch, seq):
    f4 = fcum[:, :4].reshape(batch, seq, 2, 2)
    return f4.transpose(0, 2, 1, 3), f4.transpose(0, 2, 3, 1)


def _gate_grad_cols(dfr, dfc, batch, seq):
    keys = dfr[:, :, :2, :].transpose(0, 3, 1, 2).reshape(batch * seq, 4)
    queries = dfc[:, :, :2].transpose(1, 0, 2).reshape(batch * seq, 4)
    return jnp.pad(keys + queries, ((0, 0), (0, LANES - 4)))


def _mixer_fwd(x, h, w, gate, batch, seq):
    m, d = x.shape
    tm = _tok_tile(seq)
    tpb = seq // tm

    def in_epilogue(p, e_refs, o_refs):
        o_refs[0][...] = p
        o_refs[1][...] = p.astype(BF16)

    tn_in = 896
    proj32, proj16 = _mm(h, w["w_in"], mode="nn", tm=tm, tn=tn_in, tk=d, name="mix_in",
                         outs=[((m, N_IN_PAD), F32, (tm, tn_in), lambda i, j: (i, j)),
                               ((m, N_IN_PAD), BF16, (tm, tn_in), lambda i, j: (i, j))],
                         epilogue=in_epilogue)
    y_lru, h_lru = _lru_fwd(proj32, w["conv_w"], w["conv_b"], w["wr"], w["br"], w["wi"], w["bi"], w["lam"], batch, seq)
    p16 = proj16.reshape(batch, seq, N_IN_PAD)
    y_sb, t_sb = _sb_fwd(p16, batch, seq)
    fq, fk, fcum = _fox_pre(proj32, w["gq"], w["gk"], w["bf"], w["group_mean"], batch, seq)
    fcol, frow = _pair_layouts(fcum, batch, seq)
    fq3, fk3 = fq.reshape(batch, seq, ATT_W), fk.reshape(batch, seq, ATT_W)
    y_fox, lse = _fox_fwd(fq3, fk3, p16, fcol, frow, batch, seq)
    ynorm = _outnorm(y_lru, y_sb, y_fox, w["gmix"], seq)

    def out_epilogue(p, e_refs, o_refs):
        x_ref, g_ref = e_refs
        o_refs[0][...] = x_ref[...] + (1.0 + g_ref[0]) * p
        o_refs[1][...] = p.astype(BF16)

    x_out, out = _mm(ynorm, w["w_out"], mode="nn", tm=tm, tn=d, tk=d, name="mix_out",
                     extras=[(x, (tm, d), lambda i, j: (i, 0)), (gate, (1, 1, d), lambda i, j: (i // tpb, 0, 0))],
                     outs=[((m, d), F32, (tm, d), lambda i, j: (i, 0)), ((m, d), BF16, (tm, d), lambda i, j: (i, 0))],
                     epilogue=out_epilogue)
    saved = dict(proj32=proj32, p16=p16, h_lru=h_lru, y_lru=y_lru, y_sb=y_sb, t_sb=t_sb, fq3=fq3, fk3=fk3,
                 fcol=fcol, frow=frow, y_fox=y_fox, lse=lse, ynorm=ynorm, out=out)
    return x_out, saved


def _mixer_bwd(dx_out, x, h, s, w, gn, scale, gate, batch, seq, on_grads):
    m, d = x.shape
    tm = _tok_tile(seq)
    tpb = seq // tm
    dout, dgate_parts = _residual_bwd(dx_out, s["out"], gate, 1.0, seq, "mix_res_bwd")
    (dw_out,) = _mm(s["ynorm"], dout, mode="tn", tm=d, tn=d, tk=tm, name="mix_dwout",
                    outs=[((d, d), BF16, (d, d), lambda i, j: (i, j))], epilogue=_store_epilogue([BF16]))
    dy_lru, dy_sb, dy_fox, gmix_parts = _mm(
        dout, w["w_out"], mode="nt", tm=tm, tn=d, tk=d, name="mix_out_dx",
        extras=[(s["y_lru"], (tm, LRU_W), lambda i, j: (i, 0)), (s["y_sb"], (tm, ATT_W), lambda i, j: (i, 0)),
                (s["y_fox"], (tm, ATT_W), lambda i, j: (i, 0)), (w["gmix"], (1, d), lambda i, j: (0, 0))],
        outs=[((m, LRU_W), F32, (tm, LRU_W), lambda i, j: (i, 0)), ((m, ATT_W), F32, (tm, ATT_W), lambda i, j: (i, 0)),
              ((m, ATT_W), F32, (tm, ATT_W), lambda i, j: (i, 0)),
              ((m // tm, SUBLANES, d), F32, (1, SUBLANES, d), lambda i, j: (i, 0, 0))],
        epilogue=_outnorm_bwd_epilogue)

    dsq, dsk, dsv = _sb_bwd(dy_sb, s["t_sb"], s["p16"], batch, seq)
    dfq, dfk, dfv, dfr, dfc = _fox_bwd(dy_fox, s["y_fox"], s["lse"], s["fq3"], s["fk3"], s["p16"], s["fcol"],
                                       s["frow"], batch, seq)
    dfc_cols = _gate_grad_cols(dfr, dfc, batch, seq)
    dxq, dxk, dxf, gqk_parts, bf_parts = _fox_post_bwd(dfq, dfk.reshape(m, ATT_W), dfc_cols, s["proj32"],
                                                       w["gq"], w["gk"], w["bf"], w["group_mean"], batch, seq)
    dlx, dlg, dwr, dwi, lru_sums = _lru_bwd(dy_lru, s["proj32"], s["h_lru"], w["conv_w"], w["conv_b"], w["wr"],
                                            w["br"], w["wi"], w["bi"], w["lam"], batch, seq)
    dproj = jnp.concatenate([dlx, dlg, dsq.astype(BF16), dsk.reshape(m, ATT_W).astype(BF16),
                             dsv.reshape(m, ATT_W).astype(BF16), dxq, dxk, dfv.reshape(m, ATT_W).astype(BF16), dxf],
                            axis=1)
    tn_in = 896
    (dw_in,) = _mm(h, dproj, mode="tn", tm=d, tn=tn_in, tk=tm, name="mix_dwin",
                   outs=[((d, N_IN_PAD), BF16, (d, tn_in), lambda i, j: (i, j))], epilogue=_store_epilogue([BF16]))
    scale = scale + on_grads(dw_in, dw_out)[0, 0]
    dx, nm_parts = _mm(dproj, w["w_in"], mode="nt", tm=tm, tn=d, tk=tn_in, name="mix_in_dx",
                       extras=[(x, (tm, d), lambda i, j: (i, 0)), (dx_out, (tm, d), lambda i, j: (i, 0)),
                               (gn, (1, d), lambda i, j: (0, 0)), (scale, (1, 1, d), lambda i, j: (i // tpb, 0, 0))],
                       outs=[((m, d), F32, (tm, d), lambda i, j: (i, 0)),
                             ((m // tm, SUBLANES, d), F32, (1, SUBLANES, d), lambda i, j: (i, 0, 0))],
                       epilogue=_normmod_bwd_epilogue)
    grads = dict(dwr=dwr, dwi=dwi, lru_sums=lru_sums, gmix_parts=gmix_parts,
                 gqk_parts=gqk_parts, bf_parts=bf_parts)
    return dx, grads, nm_parts, dgate_parts


def _block_diag(w):
    nb = w.shape[0]
    eye = jnp.eye(nb, dtype=w.dtype)
    return (eye[:, None, :, None] * w[:, :, None, :]).reshape(nb * HEAD_DIM, nb * HEAD_DIM)


def _block_diag_grad(g):
    nb = LRU_W // HEAD_DIM
    g4 = g.reshape(nb, HEAD_DIM, nb, HEAD_DIM)
    return jnp.stack([g4[n, :, n, :] for n in range(nb)])


def _per_batch(parts, batch, row):
    r = parts[:, row, :]
    return r.reshape(batch, -1, r.shape[-1]).sum(axis=1)


def _local_step(x3, target3, mod, wts, big_weights):
    batch, seq, d = x3.shape
    assert seq % TQ == 0, seq
    m = batch * seq
    n_layers = mod.shape[0]
    x = x3.reshape(m, d)
    group_mean = _block_diag(jnp.full((ATT_W // HEAD_DIM, HEAD_DIM, HEAD_DIM), 1.0 / HEAD_DIM, BF16))
    vec = lambda l, j, t: mod[l, :, j, t][:, None, :]

    layers, saved = [], []
    for l in range(n_layers):
        gq = jnp.tile(wts["g_qk"][l, 0], ATT_W // HEAD_DIM)[None, :]
        gk = jnp.tile(wts["g_qk"][l, 1], ATT_W // HEAD_DIM)[None, :]
        bf = jnp.pad(wts["b_fgate"][l], (0, LANES - 4))[None, :]
        lw = dict(conv_w=wts["conv_w"][l],
                  conv_b=wts["conv_b"][l][None, :], wr=_block_diag(wts["w_rgate"][l]).astype(BF16),
                  br=wts["b_rgate"][l][None, :], wi=_block_diag(wts["w_igate"][l]).astype(BF16),
                  bi=wts["b_igate"][l][None, :], lam=wts["lru_lambda"][l][None, :], gq=gq, gk=gk, bf=bf,
                  group_mean=group_mean, gmix=wts["g_mix_out"][l][None, :])
        layers.append(lw)
        gn = lambda j: wts["g_norm"][l, j][None, :]
        sv = dict(x0=x)
        sv["h0"] = _normmod(x, gn(0), vec(l, 0, 1), vec(l, 0, 0), seq, f"normmod_{l}_0")
        wup, wdown = big_weights(l, "ffn0", sv["h0"])
        x, sv["ffn0"], wdown = _ffn_fwd(x, sv["h0"], wup, wdown, vec(l, 0, 2), seq, f"{l}_0")
        sv["w_ffn0"] = (wup, wdown)
        sv["x1"] = x
        sv["h1"] = _normmod(x, gn(1), vec(l, 1, 1), vec(l, 1, 0), seq, f"normmod_{l}_1")
        lw["w_in"], lw["w_out"] = big_weights(l, "mix", sv["h1"])
        x, sv["mix"] = _mixer_fwd(x, sv["h1"], lw, vec(l, 1, 2), batch, seq)
        sv["x2"] = x
        sv["h2"] = _normmod(x, gn(2), vec(l, 2, 1), vec(l, 2, 0), seq, f"normmod_{l}_2")
        wup, wdown = big_weights(l, "ffn1", sv["h2"])
        x, sv["ffn1"], wdown = _ffn_fwd(x, sv["h2"], wup, wdown, vec(l, 2, 2), seq, f"{l}_1")
        sv["w_ffn1"] = (wup, wdown)
        saved.append(sv)

    dx, loss_parts = _loss_head(x, target3.reshape(m, d), seq)
    loss = jnp.sum(loss_parts[:, 0, 0])

    handles = {}

    def scatter(key, shapes):
        def on_grads(*grads):
            ops = [(g.reshape(shape), 0, "scatter") for g, shape in zip(grads, shapes)]
            handles[key], token = _flight_start(ops, f"grads_{key[0]}_{key[1]}_start")
            return token
        return on_grads

    ffn_shapes = ((2 * N_FF_SHARD, d, FF_SHARD), (N_DEV, D_FF // N_DEV, d))
    mix_shapes = ((N_DEV, d // N_DEV, N_IN_PAD), (N_DEV, d // N_DEV, d))
    small = {k: [] for k in ("dmod", "g_norm", "b_fgate", "conv_w", "conv_b", "w_rgate", "b_rgate", "w_igate",
                             "b_igate", "lru_lambda", "g_qk", "g_mix_out")}
    for l in reversed(range(n_layers)):
        sv, lw = saved[l], layers[l]
        gn = lambda j: wts["g_norm"][l, j][None, :]
        dx, nm2, dg2 = _ffn_bwd(dx, sv["x2"], sv["h2"], sv["ffn1"], *sv["w_ffn1"], gn(2), vec(l, 2, 1), vec(l, 2, 2),
                                seq, f"{l}_1", scatter((l, "ffn1"), ffn_shapes))
        dx, mg, nm1, dg1 = _mixer_bwd(dx, sv["x1"], sv["h1"], sv["mix"], lw, gn(1), vec(l, 1, 1), vec(l, 1, 2),
                                      batch, seq, scatter((l, "mix"), mix_shapes))
        dx, nm0, dg0 = _ffn_bwd(dx, sv["x0"], sv["h0"], sv["ffn0"], *sv["w_ffn0"], gn(0), vec(l, 0, 1), vec(l, 0, 2),
                                seq, f"{l}_0", scatter((l, "ffn0"), ffn_shapes))
        dmod_l, gnorm_l = [], []
        for nm, dg in ((nm0, dg0), (nm1, dg1), (nm2, dg2)):
            dmod_l.append(jnp.stack([_per_batch(nm, batch, 0), _per_batch(nm, batch, 1), _per_batch(dg, batch, 0)],
                                    axis=1))
            gnorm_l.append(jnp.sum(nm[:, 2, :], axis=0))
        small["dmod"].insert(0, jnp.stack(dmod_l, axis=1))
        small["g_norm"].insert(0, jnp.stack(gnorm_l))
        ls = mg["lru_sums"]
        small["b_rgate"].insert(0, ls[0])
        small["b_igate"].insert(0, ls[1])
        small["lru_lambda"].insert(0, ls[2] * (-_sigmoid(-wts["lru_lambda"][l])))
        small["conv_b"].insert(0, ls[3])
        small["conv_w"].insert(0, ls[4:8])
        small["w_rgate"].insert(0, _block_diag_grad(mg["dwr"]))
        small["w_igate"].insert(0, _block_diag_grad(mg["dwi"]))
        small["g_mix_out"].insert(0, jnp.sum(mg["gmix_parts"][:, 0, :], axis=0))
        gqk = jnp.sum(mg["gqk_parts"][:, :2, :], axis=0).reshape(2, ATT_W // HEAD_DIM, HEAD_DIM).sum(axis=1)
        small["g_qk"].insert(0, gqk)
        small["b_fgate"].insert(0, jnp.sum(mg["bf_parts"][:, 0, :4], axis=0))
    small = {k: jnp.stack(v) for k, v in small.items()}
    return loss, dx.reshape(batch, seq, d), handles, small


def _row_tile(rows, row_bytes):
    for t in (512, 256, 128, 64, 32, 16):
        if rows % t == 0 and t * row_bytes <= 4 * 1024 * 1024:
            return t
    return rows


def _adamw(parts, w, m, v, name):
    groups, n_parts, rows, cols = parts.shape
    tr = _row_tile(rows, cols * (n_parts * parts.dtype.itemsize + 7 * 4))
    c1 = 1.0 - ADAM_B1 ** ADAM_STEP
    c2 = 1.0 - ADAM_B2 ** ADAM_STEP

    def body(p_ref, w_ref, m_ref, v_ref, g_out, d_out, m_out, v_out):
        g = p_ref[0].astype(F32)
        for n in range(1, n_parts):
            g = g + p_ref[n].astype(F32)
        m_new = ADAM_B1 * m_ref[...] + (1.0 - ADAM_B1) * g
        v_new = ADAM_B2 * v_ref[...] + (1.0 - ADAM_B2) * (g * g)
        g_out[...] = g
        d_out[...] = -ADAM_LR * ((m_new / c1) / (jnp.sqrt(v_new / c2) + ADAM_EPS) + ADAM_WD * w_ref[...])
        m_out[...] = m_new
        v_out[...] = v_new

    tile = pl.BlockSpec((None, tr, cols), lambda g, i: (g, i, 0))
    return pl.pallas_call(
        body, name=name, grid=(groups, rows // tr),
        in_specs=[pl.BlockSpec((None, n_parts, tr, cols), lambda g, i: (g, 0, i, 0)), tile, tile, tile],
        out_specs=[tile] * 4, out_shape=[jax.ShapeDtypeStruct((groups, rows, cols), F32)] * 4,
        compiler_params=_params(2),
    )(parts, w, m, v)


def _sum_parts(parts):
    n_parts, rows, cols = parts.shape

    def body(p_ref, o_ref):
        acc = p_ref[0]
        for n in range(1, n_parts):
            acc = acc + p_ref[n]
        o_ref[...] = acc

    return pl.pallas_call(body, name="sum_small", out_shape=jax.ShapeDtypeStruct((rows, cols), F32),
                          compiler_params=pltpu.CompilerParams(vmem_limit_bytes=VMEM_LIMIT_BYTES))(parts)


def _flatten(arrays, multiple):
    chunks = []
    for a in arrays:
        flat = a.reshape(-1).astype(F32)
        chunks.append(jnp.pad(flat, (0, (-flat.shape[0]) % multiple)).reshape(-1, LANES))
    return jnp.concatenate(chunks, axis=0)


def _unflatten(flat2d, shapes, multiple):
    flat2d, out, row = flat2d.reshape(-1, LANES), [], 0
    for s in shapes:
        n = math.prod(s)
        rows = (n + multiple - 1) // multiple * (multiple // LANES)
        out.append(flat2d[row:row + rows].reshape(-1)[:n].reshape(s))
        row += rows
    return out


SMALL_NAMES = ("b_ada", "g_norm", "b_fgate", "conv_w", "conv_b", "w_rgate", "b_rgate", "w_igate", "b_igate",
               "lru_lambda", "g_qk", "g_mix_out")
WEIGHT_NAMES = ("w_ada", "b_ada", "g_norm", "w_ffn_up", "w_ffn_down", "w_in", "b_fgate", "conv_w", "conv_b",
                "w_rgate", "b_rgate", "w_igate", "b_igate", "lru_lambda", "g_qk", "g_mix_out", "w_out")


def kernel(x, c, w_ada, b_ada, g_norm, w_ffn_up, w_ffn_down, w_in, b_fgate, conv_w, conv_b, w_rgate, b_rgate, w_igate, b_igate, lru_lambda, g_qk, g_mix_out, w_out, loss_target, m_w_ada, m_b_ada, m_g_norm, m_w_ffn_up, m_w_ffn_down, m_w_in, m_b_fgate, m_conv_w, m_conv_b, m_w_rgate, m_b_rgate, m_w_igate, m_b_igate, m_lru_lambda, m_g_qk, m_g_mix_out, m_w_out, v_w_ada, v_b_ada, v_g_norm, v_w_ffn_up, v_w_ffn_down, v_w_in, v_b_fgate, v_conv_w, v_conv_b, v_w_rgate, v_b_rgate, v_w_igate, v_b_igate, v_lru_lambda, v_g_qk, v_g_mix_out, v_w_out):
    batch, seq, d = x.shape
    n_layers = w_ada.shape[0]
    me = 4 * lax.axis_index("x") + 2 * lax.axis_index("y") + lax.axis_index("c")
    weights = dict(w_ada=w_ada, b_ada=b_ada, g_norm=g_norm, w_ffn_up=w_ffn_up, w_ffn_down=w_ffn_down, w_in=w_in,
                   b_fgate=b_fgate, conv_w=conv_w, conv_b=conv_b, w_rgate=w_rgate, b_rgate=b_rgate, w_igate=w_igate,
                   b_igate=b_igate, lru_lambda=lru_lambda, g_qk=g_qk, g_mix_out=g_mix_out, w_out=w_out)
    moments_m = dict(w_ada=m_w_ada, b_ada=m_b_ada, g_norm=m_g_norm, w_ffn_up=m_w_ffn_up, w_ffn_down=m_w_ffn_down,
                     w_in=m_w_in, b_fgate=m_b_fgate, conv_w=m_conv_w, conv_b=m_conv_b, w_rgate=m_w_rgate,
                     b_rgate=m_b_rgate, w_igate=m_w_igate, b_igate=m_b_igate, lru_lambda=m_lru_lambda, g_qk=m_g_qk,
                     g_mix_out=m_g_mix_out, w_out=m_w_out)
    moments_v = dict(w_ada=v_w_ada, b_ada=v_b_ada, g_norm=v_g_norm, w_ffn_up=v_w_ffn_up, w_ffn_down=v_w_ffn_down,
                     w_in=v_w_in, b_fgate=v_b_fgate, conv_w=v_conv_w, conv_b=v_conv_b, w_rgate=v_w_rgate,
                     b_rgate=v_b_rgate, w_igate=v_w_igate, b_igate=v_b_igate, lru_lambda=v_lru_lambda, g_qk=v_g_qk,
                     g_mix_out=v_g_mix_out, w_out=v_w_out)

    c_all, gn_all, cw_all = _exchange([(c, 0), (g_norm, 0), (conv_w, 0)], [], "gather_small_weights")
    c_all = c_all.reshape(N_DEV * batch, d)
    n_ada = w_ada.shape[-1]
    g_norm_full = gn_all.transpose(1, 2, 0, 3).reshape(n_layers, 3, d)
    conv_w_full = cw_all.transpose(1, 2, 0, 3).reshape(n_layers, 4, LRU_W)

    b_ada_loc = lax.dynamic_slice_in_dim(b_ada, me * n_ada, n_ada, axis=1)
    silu = lambda t: t * _sigmoid(t)

    def bias_epilogue(p, e_refs, o_refs):
        o_refs[0][...] = p + e_refs[0][...]

    mod_loc = []
    for l in range(n_layers):
        (ml,) = _mm(c_all, w_ada, mode="nn", tm=c_all.shape[0], tn=n_ada, tk=d, b_lead=(l,), a_pre=silu,
                    name=f"ada_{l}", extras=[(b_ada_loc[l][None, :], (1, n_ada), lambda i, j: (0, 0))],
                    outs=[((c_all.shape[0], n_ada), F32, (c_all.shape[0], n_ada), lambda i, j: (0, 0))],
                    epilogue=bias_epilogue)
        mod_loc.append(ml)
    (mod_all,) = _exchange([(jnp.stack(mod_loc), 0)], [], "gather_mod")
    mod_all = mod_all.transpose(1, 2, 0, 3).reshape(n_layers, N_DEV * batch, 9 * d)

    cast = lambda w, token: (w + token[0, 0]).astype(BF16)
    ffn_ops = lambda l, f, token: [(cast(w_ffn_up[l, f], token), 0, "gather"),
                                   (cast(w_ffn_down[l, f], token), 0, "gather")]
    mix_ops = lambda l, token: [(cast(jnp.pad(w_in[l], ((0, 0), (0, N_IN_PAD - N_IN))), token), 0, "gather"),
                                (cast(w_out[l], token), 0, "gather")]
    behind = lambda w, token: w + token[0, 0].astype(BF16)
    flights, landed_rest = {}, []

    def start(key, ops):
        flights[key], token = _flight_start(ops, f"weights_{key}_start")
        return token

    def wait(key, after):
        return _flight_wait(flights[key], after, f"weights_{key}_wait")

    def big_weights(l, part, after):
        if (l, part) == (0, "ffn0"):
            (wup,), landed = wait("up", after)
            token = start("down", ffn_ops(0, 0, landed)[1:])

            def wdown(after_up):
                (wd,), landed_down = wait("down", after_up)
                return behind(wd, start("mix", mix_ops(0, landed_down))).reshape(D_FF, d)

            return behind(wup, token), wdown
        if (l, part) == (0, "mix"):
            (wi, wo), landed = wait("mix", after)
            rest = ffn_ops(0, 1, landed)
            for ll in range(1, n_layers):
                rest += ffn_ops(ll, 0, landed) + mix_ops(ll, landed) + ffn_ops(ll, 1, landed)
            return behind(wi, start("rest", rest)).reshape(d, N_IN_PAD), wo.reshape(d, d)
        if not landed_rest:
            landed_rest.extend(wait("rest", after)[0])
        at = 0 if l == 0 else 2 + 6 * (l - 1) + {"ffn0": 0, "mix": 2, "ffn1": 4}[part]
        first, second = landed_rest[at], landed_rest[at + 1]
        if part == "mix":
            return first.reshape(d, N_IN_PAD), second.reshape(d, d)
        return first, second.reshape(D_FF, d)

    w_up_first, mod_all = lax.optimization_barrier((w_ffn_up[0, 0].astype(BF16), mod_all))
    token = start("up", [(w_up_first, 0, "gather")])
    mod_me = lax.dynamic_slice_in_dim(mod_all + token[0, 0], me * batch, batch, axis=1)
    mod_me = mod_me.reshape(n_layers, batch, 3, 3, d)

    wts = dict(g_norm=g_norm_full, conv_w=conv_w_full, conv_b=conv_b, w_rgate=w_rgate, b_rgate=b_rgate, w_igate=w_igate, b_igate=b_igate,
               lru_lambda=lru_lambda, g_qk=g_qk, g_mix_out=g_mix_out, b_fgate=b_fgate)
    loss_part, grad_x, handles, small = _local_step(x, loss_target, mod_me, wts, big_weights)

    dmod_me = small.pop("dmod").reshape(n_layers, batch, 9 * d)
    small["b_ada"] = jnp.sum(dmod_me, axis=1)
    small_shapes = [(1,)] + [weights[k].shape if k not in ("g_norm", "conv_w") else small[k].shape for k in SMALL_NAMES]
    small_flat = _flatten([loss_part.reshape(1)] + [small[k] for k in SMALL_NAMES], 16 * LANES)
    dmod_all, small_all = _exchange([(dmod_me, 0), (small_flat, 0)], [], "gather_small")
    landed = {key: _flight_wait(h, small_all, f"grads_{key[0]}_{key[1]}_wait")[0] for key, h in handles.items()}
    layer_range = range(n_layers)
    p_up = jnp.stack([jnp.stack([landed[(l, "ffn0")][0], landed[(l, "ffn1")][0]]) for l in layer_range])
    p_down = jnp.stack([jnp.stack([landed[(l, "ffn0")][1], landed[(l, "ffn1")][1]]) for l in layer_range])
    p_in = jnp.stack([landed[(l, "mix")][0] for l in layer_range])
    p_out = jnp.stack([landed[(l, "mix")][1] for l in layer_range])
    small_sum = _unflatten(_sum_parts(small_all), small_shapes, 16 * LANES)
    loss = small_sum[0].reshape(())
    small_grads = dict(zip(SMALL_NAMES, small_sum[1:]))
    small_grads["g_norm"] = lax.dynamic_slice_in_dim(small_grads["g_norm"], me * g_norm.shape[-1], g_norm.shape[-1], 2)
    small_grads["conv_w"] = lax.dynamic_slice_in_dim(small_grads["conv_w"], me * conv_w.shape[-1], conv_w.shape[-1], 2)

    dmod_all = dmod_all.transpose(1, 0, 2, 3).reshape(n_layers, N_DEV * batch, 9 * d)
    dmod_loc = lax.dynamic_slice_in_dim(dmod_all, me * n_ada, n_ada, axis=2)
    g_ada = []
    for l in range(n_layers):
        (gl,) = _mm(c_all, dmod_loc[l], mode="tn", tm=d, tn=n_ada, tk=c_all.shape[0], a_pre=silu, name=f"dw_ada_{l}",
                    outs=[((d, n_ada), F32, (d, n_ada), lambda i, j: (0, 0))], epilogue=_store_epilogue([F32]))
        g_ada.append(gl)
    g_ada = jnp.stack(g_ada)

    results = {}

    def update(name, parts):
        shape = weights[name].shape
        as3d = lambda t: t.reshape((-1,) + shape[-2:])
        outs = _adamw(parts.reshape((-1,) + parts.shape[-3:]), as3d(weights[name]), as3d(moments_m[name]),
                      as3d(moments_v[name]), f"adamw_{name}")
        results[name] = [o.reshape(shape) for o in outs]

    update("w_ada", g_ada[:, None])
    update("w_ffn_up", p_up)
    update("w_ffn_down", p_down)
    update("w_in", p_in[..., :N_IN])
    update("w_out", p_out)
    sm_shapes = [weights[k].shape for k in SMALL_NAMES]
    flat = lambda src: _flatten([src[k] for k in SMALL_NAMES], 16 * LANES)
    sm_out = _adamw(flat(small_grads)[None, None], flat(weights)[None], flat(moments_m)[None], flat(moments_v)[None],
                    "adamw_small")
    for k, vals in zip(SMALL_NAMES, zip(*[_unflatten(o, sm_shapes, 16 * LANES) for o in sm_out])):
        results[k] = list(vals)

    outs = [loss, grad_x]
    for n in range(4):
        outs += [results[k][n] for k in WEIGHT_NAMES]
    return tuple(outs)
```

```python
import functools
import math

import jax
import jax.numpy as jnp
from jax import lax
from jax.experimental import pallas as pl
from jax.experimental.pallas import tpu as pltpu

F32 = jnp.float32
BF16 = jnp.bfloat16

N_DEV = 8
D_MODEL = 1024
D_FF = 2816
FF_SHARD = 2 * D_FF // N_DEV
N_FF_SHARD = D_FF // FF_SHARD
HEAD_DIM = 64
LRU_W = 512
ATT_W = 256
N_IN = 2564
N_IN_PAD = 2688
LANES = 128
SUBLANES = 8
BLK = 256
TQ = 512
KB_PER_Q = TQ // BLK
EPS = 1e-6
LRU_C = 8.0
NEG_BIG = -1e30
VMEM_LIMIT_BYTES = 48 * 1024 * 1024

ADAM_LR, ADAM_B1, ADAM_B2, ADAM_EPS, ADAM_WD, ADAM_STEP = 0.001, 0.9, 0.999, 1e-08, 0.01, 10

COL_SBQ, COL_SBK, COL_SBV = 8, 10, 12
COL_FXV, COL_FXF = 18, 20

NN = (((1,), (0,)), ((), ()))
NT = (((1,), (1,)), ((), ()))
TN = (((0,), (0,)), ((), ()))


def _params(n_axes):
    return pltpu.CompilerParams(dimension_semantics=("arbitrary",) * n_axes, vmem_limit_bytes=VMEM_LIMIT_BYTES)


def _tok_tile(seq):
    for t in (512, 256, 128):
        if seq % t == 0:
            return t
    raise ValueError(f"sequence length {seq} is not a multiple of 128")


def _dot(a, b, dims=NN):
    return lax.dot_general(a, b, dims, preferred_element_type=F32)


def _sigmoid(x):
    return 1.0 / (1.0 + jnp.exp(-x))


def _softplus(x):
    return jnp.maximum(x, 0.0) + jnp.log(1.0 + jnp.exp(-jnp.abs(x)))


def _gelu_parts(x):
    k0, k1 = math.sqrt(2.0 / math.pi), 0.044715
    t = jnp.tanh(k0 * (x + k1 * x * x * x))
    gelu = 0.5 * x * (1.0 + t)
    dgelu = 0.5 * (1.0 + t) + 0.5 * x * (1.0 - t * t) * k0 * (1.0 + 3.0 * k1 * x * x)
    return gelu, dgelu


def _neg_expm1(x):
    series = -x * (1.0 + x * (0.5 + x * (1.0 / 6.0 + x * (1.0 / 24.0 + x * (1.0 / 120.0 + x * (1.0 / 720.0))))))
    return jnp.where(x > -0.25, series, 1.0 - jnp.exp(x))


def _split2(x):
    hi = x.astype(BF16)
    lo = (x - hi.astype(F32)).astype(BF16)
    return hi, lo


def _split3(x):
    hi = x.astype(BF16)
    r = x - hi.astype(F32)
    mid = r.astype(BF16)
    lo = (r - mid.astype(F32)).astype(BF16)
    return hi, mid, lo


def _rows_to_block(rows, width):
    r = lax.broadcasted_iota(jnp.int32, (SUBLANES, width), 0)
    out = jnp.zeros((SUBLANES, width), F32)
    for n, v in enumerate(rows):
        out = jnp.where(r == n, jnp.broadcast_to(v, (SUBLANES, width)), out)
    return out


def _colsum(x):
    return jnp.sum(x, axis=0, keepdims=True)


def _exchange(gathers, scatters, name, two_level=False):
    assert not (two_level and scatters)
    n_g = len(gathers)
    ops = [a for a, _ in gathers] + [a for a, _ in scatters]
    n = len(ops)
    out_shape = [jax.ShapeDtypeStruct(a.shape[:nl] + (N_DEV,) + a.shape[nl:], a.dtype) for a, nl in gathers]
    out_shape += [jax.ShapeDtypeStruct(a.shape, a.dtype) for a, _ in scatters]
    items = []
    for k, (a, nl) in enumerate(list(gathers) + list(scatters)):
        for flat in range(math.prod(a.shape[:nl])):
            idx, rem = [], flat
            for dim in reversed(a.shape[:nl]):
                idx.insert(0, rem % dim)
                rem //= dim
            items.append((k, tuple(idx)))
    n_items = len(items)

    def body(*refs):
        ins, outs = refs[:n], refs[n:2 * n]
        send_sems, recv_sems, local_sems = refs[2 * n:]
        x, y, c = lax.axis_index("x"), lax.axis_index("y"), lax.axis_index("c")
        me = 4 * x + 2 * y + c

        def at(ref, idx):
            return ref.at[idx] if idx else ref

        def src(it, peer):
            k, idx = items[it]
            return at(ins[k], idx) if k < n_g else at(ins[k], idx + (peer,))

        def slot(it, s):
            k, idx = items[it]
            return at(outs[k], idx + (s,))

        def remote(it, rel, source, s, to):
            return pltpu.make_async_remote_copy(
                src_ref=source, dst_ref=slot(it, s), send_sem=send_sems.at[it, rel], recv_sem=recv_sems.at[it, rel],
                device_id=to, device_id_type=pl.DeviceIdType.MESH)

        local = [pltpu.make_async_copy(src(it, me), slot(it, me), local_sems.at[it]) for it in range(n_items)]
        for cp in local:
            cp.start()

        if not two_level:
            started = []
            for r in range(1, N_DEV):
                px = 1 - x if (r >> 2) & 1 else x
                py = 1 - y if (r >> 1) & 1 else y
                pc = 1 - c if r & 1 else c
                for it in range(n_items):
                    cp = remote(it, r - 1, src(it, 4 * px + 2 * py + pc), me, (px, py, pc))
                    cp.start()
                    started.append(cp)
            for cp in started:
                cp.wait()
        else:
            sibling, chips = (x, y, 1 - c), [(1 - x, y), (x, 1 - y), (1 - x, 1 - y)]
            sib = 4 * x + 2 * y + (1 - c)
            started = []
            for it in range(n_items):
                started.append(remote(it, 0, src(it, me), me, sibling))
                started += [remote(it, 1 + j, src(it, me), me, (cx, cy, c)) for j, (cx, cy) in enumerate(chips)]
            for cp in started:
                cp.start()
            for j, (cx, cy) in enumerate(chips):
                s = 4 * cx + 2 * cy + c
                for it in range(n_items):
                    remote(it, 1 + j, slot(it, s), s, sibling).wait_recv()
                    cp = remote(it, 4 + j, slot(it, s), s, sibling)
                    cp.start()
                    started.append(cp)
            for it in range(n_items):
                remote(it, 0, slot(it, sib), sib, sibling).wait_recv()
                for j, (cx, cy) in enumerate(chips):
                    s = 4 * cx + 2 * cy + (1 - c)
                    remote(it, 4 + j, slot(it, s), s, sibling).wait_recv()
            for cp in started:
                cp.wait_send()
        for cp in local:
            cp.wait()

    hbm = pl.BlockSpec(memory_space=pltpu.HBM)
    return pl.pallas_call(
        body, name=name, out_shape=out_shape,
        in_specs=[hbm] * n, out_specs=[hbm] * n,
        scratch_shapes=[pltpu.SemaphoreType.DMA((n_items, N_DEV - 1)), pltpu.SemaphoreType.DMA((n_items, N_DEV - 1)),
                        pltpu.SemaphoreType.DMA((n_items,))],
    )(*ops)


def _lead_items(ops):
    items = []
    for k, (a, nl) in enumerate(ops):
        for flat in range(math.prod(a.shape[:nl])):
            idx, rem = [], flat
            for dim in reversed(a.shape[:nl]):
                idx.insert(0, rem % dim)
                rem //= dim
            items.append((k, tuple(idx)))
    return items


def _flight_copies(ops, srcs, lands, send_sems, recv_sems):
    x, y, c = lax.axis_index("x"), lax.axis_index("y"), lax.axis_index("c")
    me = 4 * x + 2 * y + c
    copies = []
    for r in range(1, N_DEV):
        px = 1 - x if (r >> 2) & 1 else x
        py = 1 - y if (r >> 1) & 1 else y
        pc = 1 - c if r & 1 else c
        for it, (k, idx) in enumerate(_lead_items([(a, nl) for a, nl, _ in ops])):
            src = srcs[k].at[idx + (4 * px + 2 * py + pc,)] if ops[k][2] == "scatter" else (
                srcs[k].at[idx] if idx else srcs[k])
            copies.append(pltpu.make_async_remote_copy(
                src_ref=src, dst_ref=lands[k].at[idx + (me,)],
                send_sem=send_sems.at[it * (N_DEV - 1) + r - 1], recv_sem=recv_sems.at[it * (N_DEV - 1) + r - 1],
                device_id=(px, py, pc), device_id_type=pl.DeviceIdType.MESH))
    return copies


def _flight_start(ops, name):
    n = len(ops)
    me = 4 * lax.axis_index("x") + 2 * lax.axis_index("y") + lax.axis_index("c")
    srcs, lands = [], []
    for a, nl, kind in ops:
        if kind == "scatter":
            own, shape = lax.dynamic_slice_in_dim(a, me, 1, axis=nl), a.shape
        else:
            own, shape = jnp.expand_dims(a, nl), a.shape[:nl] + (N_DEV,) + a.shape[nl:]
        start = (0,) * nl + (me,) + (0,) * (len(shape) - nl - 1)
        lands.append(pltpu.with_memory_space_constraint(
            lax.dynamic_update_slice(lax.empty(shape, a.dtype), own, start), pltpu.HBM))
        srcs.append(pltpu.with_memory_space_constraint(a, pltpu.HBM))

    def body(*refs):
        for cp in _flight_copies(ops, refs[:n], refs[n:2 * n], refs[2 * n], refs[2 * n + 1]):
            cp.start()
        refs[-1][...] = jnp.zeros_like(refs[-1])

    hbm, sem = pl.BlockSpec(memory_space=pltpu.HBM), pl.BlockSpec(memory_space=pltpu.SEMAPHORE)
    n_items = len(_lead_items([(a, nl) for a, nl, _ in ops]))
    sems = pltpu.SemaphoreType.DMA((n_items * (N_DEV - 1),))
    res = pl.pallas_call(
        body, name=name,
        out_shape=[sems, sems] + [pltpu.HBM(a.shape, a.dtype) for a in srcs + lands]
        + [jax.ShapeDtypeStruct((SUBLANES, LANES), F32)],
        in_specs=[hbm] * (2 * n), out_specs=[sem, sem] + [hbm] * (2 * n) + [pl.BlockSpec(memory_space=pltpu.VMEM)],
        input_output_aliases={i: 2 + i for i in range(2 * n)},
        compiler_params=pltpu.CompilerParams(has_side_effects=pltpu.SideEffectType.DATAFLOW_SIDE_EFFECTING),
    )(*srcs, *lands)
    return (ops, res[0], res[1], res[2:2 + n], res[2 + n:2 + 2 * n]), res[-1]


def _flight_wait(handle, after, name):
    ops, send_sems, recv_sems, srcs, lands = handle
    n = len(ops)

    def body(*refs):
        for cp in _flight_copies(ops, refs[:n], refs[n:2 * n], refs[2 * n], refs[2 * n + 1]):
            cp.wait_send()
            cp.wait_recv()
        refs[-1][...] = jnp.zeros_like(refs[-1])

    hbm, sem = pl.BlockSpec(memory_space=pltpu.HBM), pl.BlockSpec(memory_space=pltpu.SEMAPHORE)
    res = pl.pallas_call(
        body, name=name,
        out_shape=[pltpu.HBM(a.shape, a.dtype) for a in list(srcs) + list(lands)]
        + [jax.ShapeDtypeStruct((SUBLANES, LANES), F32)],
        in_specs=[hbm] * (2 * n) + [sem, sem, pl.BlockSpec(memory_space=pl.ANY)],
        out_specs=[hbm] * (2 * n) + [pl.BlockSpec(memory_space=pltpu.VMEM)],
        input_output_aliases={i: i for i in range(2 * n)},
        compiler_params=pltpu.CompilerParams(has_side_effects=pltpu.SideEffectType.DATAFLOW_SIDE_EFFECTING),
    )(*srcs, *lands, send_sems, recv_sems, after)
    return res[n:2 * n], res[-1]


def _mm(a, b, *, mode, tm, tn, tk, outs, epilogue, name, extras=(), a_lead=(), b_lead=(), a_pre=None,
        a_spec=None, b_spec=None, shape=None, ksub=1):
    if shape is not None:
        mdim, ndim, kdim = shape
    else:
        if mode == "tn":
            kdim, mdim = a.shape[-2:]
        else:
            mdim, kdim = a.shape[-2:]
        ndim = b.shape[-2] if mode == "nt" else b.shape[-1]
    assert mdim % tm == 0 and ndim % tn == 0 and kdim % tk == 0, (name, mdim, ndim, kdim, tm, tn, tk)
    ni, nj, nk = mdim // tm, ndim // tn, kdim // tk
    a_lead, b_lead = tuple(a_lead), tuple(b_lead)
    a_block = (None,) * len(a_lead) + ((tk, tm) if mode == "tn" else (tm, tk))
    b_block = (None,) * len(b_lead) + ((tn, tk) if mode == "nt" else (tk, tn))
    dims = {"nn": NN, "nt": NT, "tn": TN}[mode]
    ne, no = len(extras), len(outs)

    def a_index(i, j, k):
        return a_lead + ((k, i) if mode == "tn" else (i, k))

    def b_index(i, j, k):
        return b_lead + ((j, k) if mode == "nt" else (k, j))

    if a_spec is not None:
        a_block, a_index = a_spec
    if b_spec is not None:
        b_block, b_index = b_spec

    def body(*refs):
        a_ref, b_ref = refs[0], refs[1]
        e_refs, o_refs = refs[2:2 + ne], refs[2 + ne:2 + ne + no]
        if ksub == 1:
            av = a_ref[...] if a_pre is None else a_pre(a_ref[...])
            p = _dot(av.astype(BF16), b_ref[...].astype(BF16), dims)
        else:
            p = _dot(a_ref[0], b_ref[0], dims)
            for s in range(1, ksub):
                p = p + _dot(a_ref[s], b_ref[s], dims)
        if nk == 1:
            epilogue(p, e_refs, o_refs)
        else:
            acc = refs[-1]
            k = pl.program_id(2)

            @pl.when(k == 0)
            def _():
                acc[...] = p

            @pl.when(k > 0)
            def _():
                acc[...] += p

            @pl.when(k == nk - 1)
            def _():
                epilogue(acc[...], e_refs, o_refs)

    in_specs = [pl.BlockSpec(a_block, a_index), pl.BlockSpec(b_block, b_index)]
    in_specs += [pl.BlockSpec(blk, functools.partial(lambda i, j, k, f: f(i, j), f=f)) for _, blk, f in extras]
    out_specs = [pl.BlockSpec(blk, functools.partial(lambda i, j, k, f: f(i, j), f=f)) for _, _, blk, f in outs]
    res = pl.pallas_call(
        body, name=name, grid=(ni, nj, nk), in_specs=in_specs, out_specs=out_specs,
        out_shape=[jax.ShapeDtypeStruct(s, d) for s, d, _, _ in outs],
        scratch_shapes=[pltpu.VMEM((tm, tn), F32)] if nk > 1 else [],
        compiler_params=_params(3),
    )(a, b, *[e[0] for e in extras])
    return res


def _store_epilogue(dtypes):
    def epi(p, e_refs, o_refs):
        for o, dt in zip(o_refs, dtypes):
            o[...] = p.astype(dt)
    return epi


def _normmod(x, gn, scale, shift, seq, name):
    m, d = x.shape
    tm = _tok_tile(seq)
    tpb = seq // tm

    def body(x_ref, gn_ref, sc_ref, sh_ref, h_ref):
        xv = x_ref[...]
        rstd = lax.rsqrt(jnp.mean(xv * xv, axis=-1, keepdims=True) + EPS)
        h_ref[...] = (xv * rstd * gn_ref[...] * (1.0 + sc_ref[0]) + sh_ref[0]).astype(BF16)

    vec = pl.BlockSpec((1, 1, d), lambda i: (i // tpb, 0, 0))
    return pl.pallas_call(
        body, name=name, grid=(m // tm,),
        in_specs=[pl.BlockSpec((tm, d), lambda i: (i, 0)), pl.BlockSpec((1, d), lambda i: (0, 0)), vec, vec],
        out_specs=pl.BlockSpec((tm, d), lambda i: (i, 0)),
        out_shape=jax.ShapeDtypeStruct((m, d), BF16), compiler_params=_params(1),
    )(x, gn, scale, shift)


def _normmod_bwd_epilogue(p, e_refs, o_refs):
    x_ref, dxo_ref, gn_ref, sc_ref = e_refs
    xv = x_ref[...]
    rstd = lax.rsqrt(jnp.mean(xv * xv, axis=-1, keepdims=True) + EPS)
    xhat = xv * rstd
    gn, sc1 = gn_ref[...], 1.0 + sc_ref[0]
    dxhat = p * (gn * sc1)
    dx = rstd * (dxhat - xhat * jnp.mean(dxhat * xhat, axis=-1, keepdims=True))
    o_refs[0][...] = dxo_ref[...] + dx
    t = p * xhat
    o_refs[1][0] = _rows_to_block([_colsum(p), _colsum(t * gn), _colsum(t * sc1)], p.shape[1])


def _residual_bwd(dx, f, gate, fac, seq, name):
    m, d = dx.shape
    tm = _tok_tile(seq)
    tpb = seq // tm

    def body(dx_ref, f_ref, g_ref, df_ref, dg_ref):
        dxv = dx_ref[...]
        df_ref[...] = ((fac * (1.0 + g_ref[0])) * dxv).astype(BF16)
        dg_ref[0] = _rows_to_block([_colsum((fac * dxv) * f_ref[...].astype(F32))], d)

    tile = pl.BlockSpec((tm, d), lambda i: (i, 0))
    return pl.pallas_call(
        body, name=name, grid=(m // tm,),
        in_specs=[tile, tile, pl.BlockSpec((1, 1, d), lambda i: (i // tpb, 0, 0))],
        out_specs=[tile, pl.BlockSpec((1, SUBLANES, d), lambda i: (i, 0, 0))],
        out_shape=[jax.ShapeDtypeStruct((m, d), BF16), jax.ShapeDtypeStruct((m // tm, SUBLANES, d), F32)],
        compiler_params=_params(1),
    )(dx, f, gate)


def _loss_head(y, target, seq):
    m, d = y.shape
    tm = _tok_tile(seq)

    def body(y_ref, t_ref, dy_ref, l_ref):
        err = y_ref[...] - t_ref[...]
        dy_ref[...] = err * (1.0 / d)
        part = 0.5 * jnp.sum(jnp.mean(err * err, axis=-1, keepdims=True), axis=0, keepdims=True)
        l_ref[0] = jnp.broadcast_to(part, (SUBLANES, LANES))

    tile = pl.BlockSpec((tm, d), lambda i: (i, 0))
    return pl.pallas_call(
        body, name="loss_head", grid=(m // tm,), in_specs=[tile, tile],
        out_specs=[tile, pl.BlockSpec((1, SUBLANES, LANES), lambda i: (i, 0, 0))],
        out_shape=[jax.ShapeDtypeStruct((m, d), F32), jax.ShapeDtypeStruct((m // tm, SUBLANES, LANES), F32)],
        compiler_params=_params(1),
    )(y, target)


def _ffn_fwd(x, h, wup, wdown, gate, seq, tag):
    m, d = x.shape
    tm = _tok_tile(seq)
    tpb = seq // tm

    def up_body(h_ref, wg_ref, wu_ref, a_ref, gu_ref):
        hv = h_ref[...]
        g, u = _dot(hv, wg_ref[...]), _dot(hv, wu_ref[...])
        sg = _sigmoid(g)
        silu = g * sg
        a_ref[...] = (silu * u).astype(BF16)
        gu_ref[0] = (u * (sg * (1.0 + g * (1.0 - sg)))).astype(BF16)
        gu_ref[1] = silu.astype(BF16)

    wblk = (None, d, FF_SHARD)
    a, gu = pl.pallas_call(
        up_body, name=f"ffn_up_{tag}", grid=(N_FF_SHARD, m // tm),
        in_specs=[pl.BlockSpec((tm, d), lambda j, i: (i, 0)),
                  pl.BlockSpec(wblk, lambda j, i: (j, 0, 0)),
                  pl.BlockSpec(wblk, lambda j, i: (j + N_FF_SHARD, 0, 0))],
        out_specs=[pl.BlockSpec((None, tm, FF_SHARD), lambda j, i: (j, i, 0)),
                   pl.BlockSpec((2, None, tm, FF_SHARD), lambda j, i: (0, j, i, 0))],
        out_shape=[jax.ShapeDtypeStruct((N_FF_SHARD, m, FF_SHARD), BF16),
                   jax.ShapeDtypeStruct((2, N_FF_SHARD, m, FF_SHARD), BF16)],
        compiler_params=_params(2),
    )(h, wup, wup)

    def down_epilogue(p, e_refs, o_refs):
        x_ref, g_ref = e_refs
        o_refs[0][...] = x_ref[...] + (0.5 * (1.0 + g_ref[0])) * p
        o_refs[1][...] = p.astype(BF16)

    if callable(wdown):
        wdown = wdown(a)
    wdown3 = wdown.reshape(N_FF_SHARD, FF_SHARD, d)
    x_out, f = _mm(a, wdown3, mode="nn", tm=tm, tn=d, tk=D_FF, ksub=N_FF_SHARD, name=f"ffn_down_{tag}",
                   shape=(m, d, D_FF), a_spec=((N_FF_SHARD, tm, FF_SHARD), lambda i, j, k: (0, i, 0)),
                   b_spec=((N_FF_SHARD, FF_SHARD, d), lambda i, j, k: (0, 0, 0)),
                   extras=[(x, (tm, d), lambda i, j: (i, 0)), (gate, (1, 1, d), lambda i, j: (i // tpb, 0, 0))],
                   outs=[((m, d), F32, (tm, d), lambda i, j: (i, 0)), ((m, d), BF16, (tm, d), lambda i, j: (i, 0))],
                   epilogue=down_epilogue)
    return x_out, (a, gu, f), wdown


def _ffn_bwd(dx_out, x, h, saved, wup, wdown, gn, scale, gate, seq, tag, on_grads):
    a, gu, f = saved
    m, d = x.shape
    tm = _tok_tile(seq)
    tpb = seq // tm
    df, dgate_parts = _residual_bwd(dx_out, f, gate, 0.5, seq, f"ffn_res_bwd_{tag}")

    def act_bwd_epilogue(p, e_refs, o_refs):
        o_refs[0][0] = (p * e_refs[0][0].astype(F32)).astype(BF16)
        o_refs[0][1] = (p * e_refs[0][1].astype(F32)).astype(BF16)

    gu_blk = (2, None, tm, FF_SHARD)
    (dgu,) = _mm(df, wdown, mode="nt", tm=tm, tn=FF_SHARD, tk=d, name=f"ffn_down_dx_{tag}", shape=(m, D_FF, d),
                 b_spec=((FF_SHARD, d), lambda i, j, k: (j, 0)),
                 extras=[(gu, gu_blk, lambda i, j: (0, j, i, 0))],
                 outs=[((2, N_FF_SHARD, m, FF_SHARD), BF16, gu_blk, lambda i, j: (0, j, i, 0))],
                 epilogue=act_bwd_epilogue)
    tt = 2 * tm if m % (2 * tm) == 0 else tm
    (dwdown,) = _mm(a, df, mode="tn", tm=FF_SHARD, tn=d, tk=tt, name=f"ffn_dwdown_{tag}", shape=(D_FF, d, m),
                    a_spec=((None, tt, FF_SHARD), lambda i, j, k: (i, k, 0)),
                    outs=[((D_FF, d), BF16, (FF_SHARD, d), lambda i, j: (i, 0))], epilogue=_store_epilogue([BF16]))
    dgu8 = dgu.reshape(2 * N_FF_SHARD, m, FF_SHARD)
    (dwup,) = _mm(h, dgu8, mode="tn", tm=d, tn=FF_SHARD, tk=tt, name=f"ffn_dwup_{tag}", shape=(d, 2 * D_FF, m),
                  b_spec=((None, tt, FF_SHARD), lambda i, j, k: (j, k, 0)),
                  outs=[((2 * N_FF_SHARD, d, FF_SHARD), BF16, (None, d, FF_SHARD), lambda i, j: (j, 0, 0))],
                  epilogue=_store_epilogue([BF16]))
    scale = scale + on_grads(dwup, dwdown)[0, 0]
    dx, nm_parts = _mm(dgu8, wup, mode="nt", tm=tm, tn=d, tk=D_FF, ksub=N_FF_SHARD, name=f"ffn_up_dx_{tag}",
                       shape=(m, d, 2 * D_FF), a_spec=((N_FF_SHARD, tm, FF_SHARD), lambda i, j, k: (k, i, 0)),
                       b_spec=((N_FF_SHARD, d, FF_SHARD), lambda i, j, k: (k, 0, 0)),
                       extras=[(x, (tm, d), lambda i, j: (i, 0)), (dx_out, (tm, d), lambda i, j: (i, 0)),
                               (gn, (1, d), lambda i, j: (0, 0)), (scale, (1, 1, d), lambda i, j: (i // tpb, 0, 0))],
                       outs=[((m, d), F32, (tm, d), lambda i, j: (i, 0)),
                             ((m // tm, SUBLANES, d), F32, (1, SUBLANES, d), lambda i, j: (i, 0, 0))],
                       epilogue=_normmod_bwd_epilogue)
    return dx, nm_parts, dgate_parts


def _shift_down(ext, n, rows):
    if n:
        ext = pltpu.roll(ext, n, 0)
    return ext[SUBLANES:SUBLANES + rows]


def _lru_gates(u, wr_ref, br_ref, wi_ref, bi_ref, lam_ref):
    ub = u.astype(BF16)
    r = _sigmoid(_dot(ub, wr_ref[...]) + br_ref[...])
    ig = _sigmoid(_dot(ub, wi_ref[...]) + bi_ref[...])
    sp = _softplus(-lam_ref[...])
    log_a = (-LRU_C * r) * sp
    a = jnp.exp(log_a)
    mult = jnp.sqrt(_neg_expm1(2.0 * log_a))
    return r, ig, sp, a, mult


def _conv(ext, cw_ref, cb_ref, rows):
    u = cb_ref[...] + cw_ref[3:4, :] * _shift_down(ext, 0, rows)
    for k in range(3):
        u = u + cw_ref[k:k + 1, :] * _shift_down(ext, 3 - k, rows)
    return u


def _lru_halo_spec(seq, ts):
    return pl.BlockSpec((SUBLANES, LRU_W),
                        lambda b, i: (jnp.maximum(b * (seq // SUBLANES) + i * (ts // SUBLANES) - 1, 0), 0))


def _lru_fwd(proj32, conv_w, conv_b, wr, br, wi, bi, lam, batch, seq):
    m = proj32.shape[0]
    ts = _tok_tile(seq)
    nt = seq // ts
    row = lambda b, i: (b * nt + i, 0)

    def body(x_ref, halo_ref, g_ref, cw_ref, cb_ref, wr_ref, br_ref, wi_ref, bi_ref, lam_ref,
             y_ref, h_ref, a_scr, b_scr, carry):
        i = pl.program_id(1)
        halo = jnp.where(i > 0, halo_ref[...], 0.0)
        ext = jnp.concatenate([halo, x_ref[...]], axis=0)
        u = _conv(ext, cw_ref, cb_ref, ts)
        _, ig, _, a, mult = _lru_gates(u, wr_ref, br_ref, wi_ref, bi_ref, lam_ref)
        a_scr[...] = a
        b_scr[...] = mult * (ig * u)

        @pl.when(i == 0)
        def _():
            carry[...] = jnp.zeros_like(carry)

        rid = lax.broadcasted_iota(jnp.int32, (SUBLANES, LRU_W), 0)

        def chunk(c, hprev):
            off = pl.multiple_of(c * SUBLANES, SUBLANES)
            av, bv = a_scr[pl.ds(off, SUBLANES), :], b_scr[pl.ds(off, SUBLANES), :]
            for d in (1, 2, 4):
                keep = rid >= d
                bv = jnp.where(keep, av * pltpu.roll(bv, d, 0) + bv, bv)
                av = jnp.where(keep, av * pltpu.roll(av, d, 0), av)
            h = av * hprev + bv
            h_ref[pl.ds(off, SUBLANES), :] = h
            return h[SUBLANES - 1:SUBLANES, :]

        carry[...] = lax.fori_loop(0, ts // SUBLANES, chunk, carry[...])
        gelu, _ = _gelu_parts(g_ref[...])
        y_ref[...] = h_ref[...] * gelu

    full = lambda shape: pl.BlockSpec(shape, lambda b, i: (0,) * len(shape))
    return pl.pallas_call(
        body, name="lru_fwd", grid=(batch, nt),
        in_specs=[pl.BlockSpec((ts, LRU_W), row), _lru_halo_spec(seq, ts),
                  pl.BlockSpec((ts, LRU_W), lambda b, i: (b * nt + i, 1)),
                  full((4, LRU_W)), full((1, LRU_W)), full((LRU_W, LRU_W)), full((1, LRU_W)),
                  full((LRU_W, LRU_W)), full((1, LRU_W)), full((1, LRU_W))],
        out_specs=[pl.BlockSpec((ts, LRU_W), row), pl.BlockSpec((ts, LRU_W), row)],
        out_shape=[jax.ShapeDtypeStruct((m, LRU_W), F32), jax.ShapeDtypeStruct((m, LRU_W), F32)],
        scratch_shapes=[pltpu.VMEM((ts, LRU_W), F32), pltpu.VMEM((ts, LRU_W), F32), pltpu.VMEM((1, LRU_W), F32)],
        compiler_params=_params(2),
    )(proj32, proj32, proj32, conv_w, conv_b, wr, br, wi, bi, lam)


def _lru_bwd(dy, proj32, h, conv_w, conv_b, wr, br, wi, bi, lam, batch, seq):
    m = proj32.shape[0]
    ts = _tok_tile(seq)
    nt = seq // ts
    row = lambda b, i: (b * nt + (nt - 1 - i), 0)
    halo = pl.BlockSpec((SUBLANES, LRU_W),
                        lambda b, i: (jnp.maximum(b * (seq // SUBLANES) + (nt - 1 - i) * (ts // SUBLANES) - 1, 0), 0))

    def body(dy_ref, x_ref, xhalo_ref, g_ref, h_ref, hhalo_ref, cw_ref, cb_ref, wr_ref, br_ref, wi_ref, bi_ref,
             lam_ref, dx_ref, dg_ref, dwr_ref, dwi_ref, sums_ref, a_scr, dh_scr, g_scr, carry, du_next):
        b, i = pl.program_id(0), pl.program_id(1)
        first_tile = i == nt - 1

        @pl.when((b == 0) & (i == 0))
        def _():
            dwr_ref[...] = jnp.zeros_like(dwr_ref)
            dwi_ref[...] = jnp.zeros_like(dwi_ref)
            sums_ref[...] = jnp.zeros_like(sums_ref)

        @pl.when(i == 0)
        def _():
            carry[...] = jnp.zeros_like(carry)
            du_next[...] = jnp.zeros_like(du_next)

        xhalo = jnp.where(first_tile, 0.0, xhalo_ref[...])
        ext = jnp.concatenate([xhalo, x_ref[...]], axis=0)
        u = _conv(ext, cw_ref, cb_ref, ts)
        r, ig, sp, a, mult = _lru_gates(u, wr_ref, br_ref, wi_ref, bi_ref, lam_ref)
        gelu, dgelu = _gelu_parts(g_ref[...])
        dyv, hv = dy_ref[...], h_ref[...]
        dg_ref[...] = (dyv * hv * dgelu).astype(BF16)
        a_scr[...] = a
        dh_scr[...] = dyv * gelu

        rid = lax.broadcasted_iota(jnp.int32, (SUBLANES, LRU_W), 0)
        nchunk = ts // SUBLANES

        def chunk(n, cg):
            off = pl.multiple_of((nchunk - 1 - n) * SUBLANES, SUBLANES)
            av, beta = a_scr[pl.ds(off, SUBLANES), :], dh_scr[pl.ds(off, SUBLANES), :]
            alpha = jnp.where(rid == SUBLANES - 1, 1.0, pltpu.roll(av, SUBLANES - 1, 0))
            for d in (1, 2, 4):
                keep = rid + d <= SUBLANES - 1
                beta = jnp.where(keep, beta + alpha * pltpu.roll(beta, SUBLANES - d, 0), beta)
                alpha = jnp.where(keep, alpha * pltpu.roll(alpha, SUBLANES - d, 0), alpha)
            gv = beta + alpha * cg
            g_scr[pl.ds(off, SUBLANES), :] = gv
            return av[0:1, :] * gv[0:1, :]

        carry[...] = lax.fori_loop(0, nchunk, chunk, carry[...])
        gv = g_scr[...]
        hhalo = jnp.where(first_tile, 0.0, hhalo_ref[...])
        hprev = _shift_down(jnp.concatenate([hhalo, hv], axis=0), 1, ts)
        dmult = gv * ig * u
        dig = gv * mult * u
        du = gv * mult * ig
        dlog_a = gv * hprev * a - dmult * a * a / mult
        dr = dlog_a * (-LRU_C * sp)
        dr_pre = dr * r * (1.0 - r)
        di_pre = dig * ig * (1.0 - ig)
        drb, dib, ub = dr_pre.astype(BF16), di_pre.astype(BF16), u.astype(BF16)
        du = du + _dot(drb, wr_ref[...], NT) + _dot(dib, wi_ref[...], NT)
        dwr_ref[...] += _dot(ub, drb, TN)
        dwi_ref[...] += _dot(ub, dib, TN)

        ext_du = jnp.concatenate([du, du_next[...]], axis=0)
        du_next[...] = du[0:SUBLANES, :]
        n_ext = ts + SUBLANES
        dx = cw_ref[3:4, :] * du
        sums = [_colsum(dr_pre), _colsum(di_pre), _colsum(dlog_a * (-LRU_C * r)), _colsum(du)]
        dcw = []
        for k in range(3):
            dx = dx + cw_ref[k:k + 1, :] * pltpu.roll(ext_du, n_ext - (3 - k), 0)[0:ts]
            dcw.append(_colsum(du * _shift_down(ext, 3 - k, ts)))
        dcw.append(_colsum(du * _shift_down(ext, 0, ts)))
        dx_ref[...] = dx.astype(BF16)
        sums_ref[...] += _rows_to_block(sums + dcw, LRU_W)

    full = lambda shape: pl.BlockSpec(shape, lambda b, i: (0,) * len(shape))
    tile = pl.BlockSpec((ts, LRU_W), row)
    return pl.pallas_call(
        body, name="lru_bwd", grid=(batch, nt),
        in_specs=[tile, tile, halo, pl.BlockSpec((ts, LRU_W), lambda b, i: (b * nt + (nt - 1 - i), 1)), tile, halo,
                  full((4, LRU_W)), full((1, LRU_W)), full((LRU_W, LRU_W)), full((1, LRU_W)),
                  full((LRU_W, LRU_W)), full((1, LRU_W)), full((1, LRU_W))],
        out_specs=[tile, tile, full((LRU_W, LRU_W)), full((LRU_W, LRU_W)), full((SUBLANES, LRU_W))],
        out_shape=[jax.ShapeDtypeStruct((m, LRU_W), BF16), jax.ShapeDtypeStruct((m, LRU_W), BF16),
                   jax.ShapeDtypeStruct((LRU_W, LRU_W), F32), jax.ShapeDtypeStruct((LRU_W, LRU_W), F32),
                   jax.ShapeDtypeStruct((SUBLANES, LRU_W), F32)],
        scratch_shapes=[pltpu.VMEM((ts, LRU_W), F32), pltpu.VMEM((ts, LRU_W), F32), pltpu.VMEM((ts, LRU_W), F32),
                        pltpu.VMEM((1, LRU_W), F32), pltpu.VMEM((SUBLANES, LRU_W), F32)],
        compiler_params=_params(2),
    )(dy, proj32, proj32, proj32, h, h, conv_w, conv_b, wr, br, wi, bi, lam)


def _head_masks():
    lane = lax.broadcasted_iota(jnp.int32, (1, LANES), 1)
    return lane < HEAD_DIM


def _stack_heads(x2):
    lo, zero = _head_masks(), jnp.zeros_like(x2)
    return jnp.concatenate([jnp.where(lo, x2, zero), jnp.where(lo, zero, x2)], axis=0)


def _unstack_heads(y):
    return jnp.where(_head_masks(), y[:TQ], y[TQ:])


def _stack_cols(a, b):
    return jnp.concatenate([a, b], axis=0)


def _causal(qi, kb, strict):
    r = jnp.bitwise_and(lax.broadcasted_iota(jnp.int32, (2 * TQ, BLK), 0), TQ - 1) + qi * TQ
    c = lax.broadcasted_iota(jnp.int32, (2 * TQ, BLK), 1) + kb * BLK
    return (c < r) if strict else (c <= r)


def _key_loop(qi, group, carry, descending=False):
    def trip(n, cr):
        done = [n * KB_PER_Q + j for j in range(KB_PER_Q)]
        return group([qi * KB_PER_Q - 1 - t for t in done] if descending else done, cr)

    return lax.fori_loop(0, qi, trip, carry)


def _one_by_one(block):
    def group(kbs, carry):
        for kb in kbs:
            carry = block(kb, carry, False)
        return carry
    return group


def _tri(cmp):
    r = lax.broadcasted_iota(jnp.int32, (BLK, BLK), 0)
    c = lax.broadcasted_iota(jnp.int32, (BLK, BLK), 1)
    return cmp(r, c)


def _dot_split(x, tri):
    hi, lo = _split2(x)
    return _dot(hi, tri) + _dot(lo, tri)


def _sb_fwd(proj16, batch, seq):
    nq = seq // TQ
    scale = HEAD_DIM ** -0.5

    def body(q_ref, k_ref, v_ref, y_ref, t_ref):
        qi = pl.program_id(2)
        qs = _stack_heads(q_ref[0])
        tri_after = _tri(lambda r, c: r > c).astype(BF16)

        def block(kb, carry, masked):
            acc, c = carry
            ks = pl.multiple_of(kb * BLK, BLK)
            k2, v2 = k_ref[0, pl.ds(ks, BLK), :], v_ref[0, pl.ds(ks, BLK), :]
            z = _dot(qs, k2, NT) * scale
            sp = _softplus(z)
            l = -sp
            if masked:
                valid = _causal(qi, kb, True)
                l = jnp.where(valid, l, 0.0)
            w = jnp.exp((z - sp) + _dot_split(l, tri_after) + c)
            if masked:
                w = jnp.where(valid, w, 0.0)
            return acc + _dot(w.astype(BF16), v2), c + jnp.sum(l, axis=1, keepdims=True)

        def group(kbs, carry):
            acc, c = carry
            kv = [(k_ref[0, pl.ds(pl.multiple_of(kb * BLK, BLK), BLK), :],
                   v_ref[0, pl.ds(pl.multiple_of(kb * BLK, BLK), BLK), :]) for kb in kbs]
            zs = [_dot(qs, k2, NT) * scale for k2, _ in kv]
            sps = [_softplus(z) for z in zs]
            afters = [_dot_split(-sp, tri_after) for sp in sps]
            for z, sp, after, (_, v2) in zip(zs, sps, afters, kv):
                acc = acc + _dot(jnp.exp((z - sp) + after + c).astype(BF16), v2)
                c = c - jnp.sum(sp, axis=1, keepdims=True)
            return acc, c

        carry = (jnp.zeros((2 * TQ, LANES), F32), jnp.zeros((2 * TQ, 1), F32))
        first = qi * KB_PER_Q
        for n in reversed(range(KB_PER_Q)):
            carry = block(first + n, carry, True)
        acc, c = _key_loop(qi, group, carry, descending=True)
        y_ref[...] = _unstack_heads(acc)
        t_ref[0] = _unstack_heads(jnp.broadcast_to(c, (2 * TQ, LANES)))

    m = batch * seq
    return pl.pallas_call(
        body, name="sb_fwd", grid=(batch, 2, nq),
        in_specs=[pl.BlockSpec((1, TQ, LANES), lambda b, p, q: (b, q, COL_SBQ + p)),
                  pl.BlockSpec((1, seq, LANES), lambda b, p, q: (b, 0, COL_SBK + p)),
                  pl.BlockSpec((1, seq, LANES), lambda b, p, q: (b, 0, COL_SBV + p))],
        out_specs=[pl.BlockSpec((TQ, LANES), lambda b, p, q: (b * nq + q, p)),
                   pl.BlockSpec((1, TQ, LANES), lambda b, p, q: (p, b * nq + q, 0))],
        out_shape=[jax.ShapeDtypeStruct((m, ATT_W), F32), jax.ShapeDtypeStruct((2, m, LANES), F32)],
        compiler_params=_params(3),
    )(proj16, proj16, proj16)


def _sb_bwd(dy, t, proj16, batch, seq):
    nq = seq // TQ
    scale = HEAD_DIM ** -0.5

    def body(dy_ref, t_ref, q_ref, k_ref, v_ref, dq_ref, dk_ref, dv_ref):
        qi = pl.program_id(2)

        @pl.when(qi == 0)
        def _():
            dk_ref[...] = jnp.zeros_like(dk_ref)
            dv_ref[...] = jnp.zeros_like(dv_ref)

        t2 = t_ref[0]
        qs, dys = _stack_heads(q_ref[0]), _stack_heads(dy_ref[...].astype(BF16))
        tot = _stack_cols(t2[:, 0:1], t2[:, HEAD_DIM:HEAD_DIM + 1])
        tri_incl = _tri(lambda r, c: r <= c).astype(BF16)
        tri_excl = _tri(lambda r, c: r < c).astype(BF16)

        def block(kb, carry, masked):
            dq, pc, ec = carry
            ks = pl.multiple_of(kb * BLK, BLK)
            k2, v2 = k_ref[0, pl.ds(ks, BLK), :], v_ref[0, pl.ds(ks, BLK), :]
            z = _dot(qs, k2, NT) * scale
            sp = _softplus(z)
            l, b = -sp, z - sp
            sig = jnp.exp(b)
            if masked:
                valid = _causal(qi, kb, True)
                l = jnp.where(valid, l, 0.0)
            after = tot - (pc + _dot_split(l, tri_incl))
            w = jnp.exp(b + after)
            if masked:
                w = jnp.where(valid, w, 0.0)
            e = _dot(dys, v2, NT) * w
            et = ec + _dot_split(e, tri_excl)
            dz = e * (1.0 - sig) - et * sig
            if masked:
                dz = jnp.where(valid, dz, 0.0)
            dzb = (dz * scale).astype(BF16)
            dk_ref[0, pl.ds(ks, BLK), :] += _dot(dzb, qs, TN)
            dv_ref[0, pl.ds(ks, BLK), :] += _dot(w.astype(BF16), dys, TN)
            return (dq + _dot(dzb, k2), pc + jnp.sum(l, axis=1, keepdims=True),
                    ec + jnp.sum(e, axis=1, keepdims=True))

        def group(kbs, carry):
            dq, pc, ec = carry
            starts = [pl.multiple_of(kb * BLK, BLK) for kb in kbs]
            kv = [(k_ref[0, pl.ds(ks, BLK), :], v_ref[0, pl.ds(ks, BLK), :]) for ks in starts]
            zs = [_dot(qs, k2, NT) * scale for k2, _ in kv]
            dws = [_dot(dys, v2, NT) for _, v2 in kv]
            sps = [_softplus(z) for z in zs]
            pins = [_dot_split(-sp, tri_incl) for sp in sps]
            es, ws, sigs = [], [], []
            for z, sp, pin, dw in zip(zs, sps, pins, dws):
                b = z - sp
                w = jnp.exp(b + (tot - (pc + pin)))
                pc = pc - jnp.sum(sp, axis=1, keepdims=True)
                es.append(dw * w)
                ws.append(w)
                sigs.append(jnp.exp(b))
            eins = [_dot_split(e, tri_excl) for e in es]
            for ks, (k2, _), e, w, sig, ein in zip(starts, kv, es, ws, sigs, eins):
                dzb = ((e * (1.0 - sig) - (ec + ein) * sig) * scale).astype(BF16)
                ec = ec + jnp.sum(e, axis=1, keepdims=True)
                dk_ref[0, pl.ds(ks, BLK), :] += _dot(dzb, qs, TN)
                dv_ref[0, pl.ds(ks, BLK), :] += _dot(w.astype(BF16), dys, TN)
                dq = dq + _dot(dzb, k2)
            return dq, pc, ec

        col = jnp.zeros((2 * TQ, 1), F32)
        first = qi * KB_PER_Q
        carry = _key_loop(qi, group, (jnp.zeros((2 * TQ, LANES), F32), col, col))
        for n in range(KB_PER_Q):
            carry = block(first + n, carry, True)
        dq_ref[...] = _unstack_heads(carry[0])

    m = batch * seq
    whole = lambda col: pl.BlockSpec((1, seq, LANES), lambda b, p, q: (b, 0, col + p))
    return pl.pallas_call(
        body, name="sb_bwd", grid=(batch, 2, nq),
        in_specs=[pl.BlockSpec((TQ, LANES), lambda b, p, q: (b * nq + q, p)),
                  pl.BlockSpec((1, TQ, LANES), lambda b, p, q: (p, b * nq + q, 0)),
                  pl.BlockSpec((1, TQ, LANES), lambda b, p, q: (b, q, COL_SBQ + p)),
                  whole(COL_SBK), whole(COL_SBV)],
        out_specs=[pl.BlockSpec((TQ, LANES), lambda b, p, q: (b * nq + q, p)), whole(0), whole(0)],
        out_shape=[jax.ShapeDtypeStruct((m, ATT_W), F32), jax.ShapeDtypeStruct((batch, seq, ATT_W), F32),
                   jax.ShapeDtypeStruct((batch, seq, ATT_W), F32)],
        compiler_params=_params(3),
    )(dy, t, proj16, proj16, proj16)


def _fox_pre(proj32, gq, gk, bf, group_mean, batch, seq):
    m = proj32.shape[0]
    ts = _tok_tile(seq)
    nt = seq // ts

    def body(q_ref, k_ref, f_ref, gq_ref, gk_ref, bf_ref, gm_ref, fq_ref, fk_ref, fc_ref, carry):
        i = pl.program_id(1)

        @pl.when(i == 0)
        def _():
            carry[...] = jnp.zeros_like(carry)

        gm = gm_ref[...]
        for src, g_ref, dst in ((q_ref, gq_ref, fq_ref), (k_ref, gk_ref, fk_ref)):
            v = src[...]
            ms = _dot_split(v * v, gm)
            dst[...] = (v * lax.rsqrt(ms + EPS) * g_ref[...]).astype(BF16)
        z = f_ref[...] + bf_ref[...]
        lf = jnp.minimum(z, 0.0) - jnp.log(1.0 + jnp.exp(-jnp.abs(z)))
        r = lax.broadcasted_iota(jnp.int32, (ts, ts), 0)
        c = lax.broadcasted_iota(jnp.int32, (ts, ts), 1)
        tri = (r >= c).astype(BF16)
        hi, mid, low = _split3(lf)
        fc = _dot(tri, hi) + _dot(tri, mid) + _dot(tri, low) + carry[...]
        fc_ref[...] = fc
        carry[...] = fc[ts - 1:ts, :]

    full = lambda shape: pl.BlockSpec(shape, lambda b, i: (0,) * len(shape))
    return pl.pallas_call(
        body, name="fox_pre", grid=(batch, nt),
        in_specs=[pl.BlockSpec((ts, ATT_W), lambda b, i: (b * nt + i, 7)),
                  pl.BlockSpec((ts, ATT_W), lambda b, i: (b * nt + i, 8)),
                  pl.BlockSpec((ts, LANES), lambda b, i: (b * nt + i, COL_FXF)),
                  full((1, ATT_W)), full((1, ATT_W)), full((1, LANES)), full((ATT_W, ATT_W))],
        out_specs=[pl.BlockSpec((ts, ATT_W), lambda b, i: (b * nt + i, 0)),
                   pl.BlockSpec((ts, ATT_W), lambda b, i: (b * nt + i, 0)),
                   pl.BlockSpec((ts, LANES), lambda b, i: (b * nt + i, 0))],
        out_shape=[jax.ShapeDtypeStruct((m, ATT_W), BF16), jax.ShapeDtypeStruct((m, ATT_W), BF16),
                   jax.ShapeDtypeStruct((m, LANES), F32)],
        scratch_shapes=[pltpu.VMEM((1, LANES), F32)],
        compiler_params=_params(2),
    )(proj32, proj32, proj32, gq, gk, bf, group_mean)


def _fox_specs(batch, seq):
    nq = seq // TQ
    return dict(
        qblk=pl.BlockSpec((1, TQ, LANES), lambda b, p, q: (b, q, p)),
        whole=pl.BlockSpec((1, seq, LANES), lambda b, p, q: (b, 0, p)),
        vwhole=pl.BlockSpec((1, seq, LANES), lambda b, p, q: (b, 0, COL_FXV + p)),
        fcol=pl.BlockSpec((1, 1, TQ, 2), lambda b, p, q: (b, p, q, 0)),
        frow=pl.BlockSpec((1, 1, 2, seq), lambda b, p, q: (b, p, 0, 0)),
        rows=pl.BlockSpec((TQ, LANES), lambda b, p, q: (b * nq + q, p)),
        stat=pl.BlockSpec((1, TQ, LANES), lambda b, p, q: (p, b * nq + q, 0)),
    )


def _fox_logits(qs, k2, fq_col, fr_ref, ks, is_a, scale):
    fk_row = jnp.where(is_a, fr_ref[0, 0, 0:1, pl.ds(ks, BLK)], fr_ref[0, 0, 1:2, pl.ds(ks, BLK)])
    return _dot(qs, k2, NT) * scale + fq_col - fk_row


def _fox_fwd(fq, fk, proj16, fcol, frow, batch, seq):
    nq = seq // TQ
    scale = HEAD_DIM ** -0.5

    def body(q_ref, k_ref, v_ref, fc_ref, fr_ref, y_ref, lse_ref):
        qi = pl.program_id(2)
        qs = _stack_heads(q_ref[0])
        fcv = fc_ref[0, 0]
        fq_col = _stack_cols(fcv[:, 0:1], fcv[:, 1:2])
        is_a = lax.broadcasted_iota(jnp.int32, (2 * TQ, 1), 0) < TQ

        def block(kb, carry, masked):
            acc, mx, den = carry
            ks = pl.multiple_of(kb * BLK, BLK)
            k2, v2 = k_ref[0, pl.ds(ks, BLK), :], v_ref[0, pl.ds(ks, BLK), :]
            s = _fox_logits(qs, k2, fq_col, fr_ref, ks, is_a, scale)
            if masked:
                s = jnp.where(_causal(qi, kb, False), s, NEG_BIG)
            mx_new = jnp.maximum(mx, jnp.max(s, axis=1, keepdims=True))
            p = jnp.exp(s - mx_new)
            alpha = jnp.exp(mx - mx_new)
            return (alpha * acc + _dot(p.astype(BF16), v2), mx_new, alpha * den + jnp.sum(p, axis=1, keepdims=True))

        first = qi * KB_PER_Q
        carry = (jnp.zeros((2 * TQ, LANES), F32), jnp.full((2 * TQ, 1), NEG_BIG, F32), jnp.zeros((2 * TQ, 1), F32))
        carry = _key_loop(qi, _one_by_one(block), carry)
        for n in range(KB_PER_Q):
            carry = block(first + n, carry, True)
        acc, mx, den = carry
        y_ref[...] = _unstack_heads(acc / den)
        lse_ref[0] = _unstack_heads(jnp.broadcast_to(mx + jnp.log(den), (2 * TQ, LANES)))

    m = batch * seq
    sp = _fox_specs(batch, seq)
    return pl.pallas_call(
        body, name="fox_fwd", grid=(batch, 2, nq),
        in_specs=[sp["qblk"], sp["whole"], sp["vwhole"], sp["fcol"], sp["frow"]],
        out_specs=[sp["rows"], sp["stat"]],
        out_shape=[jax.ShapeDtypeStruct((m, ATT_W), F32), jax.ShapeDtypeStruct((2, m, LANES), F32)],
        compiler_params=_params(3),
    )(fq, fk, proj16, fcol, frow)


def _fox_bwd(dy, y, lse, fq, fk, proj16, fcol, frow, batch, seq):
    nq = seq // TQ
    scale = HEAD_DIM ** -0.5

    def body(dy_ref, y_ref, lse_ref, q_ref, k_ref, v_ref, fc_ref, fr_ref, dq_ref, dk_ref, dv_ref, dfr_ref, dfc_ref):
        qi = pl.program_id(2)

        @pl.when(qi == 0)
        def _():
            dk_ref[...] = jnp.zeros_like(dk_ref)
            dv_ref[...] = jnp.zeros_like(dv_ref)
            dfr_ref[...] = jnp.zeros_like(dfr_ref)

        lo = _head_masks()
        lane = lax.broadcasted_iota(jnp.int32, (1, LANES), 1)
        dy2, lse2, fcv = dy_ref[...], lse_ref[0], fc_ref[0, 0]
        qs, dys = _stack_heads(q_ref[0]), _stack_heads(dy2.astype(BF16))
        dyy = dy2 * y_ref[...]
        delta = _stack_cols(jnp.sum(jnp.where(lo, dyy, 0.0), axis=1, keepdims=True),
                            jnp.sum(jnp.where(lo, 0.0, dyy), axis=1, keepdims=True))
        lse_col = _stack_cols(lse2[:, 0:1], lse2[:, HEAD_DIM:HEAD_DIM + 1])
        fq_col = _stack_cols(fcv[:, 0:1], fcv[:, 1:2])
        is_a = lax.broadcasted_iota(jnp.int32, (2 * TQ, 1), 0) < TQ

        def block(kb, carry, masked):
            dq, rs = carry
            ks = pl.multiple_of(kb * BLK, BLK)
            k2, v2 = k_ref[0, pl.ds(ks, BLK), :], v_ref[0, pl.ds(ks, BLK), :]
            p = jnp.exp(_fox_logits(qs, k2, fq_col, fr_ref, ks, is_a, scale) - lse_col)
            if masked:
                p = jnp.where(_causal(qi, kb, False), p, 0.0)
            ds = p * (_dot(dys, v2, NT) - delta)
            dsb = (ds * scale).astype(BF16)
            dk_ref[0, pl.ds(ks, BLK), :] += _dot(dsb, qs, TN)
            dv_ref[0, pl.ds(ks, BLK), :] += _dot(p.astype(BF16), dys, TN)
            dfr_ref[0, 0, 0:1, pl.ds(ks, BLK)] -= jnp.sum(ds[:TQ], axis=0, keepdims=True)
            dfr_ref[0, 0, 1:2, pl.ds(ks, BLK)] -= jnp.sum(ds[TQ:], axis=0, keepdims=True)
            return dq + _dot(dsb, k2), rs + jnp.sum(ds, axis=1, keepdims=True)

        def group(kbs, carry):
            dq, rs = carry
            starts = [pl.multiple_of(kb * BLK, BLK) for kb in kbs]
            kv = [(k_ref[0, pl.ds(ks, BLK), :], v_ref[0, pl.ds(ks, BLK), :]) for ks in starts]
            ss = [_fox_logits(qs, k2, fq_col, fr_ref, ks, is_a, scale) for ks, (k2, _) in zip(starts, kv)]
            dps = [_dot(dys, v2, NT) for _, v2 in kv]
            ps = [jnp.exp(s - lse_col) for s in ss]
            dss = [p * (dp - delta) for p, dp in zip(ps, dps)]
            for ks, (k2, _), p, ds in zip(starts, kv, ps, dss):
                dsb = (ds * scale).astype(BF16)
                dk_ref[0, pl.ds(ks, BLK), :] += _dot(dsb, qs, TN)
                dv_ref[0, pl.ds(ks, BLK), :] += _dot(p.astype(BF16), dys, TN)
                dfr_ref[0, 0, 0:1, pl.ds(ks, BLK)] -= jnp.sum(ds[:TQ], axis=0, keepdims=True)
                dfr_ref[0, 0, 1:2, pl.ds(ks, BLK)] -= jnp.sum(ds[TQ:], axis=0, keepdims=True)
                dq = dq + _dot(dsb, k2)
            return dq, rs + jnp.sum(functools.reduce(jnp.add, dss), axis=1, keepdims=True)

        first = qi * KB_PER_Q
        carry = _key_loop(qi, group, (jnp.zeros((2 * TQ, LANES), F32), jnp.zeros((2 * TQ, 1), F32)))
        for n in range(KB_PER_Q):
            carry = block(first + n, carry, True)
        dq, rs = carry
        dq_ref[...] = _unstack_heads(dq)
        dfc_ref[0] = jnp.where(lane == 0, rs[:TQ], jnp.where(lane == 1, rs[TQ:], 0.0))

    m = batch * seq
    sp = _fox_specs(batch, seq)
    return pl.pallas_call(
        body, name="fox_bwd", grid=(batch, 2, nq),
        in_specs=[sp["rows"], sp["rows"], sp["stat"], sp["qblk"], sp["whole"], sp["vwhole"], sp["fcol"], sp["frow"]],
        out_specs=[sp["rows"], sp["whole"], sp["whole"],
                   pl.BlockSpec((1, 1, SUBLANES, seq), lambda b, p, q: (b, p, 0, 0)), sp["stat"]],
        out_shape=[jax.ShapeDtypeStruct((m, ATT_W), F32), jax.ShapeDtypeStruct((batch, seq, ATT_W), F32),
                   jax.ShapeDtypeStruct((batch, seq, ATT_W), F32),
                   jax.ShapeDtypeStruct((batch, 2, SUBLANES, seq), F32), jax.ShapeDtypeStruct((2, m, LANES), F32)],
        compiler_params=_params(3),
    )(dy, y, lse, fq, fk, proj16, fcol, frow)


def _fox_post_bwd(dfq, dfk, dfc, proj32, gq, gk, bf, group_mean, batch, seq):
    m = proj32.shape[0]
    ts = _tok_tile(seq)
    nt = seq // ts
    tile = lambda w, col: pl.BlockSpec((ts, w), lambda b, i: (b * nt + (nt - 1 - i), col))

    def body(dfq_ref, dfk_ref, dfc_ref, q_ref, k_ref, f_ref, gq_ref, gk_ref, bf_ref, gm_ref,
             dq_ref, dk_ref, df_ref, gs_ref, bs_ref, carry):
        i = pl.program_id(1)

        @pl.when(i == 0)
        def _():
            carry[...] = jnp.zeros_like(carry)

        gm = gm_ref[...]
        rows = []
        for src, g_ref, d_ref, dst in ((q_ref, gq_ref, dfq_ref, dq_ref), (k_ref, gk_ref, dfk_ref, dk_ref)):
            v, dv = src[...], d_ref[...]
            rstd = lax.rsqrt(_dot_split(v * v, gm) + EPS)
            vhat = v * rstd
            rows.append(_colsum(dv * vhat))
            dvh = dv * g_ref[...]
            dst[...] = (rstd * (dvh - vhat * _dot_split(dvh * vhat, gm))).astype(BF16)
        gs_ref[0] = _rows_to_block(rows, ATT_W)

        dfc_v = dfc_ref[...]
        r = lax.broadcasted_iota(jnp.int32, (ts, ts), 0)
        c = lax.broadcasted_iota(jnp.int32, (ts, ts), 1)
        tri = (r <= c).astype(BF16)
        hi, mid, low = _split3(dfc_v)
        dlf = _dot(tri, hi) + _dot(tri, mid) + _dot(tri, low) + carry[...]
        carry[...] = dlf[0:1, :]
        z = f_ref[...] + bf_ref[...]
        dz = dlf * _sigmoid(-z)
        df_ref[...] = dz.astype(BF16)
        bs_ref[0] = _rows_to_block([_colsum(dz)], LANES)

    full = lambda shape: pl.BlockSpec(shape, lambda b, i: (0,) * len(shape))
    part = lambda w: pl.BlockSpec((1, SUBLANES, w), lambda b, i: (b * nt + (nt - 1 - i), 0, 0))
    return pl.pallas_call(
        body, name="fox_post_bwd", grid=(batch, nt),
        in_specs=[tile(ATT_W, 0), tile(ATT_W, 0), tile(LANES, 0), tile(ATT_W, 7), tile(ATT_W, 8), tile(LANES, COL_FXF),
                  full((1, ATT_W)), full((1, ATT_W)), full((1, LANES)), full((ATT_W, ATT_W))],
        out_specs=[tile(ATT_W, 0), tile(ATT_W, 0), tile(LANES, 0), part(ATT_W), part(LANES)],
        out_shape=[jax.ShapeDtypeStruct((m, ATT_W), BF16), jax.ShapeDtypeStruct((m, ATT_W), BF16),
                   jax.ShapeDtypeStruct((m, LANES), BF16),
                   jax.ShapeDtypeStruct((batch * nt, SUBLANES, ATT_W), F32),
                   jax.ShapeDtypeStruct((batch * nt, SUBLANES, LANES), F32)],
        scratch_shapes=[pltpu.VMEM((1, LANES), F32)],
        compiler_params=_params(2),
    )(dfq, dfk, dfc, proj32, proj32, proj32, gq, gk, bf, group_mean)


_GROUPS = ((0, LRU_W), (LRU_W, LRU_W + ATT_W), (LRU_W + ATT_W, LRU_W + 2 * ATT_W))


def _outnorm(y_lru, y_sb, y_fox, gmix, seq):
    m = y_lru.shape[0]
    tm = _tok_tile(seq)

    def body(a_ref, b_ref, c_ref, g_ref, o_ref):
        parts = []
        for ref in (a_ref, b_ref, c_ref):
            v = ref[...]
            parts.append(v * lax.rsqrt(jnp.mean(v * v, axis=-1, keepdims=True) + EPS))
        o_ref[...] = (jnp.concatenate(parts, axis=1) * g_ref[...]).astype(BF16)

    t = lambda w: pl.BlockSpec((tm, w), lambda i: (i, 0))
    return pl.pallas_call(
        body, name="outnorm", grid=(m // tm,),
        in_specs=[t(LRU_W), t(ATT_W), t(ATT_W), pl.BlockSpec((1, D_MODEL), lambda i: (0, 0))],
        out_specs=t(D_MODEL), out_shape=jax.ShapeDtypeStruct((m, D_MODEL), BF16), compiler_params=_params(1),
    )(y_lru, y_sb, y_fox, gmix)


def _outnorm_bwd_epilogue(p, e_refs, o_refs):
    gmix = e_refs[3][...]
    dg = []
    for n, (lo, hi) in enumerate(_GROUPS):
        v, dyn = e_refs[n][...], p[:, lo:hi]
        rstd = lax.rsqrt(jnp.mean(v * v, axis=-1, keepdims=True) + EPS)
        vhat = v * rstd
        dg.append(_colsum(dyn * vhat))
        dvh = dyn * gmix[:, lo:hi]
        o_refs[n][...] = rstd * (dvh - vhat * jnp.mean(dvh * vhat, axis=-1, keepdims=True))
    o_refs[3][0] = _rows_to_block([jnp.concatenate(dg, axis=1)], p.shape[1])


def _pair_layouts(fcum, batch, seq):
    f4 = fcum[:, :4].reshape(batch, seq, 2, 2)
    return f4.transpose(0, 2, 1, 3), f4.transpose(0, 2, 3, 1)


def _gate_grad_cols(dfr, dfc, batch, seq):
    keys = dfr[:, :, :2, :].transpose(0, 3, 1, 2).reshape(batch * seq, 4)
    queries = dfc[:, :, :2].transpose(1, 0, 2).reshape(batch * seq, 4)
    return jnp.pad(keys + queries, ((0, 0), (0, LANES - 4)))


def _mixer_fwd(x, h, w, gate, batch, seq):
    m, d = x.shape
    tm = _tok_tile(seq)
    tpb = seq // tm

    def in_epilogue(p, e_refs, o_refs):
        o_refs[0][...] = p
        o_refs[1][...] = p.astype(BF16)

    tn_in = 896
    proj32, proj16 = _mm(h, w["w_in"], mode="nn", tm=tm, tn=tn_in, tk=d, name="mix_in",
                         outs=[((m, N_IN_PAD), F32, (tm, tn_in), lambda i, j: (i, j)),
                               ((m, N_IN_PAD), BF16, (tm, tn_in), lambda i, j: (i, j))],
                         epilogue=in_epilogue)
    y_lru, h_lru = _lru_fwd(proj32, w["conv_w"], w["conv_b"], w["wr"], w["br"], w["wi"], w["bi"], w["lam"], batch, seq)
    p16 = proj16.reshape(batch, seq, N_IN_PAD)
    y_sb, t_sb = _sb_fwd(p16, batch, seq)
    fq, fk, fcum = _fox_pre(proj32, w["gq"], w["gk"], w["bf"], w["group_mean"], batch, seq)
    fcol, frow = _pair_layouts(fcum, batch, seq)
    fq3, fk3 = fq.reshape(batch, seq, ATT_W), fk.reshape(batch, seq, ATT_W)
    y_fox, lse = _fox_fwd(fq3, fk3, p16, fcol, frow, batch, seq)
    ynorm = _outnorm(y_lru, y_sb, y_fox, w["gmix"], seq)

    def out_epilogue(p, e_refs, o_refs):
        x_ref, g_ref = e_refs
        o_refs[0][...] = x_ref[...] + (1.0 + g_ref[0]) * p
        o_refs[1][...] = p.astype(BF16)

    x_out, out = _mm(ynorm, w["w_out"], mode="nn", tm=tm, tn=d, tk=d, name="mix_out",
                     extras=[(x, (tm, d), lambda i, j: (i, 0)), (gate, (1, 1, d), lambda i, j: (i // tpb, 0, 0))],
                     outs=[((m, d), F32, (tm, d), lambda i, j: (i, 0)), ((m, d), BF16, (tm, d), lambda i, j: (i, 0))],
                     epilogue=out_epilogue)
    saved = dict(proj32=proj32, p16=p16, h_lru=h_lru, y_lru=y_lru, y_sb=y_sb, t_sb=t_sb, fq3=fq3, fk3=fk3,
                 fcol=fcol, frow=frow, y_fox=y_fox, lse=lse, ynorm=ynorm, out=out)
    return x_out, saved


def _mixer_bwd(dx_out, x, h, s, w, gn, scale, gate, batch, seq, on_grads):
    m, d = x.shape
    tm = _tok_tile(seq)
    tpb = seq // tm
    dout, dgate_parts = _residual_bwd(dx_out, s["out"], gate, 1.0, seq, "mix_res_bwd")
    (dw_out,) = _mm(s["ynorm"], dout, mode="tn", tm=d, tn=d, tk=tm, name="mix_dwout",
                    outs=[((d, d), BF16, (d, d), lambda i, j: (i, j))], epilogue=_store_epilogue([BF16]))
    dy_lru, dy_sb, dy_fox, gmix_parts = _mm(
        dout, w["w_out"], mode="nt", tm=tm, tn=d, tk=d, name="mix_out_dx",
        extras=[(s["y_lru"], (tm, LRU_W), lambda i, j: (i, 0)), (s["y_sb"], (tm, ATT_W), lambda i, j: (i, 0)),
                (s["y_fox"], (tm, ATT_W), lambda i, j: (i, 0)), (w["gmix"], (1, d), lambda i, j: (0, 0))],
        outs=[((m, LRU_W), F32, (tm, LRU_W), lambda i, j: (i, 0)), ((m, ATT_W), F32, (tm, ATT_W), lambda i, j: (i, 0)),
              ((m, ATT_W), F32, (tm, ATT_W), lambda i, j: (i, 0)),
              ((m // tm, SUBLANES, d), F32, (1, SUBLANES, d), lambda i, j: (i, 0, 0))],
        epilogue=_outnorm_bwd_epilogue)

    dsq, dsk, dsv = _sb_bwd(dy_sb, s["t_sb"], s["p16"], batch, seq)
    dfq, dfk, dfv, dfr, dfc = _fox_bwd(dy_fox, s["y_fox"], s["lse"], s["fq3"], s["fk3"], s["p16"], s["fcol"],
                                       s["frow"], batch, seq)
    dfc_cols = _gate_grad_cols(dfr, dfc, batch, seq)
    dxq, dxk, dxf, gqk_parts, bf_parts = _fox_post_bwd(dfq, dfk.reshape(m, ATT_W), dfc_cols, s["proj32"],
                                                       w["gq"], w["gk"], w["bf"], w["group_mean"], batch, seq)
    dlx, dlg, dwr, dwi, lru_sums = _lru_bwd(dy_lru, s["proj32"], s["h_lru"], w["conv_w"], w["conv_b"], w["wr"],
                                            w["br"], w["wi"], w["bi"], w["lam"], batch, seq)
    dproj = jnp.concatenate([dlx, dlg, dsq.astype(BF16), dsk.reshape(m, ATT_W).astype(BF16),
                             dsv.reshape(m, ATT_W).astype(BF16), dxq, dxk, dfv.reshape(m, ATT_W).astype(BF16), dxf],
                            axis=1)
    tn_in = 896
    (dw_in,) = _mm(h, dproj, mode="tn", tm=d, tn=tn_in, tk=tm, name="mix_dwin",
                   outs=[((d, N_IN_PAD), BF16, (d, tn_in), lambda i, j: (i, j))], epilogue=_store_epilogue([BF16]))
    scale = scale + on_grads(dw_in, dw_out)[0, 0]
    dx, nm_parts = _mm(dproj, w["w_in"], mode="nt", tm=tm, tn=d, tk=tn_in, name="mix_in_dx",
                       extras=[(x, (tm, d), lambda i, j: (i, 0)), (dx_out, (tm, d), lambda i, j: (i, 0)),
                               (gn, (1, d), lambda i, j: (0, 0)), (scale, (1, 1, d), lambda i, j: (i // tpb, 0, 0))],
                       outs=[((m, d), F32, (tm, d), lambda i, j: (i, 0)),
                             ((m // tm, SUBLANES, d), F32, (1, SUBLANES, d), lambda i, j: (i, 0, 0))],
                       epilogue=_normmod_bwd_epilogue)
    grads = dict(dwr=dwr, dwi=dwi, lru_sums=lru_sums, gmix_parts=gmix_parts,
                 gqk_parts=gqk_parts, bf_parts=bf_parts)
    return dx, grads, nm_parts, dgate_parts


def _block_diag(w):
    nb = w.shape[0]
    eye = jnp.eye(nb, dtype=w.dtype)
    return (eye[:, None, :, None] * w[:, :, None, :]).reshape(nb * HEAD_DIM, nb * HEAD_DIM)


def _block_diag_grad(g):
    nb = LRU_W // HEAD_DIM
    g4 = g.reshape(nb, HEAD_DIM, nb, HEAD_DIM)
    return jnp.stack([g4[n, :, n, :] for n in range(nb)])


def _per_batch(parts, batch, row):
    r = parts[:, row, :]
    return r.reshape(batch, -1, r.shape[-1]).sum(axis=1)


def _local_step(x3, target3, mod, wts, big_weights):
    batch, seq, d = x3.shape
    assert seq % TQ == 0, seq
    m = batch * seq
    n_layers = mod.shape[0]
    x = x3.reshape(m, d)
    group_mean = _block_diag(jnp.full((ATT_W // HEAD_DIM, HEAD_DIM, HEAD_DIM), 1.0 / HEAD_DIM, BF16))
    vec = lambda l, j, t: mod[l, :, j, t][:, None, :]

    layers, saved = [], []
    for l in range(n_layers):
        gq = jnp.tile(wts["g_qk"][l, 0], ATT_W // HEAD_DIM)[None, :]
        gk = jnp.tile(wts["g_qk"][l, 1], ATT_W // HEAD_DIM)[None, :]
        bf = jnp.pad(wts["b_fgate"][l], (0, LANES - 4))[None, :]
        lw = dict(conv_w=wts["conv_w"][l],
                  conv_b=wts["conv_b"][l][None, :], wr=_block_diag(wts["w_rgate"][l]).astype(BF16),
                  br=wts["b_rgate"][l][None, :], wi=_block_diag(wts["w_igate"][l]).astype(BF16),
                  bi=wts["b_igate"][l][None, :], lam=wts["lru_lambda"][l][None, :], gq=gq, gk=gk, bf=bf,
                  group_mean=group_mean, gmix=wts["g_mix_out"][l][None, :])
        layers.append(lw)
        gn = lambda j: wts["g_norm"][l, j][None, :]
        sv = dict(x0=x)
        sv["h0"] = _normmod(x, gn(0), vec(l, 0, 1), vec(l, 0, 0), seq, f"normmod_{l}_0")
        wup, wdown = big_weights(l, "ffn0", sv["h0"])
        x, sv["ffn0"], wdown = _ffn_fwd(x, sv["h0"], wup, wdown, vec(l, 0, 2), seq, f"{l}_0")
        sv["w_ffn0"] = (wup, wdown)
        sv["x1"] = x
        sv["h1"] = _normmod(x, gn(1), vec(l, 1, 1), vec(l, 1, 0), seq, f"normmod_{l}_1")
        lw["w_in"], lw["w_out"] = big_weights(l, "mix", sv["h1"])
        x, sv["mix"] = _mixer_fwd(x, sv["h1"], lw, vec(l, 1, 2), batch, seq)
        sv["x2"] = x
        sv["h2"] = _normmod(x, gn(2), vec(l, 2, 1), vec(l, 2, 0), seq, f"normmod_{l}_2")
        wup, wdown = big_weights(l, "ffn1", sv["h2"])
        x, sv["ffn1"], wdown = _ffn_fwd(x, sv["h2"], wup, wdown, vec(l, 2, 2), seq, f"{l}_1")
        sv["w_ffn1"] = (wup, wdown)
        saved.append(sv)

    dx, loss_parts = _loss_head(x, target3.reshape(m, d), seq)
    loss = jnp.sum(loss_parts[:, 0, 0])

    handles = {}

    def scatter(key, shapes):
        def on_grads(*grads):
            ops = [(g.reshape(shape), 0, "scatter") for g, shape in zip(grads, shapes)]
            handles[key], token = _flight_start(ops, f"grads_{key[0]}_{key[1]}_start")
            return token
        return on_grads

    ffn_shapes = ((2 * N_FF_SHARD, d, FF_SHARD), (N_DEV, D_FF // N_DEV, d))
    mix_shapes = ((N_DEV, d // N_DEV, N_IN_PAD), (N_DEV, d // N_DEV, d))
    small = {k: [] for k in ("dmod", "g_norm", "b_fgate", "conv_w", "conv_b", "w_rgate", "b_rgate", "w_igate",
                             "b_igate", "lru_lambda", "g_qk", "g_mix_out")}
    for l in reversed(range(n_layers)):
        sv, lw = saved[l], layers[l]
        gn = lambda j: wts["g_norm"][l, j][None, :]
        dx, nm2, dg2 = _ffn_bwd(dx, sv["x2"], sv["h2"], sv["ffn1"], *sv["w_ffn1"], gn(2), vec(l, 2, 1), vec(l, 2, 2),
                                seq, f"{l}_1", scatter((l, "ffn1"), ffn_shapes))
        dx, mg, nm1, dg1 = _mixer_bwd(dx, sv["x1"], sv["h1"], sv["mix"], lw, gn(1), vec(l, 1, 1), vec(l, 1, 2),
                                      batch, seq, scatter((l, "mix"), mix_shapes))
        dx, nm0, dg0 = _ffn_bwd(dx, sv["x0"], sv["h0"], sv["ffn0"], *sv["w_ffn0"], gn(0), vec(l, 0, 1), vec(l, 0, 2),
                                seq, f"{l}_0", scatter((l, "ffn0"), ffn_shapes))
        dmod_l, gnorm_l = [], []
        for nm, dg in ((nm0, dg0), (nm1, dg1), (nm2, dg2)):
            dmod_l.append(jnp.stack([_per_batch(nm, batch, 0), _per_batch(nm, batch, 1), _per_batch(dg, batch, 0)],
                                    axis=1))
            gnorm_l.append(jnp.sum(nm[:, 2, :], axis=0))
        small["dmod"].insert(0, jnp.stack(dmod_l, axis=1))
        small["g_norm"].insert(0, jnp.stack(gnorm_l))
        ls = mg["lru_sums"]
        small["b_rgate"].insert(0, ls[0])
        small["b_igate"].insert(0, ls[1])
        small["lru_lambda"].insert(0, ls[2] * (-_sigmoid(-wts["lru_lambda"][l])))
        small["conv_b"].insert(0, ls[3])
        small["conv_w"].insert(0, ls[4:8])
        small["w_rgate"].insert(0, _block_diag_grad(mg["dwr"]))
        small["w_igate"].insert(0, _block_diag_grad(mg["dwi"]))
        small["g_mix_out"].insert(0, jnp.sum(mg["gmix_parts"][:, 0, :], axis=0))
        gqk = jnp.sum(mg["gqk_parts"][:, :2, :], axis=0).reshape(2, ATT_W // HEAD_DIM, HEAD_DIM).sum(axis=1)
        small["g_qk"].insert(0, gqk)
        small["b_fgate"].insert(0, jnp.sum(mg["bf_parts"][:, 0, :4], axis=0))
    small = {k: jnp.stack(v) for k, v in small.items()}
    return loss, dx.reshape(batch, seq, d), handles, small


def _row_tile(rows, row_bytes):
    for t in (512, 256, 128, 64, 32, 16):
        if rows % t == 0 and t * row_bytes <= 4 * 1024 * 1024:
            return t
    return rows


def _adamw(parts, w, m, v, name):
    groups, n_parts, rows, cols = parts.shape
    tr = _row_tile(rows, cols * (n_parts * parts.dtype.itemsize + 7 * 4))
    c1 = 1.0 - ADAM_B1 ** ADAM_STEP
    c2 = 1.0 - ADAM_B2 ** ADAM_STEP

    def body(p_ref, w_ref, m_ref, v_ref, g_out, d_out, m_out, v_out):
        g = p_ref[0].astype(F32)
        for n in range(1, n_parts):
            g = g + p_ref[n].astype(F32)
        m_new = ADAM_B1 * m_ref[...] + (1.0 - ADAM_B1) * g
        v_new = ADAM_B2 * v_ref[...] + (1.0 - ADAM_B2) * (g * g)
        g_out[...] = g
        d_out[...] = -ADAM_LR * ((m_new / c1) / (jnp.sqrt(v_new / c2) + ADAM_EPS) + ADAM_WD * w_ref[...])
        m_out[...] = m_new
        v_out[...] = v_new

    tile = pl.BlockSpec((None, tr, cols), lambda g, i: (g, i, 0))
    return pl.pallas_call(
        body, name=name, grid=(groups, rows // tr),
        in_specs=[pl.BlockSpec((None, n_parts, tr, cols), lambda g, i: (g, 0, i, 0)), tile, tile, tile],
        out_specs=[tile] * 4, out_shape=[jax.ShapeDtypeStruct((groups, rows, cols), F32)] * 4,
        compiler_params=_params(2),
    )(parts, w, m, v)


def _sum_parts(parts):
    n_parts, rows, cols = parts.shape

    def body(p_ref, o_ref):
        acc = p_ref[0]
        for n in range(1, n_parts):
            acc = acc + p_ref[n]
        o_ref[...] = acc

    return pl.pallas_call(body, name="sum_small", out_shape=jax.ShapeDtypeStruct((rows, cols), F32),
                          compiler_params=pltpu.CompilerParams(vmem_limit_bytes=VMEM_LIMIT_BYTES))(parts)


def _flatten(arrays, multiple):
    chunks = []
    for a in arrays:
        flat = a.reshape(-1).astype(F32)
        chunks.append(jnp.pad(flat, (0, (-flat.shape[0]) % multiple)).reshape(-1, LANES))
    return jnp.concatenate(chunks, axis=0)


def _unflatten(flat2d, shapes, multiple):
    flat2d, out, row = flat2d.reshape(-1, LANES), [], 0
    for s in shapes:
        n = math.prod(s)
        rows = (n + multiple - 1) // multiple * (multiple // LANES)
        out.append(flat2d[row:row + rows].reshape(-1)[:n].reshape(s))
        row += rows
    return out


SMALL_NAMES = ("b_ada", "g_norm", "b_fgate", "conv_w", "conv_b", "w_rgate", "b_rgate", "w_igate", "b_igate",
               "lru_lambda", "g_qk", "g_mix_out")
WEIGHT_NAMES = ("w_ada", "b_ada", "g_norm", "w_ffn_up", "w_ffn_down", "w_in", "b_fgate", "conv_w", "conv_b",
                "w_rgate", "b_rgate", "w_igate", "b_igate", "lru_lambda", "g_qk", "g_mix_out", "w_out")


def kernel(x, c, w_ada, b_ada, g_norm, w_ffn_up, w_ffn_down, w_in, b_fgate, conv_w, conv_b, w_rgate, b_rgate, w_igate, b_igate, lru_lambda, g_qk, g_mix_out, w_out, loss_target, m_w_ada, m_b_ada, m_g_norm, m_w_ffn_up, m_w_ffn_down, m_w_in, m_b_fgate, m_conv_w, m_conv_b, m_w_rgate, m_b_rgate, m_w_igate, m_b_igate, m_lru_lambda, m_g_qk, m_g_mix_out, m_w_out, v_w_ada, v_b_ada, v_g_norm, v_w_ffn_up, v_w_ffn_down, v_w_in, v_b_fgate, v_conv_w, v_conv_b, v_w_rgate, v_b_rgate, v_w_igate, v_b_igate, v_lru_lambda, v_g_qk, v_g_mix_out, v_w_out):
    batch, seq, d = x.shape
    n_layers = w_ada.shape[0]
    me = 4 * lax.axis_index("x") + 2 * lax.axis_index("y") + lax.axis_index("c")
    weights = dict(w_ada=w_ada, b_ada=b_ada, g_norm=g_norm, w_ffn_up=w_ffn_up, w_ffn_down=w_ffn_down, w_in=w_in,
                   b_fgate=b_fgate, conv_w=conv_w, conv_b=conv_b, w_rgate=w_rgate, b_rgate=b_rgate, w_igate=w_igate,
                   b_igate=b_igate, lru_lambda=lru_lambda, g_qk=g_qk, g_mix_out=g_mix_out, w_out=w_out)
    moments_m = dict(w_ada=m_w_ada, b_ada=m_b_ada, g_norm=m_g_norm, w_ffn_up=m_w_ffn_up, w_ffn_down=m_w_ffn_down,
                     w_in=m_w_in, b_fgate=m_b_fgate, conv_w=m_conv_w, conv_b=m_conv_b, w_rgate=m_w_rgate,
                     b_rgate=m_b_rgate, w_igate=m_w_igate, b_igate=m_b_igate, lru_lambda=m_lru_lambda, g_qk=m_g_qk,
                     g_mix_out=m_g_mix_out, w_out=m_w_out)
    moments_v = dict(w_ada=v_w_ada, b_ada=v_b_ada, g_norm=v_g_norm, w_ffn_up=v_w_ffn_up, w_ffn_down=v_w_ffn_down,
                     w_in=v_w_in, b_fgate=v_b_fgate, conv_w=v_conv_w, conv_b=v_conv_b, w_rgate=v_w_rgate,
                     b_rgate=v_b_rgate, w_igate=v_w_igate, b_igate=v_b_igate, lru_lambda=v_lru_lambda, g_qk=v_g_qk,
                     g_mix_out=v_g_mix_out, w_out=v_w_out)

    c_all, gn_all, cw_all = _exchange([(c, 0), (g_norm, 0), (conv_w, 0)], [], "gather_small_weights")
    c_all = c_all.reshape(N_DEV * batch, d)
    n_ada = w_ada.shape[-1]
    g_norm_full = gn_all.transpose(1, 2, 0, 3).reshape(n_layers, 3, d)
    conv_w_full = cw_all.transpose(1, 2, 0, 3).reshape(n_layers, 4, LRU_W)

    b_ada_loc = lax.dynamic_slice_in_dim(b_ada, me * n_ada, n_ada, axis=1)
    silu = lambda t: t * _sigmoid(t)

    def bias_epilogue(p, e_refs, o_refs):
        o_refs[0][...] = p + e_refs[0][...]

    mod_loc = []
    for l in range(n_layers):
        (ml,) = _mm(c_all, w_ada, mode="nn", tm=c_all.shape[0], tn=n_ada, tk=d, b_lead=(l,), a_pre=silu,
                    name=f"ada_{l}", extras=[(b_ada_loc[l][None, :], (1, n_ada), lambda i, j: (0, 0))],
                    outs=[((c_all.shape[0], n_ada), F32, (c_all.shape[0], n_ada), lambda i, j: (0, 0))],
                    epilogue=bias_epilogue)
        mod_loc.append(ml)
    (mod_all,) = _exchange([(jnp.stack(mod_loc), 0)], [], "gather_mod")
    mod_all = mod_all.transpose(1, 2, 0, 3).reshape(n_layers, N_DEV * batch, 9 * d)

    cast = lambda w, token: (w + token[0, 0]).astype(BF16)
    ffn_ops = lambda l, f, token: [(cast(w_ffn_up[l, f], token), 0, "gather"),
                                   (cast(w_ffn_down[l, f], token), 0, "gather")]
    mix_ops = lambda l, token: [(cast(jnp.pad(w_in[l], ((0, 0), (0, N_IN_PAD - N_IN))), token), 0, "gather"),
                                (cast(w_out[l], token), 0, "gather")]
    behind = lambda w, token: w + token[0, 0].astype(BF16)
    flights, landed_rest = {}, []

    def start(key, ops):
        flights[key], token = _flight_start(ops, f"weights_{key}_start")
        return token

    def wait(key, after):
        return _flight_wait(flights[key], after, f"weights_{key}_wait")

    def big_weights(l, part, after):
        if (l, part) == (0, "ffn0"):
            def wdown(after_up):
                (wd,), landed_down = wait("down", after_up)
                return behind(wd, start("mix", mix_ops(0, landed_down))).reshape(D_FF, d)

            return wup_first, wdown
        if (l, part) == (0, "mix"):
            (wi, wo), landed = wait("mix", after)
            rest = ffn_ops(0, 1, landed)
            for ll in range(1, n_layers):
                rest += ffn_ops(ll, 0, landed) + mix_ops(ll, landed) + ffn_ops(ll, 1, landed)
            return behind(wi, start("rest", rest)).reshape(d, N_IN_PAD), wo.reshape(d, d)
        if not landed_rest:
            landed_rest.extend(wait("rest", after)[0])
        at = 0 if l == 0 else 2 + 6 * (l - 1) + {"ffn0": 0, "mix": 2, "ffn1": 4}[part]
        first, second = landed_rest[at], landed_rest[at + 1]
        if part == "mix":
            return first.reshape(d, N_IN_PAD), second.reshape(d, d)
        return first, second.reshape(D_FF, d)

    w_up_first, mod_all = lax.optimization_barrier((w_ffn_up[0, 0].astype(BF16), mod_all))
    (wup_first,) = _exchange([(w_up_first, 0)], [], "gather_w_up_first", two_level=True)
    w_down_first, wup_first = lax.optimization_barrier((w_ffn_down[0, 0].astype(BF16), wup_first))
    token = start("down", [(w_down_first, 0, "gather")])
    mod_me = lax.dynamic_slice_in_dim(mod_all + token[0, 0], me * batch, batch, axis=1)
    mod_me = mod_me.reshape(n_layers, batch, 3, 3, d)

    wts = dict(g_norm=g_norm_full, conv_w=conv_w_full, conv_b=conv_b, w_rgate=w_rgate, b_rgate=b_rgate, w_igate=w_igate, b_igate=b_igate,
               lru_lambda=lru_lambda, g_qk=g_qk, g_mix_out=g_mix_out, b_fgate=b_fgate)
    loss_part, grad_x, handles, small = _local_step(x, loss_target, mod_me, wts, big_weights)

    dmod_me = small.pop("dmod").reshape(n_layers, batch, 9 * d)
    small["b_ada"] = jnp.sum(dmod_me, axis=1)
    small_shapes = [(1,)] + [weights[k].shape if k not in ("g_norm", "conv_w") else small[k].shape for k in SMALL_NAMES]
    small_flat = _flatten([loss_part.reshape(1)] + [small[k] for k in SMALL_NAMES], 16 * LANES)
    dmod_all, small_all = _exchange([(dmod_me, 0), (small_flat, 0)], [], "gather_small", two_level=True)
    landed = {key: _flight_wait(h, small_all, f"grads_{key[0]}_{key[1]}_wait")[0] for key, h in handles.items()}
    layer_range = range(n_layers)
    p_up = jnp.stack([jnp.stack([landed[(l, "ffn0")][0], landed[(l, "ffn1")][0]]) for l in layer_range])
    p_down = jnp.stack([jnp.stack([landed[(l, "ffn0")][1], landed[(l, "ffn1")][1]]) for l in layer_range])
    p_in = jnp.stack([landed[(l, "mix")][0] for l in layer_range])
    p_out = jnp.stack([landed[(l, "mix")][1] for l in layer_range])
    small_sum = _unflatten(_sum_parts(small_all), small_shapes, 16 * LANES)
    loss = small_sum[0].reshape(())
    small_grads = dict(zip(SMALL_NAMES, small_sum[1:]))
    small_grads["g_norm"] = lax.dynamic_slice_in_dim(small_grads["g_norm"], me * g_norm.shape[-1], g_norm.shape[-1], 2)
    small_grads["conv_w"] = lax.dynamic_slice_in_dim(small_grads["conv_w"], me * conv_w.shape[-1], conv_w.shape[-1], 2)

    dmod_all = dmod_all.transpose(1, 0, 2, 3).reshape(n_layers, N_DEV * batch, 9 * d)
    dmod_loc = lax.dynamic_slice_in_dim(dmod_all, me * n_ada, n_ada, axis=2)
    g_ada = []
    for l in range(n_layers):
        (gl,) = _mm(c_all, dmod_loc[l], mode="tn", tm=d, tn=n_ada, tk=c_all.shape[0], a_pre=silu, name=f"dw_ada_{l}",
                    outs=[((d, n_ada), F32, (d, n_ada), lambda i, j: (0, 0))], epilogue=_store_epilogue([F32]))
        g_ada.append(gl)
    g_ada = jnp.stack(g_ada)

    results = {}

    def update(name, parts):
        shape = weights[name].shape
        as3d = lambda t: t.reshape((-1,) + shape[-2:])
        outs = _adamw(parts.reshape((-1,) + parts.shape[-3:]), as3d(weights[name]), as3d(moments_m[name]),
                      as3d(moments_v[name]), f"adamw_{name}")
        results[name] = [o.reshape(shape) for o in outs]

    update("w_ada", g_ada[:, None])
    update("w_ffn_up", p_up)
    update("w_ffn_down", p_down)
    update("w_in", p_in[..., :N_IN])
    update("w_out", p_out)
    sm_shapes = [weights[k].shape for k in SMALL_NAMES]
    flat = lambda src: _flatten([src[k] for k in SMALL_NAMES], 16 * LANES)
    sm_out = _adamw(flat(small_grads)[None, None], flat(weights)[None], flat(moments_m)[None], flat(moments_v)[None],
                    "adamw_small")
    for k, vals in zip(SMALL_NAMES, zip(*[_unflatten(o, sm_shapes, 16 * LANES) for o in sm_out])):
        results[k] = list(vals)

    outs = [loss, grad_x]
    for n in range(4):
        outs += [results[k][n] for k in WEIGHT_NAMES]
    return tuple(outs)
```

```python
import functools
import math

import jax
import jax.numpy as jnp
from jax import lax
from jax.experimental import pallas as pl
from jax.experimental.pallas import tpu as pltpu

F32 = jnp.float32
BF16 = jnp.bfloat16

N_DEV = 8
D_MODEL = 1024
D_FF = 2816
FF_SHARD = 2 * D_FF // N_DEV
N_FF_SHARD = D_FF // FF_SHARD
HEAD_DIM = 64
LRU_W = 512
ATT_W = 256
N_IN = 2564
N_IN_PAD = 2688
LANES = 128
SUBLANES = 8
BLK = 256
TQ = 512
KB_PER_Q = TQ // BLK
EPS = 1e-6
LRU_C = 8.0
NEG_BIG = -1e30
VMEM_LIMIT_BYTES = 48 * 1024 * 1024

ADAM_LR, ADAM_B1, ADAM_B2, ADAM_EPS, ADAM_WD, ADAM_STEP = 0.001, 0.9, 0.999, 1e-08, 0.01, 10

COL_SBQ, COL_SBK, COL_SBV = 8, 10, 12
COL_FXV, COL_FXF = 18, 20

NN = (((1,), (0,)), ((), ()))
NT = (((1,), (1,)), ((), ()))
TN = (((0,), (0,)), ((), ()))


def _params(n_axes):
    return pltpu.CompilerParams(dimension_semantics=("arbitrary",) * n_axes, vmem_limit_bytes=VMEM_LIMIT_BYTES)


def _tok_tile(seq):
    for t in (512, 256, 128):
        if seq % t == 0:
            return t
    raise ValueError(f"sequence length {seq} is not a multiple of 128")


def _dot(a, b, dims=NN):
    return lax.dot_general(a, b, dims, preferred_element_type=F32)


def _sigmoid(x):
    return 1.0 / (1.0 + jnp.exp(-x))


def _softplus(x):
    return jnp.maximum(x, 0.0) + jnp.log(1.0 + jnp.exp(-jnp.abs(x)))


def _gelu_parts(x):
    k0, k1 = math.sqrt(2.0 / math.pi), 0.044715
    t = jnp.tanh(k0 * (x + k1 * x * x * x))
    gelu = 0.5 * x * (1.0 + t)
    dgelu = 0.5 * (1.0 + t) + 0.5 * x * (1.0 - t * t) * k0 * (1.0 + 3.0 * k1 * x * x)
    return gelu, dgelu


def _neg_expm1(x):
    series = -x * (1.0 + x * (0.5 + x * (1.0 / 6.0 + x * (1.0 / 24.0 + x * (1.0 / 120.0 + x * (1.0 / 720.0))))))
    return jnp.where(x > -0.25, series, 1.0 - jnp.exp(x))


def _split2(x):
    hi = x.astype(BF16)
    lo = (x - hi.astype(F32)).astype(BF16)
    return hi, lo


def _split3(x):
    hi = x.astype(BF16)
    r = x - hi.astype(F32)
    mid = r.astype(BF16)
    lo = (r - mid.astype(F32)).astype(BF16)
    return hi, mid, lo


def _rows_to_block(rows, width):
    r = lax.broadcasted_iota(jnp.int32, (SUBLANES, width), 0)
    out = jnp.zeros((SUBLANES, width), F32)
    for n, v in enumerate(rows):
        out = jnp.where(r == n, jnp.broadcast_to(v, (SUBLANES, width)), out)
    return out


def _colsum(x):
    return jnp.sum(x, axis=0, keepdims=True)


def _exchange(gathers, scatters, name, two_level=False):
    assert not (two_level and scatters)
    n_g = len(gathers)
    ops = [a for a, _ in gathers] + [a for a, _ in scatters]
    n = len(ops)
    out_shape = [jax.ShapeDtypeStruct(a.shape[:nl] + (N_DEV,) + a.shape[nl:], a.dtype) for a, nl in gathers]
    out_shape += [jax.ShapeDtypeStruct(a.shape, a.dtype) for a, _ in scatters]
    items = []
    for k, (a, nl) in enumerate(list(gathers) + list(scatters)):
        for flat in range(math.prod(a.shape[:nl])):
            idx, rem = [], flat
            for dim in reversed(a.shape[:nl]):
                idx.insert(0, rem % dim)
                rem //= dim
            items.append((k, tuple(idx)))
    n_items = len(items)

    def body(*refs):
        ins, outs = refs[:n], refs[n:2 * n]
        send_sems, recv_sems, local_sems = refs[2 * n:]
        x, y, c = lax.axis_index("x"), lax.axis_index("y"), lax.axis_index("c")
        me = 4 * x + 2 * y + c

        def at(ref, idx):
            return ref.at[idx] if idx else ref

        def src(it, peer):
            k, idx = items[it]
            return at(ins[k], idx) if k < n_g else at(ins[k], idx + (peer,))

        def slot(it, s):
            k, idx = items[it]
            return at(outs[k], idx + (s,))

        def remote(it, rel, source, s, to):
            return pltpu.make_async_remote_copy(
                src_ref=source, dst_ref=slot(it, s), send_sem=send_sems.at[it, rel], recv_sem=recv_sems.at[it, rel],
                device_id=to, device_id_type=pl.DeviceIdType.MESH)

        local = [pltpu.make_async_copy(src(it, me), slot(it, me), local_sems.at[it]) for it in range(n_items)]
        for cp in local:
            cp.start()

        if not two_level:
            started = []
            for r in range(1, N_DEV):
                px = 1 - x if (r >> 2) & 1 else x
                py = 1 - y if (r >> 1) & 1 else y
                pc = 1 - c if r & 1 else c
                for it in range(n_items):
                    cp = remote(it, r - 1, src(it, 4 * px + 2 * py + pc), me, (px, py, pc))
                    cp.start()
                    started.append(cp)
            for cp in started:
                cp.wait()
        else:
            sibling, chips = (x, y, 1 - c), [(1 - x, y), (x, 1 - y), (1 - x, 1 - y)]
            sib = 4 * x + 2 * y + (1 - c)
            started = []
            for it in range(n_items):
                started.append(remote(it, 0, src(it, me), me, sibling))
                started += [remote(it, 1 + j, src(it, me), me, (cx, cy, c)) for j, (cx, cy) in enumerate(chips)]
            for cp in started:
                cp.start()
            for j, (cx, cy) in enumerate(chips):
                s = 4 * cx + 2 * cy + c
                for it in range(n_items):
                    remote(it, 1 + j, slot(it, s), s, sibling).wait_recv()
                    cp = remote(it, 4 + j, slot(it, s), s, sibling)
                    cp.start()
                    started.append(cp)
            for it in range(n_items):
                remote(it, 0, slot(it, sib), sib, sibling).wait_recv()
                for j, (cx, cy) in enumerate(chips):
                    s = 4 * cx + 2 * cy + (1 - c)
                    remote(it, 4 + j, slot(it, s), s, sibling).wait_recv()
            for cp in started:
                cp.wait_send()
        for cp in local:
            cp.wait()

    hbm = pl.BlockSpec(memory_space=pltpu.HBM)
    return pl.pallas_call(
        body, name=name, out_shape=out_shape,
        in_specs=[hbm] * n, out_specs=[hbm] * n,
        scratch_shapes=[pltpu.SemaphoreType.DMA((n_items, N_DEV - 1)), pltpu.SemaphoreType.DMA((n_items, N_DEV - 1)),
                        pltpu.SemaphoreType.DMA((n_items,))],
    )(*ops)


def _lead_items(ops):
    items = []
    for k, (a, nl) in enumerate(ops):
        for flat in range(math.prod(a.shape[:nl])):
            idx, rem = [], flat
            for dim in reversed(a.shape[:nl]):
                idx.insert(0, rem % dim)
                rem //= dim
            items.append((k, tuple(idx)))
    return items


def _flight_copies(ops, srcs, lands, send_sems, recv_sems):
    x, y, c = lax.axis_index("x"), lax.axis_index("y"), lax.axis_index("c")
    me = 4 * x + 2 * y + c
    copies = []
    for r in range(1, N_DEV):
        px = 1 - x if (r >> 2) & 1 else x
        py = 1 - y if (r >> 1) & 1 else y
        pc = 1 - c if r & 1 else c
        for it, (k, idx) in enumerate(_lead_items([(a, nl) for a, nl, _ in ops])):
            src = srcs[k].at[idx + (4 * px + 2 * py + pc,)] if ops[k][2] == "scatter" else (
                srcs[k].at[idx] if idx else srcs[k])
            copies.append(pltpu.make_async_remote_copy(
                src_ref=src, dst_ref=lands[k].at[idx + (me,)],
                send_sem=send_sems.at[it * (N_DEV - 1) + r - 1], recv_sem=recv_sems.at[it * (N_DEV - 1) + r - 1],
                device_id=(px, py, pc), device_id_type=pl.DeviceIdType.MESH))
    return copies


def _flight_start(ops, name):
    n = len(ops)
    me = 4 * lax.axis_index("x") + 2 * lax.axis_index("y") + lax.axis_index("c")
    srcs, lands = [], []
    for a, nl, kind in ops:
        if kind == "scatter":
            own, shape = lax.dynamic_slice_in_dim(a, me, 1, axis=nl), a.shape
        else:
            own, shape = jnp.expand_dims(a, nl), a.shape[:nl] + (N_DEV,) + a.shape[nl:]
        start = (0,) * nl + (me,) + (0,) * (len(shape) - nl - 1)
        lands.append(pltpu.with_memory_space_constraint(
            lax.dynamic_update_slice(lax.empty(shape, a.dtype), own, start), pltpu.HBM))
        srcs.append(pltpu.with_memory_space_constraint(a, pltpu.HBM))

    def body(*refs):
        for cp in _flight_copies(ops, refs[:n], refs[n:2 * n], refs[2 * n], refs[2 * n + 1]):
            cp.start()
        refs[-1][...] = jnp.zeros_like(refs[-1])

    hbm, sem = pl.BlockSpec(memory_space=pltpu.HBM), pl.BlockSpec(memory_space=pltpu.SEMAPHORE)
    n_items = len(_lead_items([(a, nl) for a, nl, _ in ops]))
    sems = pltpu.SemaphoreType.DMA((n_items * (N_DEV - 1),))
    res = pl.pallas_call(
        body, name=name,
        out_shape=[sems, sems] + [pltpu.HBM(a.shape, a.dtype) for a in srcs + lands]
        + [jax.ShapeDtypeStruct((SUBLANES, LANES), F32)],
        in_specs=[hbm] * (2 * n), out_specs=[sem, sem] + [hbm] * (2 * n) + [pl.BlockSpec(memory_space=pltpu.VMEM)],
        input_output_aliases={i: 2 + i for i in range(2 * n)},
        compiler_params=pltpu.CompilerParams(has_side_effects=pltpu.SideEffectType.DATAFLOW_SIDE_EFFECTING),
    )(*srcs, *lands)
    return (ops, res[0], res[1], res[2:2 + n], res[2 + n:2 + 2 * n]), res[-1]


def _flight_wait(handle, after, name):
    ops, send_sems, recv_sems, srcs, lands = handle
    n = len(ops)

    def body(*refs):
        for cp in _flight_copies(ops, refs[:n], refs[n:2 * n], refs[2 * n], refs[2 * n + 1]):
            cp.wait_send()
            cp.wait_recv()
        refs[-1][...] = jnp.zeros_like(refs[-1])

    hbm, sem = pl.BlockSpec(memory_space=pltpu.HBM), pl.BlockSpec(memory_space=pltpu.SEMAPHORE)
    res = pl.pallas_call(
        body, name=name,
        out_shape=[pltpu.HBM(a.shape, a.dtype) for a in list(srcs) + list(lands)]
        + [jax.ShapeDtypeStruct((SUBLANES, LANES), F32)],
        in_specs=[hbm] * (2 * n) + [sem, sem, pl.BlockSpec(memory_space=pl.ANY)],
        out_specs=[hbm] * (2 * n) + [pl.BlockSpec(memory_space=pltpu.VMEM)],
        input_output_aliases={i: i for i in range(2 * n)},
        compiler_params=pltpu.CompilerParams(has_side_effects=pltpu.SideEffectType.DATAFLOW_SIDE_EFFECTING),
    )(*srcs, *lands, send_sems, recv_sems, after)
    return res[n:2 * n], res[-1]


def _mm(a, b, *, mode, tm, tn, tk, outs, epilogue, name, extras=(), a_lead=(), b_lead=(), a_pre=None,
        a_spec=None, b_spec=None, shape=None, ksub=1):
    if shape is not None:
        mdim, ndim, kdim = shape
    else:
        if mode == "tn":
            kdim, mdim = a.shape[-2:]
        else:
            mdim, kdim = a.shape[-2:]
        ndim = b.shape[-2] if mode == "nt" else b.shape[-1]
    assert mdim % tm == 0 and ndim % tn == 0 and kdim % tk == 0, (name, mdim, ndim, kdim, tm, tn, tk)
    ni, nj, nk = mdim // tm, ndim // tn, kdim // tk
    a_lead, b_lead = tuple(a_lead), tuple(b_lead)
    a_block = (None,) * len(a_lead) + ((tk, tm) if mode == "tn" else (tm, tk))
    b_block = (None,) * len(b_lead) + ((tn, tk) if mode == "nt" else (tk, tn))
    dims = {"nn": NN, "nt": NT, "tn": TN}[mode]
    ne, no = len(extras), len(outs)

    def a_index(i, j, k):
        return a_lead + ((k, i) if mode == "tn" else (i, k))

    def b_index(i, j, k):
        return b_lead + ((j, k) if mode == "nt" else (k, j))

    if a_spec is not None:
        a_block, a_index = a_spec
    if b_spec is not None:
        b_block, b_index = b_spec

    def body(*refs):
        a_ref, b_ref = refs[0], refs[1]
        e_refs, o_refs = refs[2:2 + ne], refs[2 + ne:2 + ne + no]
        if ksub == 1:
            av = a_ref[...] if a_pre is None else a_pre(a_ref[...])
            p = _dot(av.astype(BF16), b_ref[...].astype(BF16), dims)
        else:
            p = _dot(a_ref[0], b_ref[0], dims)
            for s in range(1, ksub):
                p = p + _dot(a_ref[s], b_ref[s], dims)
        if nk == 1:
            epilogue(p, e_refs, o_refs)
        else:
            acc = refs[-1]
            k = pl.program_id(2)

            @pl.when(k == 0)
            def _():
                acc[...] = p

            @pl.when(k > 0)
            def _():
                acc[...] += p

            @pl.when(k == nk - 1)
            def _():
                epilogue(acc[...], e_refs, o_refs)

    in_specs = [pl.BlockSpec(a_block, a_index), pl.BlockSpec(b_block, b_index)]
    in_specs += [pl.BlockSpec(blk, functools.partial(lambda i, j, k, f: f(i, j), f=f)) for _, blk, f in extras]
    out_specs = [pl.BlockSpec(blk, functools.partial(lambda i, j, k, f: f(i, j), f=f)) for _, _, blk, f in outs]
    res = pl.pallas_call(
        body, name=name, grid=(ni, nj, nk), in_specs=in_specs, out_specs=out_specs,
        out_shape=[jax.ShapeDtypeStruct(s, d) for s, d, _, _ in outs],
        scratch_shapes=[pltpu.VMEM((tm, tn), F32)] if nk > 1 else [],
        compiler_params=_params(3),
    )(a, b, *[e[0] for e in extras])
    return res


def _store_epilogue(dtypes):
    def epi(p, e_refs, o_refs):
        for o, dt in zip(o_refs, dtypes):
            o[...] = p.astype(dt)
    return epi


def _normmod(x, gn, scale, shift, seq, name):
    m, d = x.shape
    tm = _tok_tile(seq)
    tpb = seq // tm

    def body(x_ref, gn_ref, sc_ref, sh_ref, h_ref):
        xv = x_ref[...]
        rstd = lax.rsqrt(jnp.mean(xv * xv, axis=-1, keepdims=True) + EPS)
        h_ref[...] = (xv * rstd * gn_ref[...] * (1.0 + sc_ref[0]) + sh_ref[0]).astype(BF16)

    vec = pl.BlockSpec((1, 1, d), lambda i: (i // tpb, 0, 0))
    return pl.pallas_call(
        body, name=name, grid=(m // tm,),
        in_specs=[pl.BlockSpec((tm, d), lambda i: (i, 0)), pl.BlockSpec((1, d), lambda i: (0, 0)), vec, vec],
        out_specs=pl.BlockSpec((tm, d), lambda i: (i, 0)),
        out_shape=jax.ShapeDtypeStruct((m, d), BF16), compiler_params=_params(1),
    )(x, gn, scale, shift)


def _normmod_bwd_epilogue(p, e_refs, o_refs):
    x_ref, dxo_ref, gn_ref, sc_ref = e_refs
    xv = x_ref[...]
    rstd = lax.rsqrt(jnp.mean(xv * xv, axis=-1, keepdims=True) + EPS)
    xhat = xv * rstd
    gn, sc1 = gn_ref[...], 1.0 + sc_ref[0]
    dxhat = p * (gn * sc1)
    dx = rstd * (dxhat - xhat * jnp.mean(dxhat * xhat, axis=-1, keepdims=True))
    o_refs[0][...] = dxo_ref[...] + dx
    t = p * xhat
    o_refs[1][0] = _rows_to_block([_colsum(p), _colsum(t * gn), _colsum(t * sc1)], p.shape[1])


def _residual_bwd(dx, f, gate, fac, seq, name):
    m, d = dx.shape
    tm = _tok_tile(seq)
    tpb = seq // tm

    def body(dx_ref, f_ref, g_ref, df_ref, dg_ref):
        dxv = dx_ref[...]
        df_ref[...] = ((fac * (1.0 + g_ref[0])) * dxv).astype(BF16)
        dg_ref[0] = _rows_to_block([_colsum((fac * dxv) * f_ref[...].astype(F32))], d)

    tile = pl.BlockSpec((tm, d), lambda i: (i, 0))
    return pl.pallas_call(
        body, name=name, grid=(m // tm,),
        in_specs=[tile, tile, pl.BlockSpec((1, 1, d), lambda i: (i // tpb, 0, 0))],
        out_specs=[tile, pl.BlockSpec((1, SUBLANES, d), lambda i: (i, 0, 0))],
        out_shape=[jax.ShapeDtypeStruct((m, d), BF16), jax.ShapeDtypeStruct((m // tm, SUBLANES, d), F32)],
        compiler_params=_params(1),
    )(dx, f, gate)


def _loss_head(y, target, seq):
    m, d = y.shape
    tm = _tok_tile(seq)

    def body(y_ref, t_ref, dy_ref, l_ref):
        err = y_ref[...] - t_ref[...]
        dy_ref[...] = err * (1.0 / d)
        part = 0.5 * jnp.sum(jnp.mean(err * err, axis=-1, keepdims=True), axis=0, keepdims=True)
        l_ref[0] = jnp.broadcast_to(part, (SUBLANES, LANES))

    tile = pl.BlockSpec((tm, d), lambda i: (i, 0))
    return pl.pallas_call(
        body, name="loss_head", grid=(m // tm,), in_specs=[tile, tile],
        out_specs=[tile, pl.BlockSpec((1, SUBLANES, LANES), lambda i: (i, 0, 0))],
        out_shape=[jax.ShapeDtypeStruct((m, d), F32), jax.ShapeDtypeStruct((m // tm, SUBLANES, LANES), F32)],
        compiler_params=_params(1),
    )(y, target)


def _residual_io(x, gate, next_norm, tm, tpb):
    m, d = x.shape
    row = lambda i, j: (i, 0)
    vec = lambda i, j: (i // tpb, 0, 0)
    extras = [(x, (tm, d), row), (gate, (1, 1, d), vec)]
    outs = [((m, d), F32, (tm, d), row), ((m, d), BF16, (tm, d), row)]
    if next_norm:
        gn, scale, shift = next_norm
        extras += [(gn, (1, d), lambda i, j: (0, 0)), (scale, (1, 1, d), vec), (shift, (1, 1, d), vec)]
        outs.append(((m, d), BF16, (tm, d), row))
    return extras, outs


def _residual_epilogue(fac, p, e_refs, o_refs):
    x_out = e_refs[0][...] + (fac * (1.0 + e_refs[1][0])) * p
    o_refs[0][...] = x_out
    o_refs[1][...] = p.astype(BF16)
    if len(o_refs) > 2:
        rstd = lax.rsqrt(jnp.mean(x_out * x_out, axis=-1, keepdims=True) + EPS)
        o_refs[2][...] = (x_out * rstd * e_refs[2][...] * (1.0 + e_refs[3][0]) + e_refs[4][0]).astype(BF16)


def _ffn_fwd(x, h, wup, wdown, gate, seq, tag, next_norm=None):
    m, d = x.shape
    tm = _tok_tile(seq)
    tpb = seq // tm

    def up_body(h_ref, wg_ref, wu_ref, a_ref, gu_ref):
        hv = h_ref[...]
        g, u = _dot(hv, wg_ref[...]), _dot(hv, wu_ref[...])
        sg = _sigmoid(g)
        silu = g * sg
        a_ref[...] = (silu * u).astype(BF16)
        gu_ref[0] = (u * (sg * (1.0 + g * (1.0 - sg)))).astype(BF16)
        gu_ref[1] = silu.astype(BF16)

    wblk = (None, d, FF_SHARD)
    a, gu = pl.pallas_call(
        up_body, name=f"ffn_up_{tag}", grid=(N_FF_SHARD, m // tm),
        in_specs=[pl.BlockSpec((tm, d), lambda j, i: (i, 0)),
                  pl.BlockSpec(wblk, lambda j, i: (j, 0, 0)),
                  pl.BlockSpec(wblk, lambda j, i: (j + N_FF_SHARD, 0, 0))],
        out_specs=[pl.BlockSpec((None, tm, FF_SHARD), lambda j, i: (j, i, 0)),
                   pl.BlockSpec((2, None, tm, FF_SHARD), lambda j, i: (0, j, i, 0))],
        out_shape=[jax.ShapeDtypeStruct((N_FF_SHARD, m, FF_SHARD), BF16),
                   jax.ShapeDtypeStruct((2, N_FF_SHARD, m, FF_SHARD), BF16)],
        compiler_params=_params(2),
    )(h, wup, wup)

    if callable(wdown):
        wdown = wdown(a)
    wdown3 = wdown.reshape(N_FF_SHARD, FF_SHARD, d)
    extras, outs = _residual_io(x, gate, next_norm, tm, tpb)
    res = _mm(a, wdown3, mode="nn", tm=tm, tn=d, tk=D_FF, ksub=N_FF_SHARD, name=f"ffn_down_{tag}",
              shape=(m, d, D_FF), a_spec=((N_FF_SHARD, tm, FF_SHARD), lambda i, j, k: (0, i, 0)),
              b_spec=((N_FF_SHARD, FF_SHARD, d), lambda i, j, k: (0, 0, 0)),
              extras=extras, outs=outs, epilogue=functools.partial(_residual_epilogue, 0.5))
    return res[0], (a, gu, res[1]), wdown, (res[2] if next_norm else None)


def _ffn_bwd(dx_out, x, h, saved, wup, wdown, gn, scale, gate, seq, tag, on_grads):
    a, gu, f = saved
    m, d = x.shape
    tm = _tok_tile(seq)
    tpb = seq // tm
    df, dgate_parts = _residual_bwd(dx_out, f, gate, 0.5, seq, f"ffn_res_bwd_{tag}")

    def act_bwd_epilogue(p, e_refs, o_refs):
        o_refs[0][0] = (p * e_refs[0][0].astype(F32)).astype(BF16)
        o_refs[0][1] = (p * e_refs[0][1].astype(F32)).astype(BF16)

    gu_blk = (2, None, tm, FF_SHARD)
    (dgu,) = _mm(df, wdown, mode="nt", tm=tm, tn=FF_SHARD, tk=d, name=f"ffn_down_dx_{tag}", shape=(m, D_FF, d),
                 b_spec=((FF_SHARD, d), lambda i, j, k: (j, 0)),
                 extras=[(gu, gu_blk, lambda i, j: (0, j, i, 0))],
                 outs=[((2, N_FF_SHARD, m, FF_SHARD), BF16, gu_blk, lambda i, j: (0, j, i, 0))],
                 epilogue=act_bwd_epilogue)
    tt = 2 * tm if m % (2 * tm) == 0 else tm
    (dwdown,) = _mm(a, df, mode="tn", tm=FF_SHARD, tn=d, tk=tt, name=f"ffn_dwdown_{tag}", shape=(D_FF, d, m),
                    a_spec=((None, tt, FF_SHARD), lambda i, j, k: (i, k, 0)),
                    outs=[((D_FF, d), BF16, (FF_SHARD, d), lambda i, j: (i, 0))], epilogue=_store_epilogue([BF16]))
    dgu8 = dgu.reshape(2 * N_FF_SHARD, m, FF_SHARD)
    (dwup,) = _mm(h, dgu8, mode="tn", tm=d, tn=FF_SHARD, tk=tt, name=f"ffn_dwup_{tag}", shape=(d, 2 * D_FF, m),
                  b_spec=((None, tt, FF_SHARD), lambda i, j, k: (j, k, 0)),
                  outs=[((2 * N_FF_SHARD, d, FF_SHARD), BF16, (None, d, FF_SHARD), lambda i, j: (j, 0, 0))],
                  epilogue=_store_epilogue([BF16]))
    scale = scale + on_grads(dwup, dwdown)[0, 0]
    dx, nm_parts = _mm(dgu8, wup, mode="nt", tm=tm, tn=d, tk=D_FF, ksub=N_FF_SHARD, name=f"ffn_up_dx_{tag}",
                       shape=(m, d, 2 * D_FF), a_spec=((N_FF_SHARD, tm, FF_SHARD), lambda i, j, k: (k, i, 0)),
                       b_spec=((N_FF_SHARD, d, FF_SHARD), lambda i, j, k: (k, 0, 0)),
                       extras=[(x, (tm, d), lambda i, j: (i, 0)), (dx_out, (tm, d), lambda i, j: (i, 0)),
                               (gn, (1, d), lambda i, j: (0, 0)), (scale, (1, 1, d), lambda i, j: (i // tpb, 0, 0))],
                       outs=[((m, d), F32, (tm, d), lambda i, j: (i, 0)),
                             ((m // tm, SUBLANES, d), F32, (1, SUBLANES, d), lambda i, j: (i, 0, 0))],
                       epilogue=_normmod_bwd_epilogue)
    return dx, nm_parts, dgate_parts


def _shift_down(ext, n, rows):
    if n:
        ext = pltpu.roll(ext, n, 0)
    return ext[SUBLANES:SUBLANES + rows]


def _lru_gates(u, wr_ref, br_ref, wi_ref, bi_ref, lam_ref):
    ub = u.astype(BF16)
    r = _sigmoid(_dot(ub, wr_ref[...]) + br_ref[...])
    ig = _sigmoid(_dot(ub, wi_ref[...]) + bi_ref[...])
    sp = _softplus(-lam_ref[...])
    log_a = (-LRU_C * r) * sp
    a = jnp.exp(log_a)
    mult = jnp.sqrt(_neg_expm1(2.0 * log_a))
    return r, ig, sp, a, mult


def _conv(ext, cw_ref, cb_ref, rows):
    u = cb_ref[...] + cw_ref[3:4, :] * _shift_down(ext, 0, rows)
    for k in range(3):
        u = u + cw_ref[k:k + 1, :] * _shift_down(ext, 3 - k, rows)
    return u


def _lru_halo_spec(seq, ts):
    return pl.BlockSpec((SUBLANES, LRU_W),
                        lambda b, i: (jnp.maximum(b * (seq // SUBLANES) + i * (ts // SUBLANES) - 1, 0), 0))


def _lru_fwd(proj32, conv_w, conv_b, wr, br, wi, bi, lam, batch, seq):
    m = proj32.shape[0]
    ts = _tok_tile(seq)
    nt = seq // ts
    row = lambda b, i: (b * nt + i, 0)

    def body(x_ref, halo_ref, g_ref, cw_ref, cb_ref, wr_ref, br_ref, wi_ref, bi_ref, lam_ref,
             y_ref, h_ref, a_scr, b_scr, carry):
        i = pl.program_id(1)
        halo = jnp.where(i > 0, halo_ref[...], 0.0)
        ext = jnp.concatenate([halo, x_ref[...]], axis=0)
        u = _conv(ext, cw_ref, cb_ref, ts)
        _, ig, _, a, mult = _lru_gates(u, wr_ref, br_ref, wi_ref, bi_ref, lam_ref)
        a_scr[...] = a
        b_scr[...] = mult * (ig * u)

        @pl.when(i == 0)
        def _():
            carry[...] = jnp.zeros_like(carry)

        rid = lax.broadcasted_iota(jnp.int32, (SUBLANES, LRU_W), 0)

        def chunk(c, hprev):
            off = pl.multiple_of(c * SUBLANES, SUBLANES)
            av, bv = a_scr[pl.ds(off, SUBLANES), :], b_scr[pl.ds(off, SUBLANES), :]
            for d in (1, 2, 4):
                keep = rid >= d
                bv = jnp.where(keep, av * pltpu.roll(bv, d, 0) + bv, bv)
                av = jnp.where(keep, av * pltpu.roll(av, d, 0), av)
            h = av * hprev + bv
            h_ref[pl.ds(off, SUBLANES), :] = h
            return h[SUBLANES - 1:SUBLANES, :]

        carry[...] = lax.fori_loop(0, ts // SUBLANES, chunk, carry[...])
        gelu, _ = _gelu_parts(g_ref[...])
        y_ref[...] = h_ref[...] * gelu

    full = lambda shape: pl.BlockSpec(shape, lambda b, i: (0,) * len(shape))
    return pl.pallas_call(
        body, name="lru_fwd", grid=(batch, nt),
        in_specs=[pl.BlockSpec((ts, LRU_W), row), _lru_halo_spec(seq, ts),
                  pl.BlockSpec((ts, LRU_W), lambda b, i: (b * nt + i, 1)),
                  full((4, LRU_W)), full((1, LRU_W)), full((LRU_W, LRU_W)), full((1, LRU_W)),
                  full((LRU_W, LRU_W)), full((1, LRU_W)), full((1, LRU_W))],
        out_specs=[pl.BlockSpec((ts, LRU_W), row), pl.BlockSpec((ts, LRU_W), row)],
        out_shape=[jax.ShapeDtypeStruct((m, LRU_W), F32), jax.ShapeDtypeStruct((m, LRU_W), F32)],
        scratch_shapes=[pltpu.VMEM((ts, LRU_W), F32), pltpu.VMEM((ts, LRU_W), F32), pltpu.VMEM((1, LRU_W), F32)],
        compiler_params=_params(2),
    )(proj32, proj32, proj32, conv_w, conv_b, wr, br, wi, bi, lam)


def _lru_bwd(dy, proj32, h, conv_w, conv_b, wr, br, wi, bi, lam, batch, seq):
    m = proj32.shape[0]
    ts = _tok_tile(seq)
    nt = seq // ts
    row = lambda b, i: (b * nt + (nt - 1 - i), 0)
    halo = pl.BlockSpec((SUBLANES, LRU_W),
                        lambda b, i: (jnp.maximum(b * (seq // SUBLANES) + (nt - 1 - i) * (ts // SUBLANES) - 1, 0), 0))

    def body(dy_ref, x_ref, xhalo_ref, g_ref, h_ref, hhalo_ref, cw_ref, cb_ref, wr_ref, br_ref, wi_ref, bi_ref,
             lam_ref, dx_ref, dg_ref, dwr_ref, dwi_ref, sums_ref, a_scr, dh_scr, g_scr, carry, du_next):
        b, i = pl.program_id(0), pl.program_id(1)
        first_tile = i == nt - 1

        @pl.when((b == 0) & (i == 0))
        def _():
            dwr_ref[...] = jnp.zeros_like(dwr_ref)
            dwi_ref[...] = jnp.zeros_like(dwi_ref)
            sums_ref[...] = jnp.zeros_like(sums_ref)

        @pl.when(i == 0)
        def _():
            carry[...] = jnp.zeros_like(carry)
            du_next[...] = jnp.zeros_like(du_next)

        xhalo = jnp.where(first_tile, 0.0, xhalo_ref[...])
        ext = jnp.concatenate([xhalo, x_ref[...]], axis=0)
        u = _conv(ext, cw_ref, cb_ref, ts)
        r, ig, sp, a, mult = _lru_gates(u, wr_ref, br_ref, wi_ref, bi_ref, lam_ref)
        gelu, dgelu = _gelu_parts(g_ref[...])
        dyv, hv = dy_ref[...], h_ref[...]
        dg_ref[...] = (dyv * hv * dgelu).astype(BF16)
        a_scr[...] = a
        dh_scr[...] = dyv * gelu

        rid = lax.broadcasted_iota(jnp.int32, (SUBLANES, LRU_W), 0)
        nchunk = ts // SUBLANES

        def chunk(n, cg):
            off = pl.multiple_of((nchunk - 1 - n) * SUBLANES, SUBLANES)
            av, beta = a_scr[pl.ds(off, SUBLANES), :], dh_scr[pl.ds(off, SUBLANES), :]
            alpha = jnp.where(rid == SUBLANES - 1, 1.0, pltpu.roll(av, SUBLANES - 1, 0))
            for d in (1, 2, 4):
                keep = rid + d <= SUBLANES - 1
                beta = jnp.where(keep, beta + alpha * pltpu.roll(beta, SUBLANES - d, 0), beta)
                alpha = jnp.where(keep, alpha * pltpu.roll(alpha, SUBLANES - d, 0), alpha)
            gv = beta + alpha * cg
            g_scr[pl.ds(off, SUBLANES), :] = gv
            return av[0:1, :] * gv[0:1, :]

        carry[...] = lax.fori_loop(0, nchunk, chunk, carry[...])
        gv = g_scr[...]
        hhalo = jnp.where(first_tile, 0.0, hhalo_ref[...])
        hprev = _shift_down(jnp.concatenate([hhalo, hv], axis=0), 1, ts)
        dmult = gv * ig * u
        dig = gv * mult * u
        du = gv * mult * ig
        dlog_a = gv * hprev * a - dmult * a * a / mult
        dr = dlog_a * (-LRU_C * sp)
        dr_pre = dr * r * (1.0 - r)
        di_pre = dig * ig * (1.0 - ig)
        drb, dib, ub = dr_pre.astype(BF16), di_pre.astype(BF16), u.astype(BF16)
        du = du + _dot(drb, wr_ref[...], NT) + _dot(dib, wi_ref[...], NT)
        dwr_ref[...] += _dot(ub, drb, TN)
        dwi_ref[...] += _dot(ub, dib, TN)

        ext_du = jnp.concatenate([du, du_next[...]], axis=0)
        du_next[...] = du[0:SUBLANES, :]
        n_ext = ts + SUBLANES
        dx = cw_ref[3:4, :] * du
        sums = [_colsum(dr_pre), _colsum(di_pre), _colsum(dlog_a * (-LRU_C * r)), _colsum(du)]
        dcw = []
        for k in range(3):
            dx = dx + cw_ref[k:k + 1, :] * pltpu.roll(ext_du, n_ext - (3 - k), 0)[0:ts]
            dcw.append(_colsum(du * _shift_down(ext, 3 - k, ts)))
        dcw.append(_colsum(du * _shift_down(ext, 0, ts)))
        dx_ref[...] = dx.astype(BF16)
        sums_ref[...] += _rows_to_block(sums + dcw, LRU_W)

    full = lambda shape: pl.BlockSpec(shape, lambda b, i: (0,) * len(shape))
    tile = pl.BlockSpec((ts, LRU_W), row)
    return pl.pallas_call(
        body, name="lru_bwd", grid=(batch, nt),
        in_specs=[tile, tile, halo, pl.BlockSpec((ts, LRU_W), lambda b, i: (b * nt + (nt - 1 - i), 1)), tile, halo,
                  full((4, LRU_W)), full((1, LRU_W)), full((LRU_W, LRU_W)), full((1, LRU_W)),
                  full((LRU_W, LRU_W)), full((1, LRU_W)), full((1, LRU_W))],
        out_specs=[tile, tile, full((LRU_W, LRU_W)), full((LRU_W, LRU_W)), full((SUBLANES, LRU_W))],
        out_shape=[jax.ShapeDtypeStruct((m, LRU_W), BF16), jax.ShapeDtypeStruct((m, LRU_W), BF16),
                   jax.ShapeDtypeStruct((LRU_W, LRU_W), F32), jax.ShapeDtypeStruct((LRU_W, LRU_W), F32),
                   jax.ShapeDtypeStruct((SUBLANES, LRU_W), F32)],
        scratch_shapes=[pltpu.VMEM((ts, LRU_W), F32), pltpu.VMEM((ts, LRU_W), F32), pltpu.VMEM((ts, LRU_W), F32),
                        pltpu.VMEM((1, LRU_W), F32), pltpu.VMEM((SUBLANES, LRU_W), F32)],
        compiler_params=_params(2),
    )(dy, proj32, proj32, proj32, h, h, conv_w, conv_b, wr, br, wi, bi, lam)


def _head_masks():
    lane = lax.broadcasted_iota(jnp.int32, (1, LANES), 1)
    return lane < HEAD_DIM


def _stack_heads(x2):
    lo, zero = _head_masks(), jnp.zeros_like(x2)
    return jnp.concatenate([jnp.where(lo, x2, zero), jnp.where(lo, zero, x2)], axis=0)


def _unstack_heads(y):
    return jnp.where(_head_masks(), y[:TQ], y[TQ:])


def _stack_cols(a, b):
    return jnp.concatenate([a, b], axis=0)


def _causal(qi, kb, strict):
    r = jnp.bitwise_and(lax.broadcasted_iota(jnp.int32, (2 * TQ, BLK), 0), TQ - 1) + qi * TQ
    c = lax.broadcasted_iota(jnp.int32, (2 * TQ, BLK), 1) + kb * BLK
    return (c < r) if strict else (c <= r)


def _key_loop(qi, group, carry, descending=False):
    def trip(n, cr):
        done = [n * KB_PER_Q + j for j in range(KB_PER_Q)]
        return group([qi * KB_PER_Q - 1 - t for t in done] if descending else done, cr)

    return lax.fori_loop(0, qi, trip, carry)


def _one_by_one(block):
    def group(kbs, carry):
        for kb in kbs:
            carry = block(kb, carry, False)
        return carry
    return group


def _tri(cmp):
    r = lax.broadcasted_iota(jnp.int32, (BLK, BLK), 0)
    c = lax.broadcasted_iota(jnp.int32, (BLK, BLK), 1)
    return cmp(r, c)


def _dot_split(x, tri):
    hi, lo = _split2(x)
    return _dot(hi, tri) + _dot(lo, tri)


def _sb_fwd(proj16, batch, seq):
    nq = seq // TQ
    scale = HEAD_DIM ** -0.5

    def body(q_ref, k_ref, v_ref, y_ref, t_ref):
        qi = pl.program_id(2)
        qs = _stack_heads(q_ref[0])
        tri_after = _tri(lambda r, c: r > c).astype(BF16)

        def block(kb, carry, masked):
            acc, c = carry
            ks = pl.multiple_of(kb * BLK, BLK)
            k2, v2 = k_ref[0, pl.ds(ks, BLK), :], v_ref[0, pl.ds(ks, BLK), :]
            z = _dot(qs, k2, NT) * scale
            sp = _softplus(z)
            l = -sp
            if masked:
                valid = _causal(qi, kb, True)
                l = jnp.where(valid, l, 0.0)
            w = jnp.exp((z - sp) + _dot_split(l, tri_after) + c)
            if masked:
                w = jnp.where(valid, w, 0.0)
            return acc + _dot(w.astype(BF16), v2), c + jnp.sum(l, axis=1, keepdims=True)

        def group(kbs, carry):
            acc, c = carry
            kv = [(k_ref[0, pl.ds(pl.multiple_of(kb * BLK, BLK), BLK), :],
                   v_ref[0, pl.ds(pl.multiple_of(kb * BLK, BLK), BLK), :]) for kb in kbs]
            zs = [_dot(qs, k2, NT) * scale for k2, _ in kv]
            sps = [_softplus(z) for z in zs]
            afters = [_dot_split(-sp, tri_after) for sp in sps]
            for z, sp, after, (_, v2) in zip(zs, sps, afters, kv):
                acc = acc + _dot(jnp.exp((z - sp) + after + c).astype(BF16), v2)
                c = c - jnp.sum(sp, axis=1, keepdims=True)
            return acc, c

        carry = (jnp.zeros((2 * TQ, LANES), F32), jnp.zeros((2 * TQ, 1), F32))
        first = qi * KB_PER_Q
        for n in reversed(range(KB_PER_Q)):
            carry = block(first + n, carry, True)
        acc, c = _key_loop(qi, group, carry, descending=True)
        y_ref[...] = _unstack_heads(acc)
        t_ref[0] = _unstack_heads(jnp.broadcast_to(c, (2 * TQ, LANES)))

    m = batch * seq
    return pl.pallas_call(
        body, name="sb_fwd", grid=(batch, 2, nq),
        in_specs=[pl.BlockSpec((1, TQ, LANES), lambda b, p, q: (b, q, COL_SBQ + p)),
                  pl.BlockSpec((1, seq, LANES), lambda b, p, q: (b, 0, COL_SBK + p)),
                  pl.BlockSpec((1, seq, LANES), lambda b, p, q: (b, 0, COL_SBV + p))],
        out_specs=[pl.BlockSpec((TQ, LANES), lambda b, p, q: (b * nq + q, p)),
                   pl.BlockSpec((1, TQ, LANES), lambda b, p, q: (p, b * nq + q, 0))],
        out_shape=[jax.ShapeDtypeStruct((m, ATT_W), F32), jax.ShapeDtypeStruct((2, m, LANES), F32)],
        compiler_params=_params(3),
    )(proj16, proj16, proj16)


def _sb_bwd(dy, t, proj16, batch, seq):
    nq = seq // TQ
    scale = HEAD_DIM ** -0.5

    def body(dy_ref, t_ref, q_ref, k_ref, v_ref, dq_ref, dk_ref, dv_ref):
        qi = pl.program_id(2)

        @pl.when(qi == 0)
        def _():
            dk_ref[...] = jnp.zeros_like(dk_ref)
            dv_ref[...] = jnp.zeros_like(dv_ref)

        t2 = t_ref[0]
        qs, dys = _stack_heads(q_ref[0]), _stack_heads(dy_ref[...].astype(BF16))
        tot = _stack_cols(t2[:, 0:1], t2[:, HEAD_DIM:HEAD_DIM + 1])
        tri_incl = _tri(lambda r, c: r <= c).astype(BF16)
        tri_excl = _tri(lambda r, c: r < c).astype(BF16)

        def block(kb, carry, masked):
            dq, pc, ec = carry
            ks = pl.multiple_of(kb * BLK, BLK)
            k2, v2 = k_ref[0, pl.ds(ks, BLK), :], v_ref[0, pl.ds(ks, BLK), :]
            z = _dot(qs, k2, NT) * scale
            sp = _softplus(z)
            l, b = -sp, z - sp
            sig = jnp.exp(b)
            if masked:
                valid = _causal(qi, kb, True)
                l = jnp.where(valid, l, 0.0)
            after = tot - (pc + _dot_split(l, tri_incl))
            w = jnp.exp(b + after)
            if masked:
                w = jnp.where(valid, w, 0.0)
            e = _dot(dys, v2, NT) * w
            et = ec + _dot_split(e, tri_excl)
            dz = e * (1.0 - sig) - et * sig
            if masked:
                dz = jnp.where(valid, dz, 0.0)
            dzb = (dz * scale).astype(BF16)
            dk_ref[0, pl.ds(ks, BLK), :] += _dot(dzb, qs, TN)
            dv_ref[0, pl.ds(ks, BLK), :] += _dot(w.astype(BF16), dys, TN)
            return (dq + _dot(dzb, k2), pc + jnp.sum(l, axis=1, keepdims=True),
                    ec + jnp.sum(e, axis=1, keepdims=True))

        def group(kbs, carry):
            dq, pc, ec = carry
            starts = [pl.multiple_of(kb * BLK, BLK) for kb in kbs]
            kv = [(k_ref[0, pl.ds(ks, BLK), :], v_ref[0, pl.ds(ks, BLK), :]) for ks in starts]
            zs = [_dot(qs, k2, NT) * scale for k2, _ in kv]
            dws = [_dot(dys, v2, NT) for _, v2 in kv]
            sps = [_softplus(z) for z in zs]
            pins = [_dot_split(-sp, tri_incl) for sp in sps]
            es, ws, sigs = [], [], []
            for z, sp, pin, dw in zip(zs, sps, pins, dws):
                b = z - sp
                w = jnp.exp(b + (tot - (pc + pin)))
                pc = pc - jnp.sum(sp, axis=1, keepdims=True)
                es.append(dw * w)
                ws.append(w)
                sigs.append(jnp.exp(b))
            eins = [_dot_split(e, tri_excl) for e in es]
            for ks, (k2, _), e, w, sig, ein in zip(starts, kv, es, ws, sigs, eins):
                dzb = ((e * (1.0 - sig) - (ec + ein) * sig) * scale).astype(BF16)
                ec = ec + jnp.sum(e, axis=1, keepdims=True)
                dk_ref[0, pl.ds(ks, BLK), :] += _dot(dzb, qs, TN)
                dv_ref[0, pl.ds(ks, BLK), :] += _dot(w.astype(BF16), dys, TN)
                dq = dq + _dot(dzb, k2)
            return dq, pc, ec

        col = jnp.zeros((2 * TQ, 1), F32)
        first = qi * KB_PER_Q
        carry = _key_loop(qi, group, (jnp.zeros((2 * TQ, LANES), F32), col, col))
        for n in range(KB_PER_Q):
            carry = block(first + n, carry, True)
        dq_ref[...] = _unstack_heads(carry[0])

    m = batch * seq
    whole = lambda col: pl.BlockSpec((1, seq, LANES), lambda b, p, q: (b, 0, col + p))
    return pl.pallas_call(
        body, name="sb_bwd", grid=(batch, 2, nq),
        in_specs=[pl.BlockSpec((TQ, LANES), lambda b, p, q: (b * nq + q, p)),
                  pl.BlockSpec((1, TQ, LANES), lambda b, p, q: (p, b * nq + q, 0)),
                  pl.BlockSpec((1, TQ, LANES), lambda b, p, q: (b, q, COL_SBQ + p)),
                  whole(COL_SBK), whole(COL_SBV)],
        out_specs=[pl.BlockSpec((TQ, LANES), lambda b, p, q: (b * nq + q, p)), whole(0), whole(0)],
        out_shape=[jax.ShapeDtypeStruct((m, ATT_W), F32), jax.ShapeDtypeStruct((batch, seq, ATT_W), F32),
                   jax.ShapeDtypeStruct((batch, seq, ATT_W), F32)],
        compiler_params=_params(3),
    )(dy, t, proj16, proj16, proj16)


def _fox_pre(proj32, gq, gk, bf, group_mean, batch, seq):
    m = proj32.shape[0]
    ts = _tok_tile(seq)
    nt = seq // ts

    def body(q_ref, k_ref, f_ref, gq_ref, gk_ref, bf_ref, gm_ref, fq_ref, fk_ref, fc_ref, carry):
        i = pl.program_id(1)

        @pl.when(i == 0)
        def _():
            carry[...] = jnp.zeros_like(carry)

        gm = gm_ref[...]
        for src, g_ref, dst in ((q_ref, gq_ref, fq_ref), (k_ref, gk_ref, fk_ref)):
            v = src[...]
            ms = _dot_split(v * v, gm)
            dst[...] = (v * lax.rsqrt(ms + EPS) * g_ref[...]).astype(BF16)
        z = f_ref[...] + bf_ref[...]
        lf = jnp.minimum(z, 0.0) - jnp.log(1.0 + jnp.exp(-jnp.abs(z)))
        r = lax.broadcasted_iota(jnp.int32, (ts, ts), 0)
        c = lax.broadcasted_iota(jnp.int32, (ts, ts), 1)
        tri = (r >= c).astype(BF16)
        hi, mid, low = _split3(lf)
        fc = _dot(tri, hi) + _dot(tri, mid) + _dot(tri, low) + carry[...]
        fc_ref[...] = fc
        carry[...] = fc[ts - 1:ts, :]

    full = lambda shape: pl.BlockSpec(shape, lambda b, i: (0,) * len(shape))
    return pl.pallas_call(
        body, name="fox_pre", grid=(batch, nt),
        in_specs=[pl.BlockSpec((ts, ATT_W), lambda b, i: (b * nt + i, 7)),
                  pl.BlockSpec((ts, ATT_W), lambda b, i: (b * nt + i, 8)),
                  pl.BlockSpec((ts, LANES), lambda b, i: (b * nt + i, COL_FXF)),
                  full((1, ATT_W)), full((1, ATT_W)), full((1, LANES)), full((ATT_W, ATT_W))],
        out_specs=[pl.BlockSpec((ts, ATT_W), lambda b, i: (b * nt + i, 0)),
                   pl.BlockSpec((ts, ATT_W), lambda b, i: (b * nt + i, 0)),
                   pl.BlockSpec((ts, LANES), lambda b, i: (b * nt + i, 0))],
        out_shape=[jax.ShapeDtypeStruct((m, ATT_W), BF16), jax.ShapeDtypeStruct((m, ATT_W), BF16),
                   jax.ShapeDtypeStruct((m, LANES), F32)],
        scratch_shapes=[pltpu.VMEM((1, LANES), F32)],
        compiler_params=_params(2),
    )(proj32, proj32, proj32, gq, gk, bf, group_mean)


def _fox_specs(batch, seq):
    nq = seq // TQ
    return dict(
        qblk=pl.BlockSpec((1, TQ, LANES), lambda b, p, q: (b, q, p)),
        whole=pl.BlockSpec((1, seq, LANES), lambda b, p, q: (b, 0, p)),
        vwhole=pl.BlockSpec((1, seq, LANES), lambda b, p, q: (b, 0, COL_FXV + p)),
        fcol=pl.BlockSpec((1, 1, TQ, 2), lambda b, p, q: (b, p, q, 0)),
        frow=pl.BlockSpec((1, 1, 2, seq), lambda b, p, q: (b, p, 0, 0)),
        rows=pl.BlockSpec((TQ, LANES), lambda b, p, q: (b * nq + q, p)),
        stat=pl.BlockSpec((1, TQ, LANES), lambda b, p, q: (p, b * nq + q, 0)),
    )


def _fox_logits(qs, k2, fq_col, fr_ref, ks, is_a, scale):
    fk_row = jnp.where(is_a, fr_ref[0, 0, 0:1, pl.ds(ks, BLK)], fr_ref[0, 0, 1:2, pl.ds(ks, BLK)])
    return _dot(qs, k2, NT) * scale + fq_col - fk_row


def _fox_fwd(fq, fk, proj16, fcol, frow, batch, seq):
    nq = seq // TQ
    scale = HEAD_DIM ** -0.5

    def body(q_ref, k_ref, v_ref, fc_ref, fr_ref, y_ref, lse_ref):
        qi = pl.program_id(2)
        qs = _stack_heads(q_ref[0])
        fcv = fc_ref[0, 0]
        fq_col = _stack_cols(fcv[:, 0:1], fcv[:, 1:2])
        is_a = lax.broadcasted_iota(jnp.int32, (2 * TQ, 1), 0) < TQ

        def block(kb, carry, masked):
            acc, mx, den = carry
            ks = pl.multiple_of(kb * BLK, BLK)
            k2, v2 = k_ref[0, pl.ds(ks, BLK), :], v_ref[0, pl.ds(ks, BLK), :]
            s = _fox_logits(qs, k2, fq_col, fr_ref, ks, is_a, scale)
            if masked:
                s = jnp.where(_causal(qi, kb, False), s, NEG_BIG)
            mx_new = jnp.maximum(mx, jnp.max(s, axis=1, keepdims=True))
            p = jnp.exp(s - mx_new)
            alpha = jnp.exp(mx - mx_new)
            return (alpha * acc + _dot(p.astype(BF16), v2), mx_new, alpha * den + jnp.sum(p, axis=1, keepdims=True))

        first = qi * KB_PER_Q
        carry = (jnp.zeros((2 * TQ, LANES), F32), jnp.full((2 * TQ, 1), NEG_BIG, F32), jnp.zeros((2 * TQ, 1), F32))
        carry = _key_loop(qi, _one_by_one(block), carry)
        for n in range(KB_PER_Q):
            carry = block(first + n, carry, True)
        acc, mx, den = carry
        y_ref[...] = _unstack_heads(acc / den)
        lse_ref[0] = _unstack_heads(jnp.broadcast_to(mx + jnp.log(den), (2 * TQ, LANES)))

    m = batch * seq
    sp = _fox_specs(batch, seq)
    return pl.pallas_call(
        body, name="fox_fwd", grid=(batch, 2, nq),
        in_specs=[sp["qblk"], sp["whole"], sp["vwhole"], sp["fcol"], sp["frow"]],
        out_specs=[sp["rows"], sp["stat"]],
        out_shape=[jax.ShapeDtypeStruct((m, ATT_W), F32), jax.ShapeDtypeStruct((2, m, LANES), F32)],
        compiler_params=_params(3),
    )(fq, fk, proj16, fcol, frow)


def _fox_bwd(dy, y, lse, fq, fk, proj16, fcol, frow, batch, seq):
    nq = seq // TQ
    scale = HEAD_DIM ** -0.5

    def body(dy_ref, y_ref, lse_ref, q_ref, k_ref, v_ref, fc_ref, fr_ref, dq_ref, dk_ref, dv_ref, dfr_ref, dfc_ref):
        qi = pl.program_id(2)

        @pl.when(qi == 0)
        def _():
            dk_ref[...] = jnp.zeros_like(dk_ref)
            dv_ref[...] = jnp.zeros_like(dv_ref)
            dfr_ref[...] = jnp.zeros_like(dfr_ref)

        lo = _head_masks()
        lane = lax.broadcasted_iota(jnp.int32, (1, LANES), 1)
        dy2, lse2, fcv = dy_ref[...], lse_ref[0], fc_ref[0, 0]
        qs, dys = _stack_heads(q_ref[0]), _stack_heads(dy2.astype(BF16))
        dyy = dy2 * y_ref[...]
        delta = _stack_cols(jnp.sum(jnp.where(lo, dyy, 0.0), axis=1, keepdims=True),
                            jnp.sum(jnp.where(lo, 0.0, dyy), axis=1, keepdims=True))
        lse_col = _stack_cols(lse2[:, 0:1], lse2[:, HEAD_DIM:HEAD_DIM + 1])
        fq_col = _stack_cols(fcv[:, 0:1], fcv[:, 1:2])
        is_a = lax.broadcasted_iota(jnp.int32, (2 * TQ, 1), 0) < TQ

        def block(kb, carry, masked):
            dq, rs = carry
            ks = pl.multiple_of(kb * BLK, BLK)
            k2, v2 = k_ref[0, pl.ds(ks, BLK), :], v_ref[0, pl.ds(ks, BLK), :]
            p = jnp.exp(_fox_logits(qs, k2, fq_col, fr_ref, ks, is_a, scale) - lse_col)
            if masked:
                p = jnp.where(_causal(qi, kb, False), p, 0.0)
            ds = p * (_dot(dys, v2, NT) - delta)
            dsb = (ds * scale).astype(BF16)
            dk_ref[0, pl.ds(ks, BLK), :] += _dot(dsb, qs, TN)
            dv_ref[0, pl.ds(ks, BLK), :] += _dot(p.astype(BF16), dys, TN)
            dfr_ref[0, 0, 0:1, pl.ds(ks, BLK)] -= jnp.sum(ds[:TQ], axis=0, keepdims=True)
            dfr_ref[0, 0, 1:2, pl.ds(ks, BLK)] -= jnp.sum(ds[TQ:], axis=0, keepdims=True)
            return dq + _dot(dsb, k2), rs + jnp.sum(ds, axis=1, keepdims=True)

        def group(kbs, carry):
            dq, rs = carry
            starts = [pl.multiple_of(kb * BLK, BLK) for kb in kbs]
            kv = [(k_ref[0, pl.ds(ks, BLK), :], v_ref[0, pl.ds(ks, BLK), :]) for ks in starts]
            ss = [_fox_logits(qs, k2, fq_col, fr_ref, ks, is_a, scale) for ks, (k2, _) in zip(starts, kv)]
            dps = [_dot(dys, v2, NT) for _, v2 in kv]
            ps = [jnp.exp(s - lse_col) for s in ss]
            dss = [p * (dp - delta) for p, dp in zip(ps, dps)]
            for ks, (k2, _), p, ds in zip(starts, kv, ps, dss):
                dsb = (ds * scale).astype(BF16)
                dk_ref[0, pl.ds(ks, BLK), :] += _dot(dsb, qs, TN)
                dv_ref[0, pl.ds(ks, BLK), :] += _dot(p.astype(BF16), dys, TN)
                dfr_ref[0, 0, 0:1, pl.ds(ks, BLK)] -= jnp.sum(ds[:TQ], axis=0, keepdims=True)
                dfr_ref[0, 0, 1:2, pl.ds(ks, BLK)] -= jnp.sum(ds[TQ:], axis=0, keepdims=True)
                dq = dq + _dot(dsb, k2)
            return dq, rs + jnp.sum(functools.reduce(jnp.add, dss), axis=1, keepdims=True)

        first = qi * KB_PER_Q
        carry = _key_loop(qi, group, (jnp.zeros((2 * TQ, LANES), F32), jnp.zeros((2 * TQ, 1), F32)))
        for n in range(KB_PER_Q):
            carry = block(first + n, carry, True)
        dq, rs = carry
        dq_ref[...] = _unstack_heads(dq)
        dfc_ref[0] = jnp.where(lane == 0, rs[:TQ], jnp.where(lane == 1, rs[TQ:], 0.0))

    m = batch * seq
    sp = _fox_specs(batch, seq)
    return pl.pallas_call(
        body, name="fox_bwd", grid=(batch, 2, nq),
        in_specs=[sp["rows"], sp["rows"], sp["stat"], sp["qblk"], sp["whole"], sp["vwhole"], sp["fcol"], sp["frow"]],
        out_specs=[sp["rows"], sp["whole"], sp["whole"],
                   pl.BlockSpec((1, 1, SUBLANES, seq), lambda b, p, q: (b, p, 0, 0)), sp["stat"]],
        out_shape=[jax.ShapeDtypeStruct((m, ATT_W), F32), jax.ShapeDtypeStruct((batch, seq, ATT_W), F32),
                   jax.ShapeDtypeStruct((batch, seq, ATT_W), F32),
                   jax.ShapeDtypeStruct((batch, 2, SUBLANES, seq), F32), jax.ShapeDtypeStruct((2, m, LANES), F32)],
        compiler_params=_params(3),
    )(dy, y, lse, fq, fk, proj16, fcol, frow)


def _fox_post_bwd(dfq, dfk, dfc, proj32, gq, gk, bf, group_mean, batch, seq):
    m = proj32.shape[0]
    ts = _tok_tile(seq)
    nt = seq // ts
    tile = lambda w, col: pl.BlockSpec((ts, w), lambda b, i: (b * nt + (nt - 1 - i), col))

    def body(dfq_ref, dfk_ref, dfc_ref, q_ref, k_ref, f_ref, gq_ref, gk_ref, bf_ref, gm_ref,
             dq_ref, dk_ref, df_ref, gs_ref, bs_ref, carry):
        i = pl.program_id(1)

        @pl.when(i == 0)
        def _():
            carry[...] = jnp.zeros_like(carry)

        gm = gm_ref[...]
        rows = []
        for src, g_ref, d_ref, dst in ((q_ref, gq_ref, dfq_ref, dq_ref), (k_ref, gk_ref, dfk_ref, dk_ref)):
            v, dv = src[...], d_ref[...]
            rstd = lax.rsqrt(_dot_split(v * v, gm) + EPS)
            vhat = v * rstd
            rows.append(_colsum(dv * vhat))
            dvh = dv * g_ref[...]
            dst[...] = (rstd * (dvh - vhat * _dot_split(dvh * vhat, gm))).astype(BF16)
        gs_ref[0] = _rows_to_block(rows, ATT_W)

        dfc_v = dfc_ref[...]
        r = lax.broadcasted_iota(jnp.int32, (ts, ts), 0)
        c = lax.broadcasted_iota(jnp.int32, (ts, ts), 1)
        tri = (r <= c).astype(BF16)
        hi, mid, low = _split3(dfc_v)
        dlf = _dot(tri, hi) + _dot(tri, mid) + _dot(tri, low) + carry[...]
        carry[...] = dlf[0:1, :]
        z = f_ref[...] + bf_ref[...]
        dz = dlf * _sigmoid(-z)
        df_ref[...] = dz.astype(BF16)
        bs_ref[0] = _rows_to_block([_colsum(dz)], LANES)

    full = lambda shape: pl.BlockSpec(shape, lambda b, i: (0,) * len(shape))
    part = lambda w: pl.BlockSpec((1, SUBLANES, w), lambda b, i: (b * nt + (nt - 1 - i), 0, 0))
    return pl.pallas_call(
        body, name="fox_post_bwd", grid=(batch, nt),
        in_specs=[tile(ATT_W, 0), tile(ATT_W, 0), tile(LANES, 0), tile(ATT_W, 7), tile(ATT_W, 8), tile(LANES, COL_FXF),
                  full((1, ATT_W)), full((1, ATT_W)), full((1, LANES)), full((ATT_W, ATT_W))],
        out_specs=[tile(ATT_W, 0), tile(ATT_W, 0), tile(LANES, 0), part(ATT_W), part(LANES)],
        out_shape=[jax.ShapeDtypeStruct((m, ATT_W), BF16), jax.ShapeDtypeStruct((m, ATT_W), BF16),
                   jax.ShapeDtypeStruct((m, LANES), BF16),
                   jax.ShapeDtypeStruct((batch * nt, SUBLANES, ATT_W), F32),
                   jax.ShapeDtypeStruct((batch * nt, SUBLANES, LANES), F32)],
        scratch_shapes=[pltpu.VMEM((1, LANES), F32)],
        compiler_params=_params(2),
    )(dfq, dfk, dfc, proj32, proj32, proj32, gq, gk, bf, group_mean)


_GROUPS = ((0, LRU_W), (LRU_W, LRU_W + ATT_W), (LRU_W + ATT_W, LRU_W + 2 * ATT_W))


def _outnorm(y_lru, y_sb, y_fox, gmix, seq):
    m = y_lru.shape[0]
    tm = _tok_tile(seq)

    def body(a_ref, b_ref, c_ref, g_ref, o_ref):
        parts = []
        for ref in (a_ref, b_ref, c_ref):
            v = ref[...]
            parts.append(v * lax.rsqrt(jnp.mean(v * v, axis=-1, keepdims=True) + EPS))
        o_ref[...] = (jnp.concatenate(parts, axis=1) * g_ref[...]).astype(BF16)

    t = lambda w: pl.BlockSpec((tm, w), lambda i: (i, 0))
    return pl.pallas_call(
        body, name="outnorm", grid=(m // tm,),
        in_specs=[t(LRU_W), t(ATT_W), t(ATT_W), pl.BlockSpec((1, D_MODEL), lambda i: (0, 0))],
        out_specs=t(D_MODEL), out_shape=jax.ShapeDtypeStruct((m, D_MODEL), BF16), compiler_params=_params(1),
    )(y_lru, y_sb, y_fox, gmix)


def _outnorm_bwd_epilogue(p, e_refs, o_refs):
    gmix = e_refs[3][...]
    dg = []
    for n, (lo, hi) in enumerate(_GROUPS):
        v, dyn = e_refs[n][...], p[:, lo:hi]
        rstd = lax.rsqrt(jnp.mean(v * v, axis=-1, keepdims=True) + EPS)
        vhat = v * rstd
        dg.append(_colsum(dyn * vhat))
        dvh = dyn * gmix[:, lo:hi]
        o_refs[n][...] = rstd * (dvh - vhat * jnp.mean(dvh * vhat, axis=-1, keepdims=True))
    o_refs[3][0] = _rows_to_block([jnp.concatenate(dg, axis=1)], p.shape[1])


def _pair_layouts(fcum, batch, seq):
    f4 = fcum[:, :4].reshape(batch, seq, 2, 2)
    return f4.transpose(0, 2, 1, 3), f4.transpose(0, 2, 3, 1)


def _gate_grad_cols(dfr, dfc, batch, seq):
    keys = dfr[:, :, :2, :].transpose(0, 3, 1, 2).reshape(batch * seq, 4)
    queries = dfc[:, :, :2].transpose(1, 0, 2).reshape(batch * seq, 4)
    return jnp.pad(keys + queries, ((0, 0), (0, LANES - 4)))


def _mixer_fwd(x, h, w, gate, batch, seq, next_norm=None):
    m, d = x.shape
    tm = _tok_tile(seq)
    tpb = seq // tm

    def in_epilogue(p, e_refs, o_refs):
        o_refs[0][...] = p
        o_refs[1][...] = p.astype(BF16)

    tn_in = 896
    proj32, proj16 = _mm(h, w["w_in"], mode="nn", tm=tm, tn=tn_in, tk=d, name="mix_in",
                         outs=[((m, N_IN_PAD), F32, (tm, tn_in), lambda i, j: (i, j)),
                               ((m, N_IN_PAD), BF16, (tm, tn_in), lambda i, j: (i, j))],
                         epilogue=in_epilogue)
    y_lru, h_lru = _lru_fwd(proj32, w["conv_w"], w["conv_b"], w["wr"], w["br"], w["wi"], w["bi"], w["lam"], batch, seq)
    p16 = proj16.reshape(batch, seq, N_IN_PAD)
    y_sb, t_sb = _sb_fwd(p16, batch, seq)
    fq, fk, fcum = _fox_pre(proj32, w["gq"], w["gk"], w["bf"], w["group_mean"], batch, seq)
    fcol, frow = _pair_layouts(fcum, batch, seq)
    fq3, fk3 = fq.reshape(batch, seq, ATT_W), fk.reshape(batch, seq, ATT_W)
    y_fox, lse = _fox_fwd(fq3, fk3, p16, fcol, frow, batch, seq)
    ynorm = _outnorm(y_lru, y_sb, y_fox, w["gmix"], seq)

    extras, outs = _residual_io(x, gate, next_norm, tm, tpb)
    res = _mm(ynorm, w["w_out"], mode="nn", tm=tm, tn=d, tk=d, name="mix_out", extras=extras, outs=outs,
              epilogue=functools.partial(_residual_epilogue, 1.0))
    saved = dict(proj32=proj32, p16=p16, h_lru=h_lru, y_lru=y_lru, y_sb=y_sb, t_sb=t_sb, fq3=fq3, fk3=fk3,
                 fcol=fcol, frow=frow, y_fox=y_fox, lse=lse, ynorm=ynorm, out=res[1])
    return res[0], saved, (res[2] if next_norm else None)


def _mixer_bwd(dx_out, x, h, s, w, gn, scale, gate, batch, seq, on_grads):
    m, d = x.shape
    tm = _tok_tile(seq)
    tpb = seq // tm
    dout, dgate_parts = _residual_bwd(dx_out, s["out"], gate, 1.0, seq, "mix_res_bwd")
    (dw_out,) = _mm(s["ynorm"], dout, mode="tn", tm=d, tn=d, tk=tm, name="mix_dwout",
                    outs=[((d, d), BF16, (d, d), lambda i, j: (i, j))], epilogue=_store_epilogue([BF16]))
    dy_lru, dy_sb, dy_fox, gmix_parts = _mm(
        dout, w["w_out"], mode="nt", tm=tm, tn=d, tk=d, name="mix_out_dx",
        extras=[(s["y_lru"], (tm, LRU_W), lambda i, j: (i, 0)), (s["y_sb"], (tm, ATT_W), lambda i, j: (i, 0)),
                (s["y_fox"], (tm, ATT_W), lambda i, j: (i, 0)), (w["gmix"], (1, d), lambda i, j: (0, 0))],
        outs=[((m, LRU_W), F32, (tm, LRU_W), lambda i, j: (i, 0)), ((m, ATT_W), F32, (tm, ATT_W), lambda i, j: (i, 0)),
              ((m, ATT_W), F32, (tm, ATT_W), lambda i, j: (i, 0)),
              ((m // tm, SUBLANES, d), F32, (1, SUBLANES, d), lambda i, j: (i, 0, 0))],
        epilogue=_outnorm_bwd_epilogue)

    dsq, dsk, dsv = _sb_bwd(dy_sb, s["t_sb"], s["p16"], batch, seq)
    dfq, dfk, dfv, dfr, dfc = _fox_bwd(dy_fox, s["y_fox"], s["lse"], s["fq3"], s["fk3"], s["p16"], s["fcol"],
                                       s["frow"], batch, seq)
    dfc_cols = _gate_grad_cols(dfr, dfc, batch, seq)
    dxq, dxk, dxf, gqk_parts, bf_parts = _fox_post_bwd(dfq, dfk.reshape(m, ATT_W), dfc_cols, s["proj32"],
                                                       w["gq"], w["gk"], w["bf"], w["group_mean"], batch, seq)
    dlx, dlg, dwr, dwi, lru_sums = _lru_bwd(dy_lru, s["proj32"], s["h_lru"], w["conv_w"], w["conv_b"], w["wr"],
                                            w["br"], w["wi"], w["bi"], w["lam"], batch, seq)
    dproj = jnp.concatenate([dlx, dlg, dsq.astype(BF16), dsk.reshape(m, ATT_W).astype(BF16),
                             dsv.reshape(m, ATT_W).astype(BF16), dxq, dxk, dfv.reshape(m, ATT_W).astype(BF16), dxf],
                            axis=1)
    tn_in = 896
    (dw_in,) = _mm(h, dproj, mode="tn", tm=d, tn=tn_in, tk=tm, name="mix_dwin",
                   outs=[((d, N_IN_PAD), BF16, (d, tn_in), lambda i, j: (i, j))], epilogue=_store_epilogue([BF16]))
    scale = scale + on_grads(dw_in, dw_out)[0, 0]
    dx, nm_parts = _mm(dproj, w["w_in"], mode="nt", tm=tm, tn=d, tk=tn_in, name="mix_in_dx",
                       extras=[(x, (tm, d), lambda i, j: (i, 0)), (dx_out, (tm, d), lambda i, j: (i, 0)),
                               (gn, (1, d), lambda i, j: (0, 0)), (scale, (1, 1, d), lambda i, j: (i // tpb, 0, 0))],
                       outs=[((m, d), F32, (tm, d), lambda i, j: (i, 0)),
                             ((m // tm, SUBLANES, d), F32, (1, SUBLANES, d), lambda i, j: (i, 0, 0))],
                       epilogue=_normmod_bwd_epilogue)
    grads = dict(dwr=dwr, dwi=dwi, lru_sums=lru_sums, gmix_parts=gmix_parts,
                 gqk_parts=gqk_parts, bf_parts=bf_parts)
    return dx, grads, nm_parts, dgate_parts


def _block_diag(w):
    nb = w.shape[0]
    eye = jnp.eye(nb, dtype=w.dtype)
    return (eye[:, None, :, None] * w[:, :, None, :]).reshape(nb * HEAD_DIM, nb * HEAD_DIM)


def _block_diag_grad(g):
    nb = LRU_W // HEAD_DIM
    g4 = g.reshape(nb, HEAD_DIM, nb, HEAD_DIM)
    return jnp.stack([g4[n, :, n, :] for n in range(nb)])


def _per_batch(parts, batch, row):
    r = parts[:, row, :]
    return r.reshape(batch, -1, r.shape[-1]).sum(axis=1)


def _local_step(x3, target3, mod, wts, big_weights):
    batch, seq, d = x3.shape
    assert seq % TQ == 0, seq
    m = batch * seq
    n_layers = mod.shape[0]
    x = x3.reshape(m, d)
    group_mean = _block_diag(jnp.full((ATT_W // HEAD_DIM, HEAD_DIM, HEAD_DIM), 1.0 / HEAD_DIM, BF16))
    vec = lambda l, j, t: mod[l, :, j, t][:, None, :]

    layers, saved = [], []
    for l in range(n_layers):
        gq = jnp.tile(wts["g_qk"][l, 0], ATT_W // HEAD_DIM)[None, :]
        gk = jnp.tile(wts["g_qk"][l, 1], ATT_W // HEAD_DIM)[None, :]
        bf = jnp.pad(wts["b_fgate"][l], (0, LANES - 4))[None, :]
        lw = dict(conv_w=wts["conv_w"][l],
                  conv_b=wts["conv_b"][l][None, :], wr=_block_diag(wts["w_rgate"][l]).astype(BF16),
                  br=wts["b_rgate"][l][None, :], wi=_block_diag(wts["w_igate"][l]).astype(BF16),
                  bi=wts["b_igate"][l][None, :], lam=wts["lru_lambda"][l][None, :], gq=gq, gk=gk, bf=bf,
                  group_mean=group_mean, gmix=wts["g_mix_out"][l][None, :])
        layers.append(lw)
        gn = lambda j: wts["g_norm"][l, j][None, :]
        norm_of = lambda ll, j: (wts["g_norm"][ll, j][None, :], vec(ll, j, 1), vec(ll, j, 0))
        sv = dict(x0=x)
        sv["h0"] = h_next if l else _normmod(x, *norm_of(0, 0), seq, "normmod_first")
        wup, wdown = big_weights(l, "ffn0", sv["h0"])
        x, sv["ffn0"], wdown, sv["h1"] = _ffn_fwd(x, sv["h0"], wup, wdown, vec(l, 0, 2), seq, f"{l}_0", norm_of(l, 1))
        sv["w_ffn0"] = (wup, wdown)
        sv["x1"] = x
        lw["w_in"], lw["w_out"] = big_weights(l, "mix", sv["h1"])
        x, sv["mix"], sv["h2"] = _mixer_fwd(x, sv["h1"], lw, vec(l, 1, 2), batch, seq, norm_of(l, 2))
        sv["x2"] = x
        wup, wdown = big_weights(l, "ffn1", sv["h2"])
        x, sv["ffn1"], wdown, h_next = _ffn_fwd(x, sv["h2"], wup, wdown, vec(l, 2, 2), seq, f"{l}_1",
                                                norm_of(l + 1, 0) if l + 1 < n_layers else None)
        sv["w_ffn1"] = (wup, wdown)
        saved.append(sv)

    dx, loss_parts = _loss_head(x, target3.reshape(m, d), seq)
    loss = jnp.sum(loss_parts[:, 0, 0])

    handles = {}

    def scatter(key, shapes):
        def on_grads(*grads):
            ops = [(g.reshape(shape), 0, "scatter") for g, shape in zip(grads, shapes)]
            handles[key], token = _flight_start(ops, f"grads_{key[0]}_{key[1]}_start")
            return token
        return on_grads

    ffn_shapes = ((2 * N_FF_SHARD, d, FF_SHARD), (N_DEV, D_FF // N_DEV, d))
    mix_shapes = ((N_DEV, d // N_DEV, N_IN_PAD), (N_DEV, d // N_DEV, d))
    small = {k: [] for k in ("dmod", "g_norm", "b_fgate", "conv_w", "conv_b", "w_rgate", "b_rgate", "w_igate",
                             "b_igate", "lru_lambda", "g_qk", "g_mix_out")}
    for l in reversed(range(n_layers)):
        sv, lw = saved[l], layers[l]
        gn = lambda j: wts["g_norm"][l, j][None, :]
        dx, nm2, dg2 = _ffn_bwd(dx, sv["x2"], sv["h2"], sv["ffn1"], *sv["w_ffn1"], gn(2), vec(l, 2, 1), vec(l, 2, 2),
                                seq, f"{l}_1", scatter((l, "ffn1"), ffn_shapes))
        dx, mg, nm1, dg1 = _mixer_bwd(dx, sv["x1"], sv["h1"], sv["mix"], lw, gn(1), vec(l, 1, 1), vec(l, 1, 2),
                                      batch, seq, scatter((l, "mix"), mix_shapes))
        dx, nm0, dg0 = _ffn_bwd(dx, sv["x0"], sv["h0"], sv["ffn0"], *sv["w_ffn0"], gn(0), vec(l, 0, 1), vec(l, 0, 2),
                                seq, f"{l}_0", scatter((l, "ffn0"), ffn_shapes))
        dmod_l, gnorm_l = [], []
        for nm, dg in ((nm0, dg0), (nm1, dg1), (nm2, dg2)):
            dmod_l.append(jnp.stack([_per_batch(nm, batch, 0), _per_batch(nm, batch, 1), _per_batch(dg, batch, 0)],
                                    axis=1))
            gnorm_l.append(jnp.sum(nm[:, 2, :], axis=0))
        small["dmod"].insert(0, jnp.stack(dmod_l, axis=1))
        small["g_norm"].insert(0, jnp.stack(gnorm_l))
        ls = mg["lru_sums"]
        small["b_rgate"].insert(0, ls[0])
        small["b_igate"].insert(0, ls[1])
        small["lru_lambda"].insert(0, ls[2] * (-_sigmoid(-wts["lru_lambda"][l])))
        small["conv_b"].insert(0, ls[3])
        small["conv_w"].insert(0, ls[4:8])
        small["w_rgate"].insert(0, _block_diag_grad(mg["dwr"]))
        small["w_igate"].insert(0, _block_diag_grad(mg["dwi"]))
        small["g_mix_out"].insert(0, jnp.sum(mg["gmix_parts"][:, 0, :], axis=0))
        gqk = jnp.sum(mg["gqk_parts"][:, :2, :], axis=0).reshape(2, ATT_W // HEAD_DIM, HEAD_DIM).sum(axis=1)
        small["g_qk"].insert(0, gqk)
        small["b_fgate"].insert(0, jnp.sum(mg["bf_parts"][:, 0, :4], axis=0))
    small = {k: jnp.stack(v) for k, v in small.items()}
    return loss, dx.reshape(batch, seq, d), handles, small


def _row_tile(rows, row_bytes):
    for t in (512, 256, 128, 64, 32, 16):
        if rows % t == 0 and t * row_bytes <= 4 * 1024 * 1024:
            return t
    return rows


def _adamw(parts, w, m, v, name):
    groups, n_parts, rows, cols = parts.shape
    tr = _row_tile(rows, cols * (n_parts * parts.dtype.itemsize + 7 * 4))
    c1 = 1.0 - ADAM_B1 ** ADAM_STEP
    c2 = 1.0 - ADAM_B2 ** ADAM_STEP

    def body(p_ref, w_ref, m_ref, v_ref, g_out, d_out, m_out, v_out):
        g = p_ref[0].astype(F32)
        for n in range(1, n_parts):
            g = g + p_ref[n].astype(F32)
        m_new = ADAM_B1 * m_ref[...] + (1.0 - ADAM_B1) * g
        v_new = ADAM_B2 * v_ref[...] + (1.0 - ADAM_B2) * (g * g)
        g_out[...] = g
        d_out[...] = -ADAM_LR * ((m_new / c1) / (jnp.sqrt(v_new / c2) + ADAM_EPS) + ADAM_WD * w_ref[...])
        m_out[...] = m_new
        v_out[...] = v_new

    tile = pl.BlockSpec((None, tr, cols), lambda g, i: (g, i, 0))
    return pl.pallas_call(
        body, name=name, grid=(groups, rows // tr),
        in_specs=[pl.BlockSpec((None, n_parts, tr, cols), lambda g, i: (g, 0, i, 0)), tile, tile, tile],
        out_specs=[tile] * 4, out_shape=[jax.ShapeDtypeStruct((groups, rows, cols), F32)] * 4,
        compiler_params=_params(2),
    )(parts, w, m, v)


def _sum_parts(parts):
    n_parts, rows, cols = parts.shape

    def body(p_ref, o_ref):
        acc = p_ref[0]
        for n in range(1, n_parts):
            acc = acc + p_ref[n]
        o_ref[...] = acc

    return pl.pallas_call(body, name="sum_small", out_shape=jax.ShapeDtypeStruct((rows, cols), F32),
                          compiler_params=pltpu.CompilerParams(vmem_limit_bytes=VMEM_LIMIT_BYTES))(parts)


def _flatten(arrays, multiple):
    chunks = []
    for a in arrays:
        flat = a.reshape(-1).astype(F32)
        chunks.append(jnp.pad(flat, (0, (-flat.shape[0]) % multiple)).reshape(-1, LANES))
    return jnp.concatenate(chunks, axis=0)


def _unflatten(flat2d, shapes, multiple):
    flat2d, out, row = flat2d.reshape(-1, LANES), [], 0
    for s in shapes:
        n = math.prod(s)
        rows = (n + multiple - 1) // multiple * (multiple // LANES)
        out.append(flat2d[row:row + rows].reshape(-1)[:n].reshape(s))
        row += rows
    return out


SMALL_NAMES = ("b_ada", "g_norm", "b_fgate", "conv_w", "conv_b", "w_rgate", "b_rgate", "w_igate", "b_igate",
               "lru_lambda", "g_qk", "g_mix_out")
WEIGHT_NAMES = ("w_ada", "b_ada", "g_norm", "w_ffn_up", "w_ffn_down", "w_in", "b_fgate", "conv_w", "conv_b",
                "w_rgate", "b_rgate", "w_igate", "b_igate", "lru_lambda", "g_qk", "g_mix_out", "w_out")


def kernel(x, c, w_ada, b_ada, g_norm, w_ffn_up, w_ffn_down, w_in, b_fgate, conv_w, conv_b, w_rgate, b_rgate, w_igate, b_igate, lru_lambda, g_qk, g_mix_out, w_out, loss_target, m_w_ada, m_b_ada, m_g_norm, m_w_ffn_up, m_w_ffn_down, m_w_in, m_b_fgate, m_conv_w, m_conv_b, m_w_rgate, m_b_rgate, m_w_igate, m_b_igate, m_lru_lambda, m_g_qk, m_g_mix_out, m_w_out, v_w_ada, v_b_ada, v_g_norm, v_w_ffn_up, v_w_ffn_down, v_w_in, v_b_fgate, v_conv_w, v_conv_b, v_w_rgate, v_b_rgate, v_w_igate, v_b_igate, v_lru_lambda, v_g_qk, v_g_mix_out, v_w_out):
    batch, seq, d = x.shape
    n_layers = w_ada.shape[0]
    me = 4 * lax.axis_index("x") + 2 * lax.axis_index("y") + lax.axis_index("c")
    weights = dict(w_ada=w_ada, b_ada=b_ada, g_norm=g_norm, w_ffn_up=w_ffn_up, w_ffn_down=w_ffn_down, w_in=w_in,
                   b_fgate=b_fgate, conv_w=conv_w, conv_b=conv_b, w_rgate=w_rgate, b_rgate=b_rgate, w_igate=w_igate,
                   b_igate=b_igate, lru_lambda=lru_lambda, g_qk=g_qk, g_mix_out=g_mix_out, w_out=w_out)
    moments_m = dict(w_ada=m_w_ada, b_ada=m_b_ada, g_norm=m_g_norm, w_ffn_up=m_w_ffn_up, w_ffn_down=m_w_ffn_down,
                     w_in=m_w_in, b_fgate=m_b_fgate, conv_w=m_conv_w, conv_b=m_conv_b, w_rgate=m_w_rgate,
                     b_rgate=m_b_rgate, w_igate=m_w_igate, b_igate=m_b_igate, lru_lambda=m_lru_lambda, g_qk=m_g_qk,
                     g_mix_out=m_g_mix_out, w_out=m_w_out)
    moments_v = dict(w_ada=v_w_ada, b_ada=v_b_ada, g_norm=v_g_norm, w_ffn_up=v_w_ffn_up, w_ffn_down=v_w_ffn_down,
                     w_in=v_w_in, b_fgate=v_b_fgate, conv_w=v_conv_w, conv_b=v_conv_b, w_rgate=v_w_rgate,
                     b_rgate=v_b_rgate, w_igate=v_w_igate, b_igate=v_b_igate, lru_lambda=v_lru_lambda, g_qk=v_g_qk,
                     g_mix_out=v_g_mix_out, w_out=v_w_out)

    c_all, gn_all, cw_all = _exchange([(c, 0), (g_norm, 0), (conv_w, 0)], [], "gather_small_weights")
    c_all = c_all.reshape(N_DEV * batch, d)
    n_ada = w_ada.shape[-1]
    g_norm_full = gn_all.transpose(1, 2, 0, 3).reshape(n_layers, 3, d)
    conv_w_full = cw_all.transpose(1, 2, 0, 3).reshape(n_layers, 4, LRU_W)

    b_ada_loc = lax.dynamic_slice_in_dim(b_ada, me * n_ada, n_ada, axis=1)
    silu = lambda t: t * _sigmoid(t)

    def bias_epilogue(p, e_refs, o_refs):
        o_refs[0][...] = p + e_refs[0][...]

    mod_loc = []
    for l in range(n_layers):
        (ml,) = _mm(c_all, w_ada, mode="nn", tm=c_all.shape[0], tn=n_ada, tk=d, b_lead=(l,), a_pre=silu,
                    name=f"ada_{l}", extras=[(b_ada_loc[l][None, :], (1, n_ada), lambda i, j: (0, 0))],
                    outs=[((c_all.shape[0], n_ada), F32, (c_all.shape[0], n_ada), lambda i, j: (0, 0))],
                    epilogue=bias_epilogue)
        mod_loc.append(ml)
    (mod_all,) = _exchange([(jnp.stack(mod_loc), 0)], [], "gather_mod")
    mod_all = mod_all.transpose(1, 2, 0, 3).reshape(n_layers, N_DEV * batch, 9 * d)

    cast = lambda w, token: (w + token[0, 0]).astype(BF16)
    ffn_ops = lambda l, f, token: [(cast(w_ffn_up[l, f], token), 0, "gather"),
                                   (cast(w_ffn_down[l, f], token), 0, "gather")]
    mix_ops = lambda l, token: [(cast(jnp.pad(w_in[l], ((0, 0), (0, N_IN_PAD - N_IN))), token), 0, "gather"),
                                (cast(w_out[l], token), 0, "gather")]
    behind = lambda w, token: w + token[0, 0].astype(BF16)
    flights, landed_rest = {}, []

    def start(key, ops):
        flights[key], token = _flight_start(ops, f"weights_{key}_start")
        return token

    def wait(key, after):
        return _flight_wait(flights[key], after, f"weights_{key}_wait")

    def big_weights(l, part, after):
        if (l, part) == (0, "ffn0"):
            def wdown(after_up):
                (wd,), landed_down = wait("down", after_up)
                return behind(wd, start("mix", mix_ops(0, landed_down))).reshape(D_FF, d)

            return wup_first, wdown
        if (l, part) == (0, "mix"):
            (wi, wo), landed = wait("mix", after)
            rest = ffn_ops(0, 1, landed)
            for ll in range(1, n_layers):
                rest += ffn_ops(ll, 0, landed) + mix_ops(ll, landed) + ffn_ops(ll, 1, landed)
            return behind(wi, start("rest", rest)).reshape(d, N_IN_PAD), wo.reshape(d, d)
        if not landed_rest:
            landed_rest.extend(wait("rest", after)[0])
        at = 0 if l == 0 else 2 + 6 * (l - 1) + {"ffn0": 0, "mix": 2, "ffn1": 4}[part]
        first, second = landed_rest[at], landed_rest[at + 1]
        if part == "mix":
            return first.reshape(d, N_IN_PAD), second.reshape(d, d)
        return first, second.reshape(D_FF, d)

    w_up_first, mod_all = lax.optimization_barrier((w_ffn_up[0, 0].astype(BF16), mod_all))
    (wup_first,) = _exchange([(w_up_first, 0)], [], "gather_w_up_first", two_level=True)
    w_down_first, wup_first = lax.optimization_barrier((w_ffn_down[0, 0].astype(BF16), wup_first))
    token = start("down", [(w_down_first, 0, "gather")])
    mod_me = lax.dynamic_slice_in_dim(mod_all + token[0, 0], me * batch, batch, axis=1)
    mod_me = mod_me.reshape(n_layers, batch, 3, 3, d)

    wts = dict(g_norm=g_norm_full, conv_w=conv_w_full, conv_b=conv_b, w_rgate=w_rgate, b_rgate=b_rgate, w_igate=w_igate, b_igate=b_igate,
               lru_lambda=lru_lambda, g_qk=g_qk, g_mix_out=g_mix_out, b_fgate=b_fgate)
    loss_part, grad_x, handles, small = _local_step(x, loss_target, mod_me, wts, big_weights)

    dmod_me = small.pop("dmod").reshape(n_layers, batch, 9 * d)
    small["b_ada"] = jnp.sum(dmod_me, axis=1)
    small_shapes = [(1,)] + [weights[k].shape if k not in ("g_norm", "conv_w") else small[k].shape for k in SMALL_NAMES]
    small_flat = _flatten([loss_part.reshape(1)] + [small[k] for k in SMALL_NAMES], 16 * LANES)
    dmod_all, small_all = _exchange([(dmod_me, 0), (small_flat, 0)], [], "gather_small", two_level=True)
    landed = {key: _flight_wait(h, small_all, f"grads_{key[0]}_{key[1]}_wait")[0] for key, h in handles.items()}
    layer_range = range(n_layers)
    p_up = jnp.stack([jnp.stack([landed[(l, "ffn0")][0], landed[(l, "ffn1")][0]]) for l in layer_range])
    p_down = jnp.stack([jnp.stack([landed[(l, "ffn0")][1], landed[(l, "ffn1")][1]]) for l in layer_range])
    p_in = jnp.stack([landed[(l, "mix")][0] for l in layer_range])
    p_out = jnp.stack([landed[(l, "mix")][1] for l in layer_range])
    small_sum = _unflatten(_sum_parts(small_all), small_shapes, 16 * LANES)
    loss = small_sum[0].reshape(())
    small_grads = dict(zip(SMALL_NAMES, small_sum[1:]))
    small_grads["g_norm"] = lax.dynamic_slice_in_dim(small_grads["g_norm"], me * g_norm.shape[-1], g_norm.shape[-1], 2)
    small_grads["conv_w"] = lax.dynamic_slice_in_dim(small_grads["conv_w"], me * conv_w.shape[-1], conv_w.shape[-1], 2)

    dmod_all = dmod_all.transpose(1, 0, 2, 3).reshape(n_layers, N_DEV * batch, 9 * d)
    dmod_loc = lax.dynamic_slice_in_dim(dmod_all, me * n_ada, n_ada, axis=2)
    g_ada = []
    for l in range(n_layers):
        (gl,) = _mm(c_all, dmod_loc[l], mode="tn", tm=d, tn=n_ada, tk=c_all.shape[0], a_pre=silu, name=f"dw_ada_{l}",
                    outs=[((d, n_ada), F32, (d, n_ada), lambda i, j: (0, 0))], epilogue=_store_epilogue([F32]))
        g_ada.append(gl)
    g_ada = jnp.stack(g_ada)

    results = {}

    def update(name, parts):
        shape = weights[name].shape
        as3d = lambda t: t.reshape((-1,) + shape[-2:])
        outs = _adamw(parts.reshape((-1,) + parts.shape[-3:]), as3d(weights[name]), as3d(moments_m[name]),
                      as3d(moments_v[name]), f"adamw_{name}")
        results[name] = [o.reshape(shape) for o in outs]

    update("w_ada", g_ada[:, None])
    update("w_ffn_up", p_up)
    update("w_ffn_down", p_down)
    update("w_in", p_in[..., :N_IN])
    update("w_out", p_out)
    sm_shapes = [weights[k].shape for k in SMALL_NAMES]
    flat = lambda src: _flatten([src[k] for k in SMALL_NAMES], 16 * LANES)
    sm_out = _adamw(flat(small_grads)[None, None], flat(weights)[None], flat(moments_m)[None], flat(moments_v)[None],
                    "adamw_small")
    for k, vals in zip(SMALL_NAMES, zip(*[_unflatten(o, sm_shapes, 16 * LANES) for o in sm_out])):
        results[k] = list(vals)

    outs = [loss, grad_x]
    for n in range(4):
        outs += [results[k][n] for k in WEIGHT_NAMES]
    return tuple(outs)
```

```python
import functools
import math

import jax
import jax.numpy as jnp
from jax import lax
from jax.experimental import pallas as pl
from jax.experimental.pallas import tpu as pltpu

F32 = jnp.float32
BF16 = jnp.bfloat16

N_DEV = 8
D_MODEL = 1024
D_FF = 2816
FF_SHARD = 2 * D_FF // N_DEV
N_FF_SHARD = D_FF // FF_SHARD
HEAD_DIM = 64
LRU_W = 512
ATT_W = 256
N_IN = 2564
N_IN_PAD = 2688
LANES = 128
SUBLANES = 8
BLK = 256
TQ = 512
KB_PER_Q = TQ // BLK
EPS = 1e-6
LRU_C = 8.0
NEG_BIG = -1e30
VMEM_LIMIT_BYTES = 48 * 1024 * 1024

ADAM_LR, ADAM_B1, ADAM_B2, ADAM_EPS, ADAM_WD, ADAM_STEP = 0.001, 0.9, 0.999, 1e-08, 0.01, 10

COL_SBQ, COL_SBK, COL_SBV = 8, 10, 12
COL_FXV, COL_FXF = 18, 20

NN = (((1,), (0,)), ((), ()))
NT = (((1,), (1,)), ((), ()))
TN = (((0,), (0,)), ((), ()))


def _params(n_axes):
    return pltpu.CompilerParams(dimension_semantics=("arbitrary",) * n_axes, vmem_limit_bytes=VMEM_LIMIT_BYTES)


def _tok_tile(seq):
    for t in (512, 256, 128):
        if seq % t == 0:
            return t
    raise ValueError(f"sequence length {seq} is not a multiple of 128")


def _grad_tokens(m, tm):
    for mult in (4, 2):
        if m % (mult * tm) == 0:
            return mult * tm
    return tm


def _dot(a, b, dims=NN):
    return lax.dot_general(a, b, dims, preferred_element_type=F32)


def _sigmoid(x):
    return 1.0 / (1.0 + jnp.exp(-x))


def _softplus(x):
    return jnp.maximum(x, 0.0) + jnp.log(1.0 + jnp.exp(-jnp.abs(x)))


def _gelu_parts(x):
    k0, k1 = math.sqrt(2.0 / math.pi), 0.044715
    t = jnp.tanh(k0 * (x + k1 * x * x * x))
    gelu = 0.5 * x * (1.0 + t)
    dgelu = 0.5 * (1.0 + t) + 0.5 * x * (1.0 - t * t) * k0 * (1.0 + 3.0 * k1 * x * x)
    return gelu, dgelu


def _neg_expm1(x):
    series = -x * (1.0 + x * (0.5 + x * (1.0 / 6.0 + x * (1.0 / 24.0 + x * (1.0 / 120.0 + x * (1.0 / 720.0))))))
    return jnp.where(x > -0.25, series, 1.0 - jnp.exp(x))


def _split2(x):
    hi = x.astype(BF16)
    lo = (x - hi.astype(F32)).astype(BF16)
    return hi, lo


def _split3(x):
    hi = x.astype(BF16)
    r = x - hi.astype(F32)
    mid = r.astype(BF16)
    lo = (r - mid.astype(F32)).astype(BF16)
    return hi, mid, lo


def _rows_to_block(rows, width):
    r = lax.broadcasted_iota(jnp.int32, (SUBLANES, width), 0)
    out = jnp.zeros((SUBLANES, width), F32)
    for n, v in enumerate(rows):
        out = jnp.where(r == n, jnp.broadcast_to(v, (SUBLANES, width)), out)
    return out


def _colsum(x):
    return jnp.sum(x, axis=0, keepdims=True)


def _exchange(gathers, scatters, name, two_level=False):
    assert not (two_level and scatters)
    n_g = len(gathers)
    ops = [a for a, _ in gathers] + [a for a, _ in scatters]
    n = len(ops)
    out_shape = [jax.ShapeDtypeStruct(a.shape[:nl] + (N_DEV,) + a.shape[nl:], a.dtype) for a, nl in gathers]
    out_shape += [jax.ShapeDtypeStruct(a.shape, a.dtype) for a, _ in scatters]
    items = []
    for k, (a, nl) in enumerate(list(gathers) + list(scatters)):
        for flat in range(math.prod(a.shape[:nl])):
            idx, rem = [], flat
            for dim in reversed(a.shape[:nl]):
                idx.insert(0, rem % dim)
                rem //= dim
            items.append((k, tuple(idx)))
    n_items = len(items)

    def body(*refs):
        ins, outs = refs[:n], refs[n:2 * n]
        send_sems, recv_sems, local_sems = refs[2 * n:]
        x, y, c = lax.axis_index("x"), lax.axis_index("y"), lax.axis_index("c")
        me = 4 * x + 2 * y + c

        def at(ref, idx):
            return ref.at[idx] if idx else ref

        def src(it, peer):
            k, idx = items[it]
            return at(ins[k], idx) if k < n_g else at(ins[k], idx + (peer,))

        def slot(it, s):
            k, idx = items[it]
            return at(outs[k], idx + (s,))

        def remote(it, rel, source, s, to):
            return pltpu.make_async_remote_copy(
                src_ref=source, dst_ref=slot(it, s), send_sem=send_sems.at[it, rel], recv_sem=recv_sems.at[it, rel],
                device_id=to, device_id_type=pl.DeviceIdType.MESH)

        local = [pltpu.make_async_copy(src(it, me), slot(it, me), local_sems.at[it]) for it in range(n_items)]
        for cp in local:
            cp.start()

        if not two_level:
            started = []
            for r in range(1, N_DEV):
                px = 1 - x if (r >> 2) & 1 else x
                py = 1 - y if (r >> 1) & 1 else y
                pc = 1 - c if r & 1 else c
                for it in range(n_items):
                    cp = remote(it, r - 1, src(it, 4 * px + 2 * py + pc), me, (px, py, pc))
                    cp.start()
                    started.append(cp)
            for cp in started:
                cp.wait()
        else:
            sibling, chips = (x, y, 1 - c), [(1 - x, y), (x, 1 - y), (1 - x, 1 - y)]
            sib = 4 * x + 2 * y + (1 - c)
            started = []
            for it in range(n_items):
                started.append(remote(it, 0, src(it, me), me, sibling))
                started += [remote(it, 1 + j, src(it, me), me, (cx, cy, c)) for j, (cx, cy) in enumerate(chips)]
            for cp in started:
                cp.start()
            for j, (cx, cy) in enumerate(chips):
                s = 4 * cx + 2 * cy + c
                for it in range(n_items):
                    remote(it, 1 + j, slot(it, s), s, sibling).wait_recv()
                    cp = remote(it, 4 + j, slot(it, s), s, sibling)
                    cp.start()
                    started.append(cp)
            for it in range(n_items):
                remote(it, 0, slot(it, sib), sib, sibling).wait_recv()
                for j, (cx, cy) in enumerate(chips):
                    s = 4 * cx + 2 * cy + (1 - c)
                    remote(it, 4 + j, slot(it, s), s, sibling).wait_recv()
            for cp in started:
                cp.wait_send()
        for cp in local:
            cp.wait()

    hbm = pl.BlockSpec(memory_space=pltpu.HBM)
    return pl.pallas_call(
        body, name=name, out_shape=out_shape,
        in_specs=[hbm] * n, out_specs=[hbm] * n,
        scratch_shapes=[pltpu.SemaphoreType.DMA((n_items, N_DEV - 1)), pltpu.SemaphoreType.DMA((n_items, N_DEV - 1)),
                        pltpu.SemaphoreType.DMA((n_items,))],
    )(*ops)


def _lead_items(ops):
    items = []
    for k, (a, nl) in enumerate(ops):
        for flat in range(math.prod(a.shape[:nl])):
            idx, rem = [], flat
            for dim in reversed(a.shape[:nl]):
                idx.insert(0, rem % dim)
                rem //= dim
            items.append((k, tuple(idx)))
    return items


def _flight_copies(ops, srcs, lands, send_sems, recv_sems):
    x, y, c = lax.axis_index("x"), lax.axis_index("y"), lax.axis_index("c")
    me = 4 * x + 2 * y + c
    copies = []
    for r in range(1, N_DEV):
        px = 1 - x if (r >> 2) & 1 else x
        py = 1 - y if (r >> 1) & 1 else y
        pc = 1 - c if r & 1 else c
        for it, (k, idx) in enumerate(_lead_items([(a, nl) for a, nl, _ in ops])):
            src = srcs[k].at[idx + (4 * px + 2 * py + pc,)] if ops[k][2] == "scatter" else (
                srcs[k].at[idx] if idx else srcs[k])
            copies.append(pltpu.make_async_remote_copy(
                src_ref=src, dst_ref=lands[k].at[idx + (me,)],
                send_sem=send_sems.at[it * (N_DEV - 1) + r - 1], recv_sem=recv_sems.at[it * (N_DEV - 1) + r - 1],
                device_id=(px, py, pc), device_id_type=pl.DeviceIdType.MESH))
    return copies


def _flight_start(ops, name):
    n = len(ops)
    me = 4 * lax.axis_index("x") + 2 * lax.axis_index("y") + lax.axis_index("c")
    srcs, lands = [], []
    for a, nl, kind in ops:
        if kind == "scatter":
            own, shape = lax.dynamic_slice_in_dim(a, me, 1, axis=nl), a.shape
        else:
            own, shape = jnp.expand_dims(a, nl), a.shape[:nl] + (N_DEV,) + a.shape[nl:]
        start = (0,) * nl + (me,) + (0,) * (len(shape) - nl - 1)
        lands.append(pltpu.with_memory_space_constraint(
            lax.dynamic_update_slice(lax.empty(shape, a.dtype), own, start), pltpu.HBM))
        srcs.append(pltpu.with_memory_space_constraint(a, pltpu.HBM))

    def body(*refs):
        for cp in _flight_copies(ops, refs[:n], refs[n:2 * n], refs[2 * n], refs[2 * n + 1]):
            cp.start()
        refs[-1][...] = jnp.zeros_like(refs[-1])

    hbm, sem = pl.BlockSpec(memory_space=pltpu.HBM), pl.BlockSpec(memory_space=pltpu.SEMAPHORE)
    n_items = len(_lead_items([(a, nl) for a, nl, _ in ops]))
    sems = pltpu.SemaphoreType.DMA((n_items * (N_DEV - 1),))
    res = pl.pallas_call(
        body, name=name,
        out_shape=[sems, sems] + [pltpu.HBM(a.shape, a.dtype) for a in srcs + lands]
        + [jax.ShapeDtypeStruct((SUBLANES, LANES), F32)],
        in_specs=[hbm] * (2 * n), out_specs=[sem, sem] + [hbm] * (2 * n) + [pl.BlockSpec(memory_space=pltpu.VMEM)],
        input_output_aliases={i: 2 + i for i in range(2 * n)},
        compiler_params=pltpu.CompilerParams(has_side_effects=pltpu.SideEffectType.DATAFLOW_SIDE_EFFECTING),
    )(*srcs, *lands)
    return (ops, res[0], res[1], res[2:2 + n], res[2 + n:2 + 2 * n]), res[-1]


def _flight_wait(handle, after, name):
    ops, send_sems, recv_sems, srcs, lands = handle
    n = len(ops)

    def body(*refs):
        for cp in _flight_copies(ops, refs[:n], refs[n:2 * n], refs[2 * n], refs[2 * n + 1]):
            cp.wait_send()
            cp.wait_recv()
        refs[-1][...] = jnp.zeros_like(refs[-1])

    hbm, sem = pl.BlockSpec(memory_space=pltpu.HBM), pl.BlockSpec(memory_space=pltpu.SEMAPHORE)
    res = pl.pallas_call(
        body, name=name,
        out_shape=[pltpu.HBM(a.shape, a.dtype) for a in list(srcs) + list(lands)]
        + [jax.ShapeDtypeStruct((SUBLANES, LANES), F32)],
        in_specs=[hbm] * (2 * n) + [sem, sem, pl.BlockSpec(memory_space=pl.ANY)],
        out_specs=[hbm] * (2 * n) + [pl.BlockSpec(memory_space=pltpu.VMEM)],
        input_output_aliases={i: i for i in range(2 * n)},
        compiler_params=pltpu.CompilerParams(has_side_effects=pltpu.SideEffectType.DATAFLOW_SIDE_EFFECTING),
    )(*srcs, *lands, send_sems, recv_sems, after)
    return res[n:2 * n], res[-1]


def _mm(a, b, *, mode, tm, tn, tk, outs, epilogue, name, extras=(), a_lead=(), b_lead=(), a_pre=None,
        a_spec=None, b_spec=None, shape=None, ksub=1):
    if shape is not None:
        mdim, ndim, kdim = shape
    else:
        if mode == "tn":
            kdim, mdim = a.shape[-2:]
        else:
            mdim, kdim = a.shape[-2:]
        ndim = b.shape[-2] if mode == "nt" else b.shape[-1]
    assert mdim % tm == 0 and ndim % tn == 0 and kdim % tk == 0, (name, mdim, ndim, kdim, tm, tn, tk)
    ni, nj, nk = mdim // tm, ndim // tn, kdim // tk
    a_lead, b_lead = tuple(a_lead), tuple(b_lead)
    a_block = (None,) * len(a_lead) + ((tk, tm) if mode == "tn" else (tm, tk))
    b_block = (None,) * len(b_lead) + ((tn, tk) if mode == "nt" else (tk, tn))
    dims = {"nn": NN, "nt": NT, "tn": TN}[mode]
    ne, no = len(extras), len(outs)

    def a_index(i, j, k):
        return a_lead + ((k, i) if mode == "tn" else (i, k))

    def b_index(i, j, k):
        return b_lead + ((j, k) if mode == "nt" else (k, j))

    if a_spec is not None:
        a_block, a_index = a_spec
    if b_spec is not None:
        b_block, b_index = b_spec

    def body(*refs):
        a_ref, b_ref = refs[0], refs[1]
        e_refs, o_refs = refs[2:2 + ne], refs[2 + ne:2 + ne + no]
        if ksub == 1:
            av = a_ref[...] if a_pre is None else a_pre(a_ref[...])
            p = _dot(av.astype(BF16), b_ref[...].astype(BF16), dims)
        else:
            p = _dot(a_ref[0], b_ref[0], dims)
            for s in range(1, ksub):
                p = p + _dot(a_ref[s], b_ref[s], dims)
        if nk == 1:
            epilogue(p, e_refs, o_refs)
        else:
            acc = refs[-1]
            k = pl.program_id(2)

            @pl.when(k == 0)
            def _():
                acc[...] = p

            @pl.when(k > 0)
            def _():
                acc[...] += p

            @pl.when(k == nk - 1)
            def _():
                epilogue(acc[...], e_refs, o_refs)

    in_specs = [pl.BlockSpec(a_block, a_index), pl.BlockSpec(b_block, b_index)]
    in_specs += [pl.BlockSpec(blk, functools.partial(lambda i, j, k, f: f(i, j), f=f)) for _, blk, f in extras]
    out_specs = [pl.BlockSpec(blk, functools.partial(lambda i, j, k, f: f(i, j), f=f)) for _, _, blk, f in outs]
    res = pl.pallas_call(
        body, name=name, grid=(ni, nj, nk), in_specs=in_specs, out_specs=out_specs,
        out_shape=[jax.ShapeDtypeStruct(s, d) for s, d, _, _ in outs],
        scratch_shapes=[pltpu.VMEM((tm, tn), F32)] if nk > 1 else [],
        compiler_params=_params(3),
    )(a, b, *[e[0] for e in extras])
    return res


def _store_epilogue(dtypes):
    def epi(p, e_refs, o_refs):
        for o, dt in zip(o_refs, dtypes):
            o[...] = p.astype(dt)
    return epi


def _normmod(x, gn, scale, shift, seq, name):
    m, d = x.shape
    tm = _tok_tile(seq)
    tpb = seq // tm

    def body(x_ref, gn_ref, sc_ref, sh_ref, h_ref):
        xv = x_ref[...]
        rstd = lax.rsqrt(jnp.mean(xv * xv, axis=-1, keepdims=True) + EPS)
        h_ref[...] = (xv * rstd * gn_ref[...] * (1.0 + sc_ref[0]) + sh_ref[0]).astype(BF16)

    vec = pl.BlockSpec((1, 1, d), lambda i: (i // tpb, 0, 0))
    return pl.pallas_call(
        body, name=name, grid=(m // tm,),
        in_specs=[pl.BlockSpec((tm, d), lambda i: (i, 0)), pl.BlockSpec((1, d), lambda i: (0, 0)), vec, vec],
        out_specs=pl.BlockSpec((tm, d), lambda i: (i, 0)),
        out_shape=jax.ShapeDtypeStruct((m, d), BF16), compiler_params=_params(1),
    )(x, gn, scale, shift)


def _normmod_bwd_epilogue(p, e_refs, o_refs):
    x_ref, dxo_ref, gn_ref, sc_ref = e_refs
    xv = x_ref[...]
    rstd = lax.rsqrt(jnp.mean(xv * xv, axis=-1, keepdims=True) + EPS)
    xhat = xv * rstd
    gn, sc1 = gn_ref[...], 1.0 + sc_ref[0]
    dxhat = p * (gn * sc1)
    dx = rstd * (dxhat - xhat * jnp.mean(dxhat * xhat, axis=-1, keepdims=True))
    o_refs[0][...] = dxo_ref[...] + dx
    t = p * xhat
    o_refs[1][0] = _rows_to_block([_colsum(p), _colsum(t * gn), _colsum(t * sc1)], p.shape[1])


def _residual_bwd(dx, f, gate, fac, seq, name):
    m, d = dx.shape
    tm = _tok_tile(seq)
    tpb = seq // tm

    def body(dx_ref, f_ref, g_ref, df_ref, dg_ref):
        dxv = dx_ref[...]
        df_ref[...] = ((fac * (1.0 + g_ref[0])) * dxv).astype(BF16)
        dg_ref[0] = _rows_to_block([_colsum((fac * dxv) * f_ref[...].astype(F32))], d)

    tile = pl.BlockSpec((tm, d), lambda i: (i, 0))
    return pl.pallas_call(
        body, name=name, grid=(m // tm,),
        in_specs=[tile, tile, pl.BlockSpec((1, 1, d), lambda i: (i // tpb, 0, 0))],
        out_specs=[tile, pl.BlockSpec((1, SUBLANES, d), lambda i: (i, 0, 0))],
        out_shape=[jax.ShapeDtypeStruct((m, d), BF16), jax.ShapeDtypeStruct((m // tm, SUBLANES, d), F32)],
        compiler_params=_params(1),
    )(dx, f, gate)


def _loss_head(y, target, seq):
    m, d = y.shape
    tm = _tok_tile(seq)

    def body(y_ref, t_ref, dy_ref, l_ref):
        err = y_ref[...] - t_ref[...]
        dy_ref[...] = err * (1.0 / d)
        part = 0.5 * jnp.sum(jnp.mean(err * err, axis=-1, keepdims=True), axis=0, keepdims=True)
        l_ref[0] = jnp.broadcast_to(part, (SUBLANES, LANES))

    tile = pl.BlockSpec((tm, d), lambda i: (i, 0))
    return pl.pallas_call(
        body, name="loss_head", grid=(m // tm,), in_specs=[tile, tile],
        out_specs=[tile, pl.BlockSpec((1, SUBLANES, LANES), lambda i: (i, 0, 0))],
        out_shape=[jax.ShapeDtypeStruct((m, d), F32), jax.ShapeDtypeStruct((m // tm, SUBLANES, LANES), F32)],
        compiler_params=_params(1),
    )(y, target)


def _residual_io(x, gate, next_norm, tm, tpb):
    m, d = x.shape
    row = lambda i, j: (i, 0)
    vec = lambda i, j: (i // tpb, 0, 0)
    extras = [(x, (tm, d), row), (gate, (1, 1, d), vec)]
    outs = [((m, d), F32, (tm, d), row), ((m, d), BF16, (tm, d), row)]
    if next_norm:
        gn, scale, shift = next_norm
        extras += [(gn, (1, d), lambda i, j: (0, 0)), (scale, (1, 1, d), vec), (shift, (1, 1, d), vec)]
        outs.append(((m, d), BF16, (tm, d), row))
    return extras, outs


def _residual_epilogue(fac, p, e_refs, o_refs):
    x_out = e_refs[0][...] + (fac * (1.0 + e_refs[1][0])) * p
    o_refs[0][...] = x_out
    o_refs[1][...] = p.astype(BF16)
    if len(o_refs) > 2:
        rstd = lax.rsqrt(jnp.mean(x_out * x_out, axis=-1, keepdims=True) + EPS)
        o_refs[2][...] = (x_out * rstd * e_refs[2][...] * (1.0 + e_refs[3][0]) + e_refs[4][0]).astype(BF16)


def _ffn_fwd(x, h, wup, wdown, gate, seq, tag, next_norm=None):
    m, d = x.shape
    tm = _tok_tile(seq)
    tpb = seq // tm

    def up_body(h_ref, wg_ref, wu_ref, a_ref, gu_ref):
        hv = h_ref[...]
        g, u = _dot(hv, wg_ref[...]), _dot(hv, wu_ref[...])
        sg = _sigmoid(g)
        silu = g * sg
        a_ref[...] = (silu * u).astype(BF16)
        gu_ref[0] = (u * (sg * (1.0 + g * (1.0 - sg)))).astype(BF16)
        gu_ref[1] = silu.astype(BF16)

    wblk = (None, d, FF_SHARD)
    a, gu = pl.pallas_call(
        up_body, name=f"ffn_up_{tag}", grid=(N_FF_SHARD, m // tm),
        in_specs=[pl.BlockSpec((tm, d), lambda j, i: (i, 0)),
                  pl.BlockSpec(wblk, lambda j, i: (j, 0, 0)),
                  pl.BlockSpec(wblk, lambda j, i: (j + N_FF_SHARD, 0, 0))],
        out_specs=[pl.BlockSpec((None, tm, FF_SHARD), lambda j, i: (j, i, 0)),
                   pl.BlockSpec((2, None, tm, FF_SHARD), lambda j, i: (0, j, i, 0))],
        out_shape=[jax.ShapeDtypeStruct((N_FF_SHARD, m, FF_SHARD), BF16),
                   jax.ShapeDtypeStruct((2, N_FF_SHARD, m, FF_SHARD), BF16)],
        compiler_params=_params(2),
    )(h, wup, wup)

    if callable(wdown):
        wdown = wdown(a)
    wdown3 = wdown.reshape(N_FF_SHARD, FF_SHARD, d)
    extras, outs = _residual_io(x, gate, next_norm, tm, tpb)
    res = _mm(a, wdown3, mode="nn", tm=tm, tn=d, tk=D_FF, ksub=N_FF_SHARD, name=f"ffn_down_{tag}",
              shape=(m, d, D_FF), a_spec=((N_FF_SHARD, tm, FF_SHARD), lambda i, j, k: (0, i, 0)),
              b_spec=((N_FF_SHARD, FF_SHARD, d), lambda i, j, k: (0, 0, 0)),
              extras=extras, outs=outs, epilogue=functools.partial(_residual_epilogue, 0.5))
    return res[0], (a, gu, res[1]), wdown, (res[2] if next_norm else None)


def _ffn_bwd(dx_out, x, h, saved, wup, wdown, gn, scale, gate, seq, tag, on_grads):
    a, gu, f = saved
    m, d = x.shape
    tm = _tok_tile(seq)
    tpb = seq // tm
    df, dgate_parts = _residual_bwd(dx_out, f, gate, 0.5, seq, f"ffn_res_bwd_{tag}")

    def act_bwd_epilogue(p, e_refs, o_refs):
        o_refs[0][0] = (p * e_refs[0][0].astype(F32)).astype(BF16)
        o_refs[0][1] = (p * e_refs[0][1].astype(F32)).astype(BF16)

    gu_blk = (2, None, tm, FF_SHARD)
    (dgu,) = _mm(df, wdown, mode="nt", tm=tm, tn=FF_SHARD, tk=d, name=f"ffn_down_dx_{tag}", shape=(m, D_FF, d),
                 b_spec=((FF_SHARD, d), lambda i, j, k: (j, 0)),
                 extras=[(gu, gu_blk, lambda i, j: (0, j, i, 0))],
                 outs=[((2, N_FF_SHARD, m, FF_SHARD), BF16, gu_blk, lambda i, j: (0, j, i, 0))],
                 epilogue=act_bwd_epilogue)
    tt = _grad_tokens(m, tm)
    (dwdown,) = _mm(a, df, mode="tn", tm=FF_SHARD, tn=d, tk=tt, name=f"ffn_dwdown_{tag}", shape=(D_FF, d, m),
                    a_spec=((None, tt, FF_SHARD), lambda i, j, k: (i, k, 0)),
                    outs=[((D_FF, d), BF16, (FF_SHARD, d), lambda i, j: (i, 0))], epilogue=_store_epilogue([BF16]))
    dgu8 = dgu.reshape(2 * N_FF_SHARD, m, FF_SHARD)
    (dwup,) = _mm(h, dgu8, mode="tn", tm=d, tn=FF_SHARD, tk=tt, name=f"ffn_dwup_{tag}", shape=(d, 2 * D_FF, m),
                  b_spec=((None, tt, FF_SHARD), lambda i, j, k: (j, k, 0)),
                  outs=[((2 * N_FF_SHARD, d, FF_SHARD), BF16, (None, d, FF_SHARD), lambda i, j: (j, 0, 0))],
                  epilogue=_store_epilogue([BF16]))
    scale = scale + on_grads(dwup, dwdown)[0, 0]
    dx, nm_parts = _mm(dgu8, wup, mode="nt", tm=tm, tn=d, tk=D_FF, ksub=N_FF_SHARD, name=f"ffn_up_dx_{tag}",
                       shape=(m, d, 2 * D_FF), a_spec=((N_FF_SHARD, tm, FF_SHARD), lambda i, j, k: (k, i, 0)),
                       b_spec=((N_FF_SHARD, d, FF_SHARD), lambda i, j, k: (k, 0, 0)),
                       extras=[(x, (tm, d), lambda i, j: (i, 0)), (dx_out, (tm, d), lambda i, j: (i, 0)),
                               (gn, (1, d), lambda i, j: (0, 0)), (scale, (1, 1, d), lambda i, j: (i // tpb, 0, 0))],
                       outs=[((m, d), F32, (tm, d), lambda i, j: (i, 0)),
                             ((m // tm, SUBLANES, d), F32, (1, SUBLANES, d), lambda i, j: (i, 0, 0))],
                       epilogue=_normmod_bwd_epilogue)
    return dx, nm_parts, dgate_parts


def _shift_down(ext, n, rows):
    if n:
        ext = pltpu.roll(ext, n, 0)
    return ext[SUBLANES:SUBLANES + rows]


def _lru_gates(u, wr_ref, br_ref, wi_ref, bi_ref, lam_ref):
    ub = u.astype(BF16)
    r = _sigmoid(_dot(ub, wr_ref[...]) + br_ref[...])
    ig = _sigmoid(_dot(ub, wi_ref[...]) + bi_ref[...])
    sp = _softplus(-lam_ref[...])
    log_a = (-LRU_C * r) * sp
    a = jnp.exp(log_a)
    mult = jnp.sqrt(_neg_expm1(2.0 * log_a))
    return r, ig, sp, a, mult


def _conv(ext, cw_ref, cb_ref, rows):
    u = cb_ref[...] + cw_ref[3:4, :] * _shift_down(ext, 0, rows)
    for k in range(3):
        u = u + cw_ref[k:k + 1, :] * _shift_down(ext, 3 - k, rows)
    return u


def _lru_halo_spec(seq, ts):
    return pl.BlockSpec((SUBLANES, LRU_W),
                        lambda b, i: (jnp.maximum(b * (seq // SUBLANES) + i * (ts // SUBLANES) - 1, 0), 0))


def _lru_fwd(proj32, conv_w, conv_b, wr, br, wi, bi, lam, batch, seq):
    m = proj32.shape[0]
    ts = _tok_tile(seq)
    nt = seq // ts
    row = lambda b, i: (b * nt + i, 0)

    def body(x_ref, halo_ref, g_ref, cw_ref, cb_ref, wr_ref, br_ref, wi_ref, bi_ref, lam_ref,
             y_ref, h_ref, a_scr, b_scr, carry):
        i = pl.program_id(1)
        halo = jnp.where(i > 0, halo_ref[...], 0.0)
        ext = jnp.concatenate([halo, x_ref[...]], axis=0)
        u = _conv(ext, cw_ref, cb_ref, ts)
        _, ig, _, a, mult = _lru_gates(u, wr_ref, br_ref, wi_ref, bi_ref, lam_ref)
        a_scr[...] = a
        b_scr[...] = mult * (ig * u)

        @pl.when(i == 0)
        def _():
            carry[...] = jnp.zeros_like(carry)

        rid = lax.broadcasted_iota(jnp.int32, (SUBLANES, LRU_W), 0)

        def chunk(c, hprev):
            off = pl.multiple_of(c * SUBLANES, SUBLANES)
            av, bv = a_scr[pl.ds(off, SUBLANES), :], b_scr[pl.ds(off, SUBLANES), :]
            for d in (1, 2, 4):
                keep = rid >= d
                bv = jnp.where(keep, av * pltpu.roll(bv, d, 0) + bv, bv)
                av = jnp.where(keep, av * pltpu.roll(av, d, 0), av)
            h = av * hprev + bv
            h_ref[pl.ds(off, SUBLANES), :] = h
            return h[SUBLANES - 1:SUBLANES, :]

        carry[...] = lax.fori_loop(0, ts // SUBLANES, chunk, carry[...])
        gelu, _ = _gelu_parts(g_ref[...])
        y_ref[...] = h_ref[...] * gelu

    full = lambda shape: pl.BlockSpec(shape, lambda b, i: (0,) * len(shape))
    return pl.pallas_call(
        body, name="lru_fwd", grid=(batch, nt),
        in_specs=[pl.BlockSpec((ts, LRU_W), row), _lru_halo_spec(seq, ts),
                  pl.BlockSpec((ts, LRU_W), lambda b, i: (b * nt + i, 1)),
                  full((4, LRU_W)), full((1, LRU_W)), full((LRU_W, LRU_W)), full((1, LRU_W)),
                  full((LRU_W, LRU_W)), full((1, LRU_W)), full((1, LRU_W))],
        out_specs=[pl.BlockSpec((ts, LRU_W), row), pl.BlockSpec((ts, LRU_W), row)],
        out_shape=[jax.ShapeDtypeStruct((m, LRU_W), F32), jax.ShapeDtypeStruct((m, LRU_W), F32)],
        scratch_shapes=[pltpu.VMEM((ts, LRU_W), F32), pltpu.VMEM((ts, LRU_W), F32), pltpu.VMEM((1, LRU_W), F32)],
        compiler_params=_params(2),
    )(proj32, proj32, proj32, conv_w, conv_b, wr, br, wi, bi, lam)


def _lru_bwd(dy, proj32, h, conv_w, conv_b, wr, br, wi, bi, lam, batch, seq):
    m = proj32.shape[0]
    ts = _tok_tile(seq)
    nt = seq // ts
    row = lambda b, i: (b * nt + (nt - 1 - i), 0)
    halo = pl.BlockSpec((SUBLANES, LRU_W),
                        lambda b, i: (jnp.maximum(b * (seq // SUBLANES) + (nt - 1 - i) * (ts // SUBLANES) - 1, 0), 0))

    def body(dy_ref, x_ref, xhalo_ref, g_ref, h_ref, hhalo_ref, cw_ref, cb_ref, wr_ref, br_ref, wi_ref, bi_ref,
             lam_ref, dx_ref, dg_ref, dwr_ref, dwi_ref, sums_ref, a_scr, dh_scr, g_scr, carry, du_next):
        b, i = pl.program_id(0), pl.program_id(1)
        first_tile = i == nt - 1

        @pl.when((b == 0) & (i == 0))
        def _():
            dwr_ref[...] = jnp.zeros_like(dwr_ref)
            dwi_ref[...] = jnp.zeros_like(dwi_ref)
            sums_ref[...] = jnp.zeros_like(sums_ref)

        @pl.when(i == 0)
        def _():
            carry[...] = jnp.zeros_like(carry)
            du_next[...] = jnp.zeros_like(du_next)

        xhalo = jnp.where(first_tile, 0.0, xhalo_ref[...])
        ext = jnp.concatenate([xhalo, x_ref[...]], axis=0)
        u = _conv(ext, cw_ref, cb_ref, ts)
        r, ig, sp, a, mult = _lru_gates(u, wr_ref, br_ref, wi_ref, bi_ref, lam_ref)
        gelu, dgelu = _gelu_parts(g_ref[...])
        dyv, hv = dy_ref[...], h_ref[...]
        dg_ref[...] = (dyv * hv * dgelu).astype(BF16)
        a_scr[...] = a
        dh_scr[...] = dyv * gelu

        rid = lax.broadcasted_iota(jnp.int32, (SUBLANES, LRU_W), 0)
        nchunk = ts // SUBLANES

        def chunk(n, cg):
            off = pl.multiple_of((nchunk - 1 - n) * SUBLANES, SUBLANES)
            av, beta = a_scr[pl.ds(off, SUBLANES), :], dh_scr[pl.ds(off, SUBLANES), :]
            alpha = jnp.where(rid == SUBLANES - 1, 1.0, pltpu.roll(av, SUBLANES - 1, 0))
            for d in (1, 2, 4):
                keep = rid + d <= SUBLANES - 1
                beta = jnp.where(keep, beta + alpha * pltpu.roll(beta, SUBLANES - d, 0), beta)
                alpha = jnp.where(keep, alpha * pltpu.roll(alpha, SUBLANES - d, 0), alpha)
            gv = beta + alpha * cg
            g_scr[pl.ds(off, SUBLANES), :] = gv
            return av[0:1, :] * gv[0:1, :]

        carry[...] = lax.fori_loop(0, nchunk, chunk, carry[...])
        gv = g_scr[...]
        hhalo = jnp.where(first_tile, 0.0, hhalo_ref[...])
        hprev = _shift_down(jnp.concatenate([hhalo, hv], axis=0), 1, ts)
        dmult = gv * ig * u
        dig = gv * mult * u
        du = gv * mult * ig
        dlog_a = gv * hprev * a - dmult * a * a / mult
        dr = dlog_a * (-LRU_C * sp)
        dr_pre = dr * r * (1.0 - r)
        di_pre = dig * ig * (1.0 - ig)
        drb, dib, ub = dr_pre.astype(BF16), di_pre.astype(BF16), u.astype(BF16)
        du = du + _dot(drb, wr_ref[...], NT) + _dot(dib, wi_ref[...], NT)
        dwr_ref[...] += _dot(ub, drb, TN)
        dwi_ref[...] += _dot(ub, dib, TN)

        ext_du = jnp.concatenate([du, du_next[...]], axis=0)
        du_next[...] = du[0:SUBLANES, :]
        n_ext = ts + SUBLANES
        dx = cw_ref[3:4, :] * du
        sums = [_colsum(dr_pre), _colsum(di_pre), _colsum(dlog_a * (-LRU_C * r)), _colsum(du)]
        dcw = []
        for k in range(3):
            dx = dx + cw_ref[k:k + 1, :] * pltpu.roll(ext_du, n_ext - (3 - k), 0)[0:ts]
            dcw.append(_colsum(du * _shift_down(ext, 3 - k, ts)))
        dcw.append(_colsum(du * _shift_down(ext, 0, ts)))
        dx_ref[...] = dx.astype(BF16)
        sums_ref[...] += _rows_to_block(sums + dcw, LRU_W)

    full = lambda shape: pl.BlockSpec(shape, lambda b, i: (0,) * len(shape))
    tile = pl.BlockSpec((ts, LRU_W), row)
    return pl.pallas_call(
        body, name="lru_bwd", grid=(batch, nt),
        in_specs=[tile, tile, halo, pl.BlockSpec((ts, LRU_W), lambda b, i: (b * nt + (nt - 1 - i), 1)), tile, halo,
                  full((4, LRU_W)), full((1, LRU_W)), full((LRU_W, LRU_W)), full((1, LRU_W)),
                  full((LRU_W, LRU_W)), full((1, LRU_W)), full((1, LRU_W))],
        out_specs=[tile, tile, full((LRU_W, LRU_W)), full((LRU_W, LRU_W)), full((SUBLANES, LRU_W))],
        out_shape=[jax.ShapeDtypeStruct((m, LRU_W), BF16), jax.ShapeDtypeStruct((m, LRU_W), BF16),
                   jax.ShapeDtypeStruct((LRU_W, LRU_W), F32), jax.ShapeDtypeStruct((LRU_W, LRU_W), F32),
                   jax.ShapeDtypeStruct((SUBLANES, LRU_W), F32)],
        scratch_shapes=[pltpu.VMEM((ts, LRU_W), F32), pltpu.VMEM((ts, LRU_W), F32), pltpu.VMEM((ts, LRU_W), F32),
                        pltpu.VMEM((1, LRU_W), F32), pltpu.VMEM((SUBLANES, LRU_W), F32)],
        compiler_params=_params(2),
    )(dy, proj32, proj32, proj32, h, h, conv_w, conv_b, wr, br, wi, bi, lam)


def _head_masks():
    lane = lax.broadcasted_iota(jnp.int32, (1, LANES), 1)
    return lane < HEAD_DIM


def _stack_heads(x2):
    lo, zero = _head_masks(), jnp.zeros_like(x2)
    return jnp.concatenate([jnp.where(lo, x2, zero), jnp.where(lo, zero, x2)], axis=0)


def _unstack_heads(y):
    return jnp.where(_head_masks(), y[:TQ], y[TQ:])


def _stack_cols(a, b):
    return jnp.concatenate([a, b], axis=0)


def _causal(qi, kb, strict):
    r = jnp.bitwise_and(lax.broadcasted_iota(jnp.int32, (2 * TQ, BLK), 0), TQ - 1) + qi * TQ
    c = lax.broadcasted_iota(jnp.int32, (2 * TQ, BLK), 1) + kb * BLK
    return (c < r) if strict else (c <= r)


def _key_loop(qi, group, carry, descending=False):
    def trip(n, cr):
        done = [n * KB_PER_Q + j for j in range(KB_PER_Q)]
        return group([qi * KB_PER_Q - 1 - t for t in done] if descending else done, cr)

    return lax.fori_loop(0, qi, trip, carry)


def _one_by_one(block):
    def group(kbs, carry):
        for kb in kbs:
            carry = block(kb, carry, False)
        return carry
    return group


def _tri(cmp):
    r = lax.broadcasted_iota(jnp.int32, (BLK, BLK), 0)
    c = lax.broadcasted_iota(jnp.int32, (BLK, BLK), 1)
    return cmp(r, c)


def _dot_split(x, tri):
    hi, lo = _split2(x)
    return _dot(hi, tri) + _dot(lo, tri)


def _sb_fwd(proj16, batch, seq):
    nq = seq // TQ
    scale = HEAD_DIM ** -0.5

    def body(q_ref, k_ref, v_ref, y_ref, t_ref):
        qi = pl.program_id(2)
        qs = _stack_heads(q_ref[0])
        tri_after = _tri(lambda r, c: r > c).astype(BF16)

        def block(kb, carry, masked):
            acc, c = carry
            ks = pl.multiple_of(kb * BLK, BLK)
            k2, v2 = k_ref[0, pl.ds(ks, BLK), :], v_ref[0, pl.ds(ks, BLK), :]
            z = _dot(qs, k2, NT) * scale
            sp = _softplus(z)
            l = -sp
            if masked:
                valid = _causal(qi, kb, True)
                l = jnp.where(valid, l, 0.0)
            w = jnp.exp((z - sp) + _dot_split(l, tri_after) + c)
            if masked:
                w = jnp.where(valid, w, 0.0)
            return acc + _dot(w.astype(BF16), v2), c + jnp.sum(l, axis=1, keepdims=True)

        def group(kbs, carry):
            acc, c = carry
            kv = [(k_ref[0, pl.ds(pl.multiple_of(kb * BLK, BLK), BLK), :],
                   v_ref[0, pl.ds(pl.multiple_of(kb * BLK, BLK), BLK), :]) for kb in kbs]
            zs = [_dot(qs, k2, NT) * scale for k2, _ in kv]
            sps = [_softplus(z) for z in zs]
            afters = [_dot_split(-sp, tri_after) for sp in sps]
            for z, sp, after, (_, v2) in zip(zs, sps, afters, kv):
                acc = acc + _dot(jnp.exp((z - sp) + after + c).astype(BF16), v2)
                c = c - jnp.sum(sp, axis=1, keepdims=True)
            return acc, c

        carry = (jnp.zeros((2 * TQ, LANES), F32), jnp.zeros((2 * TQ, 1), F32))
        first = qi * KB_PER_Q
        for n in reversed(range(KB_PER_Q)):
            carry = block(first + n, carry, True)
        acc, c = _key_loop(qi, group, carry, descending=True)
        y_ref[...] = _unstack_heads(acc)
        t_ref[0] = _unstack_heads(jnp.broadcast_to(c, (2 * TQ, LANES)))

    m = batch * seq
    return pl.pallas_call(
        body, name="sb_fwd", grid=(batch, 2, nq),
        in_specs=[pl.BlockSpec((1, TQ, LANES), lambda b, p, q: (b, q, COL_SBQ + p)),
                  pl.BlockSpec((1, seq, LANES), lambda b, p, q: (b, 0, COL_SBK + p)),
                  pl.BlockSpec((1, seq, LANES), lambda b, p, q: (b, 0, COL_SBV + p))],
        out_specs=[pl.BlockSpec((TQ, LANES), lambda b, p, q: (b * nq + q, p)),
                   pl.BlockSpec((1, TQ, LANES), lambda b, p, q: (p, b * nq + q, 0))],
        out_shape=[jax.ShapeDtypeStruct((m, ATT_W), F32), jax.ShapeDtypeStruct((2, m, LANES), F32)],
        compiler_params=_params(3),
    )(proj16, proj16, proj16)


def _sb_bwd(dy, t, proj16, batch, seq):
    nq = seq // TQ
    scale = HEAD_DIM ** -0.5

    def body(dy_ref, t_ref, q_ref, k_ref, v_ref, dq_ref, dk_ref, dv_ref):
        qi = pl.program_id(2)

        @pl.when(qi == 0)
        def _():
            dk_ref[...] = jnp.zeros_like(dk_ref)
            dv_ref[...] = jnp.zeros_like(dv_ref)

        t2 = t_ref[0]
        qs, dys = _stack_heads(q_ref[0]), _stack_heads(dy_ref[...].astype(BF16))
        tot = _stack_cols(t2[:, 0:1], t2[:, HEAD_DIM:HEAD_DIM + 1])
        tri_incl = _tri(lambda r, c: r <= c).astype(BF16)
        tri_excl = _tri(lambda r, c: r < c).astype(BF16)

        def block(kb, carry, masked):
            dq, pc, ec = carry
            ks = pl.multiple_of(kb * BLK, BLK)
            k2, v2 = k_ref[0, pl.ds(ks, BLK), :], v_ref[0, pl.ds(ks, BLK), :]
            z = _dot(qs, k2, NT) * scale
            sp = _softplus(z)
            l, b = -sp, z - sp
            sig = jnp.exp(b)
            if masked:
                valid = _causal(qi, kb, True)
                l = jnp.where(valid, l, 0.0)
            after = tot - (pc + _dot_split(l, tri_incl))
            w = jnp.exp(b + after)
            if masked:
                w = jnp.where(valid, w, 0.0)
            e = _dot(dys, v2, NT) * w
            et = ec + _dot_split(e, tri_excl)
            dz = e * (1.0 - sig) - et * sig
            if masked:
                dz = jnp.where(valid, dz, 0.0)
            dzb = (dz * scale).astype(BF16)
            dk_ref[0, pl.ds(ks, BLK), :] += _dot(dzb, qs, TN)
            dv_ref[0, pl.ds(ks, BLK), :] += _dot(w.astype(BF16), dys, TN)
            return (dq + _dot(dzb, k2), pc + jnp.sum(l, axis=1, keepdims=True),
                    ec + jnp.sum(e, axis=1, keepdims=True))

        def group(kbs, carry):
            dq, pc, ec = carry
            starts = [pl.multiple_of(kb * BLK, BLK) for kb in kbs]
            kv = [(k_ref[0, pl.ds(ks, BLK), :], v_ref[0, pl.ds(ks, BLK), :]) for ks in starts]
            zs = [_dot(qs, k2, NT) * scale for k2, _ in kv]
            dws = [_dot(dys, v2, NT) for _, v2 in kv]
            sps = [_softplus(z) for z in zs]
            pins = [_dot_split(-sp, tri_incl) for sp in sps]
            es, ws, sigs = [], [], []
            for z, sp, pin, dw in zip(zs, sps, pins, dws):
                b = z - sp
                w = jnp.exp(b + (tot - (pc + pin)))
                pc = pc - jnp.sum(sp, axis=1, keepdims=True)
                es.append(dw * w)
                ws.append(w)
                sigs.append(jnp.exp(b))
            eins = [_dot_split(e, tri_excl) for e in es]
            for ks, (k2, _), e, w, sig, ein in zip(starts, kv, es, ws, sigs, eins):
                dzb = ((e * (1.0 - sig) - (ec + ein) * sig) * scale).astype(BF16)
                ec = ec + jnp.sum(e, axis=1, keepdims=True)
                dk_ref[0, pl.ds(ks, BLK), :] += _dot(dzb, qs, TN)
                dv_ref[0, pl.ds(ks, BLK), :] += _dot(w.astype(BF16), dys, TN)
                dq = dq + _dot(dzb, k2)
            return dq, pc, ec

        col = jnp.zeros((2 * TQ, 1), F32)
        first = qi * KB_PER_Q
        carry = _key_loop(qi, group, (jnp.zeros((2 * TQ, LANES), F32), col, col))
        for n in range(KB_PER_Q):
            carry = block(first + n, carry, True)
        dq_ref[...] = _unstack_heads(carry[0])

    m = batch * seq
    whole = lambda col: pl.BlockSpec((1, seq, LANES), lambda b, p, q: (b, 0, col + p))
    return pl.pallas_call(
        body, name="sb_bwd", grid=(batch, 2, nq),
        in_specs=[pl.BlockSpec((TQ, LANES), lambda b, p, q: (b * nq + q, p)),
                  pl.BlockSpec((1, TQ, LANES), lambda b, p, q: (p, b * nq + q, 0)),
                  pl.BlockSpec((1, TQ, LANES), lambda b, p, q: (b, q, COL_SBQ + p)),
                  whole(COL_SBK), whole(COL_SBV)],
        out_specs=[pl.BlockSpec((TQ, LANES), lambda b, p, q: (b * nq + q, p)), whole(0), whole(0)],
        out_shape=[jax.ShapeDtypeStruct((m, ATT_W), F32), jax.ShapeDtypeStruct((batch, seq, ATT_W), F32),
                   jax.ShapeDtypeStruct((batch, seq, ATT_W), F32)],
        compiler_params=_params(3),
    )(dy, t, proj16, proj16, proj16)


def _fox_pre(proj32, gq, gk, bf, group_mean, batch, seq):
    m = proj32.shape[0]
    ts = _tok_tile(seq)
    nt = seq // ts

    def body(q_ref, k_ref, f_ref, gq_ref, gk_ref, bf_ref, gm_ref, fq_ref, fk_ref, fc_ref, carry):
        i = pl.program_id(1)

        @pl.when(i == 0)
        def _():
            carry[...] = jnp.zeros_like(carry)

        gm = gm_ref[...]
        for src, g_ref, dst in ((q_ref, gq_ref, fq_ref), (k_ref, gk_ref, fk_ref)):
            v = src[...]
            ms = _dot_split(v * v, gm)
            dst[...] = (v * lax.rsqrt(ms + EPS) * g_ref[...]).astype(BF16)
        z = f_ref[...] + bf_ref[...]
        lf = jnp.minimum(z, 0.0) - jnp.log(1.0 + jnp.exp(-jnp.abs(z)))
        r = lax.broadcasted_iota(jnp.int32, (ts, ts), 0)
        c = lax.broadcasted_iota(jnp.int32, (ts, ts), 1)
        tri = (r >= c).astype(BF16)
        hi, mid, low = _split3(lf)
        fc = _dot(tri, hi) + _dot(tri, mid) + _dot(tri, low) + carry[...]
        fc_ref[...] = fc
        carry[...] = fc[ts - 1:ts, :]

    full = lambda shape: pl.BlockSpec(shape, lambda b, i: (0,) * len(shape))
    return pl.pallas_call(
        body, name="fox_pre", grid=(batch, nt),
        in_specs=[pl.BlockSpec((ts, ATT_W), lambda b, i: (b * nt + i, 7)),
                  pl.BlockSpec((ts, ATT_W), lambda b, i: (b * nt + i, 8)),
                  pl.BlockSpec((ts, LANES), lambda b, i: (b * nt + i, COL_FXF)),
                  full((1, ATT_W)), full((1, ATT_W)), full((1, LANES)), full((ATT_W, ATT_W))],
        out_specs=[pl.BlockSpec((ts, ATT_W), lambda b, i: (b * nt + i, 0)),
                   pl.BlockSpec((ts, ATT_W), lambda b, i: (b * nt + i, 0)),
                   pl.BlockSpec((ts, LANES), lambda b, i: (b * nt + i, 0))],
        out_shape=[jax.ShapeDtypeStruct((m, ATT_W), BF16), jax.ShapeDtypeStruct((m, ATT_W), BF16),
                   jax.ShapeDtypeStruct((m, LANES), F32)],
        scratch_shapes=[pltpu.VMEM((1, LANES), F32)],
        compiler_params=_params(2),
    )(proj32, proj32, proj32, gq, gk, bf, group_mean)


def _fox_specs(batch, seq):
    nq = seq // TQ
    return dict(
        qblk=pl.BlockSpec((1, TQ, LANES), lambda b, p, q: (b, q, p)),
        whole=pl.BlockSpec((1, seq, LANES), lambda b, p, q: (b, 0, p)),
        vwhole=pl.BlockSpec((1, seq, LANES), lambda b, p, q: (b, 0, COL_FXV + p)),
        fcol=pl.BlockSpec((1, 1, TQ, 2), lambda b, p, q: (b, p, q, 0)),
        frow=pl.BlockSpec((1, 1, 2, seq), lambda b, p, q: (b, p, 0, 0)),
        rows=pl.BlockSpec((TQ, LANES), lambda b, p, q: (b * nq + q, p)),
        stat=pl.BlockSpec((1, TQ, LANES), lambda b, p, q: (p, b * nq + q, 0)),
    )


def _fox_logits(qs, k2, fq_col, fr_ref, ks, is_a, scale):
    fk_row = jnp.where(is_a, fr_ref[0, 0, 0:1, pl.ds(ks, BLK)], fr_ref[0, 0, 1:2, pl.ds(ks, BLK)])
    return _dot(qs, k2, NT) * scale + fq_col - fk_row


def _fox_fwd(fq, fk, proj16, fcol, frow, batch, seq):
    nq = seq // TQ
    scale = HEAD_DIM ** -0.5

    def body(q_ref, k_ref, v_ref, fc_ref, fr_ref, y_ref, lse_ref):
        qi = pl.program_id(2)
        qs = _stack_heads(q_ref[0])
        fcv = fc_ref[0, 0]
        fq_col = _stack_cols(fcv[:, 0:1], fcv[:, 1:2])
        is_a = lax.broadcasted_iota(jnp.int32, (2 * TQ, 1), 0) < TQ

        def block(kb, carry, masked):
            acc, mx, den = carry
            ks = pl.multiple_of(kb * BLK, BLK)
            k2, v2 = k_ref[0, pl.ds(ks, BLK), :], v_ref[0, pl.ds(ks, BLK), :]
            s = _fox_logits(qs, k2, fq_col, fr_ref, ks, is_a, scale)
            if masked:
                s = jnp.where(_causal(qi, kb, False), s, NEG_BIG)
            mx_new = jnp.maximum(mx, jnp.max(s, axis=1, keepdims=True))
            p = jnp.exp(s - mx_new)
            alpha = jnp.exp(mx - mx_new)
            return (alpha * acc + _dot(p.astype(BF16), v2), mx_new, alpha * den + jnp.sum(p, axis=1, keepdims=True))

        first = qi * KB_PER_Q
        carry = (jnp.zeros((2 * TQ, LANES), F32), jnp.full((2 * TQ, 1), NEG_BIG, F32), jnp.zeros((2 * TQ, 1), F32))
        carry = _key_loop(qi, _one_by_one(block), carry)
        for n in range(KB_PER_Q):
            carry = block(first + n, carry, True)
        acc, mx, den = carry
        y_ref[...] = _unstack_heads(acc / den)
        lse_ref[0] = _unstack_heads(jnp.broadcast_to(mx + jnp.log(den), (2 * TQ, LANES)))

    m = batch * seq
    sp = _fox_specs(batch, seq)
    return pl.pallas_call(
        body, name="fox_fwd", grid=(batch, 2, nq),
        in_specs=[sp["qblk"], sp["whole"], sp["vwhole"], sp["fcol"], sp["frow"]],
        out_specs=[sp["rows"], sp["stat"]],
        out_shape=[jax.ShapeDtypeStruct((m, ATT_W), F32), jax.ShapeDtypeStruct((2, m, LANES), F32)],
        compiler_params=_params(3),
    )(fq, fk, proj16, fcol, frow)


def _fox_bwd(dy, y, lse, fq, fk, proj16, fcol, frow, batch, seq):
    nq = seq // TQ
    scale = HEAD_DIM ** -0.5

    def body(dy_ref, y_ref, lse_ref, q_ref, k_ref, v_ref, fc_ref, fr_ref, dq_ref, dk_ref, dv_ref, dfr_ref, dfc_ref):
        qi = pl.program_id(2)

        @pl.when(qi == 0)
        def _():
            dk_ref[...] = jnp.zeros_like(dk_ref)
            dv_ref[...] = jnp.zeros_like(dv_ref)
            dfr_ref[...] = jnp.zeros_like(dfr_ref)

        lo = _head_masks()
        lane = lax.broadcasted_iota(jnp.int32, (1, LANES), 1)
        dy2, lse2, fcv = dy_ref[...], lse_ref[0], fc_ref[0, 0]
        qs, dys = _stack_heads(q_ref[0]), _stack_heads(dy2.astype(BF16))
        dyy = dy2 * y_ref[...]
        delta = _stack_cols(jnp.sum(jnp.where(lo, dyy, 0.0), axis=1, keepdims=True),
                            jnp.sum(jnp.where(lo, 0.0, dyy), axis=1, keepdims=True))
        lse_col = _stack_cols(lse2[:, 0:1], lse2[:, HEAD_DIM:HEAD_DIM + 1])
        fq_col = _stack_cols(fcv[:, 0:1], fcv[:, 1:2])
        is_a = lax.broadcasted_iota(jnp.int32, (2 * TQ, 1), 0) < TQ

        def block(kb, carry, masked):
            dq, rs = carry
            ks = pl.multiple_of(kb * BLK, BLK)
            k2, v2 = k_ref[0, pl.ds(ks, BLK), :], v_ref[0, pl.ds(ks, BLK), :]
            p = jnp.exp(_fox_logits(qs, k2, fq_col, fr_ref, ks, is_a, scale) - lse_col)
            if masked:
                p = jnp.where(_causal(qi, kb, False), p, 0.0)
            ds = p * (_dot(dys, v2, NT) - delta)
            dsb = (ds * scale).astype(BF16)
            dk_ref[0, pl.ds(ks, BLK), :] += _dot(dsb, qs, TN)
            dv_ref[0, pl.ds(ks, BLK), :] += _dot(p.astype(BF16), dys, TN)
            dfr_ref[0, 0, 0:1, pl.ds(ks, BLK)] -= jnp.sum(ds[:TQ], axis=0, keepdims=True)
            dfr_ref[0, 0, 1:2, pl.ds(ks, BLK)] -= jnp.sum(ds[TQ:], axis=0, keepdims=True)
            return dq + _dot(dsb, k2), rs + jnp.sum(ds, axis=1, keepdims=True)

        def group(kbs, carry):
            dq, rs = carry
            starts = [pl.multiple_of(kb * BLK, BLK) for kb in kbs]
            kv = [(k_ref[0, pl.ds(ks, BLK), :], v_ref[0, pl.ds(ks, BLK), :]) for ks in starts]
            ss = [_fox_logits(qs, k2, fq_col, fr_ref, ks, is_a, scale) for ks, (k2, _) in zip(starts, kv)]
            dps = [_dot(dys, v2, NT) for _, v2 in kv]
            ps = [jnp.exp(s - lse_col) for s in ss]
            dss = [p * (dp - delta) for p, dp in zip(ps, dps)]
            for ks, (k2, _), p, ds in zip(starts, kv, ps, dss):
                dsb = (ds * scale).astype(BF16)
                dk_ref[0, pl.ds(ks, BLK), :] += _dot(dsb, qs, TN)
                dv_ref[0, pl.ds(ks, BLK), :] += _dot(p.astype(BF16), dys, TN)
                dfr_ref[0, 0, 0:1, pl.ds(ks, BLK)] -= jnp.sum(ds[:TQ], axis=0, keepdims=True)
                dfr_ref[0, 0, 1:2, pl.ds(ks, BLK)] -= jnp.sum(ds[TQ:], axis=0, keepdims=True)
                dq = dq + _dot(dsb, k2)
            return dq, rs + jnp.sum(functools.reduce(jnp.add, dss), axis=1, keepdims=True)

        first = qi * KB_PER_Q
        carry = _key_loop(qi, group, (jnp.zeros((2 * TQ, LANES), F32), jnp.zeros((2 * TQ, 1), F32)))
        for n in range(KB_PER_Q):
            carry = block(first + n, carry, True)
        dq, rs = carry
        dq_ref[...] = _unstack_heads(dq)
        dfc_ref[0] = jnp.where(lane == 0, rs[:TQ], jnp.where(lane == 1, rs[TQ:], 0.0))

    m = batch * seq
    sp = _fox_specs(batch, seq)
    return pl.pallas_call(
        body, name="fox_bwd", grid=(batch, 2, nq),
        in_specs=[sp["rows"], sp["rows"], sp["stat"], sp["qblk"], sp["whole"], sp["vwhole"], sp["fcol"], sp["frow"]],
        out_specs=[sp["rows"], sp["whole"], sp["whole"],
                   pl.BlockSpec((1, 1, SUBLANES, seq), lambda b, p, q: (b, p, 0, 0)), sp["stat"]],
        out_shape=[jax.ShapeDtypeStruct((m, ATT_W), F32), jax.ShapeDtypeStruct((batch, seq, ATT_W), F32),
                   jax.ShapeDtypeStruct((batch, seq, ATT_W), F32),
                   jax.ShapeDtypeStruct((batch, 2, SUBLANES, seq), F32), jax.ShapeDtypeStruct((2, m, LANES), F32)],
        compiler_params=_params(3),
    )(dy, y, lse, fq, fk, proj16, fcol, frow)


def _fox_post_bwd(dfq, dfk, dfc, proj32, gq, gk, bf, group_mean, batch, seq):
    m = proj32.shape[0]
    ts = _tok_tile(seq)
    nt = seq // ts
    tile = lambda w, col: pl.BlockSpec((ts, w), lambda b, i: (b * nt + (nt - 1 - i), col))

    def body(dfq_ref, dfk_ref, dfc_ref, q_ref, k_ref, f_ref, gq_ref, gk_ref, bf_ref, gm_ref,
             dq_ref, dk_ref, df_ref, gs_ref, bs_ref, carry):
        i = pl.program_id(1)

        @pl.when(i == 0)
        def _():
            carry[...] = jnp.zeros_like(carry)

        gm = gm_ref[...]
        rows = []
        for src, g_ref, d_ref, dst in ((q_ref, gq_ref, dfq_ref, dq_ref), (k_ref, gk_ref, dfk_ref, dk_ref)):
            v, dv = src[...], d_ref[...]
            rstd = lax.rsqrt(_dot_split(v * v, gm) + EPS)
            vhat = v * rstd
            rows.append(_colsum(dv * vhat))
            dvh = dv * g_ref[...]
            dst[...] = (rstd * (dvh - vhat * _dot_split(dvh * vhat, gm))).astype(BF16)
        gs_ref[0] = _rows_to_block(rows, ATT_W)

        dfc_v = dfc_ref[...]
        r = lax.broadcasted_iota(jnp.int32, (ts, ts), 0)
        c = lax.broadcasted_iota(jnp.int32, (ts, ts), 1)
        tri = (r <= c).astype(BF16)
        hi, mid, low = _split3(dfc_v)
        dlf = _dot(tri, hi) + _dot(tri, mid) + _dot(tri, low) + carry[...]
        carry[...] = dlf[0:1, :]
        z = f_ref[...] + bf_ref[...]
        dz = dlf * _sigmoid(-z)
        df_ref[...] = dz.astype(BF16)
        bs_ref[0] = _rows_to_block([_colsum(dz)], LANES)

    full = lambda shape: pl.BlockSpec(shape, lambda b, i: (0,) * len(shape))
    part = lambda w: pl.BlockSpec((1, SUBLANES, w), lambda b, i: (b * nt + (nt - 1 - i), 0, 0))
    return pl.pallas_call(
        body, name="fox_post_bwd", grid=(batch, nt),
        in_specs=[tile(ATT_W, 0), tile(ATT_W, 0), tile(LANES, 0), tile(ATT_W, 7), tile(ATT_W, 8), tile(LANES, COL_FXF),
                  full((1, ATT_W)), full((1, ATT_W)), full((1, LANES)), full((ATT_W, ATT_W))],
        out_specs=[tile(ATT_W, 0), tile(ATT_W, 0), tile(LANES, 0), part(ATT_W), part(LANES)],
        out_shape=[jax.ShapeDtypeStruct((m, ATT_W), BF16), jax.ShapeDtypeStruct((m, ATT_W), BF16),
                   jax.ShapeDtypeStruct((m, LANES), BF16),
                   jax.ShapeDtypeStruct((batch * nt, SUBLANES, ATT_W), F32),
                   jax.ShapeDtypeStruct((batch * nt, SUBLANES, LANES), F32)],
        scratch_shapes=[pltpu.VMEM((1, LANES), F32)],
        compiler_params=_params(2),
    )(dfq, dfk, dfc, proj32, proj32, proj32, gq, gk, bf, group_mean)


_GROUPS = ((0, LRU_W), (LRU_W, LRU_W + ATT_W), (LRU_W + ATT_W, LRU_W + 2 * ATT_W))


def _outnorm(y_lru, y_sb, y_fox, gmix, seq):
    m = y_lru.shape[0]
    tm = _tok_tile(seq)

    def body(a_ref, b_ref, c_ref, g_ref, o_ref):
        parts = []
        for ref in (a_ref, b_ref, c_ref):
            v = ref[...]
            parts.append(v * lax.rsqrt(jnp.mean(v * v, axis=-1, keepdims=True) + EPS))
        o_ref[...] = (jnp.concatenate(parts, axis=1) * g_ref[...]).astype(BF16)

    t = lambda w: pl.BlockSpec((tm, w), lambda i: (i, 0))
    return pl.pallas_call(
        body, name="outnorm", grid=(m // tm,),
        in_specs=[t(LRU_W), t(ATT_W), t(ATT_W), pl.BlockSpec((1, D_MODEL), lambda i: (0, 0))],
        out_specs=t(D_MODEL), out_shape=jax.ShapeDtypeStruct((m, D_MODEL), BF16), compiler_params=_params(1),
    )(y_lru, y_sb, y_fox, gmix)


def _outnorm_bwd_epilogue(p, e_refs, o_refs):
    gmix = e_refs[3][...]
    dg = []
    for n, (lo, hi) in enumerate(_GROUPS):
        v, dyn = e_refs[n][...], p[:, lo:hi]
        rstd = lax.rsqrt(jnp.mean(v * v, axis=-1, keepdims=True) + EPS)
        vhat = v * rstd
        dg.append(_colsum(dyn * vhat))
        dvh = dyn * gmix[:, lo:hi]
        o_refs[n][...] = rstd * (dvh - vhat * jnp.mean(dvh * vhat, axis=-1, keepdims=True))
    o_refs[3][0] = _rows_to_block([jnp.concatenate(dg, axis=1)], p.shape[1])


def _pair_layouts(fcum, batch, seq):
    f4 = fcum[:, :4].reshape(batch, seq, 2, 2)
    return f4.transpose(0, 2, 1, 3), f4.transpose(0, 2, 3, 1)


def _gate_grad_cols(dfr, dfc, batch, seq):
    keys = dfr[:, :, :2, :].transpose(0, 3, 1, 2).reshape(batch * seq, 4)
    queries = dfc[:, :, :2].transpose(1, 0, 2).reshape(batch * seq, 4)
    return jnp.pad(keys + queries, ((0, 0), (0, LANES - 4)))


def _mixer_fwd(x, h, w, gate, batch, seq, next_norm=None):
    m, d = x.shape
    tm = _tok_tile(seq)
    tpb = seq // tm

    def in_epilogue(p, e_refs, o_refs):
        o_refs[0][...] = p
        o_refs[1][...] = p.astype(BF16)

    tn_in = 896
    proj32, proj16 = _mm(h, w["w_in"], mode="nn", tm=tm, tn=tn_in, tk=d, name="mix_in",
                         outs=[((m, N_IN_PAD), F32, (tm, tn_in), lambda i, j: (i, j)),
                               ((m, N_IN_PAD), BF16, (tm, tn_in), lambda i, j: (i, j))],
                         epilogue=in_epilogue)
    y_lru, h_lru = _lru_fwd(proj32, w["conv_w"], w["conv_b"], w["wr"], w["br"], w["wi"], w["bi"], w["lam"], batch, seq)
    p16 = proj16.reshape(batch, seq, N_IN_PAD)
    y_sb, t_sb = _sb_fwd(p16, batch, seq)
    fq, fk, fcum = _fox_pre(proj32, w["gq"], w["gk"], w["bf"], w["group_mean"], batch, seq)
    fcol, frow = _pair_layouts(fcum, batch, seq)
    fq3, fk3 = fq.reshape(batch, seq, ATT_W), fk.reshape(batch, seq, ATT_W)
    y_fox, lse = _fox_fwd(fq3, fk3, p16, fcol, frow, batch, seq)
    ynorm = _outnorm(y_lru, y_sb, y_fox, w["gmix"], seq)

    extras, outs = _residual_io(x, gate, next_norm, tm, tpb)
    res = _mm(ynorm, w["w_out"], mode="nn", tm=tm, tn=d, tk=d, name="mix_out", extras=extras, outs=outs,
              epilogue=functools.partial(_residual_epilogue, 1.0))
    saved = dict(proj32=proj32, p16=p16, h_lru=h_lru, y_lru=y_lru, y_sb=y_sb, t_sb=t_sb, fq3=fq3, fk3=fk3,
                 fcol=fcol, frow=frow, y_fox=y_fox, lse=lse, ynorm=ynorm, out=res[1])
    return res[0], saved, (res[2] if next_norm else None)


def _mixer_bwd(dx_out, x, h, s, w, gn, scale, gate, batch, seq, on_grads):
    m, d = x.shape
    tm = _tok_tile(seq)
    tpb = seq // tm
    dout, dgate_parts = _residual_bwd(dx_out, s["out"], gate, 1.0, seq, "mix_res_bwd")
    (dw_out,) = _mm(s["ynorm"], dout, mode="tn", tm=d, tn=d, tk=_grad_tokens(m, tm), name="mix_dwout",
                    outs=[((d, d), BF16, (d, d), lambda i, j: (i, j))], epilogue=_store_epilogue([BF16]))
    dy_lru, dy_sb, dy_fox, gmix_parts = _mm(
        dout, w["w_out"], mode="nt", tm=tm, tn=d, tk=d, name="mix_out_dx",
        extras=[(s["y_lru"], (tm, LRU_W), lambda i, j: (i, 0)), (s["y_sb"], (tm, ATT_W), lambda i, j: (i, 0)),
                (s["y_fox"], (tm, ATT_W), lambda i, j: (i, 0)), (w["gmix"], (1, d), lambda i, j: (0, 0))],
        outs=[((m, LRU_W), F32, (tm, LRU_W), lambda i, j: (i, 0)), ((m, ATT_W), F32, (tm, ATT_W), lambda i, j: (i, 0)),
              ((m, ATT_W), F32, (tm, ATT_W), lambda i, j: (i, 0)),
              ((m // tm, SUBLANES, d), F32, (1, SUBLANES, d), lambda i, j: (i, 0, 0))],
        epilogue=_outnorm_bwd_epilogue)

    dsq, dsk, dsv = _sb_bwd(dy_sb, s["t_sb"], s["p16"], batch, seq)
    dfq, dfk, dfv, dfr, dfc = _fox_bwd(dy_fox, s["y_fox"], s["lse"], s["fq3"], s["fk3"], s["p16"], s["fcol"],
                                       s["frow"], batch, seq)
    dfc_cols = _gate_grad_cols(dfr, dfc, batch, seq)
    dxq, dxk, dxf, gqk_parts, bf_parts = _fox_post_bwd(dfq, dfk.reshape(m, ATT_W), dfc_cols, s["proj32"],
                                                       w["gq"], w["gk"], w["bf"], w["group_mean"], batch, seq)
    dlx, dlg, dwr, dwi, lru_sums = _lru_bwd(dy_lru, s["proj32"], s["h_lru"], w["conv_w"], w["conv_b"], w["wr"],
                                            w["br"], w["wi"], w["bi"], w["lam"], batch, seq)
    dproj = jnp.concatenate([dlx, dlg, dsq.astype(BF16), dsk.reshape(m, ATT_W).astype(BF16),
                             dsv.reshape(m, ATT_W).astype(BF16), dxq, dxk, dfv.reshape(m, ATT_W).astype(BF16), dxf],
                            axis=1)
    tn_in = 896
    (dw_in,) = _mm(h, dproj, mode="tn", tm=d, tn=tn_in, tk=_grad_tokens(m, tm), name="mix_dwin",
                   outs=[((d, N_IN_PAD), BF16, (d, tn_in), lambda i, j: (i, j))], epilogue=_store_epilogue([BF16]))
    scale = scale + on_grads(dw_in, dw_out)[0, 0]
    dx, nm_parts = _mm(dproj, w["w_in"], mode="nt", tm=tm, tn=d, tk=tn_in, name="mix_in_dx",
                       extras=[(x, (tm, d), lambda i, j: (i, 0)), (dx_out, (tm, d), lambda i, j: (i, 0)),
                               (gn, (1, d), lambda i, j: (0, 0)), (scale, (1, 1, d), lambda i, j: (i // tpb, 0, 0))],
                       outs=[((m, d), F32, (tm, d), lambda i, j: (i, 0)),
                             ((m // tm, SUBLANES, d), F32, (1, SUBLANES, d), lambda i, j: (i, 0, 0))],
                       epilogue=_normmod_bwd_epilogue)
    grads = dict(dwr=dwr, dwi=dwi, lru_sums=lru_sums, gmix_parts=gmix_parts,
                 gqk_parts=gqk_parts, bf_parts=bf_parts)
    return dx, grads, nm_parts, dgate_parts


def _block_diag(w):
    nb = w.shape[0]
    eye = jnp.eye(nb, dtype=w.dtype)
    return (eye[:, None, :, None] * w[:, :, None, :]).reshape(nb * HEAD_DIM, nb * HEAD_DIM)


def _block_diag_grad(g):
    nb = LRU_W // HEAD_DIM
    g4 = g.reshape(nb, HEAD_DIM, nb, HEAD_DIM)
    return jnp.stack([g4[n, :, n, :] for n in range(nb)])


def _per_batch(parts, batch, row):
    r = parts[:, row, :]
    return r.reshape(batch, -1, r.shape[-1]).sum(axis=1)


def _local_step(x3, target3, mod, wts, big_weights):
    batch, seq, d = x3.shape
    assert seq % TQ == 0, seq
    m = batch * seq
    n_layers = mod.shape[0]
    x = x3.reshape(m, d)
    group_mean = _block_diag(jnp.full((ATT_W // HEAD_DIM, HEAD_DIM, HEAD_DIM), 1.0 / HEAD_DIM, BF16))
    vec = lambda l, j, t: mod[l, :, j, t][:, None, :]

    layers, saved = [], []
    for l in range(n_layers):
        gq = jnp.tile(wts["g_qk"][l, 0], ATT_W // HEAD_DIM)[None, :]
        gk = jnp.tile(wts["g_qk"][l, 1], ATT_W // HEAD_DIM)[None, :]
        bf = jnp.pad(wts["b_fgate"][l], (0, LANES - 4))[None, :]
        lw = dict(conv_w=wts["conv_w"][l],
                  conv_b=wts["conv_b"][l][None, :], wr=_block_diag(wts["w_rgate"][l]).astype(BF16),
                  br=wts["b_rgate"][l][None, :], wi=_block_diag(wts["w_igate"][l]).astype(BF16),
                  bi=wts["b_igate"][l][None, :], lam=wts["lru_lambda"][l][None, :], gq=gq, gk=gk, bf=bf,
                  group_mean=group_mean, gmix=wts["g_mix_out"][l][None, :])
        layers.append(lw)
        gn = lambda j: wts["g_norm"][l, j][None, :]
        norm_of = lambda ll, j: (wts["g_norm"][ll, j][None, :], vec(ll, j, 1), vec(ll, j, 0))
        sv = dict(x0=x)
        sv["h0"] = h_next if l else _normmod(x, *norm_of(0, 0), seq, "normmod_first")
        wup, wdown = big_weights(l, "ffn0", sv["h0"])
        x, sv["ffn0"], wdown, sv["h1"] = _ffn_fwd(x, sv["h0"], wup, wdown, vec(l, 0, 2), seq, f"{l}_0", norm_of(l, 1))
        sv["w_ffn0"] = (wup, wdown)
        sv["x1"] = x
        lw["w_in"], lw["w_out"] = big_weights(l, "mix", sv["h1"])
        x, sv["mix"], sv["h2"] = _mixer_fwd(x, sv["h1"], lw, vec(l, 1, 2), batch, seq, norm_of(l, 2))
        sv["x2"] = x
        wup, wdown = big_weights(l, "ffn1", sv["h2"])
        x, sv["ffn1"], wdown, h_next = _ffn_fwd(x, sv["h2"], wup, wdown, vec(l, 2, 2), seq, f"{l}_1",
                                                norm_of(l + 1, 0) if l + 1 < n_layers else None)
        sv["w_ffn1"] = (wup, wdown)
        saved.append(sv)

    dx, loss_parts = _loss_head(x, target3.reshape(m, d), seq)
    loss = jnp.sum(loss_parts[:, 0, 0])

    handles = {}

    def scatter(key, shapes):
        def on_grads(*grads):
            ops = [(g.reshape(shape), 0, "scatter") for g, shape in zip(grads, shapes)]
            handles[key], token = _flight_start(ops, f"grads_{key[0]}_{key[1]}_start")
            return token
        return on_grads

    ffn_shapes = ((2 * N_FF_SHARD, d, FF_SHARD), (N_DEV, D_FF // N_DEV, d))
    mix_shapes = ((N_DEV, d // N_DEV, N_IN_PAD), (N_DEV, d // N_DEV, d))
    small = {k: [] for k in ("dmod", "g_norm", "b_fgate", "conv_w", "conv_b", "w_rgate", "b_rgate", "w_igate",
                             "b_igate", "lru_lambda", "g_qk", "g_mix_out")}
    for l in reversed(range(n_layers)):
        sv, lw = saved[l], layers[l]
        gn = lambda j: wts["g_norm"][l, j][None, :]
        dx, nm2, dg2 = _ffn_bwd(dx, sv["x2"], sv["h2"], sv["ffn1"], *sv["w_ffn1"], gn(2), vec(l, 2, 1), vec(l, 2, 2),
                                seq, f"{l}_1", scatter((l, "ffn1"), ffn_shapes))
        dx, mg, nm1, dg1 = _mixer_bwd(dx, sv["x1"], sv["h1"], sv["mix"], lw, gn(1), vec(l, 1, 1), vec(l, 1, 2),
                                      batch, seq, scatter((l, "mix"), mix_shapes))
        dx, nm0, dg0 = _ffn_bwd(dx, sv["x0"], sv["h0"], sv["ffn0"], *sv["w_ffn0"], gn(0), vec(l, 0, 1), vec(l, 0, 2),
                                seq, f"{l}_0", scatter((l, "ffn0"), ffn_shapes))
        dmod_l, gnorm_l = [], []
        for nm, dg in ((nm0, dg0), (nm1, dg1), (nm2, dg2)):
            dmod_l.append(jnp.stack([_per_batch(nm, batch, 0), _per_batch(nm, batch, 1), _per_batch(dg, batch, 0)],
                                    axis=1))
            gnorm_l.append(jnp.sum(nm[:, 2, :], axis=0))
        small["dmod"].insert(0, jnp.stack(dmod_l, axis=1))
        small["g_norm"].insert(0, jnp.stack(gnorm_l))
        ls = mg["lru_sums"]
        small["b_rgate"].insert(0, ls[0])
        small["b_igate"].insert(0, ls[1])
        small["lru_lambda"].insert(0, ls[2] * (-_sigmoid(-wts["lru_lambda"][l])))
        small["conv_b"].insert(0, ls[3])
        small["conv_w"].insert(0, ls[4:8])
        small["w_rgate"].insert(0, _block_diag_grad(mg["dwr"]))
        small["w_igate"].insert(0, _block_diag_grad(mg["dwi"]))
        small["g_mix_out"].insert(0, jnp.sum(mg["gmix_parts"][:, 0, :], axis=0))
        gqk = jnp.sum(mg["gqk_parts"][:, :2, :], axis=0).reshape(2, ATT_W // HEAD_DIM, HEAD_DIM).sum(axis=1)
        small["g_qk"].insert(0, gqk)
        small["b_fgate"].insert(0, jnp.sum(mg["bf_parts"][:, 0, :4], axis=0))
    small = {k: jnp.stack(v) for k, v in small.items()}
    return loss, dx.reshape(batch, seq, d), handles, small


def _row_tile(rows, row_bytes):
    for t in (512, 256, 128, 64, 32, 16):
        if rows % t == 0 and t * row_bytes <= 4 * 1024 * 1024:
            return t
    return rows


def _adamw(parts, w, m, v, name):
    groups, n_parts, rows, cols = parts.shape
    tr = _row_tile(rows, cols * (n_parts * parts.dtype.itemsize + 7 * 4))
    c1 = 1.0 - ADAM_B1 ** ADAM_STEP
    c2 = 1.0 - ADAM_B2 ** ADAM_STEP

    def body(p_ref, w_ref, m_ref, v_ref, g_out, d_out, m_out, v_out):
        g = p_ref[0].astype(F32)
        for n in range(1, n_parts):
            g = g + p_ref[n].astype(F32)
        m_new = ADAM_B1 * m_ref[...] + (1.0 - ADAM_B1) * g
        v_new = ADAM_B2 * v_ref[...] + (1.0 - ADAM_B2) * (g * g)
        g_out[...] = g
        d_out[...] = -ADAM_LR * ((m_new / c1) / (jnp.sqrt(v_new / c2) + ADAM_EPS) + ADAM_WD * w_ref[...])
        m_out[...] = m_new
        v_out[...] = v_new

    tile = pl.BlockSpec((None, tr, cols), lambda g, i: (g, i, 0))
    return pl.pallas_call(
        body, name=name, grid=(groups, rows // tr),
        in_specs=[pl.BlockSpec((None, n_parts, tr, cols), lambda g, i: (g, 0, i, 0)), tile, tile, tile],
        out_specs=[tile] * 4, out_shape=[jax.ShapeDtypeStruct((groups, rows, cols), F32)] * 4,
        compiler_params=_params(2),
    )(parts, w, m, v)


def _sum_parts(parts):
    n_parts, rows, cols = parts.shape

    def body(p_ref, o_ref):
        acc = p_ref[0]
        for n in range(1, n_parts):
            acc = acc + p_ref[n]
        o_ref[...] = acc

    return pl.pallas_call(body, name="sum_small", out_shape=jax.ShapeDtypeStruct((rows, cols), F32),
                          compiler_params=pltpu.CompilerParams(vmem_limit_bytes=VMEM_LIMIT_BYTES))(parts)


def _flatten(arrays, multiple):
    chunks = []
    for a in arrays:
        flat = a.reshape(-1).astype(F32)
        chunks.append(jnp.pad(flat, (0, (-flat.shape[0]) % multiple)).reshape(-1, LANES))
    return jnp.concatenate(chunks, axis=0)


def _unflatten(flat2d, shapes, multiple):
    flat2d, out, row = flat2d.reshape(-1, LANES), [], 0
    for s in shapes:
        n = math.prod(s)
        rows = (n + multiple - 1) // multiple * (multiple // LANES)
        out.append(flat2d[row:row + rows].reshape(-1)[:n].reshape(s))
        row += rows
    return out


SMALL_NAMES = ("b_ada", "g_norm", "b_fgate", "conv_w", "conv_b", "w_rgate", "b_rgate", "w_igate", "b_igate",
               "lru_lambda", "g_qk", "g_mix_out")
WEIGHT_NAMES = ("w_ada", "b_ada", "g_norm", "w_ffn_up", "w_ffn_down", "w_in", "b_fgate", "conv_w", "conv_b",
                "w_rgate", "b_rgate", "w_igate", "b_igate", "lru_lambda", "g_qk", "g_mix_out", "w_out")


def kernel(x, c, w_ada, b_ada, g_norm, w_ffn_up, w_ffn_down, w_in, b_fgate, conv_w, conv_b, w_rgate, b_rgate, w_igate, b_igate, lru_lambda, g_qk, g_mix_out, w_out, loss_target, m_w_ada, m_b_ada, m_g_norm, m_w_ffn_up, m_w_ffn_down, m_w_in, m_b_fgate, m_conv_w, m_conv_b, m_w_rgate, m_b_rgate, m_w_igate, m_b_igate, m_lru_lambda, m_g_qk, m_g_mix_out, m_w_out, v_w_ada, v_b_ada, v_g_norm, v_w_ffn_up, v_w_ffn_down, v_w_in, v_b_fgate, v_conv_w, v_conv_b, v_w_rgate, v_b_rgate, v_w_igate, v_b_igate, v_lru_lambda, v_g_qk, v_g_mix_out, v_w_out):
    batch, seq, d = x.shape
    n_layers = w_ada.shape[0]
    me = 4 * lax.axis_index("x") + 2 * lax.axis_index("y") + lax.axis_index("c")
    weights = dict(w_ada=w_ada, b_ada=b_ada, g_norm=g_norm, w_ffn_up=w_ffn_up, w_ffn_down=w_ffn_down, w_in=w_in,
                   b_fgate=b_fgate, conv_w=conv_w, conv_b=conv_b, w_rgate=w_rgate, b_rgate=b_rgate, w_igate=w_igate,
                   b_igate=b_igate, lru_lambda=lru_lambda, g_qk=g_qk, g_mix_out=g_mix_out, w_out=w_out)
    moments_m = dict(w_ada=m_w_ada, b_ada=m_b_ada, g_norm=m_g_norm, w_ffn_up=m_w_ffn_up, w_ffn_down=m_w_ffn_down,
                     w_in=m_w_in, b_fgate=m_b_fgate, conv_w=m_conv_w, conv_b=m_conv_b, w_rgate=m_w_rgate,
                     b_rgate=m_b_rgate, w_igate=m_w_igate, b_igate=m_b_igate, lru_lambda=m_lru_lambda, g_qk=m_g_qk,
                     g_mix_out=m_g_mix_out, w_out=m_w_out)
    moments_v = dict(w_ada=v_w_ada, b_ada=v_b_ada, g_norm=v_g_norm, w_ffn_up=v_w_ffn_up, w_ffn_down=v_w_ffn_down,
                     w_in=v_w_in, b_fgate=v_b_fgate, conv_w=v_conv_w, conv_b=v_conv_b, w_rgate=v_w_rgate,
                     b_rgate=v_b_rgate, w_igate=v_w_igate, b_igate=v_b_igate, lru_lambda=v_lru_lambda, g_qk=v_g_qk,
                     g_mix_out=v_g_mix_out, w_out=v_w_out)

    c_all, gn_all, cw_all = _exchange([(c, 0), (g_norm, 0), (conv_w, 0)], [], "gather_small_weights")
    c_all = c_all.reshape(N_DEV * batch, d)
    n_ada = w_ada.shape[-1]
    g_norm_full = gn_all.transpose(1, 2, 0, 3).reshape(n_layers, 3, d)
    conv_w_full = cw_all.transpose(1, 2, 0, 3).reshape(n_layers, 4, LRU_W)

    b_ada_loc = lax.dynamic_slice_in_dim(b_ada, me * n_ada, n_ada, axis=1)
    silu = lambda t: t * _sigmoid(t)

    def bias_epilogue(p, e_refs, o_refs):
        o_refs[0][...] = p + e_refs[0][...]

    mod_loc = []
    for l in range(n_layers):
        (ml,) = _mm(c_all, w_ada, mode="nn", tm=c_all.shape[0], tn=n_ada, tk=d, b_lead=(l,), a_pre=silu,
                    name=f"ada_{l}", extras=[(b_ada_loc[l][None, :], (1, n_ada), lambda i, j: (0, 0))],
                    outs=[((c_all.shape[0], n_ada), F32, (c_all.shape[0], n_ada), lambda i, j: (0, 0))],
                    epilogue=bias_epilogue)
        mod_loc.append(ml)
    (mod_all,) = _exchange([(jnp.stack(mod_loc), 0)], [], "gather_mod")
    mod_all = mod_all.transpose(1, 2, 0, 3).reshape(n_layers, N_DEV * batch, 9 * d)

    cast = lambda w, token: (w + token[0, 0]).astype(BF16)
    ffn_ops = lambda l, f, token: [(cast(w_ffn_up[l, f], token), 0, "gather"),
                                   (cast(w_ffn_down[l, f], token), 0, "gather")]
    mix_ops = lambda l, token: [(cast(jnp.pad(w_in[l], ((0, 0), (0, N_IN_PAD - N_IN))), token), 0, "gather"),
                                (cast(w_out[l], token), 0, "gather")]
    behind = lambda w, token: w + token[0, 0].astype(BF16)
    flights, landed_rest = {}, []

    def start(key, ops):
        flights[key], token = _flight_start(ops, f"weights_{key}_start")
        return token

    def wait(key, after):
        return _flight_wait(flights[key], after, f"weights_{key}_wait")

    def big_weights(l, part, after):
        if (l, part) == (0, "ffn0"):
            def wdown(after_up):
                (wd,), landed_down = wait("down", after_up)
                return behind(wd, start("mix", mix_ops(0, landed_down))).reshape(D_FF, d)

            return wup_first, wdown
        if (l, part) == (0, "mix"):
            (wi, wo), landed = wait("mix", after)
            rest = ffn_ops(0, 1, landed)
            for ll in range(1, n_layers):
                rest += ffn_ops(ll, 0, landed) + mix_ops(ll, landed) + ffn_ops(ll, 1, landed)
            return behind(wi, start("rest", rest)).reshape(d, N_IN_PAD), wo.reshape(d, d)
        if not landed_rest:
            landed_rest.extend(wait("rest", after)[0])
        at = 0 if l == 0 else 2 + 6 * (l - 1) + {"ffn0": 0, "mix": 2, "ffn1": 4}[part]
        first, second = landed_rest[at], landed_rest[at + 1]
        if part == "mix":
            return first.reshape(d, N_IN_PAD), second.reshape(d, d)
        return first, second.reshape(D_FF, d)

    w_up_first, mod_all = lax.optimization_barrier((w_ffn_up[0, 0].astype(BF16), mod_all))
    (wup_first,) = _exchange([(w_up_first, 0)], [], "gather_w_up_first", two_level=True)
    w_down_first, wup_first = lax.optimization_barrier((w_ffn_down[0, 0].astype(BF16), wup_first))
    token = start("down", [(w_down_first, 0, "gather")])
    mod_me = lax.dynamic_slice_in_dim(mod_all + token[0, 0], me * batch, batch, axis=1)
    mod_me = mod_me.reshape(n_layers, batch, 3, 3, d)

    wts = dict(g_norm=g_norm_full, conv_w=conv_w_full, conv_b=conv_b, w_rgate=w_rgate, b_rgate=b_rgate, w_igate=w_igate, b_igate=b_igate,
               lru_lambda=lru_lambda, g_qk=g_qk, g_mix_out=g_mix_out, b_fgate=b_fgate)
    loss_part, grad_x, handles, small = _local_step(x, loss_target, mod_me, wts, big_weights)

    dmod_me = small.pop("dmod").reshape(n_layers, batch, 9 * d)
    small["b_ada"] = jnp.sum(dmod_me, axis=1)
    small_shapes = [(1,)] + [weights[k].shape if k not in ("g_norm", "conv_w") else small[k].shape for k in SMALL_NAMES]
    small_flat = _flatten([loss_part.reshape(1)] + [small[k] for k in SMALL_NAMES], 16 * LANES)
    dmod_all, small_all = _exchange([(dmod_me, 0), (small_flat, 0)], [], "gather_small", two_level=True)
    landed = {key: _flight_wait(h, small_all, f"grads_{key[0]}_{key[1]}_wait")[0] for key, h in handles.items()}
    layer_range = range(n_layers)
    p_up = jnp.stack([jnp.stack([landed[(l, "ffn0")][0], landed[(l, "ffn1")][0]]) for l in layer_range])
    p_down = jnp.stack([jnp.stack([landed[(l, "ffn0")][1], landed[(l, "ffn1")][1]]) for l in layer_range])
    p_in = jnp.stack([landed[(l, "mix")][0] for l in layer_range])
    p_out = jnp.stack([landed[(l, "mix")][1] for l in layer_range])
    small_sum = _unflatten(_sum_parts(small_all), small_shapes, 16 * LANES)
    loss = small_sum[0].reshape(())
    small_grads = dict(zip(SMALL_NAMES, small_sum[1:]))
    small_grads["g_norm"] = lax.dynamic_slice_in_dim(small_grads["g_norm"], me * g_norm.shape[-1], g_norm.shape[-1], 2)
    small_grads["conv_w"] = lax.dynamic_slice_in_dim(small_grads["conv_w"], me * conv_w.shape[-1], conv_w.shape[-1], 2)

    dmod_all = dmod_all.transpose(1, 0, 2, 3).reshape(n_layers, N_DEV * batch, 9 * d)
    dmod_loc = lax.dynamic_slice_in_dim(dmod_all, me * n_ada, n_ada, axis=2)
    g_ada = []
    for l in range(n_layers):
        (gl,) = _mm(c_all, dmod_loc[l], mode="tn", tm=d, tn=n_ada, tk=c_all.shape[0], a_pre=silu, name=f"dw_ada_{l}",
                    outs=[((d, n_ada), F32, (d, n_ada), lambda i, j: (0, 0))], epilogue=_store_epilogue([F32]))
        g_ada.append(gl)
    g_ada = jnp.stack(g_ada)

    results = {}

    def update(name, parts):
        shape = weights[name].shape
        as3d = lambda t: t.reshape((-1,) + shape[-2:])
        outs = _adamw(parts.reshape((-1,) + parts.shape[-3:]), as3d(weights[name]), as3d(moments_m[name]),
                      as3d(moments_v[name]), f"adamw_{name}")
        results[name] = [o.reshape(shape) for o in outs]

    update("w_ada", g_ada[:, None])
    update("w_ffn_up", p_up)
    update("w_ffn_down", p_down)
    update("w_in", p_in[..., :N_IN])
    update("w_out", p_out)
    sm_shapes = [weights[k].shape for k in SMALL_NAMES]
    flat = lambda src: _flatten([src[k] for k in SMALL_NAMES], 16 * LANES)
    sm_out = _adamw(flat(small_grads)[None, None], flat(weights)[None], flat(moments_m)[None], flat(moments_v)[None],
                    "adamw_small")
    for k, vals in zip(SMALL_NAMES, zip(*[_unflatten(o, sm_shapes, 16 * LANES) for o in sm_out])):
        results[k] = list(vals)

    outs = [loss, grad_x]
    for n in range(4):
        outs += [results[k][n] for k in WEIGHT_NAMES]
    return tuple(outs)
```

```python
import functools
import math

import jax
import jax.numpy as jnp
from jax import lax
from jax.experimental import pallas as pl
from jax.experimental.pallas import tpu as pltpu

F32 = jnp.float32
BF16 = jnp.bfloat16

N_DEV = 8
D_MODEL = 1024
D_FF = 2816
FF_SHARD = 2 * D_FF // N_DEV
N_FF_SHARD = D_FF // FF_SHARD
HEAD_DIM = 64
LRU_W = 512
ATT_W = 256
N_IN = 2564
N_IN_PAD = 2688
LANES = 128
SUBLANES = 8
BLK = 256
TQ = 512
KB_PER_Q = TQ // BLK
EPS = 1e-6
LRU_C = 8.0
NEG_BIG = -1e30
VMEM_LIMIT_BYTES = 48 * 1024 * 1024

ADAM_LR, ADAM_B1, ADAM_B2, ADAM_EPS, ADAM_WD, ADAM_STEP = 0.001, 0.9, 0.999, 1e-08, 0.01, 10

COL_SBQ, COL_SBK, COL_SBV = 8, 10, 12
COL_FXV, COL_FXF = 18, 20

NN = (((1,), (0,)), ((), ()))
NT = (((1,), (1,)), ((), ()))
TN = (((0,), (0,)), ((), ()))


def _params(n_axes):
    return pltpu.CompilerParams(dimension_semantics=("arbitrary",) * n_axes, vmem_limit_bytes=VMEM_LIMIT_BYTES)


def _tok_tile(seq):
    for t in (512, 256, 128):
        if seq % t == 0:
            return t
    raise ValueError(f"sequence length {seq} is not a multiple of 128")


def _grad_tokens(m, tm):
    for mult in (8, 4, 2):
        if m % (mult * tm) == 0:
            return mult * tm
    return tm


def _dot(a, b, dims=NN):
    return lax.dot_general(a, b, dims, preferred_element_type=F32)


def _sigmoid(x):
    return 1.0 / (1.0 + jnp.exp(-x))


def _softplus(x):
    return jnp.maximum(x, 0.0) + jnp.log(1.0 + jnp.exp(-jnp.abs(x)))


def _gelu_parts(x):
    k0, k1 = math.sqrt(2.0 / math.pi), 0.044715
    t = jnp.tanh(k0 * (x + k1 * x * x * x))
    gelu = 0.5 * x * (1.0 + t)
    dgelu = 0.5 * (1.0 + t) + 0.5 * x * (1.0 - t * t) * k0 * (1.0 + 3.0 * k1 * x * x)
    return gelu, dgelu


def _neg_expm1(x):
    series = -x * (1.0 + x * (0.5 + x * (1.0 / 6.0 + x * (1.0 / 24.0 + x * (1.0 / 120.0 + x * (1.0 / 720.0))))))
    return jnp.where(x > -0.25, series, 1.0 - jnp.exp(x))


def _split2(x):
    hi = x.astype(BF16)
    lo = (x - hi.astype(F32)).astype(BF16)
    return hi, lo


def _split3(x):
    hi = x.astype(BF16)
    r = x - hi.astype(F32)
    mid = r.astype(BF16)
    lo = (r - mid.astype(F32)).astype(BF16)
    return hi, mid, lo


def _rows_to_block(rows, width):
    r = lax.broadcasted_iota(jnp.int32, (SUBLANES, width), 0)
    out = jnp.zeros((SUBLANES, width), F32)
    for n, v in enumerate(rows):
        out = jnp.where(r == n, jnp.broadcast_to(v, (SUBLANES, width)), out)
    return out


def _colsum(x):
    return jnp.sum(x, axis=0, keepdims=True)


def _exchange(gathers, scatters, name, two_level=False):
    assert not (two_level and scatters)
    n_g = len(gathers)
    ops = [a for a, _ in gathers] + [a for a, _ in scatters]
    n = len(ops)
    out_shape = [jax.ShapeDtypeStruct(a.shape[:nl] + (N_DEV,) + a.shape[nl:], a.dtype) for a, nl in gathers]
    out_shape += [jax.ShapeDtypeStruct(a.shape, a.dtype) for a, _ in scatters]
    items = []
    for k, (a, nl) in enumerate(list(gathers) + list(scatters)):
        for flat in range(math.prod(a.shape[:nl])):
            idx, rem = [], flat
            for dim in reversed(a.shape[:nl]):
                idx.insert(0, rem % dim)
                rem //= dim
            items.append((k, tuple(idx)))
    n_items = len(items)

    def body(*refs):
        ins, outs = refs[:n], refs[n:2 * n]
        send_sems, recv_sems, local_sems = refs[2 * n:]
        x, y, c = lax.axis_index("x"), lax.axis_index("y"), lax.axis_index("c")
        me = 4 * x + 2 * y + c

        def at(ref, idx):
            return ref.at[idx] if idx else ref

        def src(it, peer):
            k, idx = items[it]
            return at(ins[k], idx) if k < n_g else at(ins[k], idx + (peer,))

        def slot(it, s):
            k, idx = items[it]
            return at(outs[k], idx + (s,))

        def remote(it, rel, source, s, to):
            return pltpu.make_async_remote_copy(
                src_ref=source, dst_ref=slot(it, s), send_sem=send_sems.at[it, rel], recv_sem=recv_sems.at[it, rel],
                device_id=to, device_id_type=pl.DeviceIdType.MESH)

        local = [pltpu.make_async_copy(src(it, me), slot(it, me), local_sems.at[it]) for it in range(n_items)]
        for cp in local:
            cp.start()

        if not two_level:
            started = []
            for r in range(1, N_DEV):
                px = 1 - x if (r >> 2) & 1 else x
                py = 1 - y if (r >> 1) & 1 else y
                pc = 1 - c if r & 1 else c
                for it in range(n_items):
                    cp = remote(it, r - 1, src(it, 4 * px + 2 * py + pc), me, (px, py, pc))
                    cp.start()
                    started.append(cp)
            for cp in started:
                cp.wait()
        else:
            sibling, chips = (x, y, 1 - c), [(1 - x, y), (x, 1 - y), (1 - x, 1 - y)]
            sib = 4 * x + 2 * y + (1 - c)
            started = []
            for it in range(n_items):
                started.append(remote(it, 0, src(it, me), me, sibling))
                started += [remote(it, 1 + j, src(it, me), me, (cx, cy, c)) for j, (cx, cy) in enumerate(chips)]
            for cp in started:
                cp.start()
            for j, (cx, cy) in enumerate(chips):
                s = 4 * cx + 2 * cy + c
                for it in range(n_items):
                    remote(it, 1 + j, slot(it, s), s, sibling).wait_recv()
                    cp = remote(it, 4 + j, slot(it, s), s, sibling)
                    cp.start()
                    started.append(cp)
            for it in range(n_items):
                remote(it, 0, slot(it, sib), sib, sibling).wait_recv()
                for j, (cx, cy) in enumerate(chips):
                    s = 4 * cx + 2 * cy + (1 - c)
                    remote(it, 4 + j, slot(it, s), s, sibling).wait_recv()
            for cp in started:
                cp.wait_send()
        for cp in local:
            cp.wait()

    hbm = pl.BlockSpec(memory_space=pltpu.HBM)
    return pl.pallas_call(
        body, name=name, out_shape=out_shape,
        in_specs=[hbm] * n, out_specs=[hbm] * n,
        scratch_shapes=[pltpu.SemaphoreType.DMA((n_items, N_DEV - 1)), pltpu.SemaphoreType.DMA((n_items, N_DEV - 1)),
                        pltpu.SemaphoreType.DMA((n_items,))],
    )(*ops)


def _lead_items(ops):
    items = []
    for k, (a, nl) in enumerate(ops):
        for flat in range(math.prod(a.shape[:nl])):
            idx, rem = [], flat
            for dim in reversed(a.shape[:nl]):
                idx.insert(0, rem % dim)
                rem //= dim
            items.append((k, tuple(idx)))
    return items


def _flight_copies(ops, srcs, lands, send_sems, recv_sems):
    x, y, c = lax.axis_index("x"), lax.axis_index("y"), lax.axis_index("c")
    me = 4 * x + 2 * y + c
    copies = []
    for r in range(1, N_DEV):
        px = 1 - x if (r >> 2) & 1 else x
        py = 1 - y if (r >> 1) & 1 else y
        pc = 1 - c if r & 1 else c
        for it, (k, idx) in enumerate(_lead_items([(a, nl) for a, nl, _ in ops])):
            src = srcs[k].at[idx + (4 * px + 2 * py + pc,)] if ops[k][2] == "scatter" else (
                srcs[k].at[idx] if idx else srcs[k])
            copies.append(pltpu.make_async_remote_copy(
                src_ref=src, dst_ref=lands[k].at[idx + (me,)],
                send_sem=send_sems.at[it * (N_DEV - 1) + r - 1], recv_sem=recv_sems.at[it * (N_DEV - 1) + r - 1],
                device_id=(px, py, pc), device_id_type=pl.DeviceIdType.MESH))
    return copies


def _flight_start(ops, name):
    n = len(ops)
    me = 4 * lax.axis_index("x") + 2 * lax.axis_index("y") + lax.axis_index("c")
    srcs, lands = [], []
    for a, nl, kind in ops:
        if kind == "scatter":
            own, shape = lax.dynamic_slice_in_dim(a, me, 1, axis=nl), a.shape
        else:
            own, shape = jnp.expand_dims(a, nl), a.shape[:nl] + (N_DEV,) + a.shape[nl:]
        start = (0,) * nl + (me,) + (0,) * (len(shape) - nl - 1)
        lands.append(pltpu.with_memory_space_constraint(
            lax.dynamic_update_slice(lax.empty(shape, a.dtype), own, start), pltpu.HBM))
        srcs.append(pltpu.with_memory_space_constraint(a, pltpu.HBM))

    def body(*refs):
        for cp in _flight_copies(ops, refs[:n], refs[n:2 * n], refs[2 * n], refs[2 * n + 1]):
            cp.start()
        refs[-1][...] = jnp.zeros_like(refs[-1])

    hbm, sem = pl.BlockSpec(memory_space=pltpu.HBM), pl.BlockSpec(memory_space=pltpu.SEMAPHORE)
    n_items = len(_lead_items([(a, nl) for a, nl, _ in ops]))
    sems = pltpu.SemaphoreType.DMA((n_items * (N_DEV - 1),))
    res = pl.pallas_call(
        body, name=name,
        out_shape=[sems, sems] + [pltpu.HBM(a.shape, a.dtype) for a in srcs + lands]
        + [jax.ShapeDtypeStruct((SUBLANES, LANES), F32)],
        in_specs=[hbm] * (2 * n), out_specs=[sem, sem] + [hbm] * (2 * n) + [pl.BlockSpec(memory_space=pltpu.VMEM)],
        input_output_aliases={i: 2 + i for i in range(2 * n)},
        compiler_params=pltpu.CompilerParams(has_side_effects=pltpu.SideEffectType.DATAFLOW_SIDE_EFFECTING),
    )(*srcs, *lands)
    return (ops, res[0], res[1], res[2:2 + n], res[2 + n:2 + 2 * n]), res[-1]


def _flight_wait(handle, after, name):
    ops, send_sems, recv_sems, srcs, lands = handle
    n = len(ops)

    def body(*refs):
        for cp in _flight_copies(ops, refs[:n], refs[n:2 * n], refs[2 * n], refs[2 * n + 1]):
            cp.wait_send()
            cp.wait_recv()
        refs[-1][...] = jnp.zeros_like(refs[-1])

    hbm, sem = pl.BlockSpec(memory_space=pltpu.HBM), pl.BlockSpec(memory_space=pltpu.SEMAPHORE)
    res = pl.pallas_call(
        body, name=name,
        out_shape=[pltpu.HBM(a.shape, a.dtype) for a in list(srcs) + list(lands)]
        + [jax.ShapeDtypeStruct((SUBLANES, LANES), F32)],
        in_specs=[hbm] * (2 * n) + [sem, sem, pl.BlockSpec(memory_space=pl.ANY)],
        out_specs=[hbm] * (2 * n) + [pl.BlockSpec(memory_space=pltpu.VMEM)],
        input_output_aliases={i: i for i in range(2 * n)},
        compiler_params=pltpu.CompilerParams(has_side_effects=pltpu.SideEffectType.DATAFLOW_SIDE_EFFECTING),
    )(*srcs, *lands, send_sems, recv_sems, after)
    return res[n:2 * n], res[-1]


def _mm(a, b, *, mode, tm, tn, tk, outs, epilogue, name, extras=(), a_lead=(), b_lead=(), a_pre=None,
        a_spec=None, b_spec=None, shape=None, ksub=1):
    if shape is not None:
        mdim, ndim, kdim = shape
    else:
        if mode == "tn":
            kdim, mdim = a.shape[-2:]
        else:
            mdim, kdim = a.shape[-2:]
        ndim = b.shape[-2] if mode == "nt" else b.shape[-1]
    assert mdim % tm == 0 and ndim % tn == 0 and kdim % tk == 0, (name, mdim, ndim, kdim, tm, tn, tk)
    ni, nj, nk = mdim // tm, ndim // tn, kdim // tk
    a_lead, b_lead = tuple(a_lead), tuple(b_lead)
    a_block = (None,) * len(a_lead) + ((tk, tm) if mode == "tn" else (tm, tk))
    b_block = (None,) * len(b_lead) + ((tn, tk) if mode == "nt" else (tk, tn))
    dims = {"nn": NN, "nt": NT, "tn": TN}[mode]
    ne, no = len(extras), len(outs)

    def a_index(i, j, k):
        return a_lead + ((k, i) if mode == "tn" else (i, k))

    def b_index(i, j, k):
        return b_lead + ((j, k) if mode == "nt" else (k, j))

    if a_spec is not None:
        a_block, a_index = a_spec
    if b_spec is not None:
        b_block, b_index = b_spec

    def body(*refs):
        a_ref, b_ref = refs[0], refs[1]
        e_refs, o_refs = refs[2:2 + ne], refs[2 + ne:2 + ne + no]
        if ksub == 1:
            av = a_ref[...] if a_pre is None else a_pre(a_ref[...])
            p = _dot(av.astype(BF16), b_ref[...].astype(BF16), dims)
        else:
            p = _dot(a_ref[0], b_ref[0], dims)
            for s in range(1, ksub):
                p = p + _dot(a_ref[s], b_ref[s], dims)
        if nk == 1:
            epilogue(p, e_refs, o_refs)
        else:
            acc = refs[-1]
            k = pl.program_id(2)

            @pl.when(k == 0)
            def _():
                acc[...] = p

            @pl.when(k > 0)
            def _():
                acc[...] += p

            @pl.when(k == nk - 1)
            def _():
                epilogue(acc[...], e_refs, o_refs)

    in_specs = [pl.BlockSpec(a_block, a_index), pl.BlockSpec(b_block, b_index)]
    in_specs += [pl.BlockSpec(blk, functools.partial(lambda i, j, k, f: f(i, j), f=f)) for _, blk, f in extras]
    out_specs = [pl.BlockSpec(blk, functools.partial(lambda i, j, k, f: f(i, j), f=f)) for _, _, blk, f in outs]
    res = pl.pallas_call(
        body, name=name, grid=(ni, nj, nk), in_specs=in_specs, out_specs=out_specs,
        out_shape=[jax.ShapeDtypeStruct(s, d) for s, d, _, _ in outs],
        scratch_shapes=[pltpu.VMEM((tm, tn), F32)] if nk > 1 else [],
        compiler_params=_params(3),
    )(a, b, *[e[0] for e in extras])
    return res


def _store_epilogue(dtypes):
    def epi(p, e_refs, o_refs):
        for o, dt in zip(o_refs, dtypes):
            o[...] = p.astype(dt)
    return epi


def _normmod(x, gn, scale, shift, seq, name):
    m, d = x.shape
    tm = _tok_tile(seq)
    tpb = seq // tm

    def body(x_ref, gn_ref, sc_ref, sh_ref, h_ref):
        xv = x_ref[...]
        rstd = lax.rsqrt(jnp.mean(xv * xv, axis=-1, keepdims=True) + EPS)
        h_ref[...] = (xv * rstd * gn_ref[...] * (1.0 + sc_ref[0]) + sh_ref[0]).astype(BF16)

    vec = pl.BlockSpec((1, 1, d), lambda i: (i // tpb, 0, 0))
    return pl.pallas_call(
        body, name=name, grid=(m // tm,),
        in_specs=[pl.BlockSpec((tm, d), lambda i: (i, 0)), pl.BlockSpec((1, d), lambda i: (0, 0)), vec, vec],
        out_specs=pl.BlockSpec((tm, d), lambda i: (i, 0)),
        out_shape=jax.ShapeDtypeStruct((m, d), BF16), compiler_params=_params(1),
    )(x, gn, scale, shift)


def _normmod_bwd_epilogue(p, e_refs, o_refs):
    x_ref, dxo_ref, gn_ref, sc_ref = e_refs
    xv = x_ref[...]
    rstd = lax.rsqrt(jnp.mean(xv * xv, axis=-1, keepdims=True) + EPS)
    xhat = xv * rstd
    gn, sc1 = gn_ref[...], 1.0 + sc_ref[0]
    dxhat = p * (gn * sc1)
    dx = rstd * (dxhat - xhat * jnp.mean(dxhat * xhat, axis=-1, keepdims=True))
    o_refs[0][...] = dxo_ref[...] + dx
    t = p * xhat
    o_refs[1][0] = _rows_to_block([_colsum(p), _colsum(t * gn), _colsum(t * sc1)], p.shape[1])


def _residual_bwd(dx, f, gate, fac, seq, name):
    m, d = dx.shape
    tm = _tok_tile(seq)
    tpb = seq // tm

    def body(dx_ref, f_ref, g_ref, df_ref, dg_ref):
        dxv = dx_ref[...]
        df_ref[...] = ((fac * (1.0 + g_ref[0])) * dxv).astype(BF16)
        dg_ref[0] = _rows_to_block([_colsum((fac * dxv) * f_ref[...].astype(F32))], d)

    tile = pl.BlockSpec((tm, d), lambda i: (i, 0))
    return pl.pallas_call(
        body, name=name, grid=(m // tm,),
        in_specs=[tile, tile, pl.BlockSpec((1, 1, d), lambda i: (i // tpb, 0, 0))],
        out_specs=[tile, pl.BlockSpec((1, SUBLANES, d), lambda i: (i, 0, 0))],
        out_shape=[jax.ShapeDtypeStruct((m, d), BF16), jax.ShapeDtypeStruct((m // tm, SUBLANES, d), F32)],
        compiler_params=_params(1),
    )(dx, f, gate)


def _loss_head(y, target, seq):
    m, d = y.shape
    tm = _tok_tile(seq)

    def body(y_ref, t_ref, dy_ref, l_ref):
        err = y_ref[...] - t_ref[...]
        dy_ref[...] = err * (1.0 / d)
        part = 0.5 * jnp.sum(jnp.mean(err * err, axis=-1, keepdims=True), axis=0, keepdims=True)
        l_ref[0] = jnp.broadcast_to(part, (SUBLANES, LANES))

    tile = pl.BlockSpec((tm, d), lambda i: (i, 0))
    return pl.pallas_call(
        body, name="loss_head", grid=(m // tm,), in_specs=[tile, tile],
        out_specs=[tile, pl.BlockSpec((1, SUBLANES, LANES), lambda i: (i, 0, 0))],
        out_shape=[jax.ShapeDtypeStruct((m, d), F32), jax.ShapeDtypeStruct((m // tm, SUBLANES, LANES), F32)],
        compiler_params=_params(1),
    )(y, target)


def _residual_io(x, gate, next_norm, tm, tpb):
    m, d = x.shape
    row = lambda i, j: (i, 0)
    vec = lambda i, j: (i // tpb, 0, 0)
    extras = [(x, (tm, d), row), (gate, (1, 1, d), vec)]
    outs = [((m, d), F32, (tm, d), row), ((m, d), BF16, (tm, d), row)]
    if next_norm:
        gn, scale, shift = next_norm
        extras += [(gn, (1, d), lambda i, j: (0, 0)), (scale, (1, 1, d), vec), (shift, (1, 1, d), vec)]
        outs.append(((m, d), BF16, (tm, d), row))
    return extras, outs


def _residual_epilogue(fac, p, e_refs, o_refs):
    x_out = e_refs[0][...] + (fac * (1.0 + e_refs[1][0])) * p
    o_refs[0][...] = x_out
    o_refs[1][...] = p.astype(BF16)
    if len(o_refs) > 2:
        rstd = lax.rsqrt(jnp.mean(x_out * x_out, axis=-1, keepdims=True) + EPS)
        o_refs[2][...] = (x_out * rstd * e_refs[2][...] * (1.0 + e_refs[3][0]) + e_refs[4][0]).astype(BF16)


def _ffn_fwd(x, h, wup, wdown, gate, seq, tag, next_norm=None):
    m, d = x.shape
    tm = _tok_tile(seq)
    tpb = seq // tm

    def up_body(h_ref, wg_ref, wu_ref, a_ref, gu_ref):
        hv = h_ref[...]
        g, u = _dot(hv, wg_ref[...]), _dot(hv, wu_ref[...])
        sg = _sigmoid(g)
        silu = g * sg
        a_ref[...] = (silu * u).astype(BF16)
        gu_ref[0] = (u * (sg * (1.0 + g * (1.0 - sg)))).astype(BF16)
        gu_ref[1] = silu.astype(BF16)

    wblk = (None, d, FF_SHARD)
    a, gu = pl.pallas_call(
        up_body, name=f"ffn_up_{tag}", grid=(N_FF_SHARD, m // tm),
        in_specs=[pl.BlockSpec((tm, d), lambda j, i: (i, 0)),
                  pl.BlockSpec(wblk, lambda j, i: (j, 0, 0)),
                  pl.BlockSpec(wblk, lambda j, i: (j + N_FF_SHARD, 0, 0))],
        out_specs=[pl.BlockSpec((None, tm, FF_SHARD), lambda j, i: (j, i, 0)),
                   pl.BlockSpec((2, None, tm, FF_SHARD), lambda j, i: (0, j, i, 0))],
        out_shape=[jax.ShapeDtypeStruct((N_FF_SHARD, m, FF_SHARD), BF16),
                   jax.ShapeDtypeStruct((2, N_FF_SHARD, m, FF_SHARD), BF16)],
        compiler_params=_params(2),
    )(h, wup, wup)

    if callable(wdown):
        wdown = wdown(a)
    wdown3 = wdown.reshape(N_FF_SHARD, FF_SHARD, d)
    extras, outs = _residual_io(x, gate, next_norm, tm, tpb)
    res = _mm(a, wdown3, mode="nn", tm=tm, tn=d, tk=D_FF, ksub=N_FF_SHARD, name=f"ffn_down_{tag}",
              shape=(m, d, D_FF), a_spec=((N_FF_SHARD, tm, FF_SHARD), lambda i, j, k: (0, i, 0)),
              b_spec=((N_FF_SHARD, FF_SHARD, d), lambda i, j, k: (0, 0, 0)),
              extras=extras, outs=outs, epilogue=functools.partial(_residual_epilogue, 0.5))
    return res[0], (a, gu, res[1]), wdown, (res[2] if next_norm else None)


def _ffn_bwd(dx_out, x, h, saved, wup, wdown, gn, scale, gate, seq, tag, on_grads):
    a, gu, f = saved
    m, d = x.shape
    tm = _tok_tile(seq)
    tpb = seq // tm
    df, dgate_parts = _residual_bwd(dx_out, f, gate, 0.5, seq, f"ffn_res_bwd_{tag}")

    def act_bwd_epilogue(p, e_refs, o_refs):
        o_refs[0][0] = (p * e_refs[0][0].astype(F32)).astype(BF16)
        o_refs[0][1] = (p * e_refs[0][1].astype(F32)).astype(BF16)

    gu_blk = (2, None, tm, FF_SHARD)
    (dgu,) = _mm(df, wdown, mode="nt", tm=tm, tn=FF_SHARD, tk=d, name=f"ffn_down_dx_{tag}", shape=(m, D_FF, d),
                 b_spec=((FF_SHARD, d), lambda i, j, k: (j, 0)),
                 extras=[(gu, gu_blk, lambda i, j: (0, j, i, 0))],
                 outs=[((2, N_FF_SHARD, m, FF_SHARD), BF16, gu_blk, lambda i, j: (0, j, i, 0))],
                 epilogue=act_bwd_epilogue)
    tt = _grad_tokens(m, tm)
    (dwdown,) = _mm(a, df, mode="tn", tm=FF_SHARD, tn=d, tk=tt, name=f"ffn_dwdown_{tag}", shape=(D_FF, d, m),
                    a_spec=((None, tt, FF_SHARD), lambda i, j, k: (i, k, 0)),
                    outs=[((D_FF, d), BF16, (FF_SHARD, d), lambda i, j: (i, 0))], epilogue=_store_epilogue([BF16]))
    dgu8 = dgu.reshape(2 * N_FF_SHARD, m, FF_SHARD)
    (dwup,) = _mm(h, dgu8, mode="tn", tm=d, tn=FF_SHARD, tk=tt, name=f"ffn_dwup_{tag}", shape=(d, 2 * D_FF, m),
                  b_spec=((None, tt, FF_SHARD), lambda i, j, k: (j, k, 0)),
                  outs=[((2 * N_FF_SHARD, d, FF_SHARD), BF16, (None, d, FF_SHARD), lambda i, j: (j, 0, 0))],
                  epilogue=_store_epilogue([BF16]))
    scale = scale + on_grads(dwup, dwdown)[0, 0]
    dx, nm_parts = _mm(dgu8, wup, mode="nt", tm=tm, tn=d, tk=D_FF, ksub=N_FF_SHARD, name=f"ffn_up_dx_{tag}",
                       shape=(m, d, 2 * D_FF), a_spec=((N_FF_SHARD, tm, FF_SHARD), lambda i, j, k: (k, i, 0)),
                       b_spec=((N_FF_SHARD, d, FF_SHARD), lambda i, j, k: (k, 0, 0)),
                       extras=[(x, (tm, d), lambda i, j: (i, 0)), (dx_out, (tm, d), lambda i, j: (i, 0)),
                               (gn, (1, d), lambda i, j: (0, 0)), (scale, (1, 1, d), lambda i, j: (i // tpb, 0, 0))],
                       outs=[((m, d), F32, (tm, d), lambda i, j: (i, 0)),
                             ((m // tm, SUBLANES, d), F32, (1, SUBLANES, d), lambda i, j: (i, 0, 0))],
                       epilogue=_normmod_bwd_epilogue)
    return dx, nm_parts, dgate_parts


def _shift_down(ext, n, rows):
    if n:
        ext = pltpu.roll(ext, n, 0)
    return ext[SUBLANES:SUBLANES + rows]


def _lru_gates(u, wr_ref, br_ref, wi_ref, bi_ref, lam_ref):
    ub = u.astype(BF16)
    r = _sigmoid(_dot(ub, wr_ref[...]) + br_ref[...])
    ig = _sigmoid(_dot(ub, wi_ref[...]) + bi_ref[...])
    sp = _softplus(-lam_ref[...])
    log_a = (-LRU_C * r) * sp
    a = jnp.exp(log_a)
    mult = jnp.sqrt(_neg_expm1(2.0 * log_a))
    return r, ig, sp, a, mult


def _conv(ext, cw_ref, cb_ref, rows):
    u = cb_ref[...] + cw_ref[3:4, :] * _shift_down(ext, 0, rows)
    for k in range(3):
        u = u + cw_ref[k:k + 1, :] * _shift_down(ext, 3 - k, rows)
    return u


def _lru_halo_spec(seq, ts):
    return pl.BlockSpec((SUBLANES, LRU_W),
                        lambda b, i: (jnp.maximum(b * (seq // SUBLANES) + i * (ts // SUBLANES) - 1, 0), 0))


def _lru_fwd(proj32, conv_w, conv_b, wr, br, wi, bi, lam, batch, seq):
    m = proj32.shape[0]
    ts = _tok_tile(seq)
    nt = seq // ts
    row = lambda b, i: (b * nt + i, 0)

    def body(x_ref, halo_ref, g_ref, cw_ref, cb_ref, wr_ref, br_ref, wi_ref, bi_ref, lam_ref,
             y_ref, h_ref, a_scr, b_scr, carry):
        i = pl.program_id(1)
        halo = jnp.where(i > 0, halo_ref[...], 0.0)
        ext = jnp.concatenate([halo, x_ref[...]], axis=0)
        u = _conv(ext, cw_ref, cb_ref, ts)
        _, ig, _, a, mult = _lru_gates(u, wr_ref, br_ref, wi_ref, bi_ref, lam_ref)
        a_scr[...] = a
        b_scr[...] = mult * (ig * u)

        @pl.when(i == 0)
        def _():
            carry[...] = jnp.zeros_like(carry)

        rid = lax.broadcasted_iota(jnp.int32, (SUBLANES, LRU_W), 0)

        def chunk(c, hprev):
            off = pl.multiple_of(c * SUBLANES, SUBLANES)
            av, bv = a_scr[pl.ds(off, SUBLANES), :], b_scr[pl.ds(off, SUBLANES), :]
            for d in (1, 2, 4):
                keep = rid >= d
                bv = jnp.where(keep, av * pltpu.roll(bv, d, 0) + bv, bv)
                av = jnp.where(keep, av * pltpu.roll(av, d, 0), av)
            h = av * hprev + bv
            h_ref[pl.ds(off, SUBLANES), :] = h
            return h[SUBLANES - 1:SUBLANES, :]

        carry[...] = lax.fori_loop(0, ts // SUBLANES, chunk, carry[...])
        gelu, _ = _gelu_parts(g_ref[...])
        y_ref[...] = h_ref[...] * gelu

    full = lambda shape: pl.BlockSpec(shape, lambda b, i: (0,) * len(shape))
    return pl.pallas_call(
        body, name="lru_fwd", grid=(batch, nt),
        in_specs=[pl.BlockSpec((ts, LRU_W), row), _lru_halo_spec(seq, ts),
                  pl.BlockSpec((ts, LRU_W), lambda b, i: (b * nt + i, 1)),
                  full((4, LRU_W)), full((1, LRU_W)), full((LRU_W, LRU_W)), full((1, LRU_W)),
                  full((LRU_W, LRU_W)), full((1, LRU_W)), full((1, LRU_W))],
        out_specs=[pl.BlockSpec((ts, LRU_W), row), pl.BlockSpec((ts, LRU_W), row)],
        out_shape=[jax.ShapeDtypeStruct((m, LRU_W), F32), jax.ShapeDtypeStruct((m, LRU_W), F32)],
        scratch_shapes=[pltpu.VMEM((ts, LRU_W), F32), pltpu.VMEM((ts, LRU_W), F32), pltpu.VMEM((1, LRU_W), F32)],
        compiler_params=_params(2),
    )(proj32, proj32, proj32, conv_w, conv_b, wr, br, wi, bi, lam)


def _lru_bwd(dy, proj32, h, conv_w, conv_b, wr, br, wi, bi, lam, batch, seq):
    m = proj32.shape[0]
    ts = _tok_tile(seq)
    nt = seq // ts
    row = lambda b, i: (b * nt + (nt - 1 - i), 0)
    halo = pl.BlockSpec((SUBLANES, LRU_W),
                        lambda b, i: (jnp.maximum(b * (seq // SUBLANES) + (nt - 1 - i) * (ts // SUBLANES) - 1, 0), 0))

    def body(dy_ref, x_ref, xhalo_ref, g_ref, h_ref, hhalo_ref, cw_ref, cb_ref, wr_ref, br_ref, wi_ref, bi_ref,
             lam_ref, dx_ref, dg_ref, dwr_ref, dwi_ref, sums_ref, a_scr, dh_scr, g_scr, carry, du_next):
        b, i = pl.program_id(0), pl.program_id(1)
        first_tile = i == nt - 1

        @pl.when((b == 0) & (i == 0))
        def _():
            dwr_ref[...] = jnp.zeros_like(dwr_ref)
            dwi_ref[...] = jnp.zeros_like(dwi_ref)
            sums_ref[...] = jnp.zeros_like(sums_ref)

        @pl.when(i == 0)
        def _():
            carry[...] = jnp.zeros_like(carry)
            du_next[...] = jnp.zeros_like(du_next)

        xhalo = jnp.where(first_tile, 0.0, xhalo_ref[...])
        ext = jnp.concatenate([xhalo, x_ref[...]], axis=0)
        u = _conv(ext, cw_ref, cb_ref, ts)
        r, ig, sp, a, mult = _lru_gates(u, wr_ref, br_ref, wi_ref, bi_ref, lam_ref)
        gelu, dgelu = _gelu_parts(g_ref[...])
        dyv, hv = dy_ref[...], h_ref[...]
        dg_ref[...] = (dyv * hv * dgelu).astype(BF16)
        a_scr[...] = a
        dh_scr[...] = dyv * gelu

        rid = lax.broadcasted_iota(jnp.int32, (SUBLANES, LRU_W), 0)
        nchunk = ts // SUBLANES

        def chunk(n, cg):
            off = pl.multiple_of((nchunk - 1 - n) * SUBLANES, SUBLANES)
            av, beta = a_scr[pl.ds(off, SUBLANES), :], dh_scr[pl.ds(off, SUBLANES), :]
            alpha = jnp.where(rid == SUBLANES - 1, 1.0, pltpu.roll(av, SUBLANES - 1, 0))
            for d in (1, 2, 4):
                keep = rid + d <= SUBLANES - 1
                beta = jnp.where(keep, beta + alpha * pltpu.roll(beta, SUBLANES - d, 0), beta)
                alpha = jnp.where(keep, alpha * pltpu.roll(alpha, SUBLANES - d, 0), alpha)
            gv = beta + alpha * cg
            g_scr[pl.ds(off, SUBLANES), :] = gv
            return av[0:1, :] * gv[0:1, :]

        carry[...] = lax.fori_loop(0, nchunk, chunk, carry[...])
        gv = g_scr[...]
        hhalo = jnp.where(first_tile, 0.0, hhalo_ref[...])
        hprev = _shift_down(jnp.concatenate([hhalo, hv], axis=0), 1, ts)
        dmult = gv * ig * u
        dig = gv * mult * u
        du = gv * mult * ig
        dlog_a = gv * hprev * a - dmult * a * a / mult
        dr = dlog_a * (-LRU_C * sp)
        dr_pre = dr * r * (1.0 - r)
        di_pre = dig * ig * (1.0 - ig)
        drb, dib, ub = dr_pre.astype(BF16), di_pre.astype(BF16), u.astype(BF16)
        du = du + _dot(drb, wr_ref[...], NT) + _dot(dib, wi_ref[...], NT)
        dwr_ref[...] += _dot(ub, drb, TN)
        dwi_ref[...] += _dot(ub, dib, TN)

        ext_du = jnp.concatenate([du, du_next[...]], axis=0)
        du_next[...] = du[0:SUBLANES, :]
        n_ext = ts + SUBLANES
        dx = cw_ref[3:4, :] * du
        sums = [_colsum(dr_pre), _colsum(di_pre), _colsum(dlog_a * (-LRU_C * r)), _colsum(du)]
        dcw = []
        for k in range(3):
            dx = dx + cw_ref[k:k + 1, :] * pltpu.roll(ext_du, n_ext - (3 - k), 0)[0:ts]
            dcw.append(_colsum(du * _shift_down(ext, 3 - k, ts)))
        dcw.append(_colsum(du * _shift_down(ext, 0, ts)))
        dx_ref[...] = dx.astype(BF16)
        sums_ref[...] += _rows_to_block(sums + dcw, LRU_W)

    full = lambda shape: pl.BlockSpec(shape, lambda b, i: (0,) * len(shape))
    tile = pl.BlockSpec((ts, LRU_W), row)
    return pl.pallas_call(
        body, name="lru_bwd", grid=(batch, nt),
        in_specs=[tile, tile, halo, pl.BlockSpec((ts, LRU_W), lambda b, i: (b * nt + (nt - 1 - i), 1)), tile, halo,
                  full((4, LRU_W)), full((1, LRU_W)), full((LRU_W, LRU_W)), full((1, LRU_W)),
                  full((LRU_W, LRU_W)), full((1, LRU_W)), full((1, LRU_W))],
        out_specs=[tile, tile, full((LRU_W, LRU_W)), full((LRU_W, LRU_W)), full((SUBLANES, LRU_W))],
        out_shape=[jax.ShapeDtypeStruct((m, LRU_W), BF16), jax.ShapeDtypeStruct((m, LRU_W), BF16),
                   jax.ShapeDtypeStruct((LRU_W, LRU_W), F32), jax.ShapeDtypeStruct((LRU_W, LRU_W), F32),
                   jax.ShapeDtypeStruct((SUBLANES, LRU_W), F32)],
        scratch_shapes=[pltpu.VMEM((ts, LRU_W), F32), pltpu.VMEM((ts, LRU_W), F32), pltpu.VMEM((ts, LRU_W), F32),
                        pltpu.VMEM((1, LRU_W), F32), pltpu.VMEM((SUBLANES, LRU_W), F32)],
        compiler_params=_params(2),
    )(dy, proj32, proj32, proj32, h, h, conv_w, conv_b, wr, br, wi, bi, lam)


def _head_masks():
    lane = lax.broadcasted_iota(jnp.int32, (1, LANES), 1)
    return lane < HEAD_DIM


def _stack_heads(x2):
    lo, zero = _head_masks(), jnp.zeros_like(x2)
    return jnp.concatenate([jnp.where(lo, x2, zero), jnp.where(lo, zero, x2)], axis=0)


def _unstack_heads(y):
    return jnp.where(_head_masks(), y[:TQ], y[TQ:])


def _stack_cols(a, b):
    return jnp.concatenate([a, b], axis=0)


def _causal(qi, kb, strict):
    r = jnp.bitwise_and(lax.broadcasted_iota(jnp.int32, (2 * TQ, BLK), 0), TQ - 1) + qi * TQ
    c = lax.broadcasted_iota(jnp.int32, (2 * TQ, BLK), 1) + kb * BLK
    return (c < r) if strict else (c <= r)


def _key_loop(qi, group, carry, descending=False):
    def trip(n, cr):
        done = [n * KB_PER_Q + j for j in range(KB_PER_Q)]
        return group([qi * KB_PER_Q - 1 - t for t in done] if descending else done, cr)

    return lax.fori_loop(0, qi, trip, carry)


def _one_by_one(block):
    def group(kbs, carry):
        for kb in kbs:
            carry = block(kb, carry, False)
        return carry
    return group


def _tri(cmp):
    r = lax.broadcasted_iota(jnp.int32, (BLK, BLK), 0)
    c = lax.broadcasted_iota(jnp.int32, (BLK, BLK), 1)
    return cmp(r, c)


def _dot_split(x, tri):
    hi, lo = _split2(x)
    return _dot(hi, tri) + _dot(lo, tri)


def _sb_fwd(proj16, batch, seq):
    nq = seq // TQ
    scale = HEAD_DIM ** -0.5

    def body(q_ref, k_ref, v_ref, y_ref, t_ref):
        qi = pl.program_id(2)
        qs = _stack_heads(q_ref[0])
        tri_after = _tri(lambda r, c: r > c).astype(BF16)

        def block(kb, carry, masked):
            acc, c = carry
            ks = pl.multiple_of(kb * BLK, BLK)
            k2, v2 = k_ref[0, pl.ds(ks, BLK), :], v_ref[0, pl.ds(ks, BLK), :]
            z = _dot(qs, k2, NT) * scale
            sp = _softplus(z)
            l = -sp
            if masked:
                valid = _causal(qi, kb, True)
                l = jnp.where(valid, l, 0.0)
            w = jnp.exp((z - sp) + _dot_split(l, tri_after) + c)
            if masked:
                w = jnp.where(valid, w, 0.0)
            return acc + _dot(w.astype(BF16), v2), c + jnp.sum(l, axis=1, keepdims=True)

        def group(kbs, carry):
            acc, c = carry
            kv = [(k_ref[0, pl.ds(pl.multiple_of(kb * BLK, BLK), BLK), :],
                   v_ref[0, pl.ds(pl.multiple_of(kb * BLK, BLK), BLK), :]) for kb in kbs]
            zs = [_dot(qs, k2, NT) * scale for k2, _ in kv]
            sps = [_softplus(z) for z in zs]
            afters = [_dot_split(-sp, tri_after) for sp in sps]
            for z, sp, after, (_, v2) in zip(zs, sps, afters, kv):
                acc = acc + _dot(jnp.exp((z - sp) + after + c).astype(BF16), v2)
                c = c - jnp.sum(sp, axis=1, keepdims=True)
            return acc, c

        carry = (jnp.zeros((2 * TQ, LANES), F32), jnp.zeros((2 * TQ, 1), F32))
        first = qi * KB_PER_Q
        for n in reversed(range(KB_PER_Q)):
            carry = block(first + n, carry, True)
        acc, c = _key_loop(qi, group, carry, descending=True)
        y_ref[...] = _unstack_heads(acc)
        t_ref[0] = _unstack_heads(jnp.broadcast_to(c, (2 * TQ, LANES)))

    m = batch * seq
    return pl.pallas_call(
        body, name="sb_fwd", grid=(batch, 2, nq),
        in_specs=[pl.BlockSpec((1, TQ, LANES), lambda b, p, q: (b, q, COL_SBQ + p)),
                  pl.BlockSpec((1, seq, LANES), lambda b, p, q: (b, 0, COL_SBK + p)),
                  pl.BlockSpec((1, seq, LANES), lambda b, p, q: (b, 0, COL_SBV + p))],
        out_specs=[pl.BlockSpec((TQ, LANES), lambda b, p, q: (b * nq + q, p)),
                   pl.BlockSpec((1, TQ, LANES), lambda b, p, q: (p, b * nq + q, 0))],
        out_shape=[jax.ShapeDtypeStruct((m, ATT_W), F32), jax.ShapeDtypeStruct((2, m, LANES), F32)],
        compiler_params=_params(3),
    )(proj16, proj16, proj16)


def _sb_bwd(dy, t, proj16, batch, seq):
    nq = seq // TQ
    scale = HEAD_DIM ** -0.5

    def body(dy_ref, t_ref, q_ref, k_ref, v_ref, dq_ref, dk_ref, dv_ref):
        qi = pl.program_id(2)

        @pl.when(qi == 0)
        def _():
            dk_ref[...] = jnp.zeros_like(dk_ref)
            dv_ref[...] = jnp.zeros_like(dv_ref)

        t2 = t_ref[0]
        qs, dys = _stack_heads(q_ref[0]), _stack_heads(dy_ref[...].astype(BF16))
        tot = _stack_cols(t2[:, 0:1], t2[:, HEAD_DIM:HEAD_DIM + 1])
        tri_incl = _tri(lambda r, c: r <= c).astype(BF16)
        tri_excl = _tri(lambda r, c: r < c).astype(BF16)

        def block(kb, carry, masked):
            dq, pc, ec = carry
            ks = pl.multiple_of(kb * BLK, BLK)
            k2, v2 = k_ref[0, pl.ds(ks, BLK), :], v_ref[0, pl.ds(ks, BLK), :]
            z = _dot(qs, k2, NT) * scale
            sp = _softplus(z)
            l, b = -sp, z - sp
            sig = jnp.exp(b)
            if masked:
                valid = _causal(qi, kb, True)
                l = jnp.where(valid, l, 0.0)
            after = tot - (pc + _dot_split(l, tri_incl))
            w = jnp.exp(b + after)
            if masked:
                w = jnp.where(valid, w, 0.0)
            e = _dot(dys, v2, NT) * w
            et = ec + _dot_split(e, tri_excl)
            dz = e * (1.0 - sig) - et * sig
            if masked:
                dz = jnp.where(valid, dz, 0.0)
            dzb = (dz * scale).astype(BF16)
            dk_ref[0, pl.ds(ks, BLK), :] += _dot(dzb, qs, TN)
            dv_ref[0, pl.ds(ks, BLK), :] += _dot(w.astype(BF16), dys, TN)
            return (dq + _dot(dzb, k2), pc + jnp.sum(l, axis=1, keepdims=True),
                    ec + jnp.sum(e, axis=1, keepdims=True))

        def group(kbs, carry):
            dq, pc, ec = carry
            starts = [pl.multiple_of(kb * BLK, BLK) for kb in kbs]
            kv = [(k_ref[0, pl.ds(ks, BLK), :], v_ref[0, pl.ds(ks, BLK), :]) for ks in starts]
            zs = [_dot(qs, k2, NT) * scale for k2, _ in kv]
            dws = [_dot(dys, v2, NT) for _, v2 in kv]
            sps = [_softplus(z) for z in zs]
            pins = [_dot_split(-sp, tri_incl) for sp in sps]
            es, ws, sigs = [], [], []
            for z, sp, pin, dw in zip(zs, sps, pins, dws):
                b = z - sp
                w = jnp.exp(b + (tot - (pc + pin)))
                pc = pc - jnp.sum(sp, axis=1, keepdims=True)
                es.append(dw * w)
                ws.append(w)
                sigs.append(jnp.exp(b))
            eins = [_dot_split(e, tri_excl) for e in es]
            for ks, (k2, _), e, w, sig, ein in zip(starts, kv, es, ws, sigs, eins):
                dzb = ((e * (1.0 - sig) - (ec + ein) * sig) * scale).astype(BF16)
                ec = ec + jnp.sum(e, axis=1, keepdims=True)
                dk_ref[0, pl.ds(ks, BLK), :] += _dot(dzb, qs, TN)
                dv_ref[0, pl.ds(ks, BLK), :] += _dot(w.astype(BF16), dys, TN)
                dq = dq + _dot(dzb, k2)
            return dq, pc, ec

        col = jnp.zeros((2 * TQ, 1), F32)
        first = qi * KB_PER_Q
        carry = _key_loop(qi, group, (jnp.zeros((2 * TQ, LANES), F32), col, col))
        for n in range(KB_PER_Q):
            carry = block(first + n, carry, True)
        dq_ref[...] = _unstack_heads(carry[0])

    m = batch * seq
    whole = lambda col: pl.BlockSpec((1, seq, LANES), lambda b, p, q: (b, 0, col + p))
    return pl.pallas_call(
        body, name="sb_bwd", grid=(batch, 2, nq),
        in_specs=[pl.BlockSpec((TQ, LANES), lambda b, p, q: (b * nq + q, p)),
                  pl.BlockSpec((1, TQ, LANES), lambda b, p, q: (p, b * nq + q, 0)),
                  pl.BlockSpec((1, TQ, LANES), lambda b, p, q: (b, q, COL_SBQ + p)),
                  whole(COL_SBK), whole(COL_SBV)],
        out_specs=[pl.BlockSpec((TQ, LANES), lambda b, p, q: (b * nq + q, p)), whole(0), whole(0)],
        out_shape=[jax.ShapeDtypeStruct((m, ATT_W), F32), jax.ShapeDtypeStruct((batch, seq, ATT_W), F32),
                   jax.ShapeDtypeStruct((batch, seq, ATT_W), F32)],
        compiler_params=_params(3),
    )(dy, t, proj16, proj16, proj16)


def _fox_pre(proj32, gq, gk, bf, group_mean, batch, seq):
    m = proj32.shape[0]
    ts = _tok_tile(seq)
    nt = seq // ts

    def body(q_ref, k_ref, f_ref, gq_ref, gk_ref, bf_ref, gm_ref, fq_ref, fk_ref, fc_ref, carry):
        i = pl.program_id(1)

        @pl.when(i == 0)
        def _():
            carry[...] = jnp.zeros_like(carry)

        gm = gm_ref[...]
        for src, g_ref, dst in ((q_ref, gq_ref, fq_ref), (k_ref, gk_ref, fk_ref)):
            v = src[...]
            ms = _dot_split(v * v, gm)
            dst[...] = (v * lax.rsqrt(ms + EPS) * g_ref[...]).astype(BF16)
        z = f_ref[...] + bf_ref[...]
        lf = jnp.minimum(z, 0.0) - jnp.log(1.0 + jnp.exp(-jnp.abs(z)))
        r = lax.broadcasted_iota(jnp.int32, (ts, ts), 0)
        c = lax.broadcasted_iota(jnp.int32, (ts, ts), 1)
        tri = (r >= c).astype(BF16)
        hi, mid, low = _split3(lf)
        fc = _dot(tri, hi) + _dot(tri, mid) + _dot(tri, low) + carry[...]
        fc_ref[...] = fc
        carry[...] = fc[ts - 1:ts, :]

    full = lambda shape: pl.BlockSpec(shape, lambda b, i: (0,) * len(shape))
    return pl.pallas_call(
        body, name="fox_pre", grid=(batch, nt),
        in_specs=[pl.BlockSpec((ts, ATT_W), lambda b, i: (b * nt + i, 7)),
                  pl.BlockSpec((ts, ATT_W), lambda b, i: (b * nt + i, 8)),
                  pl.BlockSpec((ts, LANES), lambda b, i: (b * nt + i, COL_FXF)),
                  full((1, ATT_W)), full((1, ATT_W)), full((1, LANES)), full((ATT_W, ATT_W))],
        out_specs=[pl.BlockSpec((ts, ATT_W), lambda b, i: (b * nt + i, 0)),
                   pl.BlockSpec((ts, ATT_W), lambda b, i: (b * nt + i, 0)),
                   pl.BlockSpec((ts, LANES), lambda b, i: (b * nt + i, 0))],
        out_shape=[jax.ShapeDtypeStruct((m, ATT_W), BF16), jax.ShapeDtypeStruct((m, ATT_W), BF16),
                   jax.ShapeDtypeStruct((m, LANES), F32)],
        scratch_shapes=[pltpu.VMEM((1, LANES), F32)],
        compiler_params=_params(2),
    )(proj32, proj32, proj32, gq, gk, bf, group_mean)


def _fox_specs(batch, seq):
    nq = seq // TQ
    return dict(
        qblk=pl.BlockSpec((1, TQ, LANES), lambda b, p, q: (b, q, p)),
        whole=pl.BlockSpec((1, seq, LANES), lambda b, p, q: (b, 0, p)),
        vwhole=pl.BlockSpec((1, seq, LANES), lambda b, p, q: (b, 0, COL_FXV + p)),
        fcol=pl.BlockSpec((1, 1, TQ, 2), lambda b, p, q: (b, p, q, 0)),
        frow=pl.BlockSpec((1, 1, 2, seq), lambda b, p, q: (b, p, 0, 0)),
        rows=pl.BlockSpec((TQ, LANES), lambda b, p, q: (b * nq + q, p)),
        stat=pl.BlockSpec((1, TQ, LANES), lambda b, p, q: (p, b * nq + q, 0)),
    )


def _fox_logits(qs, k2, fq_col, fr_ref, ks, is_a, scale):
    fk_row = jnp.where(is_a, fr_ref[0, 0, 0:1, pl.ds(ks, BLK)], fr_ref[0, 0, 1:2, pl.ds(ks, BLK)])
    return _dot(qs, k2, NT) * scale + fq_col - fk_row


def _fox_fwd(fq, fk, proj16, fcol, frow, batch, seq):
    nq = seq // TQ
    scale = HEAD_DIM ** -0.5

    def body(q_ref, k_ref, v_ref, fc_ref, fr_ref, y_ref, lse_ref):
        qi = pl.program_id(2)
        qs = _stack_heads(q_ref[0])
        fcv = fc_ref[0, 0]
        fq_col = _stack_cols(fcv[:, 0:1], fcv[:, 1:2])
        is_a = lax.broadcasted_iota(jnp.int32, (2 * TQ, 1), 0) < TQ

        def block(kb, carry, masked):
            acc, mx, den = carry
            ks = pl.multiple_of(kb * BLK, BLK)
            k2, v2 = k_ref[0, pl.ds(ks, BLK), :], v_ref[0, pl.ds(ks, BLK), :]
            s = _fox_logits(qs, k2, fq_col, fr_ref, ks, is_a, scale)
            if masked:
                s = jnp.where(_causal(qi, kb, False), s, NEG_BIG)
            mx_new = jnp.maximum(mx, jnp.max(s, axis=1, keepdims=True))
            p = jnp.exp(s - mx_new)
            alpha = jnp.exp(mx - mx_new)
            return (alpha * acc + _dot(p.astype(BF16), v2), mx_new, alpha * den + jnp.sum(p, axis=1, keepdims=True))

        first = qi * KB_PER_Q
        carry = (jnp.zeros((2 * TQ, LANES), F32), jnp.full((2 * TQ, 1), NEG_BIG, F32), jnp.zeros((2 * TQ, 1), F32))
        carry = _key_loop(qi, _one_by_one(block), carry)
        for n in range(KB_PER_Q):
            carry = block(first + n, carry, True)
        acc, mx, den = carry
        y_ref[...] = _unstack_heads(acc / den)
        lse_ref[0] = _unstack_heads(jnp.broadcast_to(mx + jnp.log(den), (2 * TQ, LANES)))

    m = batch * seq
    sp = _fox_specs(batch, seq)
    return pl.pallas_call(
        body, name="fox_fwd", grid=(batch, 2, nq),
        in_specs=[sp["qblk"], sp["whole"], sp["vwhole"], sp["fcol"], sp["frow"]],
        out_specs=[sp["rows"], sp["stat"]],
        out_shape=[jax.ShapeDtypeStruct((m, ATT_W), F32), jax.ShapeDtypeStruct((2, m, LANES), F32)],
        compiler_params=_params(3),
    )(fq, fk, proj16, fcol, frow)


def _fox_bwd(dy, y, lse, fq, fk, proj16, fcol, frow, batch, seq):
    nq = seq // TQ
    scale = HEAD_DIM ** -0.5

    def body(dy_ref, y_ref, lse_ref, q_ref, k_ref, v_ref, fc_ref, fr_ref, dq_ref, dk_ref, dv_ref, dfr_ref, dfc_ref):
        qi = pl.program_id(2)

        @pl.when(qi == 0)
        def _():
            dk_ref[...] = jnp.zeros_like(dk_ref)
            dv_ref[...] = jnp.zeros_like(dv_ref)
            dfr_ref[...] = jnp.zeros_like(dfr_ref)

        lo = _head_masks()
        lane = lax.broadcasted_iota(jnp.int32, (1, LANES), 1)
        dy2, lse2, fcv = dy_ref[...], lse_ref[0], fc_ref[0, 0]
        qs, dys = _stack_heads(q_ref[0]), _stack_heads(dy2.astype(BF16))
        dyy = dy2 * y_ref[...]
        delta = _stack_cols(jnp.sum(jnp.where(lo, dyy, 0.0), axis=1, keepdims=True),
                            jnp.sum(jnp.where(lo, 0.0, dyy), axis=1, keepdims=True))
        lse_col = _stack_cols(lse2[:, 0:1], lse2[:, HEAD_DIM:HEAD_DIM + 1])
        fq_col = _stack_cols(fcv[:, 0:1], fcv[:, 1:2])
        is_a = lax.broadcasted_iota(jnp.int32, (2 * TQ, 1), 0) < TQ

        def block(kb, carry, masked):
            dq, rs = carry
            ks = pl.multiple_of(kb * BLK, BLK)
            k2, v2 = k_ref[0, pl.ds(ks, BLK), :], v_ref[0, pl.ds(ks, BLK), :]
            p = jnp.exp(_fox_logits(qs, k2, fq_col, fr_ref, ks, is_a, scale) - lse_col)
            if masked:
                p = jnp.where(_causal(qi, kb, False), p, 0.0)
            ds = p * (_dot(dys, v2, NT) - delta)
            dsb = (ds * scale).astype(BF16)
            dk_ref[0, pl.ds(ks, BLK), :] += _dot(dsb, qs, TN)
            dv_ref[0, pl.ds(ks, BLK), :] += _dot(p.astype(BF16), dys, TN)
            dfr_ref[0, 0, 0:1, pl.ds(ks, BLK)] -= jnp.sum(ds[:TQ], axis=0, keepdims=True)
            dfr_ref[0, 0, 1:2, pl.ds(ks, BLK)] -= jnp.sum(ds[TQ:], axis=0, keepdims=True)
            return dq + _dot(dsb, k2), rs + jnp.sum(ds, axis=1, keepdims=True)

        def group(kbs, carry):
            dq, rs = carry
            starts = [pl.multiple_of(kb * BLK, BLK) for kb in kbs]
            kv = [(k_ref[0, pl.ds(ks, BLK), :], v_ref[0, pl.ds(ks, BLK), :]) for ks in starts]
            ss = [_fox_logits(qs, k2, fq_col, fr_ref, ks, is_a, scale) for ks, (k2, _) in zip(starts, kv)]
            dps = [_dot(dys, v2, NT) for _, v2 in kv]
            ps = [jnp.exp(s - lse_col) for s in ss]
            dss = [p * (dp - delta) for p, dp in zip(ps, dps)]
            for ks, (k2, _), p, ds in zip(starts, kv, ps, dss):
                dsb = (ds * scale).astype(BF16)
                dk_ref[0, pl.ds(ks, BLK), :] += _dot(dsb, qs, TN)
                dv_ref[0, pl.ds(ks, BLK), :] += _dot(p.astype(BF16), dys, TN)
                dfr_ref[0, 0, 0:1, pl.ds(ks, BLK)] -= jnp.sum(ds[:TQ], axis=0, keepdims=True)
                dfr_ref[0, 0, 1:2, pl.ds(ks, BLK)] -= jnp.sum(ds[TQ:], axis=0, keepdims=True)
                dq = dq + _dot(dsb, k2)
            return dq, rs + jnp.sum(functools.reduce(jnp.add, dss), axis=1, keepdims=True)

        first = qi * KB_PER_Q
        carry = _key_loop(qi, group, (jnp.zeros((2 * TQ, LANES), F32), jnp.zeros((2 * TQ, 1), F32)))
        for n in range(KB_PER_Q):
            carry = block(first + n, carry, True)
        dq, rs = carry
        dq_ref[...] = _unstack_heads(dq)
        dfc_ref[0] = jnp.where(lane == 0, rs[:TQ], jnp.where(lane == 1, rs[TQ:], 0.0))

    m = batch * seq
    sp = _fox_specs(batch, seq)
    return pl.pallas_call(
        body, name="fox_bwd", grid=(batch, 2, nq),
        in_specs=[sp["rows"], sp["rows"], sp["stat"], sp["qblk"], sp["whole"], sp["vwhole"], sp["fcol"], sp["frow"]],
        out_specs=[sp["rows"], sp["whole"], sp["whole"],
                   pl.BlockSpec((1, 1, SUBLANES, seq), lambda b, p, q: (b, p, 0, 0)), sp["stat"]],
        out_shape=[jax.ShapeDtypeStruct((m, ATT_W), F32), jax.ShapeDtypeStruct((batch, seq, ATT_W), F32),
                   jax.ShapeDtypeStruct((batch, seq, ATT_W), F32),
                   jax.ShapeDtypeStruct((batch, 2, SUBLANES, seq), F32), jax.ShapeDtypeStruct((2, m, LANES), F32)],
        compiler_params=_params(3),
    )(dy, y, lse, fq, fk, proj16, fcol, frow)


def _fox_post_bwd(dfq, dfk, dfc, proj32, gq, gk, bf, group_mean, batch, seq):
    m = proj32.shape[0]
    ts = _tok_tile(seq)
    nt = seq // ts
    tile = lambda w, col: pl.BlockSpec((ts, w), lambda b, i: (b * nt + (nt - 1 - i), col))

    def body(dfq_ref, dfk_ref, dfc_ref, q_ref, k_ref, f_ref, gq_ref, gk_ref, bf_ref, gm_ref,
             dq_ref, dk_ref, df_ref, gs_ref, bs_ref, carry):
        i = pl.program_id(1)

        @pl.when(i == 0)
        def _():
            carry[...] = jnp.zeros_like(carry)

        gm = gm_ref[...]
        rows = []
        for src, g_ref, d_ref, dst in ((q_ref, gq_ref, dfq_ref, dq_ref), (k_ref, gk_ref, dfk_ref, dk_ref)):
            v, dv = src[...], d_ref[...]
            rstd = lax.rsqrt(_dot_split(v * v, gm) + EPS)
            vhat = v * rstd
            rows.append(_colsum(dv * vhat))
            dvh = dv * g_ref[...]
            dst[...] = (rstd * (dvh - vhat * _dot_split(dvh * vhat, gm))).astype(BF16)
        gs_ref[0] = _rows_to_block(rows, ATT_W)

        dfc_v = dfc_ref[...]
        r = lax.broadcasted_iota(jnp.int32, (ts, ts), 0)
        c = lax.broadcasted_iota(jnp.int32, (ts, ts), 1)
        tri = (r <= c).astype(BF16)
        hi, mid, low = _split3(dfc_v)
        dlf = _dot(tri, hi) + _dot(tri, mid) + _dot(tri, low) + carry[...]
        carry[...] = dlf[0:1, :]
        z = f_ref[...] + bf_ref[...]
        dz = dlf * _sigmoid(-z)
        df_ref[...] = dz.astype(BF16)
        bs_ref[0] = _rows_to_block([_colsum(dz)], LANES)

    full = lambda shape: pl.BlockSpec(shape, lambda b, i: (0,) * len(shape))
    part = lambda w: pl.BlockSpec((1, SUBLANES, w), lambda b, i: (b * nt + (nt - 1 - i), 0, 0))
    return pl.pallas_call(
        body, name="fox_post_bwd", grid=(batch, nt),
        in_specs=[tile(ATT_W, 0), tile(ATT_W, 0), tile(LANES, 0), tile(ATT_W, 7), tile(ATT_W, 8), tile(LANES, COL_FXF),
                  full((1, ATT_W)), full((1, ATT_W)), full((1, LANES)), full((ATT_W, ATT_W))],
        out_specs=[tile(ATT_W, 0), tile(ATT_W, 0), tile(LANES, 0), part(ATT_W), part(LANES)],
        out_shape=[jax.ShapeDtypeStruct((m, ATT_W), BF16), jax.ShapeDtypeStruct((m, ATT_W), BF16),
                   jax.ShapeDtypeStruct((m, LANES), BF16),
                   jax.ShapeDtypeStruct((batch * nt, SUBLANES, ATT_W), F32),
                   jax.ShapeDtypeStruct((batch * nt, SUBLANES, LANES), F32)],
        scratch_shapes=[pltpu.VMEM((1, LANES), F32)],
        compiler_params=_params(2),
    )(dfq, dfk, dfc, proj32, proj32, proj32, gq, gk, bf, group_mean)


_GROUPS = ((0, LRU_W), (LRU_W, LRU_W + ATT_W), (LRU_W + ATT_W, LRU_W + 2 * ATT_W))


def _outnorm(y_lru, y_sb, y_fox, gmix, seq):
    m = y_lru.shape[0]
    tm = _tok_tile(seq)

    def body(a_ref, b_ref, c_ref, g_ref, o_ref):
        parts = []
        for ref in (a_ref, b_ref, c_ref):
            v = ref[...]
            parts.append(v * lax.rsqrt(jnp.mean(v * v, axis=-1, keepdims=True) + EPS))
        o_ref[...] = (jnp.concatenate(parts, axis=1) * g_ref[...]).astype(BF16)

    t = lambda w: pl.BlockSpec((tm, w), lambda i: (i, 0))
    return pl.pallas_call(
        body, name="outnorm", grid=(m // tm,),
        in_specs=[t(LRU_W), t(ATT_W), t(ATT_W), pl.BlockSpec((1, D_MODEL), lambda i: (0, 0))],
        out_specs=t(D_MODEL), out_shape=jax.ShapeDtypeStruct((m, D_MODEL), BF16), compiler_params=_params(1),
    )(y_lru, y_sb, y_fox, gmix)


def _outnorm_bwd_epilogue(p, e_refs, o_refs):
    gmix = e_refs[3][...]
    dg = []
    for n, (lo, hi) in enumerate(_GROUPS):
        v, dyn = e_refs[n][...], p[:, lo:hi]
        rstd = lax.rsqrt(jnp.mean(v * v, axis=-1, keepdims=True) + EPS)
        vhat = v * rstd
        dg.append(_colsum(dyn * vhat))
        dvh = dyn * gmix[:, lo:hi]
        o_refs[n][...] = rstd * (dvh - vhat * jnp.mean(dvh * vhat, axis=-1, keepdims=True))
    o_refs[3][0] = _rows_to_block([jnp.concatenate(dg, axis=1)], p.shape[1])


def _pair_layouts(fcum, batch, seq):
    f4 = fcum[:, :4].reshape(batch, seq, 2, 2)
    return f4.transpose(0, 2, 1, 3), f4.transpose(0, 2, 3, 1)


def _gate_grad_cols(dfr, dfc, batch, seq):
    keys = dfr[:, :, :2, :].transpose(0, 3, 1, 2).reshape(batch * seq, 4)
    queries = dfc[:, :, :2].transpose(1, 0, 2).reshape(batch * seq, 4)
    return jnp.pad(keys + queries, ((0, 0), (0, LANES - 4)))


def _mixer_fwd(x, h, w, gate, batch, seq, next_norm=None):
    m, d = x.shape
    tm = _tok_tile(seq)
    tpb = seq // tm

    def in_epilogue(p, e_refs, o_refs):
        o_refs[0][...] = p
        o_refs[1][...] = p.astype(BF16)

    tn_in = 896
    proj32, proj16 = _mm(h, w["w_in"], mode="nn", tm=tm, tn=tn_in, tk=d, name="mix_in",
                         outs=[((m, N_IN_PAD), F32, (tm, tn_in), lambda i, j: (i, j)),
                               ((m, N_IN_PAD), BF16, (tm, tn_in), lambda i, j: (i, j))],
                         epilogue=in_epilogue)
    y_lru, h_lru = _lru_fwd(proj32, w["conv_w"], w["conv_b"], w["wr"], w["br"], w["wi"], w["bi"], w["lam"], batch, seq)
    p16 = proj16.reshape(batch, seq, N_IN_PAD)
    y_sb, t_sb = _sb_fwd(p16, batch, seq)
    fq, fk, fcum = _fox_pre(proj32, w["gq"], w["gk"], w["bf"], w["group_mean"], batch, seq)
    fcol, frow = _pair_layouts(fcum, batch, seq)
    fq3, fk3 = fq.reshape(batch, seq, ATT_W), fk.reshape(batch, seq, ATT_W)
    y_fox, lse = _fox_fwd(fq3, fk3, p16, fcol, frow, batch, seq)
    ynorm = _outnorm(y_lru, y_sb, y_fox, w["gmix"], seq)

    extras, outs = _residual_io(x, gate, next_norm, tm, tpb)
    res = _mm(ynorm, w["w_out"], mode="nn", tm=tm, tn=d, tk=d, name="mix_out", extras=extras, outs=outs,
              epilogue=functools.partial(_residual_epilogue, 1.0))
    saved = dict(proj32=proj32, p16=p16, h_lru=h_lru, y_lru=y_lru, y_sb=y_sb, t_sb=t_sb, fq3=fq3, fk3=fk3,
                 fcol=fcol, frow=frow, y_fox=y_fox, lse=lse, ynorm=ynorm, out=res[1])
    return res[0], saved, (res[2] if next_norm else None)


def _mixer_bwd(dx_out, x, h, s, w, gn, scale, gate, batch, seq, on_grads):
    m, d = x.shape
    tm = _tok_tile(seq)
    tpb = seq // tm
    dout, dgate_parts = _residual_bwd(dx_out, s["out"], gate, 1.0, seq, "mix_res_bwd")
    (dw_out,) = _mm(s["ynorm"], dout, mode="tn", tm=d, tn=d, tk=_grad_tokens(m, tm), name="mix_dwout",
                    outs=[((d, d), BF16, (d, d), lambda i, j: (i, j))], epilogue=_store_epilogue([BF16]))
    dy_lru, dy_sb, dy_fox, gmix_parts = _mm(
        dout, w["w_out"], mode="nt", tm=tm, tn=d, tk=d, name="mix_out_dx",
        extras=[(s["y_lru"], (tm, LRU_W), lambda i, j: (i, 0)), (s["y_sb"], (tm, ATT_W), lambda i, j: (i, 0)),
                (s["y_fox"], (tm, ATT_W), lambda i, j: (i, 0)), (w["gmix"], (1, d), lambda i, j: (0, 0))],
        outs=[((m, LRU_W), F32, (tm, LRU_W), lambda i, j: (i, 0)), ((m, ATT_W), F32, (tm, ATT_W), lambda i, j: (i, 0)),
              ((m, ATT_W), F32, (tm, ATT_W), lambda i, j: (i, 0)),
              ((m // tm, SUBLANES, d), F32, (1, SUBLANES, d), lambda i, j: (i, 0, 0))],
        epilogue=_outnorm_bwd_epilogue)

    dsq, dsk, dsv = _sb_bwd(dy_sb, s["t_sb"], s["p16"], batch, seq)
    dfq, dfk, dfv, dfr, dfc = _fox_bwd(dy_fox, s["y_fox"], s["lse"], s["fq3"], s["fk3"], s["p16"], s["fcol"],
                                       s["frow"], batch, seq)
    dfc_cols = _gate_grad_cols(dfr, dfc, batch, seq)
    dxq, dxk, dxf, gqk_parts, bf_parts = _fox_post_bwd(dfq, dfk.reshape(m, ATT_W), dfc_cols, s["proj32"],
                                                       w["gq"], w["gk"], w["bf"], w["group_mean"], batch, seq)
    dlx, dlg, dwr, dwi, lru_sums = _lru_bwd(dy_lru, s["proj32"], s["h_lru"], w["conv_w"], w["conv_b"], w["wr"],
                                            w["br"], w["wi"], w["bi"], w["lam"], batch, seq)
    dproj = jnp.concatenate([dlx, dlg, dsq.astype(BF16), dsk.reshape(m, ATT_W).astype(BF16),
                             dsv.reshape(m, ATT_W).astype(BF16), dxq, dxk, dfv.reshape(m, ATT_W).astype(BF16), dxf],
                            axis=1)
    tn_in = 896
    (dw_in,) = _mm(h, dproj, mode="tn", tm=d, tn=tn_in, tk=_grad_tokens(m, tm), name="mix_dwin",
                   outs=[((d, N_IN_PAD), BF16, (d, tn_in), lambda i, j: (i, j))], epilogue=_store_epilogue([BF16]))
    scale = scale + on_grads(dw_in, dw_out)[0, 0]
    dx, nm_parts = _mm(dproj, w["w_in"], mode="nt", tm=tm, tn=d, tk=tn_in, name="mix_in_dx",
                       extras=[(x, (tm, d), lambda i, j: (i, 0)), (dx_out, (tm, d), lambda i, j: (i, 0)),
                               (gn, (1, d), lambda i, j: (0, 0)), (scale, (1, 1, d), lambda i, j: (i // tpb, 0, 0))],
                       outs=[((m, d), F32, (tm, d), lambda i, j: (i, 0)),
                             ((m // tm, SUBLANES, d), F32, (1, SUBLANES, d), lambda i, j: (i, 0, 0))],
                       epilogue=_normmod_bwd_epilogue)
    grads = dict(dwr=dwr, dwi=dwi, lru_sums=lru_sums, gmix_parts=gmix_parts,
                 gqk_parts=gqk_parts, bf_parts=bf_parts)
    return dx, grads, nm_parts, dgate_parts


def _block_diag(w):
    nb = w.shape[0]
    eye = jnp.eye(nb, dtype=w.dtype)
    return (eye[:, None, :, None] * w[:, :, None, :]).reshape(nb * HEAD_DIM, nb * HEAD_DIM)


def _block_diag_grad(g):
    nb = LRU_W // HEAD_DIM
    g4 = g.reshape(nb, HEAD_DIM, nb, HEAD_DIM)
    return jnp.stack([g4[n, :, n, :] for n in range(nb)])


def _per_batch(parts, batch, row):
    r = parts[:, row, :]
    return r.reshape(batch, -1, r.shape[-1]).sum(axis=1)


def _local_step(x3, target3, mod, wts, big_weights):
    batch, seq, d = x3.shape
    assert seq % TQ == 0, seq
    m = batch * seq
    n_layers = mod.shape[0]
    x = x3.reshape(m, d)
    group_mean = _block_diag(jnp.full((ATT_W // HEAD_DIM, HEAD_DIM, HEAD_DIM), 1.0 / HEAD_DIM, BF16))
    vec = lambda l, j, t: mod[l, :, j, t][:, None, :]

    layers, saved = [], []
    for l in range(n_layers):
        gq = jnp.tile(wts["g_qk"][l, 0], ATT_W // HEAD_DIM)[None, :]
        gk = jnp.tile(wts["g_qk"][l, 1], ATT_W // HEAD_DIM)[None, :]
        bf = jnp.pad(wts["b_fgate"][l], (0, LANES - 4))[None, :]
        lw = dict(conv_w=wts["conv_w"][l],
                  conv_b=wts["conv_b"][l][None, :], wr=_block_diag(wts["w_rgate"][l]).astype(BF16),
                  br=wts["b_rgate"][l][None, :], wi=_block_diag(wts["w_igate"][l]).astype(BF16),
                  bi=wts["b_igate"][l][None, :], lam=wts["lru_lambda"][l][None, :], gq=gq, gk=gk, bf=bf,
                  group_mean=group_mean, gmix=wts["g_mix_out"][l][None, :])
        layers.append(lw)
        gn = lambda j: wts["g_norm"][l, j][None, :]
        norm_of = lambda ll, j: (wts["g_norm"][ll, j][None, :], vec(ll, j, 1), vec(ll, j, 0))
        sv = dict(x0=x)
        sv["h0"] = h_next if l else _normmod(x, *norm_of(0, 0), seq, "normmod_first")
        wup, wdown = big_weights(l, "ffn0", sv["h0"])
        x, sv["ffn0"], wdown, sv["h1"] = _ffn_fwd(x, sv["h0"], wup, wdown, vec(l, 0, 2), seq, f"{l}_0", norm_of(l, 1))
        sv["w_ffn0"] = (wup, wdown)
        sv["x1"] = x
        lw["w_in"], lw["w_out"] = big_weights(l, "mix", sv["h1"])
        x, sv["mix"], sv["h2"] = _mixer_fwd(x, sv["h1"], lw, vec(l, 1, 2), batch, seq, norm_of(l, 2))
        sv["x2"] = x
        wup, wdown = big_weights(l, "ffn1", sv["h2"])
        x, sv["ffn1"], wdown, h_next = _ffn_fwd(x, sv["h2"], wup, wdown, vec(l, 2, 2), seq, f"{l}_1",
                                                norm_of(l + 1, 0) if l + 1 < n_layers else None)
        sv["w_ffn1"] = (wup, wdown)
        saved.append(sv)

    dx, loss_parts = _loss_head(x, target3.reshape(m, d), seq)
    loss = jnp.sum(loss_parts[:, 0, 0])

    handles = {}

    def scatter(key, shapes):
        def on_grads(*grads):
            ops = [(g.reshape(shape), 0, "scatter") for g, shape in zip(grads, shapes)]
            handles[key], token = _flight_start(ops, f"grads_{key[0]}_{key[1]}_start")
            return token
        return on_grads

    ffn_shapes = ((2 * N_FF_SHARD, d, FF_SHARD), (N_DEV, D_FF // N_DEV, d))
    mix_shapes = ((N_DEV, d // N_DEV, N_IN_PAD), (N_DEV, d // N_DEV, d))
    small = {k: [] for k in ("dmod", "g_norm", "b_fgate", "conv_w", "conv_b", "w_rgate", "b_rgate", "w_igate",
                             "b_igate", "lru_lambda", "g_qk", "g_mix_out")}
    for l in reversed(range(n_layers)):
        sv, lw = saved[l], layers[l]
        gn = lambda j: wts["g_norm"][l, j][None, :]
        dx, nm2, dg2 = _ffn_bwd(dx, sv["x2"], sv["h2"], sv["ffn1"], *sv["w_ffn1"], gn(2), vec(l, 2, 1), vec(l, 2, 2),
                                seq, f"{l}_1", scatter((l, "ffn1"), ffn_shapes))
        dx, mg, nm1, dg1 = _mixer_bwd(dx, sv["x1"], sv["h1"], sv["mix"], lw, gn(1), vec(l, 1, 1), vec(l, 1, 2),
                                      batch, seq, scatter((l, "mix"), mix_shapes))
        dx, nm0, dg0 = _ffn_bwd(dx, sv["x0"], sv["h0"], sv["ffn0"], *sv["w_ffn0"], gn(0), vec(l, 0, 1), vec(l, 0, 2),
                                seq, f"{l}_0", scatter((l, "ffn0"), ffn_shapes))
        dmod_l, gnorm_l = [], []
        for nm, dg in ((nm0, dg0), (nm1, dg1), (nm2, dg2)):
            dmod_l.append(jnp.stack([_per_batch(nm, batch, 0), _per_batch(nm, batch, 1), _per_batch(dg, batch, 0)],
                                    axis=1))
            gnorm_l.append(jnp.sum(nm[:, 2, :], axis=0))
        small["dmod"].insert(0, jnp.stack(dmod_l, axis=1))
        small["g_norm"].insert(0, jnp.stack(gnorm_l))
        ls = mg["lru_sums"]
        small["b_rgate"].insert(0, ls[0])
        small["b_igate"].insert(0, ls[1])
        small["lru_lambda"].insert(0, ls[2] * (-_sigmoid(-wts["lru_lambda"][l])))
        small["conv_b"].insert(0, ls[3])
        small["conv_w"].insert(0, ls[4:8])
        small["w_rgate"].insert(0, _block_diag_grad(mg["dwr"]))
        small["w_igate"].insert(0, _block_diag_grad(mg["dwi"]))
        small["g_mix_out"].insert(0, jnp.sum(mg["gmix_parts"][:, 0, :], axis=0))
        gqk = jnp.sum(mg["gqk_parts"][:, :2, :], axis=0).reshape(2, ATT_W // HEAD_DIM, HEAD_DIM).sum(axis=1)
        small["g_qk"].insert(0, gqk)
        small["b_fgate"].insert(0, jnp.sum(mg["bf_parts"][:, 0, :4], axis=0))
    small = {k: jnp.stack(v) for k, v in small.items()}
    return loss, dx.reshape(batch, seq, d), handles, small


def _row_tile(rows, row_bytes):
    for t in (512, 256, 128, 64, 32, 16):
        if rows % t == 0 and t * row_bytes <= 4 * 1024 * 1024:
            return t
    return rows


def _adamw(parts, w, m, v, name):
    groups, n_parts, rows, cols = parts.shape
    tr = _row_tile(rows, cols * (n_parts * parts.dtype.itemsize + 7 * 4))
    c1 = 1.0 - ADAM_B1 ** ADAM_STEP
    c2 = 1.0 - ADAM_B2 ** ADAM_STEP

    def body(p_ref, w_ref, m_ref, v_ref, g_out, d_out, m_out, v_out):
        g = p_ref[0].astype(F32)
        for n in range(1, n_parts):
            g = g + p_ref[n].astype(F32)
        m_new = ADAM_B1 * m_ref[...] + (1.0 - ADAM_B1) * g
        v_new = ADAM_B2 * v_ref[...] + (1.0 - ADAM_B2) * (g * g)
        g_out[...] = g
        d_out[...] = -ADAM_LR * ((m_new / c1) / (jnp.sqrt(v_new / c2) + ADAM_EPS) + ADAM_WD * w_ref[...])
        m_out[...] = m_new
        v_out[...] = v_new

    tile = pl.BlockSpec((None, tr, cols), lambda g, i: (g, i, 0))
    return pl.pallas_call(
        body, name=name, grid=(groups, rows // tr),
        in_specs=[pl.BlockSpec((None, n_parts, tr, cols), lambda g, i: (g, 0, i, 0)), tile, tile, tile],
        out_specs=[tile] * 4, out_shape=[jax.ShapeDtypeStruct((groups, rows, cols), F32)] * 4,
        compiler_params=_params(2),
    )(parts, w, m, v)


def _sum_parts(parts):
    n_parts, rows, cols = parts.shape

    def body(p_ref, o_ref):
        acc = p_ref[0]
        for n in range(1, n_parts):
            acc = acc + p_ref[n]
        o_ref[...] = acc

    return pl.pallas_call(body, name="sum_small", out_shape=jax.ShapeDtypeStruct((rows, cols), F32),
                          compiler_params=pltpu.CompilerParams(vmem_limit_bytes=VMEM_LIMIT_BYTES))(parts)


def _flatten(arrays, multiple):
    chunks = []
    for a in arrays:
        flat = a.reshape(-1).astype(F32)
        chunks.append(jnp.pad(flat, (0, (-flat.shape[0]) % multiple)).reshape(-1, LANES))
    return jnp.concatenate(chunks, axis=0)


def _unflatten(flat2d, shapes, multiple):
    flat2d, out, row = flat2d.reshape(-1, LANES), [], 0
    for s in shapes:
        n = math.prod(s)
        rows = (n + multiple - 1) // multiple * (multiple // LANES)
        out.append(flat2d[row:row + rows].reshape(-1)[:n].reshape(s))
        row += rows
    return out


SMALL_NAMES = ("b_ada", "g_norm", "b_fgate", "conv_w", "conv_b", "w_rgate", "b_rgate", "w_igate", "b_igate",
               "lru_lambda", "g_qk", "g_mix_out")
WEIGHT_NAMES = ("w_ada", "b_ada", "g_norm", "w_ffn_up", "w_ffn_down", "w_in", "b_fgate", "conv_w", "conv_b",
                "w_rgate", "b_rgate", "w_igate", "b_igate", "lru_lambda", "g_qk", "g_mix_out", "w_out")


def kernel(x, c, w_ada, b_ada, g_norm, w_ffn_up, w_ffn_down, w_in, b_fgate, conv_w, conv_b, w_rgate, b_rgate, w_igate, b_igate, lru_lambda, g_qk, g_mix_out, w_out, loss_target, m_w_ada, m_b_ada, m_g_norm, m_w_ffn_up, m_w_ffn_down, m_w_in, m_b_fgate, m_conv_w, m_conv_b, m_w_rgate, m_b_rgate, m_w_igate, m_b_igate, m_lru_lambda, m_g_qk, m_g_mix_out, m_w_out, v_w_ada, v_b_ada, v_g_norm, v_w_ffn_up, v_w_ffn_down, v_w_in, v_b_fgate, v_conv_w, v_conv_b, v_w_rgate, v_b_rgate, v_w_igate, v_b_igate, v_lru_lambda, v_g_qk, v_g_mix_out, v_w_out):
    batch, seq, d = x.shape
    n_layers = w_ada.shape[0]
    me = 4 * lax.axis_index("x") + 2 * lax.axis_index("y") + lax.axis_index("c")
    weights = dict(w_ada=w_ada, b_ada=b_ada, g_norm=g_norm, w_ffn_up=w_ffn_up, w_ffn_down=w_ffn_down, w_in=w_in,
                   b_fgate=b_fgate, conv_w=conv_w, conv_b=conv_b, w_rgate=w_rgate, b_rgate=b_rgate, w_igate=w_igate,
                   b_igate=b_igate, lru_lambda=lru_lambda, g_qk=g_qk, g_mix_out=g_mix_out, w_out=w_out)
    moments_m = dict(w_ada=m_w_ada, b_ada=m_b_ada, g_norm=m_g_norm, w_ffn_up=m_w_ffn_up, w_ffn_down=m_w_ffn_down,
                     w_in=m_w_in, b_fgate=m_b_fgate, conv_w=m_conv_w, conv_b=m_conv_b, w_rgate=m_w_rgate,
                     b_rgate=m_b_rgate, w_igate=m_w_igate, b_igate=m_b_igate, lru_lambda=m_lru_lambda, g_qk=m_g_qk,
                     g_mix_out=m_g_mix_out, w_out=m_w_out)
    moments_v = dict(w_ada=v_w_ada, b_ada=v_b_ada, g_norm=v_g_norm, w_ffn_up=v_w_ffn_up, w_ffn_down=v_w_ffn_down,
                     w_in=v_w_in, b_fgate=v_b_fgate, conv_w=v_conv_w, conv_b=v_conv_b, w_rgate=v_w_rgate,
                     b_rgate=v_b_rgate, w_igate=v_w_igate, b_igate=v_b_igate, lru_lambda=v_lru_lambda, g_qk=v_g_qk,
                     g_mix_out=v_g_mix_out, w_out=v_w_out)

    c_all, gn_all, cw_all = _exchange([(c, 0), (g_norm, 0), (conv_w, 0)], [], "gather_small_weights")
    c_all = c_all.reshape(N_DEV * batch, d)
    n_ada = w_ada.shape[-1]
    g_norm_full = gn_all.transpose(1, 2, 0, 3).reshape(n_layers, 3, d)
    conv_w_full = cw_all.transpose(1, 2, 0, 3).reshape(n_layers, 4, LRU_W)

    b_ada_loc = lax.dynamic_slice_in_dim(b_ada, me * n_ada, n_ada, axis=1)
    silu = lambda t: t * _sigmoid(t)

    def bias_epilogue(p, e_refs, o_refs):
        o_refs[0][...] = p + e_refs[0][...]

    mod_loc = []
    for l in range(n_layers):
        (ml,) = _mm(c_all, w_ada, mode="nn", tm=c_all.shape[0], tn=n_ada, tk=d, b_lead=(l,), a_pre=silu,
                    name=f"ada_{l}", extras=[(b_ada_loc[l][None, :], (1, n_ada), lambda i, j: (0, 0))],
                    outs=[((c_all.shape[0], n_ada), F32, (c_all.shape[0], n_ada), lambda i, j: (0, 0))],
                    epilogue=bias_epilogue)
        mod_loc.append(ml)
    (mod_all,) = _exchange([(jnp.stack(mod_loc), 0)], [], "gather_mod")
    mod_all = mod_all.transpose(1, 2, 0, 3).reshape(n_layers, N_DEV * batch, 9 * d)

    cast = lambda w, token: (w + token[0, 0]).astype(BF16)
    ffn_ops = lambda l, f, token: [(cast(w_ffn_up[l, f], token), 0, "gather"),
                                   (cast(w_ffn_down[l, f], token), 0, "gather")]
    mix_ops = lambda l, token: [(cast(jnp.pad(w_in[l], ((0, 0), (0, N_IN_PAD - N_IN))), token), 0, "gather"),
                                (cast(w_out[l], token), 0, "gather")]
    behind = lambda w, token: w + token[0, 0].astype(BF16)
    flights, landed_rest = {}, []

    def start(key, ops):
        flights[key], token = _flight_start(ops, f"weights_{key}_start")
        return token

    def wait(key, after):
        return _flight_wait(flights[key], after, f"weights_{key}_wait")

    def big_weights(l, part, after):
        if (l, part) == (0, "ffn0"):
            def wdown(after_up):
                (wd,), landed_down = wait("down", after_up)
                return behind(wd, start("mix", mix_ops(0, landed_down))).reshape(D_FF, d)

            return wup_first, wdown
        if (l, part) == (0, "mix"):
            (wi, wo), landed = wait("mix", after)
            rest = ffn_ops(0, 1, landed)
            for ll in range(1, n_layers):
                rest += ffn_ops(ll, 0, landed) + mix_ops(ll, landed) + ffn_ops(ll, 1, landed)
            return behind(wi, start("rest", rest)).reshape(d, N_IN_PAD), wo.reshape(d, d)
        if not landed_rest:
            landed_rest.extend(wait("rest", after)[0])
        at = 0 if l == 0 else 2 + 6 * (l - 1) + {"ffn0": 0, "mix": 2, "ffn1": 4}[part]
        first, second = landed_rest[at], landed_rest[at + 1]
        if part == "mix":
            return first.reshape(d, N_IN_PAD), second.reshape(d, d)
        return first, second.reshape(D_FF, d)

    w_up_first, mod_all = lax.optimization_barrier((w_ffn_up[0, 0].astype(BF16), mod_all))
    (wup_first,) = _exchange([(w_up_first, 0)], [], "gather_w_up_first", two_level=True)
    w_down_first, wup_first = lax.optimization_barrier((w_ffn_down[0, 0].astype(BF16), wup_first))
    token = start("down", [(w_down_first, 0, "gather")])
    mod_me = lax.dynamic_slice_in_dim(mod_all + token[0, 0], me * batch, batch, axis=1)
    mod_me = mod_me.reshape(n_layers, batch, 3, 3, d)

    wts = dict(g_norm=g_norm_full, conv_w=conv_w_full, conv_b=conv_b, w_rgate=w_rgate, b_rgate=b_rgate, w_igate=w_igate, b_igate=b_igate,
               lru_lambda=lru_lambda, g_qk=g_qk, g_mix_out=g_mix_out, b_fgate=b_fgate)
    loss_part, grad_x, handles, small = _local_step(x, loss_target, mod_me, wts, big_weights)

    dmod_me = small.pop("dmod").reshape(n_layers, batch, 9 * d)
    small["b_ada"] = jnp.sum(dmod_me, axis=1)
    small_shapes = [(1,)] + [weights[k].shape if k not in ("g_norm", "conv_w") else small[k].shape for k in SMALL_NAMES]
    small_flat = _flatten([loss_part.reshape(1)] + [small[k] for k in SMALL_NAMES], 16 * LANES)
    dmod_all, small_all = _exchange([(dmod_me, 0), (small_flat, 0)], [], "gather_small", two_level=True)
    landed = {key: _flight_wait(h, small_all, f"grads_{key[0]}_{key[1]}_wait")[0] for key, h in handles.items()}
    layer_range = range(n_layers)
    p_up = jnp.stack([jnp.stack([landed[(l, "ffn0")][0], landed[(l, "ffn1")][0]]) for l in layer_range])
    p_down = jnp.stack([jnp.stack([landed[(l, "ffn0")][1], landed[(l, "ffn1")][1]]) for l in layer_range])
    p_in = jnp.stack([landed[(l, "mix")][0] for l in layer_range])
    p_out = jnp.stack([landed[(l, "mix")][1] for l in layer_range])
    small_sum = _unflatten(_sum_parts(small_all), small_shapes, 16 * LANES)
    loss = small_sum[0].reshape(())
    small_grads = dict(zip(SMALL_NAMES, small_sum[1:]))
    small_grads["g_norm"] = lax.dynamic_slice_in_dim(small_grads["g_norm"], me * g_norm.shape[-1], g_norm.shape[-1], 2)
    small_grads["conv_w"] = lax.dynamic_slice_in_dim(small_grads["conv_w"], me * conv_w.shape[-1], conv_w.shape[-1], 2)

    dmod_all = dmod_all.transpose(1, 0, 2, 3).reshape(n_layers, N_DEV * batch, 9 * d)
    dmod_loc = lax.dynamic_slice_in_dim(dmod_all, me * n_ada, n_ada, axis=2)
    g_ada = []
    for l in range(n_layers):
        (gl,) = _mm(c_all, dmod_loc[l], mode="tn", tm=d, tn=n_ada, tk=c_all.shape[0], a_pre=silu, name=f"dw_ada_{l}",
                    outs=[((d, n_ada), F32, (d, n_ada), lambda i, j: (0, 0))], epilogue=_store_epilogue([F32]))
        g_ada.append(gl)
    g_ada = jnp.stack(g_ada)

    results = {}

    def update(name, parts):
        shape = weights[name].shape
        as3d = lambda t: t.reshape((-1,) + shape[-2:])
        outs = _adamw(parts.reshape((-1,) + parts.shape[-3:]), as3d(weights[name]), as3d(moments_m[name]),
                      as3d(moments_v[name]), f"adamw_{name}")
        results[name] = [o.reshape(shape) for o in outs]

    update("w_ada", g_ada[:, None])
    update("w_ffn_up", p_up)
    update("w_ffn_down", p_down)
    update("w_in", p_in[..., :N_IN])
    update("w_out", p_out)
    sm_shapes = [weights[k].shape for k in SMALL_NAMES]
    flat = lambda src: _flatten([src[k] for k in SMALL_NAMES], 16 * LANES)
    sm_out = _adamw(flat(small_grads)[None, None], flat(weights)[None], flat(moments_m)[None], flat(moments_v)[None],
                    "adamw_small")
    for k, vals in zip(SMALL_NAMES, zip(*[_unflatten(o, sm_shapes, 16 * LANES) for o in sm_out])):
        results[k] = list(vals)

    outs = [loss, grad_x]
    for n in range(4):
        outs += [results[k][n] for k in WEIGHT_NAMES]
    return tuple(outs)
```
